```python
import math
import jax, jax.numpy as jnp
from jax import lax
import numpy as np

D_MODEL = 1024
BATCH = 8
SEQ = 8192
DEPTH = 4

D_MIX = D_MODEL
A_GROUPS = 8
A_GROUP_DIM = 64
A_WIDTH = A_GROUPS * A_GROUP_DIM
A_CONV = 3
DN_HEADS = 4
DN_HEAD_DIM = 128
DN_WIDTH = DN_HEADS * DN_HEAD_DIM
DN_CONV = 4
CHUNK = 64
D_FF = 2816
FF_CONV = 3
P_IN = 3 * A_WIDTH + 4 * DN_WIDTH + 2 * DN_HEADS
N_MOD = 6
EPS = 1e-6

kernel_name = "hybrid_shortconv_gdn_convffn_adaln"


def rmsnorm(x, w):
    xf = x.astype(jnp.float32)
    y = xf * lax.rsqrt(jnp.mean(xf * xf, axis=-1, keepdims=True) + EPS)
    return (y * w.astype(jnp.float32)).astype(x.dtype)


def l2norm(x):
    xf = x.astype(jnp.float32)
    return xf * lax.rsqrt(jnp.sum(xf * xf, axis=-1, keepdims=True) + EPS)


def causal_dwconv(x, w):
    width, ch = w.shape
    return lax.conv_general_dilated(
        x, w[:, None, :].astype(x.dtype), window_strides=(1,), padding=[(width - 1, 0)],
        dimension_numbers=('NWC', 'WIO', 'NWC'), feature_group_count=ch)


def to_chunks(t):
    b, l, h = t.shape[:3]
    t = t.reshape(b, l // CHUNK, CHUNK, h, *t.shape[3:])
    return jnp.moveaxis(t, 3, 1)


def gated_delta_rule(q, k, v, g, beta):
    b, l, h, dk = q.shape
    dv = v.shape[-1]
    q, k, v, g, beta = (to_chunks(t) for t in (q * dk ** -0.5, k, v, g, beta))
    g = jnp.cumsum(g, axis=-1)
    pos = jnp.arange(CHUNK)
    causal = pos[:, None] >= pos[None, :]
    strict = pos[:, None] > pos[None, :]
    gdiff = jnp.where(causal, g[..., :, None] - g[..., None, :], 0.0)
    decay = jnp.where(causal, jnp.exp(gdiff), 0.0)
    k_beta = k * beta[..., None]
    lower = jnp.where(strict, jnp.einsum('bhncd,bhnsd->bhncs', k_beta, k) * decay, 0.0)
    rhs = jnp.concatenate([v * beta[..., None], k_beta * jnp.exp(g)[..., None]], axis=-1)
    sol = lax.linalg.triangular_solve(lower, rhs, left_side=True, lower=True, unit_diagonal=True)
    u, w = sol[..., :dv], sol[..., dv:]
    intra = jnp.where(causal, jnp.einsum('bhncd,bhnsd->bhncs', q, k) * decay, 0.0)
    q_dec = q * jnp.exp(g)[..., None]
    k_dec = k * jnp.exp(g[..., -1:] - g)[..., None]
    chunk_decay = jnp.exp(g[..., -1])

    def step(state, inp):
        u_i, w_i, q_i, k_i, a_i, cd_i = inp
        v_new = u_i - jnp.einsum('bhcd,bhde->bhce', w_i, state)
        o_i = jnp.einsum('bhcd,bhde->bhce', q_i, state) + jnp.einsum('bhcs,bhse->bhce', a_i, v_new)
        state = state * cd_i[..., None, None] + jnp.einsum('bhcd,bhce->bhde', k_i, v_new)
        return state, o_i

    xs = tuple(jnp.moveaxis(t, 2, 0) for t in (u, w, q_dec, k_dec, intra, chunk_decay))
    state0 = jnp.zeros((b, h, dk, dv), jnp.float32)
    _, o = lax.scan(step, state0, xs)
    return jnp.transpose(o, (1, 0, 3, 2, 4)).reshape(b, l, h, dv)


def _fwd_setup_inputs(seed: int = 0) -> dict:
    key = jax.random.key(seed)
    ks = jax.random.split(key, 20)
    f32 = jnp.float32

    def nrm(k, shape, s):
        return jax.random.normal(k, shape, f32) * s

    def gain(k, shape):
        return 1.0 + 0.02 * jax.random.normal(k, shape, f32)

    dt = jnp.exp(jax.random.uniform(ks[10], (DEPTH, DN_HEADS), f32, math.log(1e-3), math.log(1e-1)))
    return {
        'x': nrm(ks[0], (BATCH, SEQ, D_MODEL), 1.0),
        'c': nrm(ks[1], (BATCH, D_MODEL), 1.0),
        'ada_w': nrm(ks[2], (DEPTH, D_MODEL, N_MOD * D_MODEL), 0.5 * D_MODEL ** -0.5),
        'ada_b': nrm(ks[3], (DEPTH, N_MOD * D_MODEL), 0.02),
        'norm1_w': gain(ks[4], (DEPTH, D_MODEL)),
        'w_in': nrm(ks[5], (DEPTH, D_MODEL, P_IN), D_MODEL ** -0.5),
        'conv_a_w': nrm(ks[6], (DEPTH, A_CONV, A_WIDTH), A_CONV ** -0.5),
        'norm_a_w': gain(ks[7], (DEPTH, A_WIDTH)),
        'conv_qkv_w': nrm(ks[8], (DEPTH, DN_CONV, 3 * DN_WIDTH), DN_CONV ** -0.5),
        'a_log': jnp.log(jax.random.uniform(ks[9], (DEPTH, DN_HEADS), f32, 1.0, 16.0)),
        'dt_bias': dt + jnp.log(-jnp.expm1(-dt)),
        'norm_dn_w': gain(ks[11], (DEPTH, DN_HEAD_DIM)),
        'w_out': nrm(ks[12], (DEPTH, D_MIX, D_MODEL), D_MIX ** -0.5),
        'norm2_w': gain(ks[13], (DEPTH, D_MODEL)),
        'w_up': nrm(ks[14], (DEPTH, D_MODEL, 2 * D_FF), D_MODEL ** -0.5),
        'conv_ff_w': nrm(ks[15], (DEPTH, FF_CONV, 2 * D_FF), FF_CONV ** -0.5),
        'w_down': nrm(ks[16], (DEPTH, D_FF, D_MODEL), D_FF ** -0.5),
        'norm_f_w': gain(ks[17], (D_MODEL,)),
    }


def _fwd_reference(x, c, ada_w, ada_b, norm1_w, w_in, conv_a_w, norm_a_w, conv_qkv_w, a_log,
              dt_bias, norm_dn_w, w_out, norm2_w, w_up, conv_ff_w, w_down, norm_f_w):
    b, l, _ = x.shape
    cut = np.cumsum([A_WIDTH, A_WIDTH, A_WIDTH, 3 * DN_WIDTH, DN_WIDTH, DN_HEADS]).tolist()
    c_act = jax.nn.silu(c)
    for i in range(DEPTH):
        mod = (c_act @ ada_w[i] + ada_b[i])[:, None, :]
        sh1, sc1, gt1, sh2, sc2, gt2 = jnp.split(mod, N_MOD, axis=-1)

        h = rmsnorm(x, norm1_w[i]) * (1.0 + sc1) + sh1
        p = h @ w_in[i]
        a_b, a_c, a_x, qkv, z, beta_logit, alpha = jnp.split(p, cut, axis=-1)

        ya = a_b * causal_dwconv(a_c * a_x, conv_a_w[i])
        ya = rmsnorm(ya.reshape(b, l, A_GROUPS, A_GROUP_DIM),
                     norm_a_w[i].reshape(A_GROUPS, A_GROUP_DIM)).reshape(b, l, A_WIDTH)

        qkv = jax.nn.silu(causal_dwconv(qkv, conv_qkv_w[i]))
        q, k, v = jnp.split(qkv.reshape(b, l, 3 * DN_HEADS, DN_HEAD_DIM), 3, axis=2)
        beta = jax.nn.sigmoid(beta_logit.astype(jnp.float32))
        g = -jnp.exp(a_log[i].astype(jnp.float32)) * jax.nn.softplus(
            alpha.astype(jnp.float32) + dt_bias[i].astype(jnp.float32))
        o = gated_delta_rule(l2norm(q), l2norm(k), v.astype(jnp.float32), g, beta).astype(x.dtype)
        o = rmsnorm(o, norm_dn_w[i]) * jax.nn.silu(z.reshape(b, l, DN_HEADS, DN_HEAD_DIM))
        yb = o.reshape(b, l, DN_WIDTH)

        y = jnp.concatenate([ya, yb], axis=-1) @ w_out[i]
        x = x + gt1 * y

        h = rmsnorm(x, norm2_w[i]) * (1.0 + sc2) + sh2
        gate, up = jnp.split(causal_dwconv(h @ w_up[i], conv_ff_w[i]), 2, axis=-1)
        x = x + gt2 * ((jax.nn.silu(gate) * up) @ w_down[i])
    return rmsnorm(x, norm_f_w)


import jax as _jax
import jax.numpy as _jnp

TWIN_FORMAT = 'train_step'
FWD_PARAMS = ['x', 'c', 'ada_w', 'ada_b', 'norm1_w', 'w_in', 'conv_a_w', 'norm_a_w', 'conv_qkv_w', 'a_log', 'dt_bias', 'norm_dn_w', 'w_out', 'norm2_w', 'w_up', 'conv_ff_w', 'w_down', 'norm_f_w']
TWIN_WEIGHTS = ['ada_w', 'ada_b', 'norm1_w', 'w_in', 'conv_a_w', 'norm_a_w', 'conv_qkv_w', 'a_log', 'dt_bias', 'norm_dn_w', 'w_out', 'norm2_w', 'w_up', 'conv_ff_w', 'w_down', 'norm_f_w']
TWIN_DIFF_INPUT = 'x'
TWIN_INPUTS = ['x', 'c', 'ada_w', 'ada_b', 'norm1_w', 'w_in', 'conv_a_w', 'norm_a_w', 'conv_qkv_w', 'a_log', 'dt_bias', 'norm_dn_w', 'w_out', 'norm2_w', 'w_up', 'conv_ff_w', 'w_down', 'norm_f_w', 'loss_target', 'm_ada_w', 'm_ada_b', 'm_norm1_w', 'm_w_in', 'm_conv_a_w', 'm_norm_a_w', 'm_conv_qkv_w', 'm_a_log', 'm_dt_bias', 'm_norm_dn_w', 'm_w_out', 'm_norm2_w', 'm_w_up', 'm_conv_ff_w', 'm_w_down', 'm_norm_f_w', 'v_ada_w', 'v_ada_b', 'v_norm1_w', 'v_w_in', 'v_conv_a_w', 'v_norm_a_w', 'v_conv_qkv_w', 'v_a_log', 'v_dt_bias', 'v_norm_dn_w', 'v_w_out', 'v_norm2_w', 'v_w_up', 'v_conv_ff_w', 'v_w_down', 'v_norm_f_w']
TWIN_OUTPUTS = ['loss', 'grad_x', 'grad_ada_w', 'grad_ada_b', 'grad_norm1_w', 'grad_w_in', 'grad_conv_a_w', 'grad_norm_a_w', 'grad_conv_qkv_w', 'grad_a_log', 'grad_dt_bias', 'grad_norm_dn_w', 'grad_w_out', 'grad_norm2_w', 'grad_w_up', 'grad_conv_ff_w', 'grad_w_down', 'grad_norm_f_w', 'delta_ada_w', 'delta_ada_b', 'delta_norm1_w', 'delta_w_in', 'delta_conv_a_w', 'delta_norm_a_w', 'delta_conv_qkv_w', 'delta_a_log', 'delta_dt_bias', 'delta_norm_dn_w', 'delta_w_out', 'delta_norm2_w', 'delta_w_up', 'delta_conv_ff_w', 'delta_w_down', 'delta_norm_f_w', 'new_m_ada_w', 'new_m_ada_b', 'new_m_norm1_w', 'new_m_w_in', 'new_m_conv_a_w', 'new_m_norm_a_w', 'new_m_conv_qkv_w', 'new_m_a_log', 'new_m_dt_bias', 'new_m_norm_dn_w', 'new_m_w_out', 'new_m_norm2_w', 'new_m_w_up', 'new_m_conv_ff_w', 'new_m_w_down', 'new_m_norm_f_w', 'new_v_ada_w', 'new_v_ada_b', 'new_v_norm1_w', 'new_v_w_in', 'new_v_conv_a_w', 'new_v_norm_a_w', 'new_v_conv_qkv_w', 'new_v_a_log', 'new_v_dt_bias', 'new_v_norm_dn_w', 'new_v_w_out', 'new_v_norm2_w', 'new_v_w_up', 'new_v_conv_ff_w', 'new_v_w_down', 'new_v_norm_f_w']
TWIN_LEAF_KINDS = {'loss': 'loss', 'grad_x': 'grad_x', 'grad_ada_w': 'grad_w', 'grad_ada_b': 'grad_w', 'grad_norm1_w': 'grad_w', 'grad_w_in': 'grad_w', 'grad_conv_a_w': 'grad_w', 'grad_norm_a_w': 'grad_w', 'grad_conv_qkv_w': 'grad_w', 'grad_a_log': 'grad_w', 'grad_dt_bias': 'grad_w', 'grad_norm_dn_w': 'grad_w', 'grad_w_out': 'grad_w', 'grad_norm2_w': 'grad_w', 'grad_w_up': 'grad_w', 'grad_conv_ff_w': 'grad_w', 'grad_w_down': 'grad_w', 'grad_norm_f_w': 'grad_w', 'delta_ada_w': 'delta_w', 'delta_ada_b': 'delta_w', 'delta_norm1_w': 'delta_w', 'delta_w_in': 'delta_w', 'delta_conv_a_w': 'delta_w', 'delta_norm_a_w': 'delta_w', 'delta_conv_qkv_w': 'delta_w', 'delta_a_log': 'delta_w', 'delta_dt_bias': 'delta_w', 'delta_norm_dn_w': 'delta_w', 'delta_w_out': 'delta_w', 'delta_norm2_w': 'delta_w', 'delta_w_up': 'delta_w', 'delta_conv_ff_w': 'delta_w', 'delta_w_down': 'delta_w', 'delta_norm_f_w': 'delta_w', 'new_m_ada_w': 'new_m', 'new_m_ada_b': 'new_m', 'new_m_norm1_w': 'new_m', 'new_m_w_in': 'new_m', 'new_m_conv_a_w': 'new_m', 'new_m_norm_a_w': 'new_m', 'new_m_conv_qkv_w': 'new_m', 'new_m_a_log': 'new_m', 'new_m_dt_bias': 'new_m', 'new_m_norm_dn_w': 'new_m', 'new_m_w_out': 'new_m', 'new_m_norm2_w': 'new_m', 'new_m_w_up': 'new_m', 'new_m_conv_ff_w': 'new_m', 'new_m_w_down': 'new_m', 'new_m_norm_f_w': 'new_m', 'new_v_ada_w': 'new_v', 'new_v_ada_b': 'new_v', 'new_v_norm1_w': 'new_v', 'new_v_w_in': 'new_v', 'new_v_conv_a_w': 'new_v', 'new_v_norm_a_w': 'new_v', 'new_v_conv_qkv_w': 'new_v', 'new_v_a_log': 'new_v', 'new_v_dt_bias': 'new_v', 'new_v_norm_dn_w': 'new_v', 'new_v_w_out': 'new_v', 'new_v_norm2_w': 'new_v', 'new_v_w_up': 'new_v', 'new_v_conv_ff_w': 'new_v', 'new_v_w_down': 'new_v', 'new_v_norm_f_w': 'new_v'}


def _forward(args):
    return _fwd_reference(*[args[k] for k in FWD_PARAMS])


def _output_shape():
    out = _jax.eval_shape(lambda: _forward(_fwd_setup_inputs(0)))
    return out.shape, out.dtype

N_MICROBATCH = 1
ADAM_LR = 0.001
ADAM_B1 = 0.9
ADAM_B2 = 0.999
ADAM_EPS = 1e-08
ADAM_WD = 0.01
ADAM_STEP = 10
PER_EXAMPLE_BATCH_AXIS = {'x': 0, 'c': 0, 'loss_target': 0}
SHARED_INPUTS = []
_WEIGHT_DTYPES = {'ada_w': _jnp.float32, 'ada_b': _jnp.float32, 'norm1_w': _jnp.float32, 'w_in': _jnp.float32, 'conv_a_w': _jnp.float32, 'norm_a_w': _jnp.float32, 'conv_qkv_w': _jnp.float32, 'a_log': _jnp.float32, 'dt_bias': _jnp.float32, 'norm_dn_w': _jnp.float32, 'w_out': _jnp.float32, 'norm2_w': _jnp.float32, 'w_up': _jnp.float32, 'conv_ff_w': _jnp.float32, 'w_down': _jnp.float32, 'norm_f_w': _jnp.float32}
MOMENT_SCALE = {'ada_w': 8.998714e-02, 'ada_b': 1.478875e-01, 'norm1_w': 1.120516e-01, 'w_in': 6.407949e-02, 'conv_a_w': 8.400933e-02, 'norm_a_w': 7.924027e-02, 'conv_qkv_w': 4.013548e-02, 'a_log': 5.815609e-01, 'dt_bias': 5.548155e-01, 'norm_dn_w': 1.030835e-01, 'w_out': 6.672031e-02, 'norm2_w': 7.686120e-02, 'w_up': 3.429468e-02, 'conv_ff_w': 3.462421e-02, 'w_down': 5.600884e-02, 'norm_f_w': 6.423566e+01}


def _to_microbatches(a, axis):
    t = _jnp.moveaxis(a, axis, 0)
    t = t.reshape((N_MICROBATCH, t.shape[0] // N_MICROBATCH) + t.shape[1:])
    return _jnp.moveaxis(t, 1, axis + 1)


def setup_inputs(seed: int = 0) -> dict:
    inp = _fwd_setup_inputs(seed)
    key = _jax.random.fold_in(_jax.random.key(seed), 7919)
    shape, _ = _output_shape()
    out = dict(inp)
    out["loss_target"] = _jax.random.normal(_jax.random.fold_in(key, 0), shape, _jnp.float32)
    for i, name in enumerate(TWIN_WEIGHTS):
        w = inp[name].astype(_jnp.float32)
        if MOMENT_SCALE is None:
            s = _jnp.sqrt(_jnp.mean(_jnp.square(w)) + 1e-30)
        else:
            s = MOMENT_SCALE[name]
        km, kv = _jax.random.split(_jax.random.fold_in(key, i + 1))
        out[name] = w
        out["m_" + name] = s * _jax.random.normal(km, w.shape, _jnp.float32)
        out["v_" + name] = (s * s) * _jax.random.uniform(kv, w.shape, _jnp.float32, 0.5, 1.5)
    if N_MICROBATCH > 1:
        for name, axis in PER_EXAMPLE_BATCH_AXIS.items():
            out[name] = _to_microbatches(out[name], axis)
    return {'x': out['x'], 'c': out['c'], 'ada_w': out['ada_w'], 'ada_b': out['ada_b'], 'norm1_w': out['norm1_w'], 'w_in': out['w_in'], 'conv_a_w': out['conv_a_w'], 'norm_a_w': out['norm_a_w'], 'conv_qkv_w': out['conv_qkv_w'], 'a_log': out['a_log'], 'dt_bias': out['dt_bias'], 'norm_dn_w': out['norm_dn_w'], 'w_out': out['w_out'], 'norm2_w': out['norm2_w'], 'w_up': out['w_up'], 'conv_ff_w': out['conv_ff_w'], 'w_down': out['w_down'], 'norm_f_w': out['norm_f_w'], 'loss_target': out['loss_target'], 'm_ada_w': out['m_ada_w'], 'm_ada_b': out['m_ada_b'], 'm_norm1_w': out['m_norm1_w'], 'm_w_in': out['m_w_in'], 'm_conv_a_w': out['m_conv_a_w'], 'm_norm_a_w': out['m_norm_a_w'], 'm_conv_qkv_w': out['m_conv_qkv_w'], 'm_a_log': out['m_a_log'], 'm_dt_bias': out['m_dt_bias'], 'm_norm_dn_w': out['m_norm_dn_w'], 'm_w_out': out['m_w_out'], 'm_norm2_w': out['m_norm2_w'], 'm_w_up': out['m_w_up'], 'm_conv_ff_w': out['m_conv_ff_w'], 'm_w_down': out['m_w_down'], 'm_norm_f_w': out['m_norm_f_w'], 'v_ada_w': out['v_ada_w'], 'v_ada_b': out['v_ada_b'], 'v_norm1_w': out['v_norm1_w'], 'v_w_in': out['v_w_in'], 'v_conv_a_w': out['v_conv_a_w'], 'v_norm_a_w': out['v_norm_a_w'], 'v_conv_qkv_w': out['v_conv_qkv_w'], 'v_a_log': out['v_a_log'], 'v_dt_bias': out['v_dt_bias'], 'v_norm_dn_w': out['v_norm_dn_w'], 'v_w_out': out['v_w_out'], 'v_norm2_w': out['v_norm2_w'], 'v_w_up': out['v_w_up'], 'v_conv_ff_w': out['v_conv_ff_w'], 'v_w_down': out['v_w_down'], 'v_norm_f_w': out['v_norm_f_w']}


def _loss(weights, diff, rest, loss_target):
    with _jax.named_scope("forward"):
        args = {**rest, TWIN_DIFF_INPUT: diff, **{k: w.astype(_WEIGHT_DTYPES[k]) for k, w in weights.items()}}
        y = _forward(args)
    with _jax.named_scope("loss_head"):
        err = _jnp.square(y.astype(_jnp.float32) - loss_target)
        return 0.5 * _jnp.sum(_jnp.mean(err, axis=-1)) if err.ndim else 0.5 * err


def _adamw(w, g, m, v):
    m = ADAM_B1 * m + (1.0 - ADAM_B1) * g
    v = ADAM_B2 * v + (1.0 - ADAM_B2) * _jnp.square(g)
    m_hat = m / (1.0 - ADAM_B1 ** ADAM_STEP)
    v_hat = v / (1.0 - ADAM_B2 ** ADAM_STEP)
    delta = -ADAM_LR * (m_hat / (_jnp.sqrt(v_hat) + ADAM_EPS) + ADAM_WD * w)
    return delta, m, v


def reference(x, c, ada_w, ada_b, norm1_w, w_in, conv_a_w, norm_a_w, conv_qkv_w, a_log, dt_bias, norm_dn_w, w_out, norm2_w, w_up, conv_ff_w, w_down, norm_f_w, loss_target, m_ada_w, m_ada_b, m_norm1_w, m_w_in, m_conv_a_w, m_norm_a_w, m_conv_qkv_w, m_a_log, m_dt_bias, m_norm_dn_w, m_w_out, m_norm2_w, m_w_up, m_conv_ff_w, m_w_down, m_norm_f_w, v_ada_w, v_ada_b, v_norm1_w, v_w_in, v_conv_a_w, v_norm_a_w, v_conv_qkv_w, v_a_log, v_dt_bias, v_norm_dn_w, v_w_out, v_norm2_w, v_w_up, v_conv_ff_w, v_w_down, v_norm_f_w):
    given = dict(x=x, c=c, ada_w=ada_w, ada_b=ada_b, norm1_w=norm1_w, w_in=w_in, conv_a_w=conv_a_w, norm_a_w=norm_a_w, conv_qkv_w=conv_qkv_w, a_log=a_log, dt_bias=dt_bias, norm_dn_w=norm_dn_w, w_out=w_out, norm2_w=norm2_w, w_up=w_up, conv_ff_w=conv_ff_w, w_down=w_down, norm_f_w=norm_f_w, loss_target=loss_target, m_ada_w=m_ada_w, m_ada_b=m_ada_b, m_norm1_w=m_norm1_w, m_w_in=m_w_in, m_conv_a_w=m_conv_a_w, m_norm_a_w=m_norm_a_w, m_conv_qkv_w=m_conv_qkv_w, m_a_log=m_a_log, m_dt_bias=m_dt_bias, m_norm_dn_w=m_norm_dn_w, m_w_out=m_w_out, m_norm2_w=m_norm2_w, m_w_up=m_w_up, m_conv_ff_w=m_conv_ff_w, m_w_down=m_w_down, m_norm_f_w=m_norm_f_w, v_ada_w=v_ada_w, v_ada_b=v_ada_b, v_norm1_w=v_norm1_w, v_w_in=v_w_in, v_conv_a_w=v_conv_a_w, v_norm_a_w=v_norm_a_w, v_conv_qkv_w=v_conv_qkv_w, v_a_log=v_a_log, v_dt_bias=v_dt_bias, v_norm_dn_w=v_norm_dn_w, v_w_out=v_w_out, v_norm2_w=v_norm2_w, v_w_up=v_w_up, v_conv_ff_w=v_conv_ff_w, v_w_down=v_w_down, v_norm_f_w=v_norm_f_w)
    weights = {n: given[n] for n in TWIN_WEIGHTS}
    shared = {n: given[n] for n in SHARED_INPUTS}
    per_example = {n: given[n] for n in ['x', 'c']}
    grad_fn = _jax.value_and_grad(_loss, argnums=(0, 1))

    def one_microbatch(ex, loss_target):
        ex = dict(ex)
        diff = ex.pop(TWIN_DIFF_INPUT)
        return grad_fn(weights, diff, {**shared, **ex}, loss_target)

    if N_MICROBATCH == 1:
        loss, (grad_w, grad_x) = one_microbatch(per_example, given["loss_target"])
    else:
        def body(carry, xs):
            loss_sum, grad_sum = carry
            l_k, (gw_k, gx_k) = one_microbatch(xs[0], xs[1])
            with _jax.named_scope("update"):
                return (loss_sum + l_k, _jax.tree.map(_jnp.add, grad_sum, gw_k)), gx_k

        init = (_jnp.zeros((), _jnp.float32), _jax.tree.map(_jnp.zeros_like, weights))
        (loss, grad_w), grad_x = _jax.lax.scan(body, init, (per_example, given["loss_target"]))
    with _jax.named_scope("update"):
        delta_w, new_m, new_v = {}, {}, {}
        for n in TWIN_WEIGHTS:
            delta_w[n], new_m[n], new_v[n] = _adamw(weights[n], grad_w[n], given["m_" + n], given["v_" + n])
    return (loss, grad_x, *[grad_w[n] for n in TWIN_WEIGHTS], *[delta_w[n] for n in TWIN_WEIGHTS],
            *[new_m[n] for n in TWIN_WEIGHTS], *[new_v[n] for n in TWIN_WEIGHTS])
```

```python
import functools
import math

import jax
import jax.numpy as jnp
from jax import lax
from jax.experimental import pallas as pl
from jax.experimental.pallas import tpu as pltpu

F32 = jnp.float32
MXU = jnp.bfloat16

D = 1024
DEPTH = 4
N_MOD = 6
AW = 512
A_GROUP = 64
H = 4
HD = 128
CK = 64
DFF = 2816
P_IN = 3592
P_PAD = 3712
EPS = 1e-6
N_DEV = 8
LANES = 128
SUB = 8
VMEM_LIMIT = 56 * 1024 * 1024

ADAM_LR, ADAM_B1, ADAM_B2, ADAM_EPS, ADAM_WD, ADAM_STEP = 0.001, 0.9, 0.999, 1e-08, 0.01, 10

NN = ((1,), (0,))
NT = ((1,), (1,))
TN = ((0,), (0,))
HI = lax.Precision.HIGHEST
MESH = pl.DeviceIdType.MESH


def _dot(a, b, dims, prec=None):
    return lax.dot_general(a, b, (dims, ((), ())), precision=prec, preferred_element_type=F32)


def _params(n_grid=0, limit=VMEM_LIMIT):
    sem = ("arbitrary",) * n_grid if n_grid else None
    return pltpu.CompilerParams(dimension_semantics=sem, vmem_limit_bytes=limit)


def _tile(n, want):
    if n <= want:
        return n
    t = want - want % SUB
    while n % t:
        t -= SUB
    assert t > 0, (n, want)
    return t


def _full(shape):
    nd = len(shape)
    return pl.BlockSpec(shape, lambda *_: (0,) * nd)


def _sig(x):
    return jax.nn.sigmoid(x)


def _rms(x):
    r = lax.rsqrt(jnp.mean(x * x, axis=-1, keepdims=True) + EPS)
    return x * r, r


def _rms_bwd(dn, n, r):
    return r * (dn - n * jnp.mean(dn * n, axis=-1, keepdims=True))


def _l2_bwd(dn, n, r):
    return r * (dn - n * jnp.sum(dn * n, axis=-1, keepdims=True))


def _sum0(x):
    return jnp.sum(x, axis=0, keepdims=True)


def _shift_down(x, s, halo):
    r = pltpu.roll(x, s, 0)
    row = lax.broadcasted_iota(jnp.int32, x.shape, 0)
    for k in range(s):
        r = jnp.where(row == k, halo[SUB - s + k:SUB - s + k + 1, :], r)
    return r


def _shift_up(x, s, halo):
    t = x.shape[0]
    r = pltpu.roll(x, t - s, 0)
    row = lax.broadcasted_iota(jnp.int32, x.shape, 0)
    for k in range(s):
        r = jnp.where(row == t - s + k, halo[k:k + 1, :], r)
    return r


def _conv_fwd(x, w_ref, width, halo):
    sh = [x] + [_shift_down(x, s, halo) for s in range(1, width)]
    out = w_ref[width - 1:width, :] * sh[0]
    for s in range(1, width):
        out = out + w_ref[width - 1 - s:width - s, :] * sh[s]
    return out, sh


def _conv_bwd_in(dout, w_ref, width, halo_next):
    dx = w_ref[width - 1:width, :] * dout
    for s in range(1, width):
        dx = dx + w_ref[width - 1 - s:width - s, :] * _shift_up(dout, s, halo_next)
    return dx


def _blockdiag_mean(n, group):
    r = lax.shift_right_logical(lax.broadcasted_iota(jnp.int32, (n, n), 0), int(math.log2(group)))
    c = lax.shift_right_logical(lax.broadcasted_iota(jnp.int32, (n, n), 1), int(math.log2(group)))
    return jnp.where(r == c, 1.0 / group, 0.0).astype(F32)


def _softplus(x):
    return jnp.maximum(x, 0.0) + jnp.log(1.0 + jnp.exp(-jnp.abs(x)))


def _my_place():
    return lax.axis_index("x"), lax.axis_index("y"), lax.axis_index("c")


def _all_gather(x_shard, name, in_vmem):
    m_per, n = x_shard.shape

    def body(x_ref, out_ref, send_sems, recv_sems, local_sem):
        x, y, c = _my_place()
        me, sibling = (x, y, c), (x, y, 1 - c)
        chips = [(1 - x, y), (x, 1 - y), (1 - x, 1 - y)]

        def rows(px, py, pc):
            return out_ref.at[pl.ds((4 * px + 2 * py + pc) * m_per, m_per), :]

        def copy(k, block, to, src=None):
            return pltpu.make_async_remote_copy(
                src_ref=rows(*block) if src is None else src, dst_ref=rows(*block),
                send_sem=send_sems.at[k], recv_sem=recv_sems.at[k], device_id=to, device_id_type=MESH)

        mine = pltpu.make_async_copy(x_ref, rows(*me), local_sem)
        mine.start()
        first = [copy(0, me, sibling, src=x_ref)]
        first += [copy(1 + j, me, (*chip, c), src=x_ref) for j, chip in enumerate(chips)]
        for cp in first:
            cp.start()
        passed = [copy(4 + j, (*chip, c), sibling) for j, chip in enumerate(chips)]
        for j, chip in enumerate(chips):
            copy(1 + j, (*chip, c), me).wait_recv()
            passed[j].start()
        copy(0, sibling, me).wait_recv()
        for j, chip in enumerate(chips):
            copy(4 + j, (*chip, 1 - c), me).wait_recv()
        for cp in first + passed:
            cp.wait_send()
        mine.wait()

    space = pltpu.VMEM if in_vmem else pl.ANY
    return pl.pallas_call(
        body, name=name,
        out_shape=jax.ShapeDtypeStruct((N_DEV * m_per, n), x_shard.dtype),
        in_specs=[pl.BlockSpec(memory_space=space)],
        out_specs=pl.BlockSpec(memory_space=space),
        scratch_shapes=[pltpu.SemaphoreType.DMA((7,)), pltpu.SemaphoreType.DMA((7,)), pltpu.SemaphoreType.DMA],
        compiler_params=pltpu.CompilerParams(vmem_limit_bytes=VMEM_LIMIT),
    )(x_shard)


def _rs_sibling(g):
    _, r, n = g.shape

    def body(g_ref, recv_ref, send_sems, recv_sems):
        x, y, c = _my_place()
        copies = [pltpu.make_async_remote_copy(
            src_ref=g_ref.at[2 * j + (1 - c)], dst_ref=recv_ref.at[j],
            send_sem=send_sems.at[j], recv_sem=recv_sems.at[j], device_id=(x, y, 1 - c), device_id_type=MESH)
            for j in range(4)]
        for cp in copies:
            cp.start()
        for cp in copies:
            cp.wait()

    return pl.pallas_call(
        body, name="rs_sibling",
        out_shape=jax.ShapeDtypeStruct((4, r, n), g.dtype),
        in_specs=[pl.BlockSpec(memory_space=pl.ANY)], out_specs=pl.BlockSpec(memory_space=pl.ANY),
        scratch_shapes=[pltpu.SemaphoreType.DMA((4,)), pltpu.SemaphoreType.DMA((4,))],
    )(g)


def _rs_chips(pb):
    _, r, n = pb.shape

    def body(p_ref, recv_ref, send_sems, recv_sems):
        x, y, c = _my_place()
        chips = [(1 - x, y), (x, 1 - y), (1 - x, 1 - y)]
        copies = [pltpu.make_async_remote_copy(
            src_ref=p_ref.at[2 * px + py], dst_ref=recv_ref.at[t],
            send_sem=send_sems.at[t], recv_sem=recv_sems.at[t], device_id=(px, py, c), device_id_type=MESH)
            for t, (px, py) in enumerate(chips)]
        for cp in copies:
            cp.start()
        for cp in copies:
            cp.wait()

    return pl.pallas_call(
        body, name="rs_chips",
        out_shape=jax.ShapeDtypeStruct((3, r, n), pb.dtype),
        in_specs=[pl.BlockSpec(memory_space=pl.ANY)], out_specs=pl.BlockSpec(memory_space=pl.ANY),
        scratch_shapes=[pltpu.SemaphoreType.DMA((3,)), pltpu.SemaphoreType.DMA((3,))],
    )(pb)


def _rs_add_pairs(g, recv, my_c):
    _, r, n = g.shape
    tr = _tile(r, 1136)

    def body(c_ref, g_ref, r_ref, pf_ref, pb_ref):
        s = g_ref[...] + r_ref[...]
        pf_ref[...] = s
        pb_ref[...] = s.astype(MXU)

    spec_j = pl.BlockSpec((None, tr, n), lambda j, i, c_ref: (j, i, 0))
    return pl.pallas_call(
        body, name="rs_add_pairs",
        grid_spec=pltpu.PrefetchScalarGridSpec(
            num_scalar_prefetch=1, grid=(4, r // tr),
            in_specs=[pl.BlockSpec((None, tr, n), lambda j, i, c_ref: (2 * j + c_ref[0], i, 0)), spec_j],
            out_specs=[spec_j, spec_j]),
        out_shape=[jax.ShapeDtypeStruct((4, r, n), F32), jax.ShapeDtypeStruct((4, r, n), MXU)],
        compiler_params=_params(2),
    )(my_c, g, recv)


def _rs_add_chips(pf, recv, my_chip):
    _, r, n = pf.shape
    tr = _tile(r, 1136)

    def body(j_ref, p_ref, r_ref, o_ref):
        s = p_ref[...]
        for t in range(3):
            s = s + r_ref[t].astype(F32)
        o_ref[...] = s

    return pl.pallas_call(
        body, name="rs_add_chips",
        grid_spec=pltpu.PrefetchScalarGridSpec(
            num_scalar_prefetch=1, grid=(r // tr,),
            in_specs=[pl.BlockSpec((None, tr, n), lambda i, j_ref: (j_ref[0], i, 0)),
                      pl.BlockSpec((3, tr, n), lambda i, j_ref: (0, i, 0))],
            out_specs=pl.BlockSpec((tr, n), lambda i, j_ref: (i, 0))),
        out_shape=jax.ShapeDtypeStruct((r, n), F32),
        compiler_params=_params(1),
    )(my_chip, pf, recv)


def _sum_devices(g):
    _, r, n = g.shape

    def body(g_ref, o_ref):
        s = g_ref[0]
        for t in range(1, N_DEV):
            s = s + g_ref[t]
        o_ref[...] = s

    return pl.pallas_call(
        body, name="sum_devices", out_shape=jax.ShapeDtypeStruct((r, n), F32),
        in_specs=[pl.BlockSpec(memory_space=pltpu.VMEM)], out_specs=pl.BlockSpec(memory_space=pltpu.VMEM),
        compiler_params=pltpu.CompilerParams(vmem_limit_bytes=VMEM_LIMIT),
    )(g)


def _mod_fwd(c_all, ada_w, ada_b_cols):
    nl, _, nc = ada_w.shape

    def body(c_ref, w_ref, b_ref, o_ref):
        cv = c_ref[...]
        act = (cv * _sig(cv)).astype(MXU)
        o_ref[...] = _dot(act, w_ref[...].astype(MXU), NN) + b_ref[...]

    return pl.pallas_call(
        body, name="mod_fwd", grid=(nl,),
        in_specs=[_full((16, D)), pl.BlockSpec((None, D, nc), lambda i: (i, 0, 0)),
                  pl.BlockSpec((None, 1, nc), lambda i: (i, 0, 0))],
        out_specs=pl.BlockSpec((None, 16, nc), lambda i: (i, 0, 0)),
        out_shape=jax.ShapeDtypeStruct((nl, 16, nc), F32), compiler_params=_params(1),
    )(c_all, ada_w, ada_b_cols)


def _mod_bwd(c_all, dmod_cols):
    nl, _, nc = dmod_cols.shape

    def body(c_ref, d_ref, o_ref):
        cv = c_ref[...]
        act = (cv * _sig(cv)).astype(MXU)
        o_ref[...] = _dot(act, d_ref[...].astype(MXU), TN)

    return pl.pallas_call(
        body, name="mod_bwd", grid=(nl,),
        in_specs=[_full((16, D)), pl.BlockSpec((None, 16, nc), lambda i: (i, 0, 0))],
        out_specs=pl.BlockSpec((None, D, nc), lambda i: (i, 0, 0)),
        out_shape=jax.ShapeDtypeStruct((nl, D, nc), F32), compiler_params=_params(1),
    )(c_all, dmod_cols)


def _in_proj(x, modrows, vec, w_in):
    L = x.shape[0]
    T = _tile(L, 256)

    def body(x_ref, mod_ref, vec_ref, w_ref, p_ref, h_ref):
        n, _ = _rms(x_ref[...])
        h = n * vec_ref[0:1, :] * (1.0 + mod_ref[1:2, :]) + mod_ref[0:1, :]
        hb = h.astype(MXU)
        h_ref[...] = hb
        p_ref[...] = _dot(hb, w_ref[...], NN)

    return pl.pallas_call(
        body, name="in_proj", grid=(L // T,),
        in_specs=[pl.BlockSpec((T, D), lambda i: (i, 0)), _full((SUB, D)), _full((SUB, D)), _full((D, P_PAD))],
        out_specs=[pl.BlockSpec((T, P_PAD), lambda i: (i, 0)), pl.BlockSpec((T, D), lambda i: (i, 0))],
        out_shape=[jax.ShapeDtypeStruct((L, P_PAD), F32), jax.ShapeDtypeStruct((L, D), MXU)],
        compiler_params=_params(1),
    )(x, modrows, vec, w_in)


def _gate_small(s, sp_ref):
    lane = lax.broadcasted_iota(jnp.int32, s.shape, 1)
    a = -jnp.exp(sp_ref[0:1, :])
    xb = s + sp_ref[1:2, :]
    beta = _sig(s)
    g = a * _softplus(xb)
    return lane, a, xb, beta, g


def _pre_fwd(p, pa, cq, sp):
    L = p.shape[0]
    T = _tile(L, 256)
    scale = HD ** -0.5

    def body(pm_ref, ps_ref, pa_ref, cq_ref, sp_ref, qn_ref, kn_ref, vs_ref, gb_ref, ya_ref, u_carry, q_carry):
        @pl.when(pl.program_id(0) == 0)
        def _():
            u_carry[...] = jnp.zeros_like(u_carry)
            q_carry[...] = jnp.zeros_like(q_carry)

        a_b = pm_ref[:, 0:AW]
        u = pm_ref[:, AW:2 * AW] * pm_ref[:, 2 * AW:3 * AW]
        cu, _ = _conv_fwd(u, pa_ref, 3, u_carry[...])
        u_carry[...] = u[T - SUB:T, :]
        yp = a_b * cu
        ms = _dot(yp * yp, _blockdiag_mean(AW, A_GROUP), NN, HI)
        ya_ref[...] = (yp * lax.rsqrt(ms + EPS) * pa_ref[3:4, :]).astype(MXU)

        qkv = pm_ref[:, 3 * AW:3 * AW + 3 * H * HD]
        qc, _ = _conv_fwd(qkv, cq_ref, 4, q_carry[...])
        q_carry[...] = qkv[T - SUB:T, :]
        qs = qc * _sig(qc)
        for h in range(H):
            q = qs[:, h * HD:(h + 1) * HD]
            qn_ref[:, h * HD:(h + 1) * HD] = q * (lax.rsqrt(jnp.sum(q * q, axis=-1, keepdims=True) + EPS) * scale)
            k = qs[:, (H + h) * HD:(H + h + 1) * HD]
            kn_ref[:, h * HD:(h + 1) * HD] = k * lax.rsqrt(jnp.sum(k * k, axis=-1, keepdims=True) + EPS)
        vs_ref[...] = qs[:, 2 * H * HD:3 * H * HD]

        lane, _, _, beta, g = _gate_small(ps_ref[...], sp_ref)
        gb_ref[...] = jnp.where(lane < H, beta, jnp.where(lane < 2 * H, g, 0.0))

    w3 = 3 * AW + 3 * H * HD
    row = lambda i: (i, 0)
    return pl.pallas_call(
        body, name="pre_fwd", grid=(L // T,),
        in_specs=[pl.BlockSpec((T, w3), row), pl.BlockSpec((T, LANES), lambda i: (i, (P_PAD - LANES) // LANES)),
                  _full((SUB, AW)), _full((SUB, 3 * H * HD)), _full((SUB, LANES))],
        out_specs=[pl.BlockSpec((T, H * HD), row)] * 3 + [pl.BlockSpec((T, LANES), row), pl.BlockSpec((T, AW), row)],
        out_shape=[jax.ShapeDtypeStruct((L, H * HD), F32)] * 3
        + [jax.ShapeDtypeStruct((L, LANES), F32), jax.ShapeDtypeStruct((L, AW), MXU)],
        scratch_shapes=[pltpu.VMEM((SUB, AW), F32), pltpu.VMEM((SUB, 3 * H * HD), F32)],
        compiler_params=_params(1),
    )(p, p, pa, cq, sp)


def _gdr_masks():
    r = lax.broadcasted_iota(jnp.int32, (CK, CK), 0)
    c = lax.broadcasted_iota(jnp.int32, (CK, CK), 1)
    return r >= c, r > c


def _head_cols(gbt, h):
    lane = lax.broadcasted_iota(jnp.int32, gbt.shape, 1)
    beta = jnp.sum(jnp.where(lane == h, gbt, 0.0), axis=-1, keepdims=True)
    g = jnp.sum(jnp.where(lane == H + h, gbt, 0.0), axis=-1, keepdims=True)
    return beta, g


def _gdr_chunk(q, k, v, beta, g, prec):
    causal, strict = _gdr_masks()
    tril = jnp.where(causal, 1.0, 0.0).astype(F32)
    gc = _dot(tril, jnp.broadcast_to(g, (CK, HD)), NN, HI)
    g_row = _dot(jnp.full((CK, HD), 1.0 / HD, F32), gc, NT, HI)
    decay = jnp.where(causal, jnp.exp(jnp.where(causal, gc[:, 0:CK] - g_row, 0.0)), 0.0)
    eg = jnp.exp(gc)
    gl = gc[CK - 1:CK, :]
    ek = jnp.exp(gl - gc)
    cd = jnp.exp(gl)
    kb = k * beta
    pk = _dot(kb, k, NT, prec)
    lower = jnp.where(strict, pk * decay, 0.0)
    xp = -lower
    eye = jnp.where(causal & jnp.logical_not(strict), 1.0, 0.0).astype(F32)
    tinv = eye + xp
    for _ in range(5):
        xp = _dot(xp, xp, NN, HI)
        tinv = tinv + _dot(tinv, xp, NN, HI)
    u = _dot(tinv, v * beta, NN, prec)
    w = _dot(tinv, kb * eg, NN, prec)
    qk = _dot(q, k, NT, prec)
    intra = jnp.where(causal, qk * decay, 0.0)
    return dict(gc=gc, decay=decay, eg=eg, ek=ek, cd=cd, kb=kb, pk=pk, tinv=tinv, u=u, w=w, qk=qk, intra=intra,
                q_dec=q * eg, k_dec=k * ek, tril=tril, causal=causal, strict=strict)


def _gdr_fwd(qn, kn, vs, gb, prec):
    L = qn.shape[0]
    nc = L // CK
    cb = min(8, nc)
    rb = cb * CK
    nb = nc // cb

    def body(q_ref, k_ref, v_ref, gb_ref, o_ref, st_ref, s_ref):
        h = pl.program_id(0)

        @pl.when(pl.program_id(1) == 0)
        def _():
            s_ref[...] = jnp.zeros_like(s_ref)

        def chunk(ci, carry):
            rows = pl.ds(pl.multiple_of(ci * CK, CK), CK)
            q, k, v = q_ref[rows, :], k_ref[rows, :], v_ref[rows, :]
            beta, g = _head_cols(gb_ref[rows, :], h)
            t = _gdr_chunk(q, k, v, beta, g, prec)
            s = s_ref[...]
            st_ref[ci] = s
            v_new = t["u"] - _dot(t["w"], s, NN, prec)
            o_ref[rows, :] = _dot(t["q_dec"], s, NN, prec) + _dot(t["intra"], v_new, NN, prec)
            s_ref[...] = s * t["cd"] + _dot(t["k_dec"], v_new, TN, prec)
            return carry

        lax.fori_loop(0, cb, chunk, 0)

    blk = pl.BlockSpec((rb, HD), lambda h, b: (b, h))
    return pl.pallas_call(
        body, name="gdr_fwd", grid=(H, nb),
        in_specs=[blk, blk, blk, pl.BlockSpec((rb, LANES), lambda h, b: (b, 0))],
        out_specs=[blk, pl.BlockSpec((None, cb, HD, HD), lambda h, b: (h, b, 0, 0))],
        out_shape=[jax.ShapeDtypeStruct((L, H * HD), F32), jax.ShapeDtypeStruct((H, nc, HD, HD), F32)],
        scratch_shapes=[pltpu.VMEM((HD, HD), F32)],
        compiler_params=_params(2),
    )(qn, kn, vs, gb)


def _gdr_bwd(qn, kn, vs, gb, states, do, prec):
    L = qn.shape[0]
    nc = L // CK
    cb = min(8, nc)
    rb = cb * CK
    nb = nc // cb

    def body(q_ref, k_ref, v_ref, gb_ref, st_ref, do_ref, dq_ref, dk_ref, dv_ref, dgb_ref, ds_ref):
        h = pl.program_id(0)

        @pl.when(pl.program_id(1) == 0)
        def _():
            ds_ref[...] = jnp.zeros_like(ds_ref)

        def chunk(cj, carry):
            ci = cb - 1 - cj
            rows = pl.ds(pl.multiple_of(ci * CK, CK), CK)
            q, k, v = q_ref[rows, :], k_ref[rows, :], v_ref[rows, :]
            beta, g = _head_cols(gb_ref[rows, :], h)
            t = _gdr_chunk(q, k, v, beta, g, prec)
            causal, strict = t["causal"], t["strict"]
            s = st_ref[ci]
            dout = do_ref[rows, :]
            ds_out = ds_ref[...]
            u, w, tinv, decay = t["u"], t["w"], t["tinv"], t["decay"]
            eg, ek, cd, kb = t["eg"], t["ek"], t["cd"], t["kb"]
            q_dec, k_dec, intra = t["q_dec"], t["k_dec"], t["intra"]
            v_new = u - _dot(w, s, NN, prec)

            dq_dec = _dot(dout, s, NT, prec)
            ds_new = _dot(q_dec, dout, TN, prec) + ds_out * cd
            dintra = jnp.where(causal, _dot(dout, v_new, NT, prec), 0.0)
            dv_new = _dot(intra, dout, TN, prec) + _dot(k_dec, ds_out, NN, prec)
            dk_dec = _dot(v_new, ds_out, NT, prec)
            dcd = jnp.sum(jnp.sum(ds_out * s, axis=1, keepdims=True), axis=0, keepdims=True)
            dw = -_dot(dv_new, s, NT, prec)
            ds_new = ds_new - _dot(w, dv_new, TN, prec)
            dru = _dot(tinv, dv_new, TN, prec)
            drw = _dot(tinv, dw, TN, prec)
            dlower = -jnp.where(strict, _dot(dru, u, NT, prec) + _dot(drw, w, NT, prec), 0.0)
            dv = dru * beta
            dbeta = jnp.sum(dru * v, axis=-1, keepdims=True)
            dkb = drw * eg
            dgc = jnp.sum(drw * kb, axis=-1, keepdims=True) * eg
            dpk = dlower * decay
            dkb = dkb + _dot(dpk, k, NN, prec)
            dk = _dot(dpk, kb, TN, prec)
            dqk = dintra * decay
            dq = _dot(dqk, k, NN, prec)
            dk = dk + _dot(dqk, q, TN, prec)
            m = (dlower * t["pk"] + dintra * t["qk"]) * decay
            dgc = dgc + jnp.sum(m, axis=-1, keepdims=True) - _dot(m, jnp.ones((CK, HD), F32), TN, HI)
            dq = dq + dq_dec * eg
            dgc = dgc + jnp.sum(dq_dec * q_dec, axis=-1, keepdims=True)
            dk = dk + dk_dec * ek
            e = jnp.sum(dk_dec * k_dec, axis=-1, keepdims=True)
            dgl = jnp.sum(e, axis=0, keepdims=True) + dcd * cd
            row = lax.broadcasted_iota(jnp.int32, (CK, HD), 0)
            dgc = dgc - e + jnp.where(row == CK - 1, dgl, 0.0)
            dk = dk + dkb * beta
            dbeta = dbeta + jnp.sum(dkb * k, axis=-1, keepdims=True)
            dg = _dot(t["tril"], dgc, TN, HI)

            dq_ref[rows, :] = dq
            dk_ref[rows, :] = dk
            dv_ref[rows, :] = dv
            lane = lax.broadcasted_iota(jnp.int32, (CK, LANES), 1)
            dgb_ref[rows, :] = jnp.where(lane < LANES // 2, dbeta, dg)
            ds_ref[...] = ds_new
            return carry

        lax.fori_loop(0, cb, chunk, 0)

    blk = pl.BlockSpec((rb, HD), lambda h, b: (nb - 1 - b, h))
    return pl.pallas_call(
        body, name="gdr_bwd", grid=(H, nb),
        in_specs=[blk, blk, blk, pl.BlockSpec((rb, LANES), lambda h, b: (nb - 1 - b, 0)),
                  pl.BlockSpec((None, cb, HD, HD), lambda h, b: (h, nb - 1 - b, 0, 0)), blk],
        out_specs=[blk, blk, blk, pl.BlockSpec((None, rb, LANES), lambda h, b: (h, nb - 1 - b, 0))],
        out_shape=[jax.ShapeDtypeStruct((L, H * HD), F32)] * 3 + [jax.ShapeDtypeStruct((H, L, LANES), F32)],
        scratch_shapes=[pltpu.VMEM((HD, HD), F32)],
        compiler_params=_params(2),
    )(qn, kn, vs, gb, states, do)


def _post_fwd(o, p, ya, x, modrows, sp, w_out):
    L = x.shape[0]
    T = _tile(L, 256)

    def body(o_ref, z_ref, ya_ref, x_ref, mod_ref, sp_ref, w_ref, y_ref, x2_ref, yb_ref):
        ndw = sp_ref[2:3, :]
        z = z_ref[...]
        sz = z * _sig(z)
        parts = []
        for h in range(H):
            n, _ = _rms(o_ref[:, h * HD:(h + 1) * HD])
            parts.append(n * ndw * sz[:, h * HD:(h + 1) * HD])
        yb = jnp.concatenate(parts, axis=-1).astype(MXU)
        yb_ref[...] = yb
        y = _dot(ya_ref[...], w_ref[0:AW, :], NN) + _dot(yb, w_ref[AW:2 * AW, :], NN)
        y_ref[...] = y
        x2_ref[...] = x_ref[...] + mod_ref[2:3, :] * y

    row = lambda i: (i, 0)
    zcol = (3 * AW + 3 * H * HD) // (H * HD)
    return pl.pallas_call(
        body, name="post_fwd", grid=(L // T,),
        in_specs=[pl.BlockSpec((T, H * HD), row), pl.BlockSpec((T, H * HD), lambda i: (i, zcol)),
                  pl.BlockSpec((T, AW), row), pl.BlockSpec((T, D), row), _full((SUB, D)), _full((SUB, LANES)),
                  _full((D, D))],
        out_specs=[pl.BlockSpec((T, D), row), pl.BlockSpec((T, D), row), pl.BlockSpec((T, H * HD), row)],
        out_shape=[jax.ShapeDtypeStruct((L, D), F32), jax.ShapeDtypeStruct((L, D), F32),
                   jax.ShapeDtypeStruct((L, H * HD), MXU)],
        compiler_params=_params(1),
    )(o, p, ya, x, modrows, sp, w_out)


FF_COLS = 11
FF_CW = DFF // FF_COLS
FF_ROWS = 512


def _ffn_fwd(x2, modrows, vec, w_up, cff, w_down):
    L = x2.shape[0]
    T = _tile(L, FF_ROWS)
    nj = FF_COLS

    def body(x_ref, mod_ref, vec_ref, wg_ref, wu_ref, cg_ref, cu_ref, wd_ref,
             h_ref, gp_ref, up_ref, f_ref, d_ref, x3_ref, h_s, acc, carry_g, carry_u):
        i, j = pl.program_id(0), pl.program_id(1)

        @pl.when(i == 0)
        def _():
            carry_g[j] = jnp.zeros((SUB, FF_CW), F32)
            carry_u[j] = jnp.zeros((SUB, FF_CW), F32)

        @pl.when(j == 0)
        def _():
            n, _ = _rms(x_ref[...])
            hb = (n * vec_ref[1:2, :] * (1.0 + mod_ref[4:5, :]) + mod_ref[3:4, :]).astype(MXU)
            h_s[...] = hb
            h_ref[...] = hb
            acc[...] = jnp.zeros_like(acc)

        hb = h_s[...]
        g = _dot(hb, wg_ref[...], NN)
        u = _dot(hb, wu_ref[...], NN)
        gp_ref[...] = g
        up_ref[...] = u
        gc, _ = _conv_fwd(g, cg_ref, 3, carry_g[j])
        uc, _ = _conv_fwd(u, cu_ref, 3, carry_u[j])
        carry_g[j] = g[T - SUB:T, :]
        carry_u[j] = u[T - SUB:T, :]
        fb = (gc * _sig(gc) * uc).astype(MXU)
        f_ref[...] = fb
        acc[...] += _dot(fb, wd_ref[...], NN)

        @pl.when(j == nj - 1)
        def _():
            dv = acc[...]
            d_ref[...] = dv
            x3_ref[...] = x_ref[...] + mod_ref[5:6, :] * dv

    row = lambda i, j: (i, 0)
    col = lambda i, j: (i, j)
    return pl.pallas_call(
        body, name="ffn_fwd", grid=(L // T, nj),
        in_specs=[pl.BlockSpec((T, D), row), _full((SUB, D)), _full((SUB, D)),
                  pl.BlockSpec((D, FF_CW), lambda i, j: (0, j)), pl.BlockSpec((D, FF_CW), lambda i, j: (0, nj + j)),
                  pl.BlockSpec((SUB, FF_CW), lambda i, j: (0, j)), pl.BlockSpec((SUB, FF_CW), lambda i, j: (0, nj + j)),
                  pl.BlockSpec((FF_CW, D), lambda i, j: (j, 0))],
        out_specs=[pl.BlockSpec((T, D), row), pl.BlockSpec((T, FF_CW), col), pl.BlockSpec((T, FF_CW), col),
                   pl.BlockSpec((T, FF_CW), col), pl.BlockSpec((T, D), row), pl.BlockSpec((T, D), row)],
        out_shape=[jax.ShapeDtypeStruct((L, D), MXU), jax.ShapeDtypeStruct((L, DFF), F32),
                   jax.ShapeDtypeStruct((L, DFF), F32), jax.ShapeDtypeStruct((L, DFF), MXU),
                   jax.ShapeDtypeStruct((L, D), F32), jax.ShapeDtypeStruct((L, D), F32)],
        scratch_shapes=[pltpu.VMEM((T, D), MXU), pltpu.VMEM((T, D), F32),
                        pltpu.VMEM((nj, SUB, FF_CW), F32), pltpu.VMEM((nj, SUB, FF_CW), F32)],
        compiler_params=_params(2),
    )(x2, modrows, vec, w_up, w_up, cff, cff, w_down)


def _final(x, target, nf):
    L = x.shape[0]
    T = _tile(L, 256)

    def body(x_ref, t_ref, nf_ref, dx_ref, acc_ref):
        @pl.when(pl.program_id(0) == 0)
        def _():
            acc_ref[...] = jnp.zeros_like(acc_ref)

        n, r = _rms(x_ref[...])
        w = nf_ref[0:1, :]
        err = n * w - t_ref[...]
        acc_ref[0:1, :] += (0.5 / D) * _sum0(err * err)
        dy = err * (1.0 / D)
        acc_ref[1:2, :] += _sum0(dy * n)
        dx_ref[...] = _rms_bwd(dy * w, n, r)

    row = lambda i: (i, 0)
    return pl.pallas_call(
        body, name="final_norm_loss", grid=(L // T,),
        in_specs=[pl.BlockSpec((T, D), row), pl.BlockSpec((T, D), row), _full((SUB, D))],
        out_specs=[pl.BlockSpec((T, D), row), _full((SUB, D))],
        out_shape=[jax.ShapeDtypeStruct((L, D), F32), jax.ShapeDtypeStruct((SUB, D), F32)],
        compiler_params=_params(1),
    )(x, target, nf)


def _ffn_bwd(dx3, d, x2, modrows, vec, gpre, upre, cff, w_down, w_up):
    L = dx3.shape[0]
    T = _tile(L, FF_ROWS)
    ni, nj = L // T, FF_COLS
    hb_per_t = T // SUB

    def body(dx3_ref, d_ref, x2_ref, mod_ref, vec_ref, gp_ref, up_ref, gph_ref, uph_ref, cg_ref, cu_ref,
             wd_ref, wg_ref, wu_ref,
             dd_ref, dgp_ref, dup_ref, dx2_ref, accv_ref, dcg_ref, dcu_ref,
             dd_s, acch, carry_g, carry_u):
        i, j = pl.program_id(0), pl.program_id(1)
        ri = ni - 1 - i

        @pl.when((i == 0) & (j == 0))
        def _():
            accv_ref[...] = jnp.zeros_like(accv_ref)
            dcg_ref[...] = jnp.zeros_like(dcg_ref)
            dcu_ref[...] = jnp.zeros_like(dcu_ref)

        @pl.when(i == 0)
        def _():
            carry_g[j] = jnp.zeros((SUB, FF_CW), F32)
            carry_u[j] = jnp.zeros((SUB, FF_CW), F32)

        @pl.when(j == 0)
        def _():
            dx3v = dx3_ref[...]
            accv_ref[0:1, :] += _sum0(dx3v * d_ref[...])
            ddb = (mod_ref[5:6, :] * dx3v).astype(MXU)
            dd_s[...] = ddb
            dd_ref[...] = ddb
            acch[...] = jnp.zeros_like(acch)

        ddb = dd_s[...]
        g, u = gp_ref[...], up_ref[...]
        keep = jnp.where(ri == 0, 0.0, 1.0)
        gc, gsh = _conv_fwd(g, cg_ref, 3, gph_ref[...] * keep)
        uc, ush = _conv_fwd(u, cu_ref, 3, uph_ref[...] * keep)
        sg = _sig(gc)
        df = _dot(ddb, wd_ref[...], NT)
        duc = df * (gc * sg)
        dgc = df * uc * (sg * (1.0 + gc * (1.0 - sg)))
        for s in range(3):
            dcg_ref[j, 2 - s:3 - s, :] += _sum0(dgc * gsh[s])
            dcu_ref[j, 2 - s:3 - s, :] += _sum0(duc * ush[s])
        dg = _conv_bwd_in(dgc, cg_ref, 3, carry_g[j]).astype(MXU)
        du = _conv_bwd_in(duc, cu_ref, 3, carry_u[j]).astype(MXU)
        carry_g[j] = dgc[0:SUB, :]
        carry_u[j] = duc[0:SUB, :]
        dgp_ref[...] = dg
        dup_ref[...] = du
        acch[...] += _dot(dg, wg_ref[...], NT) + _dot(du, wu_ref[...], NT)

        @pl.when(j == nj - 1)
        def _():
            dh = acch[...]
            n, r = _rms(x2_ref[...])
            nw, sc = vec_ref[1:2, :], mod_ref[4:5, :]
            accv_ref[1:2, :] += _sum0(dh)
            accv_ref[2:3, :] += _sum0(dh * n * nw)
            accv_ref[3:4, :] += _sum0(dh * n * (1.0 + sc))
            dx2_ref[...] = _rms_bwd(dh * nw * (1.0 + sc), n, r) + dx3_ref[...]

    row = lambda i, j: (ni - 1 - i, 0)
    col = lambda i, j: (ni - 1 - i, j)
    halo = lambda i, j: (jnp.maximum((ni - 1 - i) * hb_per_t - 1, 0), j)
    return pl.pallas_call(
        body, name="ffn_bwd", grid=(ni, nj),
        in_specs=[pl.BlockSpec((T, D), row), pl.BlockSpec((T, D), row), pl.BlockSpec((T, D), row),
                  _full((SUB, D)), _full((SUB, D)),
                  pl.BlockSpec((T, FF_CW), col), pl.BlockSpec((T, FF_CW), col),
                  pl.BlockSpec((SUB, FF_CW), halo), pl.BlockSpec((SUB, FF_CW), halo),
                  pl.BlockSpec((SUB, FF_CW), lambda i, j: (0, j)), pl.BlockSpec((SUB, FF_CW), lambda i, j: (0, nj + j)),
                  pl.BlockSpec((FF_CW, D), lambda i, j: (j, 0)),
                  pl.BlockSpec((D, FF_CW), lambda i, j: (0, j)), pl.BlockSpec((D, FF_CW), lambda i, j: (0, nj + j))],
        out_specs=[pl.BlockSpec((T, D), row), pl.BlockSpec((T, FF_CW), col), pl.BlockSpec((T, FF_CW), col),
                   pl.BlockSpec((T, D), row), _full((SUB, D)), _full((nj, SUB, FF_CW)), _full((nj, SUB, FF_CW))],
        out_shape=[jax.ShapeDtypeStruct((L, D), MXU), jax.ShapeDtypeStruct((L, DFF), MXU),
                   jax.ShapeDtypeStruct((L, DFF), MXU), jax.ShapeDtypeStruct((L, D), F32),
                   jax.ShapeDtypeStruct((SUB, D), F32), jax.ShapeDtypeStruct((nj, SUB, FF_CW), F32),
                   jax.ShapeDtypeStruct((nj, SUB, FF_CW), F32)],
        scratch_shapes=[pltpu.VMEM((T, D), MXU), pltpu.VMEM((T, D), F32),
                        pltpu.VMEM((nj, SUB, FF_CW), F32), pltpu.VMEM((nj, SUB, FF_CW), F32)],
        compiler_params=_params(2),
    )(dx3, d, x2, modrows, vec, gpre, upre, gpre, upre, cff, cff, w_down, w_up, w_up)


def _post_bwd(dx2, y, o, p, modrows, sp, w_out):
    L = dx2.shape[0]
    T = _tile(L, 256)

    def body(dx2_ref, y_ref, o_ref, z_ref, mod_ref, sp_ref, w_ref, dy_ref, do_ref, dz_ref, dya_ref, accv_ref, accs_ref):
        @pl.when(pl.program_id(0) == 0)
        def _():
            accv_ref[...] = jnp.zeros_like(accv_ref)
            accs_ref[...] = jnp.zeros_like(accs_ref)

        dx2v = dx2_ref[...]
        accv_ref[0:1, :] += _sum0(dx2v * y_ref[...])
        dyb = (mod_ref[2:3, :] * dx2v).astype(MXU)
        dy_ref[...] = dyb
        dyc = _dot(dyb, w_ref[...], NT)
        dya_ref[...] = dyc[:, 0:AW]
        ndw = sp_ref[2:3, :]
        z = z_ref[...]
        sgz = _sig(z)
        dsz = sgz * (1.0 + z * (1.0 - sgz))
        dndw = jnp.zeros((1, HD), F32)
        for h in range(H):
            sl = slice(h * HD, (h + 1) * HD)
            n, r = _rms(o_ref[:, sl])
            dyh = dyc[:, AW + h * HD:AW + (h + 1) * HD]
            zh = z[:, sl]
            don = dyh * (zh * sgz[:, sl])
            dz_ref[:, sl] = dyh * (n * ndw) * dsz[:, sl]
            dndw = dndw + _sum0(don * n)
            do_ref[:, sl] = _rms_bwd(don * ndw, n, r)
        accs_ref[0:1, :] += dndw

    row = lambda i: (i, 0)
    zcol = (3 * AW + 3 * H * HD) // (H * HD)
    return pl.pallas_call(
        body, name="post_bwd", grid=(L // T,),
        in_specs=[pl.BlockSpec((T, D), row), pl.BlockSpec((T, D), row), pl.BlockSpec((T, H * HD), row),
                  pl.BlockSpec((T, H * HD), lambda i: (i, zcol)), _full((SUB, D)), _full((SUB, LANES)), _full((D, D))],
        out_specs=[pl.BlockSpec((T, D), row)] + [pl.BlockSpec((T, H * HD), row)] * 3 + [_full((SUB, D)), _full((SUB, LANES))],
        out_shape=[jax.ShapeDtypeStruct((L, D), MXU)] + [jax.ShapeDtypeStruct((L, H * HD), F32)] * 3
        + [jax.ShapeDtypeStruct((SUB, D), F32), jax.ShapeDtypeStruct((SUB, LANES), F32)],
        compiler_params=_params(1),
    )(dx2, y, o, p, modrows, sp, w_out)


def _pre_bwd(p, dqn, dkn, dvs, dya, dz, dgb, pa, cq, sp):
    L = p.shape[0]
    T = _tile(L, 256)
    ni = L // T
    scale = HD ** -0.5
    w3 = 3 * AW + 3 * H * HD
    hb_per_t = T // SUB

    def body(pm_ref, ph_ref, ps_ref, dq_ref, dk_ref, dv_ref, dya_ref, dz_ref, dgb_ref, pa_ref, cq_ref, sp_ref,
             dp_ref, dpa_ref, dcq_ref, dsp_ref, carry_u, carry_q):
        i = pl.program_id(0)
        ri = ni - 1 - i

        @pl.when(i == 0)
        def _():
            dpa_ref[...] = jnp.zeros_like(dpa_ref)
            dcq_ref[...] = jnp.zeros_like(dcq_ref)
            dsp_ref[...] = jnp.zeros_like(dsp_ref)
            carry_u[...] = jnp.zeros_like(carry_u)
            carry_q[...] = jnp.zeros_like(carry_q)

        keep = jnp.where(ri == 0, 0.0, 1.0)
        a_b, a_c, a_x = pm_ref[:, 0:AW], pm_ref[:, AW:2 * AW], pm_ref[:, 2 * AW:3 * AW]
        u = a_c * a_x
        hu = ph_ref[:, AW:2 * AW] * ph_ref[:, 2 * AW:3 * AW] * keep
        cu, ush = _conv_fwd(u, pa_ref, 3, hu)
        yp = a_b * cu
        bd = _blockdiag_mean(AW, A_GROUP)
        ra = lax.rsqrt(_dot(yp * yp, bd, NN, HI) + EPS)
        na = yp * ra
        dya = dya_ref[...]
        dpa_ref[3:4, :] += _sum0(dya * na)
        dna = dya * pa_ref[3:4, :]
        dyp = ra * (dna - na * _dot(dna * na, bd, NN, HI))
        dcu = dyp * a_b
        for s in range(3):
            dpa_ref[2 - s:3 - s, :] += _sum0(dcu * ush[s])
        du = _conv_bwd_in(dcu, pa_ref, 3, carry_u[...])
        carry_u[...] = dcu[0:SUB, :]
        dp_ref[:, 0:AW] = (dyp * cu).astype(MXU)
        dp_ref[:, AW:2 * AW] = (du * a_x).astype(MXU)
        dp_ref[:, 2 * AW:3 * AW] = (du * a_c).astype(MXU)

        qkv = pm_ref[:, 3 * AW:w3]
        qc, qsh = _conv_fwd(qkv, cq_ref, 4, ph_ref[:, 3 * AW:w3] * keep)
        sg = _sig(qc)
        qs = qc * sg
        parts = []
        for h in range(H):
            q = qs[:, h * HD:(h + 1) * HD]
            rq = lax.rsqrt(jnp.sum(q * q, axis=-1, keepdims=True) + EPS)
            parts.append(_l2_bwd(dq_ref[:, h * HD:(h + 1) * HD] * scale, q * rq, rq))
        for h in range(H):
            k = qs[:, (H + h) * HD:(H + h + 1) * HD]
            rk = lax.rsqrt(jnp.sum(k * k, axis=-1, keepdims=True) + EPS)
            parts.append(_l2_bwd(dk_ref[:, h * HD:(h + 1) * HD], k * rk, rk))
        parts.append(dv_ref[...])
        dqc = jnp.concatenate(parts, axis=-1) * (sg * (1.0 + qc * (1.0 - sg)))
        for s in range(4):
            dcq_ref[3 - s:4 - s, :] += _sum0(dqc * qsh[s])
        dp_ref[:, 3 * AW:w3] = _conv_bwd_in(dqc, cq_ref, 4, carry_q[...]).astype(MXU)
        carry_q[...] = dqc[0:SUB, :]
        dp_ref[:, w3:w3 + H * HD] = dz_ref[...].astype(MXU)

        lane, a, xb, beta, g = _gate_small(ps_ref[...], sp_ref)
        dbeta = jnp.zeros((T, LANES), F32)
        dg = jnp.zeros((T, LANES), F32)
        for h in range(H):
            t = dgb_ref[h]
            dbeta = dbeta + jnp.where(lane == h, t[:, 0:1], 0.0)
            dg = dg + jnp.where(lane == H + h, t[:, LANES // 2:LANES // 2 + 1], 0.0)
        dalpha = dg * a * _sig(xb)
        dsp_ref[0:1, :] += _sum0(dg * g)
        dsp_ref[1:2, :] += _sum0(dalpha)
        dp_ref[:, w3 + H * HD:P_PAD] = (dbeta * beta * (1.0 - beta) + dalpha).astype(MXU)

    row = lambda i: (ni - 1 - i, 0)
    halo = lambda i: (jnp.maximum((ni - 1 - i) * hb_per_t - 1, 0), 0)
    hrow = pl.BlockSpec((T, H * HD), row)
    return pl.pallas_call(
        body, name="pre_bwd", grid=(ni,),
        in_specs=[pl.BlockSpec((T, w3), row), pl.BlockSpec((SUB, w3), halo),
                  pl.BlockSpec((T, LANES), lambda i: (ni - 1 - i, (P_PAD - LANES) // LANES)),
                  hrow, hrow, hrow, pl.BlockSpec((T, AW), row), hrow,
                  pl.BlockSpec((H, T, LANES), lambda i: (0, ni - 1 - i, 0)),
                  _full((SUB, AW)), _full((SUB, 3 * H * HD)), _full((SUB, LANES))],
        out_specs=[pl.BlockSpec((T, P_PAD), row), _full((SUB, AW)), _full((SUB, 3 * H * HD)), _full((SUB, LANES))],
        out_shape=[jax.ShapeDtypeStruct((L, P_PAD), MXU), jax.ShapeDtypeStruct((SUB, AW), F32),
                   jax.ShapeDtypeStruct((SUB, 3 * H * HD), F32), jax.ShapeDtypeStruct((SUB, LANES), F32)],
        scratch_shapes=[pltpu.VMEM((SUB, AW), F32), pltpu.VMEM((SUB, 3 * H * HD), F32)],
        compiler_params=_params(1),
    )(p, p, p, dqn, dkn, dvs, dya, dz, dgb, pa, cq, sp)


def _in_bwd(dp, w_in, x, dx2, modrows, vec):
    L = x.shape[0]
    T = _tile(L, 256)

    def body(dp_ref, w_ref, x_ref, dx2_ref, mod_ref, vec_ref, dx_ref, accv_ref):
        @pl.when(pl.program_id(0) == 0)
        def _():
            accv_ref[...] = jnp.zeros_like(accv_ref)

        dh = _dot(dp_ref[...], w_ref[...], NT)
        n, r = _rms(x_ref[...])
        nw, sc = vec_ref[0:1, :], mod_ref[1:2, :]
        accv_ref[0:1, :] += _sum0(dh)
        accv_ref[1:2, :] += _sum0(dh * n * nw)
        accv_ref[2:3, :] += _sum0(dh * n * (1.0 + sc))
        dx_ref[...] = _rms_bwd(dh * nw * (1.0 + sc), n, r) + dx2_ref[...]

    row = lambda i: (i, 0)
    return pl.pallas_call(
        body, name="in_bwd", grid=(L // T,),
        in_specs=[pl.BlockSpec((T, P_PAD), row), _full((D, P_PAD)), pl.BlockSpec((T, D), row),
                  pl.BlockSpec((T, D), row), _full((SUB, D)), _full((SUB, D))],
        out_specs=[pl.BlockSpec((T, D), row), _full((SUB, D))],
        out_shape=[jax.ShapeDtypeStruct((L, D), F32), jax.ShapeDtypeStruct((SUB, D), F32)],
        compiler_params=_params(1),
    )(dp, w_in, x, dx2, modrows, vec)


def _wgrad(a, b, tm, tn, name):
    L, m = a.shape
    n = b.shape[1]
    tl = _tile(L, 512)
    tm, tn = _tile(m, tm), _tile(n, tn)
    nl = L // tl

    def body(a_ref, b_ref, o_ref):
        @pl.when(pl.program_id(2) == 0)
        def _():
            o_ref[...] = jnp.zeros_like(o_ref)

        o_ref[...] += _dot(a_ref[...], b_ref[...], TN)

    return pl.pallas_call(
        body, name=name, grid=(m // tm, n // tn, nl),
        in_specs=[pl.BlockSpec((tl, tm), lambda i, j, l: (l, i)), pl.BlockSpec((tl, tn), lambda i, j, l: (l, j))],
        out_specs=pl.BlockSpec((tm, tn), lambda i, j, l: (i, j)),
        out_shape=jax.ShapeDtypeStruct((m, n), F32), compiler_params=_params(3),
    )(a, b)


def _adamw(w, g, m, v, name):
    r, n = w.shape
    tr = _tile(r, 512)
    bc1 = 1.0 - ADAM_B1 ** ADAM_STEP
    bc2 = 1.0 - ADAM_B2 ** ADAM_STEP

    def body(w_ref, g_ref, m_ref, v_ref, d_ref, nm_ref, nv_ref):
        gv = g_ref[...]
        nm = ADAM_B1 * m_ref[...] + (1.0 - ADAM_B1) * gv
        nv = ADAM_B2 * v_ref[...] + (1.0 - ADAM_B2) * (gv * gv)
        nm_ref[...] = nm
        nv_ref[...] = nv
        d_ref[...] = -ADAM_LR * ((nm / bc1) / (jnp.sqrt(nv / bc2) + ADAM_EPS) + ADAM_WD * w_ref[...])

    spec = pl.BlockSpec((tr, n), lambda i: (i, 0))
    return pl.pallas_call(
        body, name=name, grid=(r // tr,), in_specs=[spec] * 4, out_specs=[spec] * 3,
        out_shape=[jax.ShapeDtypeStruct((r, n), F32)] * 3, compiler_params=_params(1),
    )(w, g, m, v)


def _rows8(rows, width):
    out = jnp.zeros((SUB, width), F32)
    for r, vrow in enumerate(rows):
        out = out.at[r, :vrow.shape[0]].set(vrow)
    return out


def _at_lanes(v4, start):
    return jnp.zeros((LANES,), F32).at[start:start + v4.shape[0]].set(v4)


def _pad_rows(flat, mult):
    n = flat.shape[0]
    pad = (-n) % mult
    return jnp.pad(flat, (0, pad)) if pad else flat


def _local_fwd_bwd(x, target, mod_full, small_w, full_w, prec):
    norm1_w, norm2_w, norm_a_w, a_log, dt_bias, norm_dn_w, norm_f_w = small_w
    w_in_f, w_out_f, w_up_f, w_down_f, conv_a_f, conv_q_f, conv_f_f = full_w

    def layer_params(i):
        modrows = jnp.concatenate([mod_full[i], jnp.zeros((SUB - N_MOD, D), F32)], axis=0)
        vec = _rows8([norm1_w[i], norm2_w[i]], D)
        pa = _rows8([conv_a_f[i, 0], conv_a_f[i, 1], conv_a_f[i, 2], norm_a_w[i]], AW)
        cq = _rows8([conv_q_f[i, k] for k in range(4)], 3 * H * HD)
        sp = _rows8([_at_lanes(a_log[i], H), _at_lanes(dt_bias[i], H), norm_dn_w[i]], LANES)
        cff = _rows8([conv_f_f[i, k] for k in range(3)], 2 * DFF)
        return modrows, vec, pa, cq, sp, cff

    saved = []
    xi = x
    for i in range(DEPTH):
        modrows, vec, pa, cq, sp, cff = layer_params(i)
        p, h1 = _in_proj(xi, modrows, vec, w_in_f[i])
        qn, kn, vs, gb, ya = _pre_fwd(p, pa, cq, sp)
        o, states = _gdr_fwd(qn, kn, vs, gb, prec)
        y, x2, yb = _post_fwd(o, p, ya, xi, modrows, sp, w_out_f[i])
        h2, gpre, upre, f, dff, x3 = _ffn_fwd(x2, modrows, vec, w_up_f[i], cff, w_down_f[i])
        saved.append(dict(x=xi, p=p, h1=h1, qn=qn, kn=kn, vs=vs, gb=gb, ya=ya, o=o, states=states, y=y, x2=x2, yb=yb,
                          h2=h2, gpre=gpre, upre=upre, f=f, d=dff))
        xi = x3

    dx, facc = _final(xi, target, _rows8([norm_f_w], D))
    loss_local = jnp.sum(facc[0])
    d_norm_f = facc[1]

    gw_in, gw_out, gw_up, gw_down = [None] * DEPTH, [None] * DEPTH, [None] * DEPTH, [None] * DEPTH
    g_small = [None] * DEPTH
    for i in reversed(range(DEPTH)):
        s = saved[i]
        modrows, vec, pa, cq, sp, cff = layer_params(i)
        dd, dgp, dup, dx2, accf, dcg, dcu = _ffn_bwd(dx, s["d"], s["x2"], modrows, vec, s["gpre"], s["upre"], cff,
                                                       w_down_f[i], w_up_f[i])
        gw_up[i] = jnp.concatenate([_wgrad(s["h2"], dgp, 1024, DFF // 2, "wgrad_up"),
                                    _wgrad(s["h2"], dup, 1024, DFF // 2, "wgrad_up")], axis=1)
        gw_down[i] = _wgrad(s["f"], dd, DFF // 2, 1024, "wgrad_down")
        dy, do, dz, dya, accp, accs = _post_bwd(dx2, s["y"], s["o"], s["p"], modrows, sp, w_out_f[i])
        gw_out[i] = jnp.concatenate([_wgrad(s["ya"], dy, 512, 1024, "wgrad_out"),
                                     _wgrad(s["yb"], dy, 512, 1024, "wgrad_out")], axis=0)
        dqn, dkn, dvs, dgb = _gdr_bwd(s["qn"], s["kn"], s["vs"], s["gb"], s["states"], do, prec)
        dp, dpa, dcq, dsp = _pre_bwd(s["p"], dqn, dkn, dvs, dya, dz, dgb, pa, cq, sp)
        gw_in[i] = _wgrad(s["h1"], dp, 512, P_PAD, "wgrad_in")[:, :P_IN]
        dx, acci = _in_bwd(dp, w_in_f[i], s["x"], dx2, modrows, vec)
        dconv_ff = jnp.concatenate([dcg.transpose(1, 0, 2).reshape(SUB, DFF), dcu.transpose(1, 0, 2).reshape(SUB, DFF)],
                                   axis=1)[0:3]
        dmod = jnp.stack([acci[0], acci[1], accp[0], accf[1], accf[2], accf[0]])
        g_small[i] = dict(norm1=acci[2], norm2=accf[3], norm_a=dpa[3], a_log=dsp[0, H:2 * H], dt_bias=dsp[1, H:2 * H],
                          norm_dn=accs[0], conv_a=dpa[0:3], conv_qkv=dcq[0:4], conv_ff=dconv_ff, dmod=dmod.reshape(-1))
    return loss_local, dx, gw_in, gw_out, gw_up, gw_down, g_small, d_norm_f


def kernel(x, c, ada_w, ada_b, norm1_w, w_in, conv_a_w, norm_a_w, conv_qkv_w, a_log, dt_bias, norm_dn_w, w_out, norm2_w, w_up, conv_ff_w, w_down, norm_f_w, loss_target, m_ada_w, m_ada_b, m_norm1_w, m_w_in, m_conv_a_w, m_norm_a_w, m_conv_qkv_w, m_a_log, m_dt_bias, m_norm_dn_w, m_w_out, m_norm2_w, m_w_up, m_conv_ff_w, m_w_down, m_norm_f_w, v_ada_w, v_ada_b, v_norm1_w, v_w_in, v_conv_a_w, v_norm_a_w, v_conv_qkv_w, v_a_log, v_dt_bias, v_norm_dn_w, v_w_out, v_norm2_w, v_w_up, v_conv_ff_w, v_w_down, v_norm_f_w):
    ax, ay, ac = lax.axis_index("x"), lax.axis_index("y"), lax.axis_index("c")
    me = 4 * ax + 2 * ay + ac
    x = x[0]
    target = loss_target[0]
    L = x.shape[0]
    prec = HI

    big = [w_in, w_out, w_up, w_down]
    big_sizes = [int(np_.size) for np_ in big]
    blob = jnp.concatenate([t.reshape(-1) for t in big]).astype(MXU).reshape(-1, LANES)
    rows_big = blob.shape[0]
    gathered = _all_gather(blob, "gather_weights", in_vmem=False).reshape(N_DEV, rows_big * LANES)
    offs = [0]
    for s in big_sizes:
        offs.append(offs[-1] + s)

    def part(k):
        return gathered[:, offs[k]:offs[k + 1]]

    n_in, n_up = P_IN // N_DEV, 2 * DFF // N_DEV
    w_in_f = part(0).reshape(N_DEV, DEPTH, D, n_in).transpose(1, 2, 0, 3).reshape(DEPTH, D, P_IN)
    w_in_f = jnp.pad(w_in_f, ((0, 0), (0, 0), (0, P_PAD - P_IN)))
    w_out_f = part(1).reshape(N_DEV, DEPTH, D // N_DEV, D).transpose(1, 0, 2, 3).reshape(DEPTH, D, D)
    w_up_f = part(2).reshape(N_DEV, DEPTH, D, n_up).transpose(1, 2, 0, 3).reshape(DEPTH, D, 2 * DFF)
    w_down_f = part(3).reshape(N_DEV, DEPTH, DFF // N_DEV, D).transpose(1, 0, 2, 3).reshape(DEPTH, DFF, D)

    small = [conv_a_w, conv_qkv_w, conv_ff_w]
    sblob = _pad_rows(jnp.concatenate([t.reshape(-1) for t in small]), SUB * LANES).reshape(-1, LANES)
    sg = _all_gather(sblob, "gather_conv", in_vmem=True).reshape(N_DEV, -1)
    o1 = conv_a_w.size
    o2 = o1 + conv_qkv_w.size
    o3 = o2 + conv_ff_w.size
    conv_a_f = sg[:, 0:o1].reshape(N_DEV, DEPTH, 3, AW // N_DEV).transpose(1, 2, 0, 3).reshape(DEPTH, 3, AW)
    conv_q_f = sg[:, o1:o2].reshape(N_DEV, DEPTH, 4, 3 * H * HD // N_DEV).transpose(1, 2, 0, 3).reshape(DEPTH, 4, 3 * H * HD)
    conv_f_f = sg[:, o2:o3].reshape(N_DEV, DEPTH, 3, n_up).transpose(1, 2, 0, 3).reshape(DEPTH, 3, 2 * DFF)

    c_rows = jnp.zeros((SUB, D), F32).at[0].set(c[0])
    c_all = _all_gather(c_rows, "gather_c", in_vmem=True).reshape(N_DEV, SUB, D)[:, 0]
    c_all = jnp.concatenate([c_all, jnp.zeros((16 - N_DEV, D), F32)], axis=0)
    n_ada = N_MOD * D // N_DEV
    ada_b_cols = lax.dynamic_slice_in_dim(ada_b, me * n_ada, n_ada, axis=1)[:, None, :]
    mod_sh = _mod_fwd(c_all, ada_w, ada_b_cols)
    mod_all = _all_gather(mod_sh.reshape(DEPTH * 16, n_ada), "gather_mod", in_vmem=True)
    mod_all = mod_all.reshape(N_DEV, DEPTH, 16, n_ada)
    mod_mine = lax.dynamic_index_in_dim(mod_all, me, axis=2, keepdims=False)
    mod_full = mod_mine.transpose(1, 0, 2).reshape(DEPTH, N_MOD, D)

    loss_local, dx, gw_in, gw_out, gw_up, gw_down, g_small, d_norm_f = _local_fwd_bwd(
        x, target, mod_full, (norm1_w, norm2_w, norm_a_w, a_log, dt_bias, norm_dn_w, norm_f_w),
        (w_in_f, w_out_f, w_up_f, w_down_f, conv_a_f, conv_q_f, conv_f_f), prec)
    loss = lax.psum(loss_local, ("x", "y", "c"))
    grad_x = dx[None]

    keys = ["norm1", "norm2", "norm_a", "a_log", "dt_bias", "norm_dn", "conv_a", "conv_qkv", "conv_ff", "dmod"]
    stacked = {k: jnp.stack([g_small[i][k] for i in range(DEPTH)]) for k in keys}
    flat_parts = [stacked[k].reshape(-1) for k in keys] + [d_norm_f]
    sizes = [int(t.shape[0]) for t in flat_parts]
    sflat = _pad_rows(jnp.concatenate(flat_parts), SUB * LANES).reshape(-1, LANES)
    srows = sflat.shape[0]
    sall = _all_gather(sflat, "gather_small_grads", in_vmem=True).reshape(N_DEV, srows, LANES)
    ssum = _sum_devices(sall).reshape(-1)
    so = [0]
    for sz in sizes:
        so.append(so[-1] + sz)
    red = {k: ssum[so[n]:so[n + 1]].reshape(stacked[k].shape) for n, k in enumerate(keys)}
    g_norm_f = ssum[so[len(keys)]:so[len(keys) + 1]]
    dmod_all = sall.reshape(N_DEV, -1)[:, so[keys.index("dmod")]:so[keys.index("dmod") + 1]].reshape(N_DEV, DEPTH, N_MOD * D)

    g_ada_b = red["dmod"].reshape(DEPTH, N_MOD * D)
    dmod_cols = lax.dynamic_slice_in_dim(dmod_all, me * n_ada, n_ada, axis=2).transpose(1, 0, 2)
    dmod_cols = jnp.concatenate([dmod_cols, jnp.zeros((DEPTH, 16 - N_DEV, n_ada), F32)], axis=1)
    g_ada_w = _mod_bwd(c_all, dmod_cols)
    g_conv_a = lax.dynamic_slice_in_dim(red["conv_a"], me * (AW // N_DEV), AW // N_DEV, axis=2)
    g_conv_qkv = lax.dynamic_slice_in_dim(red["conv_qkv"], me * (3 * H * HD // N_DEV), 3 * H * HD // N_DEV, axis=2)
    g_conv_ff = lax.dynamic_slice_in_dim(red["conv_ff"], me * n_up, n_up, axis=2)

    def by_dest_cols(gs, ncol):
        t = jnp.stack(gs)
        return t.reshape(DEPTH, t.shape[1], N_DEV, ncol).transpose(2, 0, 1, 3).reshape(N_DEV, -1)

    def by_dest_rows(gs, nrow):
        t = jnp.stack(gs)
        return t.reshape(DEPTH, N_DEV, nrow, t.shape[2]).transpose(1, 0, 2, 3).reshape(N_DEV, -1)

    gblob = jnp.concatenate([by_dest_cols(gw_in, n_in), by_dest_rows(gw_out, D // N_DEV),
                             by_dest_cols(gw_up, n_up), by_dest_rows(gw_down, DFF // N_DEV)], axis=1)
    gblob = gblob.reshape(N_DEV, rows_big, LANES)
    recv1 = _rs_sibling(gblob)
    pf, pb = _rs_add_pairs(gblob, recv1, jnp.reshape(ac, (1,)).astype(jnp.int32))
    recv2 = _rs_chips(pb)
    gmine = _rs_add_chips(pf, recv2, jnp.reshape(2 * ax + ay, (1,)).astype(jnp.int32)).reshape(-1)
    g_w_in = gmine[offs[0]:offs[1]].reshape(w_in.shape)
    g_w_out = gmine[offs[1]:offs[2]].reshape(w_out.shape)
    g_w_up = gmine[offs[2]:offs[3]].reshape(w_up.shape)
    g_w_down = gmine[offs[3]:offs[4]].reshape(w_down.shape)

    grads = dict(ada_w=g_ada_w, ada_b=g_ada_b, norm1_w=red["norm1"], w_in=g_w_in, conv_a_w=g_conv_a,
                 norm_a_w=red["norm_a"], conv_qkv_w=g_conv_qkv, a_log=red["a_log"], dt_bias=red["dt_bias"],
                 norm_dn_w=red["norm_dn"], w_out=g_w_out, norm2_w=red["norm2"], w_up=g_w_up, conv_ff_w=g_conv_ff,
                 w_down=g_w_down, norm_f_w=g_norm_f)
    weights = dict(ada_w=ada_w, ada_b=ada_b, norm1_w=norm1_w, w_in=w_in, conv_a_w=conv_a_w, norm_a_w=norm_a_w,
                   conv_qkv_w=conv_qkv_w, a_log=a_log, dt_bias=dt_bias, norm_dn_w=norm_dn_w, w_out=w_out,
                   norm2_w=norm2_w, w_up=w_up, conv_ff_w=conv_ff_w, w_down=w_down, norm_f_w=norm_f_w)
    ms = dict(ada_w=m_ada_w, ada_b=m_ada_b, norm1_w=m_norm1_w, w_in=m_w_in, conv_a_w=m_conv_a_w, norm_a_w=m_norm_a_w,
              conv_qkv_w=m_conv_qkv_w, a_log=m_a_log, dt_bias=m_dt_bias, norm_dn_w=m_norm_dn_w, w_out=m_w_out,
              norm2_w=m_norm2_w, w_up=m_w_up, conv_ff_w=m_conv_ff_w, w_down=m_w_down, norm_f_w=m_norm_f_w)
    vs_ = dict(ada_w=v_ada_w, ada_b=v_ada_b, norm1_w=v_norm1_w, w_in=v_w_in, conv_a_w=v_conv_a_w, norm_a_w=v_norm_a_w,
               conv_qkv_w=v_conv_qkv_w, a_log=v_a_log, dt_bias=v_dt_bias, norm_dn_w=v_norm_dn_w, w_out=v_w_out,
               norm2_w=v_norm2_w, w_up=v_w_up, conv_ff_w=v_conv_ff_w, w_down=v_w_down, norm_f_w=v_norm_f_w)
    names = list(weights)
    big_names = ["ada_w", "w_in", "w_out", "w_up", "w_down"]
    delta, new_m, new_v = {}, {}, {}
    for n in big_names:
        shp = weights[n].shape
        two = lambda t: t.reshape(-1, shp[-1])
        dl, nm, nv = _adamw(two(weights[n]), two(grads[n]), two(ms[n]), two(vs_[n]), "adamw_" + n)
        delta[n], new_m[n], new_v[n] = dl.reshape(shp), nm.reshape(shp), nv.reshape(shp)
    small_names = [n for n in names if n not in big_names]

    def pack(dct):
        return _pad_rows(jnp.concatenate([dct[n].reshape(-1) for n in small_names]), SUB * LANES).reshape(-1, LANES)

    dl, nm, nv = _adamw(pack(weights), pack(grads), pack(ms), pack(vs_), "adamw_small")
    off = 0
    for n in small_names:
        sz, shp = weights[n].size, weights[n].shape
        delta[n] = dl.reshape(-1)[off:off + sz].reshape(shp)
        new_m[n] = nm.reshape(-1)[off:off + sz].reshape(shp)
        new_v[n] = nv.reshape(-1)[off:off + sz].reshape(shp)
        off += sz

    return (loss, grad_x, *[grads[n] for n in names], *[delta[n] for n in names],
            *[new_m[n] for n in names], *[new_v[n] for n in names])
```

```python
import functools
import math

import jax
import jax.numpy as jnp
from jax import lax
from jax.experimental import pallas as pl
from jax.experimental.pallas import tpu as pltpu

F32 = jnp.float32
MXU = jnp.bfloat16

D = 1024
DEPTH = 4
N_MOD = 6
AW = 512
A_GROUP = 64
H = 4
HD = 128
CK = 64
DFF = 2816
P_IN = 3592
P_PAD = 3712
EPS = 1e-6
N_DEV = 8
LANES = 128
SUB = 8
VMEM_LIMIT = 56 * 1024 * 1024

ADAM_LR, ADAM_B1, ADAM_B2, ADAM_EPS, ADAM_WD, ADAM_STEP = 0.001, 0.9, 0.999, 1e-08, 0.01, 10

NN = ((1,), (0,))
NT = ((1,), (1,))
TN = ((0,), (0,))
HI = lax.Precision.HIGHEST
MESH = pl.DeviceIdType.MESH


def _dot(a, b, dims, prec=None):
    if prec is None:
        a = a.astype(MXU) if a.dtype == F32 else a
        b = b.astype(MXU) if b.dtype == F32 else b
    return lax.dot_general(a, b, (dims, ((), ())), precision=prec, preferred_element_type=F32)


def _params(n_grid=0, limit=VMEM_LIMIT):
    sem = ("arbitrary",) * n_grid if n_grid else None
    return pltpu.CompilerParams(dimension_semantics=sem, vmem_limit_bytes=limit)


def _tile(n, want):
    if n <= want:
        return n
    t = want - want % SUB
    while n % t:
        t -= SUB
    assert t > 0, (n, want)
    return t


def _full(shape):
    nd = len(shape)
    return pl.BlockSpec(shape, lambda *_: (0,) * nd)


def _sig(x):
    return jax.nn.sigmoid(x)


def _rms(x):
    r = lax.rsqrt(jnp.mean(x * x, axis=-1, keepdims=True) + EPS)
    return x * r, r


def _rms_bwd(dn, n, r):
    return r * (dn - n * jnp.mean(dn * n, axis=-1, keepdims=True))


def _l2_bwd(dn, n, r):
    return r * (dn - n * jnp.sum(dn * n, axis=-1, keepdims=True))


def _sum0(x):
    return jnp.sum(x, axis=0, keepdims=True)


def _shift_down(x, s, halo):
    r = pltpu.roll(x, s, 0)
    row = lax.broadcasted_iota(jnp.int32, x.shape, 0)
    for k in range(s):
        r = jnp.where(row == k, halo[SUB - s + k:SUB - s + k + 1, :], r)
    return r


def _shift_up(x, s, halo):
    t = x.shape[0]
    r = pltpu.roll(x, t - s, 0)
    row = lax.broadcasted_iota(jnp.int32, x.shape, 0)
    for k in range(s):
        r = jnp.where(row == t - s + k, halo[k:k + 1, :], r)
    return r


def _conv_fwd(x, w_ref, width, halo):
    sh = [x] + [_shift_down(x, s, halo) for s in range(1, width)]
    out = w_ref[width - 1:width, :] * sh[0]
    for s in range(1, width):
        out = out + w_ref[width - 1 - s:width - s, :] * sh[s]
    return out, sh


def _conv_bwd_in(dout, w_ref, width, halo_next):
    dx = w_ref[width - 1:width, :] * dout
    for s in range(1, width):
        dx = dx + w_ref[width - 1 - s:width - s, :] * _shift_up(dout, s, halo_next)
    return dx


def _blockdiag_mean(n, group):
    r = lax.shift_right_logical(lax.broadcasted_iota(jnp.int32, (n, n), 0), int(math.log2(group)))
    c = lax.shift_right_logical(lax.broadcasted_iota(jnp.int32, (n, n), 1), int(math.log2(group)))
    return jnp.where(r == c, 1.0 / group, 0.0).astype(F32)


def _softplus(x):
    return jnp.maximum(x, 0.0) + jnp.log(1.0 + jnp.exp(-jnp.abs(x)))


def _my_place():
    return lax.axis_index("x"), lax.axis_index("y"), lax.axis_index("c")


def _all_gather(shards, name, in_vmem):
    nt = len(shards)

    def body(*refs):
        x_refs, out_refs = refs[:nt], refs[nt:2 * nt]
        send_sems, recv_sems, local_sems = refs[2 * nt:]
        x, y, c = _my_place()
        me, sibling = (x, y, c), (x, y, 1 - c)
        chips = [(1 - x, y), (x, 1 - y), (1 - x, 1 - y)]
        everything = []
        for t in range(nt):
            x_ref, out_ref = x_refs[t], out_refs[t]

            def blk(px, py, pc, out_ref=out_ref):
                return out_ref.at[4 * px + 2 * py + pc]

            def copy(k, block, to, src=None, t=t, blk=blk):
                return pltpu.make_async_remote_copy(
                    src_ref=blk(*block) if src is None else src, dst_ref=blk(*block),
                    send_sem=send_sems.at[7 * t + k], recv_sem=recv_sems.at[7 * t + k], device_id=to, device_id_type=MESH)

            mine = pltpu.make_async_copy(x_ref, blk(*me), local_sems.at[t])
            mine.start()
            first = [copy(0, me, sibling, src=x_ref)]
            first += [copy(1 + j, me, (*chip, c), src=x_ref) for j, chip in enumerate(chips)]
            for cp in first:
                cp.start()
            everything.append((copy, mine, first))
        sends = []
        for copy, mine, first in everything:
            passed = [copy(4 + j, (*chip, c), sibling) for j, chip in enumerate(chips)]
            for j, chip in enumerate(chips):
                copy(1 + j, (*chip, c), me).wait_recv()
                passed[j].start()
            sends += first + passed
        for copy, mine, first in everything:
            copy(0, sibling, me).wait_recv()
            for j, chip in enumerate(chips):
                copy(4 + j, (*chip, 1 - c), me).wait_recv()
        for cp in sends:
            cp.wait_send()
        for copy, mine, first in everything:
            mine.wait()

    space = pltpu.VMEM if in_vmem else pl.ANY
    return pl.pallas_call(
        body, name=name,
        out_shape=[jax.ShapeDtypeStruct((N_DEV,) + s.shape, s.dtype) for s in shards],
        in_specs=[pl.BlockSpec(memory_space=space)] * nt,
        out_specs=[pl.BlockSpec(memory_space=space)] * nt,
        scratch_shapes=[pltpu.SemaphoreType.DMA((7 * nt,)), pltpu.SemaphoreType.DMA((7 * nt,)),
                        pltpu.SemaphoreType.DMA((nt,))],
        compiler_params=pltpu.CompilerParams(vmem_limit_bytes=VMEM_LIMIT),
    )(*shards)


def _rs_sibling(gs):
    nt = len(gs)

    def body(*refs):
        g_refs, recv_refs = refs[:nt], refs[nt:2 * nt]
        send_sems, recv_sems = refs[2 * nt:]
        x, y, c = _my_place()
        copies = [pltpu.make_async_remote_copy(
            src_ref=g_refs[t].at[2 * j + (1 - c)], dst_ref=recv_refs[t].at[j],
            send_sem=send_sems.at[4 * t + j], recv_sem=recv_sems.at[4 * t + j],
            device_id=(x, y, 1 - c), device_id_type=MESH) for t in range(nt) for j in range(4)]
        for cp in copies:
            cp.start()
        for cp in copies:
            cp.wait()

    return pl.pallas_call(
        body, name="rs_sibling",
        out_shape=[jax.ShapeDtypeStruct((4,) + g.shape[1:], g.dtype) for g in gs],
        in_specs=[pl.BlockSpec(memory_space=pl.ANY)] * nt, out_specs=[pl.BlockSpec(memory_space=pl.ANY)] * nt,
        scratch_shapes=[pltpu.SemaphoreType.DMA((4 * nt,)), pltpu.SemaphoreType.DMA((4 * nt,))],
    )(*gs)


def _rs_chips(pbs):
    nt = len(pbs)

    def body(*refs):
        p_refs, recv_refs = refs[:nt], refs[nt:2 * nt]
        send_sems, recv_sems = refs[2 * nt:]
        x, y, c = _my_place()
        chips = [(1 - x, y), (x, 1 - y), (1 - x, 1 - y)]
        copies = [pltpu.make_async_remote_copy(
            src_ref=p_refs[t].at[2 * px + py], dst_ref=recv_refs[t].at[s],
            send_sem=send_sems.at[3 * t + s], recv_sem=recv_sems.at[3 * t + s],
            device_id=(px, py, c), device_id_type=MESH) for t in range(nt) for s, (px, py) in enumerate(chips)]
        for cp in copies:
            cp.start()
        for cp in copies:
            cp.wait()

    return pl.pallas_call(
        body, name="rs_chips",
        out_shape=[jax.ShapeDtypeStruct((3,) + p.shape[1:], p.dtype) for p in pbs],
        in_specs=[pl.BlockSpec(memory_space=pl.ANY)] * nt, out_specs=[pl.BlockSpec(memory_space=pl.ANY)] * nt,
        scratch_shapes=[pltpu.SemaphoreType.DMA((3 * nt,)), pltpu.SemaphoreType.DMA((3 * nt,))],
    )(*pbs)


def _rs_add_pairs(g, recv, my_c, name):
    _, nl, r, n = g.shape
    tr = _tile(r, 512)

    def body(c_ref, g_ref, r_ref, pf_ref, pb_ref):
        s = g_ref[...] + r_ref[...]
        pf_ref[...] = s
        pb_ref[...] = s.astype(MXU)

    spec_j = pl.BlockSpec((None, None, tr, n), lambda j, l, i, c_ref: (j, l, i, 0))
    return pl.pallas_call(
        body, name=name,
        grid_spec=pltpu.PrefetchScalarGridSpec(
            num_scalar_prefetch=1, grid=(4, nl, r // tr),
            in_specs=[pl.BlockSpec((None, None, tr, n), lambda j, l, i, c_ref: (2 * j + c_ref[0], l, i, 0)), spec_j],
            out_specs=[spec_j, spec_j]),
        out_shape=[jax.ShapeDtypeStruct((4, nl, r, n), F32), jax.ShapeDtypeStruct((4, nl, r, n), MXU)],
        compiler_params=_params(3),
    )(my_c, g, recv)


def _rs_add_chips(pf, recv, my_chip, name):
    _, nl, r, n = pf.shape
    tr = _tile(r, 512)

    def body(j_ref, p_ref, r_ref, o_ref):
        s = p_ref[...]
        for t in range(3):
            s = s + r_ref[t].astype(F32)
        o_ref[...] = s

    return pl.pallas_call(
        body, name=name,
        grid_spec=pltpu.PrefetchScalarGridSpec(
            num_scalar_prefetch=1, grid=(nl, r // tr),
            in_specs=[pl.BlockSpec((None, None, tr, n), lambda l, i, j_ref: (j_ref[0], l, i, 0)),
                      pl.BlockSpec((3, None, tr, n), lambda l, i, j_ref: (0, l, i, 0))],
            out_specs=pl.BlockSpec((None, tr, n), lambda l, i, j_ref: (l, i, 0))),
        out_shape=jax.ShapeDtypeStruct((nl, r, n), F32),
        compiler_params=_params(2),
    )(my_chip, pf, recv)


def _shard_windows(n_shard, count, first=0):
    out = []
    for k in range(first, first + count):
        off = n_shard * k
        a, s = off // LANES, off % LANES
        out.append((a, s, -(-(s + n_shard) // LANES) * LANES))
    return out


def _fit_lanes(x, width):
    have = x.shape[1]
    if have < width:
        return jnp.concatenate([x, jnp.zeros((x.shape[0], width - have), x.dtype)], axis=-1)
    return x[:, :width]


def _interleave_cols(g, n_shard, w_out, name):
    nd, nl, rows, wpad = g.shape
    rb = _tile(rows, 256)
    wins = _shard_windows(n_shard, nd)

    def body(g_ref, o_ref, acc):
        acc[...] = jnp.zeros_like(acc)
        for k, (a, s, win) in enumerate(wins):
            xk = _fit_lanes(g_ref[k].astype(F32), win)
            if s:
                xk = pltpu.roll(xk, s, 1)
            acc[:, a * LANES:a * LANES + win] += xk
        o_ref[...] = acc[...].astype(o_ref.dtype)

    return pl.pallas_call(
        body, name=name, grid=(nl, rows // rb),
        in_specs=[pl.BlockSpec((nd, None, rb, wpad), lambda l, i: (0, l, i, 0))],
        out_specs=pl.BlockSpec((None, rb, w_out), lambda l, i: (l, i, 0)),
        out_shape=jax.ShapeDtypeStruct((nl, rows, w_out), g.dtype),
        scratch_shapes=[pltpu.VMEM((rb, w_out), F32)],
        compiler_params=_params(2),
    )(g)


def _sum_devices(g):
    _, r, n = g.shape

    def body(g_ref, o_ref):
        s = g_ref[0]
        for t in range(1, N_DEV):
            s = s + g_ref[t]
        o_ref[...] = s

    return pl.pallas_call(
        body, name="sum_devices", out_shape=jax.ShapeDtypeStruct((r, n), F32),
        in_specs=[pl.BlockSpec(memory_space=pltpu.VMEM)], out_specs=pl.BlockSpec(memory_space=pltpu.VMEM),
        compiler_params=pltpu.CompilerParams(vmem_limit_bytes=VMEM_LIMIT),
    )(g)


def _mod_fwd(c_all, ada_w, ada_b_cols):
    nl, _, nc = ada_w.shape

    def body(c_ref, w_ref, b_ref, o_ref):
        cv = c_ref[...]
        act = (cv * _sig(cv)).astype(MXU)
        o_ref[...] = _dot(act, w_ref[...].astype(MXU), NN) + b_ref[...]

    return pl.pallas_call(
        body, name="mod_fwd", grid=(nl,),
        in_specs=[_full((16, D)), pl.BlockSpec((None, D, nc), lambda i: (i, 0, 0)),
                  pl.BlockSpec((None, 1, nc), lambda i: (i, 0, 0))],
        out_specs=pl.BlockSpec((None, 16, nc), lambda i: (i, 0, 0)),
        out_shape=jax.ShapeDtypeStruct((nl, 16, nc), F32), compiler_params=_params(1),
    )(c_all, ada_w, ada_b_cols)


def _mod_bwd(c_all, dmod_cols):
    nl, _, nc = dmod_cols.shape

    def body(c_ref, d_ref, o_ref):
        cv = c_ref[...]
        act = (cv * _sig(cv)).astype(MXU)
        o_ref[...] = _dot(act, d_ref[...].astype(MXU), TN)

    return pl.pallas_call(
        body, name="mod_bwd", grid=(nl,),
        in_specs=[_full((16, D)), pl.BlockSpec((None, 16, nc), lambda i: (i, 0, 0))],
        out_specs=pl.BlockSpec((None, D, nc), lambda i: (i, 0, 0)),
        out_shape=jax.ShapeDtypeStruct((nl, D, nc), F32), compiler_params=_params(1),
    )(c_all, dmod_cols)


def _in_proj(x, modrows, vec, w_in):
    L = x.shape[0]
    T = _tile(L, 256)

    def body(x_ref, mod_ref, vec_ref, w_ref, p_ref, h_ref):
        n, _ = _rms(x_ref[...])
        h = n * vec_ref[0:1, :] * (1.0 + mod_ref[1:2, :]) + mod_ref[0:1, :]
        hb = h.astype(MXU)
        h_ref[...] = hb
        p_ref[...] = _dot(hb, w_ref[...], NN)

    return pl.pallas_call(
        body, name="in_proj", grid=(L // T,),
        in_specs=[pl.BlockSpec((T, D), lambda i: (i, 0)), _full((SUB, D)), _full((SUB, D)), _full((D, P_PAD))],
        out_specs=[pl.BlockSpec((T, P_PAD), lambda i: (i, 0)), pl.BlockSpec((T, D), lambda i: (i, 0))],
        out_shape=[jax.ShapeDtypeStruct((L, P_PAD), F32), jax.ShapeDtypeStruct((L, D), MXU)],
        compiler_params=_params(1),
    )(x, modrows, vec, w_in)


def _gate_small(s, sp_ref):
    lane = lax.broadcasted_iota(jnp.int32, s.shape, 1)
    a = -jnp.exp(sp_ref[0:1, :])
    xb = s + sp_ref[1:2, :]
    beta = _sig(s)
    g = a * _softplus(xb)
    return lane, a, xb, beta, g


def _pre_fwd(p, pa, cq, sp):
    L = p.shape[0]
    T = _tile(L, 256)
    scale = HD ** -0.5

    def body(pm_ref, ps_ref, pa_ref, cq_ref, sp_ref, qn_ref, kn_ref, vs_ref, gb_ref, ya_ref, u_carry, q_carry):
        @pl.when(pl.program_id(0) == 0)
        def _():
            u_carry[...] = jnp.zeros_like(u_carry)
            q_carry[...] = jnp.zeros_like(q_carry)

        a_b = pm_ref[:, 0:AW]
        u = pm_ref[:, AW:2 * AW] * pm_ref[:, 2 * AW:3 * AW]
        cu, _ = _conv_fwd(u, pa_ref, 3, u_carry[...])
        u_carry[...] = u[T - SUB:T, :]
        yp = a_b * cu
        ms = _dot(yp * yp, _blockdiag_mean(AW, A_GROUP), NN, HI)
        ya_ref[...] = (yp * lax.rsqrt(ms + EPS) * pa_ref[3:4, :]).astype(MXU)

        qkv = pm_ref[:, 3 * AW:3 * AW + 3 * H * HD]
        qc, _ = _conv_fwd(qkv, cq_ref, 4, q_carry[...])
        q_carry[...] = qkv[T - SUB:T, :]
        qs = qc * _sig(qc)
        for h in range(H):
            q = qs[:, h * HD:(h + 1) * HD]
            qn_ref[:, h * HD:(h + 1) * HD] = q * (lax.rsqrt(jnp.sum(q * q, axis=-1, keepdims=True) + EPS) * scale)
            k = qs[:, (H + h) * HD:(H + h + 1) * HD]
            kn_ref[:, h * HD:(h + 1) * HD] = k * lax.rsqrt(jnp.sum(k * k, axis=-1, keepdims=True) + EPS)
        vs_ref[...] = qs[:, 2 * H * HD:3 * H * HD]

        lane, _, _, beta, g = _gate_small(ps_ref[...], sp_ref)
        gb_ref[...] = jnp.where(lane < H, beta, jnp.where(lane < 2 * H, g, 0.0))

    w3 = 3 * AW + 3 * H * HD
    row = lambda i: (i, 0)
    return pl.pallas_call(
        body, name="pre_fwd", grid=(L // T,),
        in_specs=[pl.BlockSpec((T, w3), row), pl.BlockSpec((T, LANES), lambda i: (i, (P_PAD - LANES) // LANES)),
                  _full((SUB, AW)), _full((SUB, 3 * H * HD)), _full((SUB, LANES))],
        out_specs=[pl.BlockSpec((T, H * HD), row)] * 3 + [pl.BlockSpec((T, LANES), row), pl.BlockSpec((T, AW), row)],
        out_shape=[jax.ShapeDtypeStruct((L, H * HD), F32)] * 3
        + [jax.ShapeDtypeStruct((L, LANES), F32), jax.ShapeDtypeStruct((L, AW), MXU)],
        scratch_shapes=[pltpu.VMEM((SUB, AW), F32), pltpu.VMEM((SUB, 3 * H * HD), F32)],
        compiler_params=_params(1),
    )(p, p, pa, cq, sp)


def _gdr_masks():
    r = lax.broadcasted_iota(jnp.int32, (CK, CK), 0)
    c = lax.broadcasted_iota(jnp.int32, (CK, CK), 1)
    return r >= c, r > c


def _head_cols(gbt, h):
    return gbt[:, h:h + 1], gbt[:, H + h:H + h + 1]


def _split(x, parts):
    out = []
    for _ in range(parts):
        hi = x.astype(jnp.bfloat16)
        out.append(hi)
        x = x - hi.astype(F32)
    return out


def _dot_f32(a, b, dims, exact=None):
    if exact == "a":
        ab = a.astype(jnp.bfloat16)
        return sum(_dot(ab, t, dims) for t in _split(b, 3))
    if exact == "b":
        bb = b.astype(jnp.bfloat16)
        return sum(_dot(t, bb, dims) for t in _split(a, 3))
    ah, al = _split(a, 2)
    bh, bl = _split(b, 2)
    return _dot(ah, bh, dims) + _dot(ah, bl, dims) + _dot(al, bh, dims)


def _gdr_consts():
    causal, strict = _gdr_masks()
    return dict(causal=causal, strict=strict, tril=jnp.where(causal, 1.0, 0.0).astype(F32),
                eye=jnp.where(causal & jnp.logical_not(strict), 1.0, 0.0).astype(F32),
                bcast=jnp.full((CK, HD), 1.0 / HD, F32))


def _gdr_chunk(q, k, v, beta, g, cst):
    causal, strict = cst["causal"], cst["strict"]
    gc = _dot_f32(cst["tril"], jnp.broadcast_to(g, (CK, HD)), NN, exact="a")
    g_row = _dot_f32(cst["bcast"], gc, NT, exact="a")
    decay = jnp.where(causal, jnp.exp(jnp.where(causal, gc[:, 0:CK] - g_row, 0.0)), 0.0)
    eg = jnp.exp(gc)
    gl = gc[CK - 1:CK, :]
    ek = jnp.exp(gl - gc)
    cd = jnp.exp(gl)
    kb = k * beta
    pk = _dot(kb, k, NT)
    lower = jnp.where(strict, pk * decay, 0.0)
    xp = -lower
    tinv = cst["eye"] + xp
    for _ in range(5):
        xp = _dot_f32(xp, xp, NN)
        tinv = tinv + _dot_f32(tinv, xp, NN)
    u = _dot(tinv, v * beta, NN)
    w = _dot(tinv, kb * eg, NN)
    qk = _dot(q, k, NT)
    intra = jnp.where(causal, qk * decay, 0.0)
    return dict(gc=gc, decay=decay, eg=eg, ek=ek, cd=cd, kb=kb, pk=pk, tinv=tinv, u=u, w=w, qk=qk, intra=intra,
                q_dec=q * eg, k_dec=k * ek)


def _gdr_fwd(qn, kn, vs, gb):
    L = qn.shape[0]
    nc = L // CK
    cb = min(8, nc)
    rb = cb * CK
    nb = nc // cb

    def body(q_ref, k_ref, v_ref, gb_ref, o_ref, st_ref, s_ref):
        @pl.when(pl.program_id(0) == 0)
        def _():
            s_ref[...] = jnp.zeros_like(s_ref)

        cst = _gdr_consts()

        def chunk(ci, carry):
            rows = pl.ds(pl.multiple_of(ci * CK, CK), CK)
            gbt = gb_ref[rows, :]
            for h in range(H):
                cols = slice(h * HD, (h + 1) * HD)
                beta, g = _head_cols(gbt, h)
                t = _gdr_chunk(q_ref[rows, cols], k_ref[rows, cols], v_ref[rows, cols], beta, g, cst)
                s = s_ref[h]
                st_ref[h, ci] = s
                v_new = t["u"] - _dot(t["w"], s, NN)
                o_ref[rows, cols] = _dot(t["q_dec"], s, NN) + _dot(t["intra"], v_new, NN)
                s_ref[h] = s * t["cd"] + _dot(t["k_dec"], v_new, TN)
            return carry

        lax.fori_loop(0, cb, chunk, 0)

    blk = pl.BlockSpec((rb, H * HD), lambda b: (b, 0))
    return pl.pallas_call(
        body, name="gdr_fwd", grid=(nb,),
        in_specs=[blk, blk, blk, pl.BlockSpec((rb, LANES), lambda b: (b, 0))],
        out_specs=[blk, pl.BlockSpec((H, cb, HD, HD), lambda b: (0, b, 0, 0))],
        out_shape=[jax.ShapeDtypeStruct((L, H * HD), F32), jax.ShapeDtypeStruct((H, nc, HD, HD), F32)],
        scratch_shapes=[pltpu.VMEM((H, HD, HD), F32)],
        compiler_params=_params(1),
    )(qn, kn, vs, gb)


def _gdr_bwd(qn, kn, vs, gb, states, do):
    L = qn.shape[0]
    nc = L // CK
    cb = min(8, nc)
    rb = cb * CK
    nb = nc // cb

    def body(q_ref, k_ref, v_ref, gb_ref, st_ref, do_ref, dq_ref, dk_ref, dv_ref, dgb_ref, ds_ref):
        @pl.when(pl.program_id(0) == 0)
        def _():
            ds_ref[...] = jnp.zeros_like(ds_ref)

        cst = _gdr_consts()
        causal, strict = cst["causal"], cst["strict"]
        ones = jnp.ones((CK, HD), F32)
        row = lax.broadcasted_iota(jnp.int32, (CK, HD), 0)
        lane = lax.broadcasted_iota(jnp.int32, (CK, LANES), 1)

        def chunk(cj, carry):
            ci = cb - 1 - cj
            rows = pl.ds(pl.multiple_of(ci * CK, CK), CK)
            gbt = gb_ref[rows, :]
            dgb = jnp.zeros((CK, LANES), F32)
            for h in range(H):
                cols = slice(h * HD, (h + 1) * HD)
                q, k, v = q_ref[rows, cols], k_ref[rows, cols], v_ref[rows, cols]
                beta, g = _head_cols(gbt, h)
                t = _gdr_chunk(q, k, v, beta, g, cst)
                s = st_ref[h, ci]
                dout = do_ref[rows, cols]
                ds_out = ds_ref[h]
                u, w, tinv, decay = t["u"], t["w"], t["tinv"], t["decay"]
                eg, ek, cd, kb = t["eg"], t["ek"], t["cd"], t["kb"]
                q_dec, k_dec, intra = t["q_dec"], t["k_dec"], t["intra"]
                v_new = u - _dot(w, s, NN)

                dq_dec = _dot(dout, s, NT)
                ds_new = _dot(q_dec, dout, TN) + ds_out * cd
                dintra = jnp.where(causal, _dot(dout, v_new, NT), 0.0)
                dv_new = _dot(intra, dout, TN) + _dot(k_dec, ds_out, NN)
                dk_dec = _dot(v_new, ds_out, NT)
                dcd = jnp.sum(jnp.sum(ds_out * s, axis=1, keepdims=True), axis=0, keepdims=True)
                dw = -_dot(dv_new, s, NT)
                ds_new = ds_new - _dot(w, dv_new, TN)
                dru = _dot(tinv, dv_new, TN)
                drw = _dot(tinv, dw, TN)
                dlower = -jnp.where(strict, _dot(dru, u, NT) + _dot(drw, w, NT), 0.0)
                dv = dru * beta
                dbeta = jnp.sum(dru * v, axis=-1, keepdims=True)
                dkb = drw * eg
                dgc = jnp.sum(drw * kb, axis=-1, keepdims=True) * eg
                dpk = dlower * decay
                dkb = dkb + _dot(dpk, k, NN)
                dk = _dot(dpk, kb, TN)
                dqk = dintra * decay
                dq = _dot(dqk, k, NN)
                dk = dk + _dot(dqk, q, TN)
                m = (dlower * t["pk"] + dintra * t["qk"]) * decay
                dgc = dgc + jnp.sum(m, axis=-1, keepdims=True) - _dot_f32(m, ones, TN, exact="b")
                dq = dq + dq_dec * eg
                dgc = dgc + jnp.sum(dq_dec * q_dec, axis=-1, keepdims=True)
                dk = dk + dk_dec * ek
                e = jnp.sum(dk_dec * k_dec, axis=-1, keepdims=True)
                dgl = jnp.sum(e, axis=0, keepdims=True) + dcd * cd
                dgc = dgc - e + jnp.where(row == CK - 1, dgl, 0.0)
                dk = dk + dkb * beta
                dbeta = dbeta + jnp.sum(dkb * k, axis=-1, keepdims=True)
                dg = _dot_f32(cst["tril"], dgc, TN, exact="a")

                dq_ref[rows, cols] = dq
                dk_ref[rows, cols] = dk
                dv_ref[rows, cols] = dv
                dgb = dgb + jnp.where(lane == h, dbeta, 0.0) + jnp.where(lane == H + h, dg, 0.0)
                ds_ref[h] = ds_new
            dgb_ref[rows, :] = dgb
            return carry

        lax.fori_loop(0, cb, chunk, 0)

    blk = pl.BlockSpec((rb, H * HD), lambda b: (nb - 1 - b, 0))
    sblk = pl.BlockSpec((rb, LANES), lambda b: (nb - 1 - b, 0))
    return pl.pallas_call(
        body, name="gdr_bwd", grid=(nb,),
        in_specs=[blk, blk, blk, sblk, pl.BlockSpec((H, cb, HD, HD), lambda b: (0, nb - 1 - b, 0, 0)), blk],
        out_specs=[blk, blk, blk, sblk],
        out_shape=[jax.ShapeDtypeStruct((L, H * HD), F32)] * 3 + [jax.ShapeDtypeStruct((L, LANES), F32)],
        scratch_shapes=[pltpu.VMEM((H, HD, HD), F32)],
        compiler_params=_params(1),
    )(qn, kn, vs, gb, states, do)


def _post_fwd(o, p, ya, x, modrows, sp, w_out):
    L = x.shape[0]
    T = _tile(L, 256)

    def body(o_ref, z_ref, ya_ref, x_ref, mod_ref, sp_ref, w_ref, y_ref, x2_ref, yb_ref):
        ndw = sp_ref[2:3, :]
        z = z_ref[...]
        sz = z * _sig(z)
        parts = []
        for h in range(H):
            n, _ = _rms(o_ref[:, h * HD:(h + 1) * HD])
            parts.append(n * ndw * sz[:, h * HD:(h + 1) * HD])
        yb = jnp.concatenate(parts, axis=-1).astype(MXU)
        yb_ref[...] = yb
        y = _dot(ya_ref[...], w_ref[0:AW, :], NN) + _dot(yb, w_ref[AW:2 * AW, :], NN)
        y_ref[...] = y
        x2_ref[...] = x_ref[...] + mod_ref[2:3, :] * y

    row = lambda i: (i, 0)
    zcol = (3 * AW + 3 * H * HD) // (H * HD)
    return pl.pallas_call(
        body, name="post_fwd", grid=(L // T,),
        in_specs=[pl.BlockSpec((T, H * HD), row), pl.BlockSpec((T, H * HD), lambda i: (i, zcol)),
                  pl.BlockSpec((T, AW), row), pl.BlockSpec((T, D), row), _full((SUB, D)), _full((SUB, LANES)),
                  _full((D, D))],
        out_specs=[pl.BlockSpec((T, D), row), pl.BlockSpec((T, D), row), pl.BlockSpec((T, H * HD), row)],
        out_shape=[jax.ShapeDtypeStruct((L, D), F32), jax.ShapeDtypeStruct((L, D), F32),
                   jax.ShapeDtypeStruct((L, H * HD), MXU)],
        compiler_params=_params(1),
    )(o, p, ya, x, modrows, sp, w_out)


FF_COLS = 11
FF_CW = DFF // FF_COLS
FF_ROWS = 512


def _ffn_fwd(x2, modrows, vec, w_up, cff, w_down):
    L = x2.shape[0]
    T = _tile(L, FF_ROWS)
    nj = FF_COLS

    def body(x_ref, mod_ref, vec_ref, wg_ref, wu_ref, cg_ref, cu_ref, wd_ref,
             h_ref, gp_ref, up_ref, f_ref, d_ref, x3_ref, h_s, acc, carry_g, carry_u):
        i, j = pl.program_id(0), pl.program_id(1)

        @pl.when(i == 0)
        def _():
            carry_g[j] = jnp.zeros((SUB, FF_CW), F32)
            carry_u[j] = jnp.zeros((SUB, FF_CW), F32)

        @pl.when(j == 0)
        def _():
            n, _ = _rms(x_ref[...])
            hb = (n * vec_ref[1:2, :] * (1.0 + mod_ref[4:5, :]) + mod_ref[3:4, :]).astype(MXU)
            h_s[...] = hb
            h_ref[...] = hb
            acc[...] = jnp.zeros_like(acc)

        hb = h_s[...]
        g = _dot(hb, wg_ref[...], NN)
        u = _dot(hb, wu_ref[...], NN)
        gp_ref[...] = g
        up_ref[...] = u
        gc, _ = _conv_fwd(g, cg_ref, 3, carry_g[j])
        uc, _ = _conv_fwd(u, cu_ref, 3, carry_u[j])
        carry_g[j] = g[T - SUB:T, :]
        carry_u[j] = u[T - SUB:T, :]
        fb = (gc * _sig(gc) * uc).astype(MXU)
        f_ref[...] = fb
        acc[...] += _dot(fb, wd_ref[...], NN)

        @pl.when(j == nj - 1)
        def _():
            dv = acc[...]
            d_ref[...] = dv
            x3_ref[...] = x_ref[...] + mod_ref[5:6, :] * dv

    row = lambda i, j: (i, 0)
    col = lambda i, j: (i, j)
    return pl.pallas_call(
        body, name="ffn_fwd", grid=(L // T, nj),
        in_specs=[pl.BlockSpec((T, D), row), _full((SUB, D)), _full((SUB, D)),
                  pl.BlockSpec((D, FF_CW), lambda i, j: (0, j)), pl.BlockSpec((D, FF_CW), lambda i, j: (0, nj + j)),
                  pl.BlockSpec((SUB, FF_CW), lambda i, j: (0, j)), pl.BlockSpec((SUB, FF_CW), lambda i, j: (0, nj + j)),
                  pl.BlockSpec((FF_CW, D), lambda i, j: (j, 0))],
        out_specs=[pl.BlockSpec((T, D), row), pl.BlockSpec((T, FF_CW), col), pl.BlockSpec((T, FF_CW), col),
                   pl.BlockSpec((T, FF_CW), col), pl.BlockSpec((T, D), row), pl.BlockSpec((T, D), row)],
        out_shape=[jax.ShapeDtypeStruct((L, D), MXU), jax.ShapeDtypeStruct((L, DFF), F32),
                   jax.ShapeDtypeStruct((L, DFF), F32), jax.ShapeDtypeStruct((L, DFF), MXU),
                   jax.ShapeDtypeStruct((L, D), F32), jax.ShapeDtypeStruct((L, D), F32)],
        scratch_shapes=[pltpu.VMEM((T, D), MXU), pltpu.VMEM((T, D), F32),
                        pltpu.VMEM((nj, SUB, FF_CW), F32), pltpu.VMEM((nj, SUB, FF_CW), F32)],
        compiler_params=_params(2),
    )(x2, modrows, vec, w_up, w_up, cff, cff, w_down)


def _final(x, target, nf):
    L = x.shape[0]
    T = _tile(L, 256)

    def body(x_ref, t_ref, nf_ref, dx_ref, acc_ref):
        @pl.when(pl.program_id(0) == 0)
        def _():
            acc_ref[...] = jnp.zeros_like(acc_ref)

        n, r = _rms(x_ref[...])
        w = nf_ref[0:1, :]
        err = n * w - t_ref[...]
        acc_ref[0:1, :] += (0.5 / D) * _sum0(err * err)
        dy = err * (1.0 / D)
        acc_ref[1:2, :] += _sum0(dy * n)
        dx_ref[...] = _rms_bwd(dy * w, n, r)

    row = lambda i: (i, 0)
    return pl.pallas_call(
        body, name="final_norm_loss", grid=(L // T,),
        in_specs=[pl.BlockSpec((T, D), row), pl.BlockSpec((T, D), row), _full((SUB, D))],
        out_specs=[pl.BlockSpec((T, D), row), _full((SUB, D))],
        out_shape=[jax.ShapeDtypeStruct((L, D), F32), jax.ShapeDtypeStruct((SUB, D), F32)],
        compiler_params=_params(1),
    )(x, target, nf)


def _ffn_bwd(dx3, d, x2, modrows, vec, gpre, upre, cff, w_down, w_up):
    L = dx3.shape[0]
    T = _tile(L, FF_ROWS)
    ni, nj = L // T, FF_COLS
    hb_per_t = T // SUB

    def body(dx3_ref, d_ref, x2_ref, mod_ref, vec_ref, gp_ref, up_ref, gph_ref, uph_ref, cg_ref, cu_ref,
             wd_ref, wg_ref, wu_ref,
             dd_ref, dgp_ref, dup_ref, dx2_ref, accv_ref, dcg_ref, dcu_ref,
             dd_s, acch, carry_g, carry_u):
        i, j = pl.program_id(0), pl.program_id(1)
        ri = ni - 1 - i

        @pl.when((i == 0) & (j == 0))
        def _():
            accv_ref[...] = jnp.zeros_like(accv_ref)
            dcg_ref[...] = jnp.zeros_like(dcg_ref)
            dcu_ref[...] = jnp.zeros_like(dcu_ref)

        @pl.when(i == 0)
        def _():
            carry_g[j] = jnp.zeros((SUB, FF_CW), F32)
            carry_u[j] = jnp.zeros((SUB, FF_CW), F32)

        @pl.when(j == 0)
        def _():
            dx3v = dx3_ref[...]
            accv_ref[0:1, :] += _sum0(dx3v * d_ref[...])
            ddb = (mod_ref[5:6, :] * dx3v).astype(MXU)
            dd_s[...] = ddb
            dd_ref[...] = ddb
            acch[...] = jnp.zeros_like(acch)

        ddb = dd_s[...]
        g, u = gp_ref[...], up_ref[...]
        keep = jnp.where(ri == 0, 0.0, 1.0)
        gc, gsh = _conv_fwd(g, cg_ref, 3, gph_ref[...] * keep)
        uc, ush = _conv_fwd(u, cu_ref, 3, uph_ref[...] * keep)
        sg = _sig(gc)
        df = _dot(ddb, wd_ref[...], NT)
        duc = df * (gc * sg)
        dgc = df * uc * (sg * (1.0 + gc * (1.0 - sg)))
        for s in range(3):
            dcg_ref[j, 2 - s:3 - s, :] += _sum0(dgc * gsh[s])
            dcu_ref[j, 2 - s:3 - s, :] += _sum0(duc * ush[s])
        dg = _conv_bwd_in(dgc, cg_ref, 3, carry_g[j]).astype(MXU)
        du = _conv_bwd_in(duc, cu_ref, 3, carry_u[j]).astype(MXU)
        carry_g[j] = dgc[0:SUB, :]
        carry_u[j] = duc[0:SUB, :]
        dgp_ref[...] = dg
        dup_ref[...] = du
        acch[...] += _dot(dg, wg_ref[...], NT) + _dot(du, wu_ref[...], NT)

        @pl.when(j == nj - 1)
        def _():
            dh = acch[...]
            n, r = _rms(x2_ref[...])
            nw, sc = vec_ref[1:2, :], mod_ref[4:5, :]
            accv_ref[1:2, :] += _sum0(dh)
            accv_ref[2:3, :] += _sum0(dh * n * nw)
            accv_ref[3:4, :] += _sum0(dh * n * (1.0 + sc))
            dx2_ref[...] = _rms_bwd(dh * nw * (1.0 + sc), n, r) + dx3_ref[...]

    row = lambda i, j: (ni - 1 - i, 0)
    col = lambda i, j: (ni - 1 - i, j)
    halo = lambda i, j: (jnp.maximum((ni - 1 - i) * hb_per_t - 1, 0), j)
    return pl.pallas_call(
        body, name="ffn_bwd", grid=(ni, nj),
        in_specs=[pl.BlockSpec((T, D), row), pl.BlockSpec((T, D), row), pl.BlockSpec((T, D), row),
                  _full((SUB, D)), _full((SUB, D)),
                  pl.BlockSpec((T, FF_CW), col), pl.BlockSpec((T, FF_CW), col),
                  pl.BlockSpec((SUB, FF_CW), halo), pl.BlockSpec((SUB, FF_CW), halo),
                  pl.BlockSpec((SUB, FF_CW), lambda i, j: (0, j)), pl.BlockSpec((SUB, FF_CW), lambda i, j: (0, nj + j)),
                  pl.BlockSpec((FF_CW, D), lambda i, j: (j, 0)),
                  pl.BlockSpec((D, FF_CW), lambda i, j: (0, j)), pl.BlockSpec((D, FF_CW), lambda i, j: (0, nj + j))],
        out_specs=[pl.BlockSpec((T, D), row), pl.BlockSpec((T, FF_CW), col), pl.BlockSpec((T, FF_CW), col),
                   pl.BlockSpec((T, D), row), _full((SUB, D)), _full((nj, SUB, FF_CW)), _full((nj, SUB, FF_CW))],
        out_shape=[jax.ShapeDtypeStruct((L, D), MXU), jax.ShapeDtypeStruct((L, DFF), MXU),
                   jax.ShapeDtypeStruct((L, DFF), MXU), jax.ShapeDtypeStruct((L, D), F32),
                   jax.ShapeDtypeStruct((SUB, D), F32), jax.ShapeDtypeStruct((nj, SUB, FF_CW), F32),
                   jax.ShapeDtypeStruct((nj, SUB, FF_CW), F32)],
        scratch_shapes=[pltpu.VMEM((T, D), MXU), pltpu.VMEM((T, D), F32),
                        pltpu.VMEM((nj, SUB, FF_CW), F32), pltpu.VMEM((nj, SUB, FF_CW), F32)],
        compiler_params=_params(2),
    )(dx3, d, x2, modrows, vec, gpre, upre, gpre, upre, cff, cff, w_down, w_up, w_up)


def _post_bwd(dx2, y, o, p, modrows, sp, w_out):
    L = dx2.shape[0]
    T = _tile(L, 256)

    def body(dx2_ref, y_ref, o_ref, z_ref, mod_ref, sp_ref, w_ref, dy_ref, do_ref, dz_ref, dya_ref, accv_ref, accs_ref):
        @pl.when(pl.program_id(0) == 0)
        def _():
            accv_ref[...] = jnp.zeros_like(accv_ref)
            accs_ref[...] = jnp.zeros_like(accs_ref)

        dx2v = dx2_ref[...]
        accv_ref[0:1, :] += _sum0(dx2v * y_ref[...])
        dyb = (mod_ref[2:3, :] * dx2v).astype(MXU)
        dy_ref[...] = dyb
        dyc = _dot(dyb, w_ref[...], NT)
        dya_ref[...] = dyc[:, 0:AW]
        ndw = sp_ref[2:3, :]
        z = z_ref[...]
        sgz = _sig(z)
        dsz = sgz * (1.0 + z * (1.0 - sgz))
        dndw = jnp.zeros((1, HD), F32)
        for h in range(H):
            sl = slice(h * HD, (h + 1) * HD)
            n, r = _rms(o_ref[:, sl])
            dyh = dyc[:, AW + h * HD:AW + (h + 1) * HD]
            zh = z[:, sl]
            don = dyh * (zh * sgz[:, sl])
            dz_ref[:, sl] = dyh * (n * ndw) * dsz[:, sl]
            dndw = dndw + _sum0(don * n)
            do_ref[:, sl] = _rms_bwd(don * ndw, n, r)
        accs_ref[0:1, :] += dndw

    row = lambda i: (i, 0)
    zcol = (3 * AW + 3 * H * HD) // (H * HD)
    return pl.pallas_call(
        body, name="post_bwd", grid=(L // T,),
        in_specs=[pl.BlockSpec((T, D), row), pl.BlockSpec((T, D), row), pl.BlockSpec((T, H * HD), row),
                  pl.BlockSpec((T, H * HD), lambda i: (i, zcol)), _full((SUB, D)), _full((SUB, LANES)), _full((D, D))],
        out_specs=[pl.BlockSpec((T, D), row)] + [pl.BlockSpec((T, H * HD), row)] * 3 + [_full((SUB, D)), _full((SUB, LANES))],
        out_shape=[jax.ShapeDtypeStruct((L, D), MXU)] + [jax.ShapeDtypeStruct((L, H * HD), F32)] * 3
        + [jax.ShapeDtypeStruct((SUB, D), F32), jax.ShapeDtypeStruct((SUB, LANES), F32)],
        compiler_params=_params(1),
    )(dx2, y, o, p, modrows, sp, w_out)


def _pre_bwd(p, dqn, dkn, dvs, dya, dz, dgb, pa, cq, sp):
    L = p.shape[0]
    T = _tile(L, 256)
    ni = L // T
    scale = HD ** -0.5
    w3 = 3 * AW + 3 * H * HD
    hb_per_t = T // SUB

    def body(pm_ref, ph_ref, ps_ref, dq_ref, dk_ref, dv_ref, dya_ref, dz_ref, dgb_ref, pa_ref, cq_ref, sp_ref,
             dp_ref, dpa_ref, dcq_ref, dsp_ref, carry_u, carry_q):
        i = pl.program_id(0)
        ri = ni - 1 - i

        @pl.when(i == 0)
        def _():
            dpa_ref[...] = jnp.zeros_like(dpa_ref)
            dcq_ref[...] = jnp.zeros_like(dcq_ref)
            dsp_ref[...] = jnp.zeros_like(dsp_ref)
            carry_u[...] = jnp.zeros_like(carry_u)
            carry_q[...] = jnp.zeros_like(carry_q)

        keep = jnp.where(ri == 0, 0.0, 1.0)
        a_b, a_c, a_x = pm_ref[:, 0:AW], pm_ref[:, AW:2 * AW], pm_ref[:, 2 * AW:3 * AW]
        u = a_c * a_x
        hu = ph_ref[:, AW:2 * AW] * ph_ref[:, 2 * AW:3 * AW] * keep
        cu, ush = _conv_fwd(u, pa_ref, 3, hu)
        yp = a_b * cu
        bd = _blockdiag_mean(AW, A_GROUP)
        ra = lax.rsqrt(_dot(yp * yp, bd, NN, HI) + EPS)
        na = yp * ra
        dya = dya_ref[...]
        dpa_ref[3:4, :] += _sum0(dya * na)
        dna = dya * pa_ref[3:4, :]
        dyp = ra * (dna - na * _dot(dna * na, bd, NN, HI))
        dcu = dyp * a_b
        for s in range(3):
            dpa_ref[2 - s:3 - s, :] += _sum0(dcu * ush[s])
        du = _conv_bwd_in(dcu, pa_ref, 3, carry_u[...])
        carry_u[...] = dcu[0:SUB, :]
        dp_ref[:, 0:AW] = (dyp * cu).astype(MXU)
        dp_ref[:, AW:2 * AW] = (du * a_x).astype(MXU)
        dp_ref[:, 2 * AW:3 * AW] = (du * a_c).astype(MXU)

        qkv = pm_ref[:, 3 * AW:w3]
        qc, qsh = _conv_fwd(qkv, cq_ref, 4, ph_ref[:, 3 * AW:w3] * keep)
        sg = _sig(qc)
        qs = qc * sg
        parts = []
        for h in range(H):
            q = qs[:, h * HD:(h + 1) * HD]
            rq = lax.rsqrt(jnp.sum(q * q, axis=-1, keepdims=True) + EPS)
            parts.append(_l2_bwd(dq_ref[:, h * HD:(h + 1) * HD] * scale, q * rq, rq))
        for h in range(H):
            k = qs[:, (H + h) * HD:(H + h + 1) * HD]
            rk = lax.rsqrt(jnp.sum(k * k, axis=-1, keepdims=True) + EPS)
            parts.append(_l2_bwd(dk_ref[:, h * HD:(h + 1) * HD], k * rk, rk))
        parts.append(dv_ref[...])
        dqc = jnp.concatenate(parts, axis=-1) * (sg * (1.0 + qc * (1.0 - sg)))
        for s in range(4):
            dcq_ref[3 - s:4 - s, :] += _sum0(dqc * qsh[s])
        dp_ref[:, 3 * AW:w3] = _conv_bwd_in(dqc, cq_ref, 4, carry_q[...]).astype(MXU)
        carry_q[...] = dqc[0:SUB, :]
        dp_ref[:, w3:w3 + H * HD] = dz_ref[...].astype(MXU)

        lane, a, xb, beta, g = _gate_small(ps_ref[...], sp_ref)
        dgb = dgb_ref[...]
        dbeta = jnp.where(lane < H, dgb, 0.0)
        dg = jnp.where((lane >= H) & (lane < 2 * H), dgb, 0.0)
        dalpha = dg * a * _sig(xb)
        dsp_ref[0:1, :] += _sum0(dg * g)
        dsp_ref[1:2, :] += _sum0(dalpha)
        dp_ref[:, w3 + H * HD:P_PAD] = (dbeta * beta * (1.0 - beta) + dalpha).astype(MXU)

    row = lambda i: (ni - 1 - i, 0)
    halo = lambda i: (jnp.maximum((ni - 1 - i) * hb_per_t - 1, 0), 0)
    hrow = pl.BlockSpec((T, H * HD), row)
    return pl.pallas_call(
        body, name="pre_bwd", grid=(ni,),
        in_specs=[pl.BlockSpec((T, w3), row), pl.BlockSpec((SUB, w3), halo),
                  pl.BlockSpec((T, LANES), lambda i: (ni - 1 - i, (P_PAD - LANES) // LANES)),
                  hrow, hrow, hrow, pl.BlockSpec((T, AW), row), hrow,
                  pl.BlockSpec((T, LANES), row),
                  _full((SUB, AW)), _full((SUB, 3 * H * HD)), _full((SUB, LANES))],
        out_specs=[pl.BlockSpec((T, P_PAD), row), _full((SUB, AW)), _full((SUB, 3 * H * HD)), _full((SUB, LANES))],
        out_shape=[jax.ShapeDtypeStruct((L, P_PAD), MXU), jax.ShapeDtypeStruct((SUB, AW), F32),
                   jax.ShapeDtypeStruct((SUB, 3 * H * HD), F32), jax.ShapeDtypeStruct((SUB, LANES), F32)],
        scratch_shapes=[pltpu.VMEM((SUB, AW), F32), pltpu.VMEM((SUB, 3 * H * HD), F32)],
        compiler_params=_params(1),
    )(p, p, p, dqn, dkn, dvs, dya, dz, dgb, pa, cq, sp)


def _in_bwd(dp, w_in, x, dx2, modrows, vec):
    L = x.shape[0]
    T = _tile(L, 256)

    def body(dp_ref, w_ref, x_ref, dx2_ref, mod_ref, vec_ref, dx_ref, accv_ref):
        @pl.when(pl.program_id(0) == 0)
        def _():
            accv_ref[...] = jnp.zeros_like(accv_ref)

        dh = _dot(dp_ref[...], w_ref[...], NT)
        n, r = _rms(x_ref[...])
        nw, sc = vec_ref[0:1, :], mod_ref[1:2, :]
        accv_ref[0:1, :] += _sum0(dh)
        accv_ref[1:2, :] += _sum0(dh * n * nw)
        accv_ref[2:3, :] += _sum0(dh * n * (1.0 + sc))
        dx_ref[...] = _rms_bwd(dh * nw * (1.0 + sc), n, r) + dx2_ref[...]

    row = lambda i: (i, 0)
    return pl.pallas_call(
        body, name="in_bwd", grid=(L // T,),
        in_specs=[pl.BlockSpec((T, P_PAD), row), _full((D, P_PAD)), pl.BlockSpec((T, D), row),
                  pl.BlockSpec((T, D), row), _full((SUB, D)), _full((SUB, D))],
        out_specs=[pl.BlockSpec((T, D), row), _full((SUB, D))],
        out_shape=[jax.ShapeDtypeStruct((L, D), F32), jax.ShapeDtypeStruct((SUB, D), F32)],
        compiler_params=_params(1),
    )(dp, w_in, x, dx2, modrows, vec)


def _wgrad(a, b, tm, tn, name):
    L, m = a.shape
    n = b.shape[1]
    tl = _tile(L, 512)
    tm, tn = _tile(m, tm), _tile(n, tn)
    nl = L // tl

    def body(a_ref, b_ref, o_ref):
        @pl.when(pl.program_id(2) == 0)
        def _():
            o_ref[...] = jnp.zeros_like(o_ref)

        o_ref[...] += _dot(a_ref[...], b_ref[...], TN)

    return pl.pallas_call(
        body, name=name, grid=(m // tm, n // tn, nl),
        in_specs=[pl.BlockSpec((tl, tm), lambda i, j, l: (l, i)), pl.BlockSpec((tl, tn), lambda i, j, l: (l, j))],
        out_specs=pl.BlockSpec((tm, tn), lambda i, j, l: (i, j)),
        out_shape=jax.ShapeDtypeStruct((m, n), F32), compiler_params=_params(3),
    )(a, b)


def _wgrad_cols(a, b, tm, n_shard, wpad, count, name):
    L, m = a.shape
    n = b.shape[1]
    tl = _tile(L, 512)
    tm = _tile(m, tm)
    nl = L // tl
    wins = _shard_windows(n_shard, count)
    assert all(a_ * LANES + win <= n for a_, _, win in wins), (wins, n)

    def body(a_ref, b_ref, o_ref, acc):
        @pl.when(pl.program_id(1) == 0)
        def _():
            acc[...] = jnp.zeros_like(acc)

        acc[...] += _dot(a_ref[...], b_ref[...], TN)

        @pl.when(pl.program_id(1) == nl - 1)
        def _():
            for k, (a_, s, win) in enumerate(wins):
                xk = acc[:, a_ * LANES:a_ * LANES + win]
                if s:
                    xk = pltpu.roll(xk, win - s, 1)
                o_ref[k] = _fit_lanes(xk, wpad)

    return pl.pallas_call(
        body, name=name, grid=(m // tm, nl),
        in_specs=[pl.BlockSpec((tl, tm), lambda i, l: (l, i)), pl.BlockSpec((tl, n), lambda i, l: (l, 0))],
        out_specs=pl.BlockSpec((count, tm, wpad), lambda i, l: (0, i, 0)),
        out_shape=jax.ShapeDtypeStruct((count, m, wpad), F32),
        scratch_shapes=[pltpu.VMEM((tm, n), F32)],
        compiler_params=_params(2),
    )(a, b)


def _adamw(w, g, m, v, name):
    r, n = w.shape
    tr = _tile(r, 512)
    bc1 = 1.0 - ADAM_B1 ** ADAM_STEP
    bc2 = 1.0 - ADAM_B2 ** ADAM_STEP

    def body(w_ref, g_ref, m_ref, v_ref, d_ref, nm_ref, nv_ref):
        gv = g_ref[...]
        nm = ADAM_B1 * m_ref[...] + (1.0 - ADAM_B1) * gv
        nv = ADAM_B2 * v_ref[...] + (1.0 - ADAM_B2) * (gv * gv)
        nm_ref[...] = nm
        nv_ref[...] = nv
        d_ref[...] = -ADAM_LR * ((nm / bc1) / (jnp.sqrt(nv / bc2) + ADAM_EPS) + ADAM_WD * w_ref[...])

    spec = pl.BlockSpec((tr, n), lambda i: (i, 0))
    return pl.pallas_call(
        body, name=name, grid=(r // tr,), in_specs=[spec] * 4, out_specs=[spec] * 3,
        out_shape=[jax.ShapeDtypeStruct((r, n), F32)] * 3, compiler_params=_params(1),
    )(w, g, m, v)


def _rows8(rows, width):
    out = jnp.zeros((SUB, width), F32)
    for r, vrow in enumerate(rows):
        out = out.at[r, :vrow.shape[0]].set(vrow)
    return out


def _at_lanes(v4, start):
    return jnp.zeros((LANES,), F32).at[start:start + v4.shape[0]].set(v4)


def _pad_rows(flat, mult):
    n = flat.shape[0]
    pad = (-n) % mult
    return jnp.pad(flat, (0, pad)) if pad else flat


IN_PAD = 512
UP_PAD = 768


def _local_fwd_bwd(x, target, mod_full, small_w, full_w):
    norm1_w, norm2_w, norm_a_w, a_log, dt_bias, norm_dn_w, norm_f_w = small_w
    w_in_f, w_out_f, w_up_f, w_down_f, conv_a_f, conv_q_f, conv_f_f = full_w

    def layer_params(i):
        modrows = jnp.concatenate([mod_full[i], jnp.zeros((SUB - N_MOD, D), F32)], axis=0)
        vec = _rows8([norm1_w[i], norm2_w[i]], D)
        pa = _rows8([conv_a_f[i, 0], conv_a_f[i, 1], conv_a_f[i, 2], norm_a_w[i]], AW)
        cq = _rows8([conv_q_f[i, k] for k in range(4)], 3 * H * HD)
        sp = _rows8([_at_lanes(a_log[i], H), _at_lanes(dt_bias[i], H), norm_dn_w[i]], LANES)
        cff = _rows8([conv_f_f[i, k] for k in range(3)], 2 * DFF)
        return modrows, vec, pa, cq, sp, cff

    saved = []
    xi = x
    for i in range(DEPTH):
        modrows, vec, pa, cq, sp, cff = layer_params(i)
        p, h1 = _in_proj(xi, modrows, vec, w_in_f[i])
        qn, kn, vs, gb, ya = _pre_fwd(p, pa, cq, sp)
        o, states = _gdr_fwd(qn, kn, vs, gb)
        y, x2, yb = _post_fwd(o, p, ya, xi, modrows, sp, w_out_f[i])
        h2, gpre, upre, f, dff, x3 = _ffn_fwd(x2, modrows, vec, w_up_f[i], cff, w_down_f[i])
        saved.append(dict(x=xi, p=p, h1=h1, qn=qn, kn=kn, vs=vs, gb=gb, ya=ya, o=o, states=states, y=y, x2=x2, yb=yb,
                          h2=h2, gpre=gpre, upre=upre, f=f, d=dff))
        xi = x3

    dx, facc = _final(xi, target, _rows8([norm_f_w], D))
    loss_local = jnp.sum(facc[0])
    d_norm_f = facc[1]

    gw_in, gw_out, gw_up, gw_down = [None] * DEPTH, [None] * DEPTH, [None] * DEPTH, [None] * DEPTH
    g_small = [None] * DEPTH
    for i in reversed(range(DEPTH)):
        s = saved[i]
        modrows, vec, pa, cq, sp, cff = layer_params(i)
        dd, dgp, dup, dx2, accf, dcg, dcu = _ffn_bwd(dx, s["d"], s["x2"], modrows, vec, s["gpre"], s["upre"], cff,
                                                       w_down_f[i], w_up_f[i])
        n_up, up_pad = 2 * DFF // N_DEV, UP_PAD
        gw_up[i] = jnp.concatenate([_wgrad_cols(s["h2"], dgp, 512, n_up, up_pad, N_DEV // 2, "wgrad_up"),
                                    _wgrad_cols(s["h2"], dup, 512, n_up, up_pad, N_DEV // 2, "wgrad_up")], axis=0)
        gw_down[i] = _wgrad(s["f"], dd, DFF // 2, 1024, "wgrad_down").reshape(N_DEV, DFF // N_DEV, D)
        dy, do, dz, dya, accp, accs = _post_bwd(dx2, s["y"], s["o"], s["p"], modrows, sp, w_out_f[i])
        gw_out[i] = jnp.concatenate([_wgrad(s["ya"], dy, 512, 1024, "wgrad_out"),
                                     _wgrad(s["yb"], dy, 512, 1024, "wgrad_out")], axis=0).reshape(N_DEV, D // N_DEV, D)
        dqn, dkn, dvs, dgb = _gdr_bwd(s["qn"], s["kn"], s["vs"], s["gb"], s["states"], do)
        dp, dpa, dcq, dsp = _pre_bwd(s["p"], dqn, dkn, dvs, dya, dz, dgb, pa, cq, sp)
        gw_in[i] = _wgrad_cols(s["h1"], dp, 512, P_IN // N_DEV, IN_PAD, N_DEV, "wgrad_in")
        dx, acci = _in_bwd(dp, w_in_f[i], s["x"], dx2, modrows, vec)
        dconv_ff = jnp.concatenate([dcg.transpose(1, 0, 2).reshape(SUB, DFF), dcu.transpose(1, 0, 2).reshape(SUB, DFF)],
                                   axis=1)[0:3]
        dmod = jnp.stack([acci[0], acci[1], accp[0], accf[1], accf[2], accf[0]])
        g_small[i] = dict(norm1=acci[2], norm2=accf[3], norm_a=dpa[3], a_log=dsp[0, H:2 * H], dt_bias=dsp[1, H:2 * H],
                          norm_dn=accs[0], conv_a=dpa[0:3], conv_qkv=dcq[0:4], conv_ff=dconv_ff, dmod=dmod.reshape(-1))
    return loss_local, dx, gw_in, gw_out, gw_up, gw_down, g_small, d_norm_f


def kernel(x, c, ada_w, ada_b, norm1_w, w_in, conv_a_w, norm_a_w, conv_qkv_w, a_log, dt_bias, norm_dn_w, w_out, norm2_w, w_up, conv_ff_w, w_down, norm_f_w, loss_target, m_ada_w, m_ada_b, m_norm1_w, m_w_in, m_conv_a_w, m_norm_a_w, m_conv_qkv_w, m_a_log, m_dt_bias, m_norm_dn_w, m_w_out, m_norm2_w, m_w_up, m_conv_ff_w, m_w_down, m_norm_f_w, v_ada_w, v_ada_b, v_norm1_w, v_w_in, v_conv_a_w, v_norm_a_w, v_conv_qkv_w, v_a_log, v_dt_bias, v_norm_dn_w, v_w_out, v_norm2_w, v_w_up, v_conv_ff_w, v_w_down, v_norm_f_w):
    ax, ay, ac = lax.axis_index("x"), lax.axis_index("y"), lax.axis_index("c")
    me = 4 * ax + 2 * ay + ac
    x = x[0]
    target = loss_target[0]
    n_in, n_up = P_IN // N_DEV, 2 * DFF // N_DEV

    def lane_pad(t, width):
        return jnp.pad(t.astype(MXU), ((0, 0), (0, 0), (0, width - t.shape[-1])))

    conv_blob = _pad_rows(jnp.concatenate([t.reshape(-1) for t in (conv_a_w, conv_qkv_w, conv_ff_w)]),
                          SUB * LANES).reshape(-1, LANES)
    c_rows = jnp.zeros((SUB, D), F32).at[0].set(c[0])
    g_in, g_out, g_up, g_down, g_conv, g_c = _all_gather(
        [lane_pad(w_in, IN_PAD), w_out.astype(MXU), lane_pad(w_up, UP_PAD), w_down.astype(MXU), conv_blob, c_rows],
        "gather_weights", in_vmem=False)
    w_in_f = _interleave_cols(g_in, n_in, P_PAD, "interleave_w_in")
    w_up_f = _interleave_cols(g_up, n_up, 2 * DFF, "interleave_w_up")
    w_out_f = g_out.transpose(1, 0, 2, 3).reshape(DEPTH, D, D)
    w_down_f = g_down.transpose(1, 0, 2, 3).reshape(DEPTH, DFF, D)
    sg = g_conv.reshape(N_DEV, -1)
    o1 = conv_a_w.size
    o2 = o1 + conv_qkv_w.size
    o3 = o2 + conv_ff_w.size
    conv_a_f = sg[:, 0:o1].reshape(N_DEV, DEPTH, 3, AW // N_DEV).transpose(1, 2, 0, 3).reshape(DEPTH, 3, AW)
    conv_q_f = sg[:, o1:o2].reshape(N_DEV, DEPTH, 4, 3 * H * HD // N_DEV).transpose(1, 2, 0, 3).reshape(DEPTH, 4, 3 * H * HD)
    conv_f_f = sg[:, o2:o3].reshape(N_DEV, DEPTH, 3, n_up).transpose(1, 2, 0, 3).reshape(DEPTH, 3, 2 * DFF)

    c_all = jnp.concatenate([g_c[:, 0], jnp.zeros((16 - N_DEV, D), F32)], axis=0)
    n_ada = N_MOD * D // N_DEV
    ada_b_cols = lax.dynamic_slice_in_dim(ada_b, me * n_ada, n_ada, axis=1)[:, None, :]
    mod_sh = _mod_fwd(c_all, ada_w, ada_b_cols)
    mod_all = _all_gather([mod_sh.reshape(DEPTH * 16, n_ada)], "gather_mod", in_vmem=True)[0]
    mod_all = mod_all.reshape(N_DEV, DEPTH, 16, n_ada)
    mod_mine = lax.dynamic_index_in_dim(mod_all, me, axis=2, keepdims=False)
    mod_full = mod_mine.transpose(1, 0, 2).reshape(DEPTH, N_MOD, D)

    loss_local, dx, gw_in, gw_out, gw_up, gw_down, g_small, d_norm_f = _local_fwd_bwd(
        x, target, mod_full, (norm1_w, norm2_w, norm_a_w, a_log, dt_bias, norm_dn_w, norm_f_w),
        (w_in_f, w_out_f, w_up_f, w_down_f, conv_a_f, conv_q_f, conv_f_f))
    loss = lax.psum(loss_local, ("x", "y", "c"))
    grad_x = dx[None]

    keys = ["dmod", "norm1", "norm2", "norm_a", "a_log", "dt_bias", "norm_dn", "conv_a", "conv_qkv", "conv_ff"]
    stacked = {k: jnp.stack([g_small[i][k] for i in range(DEPTH)]) for k in keys}
    flat_parts = [stacked[k].reshape(-1) for k in keys] + [d_norm_f]
    sizes = [int(t.shape[0]) for t in flat_parts]
    sflat = _pad_rows(jnp.concatenate(flat_parts), SUB * LANES).reshape(-1, LANES)
    sall = _all_gather([sflat], "gather_small_grads", in_vmem=True)[0]
    ssum = _sum_devices(sall).reshape(-1)
    so = [0]
    for sz in sizes:
        so.append(so[-1] + sz)
    red = {k: ssum[so[n]:so[n + 1]].reshape(stacked[k].shape) for n, k in enumerate(keys)}
    g_norm_f = ssum[so[len(keys)]:so[len(keys) + 1]]
    dmod_all = sall[:, 0:sizes[0] // LANES, :].reshape(N_DEV, DEPTH, N_MOD * D)

    g_ada_b = red["dmod"].reshape(DEPTH, N_MOD * D)
    dmod_cols = lax.dynamic_slice_in_dim(dmod_all, me * n_ada, n_ada, axis=2).transpose(1, 0, 2)
    dmod_cols = jnp.concatenate([dmod_cols, jnp.zeros((DEPTH, 16 - N_DEV, n_ada), F32)], axis=1)
    g_ada_w = _mod_bwd(c_all, dmod_cols)
    g_conv_a = lax.dynamic_slice_in_dim(red["conv_a"], me * (AW // N_DEV), AW // N_DEV, axis=2)
    g_conv_qkv = lax.dynamic_slice_in_dim(red["conv_qkv"], me * (3 * H * HD // N_DEV), 3 * H * HD // N_DEV, axis=2)
    g_conv_ff = lax.dynamic_slice_in_dim(red["conv_ff"], me * n_up, n_up, axis=2)

    tags = ["w_in", "w_out", "w_up", "w_down"]
    gs = [jnp.stack(t, axis=1) for t in (gw_in, gw_out, gw_up, gw_down)]
    my_c = jnp.reshape(ac, (1,)).astype(jnp.int32)
    my_chip = jnp.reshape(2 * ax + ay, (1,)).astype(jnp.int32)
    recv1 = _rs_sibling(gs)
    pairs = [_rs_add_pairs(g, r, my_c, "rs_add_pairs_" + t) for g, r, t in zip(gs, recv1, tags)]
    recv2 = _rs_chips([pb for _, pb in pairs])
    mine = [_rs_add_chips(pf, r, my_chip, "rs_add_chips_" + t) for (pf, _), r, t in zip(pairs, recv2, tags)]
    g_w_in = mine[0][:, :, :n_in]
    g_w_out = mine[1]
    g_w_up = mine[2][:, :, :n_up]
    g_w_down = mine[3]

    grads = dict(ada_w=g_ada_w, ada_b=g_ada_b, norm1_w=red["norm1"], w_in=g_w_in, conv_a_w=g_conv_a,
                 norm_a_w=red["norm_a"], conv_qkv_w=g_conv_qkv, a_log=red["a_log"], dt_bias=red["dt_bias"],
                 norm_dn_w=red["norm_dn"], w_out=g_w_out, norm2_w=red["norm2"], w_up=g_w_up, conv_ff_w=g_conv_ff,
                 w_down=g_w_down, norm_f_w=g_norm_f)
    weights = dict(ada_w=ada_w, ada_b=ada_b, norm1_w=norm1_w, w_in=w_in, conv_a_w=conv_a_w, norm_a_w=norm_a_w,
                   conv_qkv_w=conv_qkv_w, a_log=a_log, dt_bias=dt_bias, norm_dn_w=norm_dn_w, w_out=w_out,
                   norm2_w=norm2_w, w_up=w_up, conv_ff_w=conv_ff_w, w_down=w_down, norm_f_w=norm_f_w)
    ms = dict(ada_w=m_ada_w, ada_b=m_ada_b, norm1_w=m_norm1_w, w_in=m_w_in, conv_a_w=m_conv_a_w, norm_a_w=m_norm_a_w,
              conv_qkv_w=m_conv_qkv_w, a_log=m_a_log, dt_bias=m_dt_bias, norm_dn_w=m_norm_dn_w, w_out=m_w_out,
              norm2_w=m_norm2_w, w_up=m_w_up, conv_ff_w=m_conv_ff_w, w_down=m_w_down, norm_f_w=m_norm_f_w)
    vs_ = dict(ada_w=v_ada_w, ada_b=v_ada_b, norm1_w=v_norm1_w, w_in=v_w_in, conv_a_w=v_conv_a_w, norm_a_w=v_norm_a_w,
               conv_qkv_w=v_conv_qkv_w, a_log=v_a_log, dt_bias=v_dt_bias, norm_dn_w=v_norm_dn_w, w_out=v_w_out,
               norm2_w=v_norm2_w, w_up=v_w_up, conv_ff_w=v_conv_ff_w, w_down=v_w_down, norm_f_w=v_norm_f_w)
    names = list(weights)
    big_names = ["ada_w", "w_in", "w_out", "w_up", "w_down"]
    delta, new_m, new_v = {}, {}, {}
    for n in big_names:
        shp = weights[n].shape
        two = lambda t: t.reshape(-1, shp[-1])
        dl, nm, nv = _adamw(two(weights[n]), two(grads[n]), two(ms[n]), two(vs_[n]), "adamw_" + n)
        delta[n], new_m[n], new_v[n] = dl.reshape(shp), nm.reshape(shp), nv.reshape(shp)
    small_names = [n for n in names if n not in big_names]

    def pack(dct):
        return _pad_rows(jnp.concatenate([dct[n].reshape(-1) for n in small_names]), SUB * LANES).reshape(-1, LANES)

    dl, nm, nv = _adamw(pack(weights), pack(grads), pack(ms), pack(vs_), "adamw_small")
    off = 0
    for n in small_names:
        sz, shp = weights[n].size, weights[n].shape
        delta[n] = dl.reshape(-1)[off:off + sz].reshape(shp)
        new_m[n] = nm.reshape(-1)[off:off + sz].reshape(shp)
        new_v[n] = nv.reshape(-1)[off:off + sz].reshape(shp)
        off += sz

    return (loss, grad_x, *[grads[n] for n in names], *[delta[n] for n in names],
            *[new_m[n] for n in names], *[new_v[n] for n in names])
```

```python
import functools
import math

import jax
import jax.numpy as jnp
from jax import lax
from jax.experimental import pallas as pl
from jax.experimental.pallas import tpu as pltpu

F32 = jnp.float32
MXU = jnp.bfloat16

D = 1024
DEPTH = 4
N_MOD = 6
AW = 512
A_GROUP = 64
H = 4
HD = 128
CK = 64
DFF = 2816
P_IN = 3592
P_PAD = 3712
EPS = 1e-6
N_DEV = 8
LANES = 128
SUB = 8
VMEM_LIMIT = 56 * 1024 * 1024

ADAM_LR, ADAM_B1, ADAM_B2, ADAM_EPS, ADAM_WD, ADAM_STEP = 0.001, 0.9, 0.999, 1e-08, 0.01, 10

NN = ((1,), (0,))
NT = ((1,), (1,))
TN = ((0,), (0,))
HI = lax.Precision.HIGHEST
MESH = pl.DeviceIdType.MESH


def _dot(a, b, dims, prec=None):
    if prec is None:
        a = a.astype(MXU) if a.dtype == F32 else a
        b = b.astype(MXU) if b.dtype == F32 else b
    return lax.dot_general(a, b, (dims, ((), ())), precision=prec, preferred_element_type=F32)


def _params(n_grid=0, limit=VMEM_LIMIT):
    sem = ("arbitrary",) * n_grid if n_grid else None
    return pltpu.CompilerParams(dimension_semantics=sem, vmem_limit_bytes=limit)


def _tile(n, want):
    if n <= want:
        return n
    t = want - want % SUB
    while n % t:
        t -= SUB
    assert t > 0, (n, want)
    return t


def _full(shape):
    nd = len(shape)
    return pl.BlockSpec(shape, lambda *_: (0,) * nd)


def _sig(x):
    return jax.nn.sigmoid(x)


def _rms(x):
    r = lax.rsqrt(jnp.mean(x * x, axis=-1, keepdims=True) + EPS)
    return x * r, r


def _rms_bwd(dn, n, r):
    return r * (dn - n * jnp.mean(dn * n, axis=-1, keepdims=True))


def _l2_bwd(dn, n, r):
    return r * (dn - n * jnp.sum(dn * n, axis=-1, keepdims=True))


def _sum0(x):
    return jnp.sum(x, axis=0, keepdims=True)


def _shift_down(x, s, halo):
    r = pltpu.roll(x, s, 0)
    row = lax.broadcasted_iota(jnp.int32, x.shape, 0)
    for k in range(s):
        r = jnp.where(row == k, halo[SUB - s + k:SUB - s + k + 1, :], r)
    return r


def _shift_up(x, s, halo):
    t = x.shape[0]
    r = pltpu.roll(x, t - s, 0)
    row = lax.broadcasted_iota(jnp.int32, x.shape, 0)
    for k in range(s):
        r = jnp.where(row == t - s + k, halo[k:k + 1, :], r)
    return r


def _conv_fwd(x, w_ref, width, halo):
    sh = [x] + [_shift_down(x, s, halo) for s in range(1, width)]
    out = w_ref[width - 1:width, :] * sh[0]
    for s in range(1, width):
        out = out + w_ref[width - 1 - s:width - s, :] * sh[s]
    return out, sh


def _conv_bwd_in(dout, w_ref, width, halo_next):
    dx = w_ref[width - 1:width, :] * dout
    for s in range(1, width):
        dx = dx + w_ref[width - 1 - s:width - s, :] * _shift_up(dout, s, halo_next)
    return dx


def _blockdiag_mean(n, group):
    r = lax.shift_right_logical(lax.broadcasted_iota(jnp.int32, (n, n), 0), int(math.log2(group)))
    c = lax.shift_right_logical(lax.broadcasted_iota(jnp.int32, (n, n), 1), int(math.log2(group)))
    return jnp.where(r == c, 1.0 / group, 0.0).astype(F32)


def _softplus(x):
    return jnp.maximum(x, 0.0) + jnp.log(1.0 + jnp.exp(-jnp.abs(x)))


def _my_place():
    return lax.axis_index("x"), lax.axis_index("y"), lax.axis_index("c")


def _all_gather(shards, name, in_vmem):
    nt = len(shards)

    def body(*refs):
        x_refs, out_refs = refs[:nt], refs[nt:2 * nt]
        send_sems, recv_sems, local_sems = refs[2 * nt:]
        x, y, c = _my_place()
        me, sibling = (x, y, c), (x, y, 1 - c)
        chips = [(1 - x, y), (x, 1 - y), (1 - x, 1 - y)]
        everything = []
        for t in range(nt):
            x_ref, out_ref = x_refs[t], out_refs[t]

            def blk(px, py, pc, out_ref=out_ref):
                return out_ref.at[4 * px + 2 * py + pc]

            def copy(k, block, to, src=None, t=t, blk=blk):
                return pltpu.make_async_remote_copy(
                    src_ref=blk(*block) if src is None else src, dst_ref=blk(*block),
                    send_sem=send_sems.at[7 * t + k], recv_sem=recv_sems.at[7 * t + k], device_id=to, device_id_type=MESH)

            mine = pltpu.make_async_copy(x_ref, blk(*me), local_sems.at[t])
            mine.start()
            first = [copy(0, me, sibling, src=x_ref)]
            first += [copy(1 + j, me, (*chip, c), src=x_ref) for j, chip in enumerate(chips)]
            for cp in first:
                cp.start()
            everything.append((copy, mine, first))
        sends = []
        for copy, mine, first in everything:
            passed = [copy(4 + j, (*chip, c), sibling) for j, chip in enumerate(chips)]
            for j, chip in enumerate(chips):
                copy(1 + j, (*chip, c), me).wait_recv()
                passed[j].start()
            sends += first + passed
        for copy, mine, first in everything:
            copy(0, sibling, me).wait_recv()
            for j, chip in enumerate(chips):
                copy(4 + j, (*chip, 1 - c), me).wait_recv()
        for cp in sends:
            cp.wait_send()
        for copy, mine, first in everything:
            mine.wait()

    space = pltpu.VMEM if in_vmem else pl.ANY
    return pl.pallas_call(
        body, name=name,
        out_shape=[jax.ShapeDtypeStruct((N_DEV,) + s.shape, s.dtype) for s in shards],
        in_specs=[pl.BlockSpec(memory_space=space)] * nt,
        out_specs=[pl.BlockSpec(memory_space=space)] * nt,
        scratch_shapes=[pltpu.SemaphoreType.DMA((7 * nt,)), pltpu.SemaphoreType.DMA((7 * nt,)),
                        pltpu.SemaphoreType.DMA((nt,))],
        compiler_params=pltpu.CompilerParams(vmem_limit_bytes=VMEM_LIMIT),
    )(*shards)


def _rs_sibling(gs):
    nt = len(gs)

    def body(*refs):
        g_refs, recv_refs = refs[:nt], refs[nt:2 * nt]
        send_sems, recv_sems = refs[2 * nt:]
        x, y, c = _my_place()
        copies = [pltpu.make_async_remote_copy(
            src_ref=g_refs[t].at[2 * j + (1 - c)], dst_ref=recv_refs[t].at[j],
            send_sem=send_sems.at[4 * t + j], recv_sem=recv_sems.at[4 * t + j],
            device_id=(x, y, 1 - c), device_id_type=MESH) for t in range(nt) for j in range(4)]
        for cp in copies:
            cp.start()
        for cp in copies:
            cp.wait()

    return pl.pallas_call(
        body, name="rs_sibling",
        out_shape=[jax.ShapeDtypeStruct((4,) + g.shape[1:], g.dtype) for g in gs],
        in_specs=[pl.BlockSpec(memory_space=pl.ANY)] * nt, out_specs=[pl.BlockSpec(memory_space=pl.ANY)] * nt,
        scratch_shapes=[pltpu.SemaphoreType.DMA((4 * nt,)), pltpu.SemaphoreType.DMA((4 * nt,))],
    )(*gs)


def _rs_chips(pbs):
    nt = len(pbs)

    def body(*refs):
        p_refs, recv_refs = refs[:nt], refs[nt:2 * nt]
        send_sems, recv_sems = refs[2 * nt:]
        x, y, c = _my_place()
        chips = [(1 - x, y), (x, 1 - y), (1 - x, 1 - y)]
        copies = [pltpu.make_async_remote_copy(
            src_ref=p_refs[t].at[2 * px + py], dst_ref=recv_refs[t].at[s],
            send_sem=send_sems.at[3 * t + s], recv_sem=recv_sems.at[3 * t + s],
            device_id=(px, py, c), device_id_type=MESH) for t in range(nt) for s, (px, py) in enumerate(chips)]
        for cp in copies:
            cp.start()
        for cp in copies:
            cp.wait()

    return pl.pallas_call(
        body, name="rs_chips",
        out_shape=[jax.ShapeDtypeStruct((3,) + p.shape[1:], p.dtype) for p in pbs],
        in_specs=[pl.BlockSpec(memory_space=pl.ANY)] * nt, out_specs=[pl.BlockSpec(memory_space=pl.ANY)] * nt,
        scratch_shapes=[pltpu.SemaphoreType.DMA((3 * nt,)), pltpu.SemaphoreType.DMA((3 * nt,))],
    )(*pbs)


def _rs_add_pairs(g, recv, my_c, name):
    _, nl, r, n = g.shape
    tr = _tile(r, 512)

    def body(c_ref, g_ref, r_ref, pf_ref, pb_ref):
        s = g_ref[...] + r_ref[...]
        pf_ref[...] = s
        pb_ref[...] = s.astype(MXU)

    spec_j = pl.BlockSpec((None, None, tr, n), lambda j, l, i, c_ref: (j, l, i, 0))
    return pl.pallas_call(
        body, name=name,
        grid_spec=pltpu.PrefetchScalarGridSpec(
            num_scalar_prefetch=1, grid=(4, nl, r // tr),
            in_specs=[pl.BlockSpec((None, None, tr, n), lambda j, l, i, c_ref: (2 * j + c_ref[0], l, i, 0)), spec_j],
            out_specs=[spec_j, spec_j]),
        out_shape=[jax.ShapeDtypeStruct((4, nl, r, n), F32), jax.ShapeDtypeStruct((4, nl, r, n), MXU)],
        compiler_params=_params(3),
    )(my_c, g, recv)


def _rs_add_chips(pf, recv, my_chip, name):
    _, nl, r, n = pf.shape
    tr = _tile(r, 512)

    def body(j_ref, p_ref, r_ref, o_ref):
        s = p_ref[...]
        for t in range(3):
            s = s + r_ref[t].astype(F32)
        o_ref[...] = s

    return pl.pallas_call(
        body, name=name,
        grid_spec=pltpu.PrefetchScalarGridSpec(
            num_scalar_prefetch=1, grid=(nl, r // tr),
            in_specs=[pl.BlockSpec((None, None, tr, n), lambda l, i, j_ref: (j_ref[0], l, i, 0)),
                      pl.BlockSpec((3, None, tr, n), lambda l, i, j_ref: (0, l, i, 0))],
            out_specs=pl.BlockSpec((None, tr, n), lambda l, i, j_ref: (l, i, 0))),
        out_shape=jax.ShapeDtypeStruct((nl, r, n), F32),
        compiler_params=_params(2),
    )(my_chip, pf, recv)


def _shard_windows(n_shard, count, first=0):
    out = []
    for k in range(first, first + count):
        off = n_shard * k
        a, s = off // LANES, off % LANES
        out.append((a, s, -(-(s + n_shard) // LANES) * LANES))
    return out


def _fit_lanes(x, width):
    have = x.shape[1]
    if have < width:
        return jnp.concatenate([x, jnp.zeros((x.shape[0], width - have), x.dtype)], axis=-1)
    return x[:, :width]


def _interleave_cols(g, n_shard, w_out, name):
    nd, nl, rows, wpad = g.shape
    rb = _tile(rows, 256)
    wins = _shard_windows(n_shard, nd)

    def body(g_ref, o_ref, acc):
        acc[...] = jnp.zeros_like(acc)
        for k, (a, s, win) in enumerate(wins):
            xk = _fit_lanes(g_ref[k].astype(F32), win)
            if s:
                xk = pltpu.roll(xk, s, 1)
            acc[:, a * LANES:a * LANES + win] += xk
        o_ref[...] = acc[...].astype(o_ref.dtype)

    return pl.pallas_call(
        body, name=name, grid=(nl, rows // rb),
        in_specs=[pl.BlockSpec((nd, None, rb, wpad), lambda l, i: (0, l, i, 0))],
        out_specs=pl.BlockSpec((None, rb, w_out), lambda l, i: (l, i, 0)),
        out_shape=jax.ShapeDtypeStruct((nl, rows, w_out), g.dtype),
        scratch_shapes=[pltpu.VMEM((rb, w_out), F32)],
        compiler_params=_params(2),
    )(g)


def _sum_devices(g):
    _, r, n = g.shape

    def body(g_ref, o_ref):
        s = g_ref[0]
        for t in range(1, N_DEV):
            s = s + g_ref[t]
        o_ref[...] = s

    return pl.pallas_call(
        body, name="sum_devices", out_shape=jax.ShapeDtypeStruct((r, n), F32),
        in_specs=[pl.BlockSpec(memory_space=pltpu.VMEM)], out_specs=pl.BlockSpec(memory_space=pltpu.VMEM),
        compiler_params=pltpu.CompilerParams(vmem_limit_bytes=VMEM_LIMIT),
    )(g)


def _mod_fwd(c_all, ada_w, ada_b_cols):
    nl, _, nc = ada_w.shape

    def body(c_ref, w_ref, b_ref, o_ref):
        cv = c_ref[...]
        act = (cv * _sig(cv)).astype(MXU)
        o_ref[...] = _dot(act, w_ref[...].astype(MXU), NN) + b_ref[...]

    return pl.pallas_call(
        body, name="mod_fwd", grid=(nl,),
        in_specs=[_full((16, D)), pl.BlockSpec((None, D, nc), lambda i: (i, 0, 0)),
                  pl.BlockSpec((None, 1, nc), lambda i: (i, 0, 0))],
        out_specs=pl.BlockSpec((None, 16, nc), lambda i: (i, 0, 0)),
        out_shape=jax.ShapeDtypeStruct((nl, 16, nc), F32), compiler_params=_params(1),
    )(c_all, ada_w, ada_b_cols)


def _mod_bwd(c_all, dmod_cols):
    nl, _, nc = dmod_cols.shape

    def body(c_ref, d_ref, o_ref):
        cv = c_ref[...]
        act = (cv * _sig(cv)).astype(MXU)
        o_ref[...] = _dot(act, d_ref[...].astype(MXU), TN)

    return pl.pallas_call(
        body, name="mod_bwd", grid=(nl,),
        in_specs=[_full((16, D)), pl.BlockSpec((None, 16, nc), lambda i: (i, 0, 0))],
        out_specs=pl.BlockSpec((None, D, nc), lambda i: (i, 0, 0)),
        out_shape=jax.ShapeDtypeStruct((nl, D, nc), F32), compiler_params=_params(1),
    )(c_all, dmod_cols)


def _in_proj(x, modrows, vec, w_in):
    L = x.shape[0]
    T = _tile(L, 256)

    def body(x_ref, mod_ref, vec_ref, w_ref, p_ref, h_ref):
        n, _ = _rms(x_ref[...])
        h = n * vec_ref[0:1, :] * (1.0 + mod_ref[1:2, :]) + mod_ref[0:1, :]
        hb = h.astype(MXU)
        h_ref[...] = hb
        p_ref[...] = _dot(hb, w_ref[...], NN)

    return pl.pallas_call(
        body, name="in_proj", grid=(L // T,),
        in_specs=[pl.BlockSpec((T, D), lambda i: (i, 0)), _full((SUB, D)), _full((SUB, D)), _full((D, P_PAD))],
        out_specs=[pl.BlockSpec((T, P_PAD), lambda i: (i, 0)), pl.BlockSpec((T, D), lambda i: (i, 0))],
        out_shape=[jax.ShapeDtypeStruct((L, P_PAD), F32), jax.ShapeDtypeStruct((L, D), MXU)],
        compiler_params=_params(1),
    )(x, modrows, vec, w_in)


def _gate_small(s, sp_ref):
    lane = lax.broadcasted_iota(jnp.int32, s.shape, 1)
    a = -jnp.exp(sp_ref[0:1, :])
    xb = s + sp_ref[1:2, :]
    beta = _sig(s)
    g = a * _softplus(xb)
    return lane, a, xb, beta, g


def _pre_fwd(p, pa, cq, sp):
    L = p.shape[0]
    T = _tile(L, 256)
    scale = HD ** -0.5

    def body(pm_ref, ps_ref, pa_ref, cq_ref, sp_ref, qn_ref, kn_ref, vs_ref, gb_ref, ya_ref, u_carry, q_carry):
        @pl.when(pl.program_id(0) == 0)
        def _():
            u_carry[...] = jnp.zeros_like(u_carry)
            q_carry[...] = jnp.zeros_like(q_carry)

        a_b = pm_ref[:, 0:AW]
        u = pm_ref[:, AW:2 * AW] * pm_ref[:, 2 * AW:3 * AW]
        cu, _ = _conv_fwd(u, pa_ref, 3, u_carry[...])
        u_carry[...] = u[T - SUB:T, :]
        yp = a_b * cu
        ms = _dot(yp * yp, _blockdiag_mean(AW, A_GROUP), NN, HI)
        ya_ref[...] = (yp * lax.rsqrt(ms + EPS) * pa_ref[3:4, :]).astype(MXU)

        qkv = pm_ref[:, 3 * AW:3 * AW + 3 * H * HD]
        qc, _ = _conv_fwd(qkv, cq_ref, 4, q_carry[...])
        q_carry[...] = qkv[T - SUB:T, :]
        qs = qc * _sig(qc)
        for h in range(H):
            q = qs[:, h * HD:(h + 1) * HD]
            qn_ref[:, h * HD:(h + 1) * HD] = q * (lax.rsqrt(jnp.sum(q * q, axis=-1, keepdims=True) + EPS) * scale)
            k = qs[:, (H + h) * HD:(H + h + 1) * HD]
            kn_ref[:, h * HD:(h + 1) * HD] = k * lax.rsqrt(jnp.sum(k * k, axis=-1, keepdims=True) + EPS)
        vs_ref[...] = qs[:, 2 * H * HD:3 * H * HD]

        lane, _, _, beta, g = _gate_small(ps_ref[...], sp_ref)
        gb_ref[...] = jnp.where(lane < H, beta, jnp.where(lane < 2 * H, g, 0.0))

    w3 = 3 * AW + 3 * H * HD
    row = lambda i: (i, 0)
    return pl.pallas_call(
        body, name="pre_fwd", grid=(L // T,),
        in_specs=[pl.BlockSpec((T, w3), row), pl.BlockSpec((T, LANES), lambda i: (i, (P_PAD - LANES) // LANES)),
                  _full((SUB, AW)), _full((SUB, 3 * H * HD)), _full((SUB, LANES))],
        out_specs=[pl.BlockSpec((T, H * HD), row)] * 3 + [pl.BlockSpec((T, LANES), row), pl.BlockSpec((T, AW), row)],
        out_shape=[jax.ShapeDtypeStruct((L, H * HD), F32)] * 3
        + [jax.ShapeDtypeStruct((L, LANES), F32), jax.ShapeDtypeStruct((L, AW), MXU)],
        scratch_shapes=[pltpu.VMEM((SUB, AW), F32), pltpu.VMEM((SUB, 3 * H * HD), F32)],
        compiler_params=_params(1),
    )(p, p, pa, cq, sp)


def _gdr_masks():
    r = lax.broadcasted_iota(jnp.int32, (CK, CK), 0)
    c = lax.broadcasted_iota(jnp.int32, (CK, CK), 1)
    return r >= c, r > c


def _head_cols(gbt, h):
    return gbt[:, h:h + 1], gbt[:, H + h:H + h + 1]


def _split(x, parts):
    out = []
    for _ in range(parts):
        hi = x.astype(jnp.bfloat16)
        out.append(hi)
        x = x - hi.astype(F32)
    return out


def _dot_f32(a, b, dims, exact=None):
    if exact == "a":
        ab = a.astype(jnp.bfloat16)
        return sum(_dot(ab, t, dims) for t in _split(b, 3))
    if exact == "b":
        bb = b.astype(jnp.bfloat16)
        return sum(_dot(t, bb, dims) for t in _split(a, 3))
    ah, al = _split(a, 2)
    bh, bl = _split(b, 2)
    return _dot(ah, bh, dims) + _dot(ah, bl, dims) + _dot(al, bh, dims)


def _gdr_consts():
    causal, strict = _gdr_masks()
    return dict(causal=causal, strict=strict, tril=jnp.where(causal, 1.0, 0.0).astype(F32),
                eye=jnp.where(causal & jnp.logical_not(strict), 1.0, 0.0).astype(F32),
                bcast=jnp.full((CK, HD), 1.0 / HD, F32))


def _dots(a, b, dims):
    return [_dot(x, y, dims) for x, y in zip(a, b)]


def _dots_f32(a, b, dims, exact=None):
    n = len(a)
    if exact == "a":
        lhs = [[x.astype(jnp.bfloat16)] * 3 for x in a]
        rhs = [_split(y, 3) for y in b]
    elif exact == "b":
        lhs = [_split(x, 3) for x in a]
        rhs = [[y.astype(jnp.bfloat16)] * 3 for y in b]
    else:
        sa = [_split(x, 2) for x in a]
        sb = [_split(y, 2) for y in b]
        lhs = [[s[0], s[0], s[1]] for s in sa]
        rhs = [[s[0], s[1], s[0]] for s in sb]
    terms = [[_dot(lhs[i][t], rhs[i][t], dims) for i in range(n)] for t in range(3)]
    return [terms[0][i] + terms[1][i] + terms[2][i] for i in range(n)]


def _gdr_local(q, k, v, beta, g, cst):
    n = len(q)
    R = range(n)
    causal, strict = cst["causal"], cst["strict"]
    gc = _dots_f32([cst["tril"]] * n, [jnp.broadcast_to(g[i], (CK, HD)) for i in R], NN, exact="a")
    g_row = _dots_f32([cst["bcast"]] * n, gc, NT, exact="a")
    decay = [jnp.where(causal, jnp.exp(jnp.where(causal, gc[i][:, 0:CK] - g_row[i], 0.0)), 0.0) for i in R]
    eg = [jnp.exp(gc[i]) for i in R]
    gl = [gc[i][CK - 1:CK, :] for i in R]
    ek = [jnp.exp(gl[i] - gc[i]) for i in R]
    cd = [jnp.exp(gl[i]) for i in R]
    kb = [k[i] * beta[i] for i in R]
    pk = _dots(kb, k, NT)
    xp = [-jnp.where(strict, pk[i] * decay[i], 0.0) for i in R]
    tinv = [cst["eye"] + xp[i] for i in R]
    for _ in range(5):
        xp = _dots_f32(xp, xp, NN)
        tx = _dots_f32(tinv, xp, NN)
        tinv = [tinv[i] + tx[i] for i in R]
    u = _dots(tinv, [v[i] * beta[i] for i in R], NN)
    w = _dots(tinv, [kb[i] * eg[i] for i in R], NN)
    qk = _dots(q, k, NT)
    intra = [jnp.where(causal, qk[i] * decay[i], 0.0) for i in R]
    return dict(decay=decay, eg=eg, ek=ek, cd=cd, kb=kb, pk=pk, tinv=tinv, u=u, w=w, qk=qk, intra=intra,
                q_dec=[q[i] * eg[i] for i in R], k_dec=[k[i] * ek[i] for i in R])


GDR_SUB = 2


def _gdr_fwd(qn, kn, vs, gb):
    L = qn.shape[0]
    nc = L // CK
    cb = min(8, nc)
    rb = cb * CK
    nb = nc // cb
    nsub = GDR_SUB if cb % GDR_SUB == 0 else 1

    def body(q_ref, k_ref, v_ref, gb_ref, o_ref, st_ref, s_ref):
        @pl.when(pl.program_id(0) == 0)
        def _():
            s_ref[...] = jnp.zeros_like(s_ref)

        cst = _gdr_consts()
        heads = range(H)

        def group(gi, carry):
            rows = [pl.ds(pl.multiple_of((gi * nsub + j) * CK, CK), CK) for j in range(nsub)]
            chains = [(j, h) for j in range(nsub) for h in heads]
            gbt = [gb_ref[rows[j], :] for j in range(nsub)]
            cols = lambda h: slice(h * HD, (h + 1) * HD)
            t = _gdr_local([q_ref[rows[j], cols(h)] for j, h in chains], [k_ref[rows[j], cols(h)] for j, h in chains],
                           [v_ref[rows[j], cols(h)] for j, h in chains],
                           [_head_cols(gbt[j], h)[0] for j, h in chains], [_head_cols(gbt[j], h)[1] for j, h in chains], cst)
            s = [s_ref[h] for h in heads]
            for j in range(nsub):
                at = lambda key: [t[key][j * H + h] for h in heads]
                for h in heads:
                    st_ref[h, gi * nsub + j] = s[h]
                ws = _dots(at("w"), s, NN)
                v_new = [u_h - ws_h for u_h, ws_h in zip(at("u"), ws)]
                o_s = _dots(at("q_dec"), s, NN)
                o_v = _dots(at("intra"), v_new, NN)
                kv = _dots(at("k_dec"), v_new, TN)
                cd = at("cd")
                for h in heads:
                    o_ref[rows[j], cols(h)] = o_s[h] + o_v[h]
                s = [s[h] * cd[h] + kv[h] for h in heads]
            for h in heads:
                s_ref[h] = s[h]
            return carry

        lax.fori_loop(0, cb // nsub, group, 0)

    blk = pl.BlockSpec((rb, H * HD), lambda b: (b, 0))
    return pl.pallas_call(
        body, name="gdr_fwd", grid=(nb,),
        in_specs=[blk, blk, blk, pl.BlockSpec((rb, LANES), lambda b: (b, 0))],
        out_specs=[blk, pl.BlockSpec((H, cb, HD, HD), lambda b: (0, b, 0, 0))],
        out_shape=[jax.ShapeDtypeStruct((L, H * HD), F32), jax.ShapeDtypeStruct((H, nc, HD, HD), F32)],
        scratch_shapes=[pltpu.VMEM((H, HD, HD), F32)],
        compiler_params=_params(1),
    )(qn, kn, vs, gb)


def _gdr_bwd(qn, kn, vs, gb, states, do):
    L = qn.shape[0]
    nc = L // CK
    cb = min(8, nc)
    rb = cb * CK
    nb = nc // cb
    nsub = GDR_SUB if cb % GDR_SUB == 0 else 1

    def body(q_ref, k_ref, v_ref, gb_ref, st_ref, do_ref, dq_ref, dk_ref, dv_ref, dgb_ref, ds_ref):
        @pl.when(pl.program_id(0) == 0)
        def _():
            ds_ref[...] = jnp.zeros_like(ds_ref)

        cst = _gdr_consts()
        causal, strict = cst["causal"], cst["strict"]
        ones = jnp.ones((CK, HD), F32)
        row = lax.broadcasted_iota(jnp.int32, (CK, HD), 0)
        lane = lax.broadcasted_iota(jnp.int32, (CK, LANES), 1)

        heads = range(H)
        rsum = lambda x: jnp.sum(x, axis=-1, keepdims=True)

        def group(gj, carry):
            gi = cb // nsub - 1 - gj
            rows = [pl.ds(pl.multiple_of((gi * nsub + j) * CK, CK), CK) for j in range(nsub)]
            chains = [(j, h) for j in range(nsub) for h in heads]
            gbt = [gb_ref[rows[j], :] for j in range(nsub)]
            cols = lambda h: slice(h * HD, (h + 1) * HD)
            q_all = [q_ref[rows[j], cols(h)] for j, h in chains]
            k_all = [k_ref[rows[j], cols(h)] for j, h in chains]
            v_all = [v_ref[rows[j], cols(h)] for j, h in chains]
            beta_all = [_head_cols(gbt[j], h)[0] for j, h in chains]
            t = _gdr_local(q_all, k_all, v_all, beta_all, [_head_cols(gbt[j], h)[1] for j, h in chains], cst)
            ds_out = [ds_ref[h] for h in heads]
            for j in reversed(range(nsub)):
                at = lambda key: [t[key][j * H + h] for h in heads]
                pick = lambda lst: [lst[j * H + h] for h in heads]
                q, k, v, beta = pick(q_all), pick(k_all), pick(v_all), pick(beta_all)
                u, w, tinv, decay = at("u"), at("w"), at("tinv"), at("decay")
                eg, ek, cd, kb = at("eg"), at("ek"), at("cd"), at("kb")
                q_dec, k_dec, intra, pk, qk = at("q_dec"), at("k_dec"), at("intra"), at("pk"), at("qk")
                s = [st_ref[h, gi * nsub + j] for h in heads]
                dout = [do_ref[rows[j], cols(h)] for h in heads]

                ws = _dots(w, s, NN)
                v_new = [u[h] - ws[h] for h in heads]
                dq_dec = _dots(dout, s, NT)
                qd = _dots(q_dec, dout, TN)
                di = _dots(dout, v_new, NT)
                dintra = [jnp.where(causal, di[h], 0.0) for h in heads]
                ido = _dots(intra, dout, TN)
                kds = _dots(k_dec, ds_out, NN)
                dv_new = [ido[h] + kds[h] for h in heads]
                dk_dec = _dots(v_new, ds_out, NT)
                dcd = [jnp.sum(jnp.sum(ds_out[h] * s[h], axis=1, keepdims=True), axis=0, keepdims=True) for h in heads]
                dvs = _dots(dv_new, s, NT)
                dw = [-dvs[h] for h in heads]
                wdv = _dots(w, dv_new, TN)
                ds_new = [qd[h] + ds_out[h] * cd[h] - wdv[h] for h in heads]
                dru = _dots(tinv, dv_new, TN)
                drw = _dots(tinv, dw, TN)
                dl1 = _dots(dru, u, NT)
                dl2 = _dots(drw, w, NT)
                dlower = [-jnp.where(strict, dl1[h] + dl2[h], 0.0) for h in heads]
                dv = [dru[h] * beta[h] for h in heads]
                dbeta = [rsum(dru[h] * v[h]) for h in heads]
                dgc = [rsum(drw[h] * kb[h]) * eg[h] for h in heads]
                dpk = [dlower[h] * decay[h] for h in heads]
                dqk = [dintra[h] * decay[h] for h in heads]
                dpk_k = _dots(dpk, k, NN)
                dkb = [drw[h] * eg[h] + dpk_k[h] for h in heads]
                dk1 = _dots(dpk, kb, TN)
                dq1 = _dots(dqk, k, NN)
                dk2 = _dots(dqk, q, TN)
                m = [(dlower[h] * pk[h] + dintra[h] * qk[h]) * decay[h] for h in heads]
                mcol = _dots_f32(m, [ones] * H, TN, exact="b")
                e = [rsum(dk_dec[h] * k_dec[h]) for h in heads]
                dgl = [jnp.sum(e[h], axis=0, keepdims=True) + dcd[h] * cd[h] for h in heads]
                dgc = [dgc[h] + rsum(m[h]) - mcol[h] + rsum(dq_dec[h] * q_dec[h]) - e[h]
                       + jnp.where(row == CK - 1, dgl[h], 0.0) for h in heads]
                dg = _dots_f32([cst["tril"]] * H, dgc, TN, exact="a")
                dgb = jnp.zeros((CK, LANES), F32)
                for h in heads:
                    dq_ref[rows[j], cols(h)] = dq1[h] + dq_dec[h] * eg[h]
                    dk_ref[rows[j], cols(h)] = dk1[h] + dk2[h] + dk_dec[h] * ek[h] + dkb[h] * beta[h]
                    dv_ref[rows[j], cols(h)] = dv[h]
                    db = dbeta[h] + rsum(dkb[h] * k[h])
                    dgb = dgb + jnp.where(lane == h, db, 0.0) + jnp.where(lane == H + h, dg[h], 0.0)
                dgb_ref[rows[j], :] = dgb
                ds_out = ds_new
            for h in heads:
                ds_ref[h] = ds_out[h]
            return carry

        lax.fori_loop(0, cb // nsub, group, 0)

    blk = pl.BlockSpec((rb, H * HD), lambda b: (nb - 1 - b, 0))
    sblk = pl.BlockSpec((rb, LANES), lambda b: (nb - 1 - b, 0))
    return pl.pallas_call(
        body, name="gdr_bwd", grid=(nb,),
        in_specs=[blk, blk, blk, sblk, pl.BlockSpec((H, cb, HD, HD), lambda b: (0, nb - 1 - b, 0, 0)), blk],
        out_specs=[blk, blk, blk, sblk],
        out_shape=[jax.ShapeDtypeStruct((L, H * HD), F32)] * 3 + [jax.ShapeDtypeStruct((L, LANES), F32)],
        scratch_shapes=[pltpu.VMEM((H, HD, HD), F32)],
        compiler_params=_params(1),
    )(qn, kn, vs, gb, states, do)


def _post_fwd(o, p, ya, x, modrows, sp, w_out):
    L = x.shape[0]
    T = _tile(L, 256)

    def body(o_ref, z_ref, ya_ref, x_ref, mod_ref, sp_ref, w_ref, y_ref, x2_ref, yb_ref):
        ndw = sp_ref[2:3, :]
        z = z_ref[...]
        sz = z * _sig(z)
        parts = []
        for h in range(H):
            n, _ = _rms(o_ref[:, h * HD:(h + 1) * HD])
            parts.append(n * ndw * sz[:, h * HD:(h + 1) * HD])
        yb = jnp.concatenate(parts, axis=-1).astype(MXU)
        yb_ref[...] = yb
        y = _dot(ya_ref[...], w_ref[0:AW, :], NN) + _dot(yb, w_ref[AW:2 * AW, :], NN)
        y_ref[...] = y
        x2_ref[...] = x_ref[...] + mod_ref[2:3, :] * y

    row = lambda i: (i, 0)
    zcol = (3 * AW + 3 * H * HD) // (H * HD)
    return pl.pallas_call(
        body, name="post_fwd", grid=(L // T,),
        in_specs=[pl.BlockSpec((T, H * HD), row), pl.BlockSpec((T, H * HD), lambda i: (i, zcol)),
                  pl.BlockSpec((T, AW), row), pl.BlockSpec((T, D), row), _full((SUB, D)), _full((SUB, LANES)),
                  _full((D, D))],
        out_specs=[pl.BlockSpec((T, D), row), pl.BlockSpec((T, D), row), pl.BlockSpec((T, H * HD), row)],
        out_shape=[jax.ShapeDtypeStruct((L, D), F32), jax.ShapeDtypeStruct((L, D), F32),
                   jax.ShapeDtypeStruct((L, H * HD), MXU)],
        compiler_params=_params(1),
    )(o, p, ya, x, modrows, sp, w_out)


FF_COLS = 11
FF_CW = DFF // FF_COLS
FF_ROWS = 512


def _ffn_fwd(x2, modrows, vec, w_up, cff, w_down):
    L = x2.shape[0]
    T = _tile(L, FF_ROWS)
    nj = FF_COLS

    def body(x_ref, mod_ref, vec_ref, wg_ref, wu_ref, cg_ref, cu_ref, wd_ref,
             h_ref, gp_ref, up_ref, f_ref, d_ref, x3_ref, h_s, acc, carry_g, carry_u):
        i, j = pl.program_id(0), pl.program_id(1)

        @pl.when(i == 0)
        def _():
            carry_g[j] = jnp.zeros((SUB, FF_CW), F32)
            carry_u[j] = jnp.zeros((SUB, FF_CW), F32)

        @pl.when(j == 0)
        def _():
            n, _ = _rms(x_ref[...])
            hb = (n * vec_ref[1:2, :] * (1.0 + mod_ref[4:5, :]) + mod_ref[3:4, :]).astype(MXU)
            h_s[...] = hb
            h_ref[...] = hb
            acc[...] = jnp.zeros_like(acc)

        hb = h_s[...]
        g = _dot(hb, wg_ref[...], NN)
        u = _dot(hb, wu_ref[...], NN)
        gp_ref[...] = g
        up_ref[...] = u
        gc, _ = _conv_fwd(g, cg_ref, 3, carry_g[j])
        uc, _ = _conv_fwd(u, cu_ref, 3, carry_u[j])
        carry_g[j] = g[T - SUB:T, :]
        carry_u[j] = u[T - SUB:T, :]
        fb = (gc * _sig(gc) * uc).astype(MXU)
        f_ref[...] = fb
        acc[...] += _dot(fb, wd_ref[...], NN)

        @pl.when(j == nj - 1)
        def _():
            dv = acc[...]
            d_ref[...] = dv
            x3_ref[...] = x_ref[...] + mod_ref[5:6, :] * dv

    row = lambda i, j: (i, 0)
    col = lambda i, j: (i, j)
    return pl.pallas_call(
        body, name="ffn_fwd", grid=(L // T, nj),
        in_specs=[pl.BlockSpec((T, D), row), _full((SUB, D)), _full((SUB, D)),
                  pl.BlockSpec((D, FF_CW), lambda i, j: (0, j)), pl.BlockSpec((D, FF_CW), lambda i, j: (0, nj + j)),
                  pl.BlockSpec((SUB, FF_CW), lambda i, j: (0, j)), pl.BlockSpec((SUB, FF_CW), lambda i, j: (0, nj + j)),
                  pl.BlockSpec((FF_CW, D), lambda i, j: (j, 0))],
        out_specs=[pl.BlockSpec((T, D), row), pl.BlockSpec((T, FF_CW), col), pl.BlockSpec((T, FF_CW), col),
                   pl.BlockSpec((T, FF_CW), col), pl.BlockSpec((T, D), row), pl.BlockSpec((T, D), row)],
        out_shape=[jax.ShapeDtypeStruct((L, D), MXU), jax.ShapeDtypeStruct((L, DFF), F32),
                   jax.ShapeDtypeStruct((L, DFF), F32), jax.ShapeDtypeStruct((L, DFF), MXU),
                   jax.ShapeDtypeStruct((L, D), F32), jax.ShapeDtypeStruct((L, D), F32)],
        scratch_shapes=[pltpu.VMEM((T, D), MXU), pltpu.VMEM((T, D), F32),
                        pltpu.VMEM((nj, SUB, FF_CW), F32), pltpu.VMEM((nj, SUB, FF_CW), F32)],
        compiler_params=_params(2),
    )(x2, modrows, vec, w_up, w_up, cff, cff, w_down)


def _final(x, target, nf):
    L = x.shape[0]
    T = _tile(L, 256)

    def body(x_ref, t_ref, nf_ref, dx_ref, acc_ref):
        @pl.when(pl.program_id(0) == 0)
        def _():
            acc_ref[...] = jnp.zeros_like(acc_ref)

        n, r = _rms(x_ref[...])
        w = nf_ref[0:1, :]
        err = n * w - t_ref[...]
        acc_ref[0:1, :] += (0.5 / D) * _sum0(err * err)
        dy = err * (1.0 / D)
        acc_ref[1:2, :] += _sum0(dy * n)
        dx_ref[...] = _rms_bwd(dy * w, n, r)

    row = lambda i: (i, 0)
    return pl.pallas_call(
        body, name="final_norm_loss", grid=(L // T,),
        in_specs=[pl.BlockSpec((T, D), row), pl.BlockSpec((T, D), row), _full((SUB, D))],
        out_specs=[pl.BlockSpec((T, D), row), _full((SUB, D))],
        out_shape=[jax.ShapeDtypeStruct((L, D), F32), jax.ShapeDtypeStruct((SUB, D), F32)],
        compiler_params=_params(1),
    )(x, target, nf)


def _ffn_bwd(dx3, d, x2, modrows, vec, gpre, upre, cff, w_down, w_up):
    L = dx3.shape[0]
    T = _tile(L, FF_ROWS)
    ni, nj = L // T, FF_COLS
    hb_per_t = T // SUB

    def body(dx3_ref, d_ref, x2_ref, mod_ref, vec_ref, gp_ref, up_ref, gph_ref, uph_ref, cg_ref, cu_ref,
             wd_ref, wg_ref, wu_ref,
             dd_ref, dgp_ref, dup_ref, dx2_ref, accv_ref, dcg_ref, dcu_ref,
             dd_s, acch, carry_g, carry_u):
        i, j = pl.program_id(0), pl.program_id(1)
        ri = ni - 1 - i

        @pl.when((i == 0) & (j == 0))
        def _():
            accv_ref[...] = jnp.zeros_like(accv_ref)
            dcg_ref[...] = jnp.zeros_like(dcg_ref)
            dcu_ref[...] = jnp.zeros_like(dcu_ref)

        @pl.when(i == 0)
        def _():
            carry_g[j] = jnp.zeros((SUB, FF_CW), F32)
            carry_u[j] = jnp.zeros((SUB, FF_CW), F32)

        @pl.when(j == 0)
        def _():
            dx3v = dx3_ref[...]
            accv_ref[0:1, :] += _sum0(dx3v * d_ref[...])
            ddb = (mod_ref[5:6, :] * dx3v).astype(MXU)
            dd_s[...] = ddb
            dd_ref[...] = ddb
            acch[...] = jnp.zeros_like(acch)

        ddb = dd_s[...]
        g, u = gp_ref[...], up_ref[...]
        keep = jnp.where(ri == 0, 0.0, 1.0)
        gc, gsh = _conv_fwd(g, cg_ref, 3, gph_ref[...] * keep)
        uc, ush = _conv_fwd(u, cu_ref, 3, uph_ref[...] * keep)
        sg = _sig(gc)
        df = _dot(ddb, wd_ref[...], NT)
        duc = df * (gc * sg)
        dgc = df * uc * (sg * (1.0 + gc * (1.0 - sg)))
        for s in range(3):
            dcg_ref[j, 2 - s:3 - s, :] += _sum0(dgc * gsh[s])
            dcu_ref[j, 2 - s:3 - s, :] += _sum0(duc * ush[s])
        dg = _conv_bwd_in(dgc, cg_ref, 3, carry_g[j]).astype(MXU)
        du = _conv_bwd_in(duc, cu_ref, 3, carry_u[j]).astype(MXU)
        carry_g[j] = dgc[0:SUB, :]
        carry_u[j] = duc[0:SUB, :]
        dgp_ref[...] = dg
        dup_ref[...] = du
        acch[...] += _dot(dg, wg_ref[...], NT) + _dot(du, wu_ref[...], NT)

        @pl.when(j == nj - 1)
        def _():
            dh = acch[...]
            n, r = _rms(x2_ref[...])
            nw, sc = vec_ref[1:2, :], mod_ref[4:5, :]
            accv_ref[1:2, :] += _sum0(dh)
            accv_ref[2:3, :] += _sum0(dh * n * nw)
            accv_ref[3:4, :] += _sum0(dh * n * (1.0 + sc))
            dx2_ref[...] = _rms_bwd(dh * nw * (1.0 + sc), n, r) + dx3_ref[...]

    row = lambda i, j: (ni - 1 - i, 0)
    col = lambda i, j: (ni - 1 - i, j)
    halo = lambda i, j: (jnp.maximum((ni - 1 - i) * hb_per_t - 1, 0), j)
    return pl.pallas_call(
        body, name="ffn_bwd", grid=(ni, nj),
        in_specs=[pl.BlockSpec((T, D), row), pl.BlockSpec((T, D), row), pl.BlockSpec((T, D), row),
                  _full((SUB, D)), _full((SUB, D)),
                  pl.BlockSpec((T, FF_CW), col), pl.BlockSpec((T, FF_CW), col),
                  pl.BlockSpec((SUB, FF_CW), halo), pl.BlockSpec((SUB, FF_CW), halo),
                  pl.BlockSpec((SUB, FF_CW), lambda i, j: (0, j)), pl.BlockSpec((SUB, FF_CW), lambda i, j: (0, nj + j)),
                  pl.BlockSpec((FF_CW, D), lambda i, j: (j, 0)),
                  pl.BlockSpec((D, FF_CW), lambda i, j: (0, j)), pl.BlockSpec((D, FF_CW), lambda i, j: (0, nj + j))],
        out_specs=[pl.BlockSpec((T, D), row), pl.BlockSpec((T, FF_CW), col), pl.BlockSpec((T, FF_CW), col),
                   pl.BlockSpec((T, D), row), _full((SUB, D)), _full((nj, SUB, FF_CW)), _full((nj, SUB, FF_CW))],
        out_shape=[jax.ShapeDtypeStruct((L, D), MXU), jax.ShapeDtypeStruct((L, DFF), MXU),
                   jax.ShapeDtypeStruct((L, DFF), MXU), jax.ShapeDtypeStruct((L, D), F32),
                   jax.ShapeDtypeStruct((SUB, D), F32), jax.ShapeDtypeStruct((nj, SUB, FF_CW), F32),
                   jax.ShapeDtypeStruct((nj, SUB, FF_CW), F32)],
        scratch_shapes=[pltpu.VMEM((T, D), MXU), pltpu.VMEM((T, D), F32),
                        pltpu.VMEM((nj, SUB, FF_CW), F32), pltpu.VMEM((nj, SUB, FF_CW), F32)],
        compiler_params=_params(2),
    )(dx3, d, x2, modrows, vec, gpre, upre, gpre, upre, cff, cff, w_down, w_up, w_up)


def _post_bwd(dx2, y, o, p, modrows, sp, w_out):
    L = dx2.shape[0]
    T = _tile(L, 256)

    def body(dx2_ref, y_ref, o_ref, z_ref, mod_ref, sp_ref, w_ref, dy_ref, do_ref, dz_ref, dya_ref, accv_ref, accs_ref):
        @pl.when(pl.program_id(0) == 0)
        def _():
            accv_ref[...] = jnp.zeros_like(accv_ref)
            accs_ref[...] = jnp.zeros_like(accs_ref)

        dx2v = dx2_ref[...]
        accv_ref[0:1, :] += _sum0(dx2v * y_ref[...])
        dyb = (mod_ref[2:3, :] * dx2v).astype(MXU)
        dy_ref[...] = dyb
        dyc = _dot(dyb, w_ref[...], NT)
        dya_ref[...] = dyc[:, 0:AW]
        ndw = sp_ref[2:3, :]
        z = z_ref[...]
        sgz = _sig(z)
        dsz = sgz * (1.0 + z * (1.0 - sgz))
        dndw = jnp.zeros((1, HD), F32)
        for h in range(H):
            sl = slice(h * HD, (h + 1) * HD)
            n, r = _rms(o_ref[:, sl])
            dyh = dyc[:, AW + h * HD:AW + (h + 1) * HD]
            zh = z[:, sl]
            don = dyh * (zh * sgz[:, sl])
            dz_ref[:, sl] = dyh * (n * ndw) * dsz[:, sl]
            dndw = dndw + _sum0(don * n)
            do_ref[:, sl] = _rms_bwd(don * ndw, n, r)
        accs_ref[0:1, :] += dndw

    row = lambda i: (i, 0)
    zcol = (3 * AW + 3 * H * HD) // (H * HD)
    return pl.pallas_call(
        body, name="post_bwd", grid=(L // T,),
        in_specs=[pl.BlockSpec((T, D), row), pl.BlockSpec((T, D), row), pl.BlockSpec((T, H * HD), row),
                  pl.BlockSpec((T, H * HD), lambda i: (i, zcol)), _full((SUB, D)), _full((SUB, LANES)), _full((D, D))],
        out_specs=[pl.BlockSpec((T, D), row)] + [pl.BlockSpec((T, H * HD), row)] * 3 + [_full((SUB, D)), _full((SUB, LANES))],
        out_shape=[jax.ShapeDtypeStruct((L, D), MXU)] + [jax.ShapeDtypeStruct((L, H * HD), F32)] * 3
        + [jax.ShapeDtypeStruct((SUB, D), F32), jax.ShapeDtypeStruct((SUB, LANES), F32)],
        compiler_params=_params(1),
    )(dx2, y, o, p, modrows, sp, w_out)


def _pre_bwd(p, dqn, dkn, dvs, dya, dz, dgb, pa, cq, sp):
    L = p.shape[0]
    T = _tile(L, 256)
    ni = L // T
    scale = HD ** -0.5
    w3 = 3 * AW + 3 * H * HD
    hb_per_t = T // SUB

    def body(pm_ref, ph_ref, ps_ref, dq_ref, dk_ref, dv_ref, dya_ref, dz_ref, dgb_ref, pa_ref, cq_ref, sp_ref,
             dp_ref, dpa_ref, dcq_ref, dsp_ref, carry_u, carry_q):
        i = pl.program_id(0)
        ri = ni - 1 - i

        @pl.when(i == 0)
        def _():
            dpa_ref[...] = jnp.zeros_like(dpa_ref)
            dcq_ref[...] = jnp.zeros_like(dcq_ref)
            dsp_ref[...] = jnp.zeros_like(dsp_ref)
            carry_u[...] = jnp.zeros_like(carry_u)
            carry_q[...] = jnp.zeros_like(carry_q)

        keep = jnp.where(ri == 0, 0.0, 1.0)
        a_b, a_c, a_x = pm_ref[:, 0:AW], pm_ref[:, AW:2 * AW], pm_ref[:, 2 * AW:3 * AW]
        u = a_c * a_x
        hu = ph_ref[:, AW:2 * AW] * ph_ref[:, 2 * AW:3 * AW] * keep
        cu, ush = _conv_fwd(u, pa_ref, 3, hu)
        yp = a_b * cu
        bd = _blockdiag_mean(AW, A_GROUP)
        ra = lax.rsqrt(_dot(yp * yp, bd, NN, HI) + EPS)
        na = yp * ra
        dya = dya_ref[...]
        dpa_ref[3:4, :] += _sum0(dya * na)
        dna = dya * pa_ref[3:4, :]
        dyp = ra * (dna - na * _dot(dna * na, bd, NN, HI))
        dcu = dyp * a_b
        for s in range(3):
            dpa_ref[2 - s:3 - s, :] += _sum0(dcu * ush[s])
        du = _conv_bwd_in(dcu, pa_ref, 3, carry_u[...])
        carry_u[...] = dcu[0:SUB, :]
        dp_ref[:, 0:AW] = (dyp * cu).astype(MXU)
        dp_ref[:, AW:2 * AW] = (du * a_x).astype(MXU)
        dp_ref[:, 2 * AW:3 * AW] = (du * a_c).astype(MXU)

        qkv = pm_ref[:, 3 * AW:w3]
        qc, qsh = _conv_fwd(qkv, cq_ref, 4, ph_ref[:, 3 * AW:w3] * keep)
        sg = _sig(qc)
        qs = qc * sg
        parts = []
        for h in range(H):
            q = qs[:, h * HD:(h + 1) * HD]
            rq = lax.rsqrt(jnp.sum(q * q, axis=-1, keepdims=True) + EPS)
            parts.append(_l2_bwd(dq_ref[:, h * HD:(h + 1) * HD] * scale, q * rq, rq))
        for h in range(H):
            k = qs[:, (H + h) * HD:(H + h + 1) * HD]
            rk = lax.rsqrt(jnp.sum(k * k, axis=-1, keepdims=True) + EPS)
            parts.append(_l2_bwd(dk_ref[:, h * HD:(h + 1) * HD], k * rk, rk))
        parts.append(dv_ref[...])
        dqc = jnp.concatenate(parts, axis=-1) * (sg * (1.0 + qc * (1.0 - sg)))
        for s in range(4):
            dcq_ref[3 - s:4 - s, :] += _sum0(dqc * qsh[s])
        dp_ref[:, 3 * AW:w3] = _conv_bwd_in(dqc, cq_ref, 4, carry_q[...]).astype(MXU)
        carry_q[...] = dqc[0:SUB, :]
        dp_ref[:, w3:w3 + H * HD] = dz_ref[...].astype(MXU)

        lane, a, xb, beta, g = _gate_small(ps_ref[...], sp_ref)
        dgb = dgb_ref[...]
        dbeta = jnp.where(lane < H, dgb, 0.0)
        dg = jnp.where((lane >= H) & (lane < 2 * H), dgb, 0.0)
        dalpha = dg * a * _sig(xb)
        dsp_ref[0:1, :] += _sum0(dg * g)
        dsp_ref[1:2, :] += _sum0(dalpha)
        dp_ref[:, w3 + H * HD:P_PAD] = (dbeta * beta * (1.0 - beta) + dalpha).astype(MXU)

    row = lambda i: (ni - 1 - i, 0)
    halo = lambda i: (jnp.maximum((ni - 1 - i) * hb_per_t - 1, 0), 0)
    hrow = pl.BlockSpec((T, H * HD), row)
    return pl.pallas_call(
        body, name="pre_bwd", grid=(ni,),
        in_specs=[pl.BlockSpec((T, w3), row), pl.BlockSpec((SUB, w3), halo),
                  pl.BlockSpec((T, LANES), lambda i: (ni - 1 - i, (P_PAD - LANES) // LANES)),
                  hrow, hrow, hrow, pl.BlockSpec((T, AW), row), hrow,
                  pl.BlockSpec((T, LANES), row),
                  _full((SUB, AW)), _full((SUB, 3 * H * HD)), _full((SUB, LANES))],
        out_specs=[pl.BlockSpec((T, P_PAD), row), _full((SUB, AW)), _full((SUB, 3 * H * HD)), _full((SUB, LANES))],
        out_shape=[jax.ShapeDtypeStruct((L, P_PAD), MXU), jax.ShapeDtypeStruct((SUB, AW), F32),
                   jax.ShapeDtypeStruct((SUB, 3 * H * HD), F32), jax.ShapeDtypeStruct((SUB, LANES), F32)],
        scratch_shapes=[pltpu.VMEM((SUB, AW), F32), pltpu.VMEM((SUB, 3 * H * HD), F32)],
        compiler_params=_params(1),
    )(p, p, p, dqn, dkn, dvs, dya, dz, dgb, pa, cq, sp)


def _in_bwd(dp, w_in, x, dx2, modrows, vec):
    L = x.shape[0]
    T = _tile(L, 256)

    def body(dp_ref, w_ref, x_ref, dx2_ref, mod_ref, vec_ref, dx_ref, accv_ref):
        @pl.when(pl.program_id(0) == 0)
        def _():
            accv_ref[...] = jnp.zeros_like(accv_ref)

        dh = _dot(dp_ref[...], w_ref[...], NT)
        n, r = _rms(x_ref[...])
        nw, sc = vec_ref[0:1, :], mod_ref[1:2, :]
        accv_ref[0:1, :] += _sum0(dh)
        accv_ref[1:2, :] += _sum0(dh * n * nw)
        accv_ref[2:3, :] += _sum0(dh * n * (1.0 + sc))
        dx_ref[...] = _rms_bwd(dh * nw * (1.0 + sc), n, r) + dx2_ref[...]

    row = lambda i: (i, 0)
    return pl.pallas_call(
        body, name="in_bwd", grid=(L // T,),
        in_specs=[pl.BlockSpec((T, P_PAD), row), _full((D, P_PAD)), pl.BlockSpec((T, D), row),
                  pl.BlockSpec((T, D), row), _full((SUB, D)), _full((SUB, D))],
        out_specs=[pl.BlockSpec((T, D), row), _full((SUB, D))],
        out_shape=[jax.ShapeDtypeStruct((L, D), F32), jax.ShapeDtypeStruct((SUB, D), F32)],
        compiler_params=_params(1),
    )(dp, w_in, x, dx2, modrows, vec)


def _wgrad(a, b, tm, tn, name):
    L, m = a.shape
    n = b.shape[1]
    tl = _tile(L, 512)
    tm, tn = _tile(m, tm), _tile(n, tn)
    nl = L // tl

    def body(a_ref, b_ref, o_ref):
        @pl.when(pl.program_id(2) == 0)
        def _():
            o_ref[...] = jnp.zeros_like(o_ref)

        o_ref[...] += _dot(a_ref[...], b_ref[...], TN)

    return pl.pallas_call(
        body, name=name, grid=(m // tm, n // tn, nl),
        in_specs=[pl.BlockSpec((tl, tm), lambda i, j, l: (l, i)), pl.BlockSpec((tl, tn), lambda i, j, l: (l, j))],
        out_specs=pl.BlockSpec((tm, tn), lambda i, j, l: (i, j)),
        out_shape=jax.ShapeDtypeStruct((m, n), F32), compiler_params=_params(3),
    )(a, b)


def _wgrad_cols(a, b, tm, n_shard, wpad, count, name):
    L, m = a.shape
    n = b.shape[1]
    tl = _tile(L, 512)
    tm = _tile(m, tm)
    nl = L // tl
    wins = _shard_windows(n_shard, count)
    assert all(a_ * LANES + win <= n for a_, _, win in wins), (wins, n)

    def body(a_ref, b_ref, o_ref, acc):
        @pl.when(pl.program_id(1) == 0)
        def _():
            acc[...] = jnp.zeros_like(acc)

        acc[...] += _dot(a_ref[...], b_ref[...], TN)

        @pl.when(pl.program_id(1) == nl - 1)
        def _():
            for k, (a_, s, win) in enumerate(wins):
                xk = acc[:, a_ * LANES:a_ * LANES + win]
                if s:
                    xk = pltpu.roll(xk, win - s, 1)
                o_ref[k] = _fit_lanes(xk, wpad)

    return pl.pallas_call(
        body, name=name, grid=(m // tm, nl),
        in_specs=[pl.BlockSpec((tl, tm), lambda i, l: (l, i)), pl.BlockSpec((tl, n), lambda i, l: (l, 0))],
        out_specs=pl.BlockSpec((count, tm, wpad), lambda i, l: (0, i, 0)),
        out_shape=jax.ShapeDtypeStruct((count, m, wpad), F32),
        scratch_shapes=[pltpu.VMEM((tm, n), F32)],
        compiler_params=_params(2),
    )(a, b)


def _adamw(w, g, m, v, name):
    r, n = w.shape
    tr = _tile(r, 512)
    bc1 = 1.0 - ADAM_B1 ** ADAM_STEP
    bc2 = 1.0 - ADAM_B2 ** ADAM_STEP

    def body(w_ref, g_ref, m_ref, v_ref, d_ref, nm_ref, nv_ref):
        gv = g_ref[...]
        nm = ADAM_B1 * m_ref[...] + (1.0 - ADAM_B1) * gv
        nv = ADAM_B2 * v_ref[...] + (1.0 - ADAM_B2) * (gv * gv)
        nm_ref[...] = nm
        nv_ref[...] = nv
        d_ref[...] = -ADAM_LR * ((nm / bc1) / (jnp.sqrt(nv / bc2) + ADAM_EPS) + ADAM_WD * w_ref[...])

    spec = pl.BlockSpec((tr, n), lambda i: (i, 0))
    return pl.pallas_call(
        body, name=name, grid=(r // tr,), in_specs=[spec] * 4, out_specs=[spec] * 3,
        out_shape=[jax.ShapeDtypeStruct((r, n), F32)] * 3, compiler_params=_params(1),
    )(w, g, m, v)


def _rows8(rows, width):
    out = jnp.zeros((SUB, width), F32)
    for r, vrow in enumerate(rows):
        out = out.at[r, :vrow.shape[0]].set(vrow)
    return out


def _at_lanes(v4, start):
    return jnp.zeros((LANES,), F32).at[start:start + v4.shape[0]].set(v4)


def _pad_rows(flat, mult):
    n = flat.shape[0]
    pad = (-n) % mult
    return jnp.pad(flat, (0, pad)) if pad else flat


IN_PAD = 512
UP_PAD = 768


def _local_fwd_bwd(x, target, mod_full, small_w, full_w):
    norm1_w, norm2_w, norm_a_w, a_log, dt_bias, norm_dn_w, norm_f_w = small_w
    w_in_f, w_out_f, w_up_f, w_down_f, conv_a_f, conv_q_f, conv_f_f = full_w

    def layer_params(i):
        modrows = jnp.concatenate([mod_full[i], jnp.zeros((SUB - N_MOD, D), F32)], axis=0)
        vec = _rows8([norm1_w[i], norm2_w[i]], D)
        pa = _rows8([conv_a_f[i, 0], conv_a_f[i, 1], conv_a_f[i, 2], norm_a_w[i]], AW)
        cq = _rows8([conv_q_f[i, k] for k in range(4)], 3 * H * HD)
        sp = _rows8([_at_lanes(a_log[i], H), _at_lanes(dt_bias[i], H), norm_dn_w[i]], LANES)
        cff = _rows8([conv_f_f[i, k] for k in range(3)], 2 * DFF)
        return modrows, vec, pa, cq, sp, cff

    saved = []
    xi = x
    for i in range(DEPTH):
        modrows, vec, pa, cq, sp, cff = layer_params(i)
        p, h1 = _in_proj(xi, modrows, vec, w_in_f[i])
        qn, kn, vs, gb, ya = _pre_fwd(p, pa, cq, sp)
        o, states = _gdr_fwd(qn, kn, vs, gb)
        y, x2, yb = _post_fwd(o, p, ya, xi, modrows, sp, w_out_f[i])
        h2, gpre, upre, f, dff, x3 = _ffn_fwd(x2, modrows, vec, w_up_f[i], cff, w_down_f[i])
        saved.append(dict(x=xi, p=p, h1=h1, qn=qn, kn=kn, vs=vs, gb=gb, ya=ya, o=o, states=states, y=y, x2=x2, yb=yb,
                          h2=h2, gpre=gpre, upre=upre, f=f, d=dff))
        xi = x3

    dx, facc = _final(xi, target, _rows8([norm_f_w], D))
    loss_local = jnp.sum(facc[0])
    d_norm_f = facc[1]

    gw_in, gw_out, gw_up, gw_down = [None] * DEPTH, [None] * DEPTH, [None] * DEPTH, [None] * DEPTH
    g_small = [None] * DEPTH
    for i in reversed(range(DEPTH)):
        s = saved[i]
        modrows, vec, pa, cq, sp, cff = layer_params(i)
        dd, dgp, dup, dx2, accf, dcg, dcu = _ffn_bwd(dx, s["d"], s["x2"], modrows, vec, s["gpre"], s["upre"], cff,
                                                       w_down_f[i], w_up_f[i])
        n_up, up_pad = 2 * DFF // N_DEV, UP_PAD
        gw_up[i] = jnp.concatenate([_wgrad_cols(s["h2"], dgp, 512, n_up, up_pad, N_DEV // 2, "wgrad_up"),
                                    _wgrad_cols(s["h2"], dup, 512, n_up, up_pad, N_DEV // 2, "wgrad_up")], axis=0)
        gw_down[i] = _wgrad(s["f"], dd, DFF // 2, 1024, "wgrad_down").reshape(N_DEV, DFF // N_DEV, D)
        dy, do, dz, dya, accp, accs = _post_bwd(dx2, s["y"], s["o"], s["p"], modrows, sp, w_out_f[i])
        gw_out[i] = jnp.concatenate([_wgrad(s["ya"], dy, 512, 1024, "wgrad_out"),
                                     _wgrad(s["yb"], dy, 512, 1024, "wgrad_out")], axis=0).reshape(N_DEV, D // N_DEV, D)
        dqn, dkn, dvs, dgb = _gdr_bwd(s["qn"], s["kn"], s["vs"], s["gb"], s["states"], do)
        dp, dpa, dcq, dsp = _pre_bwd(s["p"], dqn, dkn, dvs, dya, dz, dgb, pa, cq, sp)
        gw_in[i] = _wgrad_cols(s["h1"], dp, 512, P_IN // N_DEV, IN_PAD, N_DEV, "wgrad_in")
        dx, acci = _in_bwd(dp, w_in_f[i], s["x"], dx2, modrows, vec)
        dconv_ff = jnp.concatenate([dcg.transpose(1, 0, 2).reshape(SUB, DFF), dcu.transpose(1, 0, 2).reshape(SUB, DFF)],
                                   axis=1)[0:3]
        dmod = jnp.stack([acci[0], acci[1], accp[0], accf[1], accf[2], accf[0]])
        g_small[i] = dict(norm1=acci[2], norm2=accf[3], norm_a=dpa[3], a_log=dsp[0, H:2 * H], dt_bias=dsp[1, H:2 * H],
                          norm_dn=accs[0], conv_a=dpa[0:3], conv_qkv=dcq[0:4], conv_ff=dconv_ff, dmod=dmod.reshape(-1))
    return loss_local, dx, gw_in, gw_out, gw_up, gw_down, g_small, d_norm_f


def kernel(x, c, ada_w, ada_b, norm1_w, w_in, conv_a_w, norm_a_w, conv_qkv_w, a_log, dt_bias, norm_dn_w, w_out, norm2_w, w_up, conv_ff_w, w_down, norm_f_w, loss_target, m_ada_w, m_ada_b, m_norm1_w, m_w_in, m_conv_a_w, m_norm_a_w, m_conv_qkv_w, m_a_log, m_dt_bias, m_norm_dn_w, m_w_out, m_norm2_w, m_w_up, m_conv_ff_w, m_w_down, m_norm_f_w, v_ada_w, v_ada_b, v_norm1_w, v_w_in, v_conv_a_w, v_norm_a_w, v_conv_qkv_w, v_a_log, v_dt_bias, v_norm_dn_w, v_w_out, v_norm2_w, v_w_up, v_conv_ff_w, v_w_down, v_norm_f_w):
    ax, ay, ac = lax.axis_index("x"), lax.axis_index("y"), lax.axis_index("c")
    me = 4 * ax + 2 * ay + ac
    x = x[0]
    target = loss_target[0]
    n_in, n_up = P_IN // N_DEV, 2 * DFF // N_DEV

    def lane_pad(t, width):
        return jnp.pad(t.astype(MXU), ((0, 0), (0, 0), (0, width - t.shape[-1])))

    conv_blob = _pad_rows(jnp.concatenate([t.reshape(-1) for t in (conv_a_w, conv_qkv_w, conv_ff_w)]),
                          SUB * LANES).reshape(-1, LANES)
    c_rows = jnp.zeros((SUB, D), F32).at[0].set(c[0])
    g_in, g_out, g_up, g_down, g_conv, g_c = _all_gather(
        [lane_pad(w_in, IN_PAD), w_out.astype(MXU), lane_pad(w_up, UP_PAD), w_down.astype(MXU), conv_blob, c_rows],
        "gather_weights", in_vmem=False)
    w_in_f = _interleave_cols(g_in, n_in, P_PAD, "interleave_w_in")
    w_up_f = _interleave_cols(g_up, n_up, 2 * DFF, "interleave_w_up")
    w_out_f = g_out.transpose(1, 0, 2, 3).reshape(DEPTH, D, D)
    w_down_f = g_down.transpose(1, 0, 2, 3).reshape(DEPTH, DFF, D)
    sg = g_conv.reshape(N_DEV, -1)
    o1 = conv_a_w.size
    o2 = o1 + conv_qkv_w.size
    o3 = o2 + conv_ff_w.size
    conv_a_f = sg[:, 0:o1].reshape(N_DEV, DEPTH, 3, AW // N_DEV).transpose(1, 2, 0, 3).reshape(DEPTH, 3, AW)
    conv_q_f = sg[:, o1:o2].reshape(N_DEV, DEPTH, 4, 3 * H * HD // N_DEV).transpose(1, 2, 0, 3).reshape(DEPTH, 4, 3 * H * HD)
    conv_f_f = sg[:, o2:o3].reshape(N_DEV, DEPTH, 3, n_up).transpose(1, 2, 0, 3).reshape(DEPTH, 3, 2 * DFF)

    c_all = jnp.concatenate([g_c[:, 0], jnp.zeros((16 - N_DEV, D), F32)], axis=0)
    n_ada = N_MOD * D // N_DEV
    ada_b_cols = lax.dynamic_slice_in_dim(ada_b, me * n_ada, n_ada, axis=1)[:, None, :]
    mod_sh = _mod_fwd(c_all, ada_w, ada_b_cols)
    mod_all = _all_gather([mod_sh.reshape(DEPTH * 16, n_ada)], "gather_mod", in_vmem=True)[0]
    mod_all = mod_all.reshape(N_DEV, DEPTH, 16, n_ada)
    mod_mine = lax.dynamic_index_in_dim(mod_all, me, axis=2, keepdims=False)
    mod_full = mod_mine.transpose(1, 0, 2).reshape(DEPTH, N_MOD, D)

    loss_local, dx, gw_in, gw_out, gw_up, gw_down, g_small, d_norm_f = _local_fwd_bwd(
        x, target, mod_full, (norm1_w, norm2_w, norm_a_w, a_log, dt_bias, norm_dn_w, norm_f_w),
        (w_in_f, w_out_f, w_up_f, w_down_f, conv_a_f, conv_q_f, conv_f_f))
    loss = lax.psum(loss_local, ("x", "y", "c"))
    grad_x = dx[None]

    keys = ["dmod", "norm1", "norm2", "norm_a", "a_log", "dt_bias", "norm_dn", "conv_a", "conv_qkv", "conv_ff"]
    stacked = {k: jnp.stack([g_small[i][k] for i in range(DEPTH)]) for k in keys}
    flat_parts = [stacked[k].reshape(-1) for k in keys] + [d_norm_f]
    sizes = [int(t.shape[0]) for t in flat_parts]
    sflat = _pad_rows(jnp.concatenate(flat_parts), SUB * LANES).reshape(-1, LANES)
    sall = _all_gather([sflat], "gather_small_grads", in_vmem=True)[0]
    ssum = _sum_devices(sall).reshape(-1)
    so = [0]
    for sz in sizes:
        so.append(so[-1] + sz)
    red = {k: ssum[so[n]:so[n + 1]].reshape(stacked[k].shape) for n, k in enumerate(keys)}
    g_norm_f = ssum[so[len(keys)]:so[len(keys) + 1]]
    dmod_all = sall[:, 0:sizes[0] // LANES, :].reshape(N_DEV, DEPTH, N_MOD * D)

    g_ada_b = red["dmod"].reshape(DEPTH, N_MOD * D)
    dmod_cols = lax.dynamic_slice_in_dim(dmod_all, me * n_ada, n_ada, axis=2).transpose(1, 0, 2)
    dmod_cols = jnp.concatenate([dmod_cols, jnp.zeros((DEPTH, 16 - N_DEV, n_ada), F32)], axis=1)
    g_ada_w = _mod_bwd(c_all, dmod_cols)
    g_conv_a = lax.dynamic_slice_in_dim(red["conv_a"], me * (AW // N_DEV), AW // N_DEV, axis=2)
    g_conv_qkv = lax.dynamic_slice_in_dim(red["conv_qkv"], me * (3 * H * HD // N_DEV), 3 * H * HD // N_DEV, axis=2)
    g_conv_ff = lax.dynamic_slice_in_dim(red["conv_ff"], me * n_up, n_up, axis=2)

    tags = ["w_in", "w_out", "w_up", "w_down"]
    gs = [jnp.stack(t, axis=1) for t in (gw_in, gw_out, gw_up, gw_down)]
    my_c = jnp.reshape(ac, (1,)).astype(jnp.int32)
    my_chip = jnp.reshape(2 * ax + ay, (1,)).astype(jnp.int32)
    recv1 = _rs_sibling(gs)
    pairs = [_rs_add_pairs(g, r, my_c, "rs_add_pairs_" + t) for g, r, t in zip(gs, recv1, tags)]
    recv2 = _rs_chips([pb for _, pb in pairs])
    mine = [_rs_add_chips(pf, r, my_chip, "rs_add_chips_" + t) for (pf, _), r, t in zip(pairs, recv2, tags)]
    g_w_in = mine[0][:, :, :n_in]
    g_w_out = mine[1]
    g_w_up = mine[2][:, :, :n_up]
    g_w_down = mine[3]

    grads = dict(ada_w=g_ada_w, ada_b=g_ada_b, norm1_w=red["norm1"], w_in=g_w_in, conv_a_w=g_conv_a,
                 norm_a_w=red["norm_a"], conv_qkv_w=g_conv_qkv, a_log=red["a_log"], dt_bias=red["dt_bias"],
                 norm_dn_w=red["norm_dn"], w_out=g_w_out, norm2_w=red["norm2"], w_up=g_w_up, conv_ff_w=g_conv_ff,
                 w_down=g_w_down, norm_f_w=g_norm_f)
    weights = dict(ada_w=ada_w, ada_b=ada_b, norm1_w=norm1_w, w_in=w_in, conv_a_w=conv_a_w, norm_a_w=norm_a_w,
                   conv_qkv_w=conv_qkv_w, a_log=a_log, dt_bias=dt_bias, norm_dn_w=norm_dn_w, w_out=w_out,
                   norm2_w=norm2_w, w_up=w_up, conv_ff_w=conv_ff_w, w_down=w_down, norm_f_w=norm_f_w)
    ms = dict(ada_w=m_ada_w, ada_b=m_ada_b, norm1_w=m_norm1_w, w_in=m_w_in, conv_a_w=m_conv_a_w, norm_a_w=m_norm_a_w,
              conv_qkv_w=m_conv_qkv_w, a_log=m_a_log, dt_bias=m_dt_bias, norm_dn_w=m_norm_dn_w, w_out=m_w_out,
              norm2_w=m_norm2_w, w_up=m_w_up, conv_ff_w=m_conv_ff_w, w_down=m_w_down, norm_f_w=m_norm_f_w)
    vs_ = dict(ada_w=v_ada_w, ada_b=v_ada_b, norm1_w=v_norm1_w, w_in=v_w_in, conv_a_w=v_conv_a_w, norm_a_w=v_norm_a_w,
               conv_qkv_w=v_conv_qkv_w, a_log=v_a_log, dt_bias=v_dt_bias, norm_dn_w=v_norm_dn_w, w_out=v_w_out,
               norm2_w=v_norm2_w, w_up=v_w_up, conv_ff_w=v_conv_ff_w, w_down=v_w_down, norm_f_w=v_norm_f_w)
    names = list(weights)
    big_names = ["ada_w", "w_in", "w_out", "w_up", "w_down"]
    delta, new_m, new_v = {}, {}, {}
    for n in big_names:
        shp = weights[n].shape
        two = lambda t: t.reshape(-1, shp[-1])
        dl, nm, nv = _adamw(two(weights[n]), two(grads[n]), two(ms[n]), two(vs_[n]), "adamw_" + n)
        delta[n], new_m[n], new_v[n] = dl.reshape(shp), nm.reshape(shp), nv.reshape(shp)
    small_names = [n for n in names if n not in big_names]

    def pack(dct):
        return _pad_rows(jnp.concatenate([dct[n].reshape(-1) for n in small_names]), SUB * LANES).reshape(-1, LANES)

    dl, nm, nv = _adamw(pack(weights), pack(grads), pack(ms), pack(vs_), "adamw_small")
    off = 0
    for n in small_names:
        sz, shp = weights[n].size, weights[n].shape
        delta[n] = dl.reshape(-1)[off:off + sz].reshape(shp)
        new_m[n] = nm.reshape(-1)[off:off + sz].reshape(shp)
        new_v[n] = nv.reshape(-1)[off:off + sz].reshape(shp)
        off += sz

    return (loss, grad_x, *[grads[n] for n in names], *[delta[n] for n in names],
            *[new_m[n] for n in names], *[new_v[n] for n in names])
```

```python
import functools
import math

import jax
import jax.numpy as jnp
from jax import lax
from jax.experimental import pallas as pl
from jax.experimental.pallas import tpu as pltpu

F32 = jnp.float32
MXU = jnp.bfloat16

D = 1024
DEPTH = 4
N_MOD = 6
AW = 512
A_GROUP = 64
H = 4
HD = 128
CK = 64
DFF = 2816
P_IN = 3592
P_PAD = 3712
EPS = 1e-6
N_DEV = 8
LANES = 128
SUB = 8
VMEM_LIMIT = 56 * 1024 * 1024

ADAM_LR, ADAM_B1, ADAM_B2, ADAM_EPS, ADAM_WD, ADAM_STEP = 0.001, 0.9, 0.999, 1e-08, 0.01, 10

NN = ((1,), (0,))
NT = ((1,), (1,))
TN = ((0,), (0,))
HI = lax.Precision.HIGHEST
MESH = pl.DeviceIdType.MESH


def _dot(a, b, dims, prec=None):
    if prec is None:
        a = a.astype(MXU) if a.dtype == F32 else a
        b = b.astype(MXU) if b.dtype == F32 else b
    return lax.dot_general(a, b, (dims, ((), ())), precision=prec, preferred_element_type=F32)


def _params(n_grid=0, limit=VMEM_LIMIT):
    sem = ("arbitrary",) * n_grid if n_grid else None
    return pltpu.CompilerParams(dimension_semantics=sem, vmem_limit_bytes=limit)


def _tile(n, want):
    if n <= want:
        return n
    t = want - want % SUB
    while n % t:
        t -= SUB
    assert t > 0, (n, want)
    return t


def _full(shape):
    nd = len(shape)
    return pl.BlockSpec(shape, lambda *_: (0,) * nd)


def _sig(x):
    return jax.nn.sigmoid(x)


def _rms(x):
    r = lax.rsqrt(jnp.mean(x * x, axis=-1, keepdims=True) + EPS)
    return x * r, r


def _rms_bwd(dn, n, r):
    return r * (dn - n * jnp.mean(dn * n, axis=-1, keepdims=True))


def _l2_bwd(dn, n, r):
    return r * (dn - n * jnp.sum(dn * n, axis=-1, keepdims=True))


def _sum0(x):
    return jnp.sum(x, axis=0, keepdims=True)


def _shift_down(x, s, halo):
    ext = jnp.concatenate([halo, x], axis=0)
    return pltpu.roll(ext, s, 0)[SUB:, :]


def _shift_up(x, s, halo):
    t = x.shape[0]
    ext = jnp.concatenate([x, halo], axis=0)
    return pltpu.roll(ext, t + SUB - s, 0)[:t, :]


def _conv_fwd(x, w_ref, width, halo):
    sh = [x] + [_shift_down(x, s, halo) for s in range(1, width)]
    out = w_ref[width - 1:width, :] * sh[0]
    for s in range(1, width):
        out = out + w_ref[width - 1 - s:width - s, :] * sh[s]
    return out, sh


def _conv_bwd_in(dout, w_ref, width, halo_next):
    dx = w_ref[width - 1:width, :] * dout
    for s in range(1, width):
        dx = dx + w_ref[width - 1 - s:width - s, :] * _shift_up(dout, s, halo_next)
    return dx


def _blockdiag_mean(n, group):
    r = lax.shift_right_logical(lax.broadcasted_iota(jnp.int32, (n, n), 0), int(math.log2(group)))
    c = lax.shift_right_logical(lax.broadcasted_iota(jnp.int32, (n, n), 1), int(math.log2(group)))
    return jnp.where(r == c, 1.0 / group, 0.0).astype(F32)


def _softplus(x):
    return jnp.maximum(x, 0.0) + jnp.log(1.0 + jnp.exp(-jnp.abs(x)))


def _my_place():
    return lax.axis_index("x"), lax.axis_index("y"), lax.axis_index("c")


def _all_gather(shards, name, in_vmem):
    nt = len(shards)

    def body(*refs):
        x_refs, out_refs = refs[:nt], refs[nt:2 * nt]
        send_sems, recv_sems, local_sems = refs[2 * nt:]
        x, y, c = _my_place()
        me, sibling = (x, y, c), (x, y, 1 - c)
        chips = [(1 - x, y), (x, 1 - y), (1 - x, 1 - y)]
        everything = []
        for t in range(nt):
            x_ref, out_ref = x_refs[t], out_refs[t]

            def blk(px, py, pc, out_ref=out_ref):
                return out_ref.at[4 * px + 2 * py + pc]

            def copy(k, block, to, src=None, t=t, blk=blk):
                return pltpu.make_async_remote_copy(
                    src_ref=blk(*block) if src is None else src, dst_ref=blk(*block),
                    send_sem=send_sems.at[7 * t + k], recv_sem=recv_sems.at[7 * t + k], device_id=to, device_id_type=MESH)

            mine = pltpu.make_async_copy(x_ref, blk(*me), local_sems.at[t])
            mine.start()
            first = [copy(0, me, sibling, src=x_ref)]
            first += [copy(1 + j, me, (*chip, c), src=x_ref) for j, chip in enumerate(chips)]
            for cp in first:
                cp.start()
            everything.append((copy, mine, first))
        sends = []
        for copy, mine, first in everything:
            passed = [copy(4 + j, (*chip, c), sibling) for j, chip in enumerate(chips)]
            for j, chip in enumerate(chips):
                copy(1 + j, (*chip, c), me).wait_recv()
                passed[j].start()
            sends += first + passed
        for copy, mine, first in everything:
            copy(0, sibling, me).wait_recv()
            for j, chip in enumerate(chips):
                copy(4 + j, (*chip, 1 - c), me).wait_recv()
        for cp in sends:
            cp.wait_send()
        for copy, mine, first in everything:
            mine.wait()

    space = pltpu.VMEM if in_vmem else pl.ANY
    return pl.pallas_call(
        body, name=name,
        out_shape=[jax.ShapeDtypeStruct((N_DEV,) + s.shape, s.dtype) for s in shards],
        in_specs=[pl.BlockSpec(memory_space=space)] * nt,
        out_specs=[pl.BlockSpec(memory_space=space)] * nt,
        scratch_shapes=[pltpu.SemaphoreType.DMA((7 * nt,)), pltpu.SemaphoreType.DMA((7 * nt,)),
                        pltpu.SemaphoreType.DMA((nt,))],
        compiler_params=pltpu.CompilerParams(vmem_limit_bytes=VMEM_LIMIT),
    )(*shards)


def _rs_sibling(gs):
    nt = len(gs)

    def body(*refs):
        g_refs, recv_refs = refs[:nt], refs[nt:2 * nt]
        send_sems, recv_sems = refs[2 * nt:]
        x, y, c = _my_place()
        copies = [pltpu.make_async_remote_copy(
            src_ref=g_refs[t].at[2 * j + (1 - c)], dst_ref=recv_refs[t].at[j],
            send_sem=send_sems.at[4 * t + j], recv_sem=recv_sems.at[4 * t + j],
            device_id=(x, y, 1 - c), device_id_type=MESH) for t in range(nt) for j in range(4)]
        for cp in copies:
            cp.start()
        for cp in copies:
            cp.wait()

    return pl.pallas_call(
        body, name="rs_sibling",
        out_shape=[jax.ShapeDtypeStruct((4,) + g.shape[1:], g.dtype) for g in gs],
        in_specs=[pl.BlockSpec(memory_space=pl.ANY)] * nt, out_specs=[pl.BlockSpec(memory_space=pl.ANY)] * nt,
        scratch_shapes=[pltpu.SemaphoreType.DMA((4 * nt,)), pltpu.SemaphoreType.DMA((4 * nt,))],
    )(*gs)


def _rs_chips(pbs):
    nt = len(pbs)

    def body(*refs):
        p_refs, recv_refs = refs[:nt], refs[nt:2 * nt]
        send_sems, recv_sems = refs[2 * nt:]
        x, y, c = _my_place()
        chips = [(1 - x, y), (x, 1 - y), (1 - x, 1 - y)]
        copies = [pltpu.make_async_remote_copy(
            src_ref=p_refs[t].at[2 * px + py], dst_ref=recv_refs[t].at[s],
            send_sem=send_sems.at[3 * t + s], recv_sem=recv_sems.at[3 * t + s],
            device_id=(px, py, c), device_id_type=MESH) for t in range(nt) for s, (px, py) in enumerate(chips)]
        for cp in copies:
            cp.start()
        for cp in copies:
            cp.wait()

    return pl.pallas_call(
        body, name="rs_chips",
        out_shape=[jax.ShapeDtypeStruct((3,) + p.shape[1:], p.dtype) for p in pbs],
        in_specs=[pl.BlockSpec(memory_space=pl.ANY)] * nt, out_specs=[pl.BlockSpec(memory_space=pl.ANY)] * nt,
        scratch_shapes=[pltpu.SemaphoreType.DMA((3 * nt,)), pltpu.SemaphoreType.DMA((3 * nt,))],
    )(*pbs)


def _rs_add_pairs(g, recv, my_c, name):
    _, nl, r, n = g.shape
    tr = _tile(r, 512)

    def body(c_ref, g_ref, r_ref, pf_ref, pb_ref):
        s = g_ref[...] + r_ref[...]
        pf_ref[...] = s
        pb_ref[...] = s.astype(MXU)

    spec_j = pl.BlockSpec((None, None, tr, n), lambda j, l, i, c_ref: (j, l, i, 0))
    return pl.pallas_call(
        body, name=name,
        grid_spec=pltpu.PrefetchScalarGridSpec(
            num_scalar_prefetch=1, grid=(4, nl, r // tr),
            in_specs=[pl.BlockSpec((None, None, tr, n), lambda j, l, i, c_ref: (2 * j + c_ref[0], l, i, 0)), spec_j],
            out_specs=[spec_j, spec_j]),
        out_shape=[jax.ShapeDtypeStruct((4, nl, r, n), F32), jax.ShapeDtypeStruct((4, nl, r, n), MXU)],
        compiler_params=_params(3),
    )(my_c, g, recv)


def _rs_add_chips(pf, recv, my_chip, name):
    _, nl, r, n = pf.shape
    tr = _tile(r, 512)

    def body(j_ref, p_ref, r_ref, o_ref):
        s = p_ref[...]
        for t in range(3):
            s = s + r_ref[t].astype(F32)
        o_ref[...] = s

    return pl.pallas_call(
        body, name=name,
        grid_spec=pltpu.PrefetchScalarGridSpec(
            num_scalar_prefetch=1, grid=(nl, r // tr),
            in_specs=[pl.BlockSpec((None, None, tr, n), lambda l, i, j_ref: (j_ref[0], l, i, 0)),
                      pl.BlockSpec((3, None, tr, n), lambda l, i, j_ref: (0, l, i, 0))],
            out_specs=pl.BlockSpec((None, tr, n), lambda l, i, j_ref: (l, i, 0))),
        out_shape=jax.ShapeDtypeStruct((nl, r, n), F32),
        compiler_params=_params(2),
    )(my_chip, pf, recv)


def _shard_windows(n_shard, count, first=0):
    out = []
    for k in range(first, first + count):
        off = n_shard * k
        a, s = off // LANES, off % LANES
        out.append((a, s, -(-(s + n_shard) // LANES) * LANES))
    return out


def _fit_lanes(x, width):
    have = x.shape[1]
    if have < width:
        return jnp.concatenate([x, jnp.zeros((x.shape[0], width - have), x.dtype)], axis=-1)
    return x[:, :width]


def _interleave_cols(g, n_shard, w_out, name):
    nd, nl, rows, wpad = g.shape
    rb = _tile(rows, 256)
    wins = _shard_windows(n_shard, nd)

    def body(g_ref, o_ref, acc):
        acc[...] = jnp.zeros_like(acc)
        for k, (a, s, win) in enumerate(wins):
            xk = _fit_lanes(g_ref[k].astype(F32), win)
            if s:
                xk = pltpu.roll(xk, s, 1)
            acc[:, a * LANES:a * LANES + win] += xk
        o_ref[...] = acc[...].astype(o_ref.dtype)

    return pl.pallas_call(
        body, name=name, grid=(nl, rows // rb),
        in_specs=[pl.BlockSpec((nd, None, rb, wpad), lambda l, i: (0, l, i, 0))],
        out_specs=pl.BlockSpec((None, rb, w_out), lambda l, i: (l, i, 0)),
        out_shape=jax.ShapeDtypeStruct((nl, rows, w_out), g.dtype),
        scratch_shapes=[pltpu.VMEM((rb, w_out), F32)],
        compiler_params=_params(2),
    )(g)


def _sum_devices(g):
    _, r, n = g.shape

    def body(g_ref, o_ref):
        s = g_ref[0]
        for t in range(1, N_DEV):
            s = s + g_ref[t]
        o_ref[...] = s

    return pl.pallas_call(
        body, name="sum_devices", out_shape=jax.ShapeDtypeStruct((r, n), F32),
        in_specs=[pl.BlockSpec(memory_space=pltpu.VMEM)], out_specs=pl.BlockSpec(memory_space=pltpu.VMEM),
        compiler_params=pltpu.CompilerParams(vmem_limit_bytes=VMEM_LIMIT),
    )(g)


def _mod_fwd(c_all, ada_w, ada_b_cols):
    nl, _, nc = ada_w.shape

    def body(c_ref, w_ref, b_ref, o_ref):
        cv = c_ref[...]
        act = (cv * _sig(cv)).astype(MXU)
        o_ref[...] = _dot(act, w_ref[...].astype(MXU), NN) + b_ref[...]

    return pl.pallas_call(
        body, name="mod_fwd", grid=(nl,),
        in_specs=[_full((16, D)), pl.BlockSpec((None, D, nc), lambda i: (i, 0, 0)),
                  pl.BlockSpec((None, 1, nc), lambda i: (i, 0, 0))],
        out_specs=pl.BlockSpec((None, 16, nc), lambda i: (i, 0, 0)),
        out_shape=jax.ShapeDtypeStruct((nl, 16, nc), F32), compiler_params=_params(1),
    )(c_all, ada_w, ada_b_cols)


def _mod_bwd(c_all, dmod_cols):
    nl, _, nc = dmod_cols.shape

    def body(c_ref, d_ref, o_ref):
        cv = c_ref[...]
        act = (cv * _sig(cv)).astype(MXU)
        o_ref[...] = _dot(act, d_ref[...].astype(MXU), TN)

    return pl.pallas_call(
        body, name="mod_bwd", grid=(nl,),
        in_specs=[_full((16, D)), pl.BlockSpec((None, 16, nc), lambda i: (i, 0, 0))],
        out_specs=pl.BlockSpec((None, D, nc), lambda i: (i, 0, 0)),
        out_shape=jax.ShapeDtypeStruct((nl, D, nc), F32), compiler_params=_params(1),
    )(c_all, dmod_cols)


def _in_proj(x, modrows, vec, w_in):
    L = x.shape[0]
    T = _tile(L, 256)

    def body(x_ref, mod_ref, vec_ref, w_ref, p_ref, h_ref):
        n, _ = _rms(x_ref[...])
        h = n * vec_ref[0:1, :] * (1.0 + mod_ref[1:2, :]) + mod_ref[0:1, :]
        hb = h.astype(MXU)
        h_ref[...] = hb
        p_ref[...] = _dot(hb, w_ref[...], NN)

    return pl.pallas_call(
        body, name="in_proj", grid=(L // T,),
        in_specs=[pl.BlockSpec((T, D), lambda i: (i, 0)), _full((SUB, D)), _full((SUB, D)), _full((D, P_PAD))],
        out_specs=[pl.BlockSpec((T, P_PAD), lambda i: (i, 0)), pl.BlockSpec((T, D), lambda i: (i, 0))],
        out_shape=[jax.ShapeDtypeStruct((L, P_PAD), F32), jax.ShapeDtypeStruct((L, D), MXU)],
        compiler_params=_params(1),
    )(x, modrows, vec, w_in)


def _gate_small(s, sp_ref):
    lane = lax.broadcasted_iota(jnp.int32, s.shape, 1)
    a = -jnp.exp(sp_ref[0:1, :])
    xb = s + sp_ref[1:2, :]
    beta = _sig(s)
    g = a * _softplus(xb)
    return lane, a, xb, beta, g


def _pre_fwd(p, pa, cq, sp):
    L = p.shape[0]
    T = _tile(L, 256)
    scale = HD ** -0.5

    def body(pm_ref, ps_ref, pa_ref, cq_ref, sp_ref, qn_ref, kn_ref, vs_ref, gb_ref, ya_ref, u_carry, q_carry):
        @pl.when(pl.program_id(0) == 0)
        def _():
            u_carry[...] = jnp.zeros_like(u_carry)
            q_carry[...] = jnp.zeros_like(q_carry)

        a_b = pm_ref[:, 0:AW]
        u = pm_ref[:, AW:2 * AW] * pm_ref[:, 2 * AW:3 * AW]
        cu, _ = _conv_fwd(u, pa_ref, 3, u_carry[...])
        u_carry[...] = u[T - SUB:T, :]
        yp = a_b * cu
        ms = _dot_f32(yp * yp, _blockdiag_mean(AW, A_GROUP), NN, exact="b")
        ya_ref[...] = (yp * lax.rsqrt(ms + EPS) * pa_ref[3:4, :]).astype(MXU)

        qkv = pm_ref[:, 3 * AW:3 * AW + 3 * H * HD]
        qc, _ = _conv_fwd(qkv, cq_ref, 4, q_carry[...])
        q_carry[...] = qkv[T - SUB:T, :]
        qs = qc * _sig(qc)
        for h in range(H):
            q = qs[:, h * HD:(h + 1) * HD]
            qn_ref[:, h * HD:(h + 1) * HD] = q * (lax.rsqrt(jnp.sum(q * q, axis=-1, keepdims=True) + EPS) * scale)
            k = qs[:, (H + h) * HD:(H + h + 1) * HD]
            kn_ref[:, h * HD:(h + 1) * HD] = k * lax.rsqrt(jnp.sum(k * k, axis=-1, keepdims=True) + EPS)
        vs_ref[...] = qs[:, 2 * H * HD:3 * H * HD]

        lane, _, _, beta, g = _gate_small(ps_ref[...], sp_ref)
        gb_ref[...] = jnp.where(lane < H, beta, jnp.where(lane < 2 * H, g, 0.0))

    w3 = 3 * AW + 3 * H * HD
    row = lambda i: (i, 0)
    return pl.pallas_call(
        body, name="pre_fwd", grid=(L // T,),
        in_specs=[pl.BlockSpec((T, w3), row), pl.BlockSpec((T, LANES), lambda i: (i, (P_PAD - LANES) // LANES)),
                  _full((SUB, AW)), _full((SUB, 3 * H * HD)), _full((SUB, LANES))],
        out_specs=[pl.BlockSpec((T, H * HD), row)] * 3 + [pl.BlockSpec((T, LANES), row), pl.BlockSpec((T, AW), row)],
        out_shape=[jax.ShapeDtypeStruct((L, H * HD), F32)] * 3
        + [jax.ShapeDtypeStruct((L, LANES), F32), jax.ShapeDtypeStruct((L, AW), MXU)],
        scratch_shapes=[pltpu.VMEM((SUB, AW), F32), pltpu.VMEM((SUB, 3 * H * HD), F32)],
        compiler_params=_params(1),
    )(p, p, pa, cq, sp)


def _gdr_masks():
    r = lax.broadcasted_iota(jnp.int32, (CK, CK), 0)
    c = lax.broadcasted_iota(jnp.int32, (CK, CK), 1)
    return r >= c, r > c


def _head_cols(gbt, h):
    return gbt[:, h:h + 1], gbt[:, H + h:H + h + 1]


def _split(x, parts):
    out = []
    for _ in range(parts):
        hi = x.astype(jnp.bfloat16)
        out.append(hi)
        x = x - hi.astype(F32)
    return out


def _dot_f32(a, b, dims, exact=None):
    if exact == "a":
        ab = a.astype(jnp.bfloat16)
        return sum(_dot(ab, t, dims) for t in _split(b, 3))
    if exact == "b":
        bb = b.astype(jnp.bfloat16)
        return sum(_dot(t, bb, dims) for t in _split(a, 3))
    ah, al = _split(a, 2)
    bh, bl = _split(b, 2)
    return _dot(ah, bh, dims) + _dot(ah, bl, dims) + _dot(al, bh, dims)


def _gdr_consts():
    causal, strict = _gdr_masks()
    return dict(causal=causal, strict=strict, tril=jnp.where(causal, 1.0, 0.0).astype(F32),
                eye=jnp.where(causal & jnp.logical_not(strict), 1.0, 0.0).astype(F32),
                bcast=jnp.full((CK, HD), 1.0 / HD, F32))


def _dots(a, b, dims):
    return [_dot(x, y, dims) for x, y in zip(a, b)]


def _dots_f32(a, b, dims, exact=None):
    n = len(a)
    if exact == "a":
        lhs = [[x.astype(jnp.bfloat16)] * 3 for x in a]
        rhs = [_split(y, 3) for y in b]
    elif exact == "b":
        lhs = [_split(x, 3) for x in a]
        rhs = [[y.astype(jnp.bfloat16)] * 3 for y in b]
    else:
        sa = [_split(x, 2) for x in a]
        sb = [_split(y, 2) for y in b]
        lhs = [[s[0], s[0], s[1]] for s in sa]
        rhs = [[s[0], s[1], s[0]] for s in sb]
    terms = [[_dot(lhs[i][t], rhs[i][t], dims) for i in range(n)] for t in range(3)]
    return [terms[0][i] + terms[1][i] + terms[2][i] for i in range(n)]


def _gdr_local(q, k, v, beta, g, cst, tinv=None):
    n = len(q)
    R = range(n)
    causal, strict = cst["causal"], cst["strict"]
    gc = _dots_f32([cst["tril"]] * n, [jnp.broadcast_to(g[i], (CK, HD)) for i in R], NN, exact="a")
    g_row = _dots_f32([cst["bcast"]] * n, gc, NT, exact="a")
    decay = [jnp.where(causal, jnp.exp(jnp.where(causal, gc[i][:, 0:CK] - g_row[i], 0.0)), 0.0) for i in R]
    eg = [jnp.exp(gc[i]) for i in R]
    gl = [gc[i][CK - 1:CK, :] for i in R]
    ek = [jnp.exp(gl[i] - gc[i]) for i in R]
    cd = [jnp.exp(gl[i]) for i in R]
    kb = [k[i] * beta[i] for i in R]
    pk = _dots(kb, k, NT)
    if tinv is None:
        xp = [-jnp.where(strict, pk[i] * decay[i], 0.0) for i in R]
        tinv = [cst["eye"] + xp[i] for i in R]
        for _ in range(5):
            xp = _dots_f32(xp, xp, NN)
            tx = _dots_f32(tinv, xp, NN)
            tinv = [tinv[i] + tx[i] for i in R]
    u = _dots(tinv, [v[i] * beta[i] for i in R], NN)
    w = _dots(tinv, [kb[i] * eg[i] for i in R], NN)
    qk = _dots(q, k, NT)
    intra = [jnp.where(causal, qk[i] * decay[i], 0.0) for i in R]
    return dict(decay=decay, eg=eg, ek=ek, cd=cd, kb=kb, pk=pk, tinv=tinv, u=u, w=w, qk=qk, intra=intra,
                q_dec=[q[i] * eg[i] for i in R], k_dec=[k[i] * ek[i] for i in R])


GDR_SUB = 4


def _gdr_fwd(qn, kn, vs, gb):
    L = qn.shape[0]
    nc = L // CK
    cb = min(8, nc)
    rb = cb * CK
    nb = nc // cb
    nsub = GDR_SUB if cb % GDR_SUB == 0 else 1

    def body(q_ref, k_ref, v_ref, gb_ref, o_ref, st_ref, ti_ref, s_ref):
        @pl.when(pl.program_id(0) == 0)
        def _():
            s_ref[...] = jnp.zeros_like(s_ref)

        cst = _gdr_consts()
        heads = range(H)

        def group(gi, carry):
            rows = [pl.ds(pl.multiple_of((gi * nsub + j) * CK, CK), CK) for j in range(nsub)]
            chains = [(j, h) for j in range(nsub) for h in heads]
            gbt = [gb_ref[rows[j], :] for j in range(nsub)]
            cols = lambda h: slice(h * HD, (h + 1) * HD)
            t = _gdr_local([q_ref[rows[j], cols(h)] for j, h in chains], [k_ref[rows[j], cols(h)] for j, h in chains],
                           [v_ref[rows[j], cols(h)] for j, h in chains],
                           [_head_cols(gbt[j], h)[0] for j, h in chains], [_head_cols(gbt[j], h)[1] for j, h in chains], cst)
            s = [s_ref[h] for h in heads]
            for j in range(nsub):
                at = lambda key: [t[key][j * H + h] for h in heads]
                for h in heads:
                    st_ref[h, gi * nsub + j] = s[h]
                    ti_ref[h, gi * nsub + j] = t["tinv"][j * H + h]
                ws = _dots(at("w"), s, NN)
                v_new = [u_h - ws_h for u_h, ws_h in zip(at("u"), ws)]
                o_s = _dots(at("q_dec"), s, NN)
                o_v = _dots(at("intra"), v_new, NN)
                kv = _dots(at("k_dec"), v_new, TN)
                cd = at("cd")
                for h in heads:
                    o_ref[rows[j], cols(h)] = o_s[h] + o_v[h]
                s = [s[h] * cd[h] + kv[h] for h in heads]
            for h in heads:
                s_ref[h] = s[h]
            return carry

        lax.fori_loop(0, cb // nsub, group, 0)

    blk = pl.BlockSpec((rb, H * HD), lambda b: (b, 0))
    return pl.pallas_call(
        body, name="gdr_fwd", grid=(nb,),
        in_specs=[blk, blk, blk, pl.BlockSpec((rb, LANES), lambda b: (b, 0))],
        out_specs=[blk, pl.BlockSpec((H, cb, HD, HD), lambda b: (0, b, 0, 0)),
                   pl.BlockSpec((H, cb, CK, CK), lambda b: (0, b, 0, 0))],
        out_shape=[jax.ShapeDtypeStruct((L, H * HD), F32), jax.ShapeDtypeStruct((H, nc, HD, HD), F32),
                   jax.ShapeDtypeStruct((H, nc, CK, CK), F32)],
        scratch_shapes=[pltpu.VMEM((H, HD, HD), F32)],
        compiler_params=_params(1),
    )(qn, kn, vs, gb)


def _gdr_bwd(qn, kn, vs, gb, states, tinvs, do):
    L = qn.shape[0]
    nc = L // CK
    cb = min(8, nc)
    rb = cb * CK
    nb = nc // cb
    nsub = GDR_SUB if cb % GDR_SUB == 0 else 1

    def body(q_ref, k_ref, v_ref, gb_ref, st_ref, ti_ref, do_ref, dq_ref, dk_ref, dv_ref, dgb_ref, ds_ref):
        @pl.when(pl.program_id(0) == 0)
        def _():
            ds_ref[...] = jnp.zeros_like(ds_ref)

        cst = _gdr_consts()
        causal, strict = cst["causal"], cst["strict"]
        ones = jnp.ones((CK, HD), F32)
        row = lax.broadcasted_iota(jnp.int32, (CK, HD), 0)
        lane = lax.broadcasted_iota(jnp.int32, (CK, LANES), 1)

        heads = range(H)
        rsum = lambda x: jnp.sum(x, axis=-1, keepdims=True)

        def group(gj, carry):
            gi = cb // nsub - 1 - gj
            rows = [pl.ds(pl.multiple_of((gi * nsub + j) * CK, CK), CK) for j in range(nsub)]
            chains = [(j, h) for j in range(nsub) for h in heads]
            gbt = [gb_ref[rows[j], :] for j in range(nsub)]
            cols = lambda h: slice(h * HD, (h + 1) * HD)
            q_all = [q_ref[rows[j], cols(h)] for j, h in chains]
            k_all = [k_ref[rows[j], cols(h)] for j, h in chains]
            v_all = [v_ref[rows[j], cols(h)] for j, h in chains]
            beta_all = [_head_cols(gbt[j], h)[0] for j, h in chains]
            t = _gdr_local(q_all, k_all, v_all, beta_all, [_head_cols(gbt[j], h)[1] for j, h in chains], cst,
                           tinv=[ti_ref[h, gi * nsub + j] for j, h in chains])
            ds_out = [ds_ref[h] for h in heads]
            for j in reversed(range(nsub)):
                at = lambda key: [t[key][j * H + h] for h in heads]
                pick = lambda lst: [lst[j * H + h] for h in heads]
                q, k, v, beta = pick(q_all), pick(k_all), pick(v_all), pick(beta_all)
                u, w, tinv, decay = at("u"), at("w"), at("tinv"), at("decay")
                eg, ek, cd, kb = at("eg"), at("ek"), at("cd"), at("kb")
                q_dec, k_dec, intra, pk, qk = at("q_dec"), at("k_dec"), at("intra"), at("pk"), at("qk")
                s = [st_ref[h, gi * nsub + j] for h in heads]
                dout = [do_ref[rows[j], cols(h)] for h in heads]

                ws = _dots(w, s, NN)
                v_new = [u[h] - ws[h] for h in heads]
                dq_dec = _dots(dout, s, NT)
                qd = _dots(q_dec, dout, TN)
                di = _dots(dout, v_new, NT)
                dintra = [jnp.where(causal, di[h], 0.0) for h in heads]
                ido = _dots(intra, dout, TN)
                kds = _dots(k_dec, ds_out, NN)
                dv_new = [ido[h] + kds[h] for h in heads]
                dk_dec = _dots(v_new, ds_out, NT)
                dcd = [jnp.sum(jnp.sum(ds_out[h] * s[h], axis=1, keepdims=True), axis=0, keepdims=True) for h in heads]
                dvs = _dots(dv_new, s, NT)
                dw = [-dvs[h] for h in heads]
                wdv = _dots(w, dv_new, TN)
                ds_new = [qd[h] + ds_out[h] * cd[h] - wdv[h] for h in heads]
                dru = _dots(tinv, dv_new, TN)
                drw = _dots(tinv, dw, TN)
                dl1 = _dots(dru, u, NT)
                dl2 = _dots(drw, w, NT)
                dlower = [-jnp.where(strict, dl1[h] + dl2[h], 0.0) for h in heads]
                dv = [dru[h] * beta[h] for h in heads]
                dbeta = [rsum(dru[h] * v[h]) for h in heads]
                dgc = [rsum(drw[h] * kb[h]) * eg[h] for h in heads]
                dpk = [dlower[h] * decay[h] for h in heads]
                dqk = [dintra[h] * decay[h] for h in heads]
                dpk_k = _dots(dpk, k, NN)
                dkb = [drw[h] * eg[h] + dpk_k[h] for h in heads]
                dk1 = _dots(dpk, kb, TN)
                dq1 = _dots(dqk, k, NN)
                dk2 = _dots(dqk, q, TN)
                m = [(dlower[h] * pk[h] + dintra[h] * qk[h]) * decay[h] for h in heads]
                mcol = _dots_f32(m, [ones] * H, TN, exact="b")
                e = [rsum(dk_dec[h] * k_dec[h]) for h in heads]
                dgl = [jnp.sum(e[h], axis=0, keepdims=True) + dcd[h] * cd[h] for h in heads]
                dgc = [dgc[h] + rsum(m[h]) - mcol[h] + rsum(dq_dec[h] * q_dec[h]) - e[h]
                       + jnp.where(row == CK - 1, dgl[h], 0.0) for h in heads]
                dg = _dots_f32([cst["tril"]] * H, dgc, TN, exact="a")
                dgb = jnp.zeros((CK, LANES), F32)
                for h in heads:
                    dq_ref[rows[j], cols(h)] = dq1[h] + dq_dec[h] * eg[h]
                    dk_ref[rows[j], cols(h)] = dk1[h] + dk2[h] + dk_dec[h] * ek[h] + dkb[h] * beta[h]
                    dv_ref[rows[j], cols(h)] = dv[h]
                    db = dbeta[h] + rsum(dkb[h] * k[h])
                    dgb = dgb + jnp.where(lane == h, db, 0.0) + jnp.where(lane == H + h, dg[h], 0.0)
                dgb_ref[rows[j], :] = dgb
                ds_out = ds_new
            for h in heads:
                ds_ref[h] = ds_out[h]
            return carry

        lax.fori_loop(0, cb // nsub, group, 0)

    blk = pl.BlockSpec((rb, H * HD), lambda b: (nb - 1 - b, 0))
    sblk = pl.BlockSpec((rb, LANES), lambda b: (nb - 1 - b, 0))
    return pl.pallas_call(
        body, name="gdr_bwd", grid=(nb,),
        in_specs=[blk, blk, blk, sblk, pl.BlockSpec((H, cb, HD, HD), lambda b: (0, nb - 1 - b, 0, 0)),
                  pl.BlockSpec((H, cb, CK, CK), lambda b: (0, nb - 1 - b, 0, 0)), blk],
        out_specs=[blk, blk, blk, sblk],
        out_shape=[jax.ShapeDtypeStruct((L, H * HD), F32)] * 3 + [jax.ShapeDtypeStruct((L, LANES), F32)],
        scratch_shapes=[pltpu.VMEM((H, HD, HD), F32)],
        compiler_params=_params(1),
    )(qn, kn, vs, gb, states, tinvs, do)


def _post_fwd(o, p, ya, x, modrows, sp, w_out):
    L = x.shape[0]
    T = _tile(L, 256)

    def body(o_ref, z_ref, ya_ref, x_ref, mod_ref, sp_ref, w_ref, y_ref, x2_ref, yb_ref):
        ndw = sp_ref[2:3, :]
        z = z_ref[...]
        sz = z * _sig(z)
        parts = []
        for h in range(H):
            n, _ = _rms(o_ref[:, h * HD:(h + 1) * HD])
            parts.append(n * ndw * sz[:, h * HD:(h + 1) * HD])
        yb = jnp.concatenate(parts, axis=-1).astype(MXU)
        yb_ref[...] = yb
        y = _dot(ya_ref[...], w_ref[0:AW, :], NN) + _dot(yb, w_ref[AW:2 * AW, :], NN)
        y_ref[...] = y
        x2_ref[...] = x_ref[...] + mod_ref[2:3, :] * y

    row = lambda i: (i, 0)
    zcol = (3 * AW + 3 * H * HD) // (H * HD)
    return pl.pallas_call(
        body, name="post_fwd", grid=(L // T,),
        in_specs=[pl.BlockSpec((T, H * HD), row), pl.BlockSpec((T, H * HD), lambda i: (i, zcol)),
                  pl.BlockSpec((T, AW), row), pl.BlockSpec((T, D), row), _full((SUB, D)), _full((SUB, LANES)),
                  _full((D, D))],
        out_specs=[pl.BlockSpec((T, D), row), pl.BlockSpec((T, D), row), pl.BlockSpec((T, H * HD), row)],
        out_shape=[jax.ShapeDtypeStruct((L, D), F32), jax.ShapeDtypeStruct((L, D), F32),
                   jax.ShapeDtypeStruct((L, H * HD), MXU)],
        compiler_params=_params(1),
    )(o, p, ya, x, modrows, sp, w_out)


FF_COLS = 11
FF_CW = DFF // FF_COLS
FF_ROWS = 512


def _ffn_fwd(x2, modrows, vec, w_up, cff, w_down):
    L = x2.shape[0]
    T = _tile(L, FF_ROWS)
    nj = FF_COLS

    def body(x_ref, mod_ref, vec_ref, wg_ref, wu_ref, cg_ref, cu_ref, wd_ref,
             h_ref, gp_ref, up_ref, f_ref, d_ref, x3_ref, h_s, acc, carry_g, carry_u):
        i, j = pl.program_id(0), pl.program_id(1)

        @pl.when(i == 0)
        def _():
            carry_g[j] = jnp.zeros((SUB, FF_CW), F32)
            carry_u[j] = jnp.zeros((SUB, FF_CW), F32)

        @pl.when(j == 0)
        def _():
            n, _ = _rms(x_ref[...])
            hb = (n * vec_ref[1:2, :] * (1.0 + mod_ref[4:5, :]) + mod_ref[3:4, :]).astype(MXU)
            h_s[...] = hb
            h_ref[...] = hb
            acc[...] = jnp.zeros_like(acc)

        hb = h_s[...]
        g = _dot(hb, wg_ref[...], NN)
        u = _dot(hb, wu_ref[...], NN)
        gp_ref[...] = g
        up_ref[...] = u
        gc, _ = _conv_fwd(g, cg_ref, 3, carry_g[j])
        uc, _ = _conv_fwd(u, cu_ref, 3, carry_u[j])
        carry_g[j] = g[T - SUB:T, :]
        carry_u[j] = u[T - SUB:T, :]
        fb = (gc * _sig(gc) * uc).astype(MXU)
        f_ref[...] = fb
        acc[...] += _dot(fb, wd_ref[...], NN)

        @pl.when(j == nj - 1)
        def _():
            dv = acc[...]
            d_ref[...] = dv
            x3_ref[...] = x_ref[...] + mod_ref[5:6, :] * dv

    row = lambda i, j: (i, 0)
    col = lambda i, j: (i, j)
    return pl.pallas_call(
        body, name="ffn_fwd", grid=(L // T, nj),
        in_specs=[pl.BlockSpec((T, D), row), _full((SUB, D)), _full((SUB, D)),
                  pl.BlockSpec((D, FF_CW), lambda i, j: (0, j)), pl.BlockSpec((D, FF_CW), lambda i, j: (0, nj + j)),
                  pl.BlockSpec((SUB, FF_CW), lambda i, j: (0, j)), pl.BlockSpec((SUB, FF_CW), lambda i, j: (0, nj + j)),
                  pl.BlockSpec((FF_CW, D), lambda i, j: (j, 0))],
        out_specs=[pl.BlockSpec((T, D), row), pl.BlockSpec((T, FF_CW), col), pl.BlockSpec((T, FF_CW), col),
                   pl.BlockSpec((T, FF_CW), col), pl.BlockSpec((T, D), row), pl.BlockSpec((T, D), row)],
        out_shape=[jax.ShapeDtypeStruct((L, D), MXU), jax.ShapeDtypeStruct((L, DFF), F32),
                   jax.ShapeDtypeStruct((L, DFF), F32), jax.ShapeDtypeStruct((L, DFF), MXU),
                   jax.ShapeDtypeStruct((L, D), F32), jax.ShapeDtypeStruct((L, D), F32)],
        scratch_shapes=[pltpu.VMEM((T, D), MXU), pltpu.VMEM((T, D), F32),
                        pltpu.VMEM((nj, SUB, FF_CW), F32), pltpu.VMEM((nj, SUB, FF_CW), F32)],
        compiler_params=_params(2),
    )(x2, modrows, vec, w_up, w_up, cff, cff, w_down)


def _final(x, target, nf):
    L = x.shape[0]
    T = _tile(L, 256)

    def body(x_ref, t_ref, nf_ref, dx_ref, acc_ref):
        @pl.when(pl.program_id(0) == 0)
        def _():
            acc_ref[...] = jnp.zeros_like(acc_ref)

        n, r = _rms(x_ref[...])
        w = nf_ref[0:1, :]
        err = n * w - t_ref[...]
        acc_ref[0:1, :] += (0.5 / D) * _sum0(err * err)
        dy = err * (1.0 / D)
        acc_ref[1:2, :] += _sum0(dy * n)
        dx_ref[...] = _rms_bwd(dy * w, n, r)

    row = lambda i: (i, 0)
    return pl.pallas_call(
        body, name="final_norm_loss", grid=(L // T,),
        in_specs=[pl.BlockSpec((T, D), row), pl.BlockSpec((T, D), row), _full((SUB, D))],
        out_specs=[pl.BlockSpec((T, D), row), _full((SUB, D))],
        out_shape=[jax.ShapeDtypeStruct((L, D), F32), jax.ShapeDtypeStruct((SUB, D), F32)],
        compiler_params=_params(1),
    )(x, target, nf)


def _ffn_bwd(dx3, d, x2, modrows, vec, gpre, upre, cff, w_down, w_up):
    L = dx3.shape[0]
    T = _tile(L, FF_ROWS)
    ni, nj = L // T, FF_COLS
    hb_per_t = T // SUB

    def body(dx3_ref, d_ref, x2_ref, mod_ref, vec_ref, gp_ref, up_ref, gph_ref, uph_ref, cg_ref, cu_ref,
             wd_ref, wg_ref, wu_ref,
             dd_ref, dgp_ref, dup_ref, dx2_ref, accv_ref, dcg_ref, dcu_ref,
             dd_s, acch, carry_g, carry_u):
        i, j = pl.program_id(0), pl.program_id(1)
        ri = ni - 1 - i

        @pl.when((i == 0) & (j == 0))
        def _():
            accv_ref[...] = jnp.zeros_like(accv_ref)
            dcg_ref[...] = jnp.zeros_like(dcg_ref)
            dcu_ref[...] = jnp.zeros_like(dcu_ref)

        @pl.when(i == 0)
        def _():
            carry_g[j] = jnp.zeros((SUB, FF_CW), F32)
            carry_u[j] = jnp.zeros((SUB, FF_CW), F32)

        @pl.when(j == 0)
        def _():
            dx3v = dx3_ref[...]
            accv_ref[0:1, :] += _sum0(dx3v * d_ref[...])
            ddb = (mod_ref[5:6, :] * dx3v).astype(MXU)
            dd_s[...] = ddb
            dd_ref[...] = ddb
            acch[...] = jnp.zeros_like(acch)

        ddb = dd_s[...]
        g, u = gp_ref[...], up_ref[...]
        keep = jnp.where(ri == 0, 0.0, 1.0)
        gc, gsh = _conv_fwd(g, cg_ref, 3, gph_ref[...] * keep)
        uc, ush = _conv_fwd(u, cu_ref, 3, uph_ref[...] * keep)
        sg = _sig(gc)
        df = _dot(ddb, wd_ref[...], NT)
        duc = df * (gc * sg)
        dgc = df * uc * (sg * (1.0 + gc * (1.0 - sg)))
        for s in range(3):
            dcg_ref[j, 2 - s:3 - s, :] += _sum0(dgc * gsh[s])
            dcu_ref[j, 2 - s:3 - s, :] += _sum0(duc * ush[s])
        dg = _conv_bwd_in(dgc, cg_ref, 3, carry_g[j]).astype(MXU)
        du = _conv_bwd_in(duc, cu_ref, 3, carry_u[j]).astype(MXU)
        carry_g[j] = dgc[0:SUB, :]
        carry_u[j] = duc[0:SUB, :]
        dgp_ref[...] = dg
        dup_ref[...] = du
        acch[...] += _dot(dg, wg_ref[...], NT) + _dot(du, wu_ref[...], NT)

        @pl.when(j == nj - 1)
        def _():
            dh = acch[...]
            n, r = _rms(x2_ref[...])
            nw, sc = vec_ref[1:2, :], mod_ref[4:5, :]
            accv_ref[1:2, :] += _sum0(dh)
            accv_ref[2:3, :] += _sum0(dh * n * nw)
            accv_ref[3:4, :] += _sum0(dh * n * (1.0 + sc))
            dx2_ref[...] = _rms_bwd(dh * nw * (1.0 + sc), n, r) + dx3_ref[...]

    row = lambda i, j: (ni - 1 - i, 0)
    col = lambda i, j: (ni - 1 - i, j)
    halo = lambda i, j: (jnp.maximum((ni - 1 - i) * hb_per_t - 1, 0), j)
    return pl.pallas_call(
        body, name="ffn_bwd", grid=(ni, nj),
        in_specs=[pl.BlockSpec((T, D), row), pl.BlockSpec((T, D), row), pl.BlockSpec((T, D), row),
                  _full((SUB, D)), _full((SUB, D)),
                  pl.BlockSpec((T, FF_CW), col), pl.BlockSpec((T, FF_CW), col),
                  pl.BlockSpec((SUB, FF_CW), halo), pl.BlockSpec((SUB, FF_CW), halo),
                  pl.BlockSpec((SUB, FF_CW), lambda i, j: (0, j)), pl.BlockSpec((SUB, FF_CW), lambda i, j: (0, nj + j)),
                  pl.BlockSpec((FF_CW, D), lambda i, j: (j, 0)),
                  pl.BlockSpec((D, FF_CW), lambda i, j: (0, j)), pl.BlockSpec((D, FF_CW), lambda i, j: (0, nj + j))],
        out_specs=[pl.BlockSpec((T, D), row), pl.BlockSpec((T, FF_CW), col), pl.BlockSpec((T, FF_CW), col),
                   pl.BlockSpec((T, D), row), _full((SUB, D)), _full((nj, SUB, FF_CW)), _full((nj, SUB, FF_CW))],
        out_shape=[jax.ShapeDtypeStruct((L, D), MXU), jax.ShapeDtypeStruct((L, DFF), MXU),
                   jax.ShapeDtypeStruct((L, DFF), MXU), jax.ShapeDtypeStruct((L, D), F32),
                   jax.ShapeDtypeStruct((SUB, D), F32), jax.ShapeDtypeStruct((nj, SUB, FF_CW), F32),
                   jax.ShapeDtypeStruct((nj, SUB, FF_CW), F32)],
        scratch_shapes=[pltpu.VMEM((T, D), MXU), pltpu.VMEM((T, D), F32),
                        pltpu.VMEM((nj, SUB, FF_CW), F32), pltpu.VMEM((nj, SUB, FF_CW), F32)],
        compiler_params=_params(2),
    )(dx3, d, x2, modrows, vec, gpre, upre, gpre, upre, cff, cff, w_down, w_up, w_up)


def _post_bwd(dx2, y, o, p, modrows, sp, w_out):
    L = dx2.shape[0]
    T = _tile(L, 256)

    def body(dx2_ref, y_ref, o_ref, z_ref, mod_ref, sp_ref, w_ref, dy_ref, do_ref, dz_ref, dya_ref, accv_ref, accs_ref):
        @pl.when(pl.program_id(0) == 0)
        def _():
            accv_ref[...] = jnp.zeros_like(accv_ref)
            accs_ref[...] = jnp.zeros_like(accs_ref)

        dx2v = dx2_ref[...]
        accv_ref[0:1, :] += _sum0(dx2v * y_ref[...])
        dyb = (mod_ref[2:3, :] * dx2v).astype(MXU)
        dy_ref[...] = dyb
        dyc = _dot(dyb, w_ref[...], NT)
        dya_ref[...] = dyc[:, 0:AW]
        ndw = sp_ref[2:3, :]
        z = z_ref[...]
        sgz = _sig(z)
        dsz = sgz * (1.0 + z * (1.0 - sgz))
        dndw = jnp.zeros((1, HD), F32)
        for h in range(H):
            sl = slice(h * HD, (h + 1) * HD)
            n, r = _rms(o_ref[:, sl])
            dyh = dyc[:, AW + h * HD:AW + (h + 1) * HD]
            zh = z[:, sl]
            don = dyh * (zh * sgz[:, sl])
            dz_ref[:, sl] = dyh * (n * ndw) * dsz[:, sl]
            dndw = dndw + _sum0(don * n)
            do_ref[:, sl] = _rms_bwd(don * ndw, n, r)
        accs_ref[0:1, :] += dndw

    row = lambda i: (i, 0)
    zcol = (3 * AW + 3 * H * HD) // (H * HD)
    return pl.pallas_call(
        body, name="post_bwd", grid=(L // T,),
        in_specs=[pl.BlockSpec((T, D), row), pl.BlockSpec((T, D), row), pl.BlockSpec((T, H * HD), row),
                  pl.BlockSpec((T, H * HD), lambda i: (i, zcol)), _full((SUB, D)), _full((SUB, LANES)), _full((D, D))],
        out_specs=[pl.BlockSpec((T, D), row)] + [pl.BlockSpec((T, H * HD), row)] * 3 + [_full((SUB, D)), _full((SUB, LANES))],
        out_shape=[jax.ShapeDtypeStruct((L, D), MXU)] + [jax.ShapeDtypeStruct((L, H * HD), F32)] * 3
        + [jax.ShapeDtypeStruct((SUB, D), F32), jax.ShapeDtypeStruct((SUB, LANES), F32)],
        compiler_params=_params(1),
    )(dx2, y, o, p, modrows, sp, w_out)


def _pre_bwd(p, dqn, dkn, dvs, dya, dz, dgb, pa, cq, sp):
    L = p.shape[0]
    T = _tile(L, 256)
    ni = L // T
    scale = HD ** -0.5
    w3 = 3 * AW + 3 * H * HD
    hb_per_t = T // SUB

    def body(pm_ref, ph_ref, ps_ref, dq_ref, dk_ref, dv_ref, dya_ref, dz_ref, dgb_ref, pa_ref, cq_ref, sp_ref,
             dp_ref, dpa_ref, dcq_ref, dsp_ref, carry_u, carry_q):
        i = pl.program_id(0)
        ri = ni - 1 - i

        @pl.when(i == 0)
        def _():
            dpa_ref[...] = jnp.zeros_like(dpa_ref)
            dcq_ref[...] = jnp.zeros_like(dcq_ref)
            dsp_ref[...] = jnp.zeros_like(dsp_ref)
            carry_u[...] = jnp.zeros_like(carry_u)
            carry_q[...] = jnp.zeros_like(carry_q)

        keep = jnp.where(ri == 0, 0.0, 1.0)
        a_b, a_c, a_x = pm_ref[:, 0:AW], pm_ref[:, AW:2 * AW], pm_ref[:, 2 * AW:3 * AW]
        u = a_c * a_x
        hu = ph_ref[:, AW:2 * AW] * ph_ref[:, 2 * AW:3 * AW] * keep
        cu, ush = _conv_fwd(u, pa_ref, 3, hu)
        yp = a_b * cu
        bd = _blockdiag_mean(AW, A_GROUP)
        ra = lax.rsqrt(_dot_f32(yp * yp, bd, NN, exact="b") + EPS)
        na = yp * ra
        dya = dya_ref[...]
        dpa_ref[3:4, :] += _sum0(dya * na)
        dna = dya * pa_ref[3:4, :]
        dyp = ra * (dna - na * _dot_f32(dna * na, bd, NN, exact="b"))
        dcu = dyp * a_b
        for s in range(3):
            dpa_ref[2 - s:3 - s, :] += _sum0(dcu * ush[s])
        du = _conv_bwd_in(dcu, pa_ref, 3, carry_u[...])
        carry_u[...] = dcu[0:SUB, :]
        dp_ref[:, 0:AW] = (dyp * cu).astype(MXU)
        dp_ref[:, AW:2 * AW] = (du * a_x).astype(MXU)
        dp_ref[:, 2 * AW:3 * AW] = (du * a_c).astype(MXU)

        qkv = pm_ref[:, 3 * AW:w3]
        qc, qsh = _conv_fwd(qkv, cq_ref, 4, ph_ref[:, 3 * AW:w3] * keep)
        sg = _sig(qc)
        qs = qc * sg
        parts = []
        for h in range(H):
            q = qs[:, h * HD:(h + 1) * HD]
            rq = lax.rsqrt(jnp.sum(q * q, axis=-1, keepdims=True) + EPS)
            parts.append(_l2_bwd(dq_ref[:, h * HD:(h + 1) * HD] * scale, q * rq, rq))
        for h in range(H):
            k = qs[:, (H + h) * HD:(H + h + 1) * HD]
            rk = lax.rsqrt(jnp.sum(k * k, axis=-1, keepdims=True) + EPS)
            parts.append(_l2_bwd(dk_ref[:, h * HD:(h + 1) * HD], k * rk, rk))
        parts.append(dv_ref[...])
        dqc = jnp.concatenate(parts, axis=-1) * (sg * (1.0 + qc * (1.0 - sg)))
        for s in range(4):
            dcq_ref[3 - s:4 - s, :] += _sum0(dqc * qsh[s])
        dp_ref[:, 3 * AW:w3] = _conv_bwd_in(dqc, cq_ref, 4, carry_q[...]).astype(MXU)
        carry_q[...] = dqc[0:SUB, :]
        dp_ref[:, w3:w3 + H * HD] = dz_ref[...].astype(MXU)

        lane, a, xb, beta, g = _gate_small(ps_ref[...], sp_ref)
        dgb = dgb_ref[...]
        dbeta = jnp.where(lane < H, dgb, 0.0)
        dg = jnp.where((lane >= H) & (lane < 2 * H), dgb, 0.0)
        dalpha = dg * a * _sig(xb)
        dsp_ref[0:1, :] += _sum0(dg * g)
        dsp_ref[1:2, :] += _sum0(dalpha)
        dp_ref[:, w3 + H * HD:P_PAD] = (dbeta * beta * (1.0 - beta) + dalpha).astype(MXU)

    row = lambda i: (ni - 1 - i, 0)
    halo = lambda i: (jnp.maximum((ni - 1 - i) * hb_per_t - 1, 0), 0)
    hrow = pl.BlockSpec((T, H * HD), row)
    return pl.pallas_call(
        body, name="pre_bwd", grid=(ni,),
        in_specs=[pl.BlockSpec((T, w3), row), pl.BlockSpec((SUB, w3), halo),
                  pl.BlockSpec((T, LANES), lambda i: (ni - 1 - i, (P_PAD - LANES) // LANES)),
                  hrow, hrow, hrow, pl.BlockSpec((T, AW), row), hrow,
                  pl.BlockSpec((T, LANES), row),
                  _full((SUB, AW)), _full((SUB, 3 * H * HD)), _full((SUB, LANES))],
        out_specs=[pl.BlockSpec((T, P_PAD), row), _full((SUB, AW)), _full((SUB, 3 * H * HD)), _full((SUB, LANES))],
        out_shape=[jax.ShapeDtypeStruct((L, P_PAD), MXU), jax.ShapeDtypeStruct((SUB, AW), F32),
                   jax.ShapeDtypeStruct((SUB, 3 * H * HD), F32), jax.ShapeDtypeStruct((SUB, LANES), F32)],
        scratch_shapes=[pltpu.VMEM((SUB, AW), F32), pltpu.VMEM((SUB, 3 * H * HD), F32)],
        compiler_params=_params(1),
    )(p, p, p, dqn, dkn, dvs, dya, dz, dgb, pa, cq, sp)


def _in_bwd(dp, w_in, x, dx2, modrows, vec):
    L = x.shape[0]
    T = _tile(L, 256)

    def body(dp_ref, w_ref, x_ref, dx2_ref, mod_ref, vec_ref, dx_ref, accv_ref):
        @pl.when(pl.program_id(0) == 0)
        def _():
            accv_ref[...] = jnp.zeros_like(accv_ref)

        dh = _dot(dp_ref[...], w_ref[...], NT)
        n, r = _rms(x_ref[...])
        nw, sc = vec_ref[0:1, :], mod_ref[1:2, :]
        accv_ref[0:1, :] += _sum0(dh)
        accv_ref[1:2, :] += _sum0(dh * n * nw)
        accv_ref[2:3, :] += _sum0(dh * n * (1.0 + sc))
        dx_ref[...] = _rms_bwd(dh * nw * (1.0 + sc), n, r) + dx2_ref[...]

    row = lambda i: (i, 0)
    return pl.pallas_call(
        body, name="in_bwd", grid=(L // T,),
        in_specs=[pl.BlockSpec((T, P_PAD), row), _full((D, P_PAD)), pl.BlockSpec((T, D), row),
                  pl.BlockSpec((T, D), row), _full((SUB, D)), _full((SUB, D))],
        out_specs=[pl.BlockSpec((T, D), row), _full((SUB, D))],
        out_shape=[jax.ShapeDtypeStruct((L, D), F32), jax.ShapeDtypeStruct((SUB, D), F32)],
        compiler_params=_params(1),
    )(dp, w_in, x, dx2, modrows, vec)


def _wgrad(a, b, tm, tn, name):
    L, m = a.shape
    n = b.shape[1]
    tl = _tile(L, 512)
    tm, tn = _tile(m, tm), _tile(n, tn)
    nl = L // tl

    def body(a_ref, b_ref, o_ref):
        @pl.when(pl.program_id(2) == 0)
        def _():
            o_ref[...] = jnp.zeros_like(o_ref)

        o_ref[...] += _dot(a_ref[...], b_ref[...], TN)

    return pl.pallas_call(
        body, name=name, grid=(m // tm, n // tn, nl),
        in_specs=[pl.BlockSpec((tl, tm), lambda i, j, l: (l, i)), pl.BlockSpec((tl, tn), lambda i, j, l: (l, j))],
        out_specs=pl.BlockSpec((tm, tn), lambda i, j, l: (i, j)),
        out_shape=jax.ShapeDtypeStruct((m, n), F32), compiler_params=_params(3),
    )(a, b)


def _wgrad_cols(a, b, tm, n_shard, wpad, count, name):
    L, m = a.shape
    n = b.shape[1]
    tl = _tile(L, 512)
    tm = _tile(m, tm)
    nl = L // tl
    wins = _shard_windows(n_shard, count)
    assert all(a_ * LANES + win <= n for a_, _, win in wins), (wins, n)

    def body(a_ref, b_ref, o_ref, acc):
        @pl.when(pl.program_id(1) == 0)
        def _():
            acc[...] = jnp.zeros_like(acc)

        acc[...] += _dot(a_ref[...], b_ref[...], TN)

        @pl.when(pl.program_id(1) == nl - 1)
        def _():
            for k, (a_, s, win) in enumerate(wins):
                xk = acc[:, a_ * LANES:a_ * LANES + win]
                if s:
                    xk = pltpu.roll(xk, win - s, 1)
                o_ref[k] = _fit_lanes(xk, wpad)

    return pl.pallas_call(
        body, name=name, grid=(m // tm, nl),
        in_specs=[pl.BlockSpec((tl, tm), lambda i, l: (l, i)), pl.BlockSpec((tl, n), lambda i, l: (l, 0))],
        out_specs=pl.BlockSpec((count, tm, wpad), lambda i, l: (0, i, 0)),
        out_shape=jax.ShapeDtypeStruct((count, m, wpad), F32),
        scratch_shapes=[pltpu.VMEM((tm, n), F32)],
        compiler_params=_params(2),
    )(a, b)


def _adamw(w, g, m, v, name):
    r, n = w.shape
    tr = _tile(r, 512)
    bc1 = 1.0 - ADAM_B1 ** ADAM_STEP
    bc2 = 1.0 - ADAM_B2 ** ADAM_STEP

    def body(w_ref, g_ref, m_ref, v_ref, d_ref, nm_ref, nv_ref):
        gv = g_ref[...]
        nm = ADAM_B1 * m_ref[...] + (1.0 - ADAM_B1) * gv
        nv = ADAM_B2 * v_ref[...] + (1.0 - ADAM_B2) * (gv * gv)
        nm_ref[...] = nm
        nv_ref[...] = nv
        d_ref[...] = -ADAM_LR * ((nm / bc1) / (jnp.sqrt(nv / bc2) + ADAM_EPS) + ADAM_WD * w_ref[...])

    spec = pl.BlockSpec((tr, n), lambda i: (i, 0))
    return pl.pallas_call(
        body, name=name, grid=(r // tr,), in_specs=[spec] * 4, out_specs=[spec] * 3,
        out_shape=[jax.ShapeDtypeStruct((r, n), F32)] * 3, compiler_params=_params(1),
    )(w, g, m, v)


def _rows8(rows, width):
    out = jnp.zeros((SUB, width), F32)
    for r, vrow in enumerate(rows):
        out = out.at[r, :vrow.shape[0]].set(vrow)
    return out


def _at_lanes(v4, start):
    return jnp.zeros((LANES,), F32).at[start:start + v4.shape[0]].set(v4)


def _pad_rows(flat, mult):
    n = flat.shape[0]
    pad = (-n) % mult
    return jnp.pad(flat, (0, pad)) if pad else flat


IN_PAD = 512
UP_PAD = 768


def _local_fwd_bwd(x, target, mod_full, small_w, full_w):
    norm1_w, norm2_w, norm_a_w, a_log, dt_bias, norm_dn_w, norm_f_w = small_w
    w_in_f, w_out_f, w_up_f, w_down_f, conv_a_f, conv_q_f, conv_f_f = full_w

    def layer_params(i):
        modrows = jnp.concatenate([mod_full[i], jnp.zeros((SUB - N_MOD, D), F32)], axis=0)
        vec = _rows8([norm1_w[i], norm2_w[i]], D)
        pa = _rows8([conv_a_f[i, 0], conv_a_f[i, 1], conv_a_f[i, 2], norm_a_w[i]], AW)
        cq = _rows8([conv_q_f[i, k] for k in range(4)], 3 * H * HD)
        sp = _rows8([_at_lanes(a_log[i], H), _at_lanes(dt_bias[i], H), norm_dn_w[i]], LANES)
        cff = _rows8([conv_f_f[i, k] for k in range(3)], 2 * DFF)
        return modrows, vec, pa, cq, sp, cff

    saved = []
    xi = x
    for i in range(DEPTH):
        modrows, vec, pa, cq, sp, cff = layer_params(i)
        p, h1 = _in_proj(xi, modrows, vec, w_in_f[i])
        qn, kn, vs, gb, ya = _pre_fwd(p, pa, cq, sp)
        o, states, tinvs = _gdr_fwd(qn, kn, vs, gb)
        y, x2, yb = _post_fwd(o, p, ya, xi, modrows, sp, w_out_f[i])
        h2, gpre, upre, f, dff, x3 = _ffn_fwd(x2, modrows, vec, w_up_f[i], cff, w_down_f[i])
        saved.append(dict(x=xi, p=p, h1=h1, qn=qn, kn=kn, vs=vs, gb=gb, ya=ya, o=o, states=states, tinvs=tinvs, y=y, x2=x2, yb=yb,
                          h2=h2, gpre=gpre, upre=upre, f=f, d=dff))
        xi = x3

    dx, facc = _final(xi, target, _rows8([norm_f_w], D))
    loss_local = jnp.sum(facc[0])
    d_norm_f = facc[1]

    gw_in, gw_out, gw_up, gw_down = [None] * DEPTH, [None] * DEPTH, [None] * DEPTH, [None] * DEPTH
    g_small = [None] * DEPTH
    for i in reversed(range(DEPTH)):
        s = saved[i]
        modrows, vec, pa, cq, sp, cff = layer_params(i)
        dd, dgp, dup, dx2, accf, dcg, dcu = _ffn_bwd(dx, s["d"], s["x2"], modrows, vec, s["gpre"], s["upre"], cff,
                                                       w_down_f[i], w_up_f[i])
        n_up, up_pad = 2 * DFF // N_DEV, UP_PAD
        gw_up[i] = jnp.concatenate([_wgrad_cols(s["h2"], dgp, 512, n_up, up_pad, N_DEV // 2, "wgrad_up"),
                                    _wgrad_cols(s["h2"], dup, 512, n_up, up_pad, N_DEV // 2, "wgrad_up")], axis=0)
        gw_down[i] = _wgrad(s["f"], dd, DFF // 2, 1024, "wgrad_down").reshape(N_DEV, DFF // N_DEV, D)
        dy, do, dz, dya, accp, accs = _post_bwd(dx2, s["y"], s["o"], s["p"], modrows, sp, w_out_f[i])
        gw_out[i] = jnp.concatenate([_wgrad(s["ya"], dy, 512, 1024, "wgrad_out"),
                                     _wgrad(s["yb"], dy, 512, 1024, "wgrad_out")], axis=0).reshape(N_DEV, D // N_DEV, D)
        dqn, dkn, dvs, dgb = _gdr_bwd(s["qn"], s["kn"], s["vs"], s["gb"], s["states"], s["tinvs"], do)
        dp, dpa, dcq, dsp = _pre_bwd(s["p"], dqn, dkn, dvs, dya, dz, dgb, pa, cq, sp)
        gw_in[i] = _wgrad_cols(s["h1"], dp, 512, P_IN // N_DEV, IN_PAD, N_DEV, "wgrad_in")
        dx, acci = _in_bwd(dp, w_in_f[i], s["x"], dx2, modrows, vec)
        dconv_ff = jnp.concatenate([dcg.transpose(1, 0, 2).reshape(SUB, DFF), dcu.transpose(1, 0, 2).reshape(SUB, DFF)],
                                   axis=1)[0:3]
        dmod = jnp.stack([acci[0], acci[1], accp[0], accf[1], accf[2], accf[0]])
        g_small[i] = dict(norm1=acci[2], norm2=accf[3], norm_a=dpa[3], a_log=dsp[0, H:2 * H], dt_bias=dsp[1, H:2 * H],
                          norm_dn=accs[0], conv_a=dpa[0:3], conv_qkv=dcq[0:4], conv_ff=dconv_ff, dmod=dmod.reshape(-1))
    return loss_local, dx, gw_in, gw_out, gw_up, gw_down, g_small, d_norm_f


def kernel(x, c, ada_w, ada_b, norm1_w, w_in, conv_a_w, norm_a_w, conv_qkv_w, a_log, dt_bias, norm_dn_w, w_out, norm2_w, w_up, conv_ff_w, w_down, norm_f_w, loss_target, m_ada_w, m_ada_b, m_norm1_w, m_w_in, m_conv_a_w, m_norm_a_w, m_conv_qkv_w, m_a_log, m_dt_bias, m_norm_dn_w, m_w_out, m_norm2_w, m_w_up, m_conv_ff_w, m_w_down, m_norm_f_w, v_ada_w, v_ada_b, v_norm1_w, v_w_in, v_conv_a_w, v_norm_a_w, v_conv_qkv_w, v_a_log, v_dt_bias, v_norm_dn_w, v_w_out, v_norm2_w, v_w_up, v_conv_ff_w, v_w_down, v_norm_f_w):
    ax, ay, ac = lax.axis_index("x"), lax.axis_index("y"), lax.axis_index("c")
    me = 4 * ax + 2 * ay + ac
    x = x[0]
    target = loss_target[0]
    n_in, n_up = P_IN // N_DEV, 2 * DFF // N_DEV

    def lane_pad(t, width):
        return jnp.pad(t.astype(MXU), ((0, 0), (0, 0), (0, width - t.shape[-1])))

    conv_blob = _pad_rows(jnp.concatenate([t.reshape(-1) for t in (conv_a_w, conv_qkv_w, conv_ff_w)]),
                          SUB * LANES).reshape(-1, LANES)
    c_rows = jnp.zeros((SUB, D), F32).at[0].set(c[0])
    g_in, g_out, g_up, g_down, g_conv, g_c = _all_gather(
        [lane_pad(w_in, IN_PAD), w_out.astype(MXU), lane_pad(w_up, UP_PAD), w_down.astype(MXU), conv_blob, c_rows],
        "gather_weights", in_vmem=False)
    w_in_f = _interleave_cols(g_in, n_in, P_PAD, "interleave_w_in")
    w_up_f = _interleave_cols(g_up, n_up, 2 * DFF, "interleave_w_up")
    w_out_f = g_out.transpose(1, 0, 2, 3).reshape(DEPTH, D, D)
    w_down_f = g_down.transpose(1, 0, 2, 3).reshape(DEPTH, DFF, D)
    sg = g_conv.reshape(N_DEV, -1)
    o1 = conv_a_w.size
    o2 = o1 + conv_qkv_w.size
    o3 = o2 + conv_ff_w.size
    conv_a_f = sg[:, 0:o1].reshape(N_DEV, DEPTH, 3, AW // N_DEV).transpose(1, 2, 0, 3).reshape(DEPTH, 3, AW)
    conv_q_f = sg[:, o1:o2].reshape(N_DEV, DEPTH, 4, 3 * H * HD // N_DEV).transpose(1, 2, 0, 3).reshape(DEPTH, 4, 3 * H * HD)
    conv_f_f = sg[:, o2:o3].reshape(N_DEV, DEPTH, 3, n_up).transpose(1, 2, 0, 3).reshape(DEPTH, 3, 2 * DFF)

    c_all = jnp.concatenate([g_c[:, 0], jnp.zeros((16 - N_DEV, D), F32)], axis=0)
    n_ada = N_MOD * D // N_DEV
    ada_b_cols = lax.dynamic_slice_in_dim(ada_b, me * n_ada, n_ada, axis=1)[:, None, :]
    mod_sh = _mod_fwd(c_all, ada_w, ada_b_cols)
    mod_all = _all_gather([mod_sh.reshape(DEPTH * 16, n_ada)], "gather_mod", in_vmem=True)[0]
    mod_all = mod_all.reshape(N_DEV, DEPTH, 16, n_ada)
    mod_mine = lax.dynamic_index_in_dim(mod_all, me, axis=2, keepdims=False)
    mod_full = mod_mine.transpose(1, 0, 2).reshape(DEPTH, N_MOD, D)

    loss_local, dx, gw_in, gw_out, gw_up, gw_down, g_small, d_norm_f = _local_fwd_bwd(
        x, target, mod_full, (norm1_w, norm2_w, norm_a_w, a_log, dt_bias, norm_dn_w, norm_f_w),
        (w_in_f, w_out_f, w_up_f, w_down_f, conv_a_f, conv_q_f, conv_f_f))
    loss = lax.psum(loss_local, ("x", "y", "c"))
    grad_x = dx[None]

    keys = ["dmod", "norm1", "norm2", "norm_a", "a_log", "dt_bias", "norm_dn", "conv_a", "conv_qkv", "conv_ff"]
    stacked = {k: jnp.stack([g_small[i][k] for i in range(DEPTH)]) for k in keys}
    flat_parts = [stacked[k].reshape(-1) for k in keys] + [d_norm_f]
    sizes = [int(t.shape[0]) for t in flat_parts]
    sflat = _pad_rows(jnp.concatenate(flat_parts), SUB * LANES).reshape(-1, LANES)
    sall = _all_gather([sflat], "gather_small_grads", in_vmem=True)[0]
    ssum = _sum_devices(sall).reshape(-1)
    so = [0]
    for sz in sizes:
        so.append(so[-1] + sz)
    red = {k: ssum[so[n]:so[n + 1]].reshape(stacked[k].shape) for n, k in enumerate(keys)}
    g_norm_f = ssum[so[len(keys)]:so[len(keys) + 1]]
    dmod_all = sall[:, 0:sizes[0] // LANES, :].reshape(N_DEV, DEPTH, N_MOD * D)

    g_ada_b = red["dmod"].reshape(DEPTH, N_MOD * D)
    dmod_cols = lax.dynamic_slice_in_dim(dmod_all, me * n_ada, n_ada, axis=2).transpose(1, 0, 2)
    dmod_cols = jnp.concatenate([dmod_cols, jnp.zeros((DEPTH, 16 - N_DEV, n_ada), F32)], axis=1)
    g_ada_w = _mod_bwd(c_all, dmod_cols)
    g_conv_a = lax.dynamic_slice_in_dim(red["conv_a"], me * (AW // N_DEV), AW // N_DEV, axis=2)
    g_conv_qkv = lax.dynamic_slice_in_dim(red["conv_qkv"], me * (3 * H * HD // N_DEV), 3 * H * HD // N_DEV, axis=2)
    g_conv_ff = lax.dynamic_slice_in_dim(red["conv_ff"], me * n_up, n_up, axis=2)

    tags = ["w_in", "w_out", "w_up", "w_down"]
    gs = [jnp.stack(t, axis=1) for t in (gw_in, gw_out, gw_up, gw_down)]
    my_c = jnp.reshape(ac, (1,)).astype(jnp.int32)
    my_chip = jnp.reshape(2 * ax + ay, (1,)).astype(jnp.int32)
    recv1 = _rs_sibling(gs)
    pairs = [_rs_add_pairs(g, r, my_c, "rs_add_pairs_" + t) for g, r, t in zip(gs, recv1, tags)]
    recv2 = _rs_chips([pb for _, pb in pairs])
    mine = [_rs_add_chips(pf, r, my_chip, "rs_add_chips_" + t) for (pf, _), r, t in zip(pairs, recv2, tags)]
    g_w_in = mine[0][:, :, :n_in]
    g_w_out = mine[1]
    g_w_up = mine[2][:, :, :n_up]
    g_w_down = mine[3]

    grads = dict(ada_w=g_ada_w, ada_b=g_ada_b, norm1_w=red["norm1"], w_in=g_w_in, conv_a_w=g_conv_a,
                 norm_a_w=red["norm_a"], conv_qkv_w=g_conv_qkv, a_log=red["a_log"], dt_bias=red["dt_bias"],
                 norm_dn_w=red["norm_dn"], w_out=g_w_out, norm2_w=red["norm2"], w_up=g_w_up, conv_ff_w=g_conv_ff,
                 w_down=g_w_down, norm_f_w=g_norm_f)
    weights = dict(ada_w=ada_w, ada_b=ada_b, norm1_w=norm1_w, w_in=w_in, conv_a_w=conv_a_w, norm_a_w=norm_a_w,
                   conv_qkv_w=conv_qkv_w, a_log=a_log, dt_bias=dt_bias, norm_dn_w=norm_dn_w, w_out=w_out,
                   norm2_w=norm2_w, w_up=w_up, conv_ff_w=conv_ff_w, w_down=w_down, norm_f_w=norm_f_w)
    ms = dict(ada_w=m_ada_w, ada_b=m_ada_b, norm1_w=m_norm1_w, w_in=m_w_in, conv_a_w=m_conv_a_w, norm_a_w=m_norm_a_w,
              conv_qkv_w=m_conv_qkv_w, a_log=m_a_log, dt_bias=m_dt_bias, norm_dn_w=m_norm_dn_w, w_out=m_w_out,
              norm2_w=m_norm2_w, w_up=m_w_up, conv_ff_w=m_conv_ff_w, w_down=m_w_down, norm_f_w=m_norm_f_w)
    vs_ = dict(ada_w=v_ada_w, ada_b=v_ada_b, norm1_w=v_norm1_w, w_in=v_w_in, conv_a_w=v_conv_a_w, norm_a_w=v_norm_a_w,
               conv_qkv_w=v_conv_qkv_w, a_log=v_a_log, dt_bias=v_dt_bias, norm_dn_w=v_norm_dn_w, w_out=v_w_out,
               norm2_w=v_norm2_w, w_up=v_w_up, conv_ff_w=v_conv_ff_w, w_down=v_w_down, norm_f_w=v_norm_f_w)
    names = list(weights)
    big_names = ["ada_w", "w_in", "w_out", "w_up", "w_down"]
    delta, new_m, new_v = {}, {}, {}
    for n in big_names:
        shp = weights[n].shape
        two = lambda t: t.reshape(-1, shp[-1])
        dl, nm, nv = _adamw(two(weights[n]), two(grads[n]), two(ms[n]), two(vs_[n]), "adamw_" + n)
        delta[n], new_m[n], new_v[n] = dl.reshape(shp), nm.reshape(shp), nv.reshape(shp)
    small_names = [n for n in names if n not in big_names]

    def pack(dct):
        return _pad_rows(jnp.concatenate([dct[n].reshape(-1) for n in small_names]), SUB * LANES).reshape(-1, LANES)

    dl, nm, nv = _adamw(pack(weights), pack(grads), pack(ms), pack(vs_), "adamw_small")
    off = 0
    for n in small_names:
        sz, shp = weights[n].size, weights[n].shape
        delta[n] = dl.reshape(-1)[off:off + sz].reshape(shp)
        new_m[n] = nm.reshape(-1)[off:off + sz].reshape(shp)
        new_v[n] = nv.reshape(-1)[off:off + sz].reshape(shp)
        off += sz

    return (loss, grad_x, *[grads[n] for n in names], *[delta[n] for n in names],
            *[new_m[n] for n in names], *[new_v[n] for n in names])
```

```python
import functools
import math

import jax
import jax.numpy as jnp
from jax import lax
from jax.experimental import pallas as pl
from jax.experimental.pallas import tpu as pltpu

F32 = jnp.float32
MXU = jnp.bfloat16

D = 1024
DEPTH = 4
N_MOD = 6
AW = 512
A_GROUP = 64
H = 4
HD = 128
CK = 64
DFF = 2816
P_IN = 3592
P_PAD = 3712
EPS = 1e-6
N_DEV = 8
LANES = 128
SUB = 8
VMEM_LIMIT = 56 * 1024 * 1024

ADAM_LR, ADAM_B1, ADAM_B2, ADAM_EPS, ADAM_WD, ADAM_STEP = 0.001, 0.9, 0.999, 1e-08, 0.01, 10

NN = ((1,), (0,))
NT = ((1,), (1,))
TN = ((0,), (0,))
HI = lax.Precision.HIGHEST
MESH = pl.DeviceIdType.MESH


def _dot(a, b, dims, prec=None):
    if prec is None:
        a = a.astype(MXU) if a.dtype == F32 else a
        b = b.astype(MXU) if b.dtype == F32 else b
    return lax.dot_general(a, b, (dims, ((), ())), precision=prec, preferred_element_type=F32)


def _params(n_grid=0, limit=VMEM_LIMIT):
    sem = ("arbitrary",) * n_grid if n_grid else None
    return pltpu.CompilerParams(dimension_semantics=sem, vmem_limit_bytes=limit)


def _tile(n, want):
    if n <= want:
        return n
    t = want - want % SUB
    while n % t:
        t -= SUB
    assert t > 0, (n, want)
    return t


def _full(shape):
    nd = len(shape)
    return pl.BlockSpec(shape, lambda *_: (0,) * nd)


def _sig(x):
    return jax.nn.sigmoid(x)


def _rms(x):
    r = lax.rsqrt(jnp.mean(x * x, axis=-1, keepdims=True) + EPS)
    return x * r, r


def _rms_bwd(dn, n, r):
    return r * (dn - n * jnp.mean(dn * n, axis=-1, keepdims=True))


def _l2_bwd(dn, n, r):
    return r * (dn - n * jnp.sum(dn * n, axis=-1, keepdims=True))


def _sum0(x):
    return jnp.sum(x, axis=0, keepdims=True)


def _shift_down(x, s, halo):
    ext = jnp.concatenate([halo, x], axis=0)
    return pltpu.roll(ext, s, 0)[SUB:, :]


def _shift_up(x, s, halo):
    t = x.shape[0]
    ext = jnp.concatenate([x, halo], axis=0)
    return pltpu.roll(ext, t + SUB - s, 0)[:t, :]


def _conv_fwd(x, w_ref, width, halo):
    sh = [x] + [_shift_down(x, s, halo) for s in range(1, width)]
    out = w_ref[width - 1:width, :] * sh[0]
    for s in range(1, width):
        out = out + w_ref[width - 1 - s:width - s, :] * sh[s]
    return out, sh


def _conv_bwd_in(dout, w_ref, width, halo_next):
    dx = w_ref[width - 1:width, :] * dout
    for s in range(1, width):
        dx = dx + w_ref[width - 1 - s:width - s, :] * _shift_up(dout, s, halo_next)
    return dx


def _blockdiag_mean(n, group):
    r = lax.shift_right_logical(lax.broadcasted_iota(jnp.int32, (n, n), 0), int(math.log2(group)))
    c = lax.shift_right_logical(lax.broadcasted_iota(jnp.int32, (n, n), 1), int(math.log2(group)))
    return jnp.where(r == c, 1.0 / group, 0.0).astype(F32)


def _softplus(x):
    return jnp.maximum(x, 0.0) + jnp.log(1.0 + jnp.exp(-jnp.abs(x)))


def _my_place():
    return lax.axis_index("x"), lax.axis_index("y"), lax.axis_index("c")


def _all_gather(shards, name, in_vmem):
    nt = len(shards)

    def body(*refs):
        x_refs, out_refs = refs[:nt], refs[nt:2 * nt]
        send_sems, recv_sems, local_sems = refs[2 * nt:]
        x, y, c = _my_place()
        me, sibling = (x, y, c), (x, y, 1 - c)
        chips = [(1 - x, y), (x, 1 - y), (1 - x, 1 - y)]
        everything = []
        for t in range(nt):
            x_ref, out_ref = x_refs[t], out_refs[t]

            def blk(px, py, pc, out_ref=out_ref):
                return out_ref.at[4 * px + 2 * py + pc]

            def copy(k, block, to, src=None, t=t, blk=blk):
                return pltpu.make_async_remote_copy(
                    src_ref=blk(*block) if src is None else src, dst_ref=blk(*block),
                    send_sem=send_sems.at[7 * t + k], recv_sem=recv_sems.at[7 * t + k], device_id=to, device_id_type=MESH)

            mine = pltpu.make_async_copy(x_ref, blk(*me), local_sems.at[t])
            mine.start()
            first = [copy(0, me, sibling, src=x_ref)]
            first += [copy(1 + j, me, (*chip, c), src=x_ref) for j, chip in enumerate(chips)]
            for cp in first:
                cp.start()
            everything.append((copy, mine, first))
        sends = []
        for copy, mine, first in everything:
            passed = [copy(4 + j, (*chip, c), sibling) for j, chip in enumerate(chips)]
            for j, chip in enumerate(chips):
                copy(1 + j, (*chip, c), me).wait_recv()
                passed[j].start()
            sends += first + passed
        for copy, mine, first in everything:
            copy(0, sibling, me).wait_recv()
            for j, chip in enumerate(chips):
                copy(4 + j, (*chip, 1 - c), me).wait_recv()
        for cp in sends:
            cp.wait_send()
        for copy, mine, first in everything:
            mine.wait()

    space = pltpu.VMEM if in_vmem else pl.ANY
    return pl.pallas_call(
        body, name=name,
        out_shape=[jax.ShapeDtypeStruct((N_DEV,) + s.shape, s.dtype) for s in shards],
        in_specs=[pl.BlockSpec(memory_space=space)] * nt,
        out_specs=[pl.BlockSpec(memory_space=space)] * nt,
        scratch_shapes=[pltpu.SemaphoreType.DMA((7 * nt,)), pltpu.SemaphoreType.DMA((7 * nt,)),
                        pltpu.SemaphoreType.DMA((nt,))],
        compiler_params=pltpu.CompilerParams(vmem_limit_bytes=VMEM_LIMIT),
    )(*shards)


def _rs_sibling(gs):
    nt = len(gs)

    def body(*refs):
        g_refs, recv_refs = refs[:nt], refs[nt:2 * nt]
        send_sems, recv_sems = refs[2 * nt:]
        x, y, c = _my_place()
        copies = [pltpu.make_async_remote_copy(
            src_ref=g_refs[t].at[2 * j + (1 - c)], dst_ref=recv_refs[t].at[j],
            send_sem=send_sems.at[4 * t + j], recv_sem=recv_sems.at[4 * t + j],
            device_id=(x, y, 1 - c), device_id_type=MESH) for t in range(nt) for j in range(4)]
        for cp in copies:
            cp.start()
        for cp in copies:
            cp.wait()

    return pl.pallas_call(
        body, name="rs_sibling",
        out_shape=[jax.ShapeDtypeStruct((4,) + g.shape[1:], g.dtype) for g in gs],
        in_specs=[pl.BlockSpec(memory_space=pl.ANY)] * nt, out_specs=[pl.BlockSpec(memory_space=pl.ANY)] * nt,
        scratch_shapes=[pltpu.SemaphoreType.DMA((4 * nt,)), pltpu.SemaphoreType.DMA((4 * nt,))],
    )(*gs)


def _rs_chips(pbs):
    nt = len(pbs)

    def body(*refs):
        p_refs, recv_refs = refs[:nt], refs[nt:2 * nt]
        send_sems, recv_sems = refs[2 * nt:]
        x, y, c = _my_place()
        chips = [(1 - x, y), (x, 1 - y), (1 - x, 1 - y)]
        copies = [pltpu.make_async_remote_copy(
            src_ref=p_refs[t].at[2 * px + py], dst_ref=recv_refs[t].at[s],
            send_sem=send_sems.at[3 * t + s], recv_sem=recv_sems.at[3 * t + s],
            device_id=(px, py, c), device_id_type=MESH) for t in range(nt) for s, (px, py) in enumerate(chips)]
        for cp in copies:
            cp.start()
        for cp in copies:
            cp.wait()

    return pl.pallas_call(
        body, name="rs_chips",
        out_shape=[jax.ShapeDtypeStruct((3,) + p.shape[1:], p.dtype) for p in pbs],
        in_specs=[pl.BlockSpec(memory_space=pl.ANY)] * nt, out_specs=[pl.BlockSpec(memory_space=pl.ANY)] * nt,
        scratch_shapes=[pltpu.SemaphoreType.DMA((3 * nt,)), pltpu.SemaphoreType.DMA((3 * nt,))],
    )(*pbs)


def _rs_add_pairs(g, recv, my_c, name):
    _, nl, r, n = g.shape
    tr = _tile(r, 512)

    def body(c_ref, g_ref, r_ref, pf_ref, pb_ref):
        s = g_ref[...] + r_ref[...]
        pf_ref[...] = s
        pb_ref[...] = s.astype(MXU)

    spec_j = pl.BlockSpec((None, None, tr, n), lambda j, l, i, c_ref: (j, l, i, 0))
    return pl.pallas_call(
        body, name=name,
        grid_spec=pltpu.PrefetchScalarGridSpec(
            num_scalar_prefetch=1, grid=(4, nl, r // tr),
            in_specs=[pl.BlockSpec((None, None, tr, n), lambda j, l, i, c_ref: (2 * j + c_ref[0], l, i, 0)), spec_j],
            out_specs=[spec_j, spec_j]),
        out_shape=[jax.ShapeDtypeStruct((4, nl, r, n), F32), jax.ShapeDtypeStruct((4, nl, r, n), MXU)],
        compiler_params=_params(3),
    )(my_c, g, recv)


def _rs_add_chips(pf, recv, my_chip, name):
    _, nl, r, n = pf.shape
    tr = _tile(r, 512)

    def body(j_ref, p_ref, r_ref, o_ref):
        s = p_ref[...]
        for t in range(3):
            s = s + r_ref[t].astype(F32)
        o_ref[...] = s

    return pl.pallas_call(
        body, name=name,
        grid_spec=pltpu.PrefetchScalarGridSpec(
            num_scalar_prefetch=1, grid=(nl, r // tr),
            in_specs=[pl.BlockSpec((None, None, tr, n), lambda l, i, j_ref: (j_ref[0], l, i, 0)),
                      pl.BlockSpec((3, None, tr, n), lambda l, i, j_ref: (0, l, i, 0))],
            out_specs=pl.BlockSpec((None, tr, n), lambda l, i, j_ref: (l, i, 0))),
        out_shape=jax.ShapeDtypeStruct((nl, r, n), F32),
        compiler_params=_params(2),
    )(my_chip, pf, recv)


def _shard_windows(n_shard, count, first=0):
    out = []
    for k in range(first, first + count):
        off = n_shard * k
        a, s = off // LANES, off % LANES
        out.append((a, s, -(-(s + n_shard) // LANES) * LANES))
    return out


def _fit_lanes(x, width):
    have = x.shape[1]
    if have < width:
        return jnp.concatenate([x, jnp.zeros((x.shape[0], width - have), x.dtype)], axis=-1)
    return x[:, :width]


def _interleave_cols(g, n_shard, w_out, name):
    nd, nl, rows, wpad = g.shape
    rb = _tile(rows, 256)
    wins = _shard_windows(n_shard, nd)

    def body(g_ref, o_ref, acc):
        acc[...] = jnp.zeros_like(acc)
        for k, (a, s, win) in enumerate(wins):
            xk = _fit_lanes(g_ref[k].astype(F32), win)
            if s:
                xk = pltpu.roll(xk, s, 1)
            acc[:, a * LANES:a * LANES + win] += xk
        o_ref[...] = acc[...].astype(o_ref.dtype)

    return pl.pallas_call(
        body, name=name, grid=(nl, rows // rb),
        in_specs=[pl.BlockSpec((nd, None, rb, wpad), lambda l, i: (0, l, i, 0))],
        out_specs=pl.BlockSpec((None, rb, w_out), lambda l, i: (l, i, 0)),
        out_shape=jax.ShapeDtypeStruct((nl, rows, w_out), g.dtype),
        scratch_shapes=[pltpu.VMEM((rb, w_out), F32)],
        compiler_params=_params(2),
    )(g)


def _sum_devices(g):
    _, r, n = g.shape

    def body(g_ref, o_ref):
        s = g_ref[0]
        for t in range(1, N_DEV):
            s = s + g_ref[t]
        o_ref[...] = s

    return pl.pallas_call(
        body, name="sum_devices", out_shape=jax.ShapeDtypeStruct((r, n), F32),
        in_specs=[pl.BlockSpec(memory_space=pltpu.VMEM)], out_specs=pl.BlockSpec(memory_space=pltpu.VMEM),
        compiler_params=pltpu.CompilerParams(vmem_limit_bytes=VMEM_LIMIT),
    )(g)


def _mod_fwd(c_all, ada_w, ada_b_cols):
    nl, _, nc = ada_w.shape

    def body(c_ref, w_ref, b_ref, o_ref):
        cv = c_ref[...]
        act = (cv * _sig(cv)).astype(MXU)
        o_ref[...] = _dot(act, w_ref[...].astype(MXU), NN) + b_ref[...]

    return pl.pallas_call(
        body, name="mod_fwd", grid=(nl,),
        in_specs=[_full((16, D)), pl.BlockSpec((None, D, nc), lambda i: (i, 0, 0)),
                  pl.BlockSpec((None, 1, nc), lambda i: (i, 0, 0))],
        out_specs=pl.BlockSpec((None, 16, nc), lambda i: (i, 0, 0)),
        out_shape=jax.ShapeDtypeStruct((nl, 16, nc), F32), compiler_params=_params(1),
    )(c_all, ada_w, ada_b_cols)


def _mod_bwd(c_all, dmod_cols):
    nl, _, nc = dmod_cols.shape

    def body(c_ref, d_ref, o_ref):
        cv = c_ref[...]
        act = (cv * _sig(cv)).astype(MXU)
        o_ref[...] = _dot(act, d_ref[...].astype(MXU), TN)

    return pl.pallas_call(
        body, name="mod_bwd", grid=(nl,),
        in_specs=[_full((16, D)), pl.BlockSpec((None, 16, nc), lambda i: (i, 0, 0))],
        out_specs=pl.BlockSpec((None, D, nc), lambda i: (i, 0, 0)),
        out_shape=jax.ShapeDtypeStruct((nl, D, nc), F32), compiler_params=_params(1),
    )(c_all, dmod_cols)


def _in_proj(x, modrows, vec, w_in):
    L = x.shape[0]
    T = _tile(L, 256)

    def body(x_ref, mod_ref, vec_ref, w_ref, p_ref, h_ref):
        n, _ = _rms(x_ref[...])
        h = n * vec_ref[0:1, :] * (1.0 + mod_ref[1:2, :]) + mod_ref[0:1, :]
        hb = h.astype(MXU)
        h_ref[...] = hb
        p_ref[...] = _dot(hb, w_ref[...], NN)

    return pl.pallas_call(
        body, name="in_proj", grid=(L // T,),
        in_specs=[pl.BlockSpec((T, D), lambda i: (i, 0)), _full((SUB, D)), _full((SUB, D)), _full((D, P_PAD))],
        out_specs=[pl.BlockSpec((T, P_PAD), lambda i: (i, 0)), pl.BlockSpec((T, D), lambda i: (i, 0))],
        out_shape=[jax.ShapeDtypeStruct((L, P_PAD), F32), jax.ShapeDtypeStruct((L, D), MXU)],
        compiler_params=_params(1),
    )(x, modrows, vec, w_in)


def _gate_small(s, sp_ref):
    lane = lax.broadcasted_iota(jnp.int32, s.shape, 1)
    a = -jnp.exp(sp_ref[0:1, :])
    xb = s + sp_ref[1:2, :]
    beta = _sig(s)
    g = a * _softplus(xb)
    return lane, a, xb, beta, g


def _pre_fwd(p, pa, cq, sp):
    L = p.shape[0]
    T = _tile(L, 256)
    scale = HD ** -0.5

    def body(pm_ref, ps_ref, pa_ref, cq_ref, sp_ref, qn_ref, kn_ref, vs_ref, gb_ref, ya_ref, u_carry, q_carry):
        @pl.when(pl.program_id(0) == 0)
        def _():
            u_carry[...] = jnp.zeros_like(u_carry)
            q_carry[...] = jnp.zeros_like(q_carry)

        a_b = pm_ref[:, 0:AW]
        u = pm_ref[:, AW:2 * AW] * pm_ref[:, 2 * AW:3 * AW]
        cu, _ = _conv_fwd(u, pa_ref, 3, u_carry[...])
        u_carry[...] = u[T - SUB:T, :]
        yp = a_b * cu
        ms = _dot_f32(yp * yp, _blockdiag_mean(AW, A_GROUP), NN, exact="b")
        ya_ref[...] = (yp * lax.rsqrt(ms + EPS) * pa_ref[3:4, :]).astype(MXU)

        qkv = pm_ref[:, 3 * AW:3 * AW + 3 * H * HD]
        qc, _ = _conv_fwd(qkv, cq_ref, 4, q_carry[...])
        q_carry[...] = qkv[T - SUB:T, :]
        qs = qc * _sig(qc)
        for h in range(H):
            q = qs[:, h * HD:(h + 1) * HD]
            qn_ref[:, h * HD:(h + 1) * HD] = q * (lax.rsqrt(jnp.sum(q * q, axis=-1, keepdims=True) + EPS) * scale)
            k = qs[:, (H + h) * HD:(H + h + 1) * HD]
            kn_ref[:, h * HD:(h + 1) * HD] = k * lax.rsqrt(jnp.sum(k * k, axis=-1, keepdims=True) + EPS)
        vs_ref[...] = qs[:, 2 * H * HD:3 * H * HD]

        lane, _, _, beta, g = _gate_small(ps_ref[...], sp_ref)
        gb_ref[...] = jnp.where(lane < H, beta, jnp.where(lane < 2 * H, g, 0.0))

    w3 = 3 * AW + 3 * H * HD
    row = lambda i: (i, 0)
    return pl.pallas_call(
        body, name="pre_fwd", grid=(L // T,),
        in_specs=[pl.BlockSpec((T, w3), row), pl.BlockSpec((T, LANES), lambda i: (i, (P_PAD - LANES) // LANES)),
                  _full((SUB, AW)), _full((SUB, 3 * H * HD)), _full((SUB, LANES))],
        out_specs=[pl.BlockSpec((T, H * HD), row)] * 3 + [pl.BlockSpec((T, LANES), row), pl.BlockSpec((T, AW), row)],
        out_shape=[jax.ShapeDtypeStruct((L, H * HD), F32)] * 3
        + [jax.ShapeDtypeStruct((L, LANES), F32), jax.ShapeDtypeStruct((L, AW), MXU)],
        scratch_shapes=[pltpu.VMEM((SUB, AW), F32), pltpu.VMEM((SUB, 3 * H * HD), F32)],
        compiler_params=_params(1),
    )(p, p, pa, cq, sp)


def _gdr_masks():
    r = lax.broadcasted_iota(jnp.int32, (CK, CK), 0)
    c = lax.broadcasted_iota(jnp.int32, (CK, CK), 1)
    return r >= c, r > c


def _head_cols(gbt, h):
    return gbt[:, h:h + 1], gbt[:, H + h:H + h + 1]


def _split(x, parts):
    out = []
    for _ in range(parts):
        hi = x.astype(jnp.bfloat16)
        out.append(hi)
        x = x - hi.astype(F32)
    return out


def _dot_f32(a, b, dims, exact=None):
    if exact == "a":
        ab = a.astype(jnp.bfloat16)
        return sum(_dot(ab, t, dims) for t in _split(b, 3))
    if exact == "b":
        bb = b.astype(jnp.bfloat16)
        return sum(_dot(t, bb, dims) for t in _split(a, 3))
    ah, al = _split(a, 2)
    bh, bl = _split(b, 2)
    return _dot(ah, bh, dims) + _dot(ah, bl, dims) + _dot(al, bh, dims)


def _gdr_consts():
    causal, strict = _gdr_masks()
    return dict(causal=causal, strict=strict, tril=jnp.where(causal, 1.0, 0.0).astype(F32),
                eye=jnp.where(causal & jnp.logical_not(strict), 1.0, 0.0).astype(F32),
                bcast=jnp.full((CK, HD), 1.0 / HD, F32))


def _dots(a, b, dims):
    return [_dot(x, y, dims) for x, y in zip(a, b)]


def _dots_f32(a, b, dims, exact=None):
    n = len(a)
    if exact == "a":
        lhs = [[x.astype(jnp.bfloat16)] * 3 for x in a]
        rhs = [_split(y, 3) for y in b]
    elif exact == "b":
        lhs = [_split(x, 3) for x in a]
        rhs = [[y.astype(jnp.bfloat16)] * 3 for y in b]
    else:
        sa = [_split(x, 2) for x in a]
        sb = [_split(y, 2) for y in b]
        lhs = [[s[0], s[0], s[1]] for s in sa]
        rhs = [[s[0], s[1], s[0]] for s in sb]
    terms = [[_dot(lhs[i][t], rhs[i][t], dims) for i in range(n)] for t in range(3)]
    return [terms[0][i] + terms[1][i] + terms[2][i] for i in range(n)]


def _gdr_local(q, k, v, beta, g, cst, tinv=None):
    n = len(q)
    R = range(n)
    causal, strict = cst["causal"], cst["strict"]
    gc = _dots_f32([cst["tril"]] * n, [jnp.broadcast_to(g[i], (CK, HD)) for i in R], NN, exact="a")
    g_row = _dots_f32([cst["bcast"]] * n, gc, NT, exact="a")
    decay = [jnp.where(causal, jnp.exp(jnp.where(causal, gc[i][:, 0:CK] - g_row[i], 0.0)), 0.0) for i in R]
    eg = [jnp.exp(gc[i]) for i in R]
    gl = [gc[i][CK - 1:CK, :] for i in R]
    ek = [jnp.exp(gl[i] - gc[i]) for i in R]
    cd = [jnp.exp(gl[i]) for i in R]
    kb = [k[i] * beta[i] for i in R]
    pk = _dots(kb, k, NT)
    if tinv is None:
        xp = [-jnp.where(strict, pk[i] * decay[i], 0.0) for i in R]
        tinv = [cst["eye"] + xp[i] for i in R]
        for _ in range(5):
            xp = _dots_f32(xp, xp, NN)
            tx = _dots_f32(tinv, xp, NN)
            tinv = [tinv[i] + tx[i] for i in R]
    u = _dots(tinv, [v[i] * beta[i] for i in R], NN)
    w = _dots(tinv, [kb[i] * eg[i] for i in R], NN)
    qk = _dots(q, k, NT)
    intra = [jnp.where(causal, qk[i] * decay[i], 0.0) for i in R]
    return dict(decay=decay, eg=eg, ek=ek, cd=cd, kb=kb, pk=pk, tinv=tinv, u=u, w=w, qk=qk, intra=intra,
                q_dec=[q[i] * eg[i] for i in R], k_dec=[k[i] * ek[i] for i in R])


GDR_SUB = 4


def _gdr_fwd(qn, kn, vs, gb):
    L = qn.shape[0]
    nc = L // CK
    cb = min(8, nc)
    rb = cb * CK
    nb = nc // cb
    nsub = GDR_SUB if cb % GDR_SUB == 0 else 1

    def body(q_ref, k_ref, v_ref, gb_ref, o_ref, st_ref, ti_ref, s_ref):
        @pl.when(pl.program_id(0) == 0)
        def _():
            s_ref[...] = jnp.zeros_like(s_ref)

        cst = _gdr_consts()
        heads = range(H)

        def group(gi, carry):
            rows = [pl.ds(pl.multiple_of((gi * nsub + j) * CK, CK), CK) for j in range(nsub)]
            chains = [(j, h) for j in range(nsub) for h in heads]
            gbt = [gb_ref[rows[j], :] for j in range(nsub)]
            cols = lambda h: slice(h * HD, (h + 1) * HD)
            t = _gdr_local([q_ref[rows[j], cols(h)] for j, h in chains], [k_ref[rows[j], cols(h)] for j, h in chains],
                           [v_ref[rows[j], cols(h)] for j, h in chains],
                           [_head_cols(gbt[j], h)[0] for j, h in chains], [_head_cols(gbt[j], h)[1] for j, h in chains], cst)
            s = [s_ref[h] for h in heads]
            for j in range(nsub):
                at = lambda key: [t[key][j * H + h] for h in heads]
                for h in heads:
                    st_ref[h, gi * nsub + j] = s[h]
                    ti_ref[h, gi * nsub + j] = t["tinv"][j * H + h]
                ws = _dots(at("w"), s, NN)
                v_new = [u_h - ws_h for u_h, ws_h in zip(at("u"), ws)]
                o_s = _dots(at("q_dec"), s, NN)
                o_v = _dots(at("intra"), v_new, NN)
                kv = _dots(at("k_dec"), v_new, TN)
                cd = at("cd")
                for h in heads:
                    o_ref[rows[j], cols(h)] = o_s[h] + o_v[h]
                s = [s[h] * cd[h] + kv[h] for h in heads]
            for h in heads:
                s_ref[h] = s[h]
            return carry

        lax.fori_loop(0, cb // nsub, group, 0)

    blk = pl.BlockSpec((rb, H * HD), lambda b: (b, 0))
    return pl.pallas_call(
        body, name="gdr_fwd", grid=(nb,),
        in_specs=[blk, blk, blk, pl.BlockSpec((rb, LANES), lambda b: (b, 0))],
        out_specs=[blk, pl.BlockSpec((H, cb, HD, HD), lambda b: (0, b, 0, 0)),
                   pl.BlockSpec((H, cb, CK, CK), lambda b: (0, b, 0, 0))],
        out_shape=[jax.ShapeDtypeStruct((L, H * HD), F32), jax.ShapeDtypeStruct((H, nc, HD, HD), F32),
                   jax.ShapeDtypeStruct((H, nc, CK, CK), F32)],
        scratch_shapes=[pltpu.VMEM((H, HD, HD), F32)],
        compiler_params=_params(1),
    )(qn, kn, vs, gb)


def _gdr_bwd(qn, kn, vs, gb, states, tinvs, do):
    L = qn.shape[0]
    nc = L // CK
    cb = min(8, nc)
    rb = cb * CK
    nb = nc // cb
    nsub = GDR_SUB if cb % GDR_SUB == 0 else 1

    def body(q_ref, k_ref, v_ref, gb_ref, st_ref, ti_ref, do_ref, dq_ref, dk_ref, dv_ref, dgb_ref, ds_ref):
        @pl.when(pl.program_id(0) == 0)
        def _():
            ds_ref[...] = jnp.zeros_like(ds_ref)

        cst = _gdr_consts()
        causal, strict = cst["causal"], cst["strict"]
        ones = jnp.ones((CK, HD), F32)
        row = lax.broadcasted_iota(jnp.int32, (CK, HD), 0)
        lane = lax.broadcasted_iota(jnp.int32, (CK, LANES), 1)

        heads = range(H)
        rsum = lambda x: jnp.sum(x, axis=-1, keepdims=True)

        def group(gj, carry):
            gi = cb // nsub - 1 - gj
            rows = [pl.ds(pl.multiple_of((gi * nsub + j) * CK, CK), CK) for j in range(nsub)]
            chains = [(j, h) for j in range(nsub) for h in heads]
            gbt = [gb_ref[rows[j], :] for j in range(nsub)]
            cols = lambda h: slice(h * HD, (h + 1) * HD)
            q_all = [q_ref[rows[j], cols(h)] for j, h in chains]
            k_all = [k_ref[rows[j], cols(h)] for j, h in chains]
            v_all = [v_ref[rows[j], cols(h)] for j, h in chains]
            beta_all = [_head_cols(gbt[j], h)[0] for j, h in chains]
            t = _gdr_local(q_all, k_all, v_all, beta_all, [_head_cols(gbt[j], h)[1] for j, h in chains], cst,
                           tinv=[ti_ref[h, gi * nsub + j] for j, h in chains])
            ds_out = [ds_ref[h] for h in heads]
            for j in reversed(range(nsub)):
                at = lambda key: [t[key][j * H + h] for h in heads]
                pick = lambda lst: [lst[j * H + h] for h in heads]
                q, k, v, beta = pick(q_all), pick(k_all), pick(v_all), pick(beta_all)
                u, w, tinv, decay = at("u"), at("w"), at("tinv"), at("decay")
                eg, ek, cd, kb = at("eg"), at("ek"), at("cd"), at("kb")
                q_dec, k_dec, intra, pk, qk = at("q_dec"), at("k_dec"), at("intra"), at("pk"), at("qk")
                s = [st_ref[h, gi * nsub + j] for h in heads]
                dout = [do_ref[rows[j], cols(h)] for h in heads]

                ws = _dots(w, s, NN)
                v_new = [u[h] - ws[h] for h in heads]
                dq_dec = _dots(dout, s, NT)
                qd = _dots(q_dec, dout, TN)
                di = _dots(dout, v_new, NT)
                dintra = [jnp.where(causal, di[h], 0.0) for h in heads]
                ido = _dots(intra, dout, TN)
                kds = _dots(k_dec, ds_out, NN)
                dv_new = [ido[h] + kds[h] for h in heads]
                dk_dec = _dots(v_new, ds_out, NT)
                dcd = [jnp.sum(jnp.sum(ds_out[h] * s[h], axis=1, keepdims=True), axis=0, keepdims=True) for h in heads]
                dvs = _dots(dv_new, s, NT)
                dw = [-dvs[h] for h in heads]
                wdv = _dots(w, dv_new, TN)
                ds_new = [qd[h] + ds_out[h] * cd[h] - wdv[h] for h in heads]
                dru = _dots(tinv, dv_new, TN)
                drw = _dots(tinv, dw, TN)
                dl1 = _dots(dru, u, NT)
                dl2 = _dots(drw, w, NT)
                dlower = [-jnp.where(strict, dl1[h] + dl2[h], 0.0) for h in heads]
                dv = [dru[h] * beta[h] for h in heads]
                dbeta = [rsum(dru[h] * v[h]) for h in heads]
                dgc = [rsum(drw[h] * kb[h]) * eg[h] for h in heads]
                dpk = [dlower[h] * decay[h] for h in heads]
                dqk = [dintra[h] * decay[h] for h in heads]
                dpk_k = _dots(dpk, k, NN)
                dkb = [drw[h] * eg[h] + dpk_k[h] for h in heads]
                dk1 = _dots(dpk, kb, TN)
                dq1 = _dots(dqk, k, NN)
                dk2 = _dots(dqk, q, TN)
                m = [(dlower[h] * pk[h] + dintra[h] * qk[h]) * decay[h] for h in heads]
                mcol = _dots_f32(m, [ones] * H, TN, exact="b")
                e = [rsum(dk_dec[h] * k_dec[h]) for h in heads]
                dgl = [jnp.sum(e[h], axis=0, keepdims=True) + dcd[h] * cd[h] for h in heads]
                dgc = [dgc[h] + rsum(m[h]) - mcol[h] + rsum(dq_dec[h] * q_dec[h]) - e[h]
                       + jnp.where(row == CK - 1, dgl[h], 0.0) for h in heads]
                dg = _dots_f32([cst["tril"]] * H, dgc, TN, exact="a")
                dgb = jnp.zeros((CK, LANES), F32)
                for h in heads:
                    dq_ref[rows[j], cols(h)] = dq1[h] + dq_dec[h] * eg[h]
                    dk_ref[rows[j], cols(h)] = dk1[h] + dk2[h] + dk_dec[h] * ek[h] + dkb[h] * beta[h]
                    dv_ref[rows[j], cols(h)] = dv[h]
                    db = dbeta[h] + rsum(dkb[h] * k[h])
                    dgb = dgb + jnp.where(lane == h, db, 0.0) + jnp.where(lane == H + h, dg[h], 0.0)
                dgb_ref[rows[j], :] = dgb
                ds_out = ds_new
            for h in heads:
                ds_ref[h] = ds_out[h]
            return carry

        lax.fori_loop(0, cb // nsub, group, 0)

    blk = pl.BlockSpec((rb, H * HD), lambda b: (nb - 1 - b, 0))
    sblk = pl.BlockSpec((rb, LANES), lambda b: (nb - 1 - b, 0))
    return pl.pallas_call(
        body, name="gdr_bwd", grid=(nb,),
        in_specs=[blk, blk, blk, sblk, pl.BlockSpec((H, cb, HD, HD), lambda b: (0, nb - 1 - b, 0, 0)),
                  pl.BlockSpec((H, cb, CK, CK), lambda b: (0, nb - 1 - b, 0, 0)), blk],
        out_specs=[blk, blk, blk, sblk],
        out_shape=[jax.ShapeDtypeStruct((L, H * HD), F32)] * 3 + [jax.ShapeDtypeStruct((L, LANES), F32)],
        scratch_shapes=[pltpu.VMEM((H, HD, HD), F32)],
        compiler_params=_params(1),
    )(qn, kn, vs, gb, states, tinvs, do)


def _post_fwd(o, p, ya, x, modrows, sp, w_out):
    L = x.shape[0]
    T = _tile(L, 256)

    def body(o_ref, z_ref, ya_ref, x_ref, mod_ref, sp_ref, w_ref, y_ref, x2_ref, yb_ref):
        ndw = sp_ref[2:3, :]
        z = z_ref[...]
        sz = z * _sig(z)
        parts = []
        for h in range(H):
            n, _ = _rms(o_ref[:, h * HD:(h + 1) * HD])
            parts.append(n * ndw * sz[:, h * HD:(h + 1) * HD])
        yb = jnp.concatenate(parts, axis=-1).astype(MXU)
        yb_ref[...] = yb
        y = _dot(ya_ref[...], w_ref[0:AW, :], NN) + _dot(yb, w_ref[AW:2 * AW, :], NN)
        y_ref[...] = y
        x2_ref[...] = x_ref[...] + mod_ref[2:3, :] * y

    row = lambda i: (i, 0)
    zcol = (3 * AW + 3 * H * HD) // (H * HD)
    return pl.pallas_call(
        body, name="post_fwd", grid=(L // T,),
        in_specs=[pl.BlockSpec((T, H * HD), row), pl.BlockSpec((T, H * HD), lambda i: (i, zcol)),
                  pl.BlockSpec((T, AW), row), pl.BlockSpec((T, D), row), _full((SUB, D)), _full((SUB, LANES)),
                  _full((D, D))],
        out_specs=[pl.BlockSpec((T, D), row), pl.BlockSpec((T, D), row), pl.BlockSpec((T, H * HD), row)],
        out_shape=[jax.ShapeDtypeStruct((L, D), F32), jax.ShapeDtypeStruct((L, D), F32),
                   jax.ShapeDtypeStruct((L, H * HD), MXU)],
        compiler_params=_params(1),
    )(o, p, ya, x, modrows, sp, w_out)


FF_COLS = 2
FF_CW = DFF // FF_COLS
FF_ROWS = 256


def _ffn_fwd(x2, modrows, vec, w_up, cff, w_down):
    L = x2.shape[0]
    T = _tile(L, FF_ROWS)
    nj = FF_COLS

    def body(x_ref, mod_ref, vec_ref, wg_ref, wu_ref, cg_ref, cu_ref, wd_ref,
             h_ref, gp_ref, up_ref, f_ref, d_ref, x3_ref, h_s, acc, carry_g, carry_u):
        i, j = pl.program_id(0), pl.program_id(1)

        @pl.when(i == 0)
        def _():
            carry_g[j] = jnp.zeros((SUB, FF_CW), F32)
            carry_u[j] = jnp.zeros((SUB, FF_CW), F32)

        @pl.when(j == 0)
        def _():
            n, _ = _rms(x_ref[...])
            hb = (n * vec_ref[1:2, :] * (1.0 + mod_ref[4:5, :]) + mod_ref[3:4, :]).astype(MXU)
            h_s[...] = hb
            h_ref[...] = hb
            acc[...] = jnp.zeros_like(acc)

        hb = h_s[...]
        g = _dot(hb, wg_ref[...], NN)
        u = _dot(hb, wu_ref[...], NN)
        gp_ref[...] = g
        up_ref[...] = u
        gc, _ = _conv_fwd(g, cg_ref, 3, carry_g[j])
        uc, _ = _conv_fwd(u, cu_ref, 3, carry_u[j])
        carry_g[j] = g[T - SUB:T, :]
        carry_u[j] = u[T - SUB:T, :]
        fb = (gc * _sig(gc) * uc).astype(MXU)
        f_ref[...] = fb
        acc[...] += _dot(fb, wd_ref[...], NN)

        @pl.when(j == nj - 1)
        def _():
            dv = acc[...]
            d_ref[...] = dv
            x3_ref[...] = x_ref[...] + mod_ref[5:6, :] * dv

    row = lambda i, j: (i, 0)
    col = lambda i, j: (i, j)
    return pl.pallas_call(
        body, name="ffn_fwd", grid=(L // T, nj),
        in_specs=[pl.BlockSpec((T, D), row), _full((SUB, D)), _full((SUB, D)),
                  pl.BlockSpec((D, FF_CW), lambda i, j: (0, j)), pl.BlockSpec((D, FF_CW), lambda i, j: (0, nj + j)),
                  pl.BlockSpec((SUB, FF_CW), lambda i, j: (0, j)), pl.BlockSpec((SUB, FF_CW), lambda i, j: (0, nj + j)),
                  pl.BlockSpec((FF_CW, D), lambda i, j: (j, 0))],
        out_specs=[pl.BlockSpec((T, D), row), pl.BlockSpec((T, FF_CW), col), pl.BlockSpec((T, FF_CW), col),
                   pl.BlockSpec((T, FF_CW), col), pl.BlockSpec((T, D), row), pl.BlockSpec((T, D), row)],
        out_shape=[jax.ShapeDtypeStruct((L, D), MXU), jax.ShapeDtypeStruct((L, DFF), F32),
                   jax.ShapeDtypeStruct((L, DFF), F32), jax.ShapeDtypeStruct((L, DFF), MXU),
                   jax.ShapeDtypeStruct((L, D), F32), jax.ShapeDtypeStruct((L, D), F32)],
        scratch_shapes=[pltpu.VMEM((T, D), MXU), pltpu.VMEM((T, D), F32),
                        pltpu.VMEM((nj, SUB, FF_CW), F32), pltpu.VMEM((nj, SUB, FF_CW), F32)],
        compiler_params=_params(2),
    )(x2, modrows, vec, w_up, w_up, cff, cff, w_down)


def _final(x, target, nf):
    L = x.shape[0]
    T = _tile(L, 256)

    def body(x_ref, t_ref, nf_ref, dx_ref, acc_ref):
        @pl.when(pl.program_id(0) == 0)
        def _():
            acc_ref[...] = jnp.zeros_like(acc_ref)

        n, r = _rms(x_ref[...])
        w = nf_ref[0:1, :]
        err = n * w - t_ref[...]
        acc_ref[0:1, :] += (0.5 / D) * _sum0(err * err)
        dy = err * (1.0 / D)
        acc_ref[1:2, :] += _sum0(dy * n)
        dx_ref[...] = _rms_bwd(dy * w, n, r)

    row = lambda i: (i, 0)
    return pl.pallas_call(
        body, name="final_norm_loss", grid=(L // T,),
        in_specs=[pl.BlockSpec((T, D), row), pl.BlockSpec((T, D), row), _full((SUB, D))],
        out_specs=[pl.BlockSpec((T, D), row), _full((SUB, D))],
        out_shape=[jax.ShapeDtypeStruct((L, D), F32), jax.ShapeDtypeStruct((SUB, D), F32)],
        compiler_params=_params(1),
    )(x, target, nf)


def _ffn_bwd(dx3, d, x2, modrows, vec, gpre, upre, cff, w_down, w_up):
    L = dx3.shape[0]
    T = _tile(L, FF_ROWS)
    ni, nj = L // T, FF_COLS
    hb_per_t = T // SUB

    def body(dx3_ref, d_ref, x2_ref, mod_ref, vec_ref, gp_ref, up_ref, gph_ref, uph_ref, cg_ref, cu_ref,
             wd_ref, wg_ref, wu_ref,
             dd_ref, dgp_ref, dup_ref, dx2_ref, accv_ref, dcg_ref, dcu_ref,
             dd_s, acch, carry_g, carry_u):
        i, j = pl.program_id(0), pl.program_id(1)
        ri = ni - 1 - i

        @pl.when((i == 0) & (j == 0))
        def _():
            accv_ref[...] = jnp.zeros_like(accv_ref)
            dcg_ref[...] = jnp.zeros_like(dcg_ref)
            dcu_ref[...] = jnp.zeros_like(dcu_ref)

        @pl.when(i == 0)
        def _():
            carry_g[j] = jnp.zeros((SUB, FF_CW), F32)
            carry_u[j] = jnp.zeros((SUB, FF_CW), F32)

        @pl.when(j == 0)
        def _():
            dx3v = dx3_ref[...]
            accv_ref[0:1, :] += _sum0(dx3v * d_ref[...])
            ddb = (mod_ref[5:6, :] * dx3v).astype(MXU)
            dd_s[...] = ddb
            dd_ref[...] = ddb
            acch[...] = jnp.zeros_like(acch)

        ddb = dd_s[...]
        g, u = gp_ref[...], up_ref[...]
        keep = jnp.where(ri == 0, 0.0, 1.0)
        gc, gsh = _conv_fwd(g, cg_ref, 3, gph_ref[...] * keep)
        uc, ush = _conv_fwd(u, cu_ref, 3, uph_ref[...] * keep)
        sg = _sig(gc)
        df = _dot(ddb, wd_ref[...], NT)
        duc = df * (gc * sg)
        dgc = df * uc * (sg * (1.0 + gc * (1.0 - sg)))
        for s in range(3):
            dcg_ref[j, 2 - s:3 - s, :] += _sum0(dgc * gsh[s])
            dcu_ref[j, 2 - s:3 - s, :] += _sum0(duc * ush[s])
        dg = _conv_bwd_in(dgc, cg_ref, 3, carry_g[j]).astype(MXU)
        du = _conv_bwd_in(duc, cu_ref, 3, carry_u[j]).astype(MXU)
        carry_g[j] = dgc[0:SUB, :]
        carry_u[j] = duc[0:SUB, :]
        dgp_ref[...] = dg
        dup_ref[...] = du
        acch[...] += _dot(dg, wg_ref[...], NT) + _dot(du, wu_ref[...], NT)

        @pl.when(j == nj - 1)
        def _():
            dh = acch[...]
            n, r = _rms(x2_ref[...])
            nw, sc = vec_ref[1:2, :], mod_ref[4:5, :]
            accv_ref[1:2, :] += _sum0(dh)
            accv_ref[2:3, :] += _sum0(dh * n * nw)
            accv_ref[3:4, :] += _sum0(dh * n * (1.0 + sc))
            dx2_ref[...] = _rms_bwd(dh * nw * (1.0 + sc), n, r) + dx3_ref[...]

    row = lambda i, j: (ni - 1 - i, 0)
    col = lambda i, j: (ni - 1 - i, j)
    halo = lambda i, j: (jnp.maximum((ni - 1 - i) * hb_per_t - 1, 0), j)
    return pl.pallas_call(
        body, name="ffn_bwd", grid=(ni, nj),
        in_specs=[pl.BlockSpec((T, D), row), pl.BlockSpec((T, D), row), pl.BlockSpec((T, D), row),
                  _full((SUB, D)), _full((SUB, D)),
                  pl.BlockSpec((T, FF_CW), col), pl.BlockSpec((T, FF_CW), col),
                  pl.BlockSpec((SUB, FF_CW), halo), pl.BlockSpec((SUB, FF_CW), halo),
                  pl.BlockSpec((SUB, FF_CW), lambda i, j: (0, j)), pl.BlockSpec((SUB, FF_CW), lambda i, j: (0, nj + j)),
                  pl.BlockSpec((FF_CW, D), lambda i, j: (j, 0)),
                  pl.BlockSpec((D, FF_CW), lambda i, j: (0, j)), pl.BlockSpec((D, FF_CW), lambda i, j: (0, nj + j))],
        out_specs=[pl.BlockSpec((T, D), row), pl.BlockSpec((T, FF_CW), col), pl.BlockSpec((T, FF_CW), col),
                   pl.BlockSpec((T, D), row), _full((SUB, D)), _full((nj, SUB, FF_CW)), _full((nj, SUB, FF_CW))],
        out_shape=[jax.ShapeDtypeStruct((L, D), MXU), jax.ShapeDtypeStruct((L, DFF), MXU),
                   jax.ShapeDtypeStruct((L, DFF), MXU), jax.ShapeDtypeStruct((L, D), F32),
                   jax.ShapeDtypeStruct((SUB, D), F32), jax.ShapeDtypeStruct((nj, SUB, FF_CW), F32),
                   jax.ShapeDtypeStruct((nj, SUB, FF_CW), F32)],
        scratch_shapes=[pltpu.VMEM((T, D), MXU), pltpu.VMEM((T, D), F32),
                        pltpu.VMEM((nj, SUB, FF_CW), F32), pltpu.VMEM((nj, SUB, FF_CW), F32)],
        compiler_params=_params(2),
    )(dx3, d, x2, modrows, vec, gpre, upre, gpre, upre, cff, cff, w_down, w_up, w_up)


def _post_bwd(dx2, y, o, p, modrows, sp, w_out):
    L = dx2.shape[0]
    T = _tile(L, 256)

    def body(dx2_ref, y_ref, o_ref, z_ref, mod_ref, sp_ref, w_ref, dy_ref, do_ref, dz_ref, dya_ref, accv_ref, accs_ref):
        @pl.when(pl.program_id(0) == 0)
        def _():
            accv_ref[...] = jnp.zeros_like(accv_ref)
            accs_ref[...] = jnp.zeros_like(accs_ref)

        dx2v = dx2_ref[...]
        accv_ref[0:1, :] += _sum0(dx2v * y_ref[...])
        dyb = (mod_ref[2:3, :] * dx2v).astype(MXU)
        dy_ref[...] = dyb
        dyc = _dot(dyb, w_ref[...], NT)
        dya_ref[...] = dyc[:, 0:AW]
        ndw = sp_ref[2:3, :]
        z = z_ref[...]
        sgz = _sig(z)
        dsz = sgz * (1.0 + z * (1.0 - sgz))
        dndw = jnp.zeros((1, HD), F32)
        for h in range(H):
            sl = slice(h * HD, (h + 1) * HD)
            n, r = _rms(o_ref[:, sl])
            dyh = dyc[:, AW + h * HD:AW + (h + 1) * HD]
            zh = z[:, sl]
            don = dyh * (zh * sgz[:, sl])
            dz_ref[:, sl] = dyh * (n * ndw) * dsz[:, sl]
            dndw = dndw + _sum0(don * n)
            do_ref[:, sl] = _rms_bwd(don * ndw, n, r)
        accs_ref[0:1, :] += dndw

    row = lambda i: (i, 0)
    zcol = (3 * AW + 3 * H * HD) // (H * HD)
    return pl.pallas_call(
        body, name="post_bwd", grid=(L // T,),
        in_specs=[pl.BlockSpec((T, D), row), pl.BlockSpec((T, D), row), pl.BlockSpec((T, H * HD), row),
                  pl.BlockSpec((T, H * HD), lambda i: (i, zcol)), _full((SUB, D)), _full((SUB, LANES)), _full((D, D))],
        out_specs=[pl.BlockSpec((T, D), row)] + [pl.BlockSpec((T, H * HD), row)] * 3 + [_full((SUB, D)), _full((SUB, LANES))],
        out_shape=[jax.ShapeDtypeStruct((L, D), MXU)] + [jax.ShapeDtypeStruct((L, H * HD), F32)] * 3
        + [jax.ShapeDtypeStruct((SUB, D), F32), jax.ShapeDtypeStruct((SUB, LANES), F32)],
        compiler_params=_params(1),
    )(dx2, y, o, p, modrows, sp, w_out)


def _pre_bwd(p, dqn, dkn, dvs, dya, dz, dgb, pa, cq, sp):
    L = p.shape[0]
    T = _tile(L, 256)
    ni = L // T
    scale = HD ** -0.5
    w3 = 3 * AW + 3 * H * HD
    hb_per_t = T // SUB

    def body(pm_ref, ph_ref, ps_ref, dq_ref, dk_ref, dv_ref, dya_ref, dz_ref, dgb_ref, pa_ref, cq_ref, sp_ref,
             dp_ref, dpa_ref, dcq_ref, dsp_ref, carry_u, carry_q):
        i = pl.program_id(0)
        ri = ni - 1 - i

        @pl.when(i == 0)
        def _():
            dpa_ref[...] = jnp.zeros_like(dpa_ref)
            dcq_ref[...] = jnp.zeros_like(dcq_ref)
            dsp_ref[...] = jnp.zeros_like(dsp_ref)
            carry_u[...] = jnp.zeros_like(carry_u)
            carry_q[...] = jnp.zeros_like(carry_q)

        keep = jnp.where(ri == 0, 0.0, 1.0)
        a_b, a_c, a_x = pm_ref[:, 0:AW], pm_ref[:, AW:2 * AW], pm_ref[:, 2 * AW:3 * AW]
        u = a_c * a_x
        hu = ph_ref[:, AW:2 * AW] * ph_ref[:, 2 * AW:3 * AW] * keep
        cu, ush = _conv_fwd(u, pa_ref, 3, hu)
        yp = a_b * cu
        bd = _blockdiag_mean(AW, A_GROUP)
        ra = lax.rsqrt(_dot_f32(yp * yp, bd, NN, exact="b") + EPS)
        na = yp * ra
        dya = dya_ref[...]
        dpa_ref[3:4, :] += _sum0(dya * na)
        dna = dya * pa_ref[3:4, :]
        dyp = ra * (dna - na * _dot_f32(dna * na, bd, NN, exact="b"))
        dcu = dyp * a_b
        for s in range(3):
            dpa_ref[2 - s:3 - s, :] += _sum0(dcu * ush[s])
        du = _conv_bwd_in(dcu, pa_ref, 3, carry_u[...])
        carry_u[...] = dcu[0:SUB, :]
        dp_ref[:, 0:AW] = (dyp * cu).astype(MXU)
        dp_ref[:, AW:2 * AW] = (du * a_x).astype(MXU)
        dp_ref[:, 2 * AW:3 * AW] = (du * a_c).astype(MXU)

        qkv = pm_ref[:, 3 * AW:w3]
        qc, qsh = _conv_fwd(qkv, cq_ref, 4, ph_ref[:, 3 * AW:w3] * keep)
        sg = _sig(qc)
        qs = qc * sg
        parts = []
        for h in range(H):
            q = qs[:, h * HD:(h + 1) * HD]
            rq = lax.rsqrt(jnp.sum(q * q, axis=-1, keepdims=True) + EPS)
            parts.append(_l2_bwd(dq_ref[:, h * HD:(h + 1) * HD] * scale, q * rq, rq))
        for h in range(H):
            k = qs[:, (H + h) * HD:(H + h + 1) * HD]
            rk = lax.rsqrt(jnp.sum(k * k, axis=-1, keepdims=True) + EPS)
            parts.append(_l2_bwd(dk_ref[:, h * HD:(h + 1) * HD], k * rk, rk))
        parts.append(dv_ref[...])
        dqc = jnp.concatenate(parts, axis=-1) * (sg * (1.0 + qc * (1.0 - sg)))
        for s in range(4):
            dcq_ref[3 - s:4 - s, :] += _sum0(dqc * qsh[s])
        dp_ref[:, 3 * AW:w3] = _conv_bwd_in(dqc, cq_ref, 4, carry_q[...]).astype(MXU)
        carry_q[...] = dqc[0:SUB, :]
        dp_ref[:, w3:w3 + H * HD] = dz_ref[...].astype(MXU)

        lane, a, xb, beta, g = _gate_small(ps_ref[...], sp_ref)
        dgb = dgb_ref[...]
        dbeta = jnp.where(lane < H, dgb, 0.0)
        dg = jnp.where((lane >= H) & (lane < 2 * H), dgb, 0.0)
        dalpha = dg * a * _sig(xb)
        dsp_ref[0:1, :] += _sum0(dg * g)
        dsp_ref[1:2, :] += _sum0(dalpha)
        dp_ref[:, w3 + H * HD:P_PAD] = (dbeta * beta * (1.0 - beta) + dalpha).astype(MXU)

    row = lambda i: (ni - 1 - i, 0)
    halo = lambda i: (jnp.maximum((ni - 1 - i) * hb_per_t - 1, 0), 0)
    hrow = pl.BlockSpec((T, H * HD), row)
    return pl.pallas_call(
        body, name="pre_bwd", grid=(ni,),
        in_specs=[pl.BlockSpec((T, w3), row), pl.BlockSpec((SUB, w3), halo),
                  pl.BlockSpec((T, LANES), lambda i: (ni - 1 - i, (P_PAD - LANES) // LANES)),
                  hrow, hrow, hrow, pl.BlockSpec((T, AW), row), hrow,
                  pl.BlockSpec((T, LANES), row),
                  _full((SUB, AW)), _full((SUB, 3 * H * HD)), _full((SUB, LANES))],
        out_specs=[pl.BlockSpec((T, P_PAD), row), _full((SUB, AW)), _full((SUB, 3 * H * HD)), _full((SUB, LANES))],
        out_shape=[jax.ShapeDtypeStruct((L, P_PAD), MXU), jax.ShapeDtypeStruct((SUB, AW), F32),
                   jax.ShapeDtypeStruct((SUB, 3 * H * HD), F32), jax.ShapeDtypeStruct((SUB, LANES), F32)],
        scratch_shapes=[pltpu.VMEM((SUB, AW), F32), pltpu.VMEM((SUB, 3 * H * HD), F32)],
        compiler_params=_params(1),
    )(p, p, p, dqn, dkn, dvs, dya, dz, dgb, pa, cq, sp)


def _in_bwd(dp, w_in, x, dx2, modrows, vec):
    L = x.shape[0]
    T = _tile(L, 256)

    def body(dp_ref, w_ref, x_ref, dx2_ref, mod_ref, vec_ref, dx_ref, accv_ref):
        @pl.when(pl.program_id(0) == 0)
        def _():
            accv_ref[...] = jnp.zeros_like(accv_ref)

        dh = _dot(dp_ref[...], w_ref[...], NT)
        n, r = _rms(x_ref[...])
        nw, sc = vec_ref[0:1, :], mod_ref[1:2, :]
        accv_ref[0:1, :] += _sum0(dh)
        accv_ref[1:2, :] += _sum0(dh * n * nw)
        accv_ref[2:3, :] += _sum0(dh * n * (1.0 + sc))
        dx_ref[...] = _rms_bwd(dh * nw * (1.0 + sc), n, r) + dx2_ref[...]

    row = lambda i: (i, 0)
    return pl.pallas_call(
        body, name="in_bwd", grid=(L // T,),
        in_specs=[pl.BlockSpec((T, P_PAD), row), _full((D, P_PAD)), pl.BlockSpec((T, D), row),
                  pl.BlockSpec((T, D), row), _full((SUB, D)), _full((SUB, D))],
        out_specs=[pl.BlockSpec((T, D), row), _full((SUB, D))],
        out_shape=[jax.ShapeDtypeStruct((L, D), F32), jax.ShapeDtypeStruct((SUB, D), F32)],
        compiler_params=_params(1),
    )(dp, w_in, x, dx2, modrows, vec)


def _wgrad(a, b, tm, tn, name):
    L, m = a.shape
    n = b.shape[1]
    tl = _tile(L, 512)
    tm, tn = _tile(m, tm), _tile(n, tn)
    nl = L // tl

    def body(a_ref, b_ref, o_ref):
        @pl.when(pl.program_id(2) == 0)
        def _():
            o_ref[...] = jnp.zeros_like(o_ref)

        o_ref[...] += _dot(a_ref[...], b_ref[...], TN)

    return pl.pallas_call(
        body, name=name, grid=(m // tm, n // tn, nl),
        in_specs=[pl.BlockSpec((tl, tm), lambda i, j, l: (l, i)), pl.BlockSpec((tl, tn), lambda i, j, l: (l, j))],
        out_specs=pl.BlockSpec((tm, tn), lambda i, j, l: (i, j)),
        out_shape=jax.ShapeDtypeStruct((m, n), F32), compiler_params=_params(3),
    )(a, b)


def _wgrad_cols(a, b, tm, n_shard, wpad, count, name):
    L, m = a.shape
    n = b.shape[1]
    tl = _tile(L, 512)
    tm = _tile(m, tm)
    nl = L // tl
    wins = _shard_windows(n_shard, count)
    assert all(a_ * LANES + win <= n for a_, _, win in wins), (wins, n)

    def body(a_ref, b_ref, o_ref, acc):
        @pl.when(pl.program_id(1) == 0)
        def _():
            acc[...] = jnp.zeros_like(acc)

        acc[...] += _dot(a_ref[...], b_ref[...], TN)

        @pl.when(pl.program_id(1) == nl - 1)
        def _():
            for k, (a_, s, win) in enumerate(wins):
                xk = acc[:, a_ * LANES:a_ * LANES + win]
                if s:
                    xk = pltpu.roll(xk, win - s, 1)
                o_ref[k] = _fit_lanes(xk, wpad)

    return pl.pallas_call(
        body, name=name, grid=(m // tm, nl),
        in_specs=[pl.BlockSpec((tl, tm), lambda i, l: (l, i)), pl.BlockSpec((tl, n), lambda i, l: (l, 0))],
        out_specs=pl.BlockSpec((count, tm, wpad), lambda i, l: (0, i, 0)),
        out_shape=jax.ShapeDtypeStruct((count, m, wpad), F32),
        scratch_shapes=[pltpu.VMEM((tm, n), F32)],
        compiler_params=_params(2),
    )(a, b)


def _adamw(w, g, m, v, name):
    r, n = w.shape
    tr = _tile(r, 512)
    bc1 = 1.0 - ADAM_B1 ** ADAM_STEP
    bc2 = 1.0 - ADAM_B2 ** ADAM_STEP

    def body(w_ref, g_ref, m_ref, v_ref, d_ref, nm_ref, nv_ref):
        gv = g_ref[...]
        nm = ADAM_B1 * m_ref[...] + (1.0 - ADAM_B1) * gv
        nv = ADAM_B2 * v_ref[...] + (1.0 - ADAM_B2) * (gv * gv)
        nm_ref[...] = nm
        nv_ref[...] = nv
        d_ref[...] = -ADAM_LR * ((nm / bc1) / (jnp.sqrt(nv / bc2) + ADAM_EPS) + ADAM_WD * w_ref[...])

    spec = pl.BlockSpec((tr, n), lambda i: (i, 0))
    return pl.pallas_call(
        body, name=name, grid=(r // tr,), in_specs=[spec] * 4, out_specs=[spec] * 3,
        out_shape=[jax.ShapeDtypeStruct((r, n), F32)] * 3, compiler_params=_params(1),
    )(w, g, m, v)


def _rows8(rows, width):
    out = jnp.zeros((SUB, width), F32)
    for r, vrow in enumerate(rows):
        out = out.at[r, :vrow.shape[0]].set(vrow)
    return out


def _at_lanes(v4, start):
    return jnp.zeros((LANES,), F32).at[start:start + v4.shape[0]].set(v4)


def _pad_rows(flat, mult):
    n = flat.shape[0]
    pad = (-n) % mult
    return jnp.pad(flat, (0, pad)) if pad else flat


IN_PAD = 512
UP_PAD = 768


def _local_fwd_bwd(x, target, mod_full, small_w, full_w):
    norm1_w, norm2_w, norm_a_w, a_log, dt_bias, norm_dn_w, norm_f_w = small_w
    w_in_f, w_out_f, w_up_f, w_down_f, conv_a_f, conv_q_f, conv_f_f = full_w

    def layer_params(i):
        modrows = jnp.concatenate([mod_full[i], jnp.zeros((SUB - N_MOD, D), F32)], axis=0)
        vec = _rows8([norm1_w[i], norm2_w[i]], D)
        pa = _rows8([conv_a_f[i, 0], conv_a_f[i, 1], conv_a_f[i, 2], norm_a_w[i]], AW)
        cq = _rows8([conv_q_f[i, k] for k in range(4)], 3 * H * HD)
        sp = _rows8([_at_lanes(a_log[i], H), _at_lanes(dt_bias[i], H), norm_dn_w[i]], LANES)
        cff = _rows8([conv_f_f[i, k] for k in range(3)], 2 * DFF)
        return modrows, vec, pa, cq, sp, cff

    saved = []
    xi = x
    for i in range(DEPTH):
        modrows, vec, pa, cq, sp, cff = layer_params(i)
        p, h1 = _in_proj(xi, modrows, vec, w_in_f[i])
        qn, kn, vs, gb, ya = _pre_fwd(p, pa, cq, sp)
        o, states, tinvs = _gdr_fwd(qn, kn, vs, gb)
        y, x2, yb = _post_fwd(o, p, ya, xi, modrows, sp, w_out_f[i])
        h2, gpre, upre, f, dff, x3 = _ffn_fwd(x2, modrows, vec, w_up_f[i], cff, w_down_f[i])
        saved.append(dict(x=xi, p=p, h1=h1, qn=qn, kn=kn, vs=vs, gb=gb, ya=ya, o=o, states=states, tinvs=tinvs, y=y, x2=x2, yb=yb,
                          h2=h2, gpre=gpre, upre=upre, f=f, d=dff))
        xi = x3

    dx, facc = _final(xi, target, _rows8([norm_f_w], D))
    loss_local = jnp.sum(facc[0])
    d_norm_f = facc[1]

    gw_in, gw_out, gw_up, gw_down = [None] * DEPTH, [None] * DEPTH, [None] * DEPTH, [None] * DEPTH
    g_small = [None] * DEPTH
    for i in reversed(range(DEPTH)):
        s = saved[i]
        modrows, vec, pa, cq, sp, cff = layer_params(i)
        dd, dgp, dup, dx2, accf, dcg, dcu = _ffn_bwd(dx, s["d"], s["x2"], modrows, vec, s["gpre"], s["upre"], cff,
                                                       w_down_f[i], w_up_f[i])
        n_up, up_pad = 2 * DFF // N_DEV, UP_PAD
        gw_up[i] = jnp.concatenate([_wgrad_cols(s["h2"], dgp, 512, n_up, up_pad, N_DEV // 2, "wgrad_up"),
                                    _wgrad_cols(s["h2"], dup, 512, n_up, up_pad, N_DEV // 2, "wgrad_up")], axis=0)
        gw_down[i] = _wgrad(s["f"], dd, DFF // 2, 1024, "wgrad_down").reshape(N_DEV, DFF // N_DEV, D)
        dy, do, dz, dya, accp, accs = _post_bwd(dx2, s["y"], s["o"], s["p"], modrows, sp, w_out_f[i])
        gw_out[i] = jnp.concatenate([_wgrad(s["ya"], dy, 512, 1024, "wgrad_out"),
                                     _wgrad(s["yb"], dy, 512, 1024, "wgrad_out")], axis=0).reshape(N_DEV, D // N_DEV, D)
        dqn, dkn, dvs, dgb = _gdr_bwd(s["qn"], s["kn"], s["vs"], s["gb"], s["states"], s["tinvs"], do)
        dp, dpa, dcq, dsp = _pre_bwd(s["p"], dqn, dkn, dvs, dya, dz, dgb, pa, cq, sp)
        gw_in[i] = _wgrad_cols(s["h1"], dp, 512, P_IN // N_DEV, IN_PAD, N_DEV, "wgrad_in")
        dx, acci = _in_bwd(dp, w_in_f[i], s["x"], dx2, modrows, vec)
        dconv_ff = jnp.concatenate([dcg.transpose(1, 0, 2).reshape(SUB, DFF), dcu.transpose(1, 0, 2).reshape(SUB, DFF)],
                                   axis=1)[0:3]
        dmod = jnp.stack([acci[0], acci[1], accp[0], accf[1], accf[2], accf[0]])
        g_small[i] = dict(norm1=acci[2], norm2=accf[3], norm_a=dpa[3], a_log=dsp[0, H:2 * H], dt_bias=dsp[1, H:2 * H],
                          norm_dn=accs[0], conv_a=dpa[0:3], conv_qkv=dcq[0:4], conv_ff=dconv_ff, dmod=dmod.reshape(-1))
    return loss_local, dx, gw_in, gw_out, gw_up, gw_down, g_small, d_norm_f


def kernel(x, c, ada_w, ada_b, norm1_w, w_in, conv_a_w, norm_a_w, conv_qkv_w, a_log, dt_bias, norm_dn_w, w_out, norm2_w, w_up, conv_ff_w, w_down, norm_f_w, loss_target, m_ada_w, m_ada_b, m_norm1_w, m_w_in, m_conv_a_w, m_norm_a_w, m_conv_qkv_w, m_a_log, m_dt_bias, m_norm_dn_w, m_w_out, m_norm2_w, m_w_up, m_conv_ff_w, m_w_down, m_norm_f_w, v_ada_w, v_ada_b, v_norm1_w, v_w_in, v_conv_a_w, v_norm_a_w, v_conv_qkv_w, v_a_log, v_dt_bias, v_norm_dn_w, v_w_out, v_norm2_w, v_w_up, v_conv_ff_w, v_w_down, v_norm_f_w):
    ax, ay, ac = lax.axis_index("x"), lax.axis_index("y"), lax.axis_index("c")
    me = 4 * ax + 2 * ay + ac
    x = x[0]
    target = loss_target[0]
    n_in, n_up = P_IN // N_DEV, 2 * DFF // N_DEV

    def lane_pad(t, width):
        return jnp.pad(t.astype(MXU), ((0, 0), (0, 0), (0, width - t.shape[-1])))

    conv_blob = _pad_rows(jnp.concatenate([t.reshape(-1) for t in (conv_a_w, conv_qkv_w, conv_ff_w)]),
                          SUB * LANES).reshape(-1, LANES)
    c_rows = jnp.zeros((SUB, D), F32).at[0].set(c[0])
    g_in, g_out, g_up, g_down, g_conv, g_c = _all_gather(
        [lane_pad(w_in, IN_PAD), w_out.astype(MXU), lane_pad(w_up, UP_PAD), w_down.astype(MXU), conv_blob, c_rows],
        "gather_weights", in_vmem=False)
    w_in_f = _interleave_cols(g_in, n_in, P_PAD, "interleave_w_in")
    w_up_f = _interleave_cols(g_up, n_up, 2 * DFF, "interleave_w_up")
    w_out_f = g_out.transpose(1, 0, 2, 3).reshape(DEPTH, D, D)
    w_down_f = g_down.transpose(1, 0, 2, 3).reshape(DEPTH, DFF, D)
    sg = g_conv.reshape(N_DEV, -1)
    o1 = conv_a_w.size
    o2 = o1 + conv_qkv_w.size
    o3 = o2 + conv_ff_w.size
    conv_a_f = sg[:, 0:o1].reshape(N_DEV, DEPTH, 3, AW // N_DEV).transpose(1, 2, 0, 3).reshape(DEPTH, 3, AW)
    conv_q_f = sg[:, o1:o2].reshape(N_DEV, DEPTH, 4, 3 * H * HD // N_DEV).transpose(1, 2, 0, 3).reshape(DEPTH, 4, 3 * H * HD)
    conv_f_f = sg[:, o2:o3].reshape(N_DEV, DEPTH, 3, n_up).transpose(1, 2, 0, 3).reshape(DEPTH, 3, 2 * DFF)

    c_all = jnp.concatenate([g_c[:, 0], jnp.zeros((16 - N_DEV, D), F32)], axis=0)
    n_ada = N_MOD * D // N_DEV
    ada_b_cols = lax.dynamic_slice_in_dim(ada_b, me * n_ada, n_ada, axis=1)[:, None, :]
    mod_sh = _mod_fwd(c_all, ada_w, ada_b_cols)
    mod_all = _all_gather([mod_sh.reshape(DEPTH * 16, n_ada)], "gather_mod", in_vmem=True)[0]
    mod_all = mod_all.reshape(N_DEV, DEPTH, 16, n_ada)
    mod_mine = lax.dynamic_index_in_dim(mod_all, me, axis=2, keepdims=False)
    mod_full = mod_mine.transpose(1, 0, 2).reshape(DEPTH, N_MOD, D)

    loss_local, dx, gw_in, gw_out, gw_up, gw_down, g_small, d_norm_f = _local_fwd_bwd(
        x, target, mod_full, (norm1_w, norm2_w, norm_a_w, a_log, dt_bias, norm_dn_w, norm_f_w),
        (w_in_f, w_out_f, w_up_f, w_down_f, conv_a_f, conv_q_f, conv_f_f))
    loss = lax.psum(loss_local, ("x", "y", "c"))
    grad_x = dx[None]

    keys = ["dmod", "norm1", "norm2", "norm_a", "a_log", "dt_bias", "norm_dn", "conv_a", "conv_qkv", "conv_ff"]
    stacked = {k: jnp.stack([g_small[i][k] for i in range(DEPTH)]) for k in keys}
    flat_parts = [stacked[k].reshape(-1) for k in keys] + [d_norm_f]
    sizes = [int(t.shape[0]) for t in flat_parts]
    sflat = _pad_rows(jnp.concatenate(flat_parts), SUB * LANES).reshape(-1, LANES)
    sall = _all_gather([sflat], "gather_small_grads", in_vmem=True)[0]
    ssum = _sum_devices(sall).reshape(-1)
    so = [0]
    for sz in sizes:
        so.append(so[-1] + sz)
    red = {k: ssum[so[n]:so[n + 1]].reshape(stacked[k].shape) for n, k in enumerate(keys)}
    g_norm_f = ssum[so[len(keys)]:so[len(keys) + 1]]
    dmod_all = sall[:, 0:sizes[0] // LANES, :].reshape(N_DEV, DEPTH, N_MOD * D)

    g_ada_b = red["dmod"].reshape(DEPTH, N_MOD * D)
    dmod_cols = lax.dynamic_slice_in_dim(dmod_all, me * n_ada, n_ada, axis=2).transpose(1, 0, 2)
    dmod_cols = jnp.concatenate([dmod_cols, jnp.zeros((DEPTH, 16 - N_DEV, n_ada), F32)], axis=1)
    g_ada_w = _mod_bwd(c_all, dmod_cols)
    g_conv_a = lax.dynamic_slice_in_dim(red["conv_a"], me * (AW // N_DEV), AW // N_DEV, axis=2)
    g_conv_qkv = lax.dynamic_slice_in_dim(red["conv_qkv"], me * (3 * H * HD // N_DEV), 3 * H * HD // N_DEV, axis=2)
    g_conv_ff = lax.dynamic_slice_in_dim(red["conv_ff"], me * n_up, n_up, axis=2)

    tags = ["w_in", "w_out", "w_up", "w_down"]
    gs = [jnp.stack(t, axis=1) for t in (gw_in, gw_out, gw_up, gw_down)]
    my_c = jnp.reshape(ac, (1,)).astype(jnp.int32)
    my_chip = jnp.reshape(2 * ax + ay, (1,)).astype(jnp.int32)
    recv1 = _rs_sibling(gs)
    pairs = [_rs_add_pairs(g, r, my_c, "rs_add_pairs_" + t) for g, r, t in zip(gs, recv1, tags)]
    recv2 = _rs_chips([pb for _, pb in pairs])
    mine = [_rs_add_chips(pf, r, my_chip, "rs_add_chips_" + t) for (pf, _), r, t in zip(pairs, recv2, tags)]
    g_w_in = mine[0][:, :, :n_in]
    g_w_out = mine[1]
    g_w_up = mine[2][:, :, :n_up]
    g_w_down = mine[3]

    grads = dict(ada_w=g_ada_w, ada_b=g_ada_b, norm1_w=red["norm1"], w_in=g_w_in, conv_a_w=g_conv_a,
                 norm_a_w=red["norm_a"], conv_qkv_w=g_conv_qkv, a_log=red["a_log"], dt_bias=red["dt_bias"],
                 norm_dn_w=red["norm_dn"], w_out=g_w_out, norm2_w=red["norm2"], w_up=g_w_up, conv_ff_w=g_conv_ff,
                 w_down=g_w_down, norm_f_w=g_norm_f)
    weights = dict(ada_w=ada_w, ada_b=ada_b, norm1_w=norm1_w, w_in=w_in, conv_a_w=conv_a_w, norm_a_w=norm_a_w,
                   conv_qkv_w=conv_qkv_w, a_log=a_log, dt_bias=dt_bias, norm_dn_w=norm_dn_w, w_out=w_out,
                   norm2_w=norm2_w, w_up=w_up, conv_ff_w=conv_ff_w, w_down=w_down, norm_f_w=norm_f_w)
    ms = dict(ada_w=m_ada_w, ada_b=m_ada_b, norm1_w=m_norm1_w, w_in=m_w_in, conv_a_w=m_conv_a_w, norm_a_w=m_norm_a_w,
              conv_qkv_w=m_conv_qkv_w, a_log=m_a_log, dt_bias=m_dt_bias, norm_dn_w=m_norm_dn_w, w_out=m_w_out,
              norm2_w=m_norm2_w, w_up=m_w_up, conv_ff_w=m_conv_ff_w, w_down=m_w_down, norm_f_w=m_norm_f_w)
    vs_ = dict(ada_w=v_ada_w, ada_b=v_ada_b, norm1_w=v_norm1_w, w_in=v_w_in, conv_a_w=v_conv_a_w, norm_a_w=v_norm_a_w,
               conv_qkv_w=v_conv_qkv_w, a_log=v_a_log, dt_bias=v_dt_bias, norm_dn_w=v_norm_dn_w, w_out=v_w_out,
               norm2_w=v_norm2_w, w_up=v_w_up, conv_ff_w=v_conv_ff_w, w_down=v_w_down, norm_f_w=v_norm_f_w)
    names = list(weights)
    big_names = ["ada_w", "w_in", "w_out", "w_up", "w_down"]
    delta, new_m, new_v = {}, {}, {}
    for n in big_names:
        shp = weights[n].shape
        two = lambda t: t.reshape(-1, shp[-1])
        dl, nm, nv = _adamw(two(weights[n]), two(grads[n]), two(ms[n]), two(vs_[n]), "adamw_" + n)
        delta[n], new_m[n], new_v[n] = dl.reshape(shp), nm.reshape(shp), nv.reshape(shp)
    small_names = [n for n in names if n not in big_names]

    def pack(dct):
        return _pad_rows(jnp.concatenate([dct[n].reshape(-1) for n in small_names]), SUB * LANES).reshape(-1, LANES)

    dl, nm, nv = _adamw(pack(weights), pack(grads), pack(ms), pack(vs_), "adamw_small")
    off = 0
    for n in small_names:
        sz, shp = weights[n].size, weights[n].shape
        delta[n] = dl.reshape(-1)[off:off + sz].reshape(shp)
        new_m[n] = nm.reshape(-1)[off:off + sz].reshape(shp)
        new_v[n] = nv.reshape(-1)[off:off + sz].reshape(shp)
        off += sz

    return (loss, grad_x, *[grads[n] for n in names], *[delta[n] for n in names],
            *[new_m[n] for n in names], *[new_v[n] for n in names])
```

```python
import functools
import math

import jax
import jax.numpy as jnp
from jax import lax
from jax.experimental import pallas as pl
from jax.experimental.pallas import tpu as pltpu

F32 = jnp.float32
MXU = jnp.bfloat16

D = 1024
DEPTH = 4
N_MOD = 6
AW = 512
A_GROUP = 64
H = 4
HD = 128
CK = 64
DFF = 2816
P_IN = 3592
P_PAD = 3712
EPS = 1e-6
N_DEV = 8
LANES = 128
SUB = 8
VMEM_LIMIT = 56 * 1024 * 1024

ADAM_LR, ADAM_B1, ADAM_B2, ADAM_EPS, ADAM_WD, ADAM_STEP = 0.001, 0.9, 0.999, 1e-08, 0.01, 10

NN = ((1,), (0,))
NT = ((1,), (1,))
TN = ((0,), (0,))
HI = lax.Precision.HIGHEST
MESH = pl.DeviceIdType.MESH


def _dot(a, b, dims, prec=None):
    if prec is None:
        a = a.astype(MXU) if a.dtype == F32 else a
        b = b.astype(MXU) if b.dtype == F32 else b
    return lax.dot_general(a, b, (dims, ((), ())), precision=prec, preferred_element_type=F32)


def _params(n_grid=0, limit=VMEM_LIMIT):
    sem = ("arbitrary",) * n_grid if n_grid else None
    return pltpu.CompilerParams(dimension_semantics=sem, vmem_limit_bytes=limit)


def _tile(n, want):
    if n <= want:
        return n
    t = want - want % SUB
    while n % t:
        t -= SUB
    assert t > 0, (n, want)
    return t


def _full(shape):
    nd = len(shape)
    return pl.BlockSpec(shape, lambda *_: (0,) * nd)


def _sig(x):
    return jax.nn.sigmoid(x)


def _rms(x):
    r = lax.rsqrt(jnp.mean(x * x, axis=-1, keepdims=True) + EPS)
    return x * r, r


def _rms_bwd(dn, n, r):
    return r * (dn - n * jnp.mean(dn * n, axis=-1, keepdims=True))


def _l2_bwd(dn, n, r):
    return r * (dn - n * jnp.sum(dn * n, axis=-1, keepdims=True))


def _sum0(x):
    return jnp.sum(x, axis=0, keepdims=True)


def _shift_down(x, s, halo):
    ext = jnp.concatenate([halo, x], axis=0)
    return pltpu.roll(ext, s, 0)[SUB:, :]


def _shift_up(x, s, halo):
    t = x.shape[0]
    ext = jnp.concatenate([x, halo], axis=0)
    return pltpu.roll(ext, t + SUB - s, 0)[:t, :]


def _conv_fwd(x, w_ref, width, halo):
    sh = [x] + [_shift_down(x, s, halo) for s in range(1, width)]
    out = w_ref[width - 1:width, :] * sh[0]
    for s in range(1, width):
        out = out + w_ref[width - 1 - s:width - s, :] * sh[s]
    return out, sh


def _conv_bwd_in(dout, w_ref, width, halo_next):
    dx = w_ref[width - 1:width, :] * dout
    for s in range(1, width):
        dx = dx + w_ref[width - 1 - s:width - s, :] * _shift_up(dout, s, halo_next)
    return dx


def _blockdiag_mean(n, group):
    r = lax.shift_right_logical(lax.broadcasted_iota(jnp.int32, (n, n), 0), int(math.log2(group)))
    c = lax.shift_right_logical(lax.broadcasted_iota(jnp.int32, (n, n), 1), int(math.log2(group)))
    return jnp.where(r == c, 1.0 / group, 0.0).astype(F32)


def _softplus(x):
    return jnp.maximum(x, 0.0) + jnp.log(1.0 + jnp.exp(-jnp.abs(x)))


def _my_place():
    return lax.axis_index("x"), lax.axis_index("y"), lax.axis_index("c")


def _all_gather(shards, name, in_vmem):
    nt = len(shards)

    def body(*refs):
        x_refs, out_refs = refs[:nt], refs[nt:2 * nt]
        send_sems, recv_sems, local_sems = refs[2 * nt:]
        x, y, c = _my_place()
        me, sibling = (x, y, c), (x, y, 1 - c)
        chips = [(1 - x, y), (x, 1 - y), (1 - x, 1 - y)]
        everything = []
        for t in range(nt):
            x_ref, out_ref = x_refs[t], out_refs[t]

            def blk(px, py, pc, out_ref=out_ref):
                return out_ref.at[4 * px + 2 * py + pc]

            def copy(k, block, to, src=None, t=t, blk=blk):
                return pltpu.make_async_remote_copy(
                    src_ref=blk(*block) if src is None else src, dst_ref=blk(*block),
                    send_sem=send_sems.at[7 * t + k], recv_sem=recv_sems.at[7 * t + k], device_id=to, device_id_type=MESH)

            mine = pltpu.make_async_copy(x_ref, blk(*me), local_sems.at[t])
            mine.start()
            first = [copy(0, me, sibling, src=x_ref)]
            first += [copy(1 + j, me, (*chip, c), src=x_ref) for j, chip in enumerate(chips)]
            for cp in first:
                cp.start()
            everything.append((copy, mine, first))
        sends = []
        for copy, mine, first in everything:
            passed = [copy(4 + j, (*chip, c), sibling) for j, chip in enumerate(chips)]
            for j, chip in enumerate(chips):
                copy(1 + j, (*chip, c), me).wait_recv()
                passed[j].start()
            sends += first + passed
        for copy, mine, first in everything:
            copy(0, sibling, me).wait_recv()
            for j, chip in enumerate(chips):
                copy(4 + j, (*chip, 1 - c), me).wait_recv()
        for cp in sends:
            cp.wait_send()
        for copy, mine, first in everything:
            mine.wait()

    space = pltpu.VMEM if in_vmem else pl.ANY
    return pl.pallas_call(
        body, name=name,
        out_shape=[jax.ShapeDtypeStruct((N_DEV,) + s.shape, s.dtype) for s in shards],
        in_specs=[pl.BlockSpec(memory_space=space)] * nt,
        out_specs=[pl.BlockSpec(memory_space=space)] * nt,
        scratch_shapes=[pltpu.SemaphoreType.DMA((7 * nt,)), pltpu.SemaphoreType.DMA((7 * nt,)),
                        pltpu.SemaphoreType.DMA((nt,))],
        compiler_params=pltpu.CompilerParams(vmem_limit_bytes=VMEM_LIMIT),
    )(*shards)


def _rs_sibling(gs):
    nt = len(gs)

    def body(*refs):
        g_refs, recv_refs = refs[:nt], refs[nt:2 * nt]
        send_sems, recv_sems = refs[2 * nt:]
        x, y, c = _my_place()
        copies = [pltpu.make_async_remote_copy(
            src_ref=g_refs[t].at[2 * j + (1 - c)], dst_ref=recv_refs[t].at[j],
            send_sem=send_sems.at[4 * t + j], recv_sem=recv_sems.at[4 * t + j],
            device_id=(x, y, 1 - c), device_id_type=MESH) for t in range(nt) for j in range(4)]
        for cp in copies:
            cp.start()
        for cp in copies:
            cp.wait()

    return pl.pallas_call(
        body, name="rs_sibling",
        out_shape=[jax.ShapeDtypeStruct((4,) + g.shape[1:], g.dtype) for g in gs],
        in_specs=[pl.BlockSpec(memory_space=pl.ANY)] * nt, out_specs=[pl.BlockSpec(memory_space=pl.ANY)] * nt,
        scratch_shapes=[pltpu.SemaphoreType.DMA((4 * nt,)), pltpu.SemaphoreType.DMA((4 * nt,))],
    )(*gs)


def _rs_chips(pbs):
    nt = len(pbs)

    def body(*refs):
        p_refs, recv_refs = refs[:nt], refs[nt:2 * nt]
        send_sems, recv_sems = refs[2 * nt:]
        x, y, c = _my_place()
        chips = [(1 - x, y), (x, 1 - y), (1 - x, 1 - y)]
        copies = [pltpu.make_async_remote_copy(
            src_ref=p_refs[t].at[2 * px + py], dst_ref=recv_refs[t].at[s],
            send_sem=send_sems.at[3 * t + s], recv_sem=recv_sems.at[3 * t + s],
            device_id=(px, py, c), device_id_type=MESH) for t in range(nt) for s, (px, py) in enumerate(chips)]
        for cp in copies:
            cp.start()
        for cp in copies:
            cp.wait()

    return pl.pallas_call(
        body, name="rs_chips",
        out_shape=[jax.ShapeDtypeStruct((3,) + p.shape[1:], p.dtype) for p in pbs],
        in_specs=[pl.BlockSpec(memory_space=pl.ANY)] * nt, out_specs=[pl.BlockSpec(memory_space=pl.ANY)] * nt,
        scratch_shapes=[pltpu.SemaphoreType.DMA((3 * nt,)), pltpu.SemaphoreType.DMA((3 * nt,))],
    )(*pbs)


def _rs_add_pairs(g, recv, my_c, name):
    _, nl, r, n = g.shape
    tr = _tile(r, 512)

    def body(c_ref, g_ref, r_ref, pf_ref, pb_ref):
        s = g_ref[...] + r_ref[...]
        pf_ref[...] = s
        pb_ref[...] = s.astype(MXU)

    spec_j = pl.BlockSpec((None, None, tr, n), lambda j, l, i, c_ref: (j, l, i, 0))
    return pl.pallas_call(
        body, name=name,
        grid_spec=pltpu.PrefetchScalarGridSpec(
            num_scalar_prefetch=1, grid=(4, nl, r // tr),
            in_specs=[pl.BlockSpec((None, None, tr, n), lambda j, l, i, c_ref: (2 * j + c_ref[0], l, i, 0)), spec_j],
            out_specs=[spec_j, spec_j]),
        out_shape=[jax.ShapeDtypeStruct((4, nl, r, n), F32), jax.ShapeDtypeStruct((4, nl, r, n), MXU)],
        compiler_params=_params(3),
    )(my_c, g, recv)


def _rs_add_chips(pf, recv, my_chip, name):
    _, nl, r, n = pf.shape
    tr = _tile(r, 512)

    def body(j_ref, p_ref, r_ref, o_ref):
        s = p_ref[...]
        for t in range(3):
            s = s + r_ref[t].astype(F32)
        o_ref[...] = s

    return pl.pallas_call(
        body, name=name,
        grid_spec=pltpu.PrefetchScalarGridSpec(
            num_scalar_prefetch=1, grid=(nl, r // tr),
            in_specs=[pl.BlockSpec((None, None, tr, n), lambda l, i, j_ref: (j_ref[0], l, i, 0)),
                      pl.BlockSpec((3, None, tr, n), lambda l, i, j_ref: (0, l, i, 0))],
            out_specs=pl.BlockSpec((None, tr, n), lambda l, i, j_ref: (l, i, 0))),
        out_shape=jax.ShapeDtypeStruct((nl, r, n), F32),
        compiler_params=_params(2),
    )(my_chip, pf, recv)


def _shard_windows(n_shard, count, first=0):
    out = []
    for k in range(first, first + count):
        off = n_shard * k
        a, s = off // LANES, off % LANES
        out.append((a, s, -(-(s + n_shard) // LANES) * LANES))
    return out


def _fit_lanes(x, width):
    have = x.shape[1]
    if have < width:
        return jnp.concatenate([x, jnp.zeros((x.shape[0], width - have), x.dtype)], axis=-1)
    return x[:, :width]


def _interleave_cols(g, n_shard, w_out, name):
    nd, nl, rows, wpad = g.shape
    rb = _tile(rows, 256)
    wins = _shard_windows(n_shard, nd)

    def body(g_ref, o_ref, acc):
        acc[...] = jnp.zeros_like(acc)
        for k, (a, s, win) in enumerate(wins):
            xk = _fit_lanes(g_ref[k].astype(F32), win)
            if s:
                xk = pltpu.roll(xk, s, 1)
            acc[:, a * LANES:a * LANES + win] += xk
        o_ref[...] = acc[...].astype(o_ref.dtype)

    return pl.pallas_call(
        body, name=name, grid=(nl, rows // rb),
        in_specs=[pl.BlockSpec((nd, None, rb, wpad), lambda l, i: (0, l, i, 0))],
        out_specs=pl.BlockSpec((None, rb, w_out), lambda l, i: (l, i, 0)),
        out_shape=jax.ShapeDtypeStruct((nl, rows, w_out), g.dtype),
        scratch_shapes=[pltpu.VMEM((rb, w_out), F32)],
        compiler_params=_params(2),
    )(g)


def _sum_devices(g):
    _, r, n = g.shape

    def body(g_ref, o_ref):
        s = g_ref[0]
        for t in range(1, N_DEV):
            s = s + g_ref[t]
        o_ref[...] = s

    return pl.pallas_call(
        body, name="sum_devices", out_shape=jax.ShapeDtypeStruct((r, n), F32),
        in_specs=[pl.BlockSpec(memory_space=pltpu.VMEM)], out_specs=pl.BlockSpec(memory_space=pltpu.VMEM),
        compiler_params=pltpu.CompilerParams(vmem_limit_bytes=VMEM_LIMIT),
    )(g)


def _mod_fwd(c_all, ada_w, ada_b_cols):
    nl, _, nc = ada_w.shape

    def body(c_ref, w_ref, b_ref, o_ref):
        cv = c_ref[...]
        act = (cv * _sig(cv)).astype(MXU)
        o_ref[...] = _dot(act, w_ref[...].astype(MXU), NN) + b_ref[...]

    return pl.pallas_call(
        body, name="mod_fwd", grid=(nl,),
        in_specs=[_full((16, D)), pl.BlockSpec((None, D, nc), lambda i: (i, 0, 0)),
                  pl.BlockSpec((None, 1, nc), lambda i: (i, 0, 0))],
        out_specs=pl.BlockSpec((None, 16, nc), lambda i: (i, 0, 0)),
        out_shape=jax.ShapeDtypeStruct((nl, 16, nc), F32), compiler_params=_params(1),
    )(c_all, ada_w, ada_b_cols)


def _mod_bwd(c_all, dmod_cols):
    nl, _, nc = dmod_cols.shape

    def body(c_ref, d_ref, o_ref):
        cv = c_ref[...]
        act = (cv * _sig(cv)).astype(MXU)
        o_ref[...] = _dot(act, d_ref[...].astype(MXU), TN)

    return pl.pallas_call(
        body, name="mod_bwd", grid=(nl,),
        in_specs=[_full((16, D)), pl.BlockSpec((None, 16, nc), lambda i: (i, 0, 0))],
        out_specs=pl.BlockSpec((None, D, nc), lambda i: (i, 0, 0)),
        out_shape=jax.ShapeDtypeStruct((nl, D, nc), F32), compiler_params=_params(1),
    )(c_all, dmod_cols)


def _in_proj(x, modrows, vec, w_in):
    L = x.shape[0]
    T = _tile(L, 256)

    def body(x_ref, mod_ref, vec_ref, w_ref, p_ref, h_ref):
        n, _ = _rms(x_ref[...])
        h = n * vec_ref[0:1, :] * (1.0 + mod_ref[1:2, :]) + mod_ref[0:1, :]
        hb = h.astype(MXU)
        h_ref[...] = hb
        p_ref[...] = _dot(hb, w_ref[...], NN)

    return pl.pallas_call(
        body, name="in_proj", grid=(L // T,),
        in_specs=[pl.BlockSpec((T, D), lambda i: (i, 0)), _full((SUB, D)), _full((SUB, D)), _full((D, P_PAD))],
        out_specs=[pl.BlockSpec((T, P_PAD), lambda i: (i, 0)), pl.BlockSpec((T, D), lambda i: (i, 0))],
        out_shape=[jax.ShapeDtypeStruct((L, P_PAD), F32), jax.ShapeDtypeStruct((L, D), MXU)],
        compiler_params=_params(1),
    )(x, modrows, vec, w_in)


def _gate_small(s, sp_ref):
    lane = lax.broadcasted_iota(jnp.int32, s.shape, 1)
    a = -jnp.exp(sp_ref[0:1, :])
    xb = s + sp_ref[1:2, :]
    beta = _sig(s)
    g = a * _softplus(xb)
    return lane, a, xb, beta, g


def _pre_fwd(p, pa, cq, sp):
    L = p.shape[0]
    T = _tile(L, 256)
    scale = HD ** -0.5

    def body(pm_ref, ps_ref, pa_ref, cq_ref, sp_ref, qn_ref, kn_ref, vs_ref, gb_ref, ya_ref, u_carry, q_carry):
        @pl.when(pl.program_id(0) == 0)
        def _():
            u_carry[...] = jnp.zeros_like(u_carry)
            q_carry[...] = jnp.zeros_like(q_carry)

        a_b = pm_ref[:, 0:AW]
        u = pm_ref[:, AW:2 * AW] * pm_ref[:, 2 * AW:3 * AW]
        cu, _ = _conv_fwd(u, pa_ref, 3, u_carry[...])
        u_carry[...] = u[T - SUB:T, :]
        yp = a_b * cu
        ms = _dot_f32(yp * yp, _blockdiag_mean(AW, A_GROUP), NN, exact="b")
        ya_ref[...] = (yp * lax.rsqrt(ms + EPS) * pa_ref[3:4, :]).astype(MXU)

        qkv = pm_ref[:, 3 * AW:3 * AW + 3 * H * HD]
        qc, _ = _conv_fwd(qkv, cq_ref, 4, q_carry[...])
        q_carry[...] = qkv[T - SUB:T, :]
        qs = qc * _sig(qc)
        for h in range(H):
            q = qs[:, h * HD:(h + 1) * HD]
            qn_ref[:, h * HD:(h + 1) * HD] = q * (lax.rsqrt(jnp.sum(q * q, axis=-1, keepdims=True) + EPS) * scale)
            k = qs[:, (H + h) * HD:(H + h + 1) * HD]
            kn_ref[:, h * HD:(h + 1) * HD] = k * lax.rsqrt(jnp.sum(k * k, axis=-1, keepdims=True) + EPS)
        vs_ref[...] = qs[:, 2 * H * HD:3 * H * HD]

        lane, _, _, beta, g = _gate_small(ps_ref[...], sp_ref)
        gb_ref[...] = jnp.where(lane < H, beta, jnp.where(lane < 2 * H, g, 0.0))

    w3 = 3 * AW + 3 * H * HD
    row = lambda i: (i, 0)
    return pl.pallas_call(
        body, name="pre_fwd", grid=(L // T,),
        in_specs=[pl.BlockSpec((T, w3), row), pl.BlockSpec((T, LANES), lambda i: (i, (P_PAD - LANES) // LANES)),
                  _full((SUB, AW)), _full((SUB, 3 * H * HD)), _full((SUB, LANES))],
        out_specs=[pl.BlockSpec((T, H * HD), row)] * 3 + [pl.BlockSpec((T, LANES), row), pl.BlockSpec((T, AW), row)],
        out_shape=[jax.ShapeDtypeStruct((L, H * HD), F32)] * 3
        + [jax.ShapeDtypeStruct((L, LANES), F32), jax.ShapeDtypeStruct((L, AW), MXU)],
        scratch_shapes=[pltpu.VMEM((SUB, AW), F32), pltpu.VMEM((SUB, 3 * H * HD), F32)],
        compiler_params=_params(1),
    )(p, p, pa, cq, sp)


def _gdr_masks():
    r = lax.broadcasted_iota(jnp.int32, (CK, CK), 0)
    c = lax.broadcasted_iota(jnp.int32, (CK, CK), 1)
    return r >= c, r > c


def _head_cols(gbt, h):
    return gbt[:, h:h + 1], gbt[:, H + h:H + h + 1]


def _split(x, parts):
    out = []
    for _ in range(parts):
        hi = x.astype(jnp.bfloat16)
        out.append(hi)
        x = x - hi.astype(F32)
    return out


def _dot_f32(a, b, dims, exact=None):
    if exact == "a":
        ab = a.astype(jnp.bfloat16)
        return sum(_dot(ab, t, dims) for t in _split(b, 3))
    if exact == "b":
        bb = b.astype(jnp.bfloat16)
        return sum(_dot(t, bb, dims) for t in _split(a, 3))
    ah, al = _split(a, 2)
    bh, bl = _split(b, 2)
    return _dot(ah, bh, dims) + _dot(ah, bl, dims) + _dot(al, bh, dims)


def _gdr_consts():
    causal, strict = _gdr_masks()
    return dict(causal=causal, strict=strict, tril=jnp.where(causal, 1.0, 0.0).astype(F32),
                eye=jnp.where(causal & jnp.logical_not(strict), 1.0, 0.0).astype(F32),
                bcast=jnp.full((CK, HD), 1.0 / HD, F32))


def _dots(a, b, dims):
    return [_dot(x, y, dims) for x, y in zip(a, b)]


def _dots_f32(a, b, dims, exact=None):
    n = len(a)
    if exact == "a":
        lhs = [[x.astype(jnp.bfloat16)] * 3 for x in a]
        rhs = [_split(y, 3) for y in b]
    elif exact == "b":
        lhs = [_split(x, 3) for x in a]
        rhs = [[y.astype(jnp.bfloat16)] * 3 for y in b]
    else:
        sa = [_split(x, 2) for x in a]
        sb = [_split(y, 2) for y in b]
        lhs = [[s[0], s[0], s[1]] for s in sa]
        rhs = [[s[0], s[1], s[0]] for s in sb]
    terms = [[_dot(lhs[i][t], rhs[i][t], dims) for i in range(n)] for t in range(3)]
    return [terms[0][i] + terms[1][i] + terms[2][i] for i in range(n)]


def _gdr_local(q, k, v, beta, g, cst, tinv=None):
    n = len(q)
    R = range(n)
    causal, strict = cst["causal"], cst["strict"]
    gc = _dots_f32([cst["tril"]] * n, [jnp.broadcast_to(g[i], (CK, HD)) for i in R], NN, exact="a")
    g_row = _dots_f32([cst["bcast"]] * n, gc, NT, exact="a")
    decay = [jnp.where(causal, jnp.exp(jnp.where(causal, gc[i][:, 0:CK] - g_row[i], 0.0)), 0.0) for i in R]
    eg = [jnp.exp(gc[i]) for i in R]
    gl = [gc[i][CK - 1:CK, :] for i in R]
    ek = [jnp.exp(gl[i] - gc[i]) for i in R]
    cd = [jnp.exp(gl[i]) for i in R]
    kb = [k[i] * beta[i] for i in R]
    pk = _dots(kb, k, NT)
    if tinv is None:
        xp = [-jnp.where(strict, pk[i] * decay[i], 0.0) for i in R]
        tinv = [cst["eye"] + xp[i] for i in R]
        for _ in range(5):
            xp = _dots_f32(xp, xp, NN)
            tx = _dots_f32(tinv, xp, NN)
            tinv = [tinv[i] + tx[i] for i in R]
    u = _dots(tinv, [v[i] * beta[i] for i in R], NN)
    w = _dots(tinv, [kb[i] * eg[i] for i in R], NN)
    qk = _dots(q, k, NT)
    intra = [jnp.where(causal, qk[i] * decay[i], 0.0) for i in R]
    return dict(decay=decay, eg=eg, ek=ek, cd=cd, kb=kb, pk=pk, tinv=tinv, u=u, w=w, qk=qk, intra=intra,
                q_dec=[q[i] * eg[i] for i in R], k_dec=[k[i] * ek[i] for i in R])


GDR_SUB = 4


def _gdr_fwd(qn, kn, vs, gb):
    L = qn.shape[0]
    nc = L // CK
    cb = min(8, nc)
    rb = cb * CK
    nb = nc // cb
    nsub = GDR_SUB if cb % GDR_SUB == 0 else 1

    def body(q_ref, k_ref, v_ref, gb_ref, o_ref, st_ref, ti_ref, s_ref):
        @pl.when(pl.program_id(0) == 0)
        def _():
            s_ref[...] = jnp.zeros_like(s_ref)

        cst = _gdr_consts()
        heads = range(H)

        def group(gi, carry):
            rows = [pl.ds(pl.multiple_of((gi * nsub + j) * CK, CK), CK) for j in range(nsub)]
            chains = [(j, h) for j in range(nsub) for h in heads]
            gbt = [gb_ref[rows[j], :] for j in range(nsub)]
            cols = lambda h: slice(h * HD, (h + 1) * HD)
            t = _gdr_local([q_ref[rows[j], cols(h)] for j, h in chains], [k_ref[rows[j], cols(h)] for j, h in chains],
                           [v_ref[rows[j], cols(h)] for j, h in chains],
                           [_head_cols(gbt[j], h)[0] for j, h in chains], [_head_cols(gbt[j], h)[1] for j, h in chains], cst)
            s = [s_ref[h] for h in heads]
            for j in range(nsub):
                at = lambda key: [t[key][j * H + h] for h in heads]
                for h in heads:
                    st_ref[h, gi * nsub + j] = s[h]
                    ti_ref[h, gi * nsub + j] = t["tinv"][j * H + h]
                ws = _dots(at("w"), s, NN)
                v_new = [u_h - ws_h for u_h, ws_h in zip(at("u"), ws)]
                o_s = _dots(at("q_dec"), s, NN)
                o_v = _dots(at("intra"), v_new, NN)
                kv = _dots(at("k_dec"), v_new, TN)
                cd = at("cd")
                for h in heads:
                    o_ref[rows[j], cols(h)] = o_s[h] + o_v[h]
                s = [s[h] * cd[h] + kv[h] for h in heads]
            for h in heads:
                s_ref[h] = s[h]
            return carry

        lax.fori_loop(0, cb // nsub, group, 0)

    blk = pl.BlockSpec((rb, H * HD), lambda b: (b, 0))
    return pl.pallas_call(
        body, name="gdr_fwd", grid=(nb,),
        in_specs=[blk, blk, blk, pl.BlockSpec((rb, LANES), lambda b: (b, 0))],
        out_specs=[blk, pl.BlockSpec((H, cb, HD, HD), lambda b: (0, b, 0, 0)),
                   pl.BlockSpec((H, cb, CK, CK), lambda b: (0, b, 0, 0))],
        out_shape=[jax.ShapeDtypeStruct((L, H * HD), F32), jax.ShapeDtypeStruct((H, nc, HD, HD), F32),
                   jax.ShapeDtypeStruct((H, nc, CK, CK), F32)],
        scratch_shapes=[pltpu.VMEM((H, HD, HD), F32)],
        compiler_params=_params(1),
    )(qn, kn, vs, gb)


def _gdr_bwd(qn, kn, vs, gb, states, tinvs, do):
    L = qn.shape[0]
    nc = L // CK
    cb = min(8, nc)
    rb = cb * CK
    nb = nc // cb
    nsub = GDR_SUB if cb % GDR_SUB == 0 else 1

    def body(q_ref, k_ref, v_ref, gb_ref, st_ref, ti_ref, do_ref, dq_ref, dk_ref, dv_ref, dgb_ref, ds_ref):
        @pl.when(pl.program_id(0) == 0)
        def _():
            ds_ref[...] = jnp.zeros_like(ds_ref)

        cst = _gdr_consts()
        causal, strict = cst["causal"], cst["strict"]
        ones = jnp.ones((CK, HD), F32)
        row = lax.broadcasted_iota(jnp.int32, (CK, HD), 0)
        lane = lax.broadcasted_iota(jnp.int32, (CK, LANES), 1)

        heads = range(H)
        rsum = lambda x: jnp.sum(x, axis=-1, keepdims=True)

        def group(gj, carry):
            gi = cb // nsub - 1 - gj
            rows = [pl.ds(pl.multiple_of((gi * nsub + j) * CK, CK), CK) for j in range(nsub)]
            chains = [(j, h) for j in range(nsub) for h in heads]
            gbt = [gb_ref[rows[j], :] for j in range(nsub)]
            cols = lambda h: slice(h * HD, (h + 1) * HD)
            q_all = [q_ref[rows[j], cols(h)] for j, h in chains]
            k_all = [k_ref[rows[j], cols(h)] for j, h in chains]
            v_all = [v_ref[rows[j], cols(h)] for j, h in chains]
            beta_all = [_head_cols(gbt[j], h)[0] for j, h in chains]
            t = _gdr_local(q_all, k_all, v_all, beta_all, [_head_cols(gbt[j], h)[1] for j, h in chains], cst,
                           tinv=[ti_ref[h, gi * nsub + j] for j, h in chains])
            ds_out = [ds_ref[h] for h in heads]
            for j in reversed(range(nsub)):
                at = lambda key: [t[key][j * H + h] for h in heads]
                pick = lambda lst: [lst[j * H + h] for h in heads]
                q, k, v, beta = pick(q_all), pick(k_all), pick(v_all), pick(beta_all)
                u, w, tinv, decay = at("u"), at("w"), at("tinv"), at("decay")
                eg, ek, cd, kb = at("eg"), at("ek"), at("cd"), at("kb")
                q_dec, k_dec, intra, pk, qk = at("q_dec"), at("k_dec"), at("intra"), at("pk"), at("qk")
                s = [st_ref[h, gi * nsub + j] for h in heads]
                dout = [do_ref[rows[j], cols(h)] for h in heads]

                ws = _dots(w, s, NN)
                v_new = [u[h] - ws[h] for h in heads]
                dq_dec = _dots(dout, s, NT)
                qd = _dots(q_dec, dout, TN)
                di = _dots(dout, v_new, NT)
                dintra = [jnp.where(causal, di[h], 0.0) for h in heads]
                ido = _dots(intra, dout, TN)
                kds = _dots(k_dec, ds_out, NN)
                dv_new = [ido[h] + kds[h] for h in heads]
                dk_dec = _dots(v_new, ds_out, NT)
                dcd = [jnp.sum(jnp.sum(ds_out[h] * s[h], axis=1, keepdims=True), axis=0, keepdims=True) for h in heads]
                dvs = _dots(dv_new, s, NT)
                dw = [-dvs[h] for h in heads]
                wdv = _dots(w, dv_new, TN)
                ds_new = [qd[h] + ds_out[h] * cd[h] - wdv[h] for h in heads]
                dru = _dots(tinv, dv_new, TN)
                drw = _dots(tinv, dw, TN)
                dl1 = _dots(dru, u, NT)
                dl2 = _dots(drw, w, NT)
                dlower = [-jnp.where(strict, dl1[h] + dl2[h], 0.0) for h in heads]
                dv = [dru[h] * beta[h] for h in heads]
                dbeta = [rsum(dru[h] * v[h]) for h in heads]
                dgc = [rsum(drw[h] * kb[h]) * eg[h] for h in heads]
                dpk = [dlower[h] * decay[h] for h in heads]
                dqk = [dintra[h] * decay[h] for h in heads]
                dpk_k = _dots(dpk, k, NN)
                dkb = [drw[h] * eg[h] + dpk_k[h] for h in heads]
                dk1 = _dots(dpk, kb, TN)
                dq1 = _dots(dqk, k, NN)
                dk2 = _dots(dqk, q, TN)
                m = [(dlower[h] * pk[h] + dintra[h] * qk[h]) * decay[h] for h in heads]
                mcol = _dots_f32(m, [ones] * H, TN, exact="b")
                e = [rsum(dk_dec[h] * k_dec[h]) for h in heads]
                dgl = [jnp.sum(e[h], axis=0, keepdims=True) + dcd[h] * cd[h] for h in heads]
                dgc = [dgc[h] + rsum(m[h]) - mcol[h] + rsum(dq_dec[h] * q_dec[h]) - e[h]
                       + jnp.where(row == CK - 1, dgl[h], 0.0) for h in heads]
                dg = _dots_f32([cst["tril"]] * H, dgc, TN, exact="a")
                dgb = jnp.zeros((CK, LANES), F32)
                for h in heads:
                    dq_ref[rows[j], cols(h)] = dq1[h] + dq_dec[h] * eg[h]
                    dk_ref[rows[j], cols(h)] = dk1[h] + dk2[h] + dk_dec[h] * ek[h] + dkb[h] * beta[h]
                    dv_ref[rows[j], cols(h)] = dv[h]
                    db = dbeta[h] + rsum(dkb[h] * k[h])
                    dgb = dgb + jnp.where(lane == h, db, 0.0) + jnp.where(lane == H + h, dg[h], 0.0)
                dgb_ref[rows[j], :] = dgb
                ds_out = ds_new
            for h in heads:
                ds_ref[h] = ds_out[h]
            return carry

        lax.fori_loop(0, cb // nsub, group, 0)

    blk = pl.BlockSpec((rb, H * HD), lambda b: (nb - 1 - b, 0))
    sblk = pl.BlockSpec((rb, LANES), lambda b: (nb - 1 - b, 0))
    return pl.pallas_call(
        body, name="gdr_bwd", grid=(nb,),
        in_specs=[blk, blk, blk, sblk, pl.BlockSpec((H, cb, HD, HD), lambda b: (0, nb - 1 - b, 0, 0)),
                  pl.BlockSpec((H, cb, CK, CK), lambda b: (0, nb - 1 - b, 0, 0)), blk],
        out_specs=[blk, blk, blk, sblk],
        out_shape=[jax.ShapeDtypeStruct((L, H * HD), F32)] * 3 + [jax.ShapeDtypeStruct((L, LANES), F32)],
        scratch_shapes=[pltpu.VMEM((H, HD, HD), F32)],
        compiler_params=_params(1),
    )(qn, kn, vs, gb, states, tinvs, do)


def _post_fwd(o, p, ya, x, modrows, sp, w_out):
    L = x.shape[0]
    T = _tile(L, 256)

    def body(o_ref, z_ref, ya_ref, x_ref, mod_ref, sp_ref, w_ref, y_ref, x2_ref, yb_ref):
        ndw = sp_ref[2:3, :]
        z = z_ref[...]
        sz = z * _sig(z)
        parts = []
        for h in range(H):
            n, _ = _rms(o_ref[:, h * HD:(h + 1) * HD])
            parts.append(n * ndw * sz[:, h * HD:(h + 1) * HD])
        yb = jnp.concatenate(parts, axis=-1).astype(MXU)
        yb_ref[...] = yb
        y = _dot(ya_ref[...], w_ref[0:AW, :], NN) + _dot(yb, w_ref[AW:2 * AW, :], NN)
        y_ref[...] = y
        x2_ref[...] = x_ref[...] + mod_ref[2:3, :] * y

    row = lambda i: (i, 0)
    zcol = (3 * AW + 3 * H * HD) // (H * HD)
    return pl.pallas_call(
        body, name="post_fwd", grid=(L // T,),
        in_specs=[pl.BlockSpec((T, H * HD), row), pl.BlockSpec((T, H * HD), lambda i: (i, zcol)),
                  pl.BlockSpec((T, AW), row), pl.BlockSpec((T, D), row), _full((SUB, D)), _full((SUB, LANES)),
                  _full((D, D))],
        out_specs=[pl.BlockSpec((T, D), row), pl.BlockSpec((T, D), row), pl.BlockSpec((T, H * HD), row)],
        out_shape=[jax.ShapeDtypeStruct((L, D), F32), jax.ShapeDtypeStruct((L, D), F32),
                   jax.ShapeDtypeStruct((L, H * HD), MXU)],
        compiler_params=_params(1),
    )(o, p, ya, x, modrows, sp, w_out)


FF_COLS = 2
FF_CW = DFF // FF_COLS
FF_ROWS = 256


def _ffn_fwd_half(x2, modrows, vec, w_up, cff, w_down, j, d_prev):
    assert FF_COLS == 2
    L = x2.shape[0]
    T = _tile(L, FF_ROWS)
    nj = FF_COLS
    last = d_prev is not None

    def body(*refs):
        x_ref, mod_ref, vec_ref, wg_ref, wu_ref, cg_ref, cu_ref, wd_ref = refs[:8]
        if last:
            dp_ref, gp_ref, up_ref, f_ref, d_ref, x3_ref, carry_g, carry_u = refs[8:]
        else:
            h_ref, gp_ref, up_ref, f_ref, d_ref, carry_g, carry_u = refs[8:]

        @pl.when(pl.program_id(0) == 0)
        def _():
            carry_g[...] = jnp.zeros_like(carry_g)
            carry_u[...] = jnp.zeros_like(carry_u)

        xv = x_ref[...]
        n, _ = _rms(xv)
        hb = (n * vec_ref[1:2, :] * (1.0 + mod_ref[4:5, :]) + mod_ref[3:4, :]).astype(MXU)
        if not last:
            h_ref[...] = hb
        g = _dot(hb, wg_ref[...], NN)
        u = _dot(hb, wu_ref[...], NN)
        gp_ref[...] = g
        up_ref[...] = u
        gc, _ = _conv_fwd(g, cg_ref, 3, carry_g[...])
        uc, _ = _conv_fwd(u, cu_ref, 3, carry_u[...])
        carry_g[...] = g[T - SUB:T, :]
        carry_u[...] = u[T - SUB:T, :]
        fb = (gc * _sig(gc) * uc).astype(MXU)
        f_ref[...] = fb
        part = _dot(fb, wd_ref[...], NN)
        if last:
            dv = dp_ref[...] + part
            d_ref[...] = dv
            x3_ref[...] = xv + mod_ref[5:6, :] * dv
        else:
            d_ref[...] = part

    row = lambda i: (i, 0)
    rowD = pl.BlockSpec((T, D), row)
    rowC = pl.BlockSpec((T, FF_CW), row)
    in_specs = [rowD, _full((SUB, D)), _full((SUB, D)),
                pl.BlockSpec((D, FF_CW), lambda i: (0, j)), pl.BlockSpec((D, FF_CW), lambda i: (0, nj + j)),
                pl.BlockSpec((SUB, FF_CW), lambda i: (0, j)), pl.BlockSpec((SUB, FF_CW), lambda i: (0, nj + j)),
                pl.BlockSpec((FF_CW, D), lambda i: (j, 0))]
    half = [jax.ShapeDtypeStruct((L, FF_CW), F32), jax.ShapeDtypeStruct((L, FF_CW), F32),
            jax.ShapeDtypeStruct((L, FF_CW), MXU)]
    args = [x2, modrows, vec, w_up, w_up, cff, cff, w_down]
    if last:
        in_specs.append(rowD)
        args.append(d_prev)
        out_specs = [rowC, rowC, rowC, rowD, rowD]
        out_shape = half + [jax.ShapeDtypeStruct((L, D), F32), jax.ShapeDtypeStruct((L, D), F32)]
    else:
        out_specs = [rowD, rowC, rowC, rowC, rowD]
        out_shape = [jax.ShapeDtypeStruct((L, D), MXU)] + half + [jax.ShapeDtypeStruct((L, D), F32)]
    return pl.pallas_call(
        body, name="ffn_fwd_last" if last else "ffn_fwd_first", grid=(L // T,),
        in_specs=in_specs, out_specs=out_specs, out_shape=out_shape,
        scratch_shapes=[pltpu.VMEM((SUB, FF_CW), F32), pltpu.VMEM((SUB, FF_CW), F32)],
        compiler_params=_params(1),
    )(*args)


def _ffn_bwd_half(dx3, modrows, gpre, upre, cff, w_down, w_up, j, tail):
    assert FF_COLS == 2
    L = dx3.shape[0]
    T = _tile(L, FF_ROWS)
    ni, nj = L // T, FF_COLS
    hb_per_t = T // SUB
    last = tail is not None

    def body(*refs):
        dx3_ref, mod_ref, gp_ref, up_ref, gph_ref, uph_ref, cg_ref, cu_ref, wd_ref, wg_ref, wu_ref = refs[:11]
        if last:
            (d_ref, x2_ref, vec_ref, dhp_ref, dgp_ref, dup_ref, dx2_ref, accv_ref, dcg_ref, dcu_ref,
             carry_g, carry_u) = refs[11:]
        else:
            dd_ref, dgp_ref, dup_ref, dh_ref, dcg_ref, dcu_ref, carry_g, carry_u = refs[11:]
        i = pl.program_id(0)
        ri = ni - 1 - i

        @pl.when(i == 0)
        def _():
            carry_g[...] = jnp.zeros_like(carry_g)
            carry_u[...] = jnp.zeros_like(carry_u)
            dcg_ref[...] = jnp.zeros_like(dcg_ref)
            dcu_ref[...] = jnp.zeros_like(dcu_ref)
            if last:
                accv_ref[...] = jnp.zeros_like(accv_ref)

        dx3v = dx3_ref[...]
        ddb = (mod_ref[5:6, :] * dx3v).astype(MXU)
        if not last:
            dd_ref[...] = ddb
        g, u = gp_ref[...], up_ref[...]
        keep = jnp.where(ri == 0, 0.0, 1.0)
        gc, gsh = _conv_fwd(g, cg_ref, 3, gph_ref[...] * keep)
        uc, ush = _conv_fwd(u, cu_ref, 3, uph_ref[...] * keep)
        sg = _sig(gc)
        df = _dot(ddb, wd_ref[...], NT)
        duc = df * (gc * sg)
        dgc = df * uc * (sg * (1.0 + gc * (1.0 - sg)))
        for s in range(3):
            dcg_ref[2 - s:3 - s, :] += _sum0(dgc * gsh[s])
            dcu_ref[2 - s:3 - s, :] += _sum0(duc * ush[s])
        dg = _conv_bwd_in(dgc, cg_ref, 3, carry_g[...]).astype(MXU)
        du = _conv_bwd_in(duc, cu_ref, 3, carry_u[...]).astype(MXU)
        carry_g[...] = dgc[0:SUB, :]
        carry_u[...] = duc[0:SUB, :]
        dgp_ref[...] = dg
        dup_ref[...] = du
        dh = _dot(dg, wg_ref[...], NT) + _dot(du, wu_ref[...], NT)
        if last:
            dh = dh + dhp_ref[...]
            accv_ref[0:1, :] += _sum0(dx3v * d_ref[...])
            n, r = _rms(x2_ref[...])
            nw, sc = vec_ref[1:2, :], mod_ref[4:5, :]
            accv_ref[1:2, :] += _sum0(dh)
            accv_ref[2:3, :] += _sum0(dh * n * nw)
            accv_ref[3:4, :] += _sum0(dh * n * (1.0 + sc))
            dx2_ref[...] = _rms_bwd(dh * nw * (1.0 + sc), n, r) + dx3v
        else:
            dh_ref[...] = dh

    row = lambda i: (ni - 1 - i, 0)
    halo = lambda i: (jnp.maximum((ni - 1 - i) * hb_per_t - 1, 0), 0)
    rowD = pl.BlockSpec((T, D), row)
    rowC = pl.BlockSpec((T, FF_CW), row)
    haloC = pl.BlockSpec((SUB, FF_CW), halo)
    in_specs = [rowD, _full((SUB, D)), rowC, rowC, haloC, haloC,
                pl.BlockSpec((SUB, FF_CW), lambda i: (0, j)), pl.BlockSpec((SUB, FF_CW), lambda i: (0, nj + j)),
                pl.BlockSpec((FF_CW, D), lambda i: (j, 0)),
                pl.BlockSpec((D, FF_CW), lambda i: (0, j)), pl.BlockSpec((D, FF_CW), lambda i: (0, nj + j))]
    args = [dx3, modrows, gpre, upre, gpre, upre, cff, cff, w_down, w_up, w_up]
    halfb = [jax.ShapeDtypeStruct((L, FF_CW), MXU), jax.ShapeDtypeStruct((L, FF_CW), MXU)]
    dconv = [jax.ShapeDtypeStruct((SUB, FF_CW), F32)] * 2
    if last:
        d, x2, vec, dh_prev = tail
        in_specs += [rowD, rowD, _full((SUB, D)), rowD]
        args += [d, x2, vec, dh_prev]
        out_specs = [rowC, rowC, rowD, _full((SUB, D)), _full((SUB, FF_CW)), _full((SUB, FF_CW))]
        out_shape = halfb + [jax.ShapeDtypeStruct((L, D), F32), jax.ShapeDtypeStruct((SUB, D), F32)] + dconv
    else:
        out_specs = [rowD, rowC, rowC, rowD, _full((SUB, FF_CW)), _full((SUB, FF_CW))]
        out_shape = [jax.ShapeDtypeStruct((L, D), MXU)] + halfb + [jax.ShapeDtypeStruct((L, D), F32)] + dconv
    return pl.pallas_call(
        body, name="ffn_bwd_last" if last else "ffn_bwd_first", grid=(ni,),
        in_specs=in_specs, out_specs=out_specs, out_shape=out_shape,
        scratch_shapes=[pltpu.VMEM((SUB, FF_CW), F32), pltpu.VMEM((SUB, FF_CW), F32)],
        compiler_params=_params(1),
    )(*args)


def _final(x, target, nf):
    L = x.shape[0]
    T = _tile(L, 256)

    def body(x_ref, t_ref, nf_ref, dx_ref, acc_ref):
        @pl.when(pl.program_id(0) == 0)
        def _():
            acc_ref[...] = jnp.zeros_like(acc_ref)

        n, r = _rms(x_ref[...])
        w = nf_ref[0:1, :]
        err = n * w - t_ref[...]
        acc_ref[0:1, :] += (0.5 / D) * _sum0(err * err)
        dy = err * (1.0 / D)
        acc_ref[1:2, :] += _sum0(dy * n)
        dx_ref[...] = _rms_bwd(dy * w, n, r)

    row = lambda i: (i, 0)
    return pl.pallas_call(
        body, name="final_norm_loss", grid=(L // T,),
        in_specs=[pl.BlockSpec((T, D), row), pl.BlockSpec((T, D), row), _full((SUB, D))],
        out_specs=[pl.BlockSpec((T, D), row), _full((SUB, D))],
        out_shape=[jax.ShapeDtypeStruct((L, D), F32), jax.ShapeDtypeStruct((SUB, D), F32)],
        compiler_params=_params(1),
    )(x, target, nf)


def _post_bwd(dx2, y, o, p, modrows, sp, w_out):
    L = dx2.shape[0]
    T = _tile(L, 256)

    def body(dx2_ref, y_ref, o_ref, z_ref, mod_ref, sp_ref, w_ref, dy_ref, do_ref, dz_ref, dya_ref, accv_ref, accs_ref):
        @pl.when(pl.program_id(0) == 0)
        def _():
            accv_ref[...] = jnp.zeros_like(accv_ref)
            accs_ref[...] = jnp.zeros_like(accs_ref)

        dx2v = dx2_ref[...]
        accv_ref[0:1, :] += _sum0(dx2v * y_ref[...])
        dyb = (mod_ref[2:3, :] * dx2v).astype(MXU)
        dy_ref[...] = dyb
        dyc = _dot(dyb, w_ref[...], NT)
        dya_ref[...] = dyc[:, 0:AW]
        ndw = sp_ref[2:3, :]
        z = z_ref[...]
        sgz = _sig(z)
        dsz = sgz * (1.0 + z * (1.0 - sgz))
        dndw = jnp.zeros((1, HD), F32)
        for h in range(H):
            sl = slice(h * HD, (h + 1) * HD)
            n, r = _rms(o_ref[:, sl])
            dyh = dyc[:, AW + h * HD:AW + (h + 1) * HD]
            zh = z[:, sl]
            don = dyh * (zh * sgz[:, sl])
            dz_ref[:, sl] = dyh * (n * ndw) * dsz[:, sl]
            dndw = dndw + _sum0(don * n)
            do_ref[:, sl] = _rms_bwd(don * ndw, n, r)
        accs_ref[0:1, :] += dndw

    row = lambda i: (i, 0)
    zcol = (3 * AW + 3 * H * HD) // (H * HD)
    return pl.pallas_call(
        body, name="post_bwd", grid=(L // T,),
        in_specs=[pl.BlockSpec((T, D), row), pl.BlockSpec((T, D), row), pl.BlockSpec((T, H * HD), row),
                  pl.BlockSpec((T, H * HD), lambda i: (i, zcol)), _full((SUB, D)), _full((SUB, LANES)), _full((D, D))],
        out_specs=[pl.BlockSpec((T, D), row)] + [pl.BlockSpec((T, H * HD), row)] * 3 + [_full((SUB, D)), _full((SUB, LANES))],
        out_shape=[jax.ShapeDtypeStruct((L, D), MXU)] + [jax.ShapeDtypeStruct((L, H * HD), F32)] * 3
        + [jax.ShapeDtypeStruct((SUB, D), F32), jax.ShapeDtypeStruct((SUB, LANES), F32)],
        compiler_params=_params(1),
    )(dx2, y, o, p, modrows, sp, w_out)


def _pre_bwd(p, dqn, dkn, dvs, dya, dz, dgb, pa, cq, sp):
    L = p.shape[0]
    T = _tile(L, 256)
    ni = L // T
    scale = HD ** -0.5
    w3 = 3 * AW + 3 * H * HD
    hb_per_t = T // SUB

    def body(pm_ref, ph_ref, ps_ref, dq_ref, dk_ref, dv_ref, dya_ref, dz_ref, dgb_ref, pa_ref, cq_ref, sp_ref,
             dp_ref, dpa_ref, dcq_ref, dsp_ref, carry_u, carry_q):
        i = pl.program_id(0)
        ri = ni - 1 - i

        @pl.when(i == 0)
        def _():
            dpa_ref[...] = jnp.zeros_like(dpa_ref)
            dcq_ref[...] = jnp.zeros_like(dcq_ref)
            dsp_ref[...] = jnp.zeros_like(dsp_ref)
            carry_u[...] = jnp.zeros_like(carry_u)
            carry_q[...] = jnp.zeros_like(carry_q)

        keep = jnp.where(ri == 0, 0.0, 1.0)
        a_b, a_c, a_x = pm_ref[:, 0:AW], pm_ref[:, AW:2 * AW], pm_ref[:, 2 * AW:3 * AW]
        u = a_c * a_x
        hu = ph_ref[:, AW:2 * AW] * ph_ref[:, 2 * AW:3 * AW] * keep
        cu, ush = _conv_fwd(u, pa_ref, 3, hu)
        yp = a_b * cu
        bd = _blockdiag_mean(AW, A_GROUP)
        ra = lax.rsqrt(_dot_f32(yp * yp, bd, NN, exact="b") + EPS)
        na = yp * ra
        dya = dya_ref[...]
        dpa_ref[3:4, :] += _sum0(dya * na)
        dna = dya * pa_ref[3:4, :]
        dyp = ra * (dna - na * _dot_f32(dna * na, bd, NN, exact="b"))
        dcu = dyp * a_b
        for s in range(3):
            dpa_ref[2 - s:3 - s, :] += _sum0(dcu * ush[s])
        du = _conv_bwd_in(dcu, pa_ref, 3, carry_u[...])
        carry_u[...] = dcu[0:SUB, :]
        dp_ref[:, 0:AW] = (dyp * cu).astype(MXU)
        dp_ref[:, AW:2 * AW] = (du * a_x).astype(MXU)
        dp_ref[:, 2 * AW:3 * AW] = (du * a_c).astype(MXU)

        qkv = pm_ref[:, 3 * AW:w3]
        qc, qsh = _conv_fwd(qkv, cq_ref, 4, ph_ref[:, 3 * AW:w3] * keep)
        sg = _sig(qc)
        qs = qc * sg
        parts = []
        for h in range(H):
            q = qs[:, h * HD:(h + 1) * HD]
            rq = lax.rsqrt(jnp.sum(q * q, axis=-1, keepdims=True) + EPS)
            parts.append(_l2_bwd(dq_ref[:, h * HD:(h + 1) * HD] * scale, q * rq, rq))
        for h in range(H):
            k = qs[:, (H + h) * HD:(H + h + 1) * HD]
            rk = lax.rsqrt(jnp.sum(k * k, axis=-1, keepdims=True) + EPS)
            parts.append(_l2_bwd(dk_ref[:, h * HD:(h + 1) * HD], k * rk, rk))
        parts.append(dv_ref[...])
        dqc = jnp.concatenate(parts, axis=-1) * (sg * (1.0 + qc * (1.0 - sg)))
        for s in range(4):
            dcq_ref[3 - s:4 - s, :] += _sum0(dqc * qsh[s])
        dp_ref[:, 3 * AW:w3] = _conv_bwd_in(dqc, cq_ref, 4, carry_q[...]).astype(MXU)
        carry_q[...] = dqc[0:SUB, :]
        dp_ref[:, w3:w3 + H * HD] = dz_ref[...].astype(MXU)

        lane, a, xb, beta, g = _gate_small(ps_ref[...], sp_ref)
        dgb = dgb_ref[...]
        dbeta = jnp.where(lane < H, dgb, 0.0)
        dg = jnp.where((lane >= H) & (lane < 2 * H), dgb, 0.0)
        dalpha = dg * a * _sig(xb)
        dsp_ref[0:1, :] += _sum0(dg * g)
        dsp_ref[1:2, :] += _sum0(dalpha)
        dp_ref[:, w3 + H * HD:P_PAD] = (dbeta * beta * (1.0 - beta) + dalpha).astype(MXU)

    row = lambda i: (ni - 1 - i, 0)
    halo = lambda i: (jnp.maximum((ni - 1 - i) * hb_per_t - 1, 0), 0)
    hrow = pl.BlockSpec((T, H * HD), row)
    return pl.pallas_call(
        body, name="pre_bwd", grid=(ni,),
        in_specs=[pl.BlockSpec((T, w3), row), pl.BlockSpec((SUB, w3), halo),
                  pl.BlockSpec((T, LANES), lambda i: (ni - 1 - i, (P_PAD - LANES) // LANES)),
                  hrow, hrow, hrow, pl.BlockSpec((T, AW), row), hrow,
                  pl.BlockSpec((T, LANES), row),
                  _full((SUB, AW)), _full((SUB, 3 * H * HD)), _full((SUB, LANES))],
        out_specs=[pl.BlockSpec((T, P_PAD), row), _full((SUB, AW)), _full((SUB, 3 * H * HD)), _full((SUB, LANES))],
        out_shape=[jax.ShapeDtypeStruct((L, P_PAD), MXU), jax.ShapeDtypeStruct((SUB, AW), F32),
                   jax.ShapeDtypeStruct((SUB, 3 * H * HD), F32), jax.ShapeDtypeStruct((SUB, LANES), F32)],
        scratch_shapes=[pltpu.VMEM((SUB, AW), F32), pltpu.VMEM((SUB, 3 * H * HD), F32)],
        compiler_params=_params(1),
    )(p, p, p, dqn, dkn, dvs, dya, dz, dgb, pa, cq, sp)


def _in_bwd(dp, w_in, x, dx2, modrows, vec):
    L = x.shape[0]
    T = _tile(L, 256)

    def body(dp_ref, w_ref, x_ref, dx2_ref, mod_ref, vec_ref, dx_ref, accv_ref):
        @pl.when(pl.program_id(0) == 0)
        def _():
            accv_ref[...] = jnp.zeros_like(accv_ref)

        dh = _dot(dp_ref[...], w_ref[...], NT)
        n, r = _rms(x_ref[...])
        nw, sc = vec_ref[0:1, :], mod_ref[1:2, :]
        accv_ref[0:1, :] += _sum0(dh)
        accv_ref[1:2, :] += _sum0(dh * n * nw)
        accv_ref[2:3, :] += _sum0(dh * n * (1.0 + sc))
        dx_ref[...] = _rms_bwd(dh * nw * (1.0 + sc), n, r) + dx2_ref[...]

    row = lambda i: (i, 0)
    return pl.pallas_call(
        body, name="in_bwd", grid=(L // T,),
        in_specs=[pl.BlockSpec((T, P_PAD), row), _full((D, P_PAD)), pl.BlockSpec((T, D), row),
                  pl.BlockSpec((T, D), row), _full((SUB, D)), _full((SUB, D))],
        out_specs=[pl.BlockSpec((T, D), row), _full((SUB, D))],
        out_shape=[jax.ShapeDtypeStruct((L, D), F32), jax.ShapeDtypeStruct((SUB, D), F32)],
        compiler_params=_params(1),
    )(dp, w_in, x, dx2, modrows, vec)


def _wgrad(a, b, tm, tn, name):
    L, m = a.shape
    n = b.shape[1]
    tl = _tile(L, 512)
    tm, tn = _tile(m, tm), _tile(n, tn)
    nl = L // tl

    def body(a_ref, b_ref, o_ref):
        @pl.when(pl.program_id(2) == 0)
        def _():
            o_ref[...] = jnp.zeros_like(o_ref)

        o_ref[...] += _dot(a_ref[...], b_ref[...], TN)

    return pl.pallas_call(
        body, name=name, grid=(m // tm, n // tn, nl),
        in_specs=[pl.BlockSpec((tl, tm), lambda i, j, l: (l, i)), pl.BlockSpec((tl, tn), lambda i, j, l: (l, j))],
        out_specs=pl.BlockSpec((tm, tn), lambda i, j, l: (i, j)),
        out_shape=jax.ShapeDtypeStruct((m, n), F32), compiler_params=_params(3),
    )(a, b)


def _wgrad_cols(a, b, tm, n_shard, wpad, count, name):
    L, m = a.shape
    n = b.shape[1]
    tl = _tile(L, 512)
    tm = _tile(m, tm)
    nl = L // tl
    wins = _shard_windows(n_shard, count)
    assert all(a_ * LANES + win <= n for a_, _, win in wins), (wins, n)

    def body(a_ref, b_ref, o_ref, acc):
        @pl.when(pl.program_id(1) == 0)
        def _():
            acc[...] = jnp.zeros_like(acc)

        acc[...] += _dot(a_ref[...], b_ref[...], TN)

        @pl.when(pl.program_id(1) == nl - 1)
        def _():
            for k, (a_, s, win) in enumerate(wins):
                xk = acc[:, a_ * LANES:a_ * LANES + win]
                if s:
                    xk = pltpu.roll(xk, win - s, 1)
                o_ref[k] = _fit_lanes(xk, wpad)

    return pl.pallas_call(
        body, name=name, grid=(m // tm, nl),
        in_specs=[pl.BlockSpec((tl, tm), lambda i, l: (l, i)), pl.BlockSpec((tl, n), lambda i, l: (l, 0))],
        out_specs=pl.BlockSpec((count, tm, wpad), lambda i, l: (0, i, 0)),
        out_shape=jax.ShapeDtypeStruct((count, m, wpad), F32),
        scratch_shapes=[pltpu.VMEM((tm, n), F32)],
        compiler_params=_params(2),
    )(a, b)


def _adamw(w, g, m, v, name):
    r, n = w.shape
    tr = _tile(r, 512)
    bc1 = 1.0 - ADAM_B1 ** ADAM_STEP
    bc2 = 1.0 - ADAM_B2 ** ADAM_STEP

    def body(w_ref, g_ref, m_ref, v_ref, d_ref, nm_ref, nv_ref):
        gv = g_ref[...]
        nm = ADAM_B1 * m_ref[...] + (1.0 - ADAM_B1) * gv
        nv = ADAM_B2 * v_ref[...] + (1.0 - ADAM_B2) * (gv * gv)
        nm_ref[...] = nm
        nv_ref[...] = nv
        d_ref[...] = -ADAM_LR * ((nm / bc1) / (jnp.sqrt(nv / bc2) + ADAM_EPS) + ADAM_WD * w_ref[...])

    spec = pl.BlockSpec((tr, n), lambda i: (i, 0))
    return pl.pallas_call(
        body, name=name, grid=(r // tr,), in_specs=[spec] * 4, out_specs=[spec] * 3,
        out_shape=[jax.ShapeDtypeStruct((r, n), F32)] * 3, compiler_params=_params(1),
    )(w, g, m, v)


def _rows8(rows, width):
    out = jnp.zeros((SUB, width), F32)
    for r, vrow in enumerate(rows):
        out = out.at[r, :vrow.shape[0]].set(vrow)
    return out


def _at_lanes(v4, start):
    return jnp.zeros((LANES,), F32).at[start:start + v4.shape[0]].set(v4)


def _pad_rows(flat, mult):
    n = flat.shape[0]
    pad = (-n) % mult
    return jnp.pad(flat, (0, pad)) if pad else flat


IN_PAD = 512
UP_PAD = 768


def _local_fwd_bwd(x, target, mod_full, small_w, full_w):
    norm1_w, norm2_w, norm_a_w, a_log, dt_bias, norm_dn_w, norm_f_w = small_w
    w_in_f, w_out_f, w_up_f, w_down_f, conv_a_f, conv_q_f, conv_f_f = full_w

    def layer_params(i):
        modrows = jnp.concatenate([mod_full[i], jnp.zeros((SUB - N_MOD, D), F32)], axis=0)
        vec = _rows8([norm1_w[i], norm2_w[i]], D)
        pa = _rows8([conv_a_f[i, 0], conv_a_f[i, 1], conv_a_f[i, 2], norm_a_w[i]], AW)
        cq = _rows8([conv_q_f[i, k] for k in range(4)], 3 * H * HD)
        sp = _rows8([_at_lanes(a_log[i], H), _at_lanes(dt_bias[i], H), norm_dn_w[i]], LANES)
        cff = _rows8([conv_f_f[i, k] for k in range(3)], 2 * DFF)
        return modrows, vec, pa, cq, sp, cff

    saved = []
    xi = x
    for i in range(DEPTH):
        modrows, vec, pa, cq, sp, cff = layer_params(i)
        p, h1 = _in_proj(xi, modrows, vec, w_in_f[i])
        qn, kn, vs, gb, ya = _pre_fwd(p, pa, cq, sp)
        o, states, tinvs = _gdr_fwd(qn, kn, vs, gb)
        y, x2, yb = _post_fwd(o, p, ya, xi, modrows, sp, w_out_f[i])
        h2, gp0, up0, f0, d0 = _ffn_fwd_half(x2, modrows, vec, w_up_f[i], cff, w_down_f[i], 0, None)
        gp1, up1, f1, dff, x3 = _ffn_fwd_half(x2, modrows, vec, w_up_f[i], cff, w_down_f[i], 1, d0)
        saved.append(dict(x=xi, p=p, h1=h1, qn=qn, kn=kn, vs=vs, gb=gb, ya=ya, o=o, states=states, tinvs=tinvs, y=y, x2=x2, yb=yb,
                          h2=h2, gpre=(gp0, gp1), upre=(up0, up1), f=(f0, f1), d=dff))
        xi = x3

    dx, facc = _final(xi, target, _rows8([norm_f_w], D))
    loss_local = jnp.sum(facc[0])
    d_norm_f = facc[1]

    gw_in, gw_out, gw_up, gw_down = [None] * DEPTH, [None] * DEPTH, [None] * DEPTH, [None] * DEPTH
    g_small = [None] * DEPTH
    for i in reversed(range(DEPTH)):
        s = saved[i]
        modrows, vec, pa, cq, sp, cff = layer_params(i)
        dd, dgp0, dup0, dh0, dcg0, dcu0 = _ffn_bwd_half(dx, modrows, s["gpre"][0], s["upre"][0], cff,
                                                        w_down_f[i], w_up_f[i], 0, None)
        dgp1, dup1, dx2, accf, dcg1, dcu1 = _ffn_bwd_half(dx, modrows, s["gpre"][1], s["upre"][1], cff,
                                                          w_down_f[i], w_up_f[i], 1, (s["d"], s["x2"], vec, dh0))
        n_up, up_pad = 2 * DFF // N_DEV, UP_PAD
        gw_up[i] = jnp.concatenate([_wgrad_cols(s["h2"], t, 512, n_up, up_pad, FF_CW // n_up, "wgrad_up")
                                    for t in (dgp0, dgp1, dup0, dup1)], axis=0)
        gw_down[i] = jnp.concatenate([_wgrad(s["f"][0], dd, FF_CW, 1024, "wgrad_down"),
                                      _wgrad(s["f"][1], dd, FF_CW, 1024, "wgrad_down")],
                                     axis=0).reshape(N_DEV, DFF // N_DEV, D)
        dy, do, dz, dya, accp, accs = _post_bwd(dx2, s["y"], s["o"], s["p"], modrows, sp, w_out_f[i])
        gw_out[i] = jnp.concatenate([_wgrad(s["ya"], dy, 512, 1024, "wgrad_out"),
                                     _wgrad(s["yb"], dy, 512, 1024, "wgrad_out")], axis=0).reshape(N_DEV, D // N_DEV, D)
        dqn, dkn, dvs, dgb = _gdr_bwd(s["qn"], s["kn"], s["vs"], s["gb"], s["states"], s["tinvs"], do)
        dp, dpa, dcq, dsp = _pre_bwd(s["p"], dqn, dkn, dvs, dya, dz, dgb, pa, cq, sp)
        gw_in[i] = _wgrad_cols(s["h1"], dp, 512, P_IN // N_DEV, IN_PAD, N_DEV, "wgrad_in")
        dx, acci = _in_bwd(dp, w_in_f[i], s["x"], dx2, modrows, vec)
        dconv_ff = jnp.concatenate([dcg0, dcg1, dcu0, dcu1], axis=1)[0:3]
        dmod = jnp.stack([acci[0], acci[1], accp[0], accf[1], accf[2], accf[0]])
        g_small[i] = dict(norm1=acci[2], norm2=accf[3], norm_a=dpa[3], a_log=dsp[0, H:2 * H], dt_bias=dsp[1, H:2 * H],
                          norm_dn=accs[0], conv_a=dpa[0:3], conv_qkv=dcq[0:4], conv_ff=dconv_ff, dmod=dmod.reshape(-1))
    return loss_local, dx, gw_in, gw_out, gw_up, gw_down, g_small, d_norm_f


def kernel(x, c, ada_w, ada_b, norm1_w, w_in, conv_a_w, norm_a_w, conv_qkv_w, a_log, dt_bias, norm_dn_w, w_out, norm2_w, w_up, conv_ff_w, w_down, norm_f_w, loss_target, m_ada_w, m_ada_b, m_norm1_w, m_w_in, m_conv_a_w, m_norm_a_w, m_conv_qkv_w, m_a_log, m_dt_bias, m_norm_dn_w, m_w_out, m_norm2_w, m_w_up, m_conv_ff_w, m_w_down, m_norm_f_w, v_ada_w, v_ada_b, v_norm1_w, v_w_in, v_conv_a_w, v_norm_a_w, v_conv_qkv_w, v_a_log, v_dt_bias, v_norm_dn_w, v_w_out, v_norm2_w, v_w_up, v_conv_ff_w, v_w_down, v_norm_f_w):
    ax, ay, ac = lax.axis_index("x"), lax.axis_index("y"), lax.axis_index("c")
    me = 4 * ax + 2 * ay + ac
    x = x[0]
    target = loss_target[0]
    n_in, n_up = P_IN // N_DEV, 2 * DFF // N_DEV

    def lane_pad(t, width):
        return jnp.pad(t.astype(MXU), ((0, 0), (0, 0), (0, width - t.shape[-1])))

    conv_blob = _pad_rows(jnp.concatenate([t.reshape(-1) for t in (conv_a_w, conv_qkv_w, conv_ff_w)]),
                          SUB * LANES).reshape(-1, LANES)
    c_rows = jnp.zeros((SUB, D), F32).at[0].set(c[0])
    g_in, g_out, g_up, g_down, g_conv, g_c = _all_gather(
        [lane_pad(w_in, IN_PAD), w_out.astype(MXU), lane_pad(w_up, UP_PAD), w_down.astype(MXU), conv_blob, c_rows],
        "gather_weights", in_vmem=False)
    w_in_f = _interleave_cols(g_in, n_in, P_PAD, "interleave_w_in")
    w_up_f = _interleave_cols(g_up, n_up, 2 * DFF, "interleave_w_up")
    w_out_f = g_out.transpose(1, 0, 2, 3).reshape(DEPTH, D, D)
    w_down_f = g_down.transpose(1, 0, 2, 3).reshape(DEPTH, DFF, D)
    sg = g_conv.reshape(N_DEV, -1)
    o1 = conv_a_w.size
    o2 = o1 + conv_qkv_w.size
    o3 = o2 + conv_ff_w.size
    conv_a_f = sg[:, 0:o1].reshape(N_DEV, DEPTH, 3, AW // N_DEV).transpose(1, 2, 0, 3).reshape(DEPTH, 3, AW)
    conv_q_f = sg[:, o1:o2].reshape(N_DEV, DEPTH, 4, 3 * H * HD // N_DEV).transpose(1, 2, 0, 3).reshape(DEPTH, 4, 3 * H * HD)
    conv_f_f = sg[:, o2:o3].reshape(N_DEV, DEPTH, 3, n_up).transpose(1, 2, 0, 3).reshape(DEPTH, 3, 2 * DFF)

    c_all = jnp.concatenate([g_c[:, 0], jnp.zeros((16 - N_DEV, D), F32)], axis=0)
    n_ada = N_MOD * D // N_DEV
    ada_b_cols = lax.dynamic_slice_in_dim(ada_b, me * n_ada, n_ada, axis=1)[:, None, :]
    mod_sh = _mod_fwd(c_all, ada_w, ada_b_cols)
    mod_all = _all_gather([mod_sh.reshape(DEPTH * 16, n_ada)], "gather_mod", in_vmem=True)[0]
    mod_all = mod_all.reshape(N_DEV, DEPTH, 16, n_ada)
    mod_mine = lax.dynamic_index_in_dim(mod_all, me, axis=2, keepdims=False)
    mod_full = mod_mine.transpose(1, 0, 2).reshape(DEPTH, N_MOD, D)

    loss_local, dx, gw_in, gw_out, gw_up, gw_down, g_small, d_norm_f = _local_fwd_bwd(
        x, target, mod_full, (norm1_w, norm2_w, norm_a_w, a_log, dt_bias, norm_dn_w, norm_f_w),
        (w_in_f, w_out_f, w_up_f, w_down_f, conv_a_f, conv_q_f, conv_f_f))
    loss = lax.psum(loss_local, ("x", "y", "c"))
    grad_x = dx[None]

    keys = ["dmod", "norm1", "norm2", "norm_a", "a_log", "dt_bias", "norm_dn", "conv_a", "conv_qkv", "conv_ff"]
    stacked = {k: jnp.stack([g_small[i][k] for i in range(DEPTH)]) for k in keys}
    flat_parts = [stacked[k].reshape(-1) for k in keys] + [d_norm_f]
    sizes = [int(t.shape[0]) for t in flat_parts]
    sflat = _pad_rows(jnp.concatenate(flat_parts), SUB * LANES).reshape(-1, LANES)
    sall = _all_gather([sflat], "gather_small_grads", in_vmem=True)[0]
    ssum = _sum_devices(sall).reshape(-1)
    so = [0]
    for sz in sizes:
        so.append(so[-1] + sz)
    red = {k: ssum[so[n]:so[n + 1]].reshape(stacked[k].shape) for n, k in enumerate(keys)}
    g_norm_f = ssum[so[len(keys)]:so[len(keys) + 1]]
    dmod_all = sall[:, 0:sizes[0] // LANES, :].reshape(N_DEV, DEPTH, N_MOD * D)

    g_ada_b = red["dmod"].reshape(DEPTH, N_MOD * D)
    dmod_cols = lax.dynamic_slice_in_dim(dmod_all, me * n_ada, n_ada, axis=2).transpose(1, 0, 2)
    dmod_cols = jnp.concatenate([dmod_cols, jnp.zeros((DEPTH, 16 - N_DEV, n_ada), F32)], axis=1)
    g_ada_w = _mod_bwd(c_all, dmod_cols)
    g_conv_a = lax.dynamic_slice_in_dim(red["conv_a"], me * (AW // N_DEV), AW // N_DEV, axis=2)
    g_conv_qkv = lax.dynamic_slice_in_dim(red["conv_qkv"], me * (3 * H * HD // N_DEV), 3 * H * HD // N_DEV, axis=2)
    g_conv_ff = lax.dynamic_slice_in_dim(red["conv_ff"], me * n_up, n_up, axis=2)

    tags = ["w_in", "w_out", "w_up", "w_down"]
    gs = [jnp.stack(t, axis=1) for t in (gw_in, gw_out, gw_up, gw_down)]
    my_c = jnp.reshape(ac, (1,)).astype(jnp.int32)
    my_chip = jnp.reshape(2 * ax + ay, (1,)).astype(jnp.int32)
    recv1 = _rs_sibling(gs)
    pairs = [_rs_add_pairs(g, r, my_c, "rs_add_pairs_" + t) for g, r, t in zip(gs, recv1, tags)]
    recv2 = _rs_chips([pb for _, pb in pairs])
    mine = [_rs_add_chips(pf, r, my_chip, "rs_add_chips_" + t) for (pf, _), r, t in zip(pairs, recv2, tags)]
    g_w_in = mine[0][:, :, :n_in]
    g_w_out = mine[1]
    g_w_up = mine[2][:, :, :n_up]
    g_w_down = mine[3]

    grads = dict(ada_w=g_ada_w, ada_b=g_ada_b, norm1_w=red["norm1"], w_in=g_w_in, conv_a_w=g_conv_a,
                 norm_a_w=red["norm_a"], conv_qkv_w=g_conv_qkv, a_log=red["a_log"], dt_bias=red["dt_bias"],
                 norm_dn_w=red["norm_dn"], w_out=g_w_out, norm2_w=red["norm2"], w_up=g_w_up, conv_ff_w=g_conv_ff,
                 w_down=g_w_down, norm_f_w=g_norm_f)
    weights = dict(ada_w=ada_w, ada_b=ada_b, norm1_w=norm1_w, w_in=w_in, conv_a_w=conv_a_w, norm_a_w=norm_a_w,
                   conv_qkv_w=conv_qkv_w, a_log=a_log, dt_bias=dt_bias, norm_dn_w=norm_dn_w, w_out=w_out,
                   norm2_w=norm2_w, w_up=w_up, conv_ff_w=conv_ff_w, w_down=w_down, norm_f_w=norm_f_w)
    ms = dict(ada_w=m_ada_w, ada_b=m_ada_b, norm1_w=m_norm1_w, w_in=m_w_in, conv_a_w=m_conv_a_w, norm_a_w=m_norm_a_w,
              conv_qkv_w=m_conv_qkv_w, a_log=m_a_log, dt_bias=m_dt_bias, norm_dn_w=m_norm_dn_w, w_out=m_w_out,
              norm2_w=m_norm2_w, w_up=m_w_up, conv_ff_w=m_conv_ff_w, w_down=m_w_down, norm_f_w=m_norm_f_w)
    vs_ = dict(ada_w=v_ada_w, ada_b=v_ada_b, norm1_w=v_norm1_w, w_in=v_w_in, conv_a_w=v_conv_a_w, norm_a_w=v_norm_a_w,
               conv_qkv_w=v_conv_qkv_w, a_log=v_a_log, dt_bias=v_dt_bias, norm_dn_w=v_norm_dn_w, w_out=v_w_out,
               norm2_w=v_norm2_w, w_up=v_w_up, conv_ff_w=v_conv_ff_w, w_down=v_w_down, norm_f_w=v_norm_f_w)
    names = list(weights)
    big_names = ["ada_w", "w_in", "w_out", "w_up", "w_down"]
    delta, new_m, new_v = {}, {}, {}
    for n in big_names:
        shp = weights[n].shape
        two = lambda t: t.reshape(-1, shp[-1])
        dl, nm, nv = _adamw(two(weights[n]), two(grads[n]), two(ms[n]), two(vs_[n]), "adamw_" + n)
        delta[n], new_m[n], new_v[n] = dl.reshape(shp), nm.reshape(shp), nv.reshape(shp)
    small_names = [n for n in names if n not in big_names]

    def pack(dct):
        return _pad_rows(jnp.concatenate([dct[n].reshape(-1) for n in small_names]), SUB * LANES).reshape(-1, LANES)

    dl, nm, nv = _adamw(pack(weights), pack(grads), pack(ms), pack(vs_), "adamw_small")
    off = 0
    for n in small_names:
        sz, shp = weights[n].size, weights[n].shape
        delta[n] = dl.reshape(-1)[off:off + sz].reshape(shp)
        new_m[n] = nm.reshape(-1)[off:off + sz].reshape(shp)
        new_v[n] = nv.reshape(-1)[off:off + sz].reshape(shp)
        off += sz

    return (loss, grad_x, *[grads[n] for n in names], *[delta[n] for n in names],
            *[new_m[n] for n in names], *[new_v[n] for n in names])
```

```python
import functools
import math

import jax
import jax.numpy as jnp
from jax import lax
from jax.experimental import pallas as pl
from jax.experimental.pallas import tpu as pltpu
from jax.experimental.pallas import tpu_sc as plsc

F32 = jnp.float32
MXU = jnp.bfloat16

D = 1024
DEPTH = 4
N_MOD = 6
AW = 512
A_GROUP = 64
H = 4
HD = 128
CK = 64
DFF = 2816
P_IN = 3592
P_PAD = 3712
EPS = 1e-6
N_DEV = 8
LANES = 128
SUB = 8
VMEM_LIMIT = 56 * 1024 * 1024

ADAM_LR, ADAM_B1, ADAM_B2, ADAM_EPS, ADAM_WD, ADAM_STEP = 0.001, 0.9, 0.999, 1e-08, 0.01, 10

NN = ((1,), (0,))
NT = ((1,), (1,))
TN = ((0,), (0,))
HI = lax.Precision.HIGHEST
MESH = pl.DeviceIdType.MESH


def _dot(a, b, dims, prec=None):
    if prec is None:
        a = a.astype(MXU) if a.dtype == F32 else a
        b = b.astype(MXU) if b.dtype == F32 else b
    return lax.dot_general(a, b, (dims, ((), ())), precision=prec, preferred_element_type=F32)


def _params(n_grid=0, limit=VMEM_LIMIT):
    sem = ("arbitrary",) * n_grid if n_grid else None
    return pltpu.CompilerParams(dimension_semantics=sem, vmem_limit_bytes=limit)


def _tile(n, want):
    if n <= want:
        return n
    t = want - want % SUB
    while n % t:
        t -= SUB
    assert t > 0, (n, want)
    return t


def _full(shape):
    nd = len(shape)
    return pl.BlockSpec(shape, lambda *_: (0,) * nd)


def _sig(x):
    return jax.nn.sigmoid(x)


def _rms(x):
    r = lax.rsqrt(jnp.mean(x * x, axis=-1, keepdims=True) + EPS)
    return x * r, r


def _rms_bwd(dn, n, r):
    return r * (dn - n * jnp.mean(dn * n, axis=-1, keepdims=True))


def _l2_bwd(dn, n, r):
    return r * (dn - n * jnp.sum(dn * n, axis=-1, keepdims=True))


def _sum0(x):
    return jnp.sum(x, axis=0, keepdims=True)


def _shift_down(x, s, halo):
    ext = jnp.concatenate([halo, x], axis=0)
    return pltpu.roll(ext, s, 0)[SUB:, :]


def _shift_up(x, s, halo):
    t = x.shape[0]
    ext = jnp.concatenate([x, halo], axis=0)
    return pltpu.roll(ext, t + SUB - s, 0)[:t, :]


def _conv_fwd(x, w_ref, width, halo):
    sh = [x] + [_shift_down(x, s, halo) for s in range(1, width)]
    out = w_ref[width - 1:width, :] * sh[0]
    for s in range(1, width):
        out = out + w_ref[width - 1 - s:width - s, :] * sh[s]
    return out, sh


def _conv_bwd_in(dout, w_ref, width, halo_next):
    dx = w_ref[width - 1:width, :] * dout
    for s in range(1, width):
        dx = dx + w_ref[width - 1 - s:width - s, :] * _shift_up(dout, s, halo_next)
    return dx


def _blockdiag_mean(n, group):
    r = lax.shift_right_logical(lax.broadcasted_iota(jnp.int32, (n, n), 0), int(math.log2(group)))
    c = lax.shift_right_logical(lax.broadcasted_iota(jnp.int32, (n, n), 1), int(math.log2(group)))
    return jnp.where(r == c, 1.0 / group, 0.0).astype(F32)


def _softplus(x):
    return jnp.maximum(x, 0.0) + jnp.log(1.0 + jnp.exp(-jnp.abs(x)))


def _my_place():
    return lax.axis_index("x"), lax.axis_index("y"), lax.axis_index("c")


def _all_gather(shards, name, in_vmem):
    nt = len(shards)

    def body(*refs):
        x_refs, out_refs = refs[:nt], refs[nt:2 * nt]
        send_sems, recv_sems, local_sems = refs[2 * nt:]
        x, y, c = _my_place()
        me, sibling = (x, y, c), (x, y, 1 - c)
        chips = [(1 - x, y), (x, 1 - y), (1 - x, 1 - y)]
        everything = []
        for t in range(nt):
            x_ref, out_ref = x_refs[t], out_refs[t]

            def blk(px, py, pc, out_ref=out_ref):
                return out_ref.at[4 * px + 2 * py + pc]

            def copy(k, block, to, src=None, t=t, blk=blk):
                return pltpu.make_async_remote_copy(
                    src_ref=blk(*block) if src is None else src, dst_ref=blk(*block),
                    send_sem=send_sems.at[7 * t + k], recv_sem=recv_sems.at[7 * t + k], device_id=to, device_id_type=MESH)

            mine = pltpu.make_async_copy(x_ref, blk(*me), local_sems.at[t])
            mine.start()
            first = [copy(0, me, sibling, src=x_ref)]
            first += [copy(1 + j, me, (*chip, c), src=x_ref) for j, chip in enumerate(chips)]
            for cp in first:
                cp.start()
            everything.append((copy, mine, first))
        sends = []
        for copy, mine, first in everything:
            passed = [copy(4 + j, (*chip, c), sibling) for j, chip in enumerate(chips)]
            for j, chip in enumerate(chips):
                copy(1 + j, (*chip, c), me).wait_recv()
                passed[j].start()
            sends += first + passed
        for copy, mine, first in everything:
            copy(0, sibling, me).wait_recv()
            for j, chip in enumerate(chips):
                copy(4 + j, (*chip, 1 - c), me).wait_recv()
        for cp in sends:
            cp.wait_send()
        for copy, mine, first in everything:
            mine.wait()

    space = pltpu.VMEM if in_vmem else pl.ANY
    return pl.pallas_call(
        body, name=name,
        out_shape=[jax.ShapeDtypeStruct((N_DEV,) + s.shape, s.dtype) for s in shards],
        in_specs=[pl.BlockSpec(memory_space=space)] * nt,
        out_specs=[pl.BlockSpec(memory_space=space)] * nt,
        scratch_shapes=[pltpu.SemaphoreType.DMA((7 * nt,)), pltpu.SemaphoreType.DMA((7 * nt,)),
                        pltpu.SemaphoreType.DMA((nt,))],
        compiler_params=pltpu.CompilerParams(vmem_limit_bytes=VMEM_LIMIT),
    )(*shards)


def _all_gather_async(shards, name, collective_id):
    nt = len(shards)
    hbm = pltpu.MemorySpace.HBM
    x_refs = [jax.new_ref(s, memory_space=hbm) for s in shards]
    out_refs = [jax.empty_ref(jax.ShapeDtypeStruct((N_DEV,) + s.shape, s.dtype), memory_space=hbm) for s in shards]

    @pl.kernel(mesh=plsc.ScalarSubcoreMesh(axis_name="sequencer", num_cores=1), name=name,
               scratch_types=(pltpu.SemaphoreType.DMA((7 * nt,)), pltpu.SemaphoreType.DMA((7 * nt,)),
                              pltpu.SemaphoreType.DMA((nt,))),
               compiler_params=pltpu.CompilerParams(collective_id=collective_id))
    def launch(send_sems, recv_sems, local_sems):
        x, y, c = _my_place()
        me, sibling = (x, y, c), (x, y, 1 - c)
        chips = [(1 - x, y), (x, 1 - y), (1 - x, 1 - y)]
        barrier = pltpu.get_barrier_semaphore()
        for peer in [sibling] + [(*chip, c) for chip in chips]:
            pl.semaphore_signal(barrier, inc=1, device_id=peer, device_id_type=MESH)
        pl.semaphore_wait(barrier, 4)
        everything = []
        for t in range(nt):
            x_ref, out_ref = x_refs[t], out_refs[t]

            def blk(px, py, pc, out_ref=out_ref):
                return out_ref.at[4 * px + 2 * py + pc]

            def copy(k, block, to, src=None, t=t, blk=blk):
                return pltpu.make_async_remote_copy(
                    src_ref=blk(*block) if src is None else src, dst_ref=blk(*block),
                    send_sem=send_sems.at[7 * t + k], recv_sem=recv_sems.at[7 * t + k], device_id=to, device_id_type=MESH)

            mine = pltpu.make_async_copy(x_ref, blk(*me), local_sems.at[t])
            mine.start()
            first = [copy(0, me, sibling, src=x_ref)]
            first += [copy(1 + j, me, (*chip, c), src=x_ref) for j, chip in enumerate(chips)]
            for cp in first:
                cp.start()
            everything.append((copy, mine, first))
        sends = []
        for copy, mine, first in everything:
            passed = [copy(4 + j, (*chip, c), sibling) for j, chip in enumerate(chips)]
            for j, chip in enumerate(chips):
                copy(1 + j, (*chip, c), me).wait_recv()
                passed[j].start()
            sends += first + passed
        for copy, mine, first in everything:
            copy(0, sibling, me).wait_recv()
            for j, chip in enumerate(chips):
                copy(4 + j, (*chip, 1 - c), me).wait_recv()
        for cp in sends:
            cp.wait_send()
        for copy, mine, first in everything:
            mine.wait()

    launch()
    return [r[...] for r in out_refs]


def _rs_sibling(gs):
    nt = len(gs)

    def body(*refs):
        g_refs, recv_refs = refs[:nt], refs[nt:2 * nt]
        send_sems, recv_sems = refs[2 * nt:]
        x, y, c = _my_place()
        copies = [pltpu.make_async_remote_copy(
            src_ref=g_refs[t].at[2 * j + (1 - c)], dst_ref=recv_refs[t].at[j],
            send_sem=send_sems.at[4 * t + j], recv_sem=recv_sems.at[4 * t + j],
            device_id=(x, y, 1 - c), device_id_type=MESH) for t in range(nt) for j in range(4)]
        for cp in copies:
            cp.start()
        for cp in copies:
            cp.wait()

    return pl.pallas_call(
        body, name="rs_sibling",
        out_shape=[jax.ShapeDtypeStruct((4,) + g.shape[1:], g.dtype) for g in gs],
        in_specs=[pl.BlockSpec(memory_space=pl.ANY)] * nt, out_specs=[pl.BlockSpec(memory_space=pl.ANY)] * nt,
        scratch_shapes=[pltpu.SemaphoreType.DMA((4 * nt,)), pltpu.SemaphoreType.DMA((4 * nt,))],
    )(*gs)


def _rs_chips(pbs):
    nt = len(pbs)

    def body(*refs):
        p_refs, recv_refs = refs[:nt], refs[nt:2 * nt]
        send_sems, recv_sems = refs[2 * nt:]
        x, y, c = _my_place()
        chips = [(1 - x, y), (x, 1 - y), (1 - x, 1 - y)]
        copies = [pltpu.make_async_remote_copy(
            src_ref=p_refs[t].at[2 * px + py], dst_ref=recv_refs[t].at[s],
            send_sem=send_sems.at[3 * t + s], recv_sem=recv_sems.at[3 * t + s],
            device_id=(px, py, c), device_id_type=MESH) for t in range(nt) for s, (px, py) in enumerate(chips)]
        for cp in copies:
            cp.start()
        for cp in copies:
            cp.wait()

    return pl.pallas_call(
        body, name="rs_chips",
        out_shape=[jax.ShapeDtypeStruct((3,) + p.shape[1:], p.dtype) for p in pbs],
        in_specs=[pl.BlockSpec(memory_space=pl.ANY)] * nt, out_specs=[pl.BlockSpec(memory_space=pl.ANY)] * nt,
        scratch_shapes=[pltpu.SemaphoreType.DMA((3 * nt,)), pltpu.SemaphoreType.DMA((3 * nt,))],
    )(*pbs)


def _rs_add_pairs(g, recv, my_c, name):
    _, nl, r, n = g.shape
    tr = _tile(r, 512)

    def body(c_ref, g_ref, r_ref, pf_ref, pb_ref):
        s = g_ref[...] + r_ref[...]
        pf_ref[...] = s
        pb_ref[...] = s.astype(MXU)

    spec_j = pl.BlockSpec((None, None, tr, n), lambda j, l, i, c_ref: (j, l, i, 0))
    return pl.pallas_call(
        body, name=name,
        grid_spec=pltpu.PrefetchScalarGridSpec(
            num_scalar_prefetch=1, grid=(4, nl, r // tr),
            in_specs=[pl.BlockSpec((None, None, tr, n), lambda j, l, i, c_ref: (2 * j + c_ref[0], l, i, 0)), spec_j],
            out_specs=[spec_j, spec_j]),
        out_shape=[jax.ShapeDtypeStruct((4, nl, r, n), F32), jax.ShapeDtypeStruct((4, nl, r, n), MXU)],
        compiler_params=_params(3),
    )(my_c, g, recv)


def _rs_add_chips(pf, recv, my_chip, name):
    _, nl, r, n = pf.shape
    tr = _tile(r, 512)

    def body(j_ref, p_ref, r_ref, o_ref):
        s = p_ref[...]
        for t in range(3):
            s = s + r_ref[t].astype(F32)
        o_ref[...] = s

    return pl.pallas_call(
        body, name=name,
        grid_spec=pltpu.PrefetchScalarGridSpec(
            num_scalar_prefetch=1, grid=(nl, r // tr),
            in_specs=[pl.BlockSpec((None, None, tr, n), lambda l, i, j_ref: (j_ref[0], l, i, 0)),
                      pl.BlockSpec((3, None, tr, n), lambda l, i, j_ref: (0, l, i, 0))],
            out_specs=pl.BlockSpec((None, tr, n), lambda l, i, j_ref: (l, i, 0))),
        out_shape=jax.ShapeDtypeStruct((nl, r, n), F32),
        compiler_params=_params(2),
    )(my_chip, pf, recv)


def _shard_windows(n_shard, count, first=0):
    out = []
    for k in range(first, first + count):
        off = n_shard * k
        a, s = off // LANES, off % LANES
        out.append((a, s, -(-(s + n_shard) // LANES) * LANES))
    return out


def _fit_lanes(x, width):
    have = x.shape[1]
    if have < width:
        return jnp.concatenate([x, jnp.zeros((x.shape[0], width - have), x.dtype)], axis=-1)
    return x[:, :width]


def _interleave_cols(g, n_shard, w_out, name):
    nd, nl, rows, wpad = g.shape
    rb = _tile(rows, 256)
    wins = _shard_windows(n_shard, nd)

    def body(g_ref, o_ref, acc):
        acc[...] = jnp.zeros_like(acc)
        for k, (a, s, win) in enumerate(wins):
            xk = _fit_lanes(g_ref[k].astype(F32), win)
            if s:
                xk = pltpu.roll(xk, s, 1)
            acc[:, a * LANES:a * LANES + win] += xk
        o_ref[...] = acc[...].astype(o_ref.dtype)

    return pl.pallas_call(
        body, name=name, grid=(nl, rows // rb),
        in_specs=[pl.BlockSpec((nd, None, rb, wpad), lambda l, i: (0, l, i, 0))],
        out_specs=pl.BlockSpec((None, rb, w_out), lambda l, i: (l, i, 0)),
        out_shape=jax.ShapeDtypeStruct((nl, rows, w_out), g.dtype),
        scratch_shapes=[pltpu.VMEM((rb, w_out), F32)],
        compiler_params=_params(2),
    )(g)


def _sum_devices(g):
    _, r, n = g.shape

    def body(g_ref, o_ref):
        s = g_ref[0]
        for t in range(1, N_DEV):
            s = s + g_ref[t]
        o_ref[...] = s

    return pl.pallas_call(
        body, name="sum_devices", out_shape=jax.ShapeDtypeStruct((r, n), F32),
        in_specs=[pl.BlockSpec(memory_space=pltpu.VMEM)], out_specs=pl.BlockSpec(memory_space=pltpu.VMEM),
        compiler_params=pltpu.CompilerParams(vmem_limit_bytes=VMEM_LIMIT),
    )(g)


def _mod_fwd(c_all, ada_w, ada_b_cols):
    nl, _, nc = ada_w.shape

    def body(c_ref, w_ref, b_ref, o_ref):
        cv = c_ref[...]
        act = (cv * _sig(cv)).astype(MXU)
        o_ref[...] = _dot(act, w_ref[...].astype(MXU), NN) + b_ref[...]

    return pl.pallas_call(
        body, name="mod_fwd", grid=(nl,),
        in_specs=[_full((16, D)), pl.BlockSpec((None, D, nc), lambda i: (i, 0, 0)),
                  pl.BlockSpec((None, 1, nc), lambda i: (i, 0, 0))],
        out_specs=pl.BlockSpec((None, 16, nc), lambda i: (i, 0, 0)),
        out_shape=jax.ShapeDtypeStruct((nl, 16, nc), F32), compiler_params=_params(1),
    )(c_all, ada_w, ada_b_cols)


def _mod_bwd(c_all, dmod_cols):
    nl, _, nc = dmod_cols.shape

    def body(c_ref, d_ref, o_ref):
        cv = c_ref[...]
        act = (cv * _sig(cv)).astype(MXU)
        o_ref[...] = _dot(act, d_ref[...].astype(MXU), TN)

    return pl.pallas_call(
        body, name="mod_bwd", grid=(nl,),
        in_specs=[_full((16, D)), pl.BlockSpec((None, 16, nc), lambda i: (i, 0, 0))],
        out_specs=pl.BlockSpec((None, D, nc), lambda i: (i, 0, 0)),
        out_shape=jax.ShapeDtypeStruct((nl, D, nc), F32), compiler_params=_params(1),
    )(c_all, dmod_cols)


def _in_proj(x, modrows, vec, w_in):
    L = x.shape[0]
    T = _tile(L, 256)

    def body(x_ref, mod_ref, vec_ref, w_ref, p_ref, h_ref):
        n, _ = _rms(x_ref[...])
        h = n * vec_ref[0:1, :] * (1.0 + mod_ref[1:2, :]) + mod_ref[0:1, :]
        hb = h.astype(MXU)
        h_ref[...] = hb
        p_ref[...] = _dot(hb, w_ref[...], NN)

    return pl.pallas_call(
        body, name="in_proj", grid=(L // T,),
        in_specs=[pl.BlockSpec((T, D), lambda i: (i, 0)), _full((SUB, D)), _full((SUB, D)), _full((D, P_PAD))],
        out_specs=[pl.BlockSpec((T, P_PAD), lambda i: (i, 0)), pl.BlockSpec((T, D), lambda i: (i, 0))],
        out_shape=[jax.ShapeDtypeStruct((L, P_PAD), F32), jax.ShapeDtypeStruct((L, D), MXU)],
        compiler_params=_params(1),
    )(x, modrows, vec, w_in)


def _gate_small(s, sp_ref):
    lane = lax.broadcasted_iota(jnp.int32, s.shape, 1)
    a = -jnp.exp(sp_ref[0:1, :])
    xb = s + sp_ref[1:2, :]
    beta = _sig(s)
    g = a * _softplus(xb)
    return lane, a, xb, beta, g


def _pre_fwd(p, pa, cq, sp):
    L = p.shape[0]
    T = _tile(L, 256)
    scale = HD ** -0.5

    def body(pm_ref, ps_ref, pa_ref, cq_ref, sp_ref, qn_ref, kn_ref, vs_ref, gb_ref, ya_ref, u_carry, q_carry):
        @pl.when(pl.program_id(0) == 0)
        def _():
            u_carry[...] = jnp.zeros_like(u_carry)
            q_carry[...] = jnp.zeros_like(q_carry)

        a_b = pm_ref[:, 0:AW]
        u = pm_ref[:, AW:2 * AW] * pm_ref[:, 2 * AW:3 * AW]
        cu, _ = _conv_fwd(u, pa_ref, 3, u_carry[...])
        u_carry[...] = u[T - SUB:T, :]
        yp = a_b * cu
        ms = _dot_f32(yp * yp, _blockdiag_mean(AW, A_GROUP), NN, exact="b")
        ya_ref[...] = (yp * lax.rsqrt(ms + EPS) * pa_ref[3:4, :]).astype(MXU)

        qkv = pm_ref[:, 3 * AW:3 * AW + 3 * H * HD]
        qc, _ = _conv_fwd(qkv, cq_ref, 4, q_carry[...])
        q_carry[...] = qkv[T - SUB:T, :]
        qs = qc * _sig(qc)
        for h in range(H):
            q = qs[:, h * HD:(h + 1) * HD]
            qn_ref[:, h * HD:(h + 1) * HD] = q * (lax.rsqrt(jnp.sum(q * q, axis=-1, keepdims=True) + EPS) * scale)
            k = qs[:, (H + h) * HD:(H + h + 1) * HD]
            kn_ref[:, h * HD:(h + 1) * HD] = k * lax.rsqrt(jnp.sum(k * k, axis=-1, keepdims=True) + EPS)
        vs_ref[...] = qs[:, 2 * H * HD:3 * H * HD]

        lane, _, _, beta, g = _gate_small(ps_ref[...], sp_ref)
        gb_ref[...] = jnp.where(lane < H, beta, jnp.where(lane < 2 * H, g, 0.0))

    w3 = 3 * AW + 3 * H * HD
    row = lambda i: (i, 0)
    return pl.pallas_call(
        body, name="pre_fwd", grid=(L // T,),
        in_specs=[pl.BlockSpec((T, w3), row), pl.BlockSpec((T, LANES), lambda i: (i, (P_PAD - LANES) // LANES)),
                  _full((SUB, AW)), _full((SUB, 3 * H * HD)), _full((SUB, LANES))],
        out_specs=[pl.BlockSpec((T, H * HD), row)] * 3 + [pl.BlockSpec((T, LANES), row), pl.BlockSpec((T, AW), row)],
        out_shape=[jax.ShapeDtypeStruct((L, H * HD), F32)] * 3
        + [jax.ShapeDtypeStruct((L, LANES), F32), jax.ShapeDtypeStruct((L, AW), MXU)],
        scratch_shapes=[pltpu.VMEM((SUB, AW), F32), pltpu.VMEM((SUB, 3 * H * HD), F32)],
        compiler_params=_params(1),
    )(p, p, pa, cq, sp)


def _gdr_masks():
    r = lax.broadcasted_iota(jnp.int32, (CK, CK), 0)
    c = lax.broadcasted_iota(jnp.int32, (CK, CK), 1)
    return r >= c, r > c


def _head_cols(gbt, h):
    return gbt[:, h:h + 1], gbt[:, H + h:H + h + 1]


def _split(x, parts):
    out = []
    for _ in range(parts):
        hi = x.astype(jnp.bfloat16)
        out.append(hi)
        x = x - hi.astype(F32)
    return out


def _dot_f32(a, b, dims, exact=None):
    if exact == "a":
        ab = a.astype(jnp.bfloat16)
        return sum(_dot(ab, t, dims) for t in _split(b, 3))
    if exact == "b":
        bb = b.astype(jnp.bfloat16)
        return sum(_dot(t, bb, dims) for t in _split(a, 3))
    ah, al = _split(a, 2)
    bh, bl = _split(b, 2)
    return _dot(ah, bh, dims) + _dot(ah, bl, dims) + _dot(al, bh, dims)


def _gdr_consts():
    causal, strict = _gdr_masks()
    return dict(causal=causal, strict=strict, tril=jnp.where(causal, 1.0, 0.0).astype(F32),
                eye=jnp.where(causal & jnp.logical_not(strict), 1.0, 0.0).astype(F32),
                bcast=jnp.full((CK, HD), 1.0 / HD, F32))


def _dots(a, b, dims):
    return [_dot(x, y, dims) for x, y in zip(a, b)]


def _dots_f32(a, b, dims, exact=None):
    n = len(a)
    if exact == "a":
        lhs = [[x.astype(jnp.bfloat16)] * 3 for x in a]
        rhs = [_split(y, 3) for y in b]
    elif exact == "b":
        lhs = [_split(x, 3) for x in a]
        rhs = [[y.astype(jnp.bfloat16)] * 3 for y in b]
    else:
        sa = [_split(x, 2) for x in a]
        sb = [_split(y, 2) for y in b]
        lhs = [[s[0], s[0], s[1]] for s in sa]
        rhs = [[s[0], s[1], s[0]] for s in sb]
    terms = [[_dot(lhs[i][t], rhs[i][t], dims) for i in range(n)] for t in range(3)]
    return [terms[0][i] + terms[1][i] + terms[2][i] for i in range(n)]


def _gdr_local(q, k, v, beta, g, cst, tinv=None):
    n = len(q)
    R = range(n)
    causal, strict = cst["causal"], cst["strict"]
    gc = _dots_f32([cst["tril"]] * n, [jnp.broadcast_to(g[i], (CK, HD)) for i in R], NN, exact="a")
    g_row = _dots_f32([cst["bcast"]] * n, gc, NT, exact="a")
    decay = [jnp.where(causal, jnp.exp(jnp.where(causal, gc[i][:, 0:CK] - g_row[i], 0.0)), 0.0) for i in R]
    eg = [jnp.exp(gc[i]) for i in R]
    gl = [gc[i][CK - 1:CK, :] for i in R]
    ek = [jnp.exp(gl[i] - gc[i]) for i in R]
    cd = [jnp.exp(gl[i]) for i in R]
    kb = [k[i] * beta[i] for i in R]
    pk = _dots(kb, k, NT)
    if tinv is None:
        xp = [-jnp.where(strict, pk[i] * decay[i], 0.0) for i in R]
        tinv = [cst["eye"] + xp[i] for i in R]
        for _ in range(5):
            xp = _dots_f32(xp, xp, NN)
            tx = _dots_f32(tinv, xp, NN)
            tinv = [tinv[i] + tx[i] for i in R]
    u = _dots(tinv, [v[i] * beta[i] for i in R], NN)
    w = _dots(tinv, [kb[i] * eg[i] for i in R], NN)
    qk = _dots(q, k, NT)
    intra = [jnp.where(causal, qk[i] * decay[i], 0.0) for i in R]
    return dict(decay=decay, eg=eg, ek=ek, cd=cd, kb=kb, pk=pk, tinv=tinv, u=u, w=w, qk=qk, intra=intra,
                q_dec=[q[i] * eg[i] for i in R], k_dec=[k[i] * ek[i] for i in R])


GDR_SUB = 4


def _gdr_fwd(qn, kn, vs, gb):
    L = qn.shape[0]
    nc = L // CK
    cb = min(8, nc)
    rb = cb * CK
    nb = nc // cb
    nsub = GDR_SUB if cb % GDR_SUB == 0 else 1

    def body(q_ref, k_ref, v_ref, gb_ref, o_ref, st_ref, ti_ref, s_ref):
        @pl.when(pl.program_id(0) == 0)
        def _():
            s_ref[...] = jnp.zeros_like(s_ref)

        cst = _gdr_consts()
        heads = range(H)

        def group(gi, carry):
            rows = [pl.ds(pl.multiple_of((gi * nsub + j) * CK, CK), CK) for j in range(nsub)]
            chains = [(j, h) for j in range(nsub) for h in heads]
            gbt = [gb_ref[rows[j], :] for j in range(nsub)]
            cols = lambda h: slice(h * HD, (h + 1) * HD)
            t = _gdr_local([q_ref[rows[j], cols(h)] for j, h in chains], [k_ref[rows[j], cols(h)] for j, h in chains],
                           [v_ref[rows[j], cols(h)] for j, h in chains],
                           [_head_cols(gbt[j], h)[0] for j, h in chains], [_head_cols(gbt[j], h)[1] for j, h in chains], cst)
            s = [s_ref[h] for h in heads]
            for j in range(nsub):
                at = lambda key: [t[key][j * H + h] for h in heads]
                for h in heads:
                    st_ref[h, gi * nsub + j] = s[h]
                    ti_ref[h, gi * nsub + j] = t["tinv"][j * H + h]
                ws = _dots(at("w"), s, NN)
                v_new = [u_h - ws_h for u_h, ws_h in zip(at("u"), ws)]
                o_s = _dots(at("q_dec"), s, NN)
                o_v = _dots(at("intra"), v_new, NN)
                kv = _dots(at("k_dec"), v_new, TN)
                cd = at("cd")
                for h in heads:
                    o_ref[rows[j], cols(h)] = o_s[h] + o_v[h]
                s = [s[h] * cd[h] + kv[h] for h in heads]
            for h in heads:
                s_ref[h] = s[h]
            return carry

        lax.fori_loop(0, cb // nsub, group, 0)

    blk = pl.BlockSpec((rb, H * HD), lambda b: (b, 0))
    return pl.pallas_call(
        body, name="gdr_fwd", grid=(nb,),
        in_specs=[blk, blk, blk, pl.BlockSpec((rb, LANES), lambda b: (b, 0))],
        out_specs=[blk, pl.BlockSpec((H, cb, HD, HD), lambda b: (0, b, 0, 0)),
                   pl.BlockSpec((H, cb, CK, CK), lambda b: (0, b, 0, 0))],
        out_shape=[jax.ShapeDtypeStruct((L, H * HD), F32), jax.ShapeDtypeStruct((H, nc, HD, HD), F32),
                   jax.ShapeDtypeStruct((H, nc, CK, CK), F32)],
        scratch_shapes=[pltpu.VMEM((H, HD, HD), F32)],
        compiler_params=_params(1),
    )(qn, kn, vs, gb)


def _gdr_bwd(qn, kn, vs, gb, states, tinvs, do):
    L = qn.shape[0]
    nc = L // CK
    cb = min(8, nc)
    rb = cb * CK
    nb = nc // cb
    nsub = GDR_SUB if cb % GDR_SUB == 0 else 1

    def body(q_ref, k_ref, v_ref, gb_ref, st_ref, ti_ref, do_ref, dq_ref, dk_ref, dv_ref, dgb_ref, ds_ref):
        @pl.when(pl.program_id(0) == 0)
        def _():
            ds_ref[...] = jnp.zeros_like(ds_ref)

        cst = _gdr_consts()
        causal, strict = cst["causal"], cst["strict"]
        ones = jnp.ones((CK, HD), F32)
        row = lax.broadcasted_iota(jnp.int32, (CK, HD), 0)
        lane = lax.broadcasted_iota(jnp.int32, (CK, LANES), 1)

        heads = range(H)
        rsum = lambda x: jnp.sum(x, axis=-1, keepdims=True)

        def group(gj, carry):
            gi = cb // nsub - 1 - gj
            rows = [pl.ds(pl.multiple_of((gi * nsub + j) * CK, CK), CK) for j in range(nsub)]
            chains = [(j, h) for j in range(nsub) for h in heads]
            gbt = [gb_ref[rows[j], :] for j in range(nsub)]
            cols = lambda h: slice(h * HD, (h + 1) * HD)
            q_all = [q_ref[rows[j], cols(h)] for j, h in chains]
            k_all = [k_ref[rows[j], cols(h)] for j, h in chains]
            v_all = [v_ref[rows[j], cols(h)] for j, h in chains]
            beta_all = [_head_cols(gbt[j], h)[0] for j, h in chains]
            t = _gdr_local(q_all, k_all, v_all, beta_all, [_head_cols(gbt[j], h)[1] for j, h in chains], cst,
                           tinv=[ti_ref[h, gi * nsub + j] for j, h in chains])
            ds_out = [ds_ref[h] for h in heads]
            for j in reversed(range(nsub)):
                at = lambda key: [t[key][j * H + h] for h in heads]
                pick = lambda lst: [lst[j * H + h] for h in heads]
                q, k, v, beta = pick(q_all), pick(k_all), pick(v_all), pick(beta_all)
                u, w, tinv, decay = at("u"), at("w"), at("tinv"), at("decay")
                eg, ek, cd, kb = at("eg"), at("ek"), at("cd"), at("kb")
                q_dec, k_dec, intra, pk, qk = at("q_dec"), at("k_dec"), at("intra"), at("pk"), at("qk")
                s = [st_ref[h, gi * nsub + j] for h in heads]
                dout = [do_ref[rows[j], cols(h)] for h in heads]

                ws = _dots(w, s, NN)
                v_new = [u[h] - ws[h] for h in heads]
                dq_dec = _dots(dout, s, NT)
                qd = _dots(q_dec, dout, TN)
                di = _dots(dout, v_new, NT)
                dintra = [jnp.where(causal, di[h], 0.0) for h in heads]
                ido = _dots(intra, dout, TN)
                kds = _dots(k_dec, ds_out, NN)
                dv_new = [ido[h] + kds[h] for h in heads]
                dk_dec = _dots(v_new, ds_out, NT)
                dcd = [jnp.sum(jnp.sum(ds_out[h] * s[h], axis=1, keepdims=True), axis=0, keepdims=True) for h in heads]
                dvs = _dots(dv_new, s, NT)
                dw = [-dvs[h] for h in heads]
                wdv = _dots(w, dv_new, TN)
                ds_new = [qd[h] + ds_out[h] * cd[h] - wdv[h] for h in heads]
                dru = _dots(tinv, dv_new, TN)
                drw = _dots(tinv, dw, TN)
                dl1 = _dots(dru, u, NT)
                dl2 = _dots(drw, w, NT)
                dlower = [-jnp.where(strict, dl1[h] + dl2[h], 0.0) for h in heads]
                dv = [dru[h] * beta[h] for h in heads]
                dbeta = [rsum(dru[h] * v[h]) for h in heads]
                dgc = [rsum(drw[h] * kb[h]) * eg[h] for h in heads]
                dpk = [dlower[h] * decay[h] for h in heads]
                dqk = [dintra[h] * decay[h] for h in heads]
                dpk_k = _dots(dpk, k, NN)
                dkb = [drw[h] * eg[h] + dpk_k[h] for h in heads]
                dk1 = _dots(dpk, kb, TN)
                dq1 = _dots(dqk, k, NN)
                dk2 = _dots(dqk, q, TN)
                m = [(dlower[h] * pk[h] + dintra[h] * qk[h]) * decay[h] for h in heads]
                mcol = _dots_f32(m, [ones] * H, TN, exact="b")
                e = [rsum(dk_dec[h] * k_dec[h]) for h in heads]
                dgl = [jnp.sum(e[h], axis=0, keepdims=True) + dcd[h] * cd[h] for h in heads]
                dgc = [dgc[h] + rsum(m[h]) - mcol[h] + rsum(dq_dec[h] * q_dec[h]) - e[h]
                       + jnp.where(row == CK - 1, dgl[h], 0.0) for h in heads]
                dg = _dots_f32([cst["tril"]] * H, dgc, TN, exact="a")
                dgb = jnp.zeros((CK, LANES), F32)
                for h in heads:
                    dq_ref[rows[j], cols(h)] = dq1[h] + dq_dec[h] * eg[h]
                    dk_ref[rows[j], cols(h)] = dk1[h] + dk2[h] + dk_dec[h] * ek[h] + dkb[h] * beta[h]
                    dv_ref[rows[j], cols(h)] = dv[h]
                    db = dbeta[h] + rsum(dkb[h] * k[h])
                    dgb = dgb + jnp.where(lane == h, db, 0.0) + jnp.where(lane == H + h, dg[h], 0.0)
                dgb_ref[rows[j], :] = dgb
                ds_out = ds_new
            for h in heads:
                ds_ref[h] = ds_out[h]
            return carry

        lax.fori_loop(0, cb // nsub, group, 0)

    blk = pl.BlockSpec((rb, H * HD), lambda b: (nb - 1 - b, 0))
    sblk = pl.BlockSpec((rb, LANES), lambda b: (nb - 1 - b, 0))
    return pl.pallas_call(
        body, name="gdr_bwd", grid=(nb,),
        in_specs=[blk, blk, blk, sblk, pl.BlockSpec((H, cb, HD, HD), lambda b: (0, nb - 1 - b, 0, 0)),
                  pl.BlockSpec((H, cb, CK, CK), lambda b: (0, nb - 1 - b, 0, 0)), blk],
        out_specs=[blk, blk, blk, sblk],
        out_shape=[jax.ShapeDtypeStruct((L, H * HD), F32)] * 3 + [jax.ShapeDtypeStruct((L, LANES), F32)],
        scratch_shapes=[pltpu.VMEM((H, HD, HD), F32)],
        compiler_params=_params(1),
    )(qn, kn, vs, gb, states, tinvs, do)


def _post_fwd(o, p, ya, x, modrows, sp, w_out):
    L = x.shape[0]
    T = _tile(L, 256)

    def body(o_ref, z_ref, ya_ref, x_ref, mod_ref, sp_ref, w_ref, y_ref, x2_ref, yb_ref):
        ndw = sp_ref[2:3, :]
        z = z_ref[...]
        sz = z * _sig(z)
        parts = []
        for h in range(H):
            n, _ = _rms(o_ref[:, h * HD:(h + 1) * HD])
            parts.append(n * ndw * sz[:, h * HD:(h + 1) * HD])
        yb = jnp.concatenate(parts, axis=-1).astype(MXU)
        yb_ref[...] = yb
        y = _dot(ya_ref[...], w_ref[0:AW, :], NN) + _dot(yb, w_ref[AW:2 * AW, :], NN)
        y_ref[...] = y
        x2_ref[...] = x_ref[...] + mod_ref[2:3, :] * y

    row = lambda i: (i, 0)
    zcol = (3 * AW + 3 * H * HD) // (H * HD)
    return pl.pallas_call(
        body, name="post_fwd", grid=(L // T,),
        in_specs=[pl.BlockSpec((T, H * HD), row), pl.BlockSpec((T, H * HD), lambda i: (i, zcol)),
                  pl.BlockSpec((T, AW), row), pl.BlockSpec((T, D), row), _full((SUB, D)), _full((SUB, LANES)),
                  _full((D, D))],
        out_specs=[pl.BlockSpec((T, D), row), pl.BlockSpec((T, D), row), pl.BlockSpec((T, H * HD), row)],
        out_shape=[jax.ShapeDtypeStruct((L, D), F32), jax.ShapeDtypeStruct((L, D), F32),
                   jax.ShapeDtypeStruct((L, H * HD), MXU)],
        compiler_params=_params(1),
    )(o, p, ya, x, modrows, sp, w_out)


FF_COLS = 2
FF_CW = DFF // FF_COLS
FF_ROWS = 256


def _ffn_fwd_half(x2, modrows, vec, w_up, cff, w_down, j, d_prev):
    assert FF_COLS == 2
    L = x2.shape[0]
    T = _tile(L, FF_ROWS)
    nj = FF_COLS
    last = d_prev is not None

    def body(*refs):
        x_ref, mod_ref, vec_ref, wg_ref, wu_ref, cg_ref, cu_ref, wd_ref = refs[:8]
        if last:
            dp_ref, gp_ref, up_ref, f_ref, d_ref, x3_ref, carry_g, carry_u = refs[8:]
        else:
            h_ref, gp_ref, up_ref, f_ref, d_ref, carry_g, carry_u = refs[8:]

        @pl.when(pl.program_id(0) == 0)
        def _():
            carry_g[...] = jnp.zeros_like(carry_g)
            carry_u[...] = jnp.zeros_like(carry_u)

        xv = x_ref[...]
        n, _ = _rms(xv)
        hb = (n * vec_ref[1:2, :] * (1.0 + mod_ref[4:5, :]) + mod_ref[3:4, :]).astype(MXU)
        if not last:
            h_ref[...] = hb
        g = _dot(hb, wg_ref[...], NN)
        u = _dot(hb, wu_ref[...], NN)
        gp_ref[...] = g
        up_ref[...] = u
        gc, _ = _conv_fwd(g, cg_ref, 3, carry_g[...])
        uc, _ = _conv_fwd(u, cu_ref, 3, carry_u[...])
        carry_g[...] = g[T - SUB:T, :]
        carry_u[...] = u[T - SUB:T, :]
        fb = (gc * _sig(gc) * uc).astype(MXU)
        f_ref[...] = fb
        part = _dot(fb, wd_ref[...], NN)
        if last:
            dv = dp_ref[...] + part
            d_ref[...] = dv
            x3_ref[...] = xv + mod_ref[5:6, :] * dv
        else:
            d_ref[...] = part

    row = lambda i: (i, 0)
    rowD = pl.BlockSpec((T, D), row)
    rowC = pl.BlockSpec((T, FF_CW), row)
    in_specs = [rowD, _full((SUB, D)), _full((SUB, D)),
                pl.BlockSpec((D, FF_CW), lambda i: (0, j)), pl.BlockSpec((D, FF_CW), lambda i: (0, nj + j)),
                pl.BlockSpec((SUB, FF_CW), lambda i: (0, j)), pl.BlockSpec((SUB, FF_CW), lambda i: (0, nj + j)),
                pl.BlockSpec((FF_CW, D), lambda i: (j, 0))]
    half = [jax.ShapeDtypeStruct((L, FF_CW), F32), jax.ShapeDtypeStruct((L, FF_CW), F32),
            jax.ShapeDtypeStruct((L, FF_CW), MXU)]
    args = [x2, modrows, vec, w_up, w_up, cff, cff, w_down]
    if last:
        in_specs.append(rowD)
        args.append(d_prev)
        out_specs = [rowC, rowC, rowC, rowD, rowD]
        out_shape = half + [jax.ShapeDtypeStruct((L, D), F32), jax.ShapeDtypeStruct((L, D), F32)]
    else:
        out_specs = [rowD, rowC, rowC, rowC, rowD]
        out_shape = [jax.ShapeDtypeStruct((L, D), MXU)] + half + [jax.ShapeDtypeStruct((L, D), F32)]
    return pl.pallas_call(
        body, name="ffn_fwd_last" if last else "ffn_fwd_first", grid=(L // T,),
        in_specs=in_specs, out_specs=out_specs, out_shape=out_shape,
        scratch_shapes=[pltpu.VMEM((SUB, FF_CW), F32), pltpu.VMEM((SUB, FF_CW), F32)],
        compiler_params=_params(1),
    )(*args)


def _ffn_bwd_half(dx3, modrows, gpre, upre, cff, w_down, w_up, j, tail):
    assert FF_COLS == 2
    L = dx3.shape[0]
    T = _tile(L, FF_ROWS)
    ni, nj = L // T, FF_COLS
    hb_per_t = T // SUB
    last = tail is not None

    def body(*refs):
        dx3_ref, mod_ref, gp_ref, up_ref, gph_ref, uph_ref, cg_ref, cu_ref, wd_ref, wg_ref, wu_ref = refs[:11]
        if last:
            (d_ref, x2_ref, vec_ref, dhp_ref, dgp_ref, dup_ref, dx2_ref, accv_ref, dcg_ref, dcu_ref,
             carry_g, carry_u) = refs[11:]
        else:
            dd_ref, dgp_ref, dup_ref, dh_ref, dcg_ref, dcu_ref, carry_g, carry_u = refs[11:]
        i = pl.program_id(0)
        ri = ni - 1 - i

        @pl.when(i == 0)
        def _():
            carry_g[...] = jnp.zeros_like(carry_g)
            carry_u[...] = jnp.zeros_like(carry_u)
            dcg_ref[...] = jnp.zeros_like(dcg_ref)
            dcu_ref[...] = jnp.zeros_like(dcu_ref)
            if last:
                accv_ref[...] = jnp.zeros_like(accv_ref)

        dx3v = dx3_ref[...]
        ddb = (mod_ref[5:6, :] * dx3v).astype(MXU)
        if not last:
            dd_ref[...] = ddb
        g, u = gp_ref[...], up_ref[...]
        keep = jnp.where(ri == 0, 0.0, 1.0)
        gc, gsh = _conv_fwd(g, cg_ref, 3, gph_ref[...] * keep)
        uc, ush = _conv_fwd(u, cu_ref, 3, uph_ref[...] * keep)
        sg = _sig(gc)
        df = _dot(ddb, wd_ref[...], NT)
        duc = df * (gc * sg)
        dgc = df * uc * (sg * (1.0 + gc * (1.0 - sg)))
        for s in range(3):
            dcg_ref[2 - s:3 - s, :] += _sum0(dgc * gsh[s])
            dcu_ref[2 - s:3 - s, :] += _sum0(duc * ush[s])
        dg = _conv_bwd_in(dgc, cg_ref, 3, carry_g[...]).astype(MXU)
        du = _conv_bwd_in(duc, cu_ref, 3, carry_u[...]).astype(MXU)
        carry_g[...] = dgc[0:SUB, :]
        carry_u[...] = duc[0:SUB, :]
        dgp_ref[...] = dg
        dup_ref[...] = du
        dh = _dot(dg, wg_ref[...], NT) + _dot(du, wu_ref[...], NT)
        if last:
            dh = dh + dhp_ref[...]
            accv_ref[0:1, :] += _sum0(dx3v * d_ref[...])
            n, r = _rms(x2_ref[...])
            nw, sc = vec_ref[1:2, :], mod_ref[4:5, :]
            accv_ref[1:2, :] += _sum0(dh)
            accv_ref[2:3, :] += _sum0(dh * n * nw)
            accv_ref[3:4, :] += _sum0(dh * n * (1.0 + sc))
            dx2_ref[...] = _rms_bwd(dh * nw * (1.0 + sc), n, r) + dx3v
        else:
            dh_ref[...] = dh

    row = lambda i: (ni - 1 - i, 0)
    halo = lambda i: (jnp.maximum((ni - 1 - i) * hb_per_t - 1, 0), 0)
    rowD = pl.BlockSpec((T, D), row)
    rowC = pl.BlockSpec((T, FF_CW), row)
    haloC = pl.BlockSpec((SUB, FF_CW), halo)
    in_specs = [rowD, _full((SUB, D)), rowC, rowC, haloC, haloC,
                pl.BlockSpec((SUB, FF_CW), lambda i: (0, j)), pl.BlockSpec((SUB, FF_CW), lambda i: (0, nj + j)),
                pl.BlockSpec((FF_CW, D), lambda i: (j, 0)),
                pl.BlockSpec((D, FF_CW), lambda i: (0, j)), pl.BlockSpec((D, FF_CW), lambda i: (0, nj + j))]
    args = [dx3, modrows, gpre, upre, gpre, upre, cff, cff, w_down, w_up, w_up]
    halfb = [jax.ShapeDtypeStruct((L, FF_CW), MXU), jax.ShapeDtypeStruct((L, FF_CW), MXU)]
    dconv = [jax.ShapeDtypeStruct((SUB, FF_CW), F32)] * 2
    if last:
        d, x2, vec, dh_prev = tail
        in_specs += [rowD, rowD, _full((SUB, D)), rowD]
        args += [d, x2, vec, dh_prev]
        out_specs = [rowC, rowC, rowD, _full((SUB, D)), _full((SUB, FF_CW)), _full((SUB, FF_CW))]
        out_shape = halfb + [jax.ShapeDtypeStruct((L, D), F32), jax.ShapeDtypeStruct((SUB, D), F32)] + dconv
    else:
        out_specs = [rowD, rowC, rowC, rowD, _full((SUB, FF_CW)), _full((SUB, FF_CW))]
        out_shape = [jax.ShapeDtypeStruct((L, D), MXU)] + halfb + [jax.ShapeDtypeStruct((L, D), F32)] + dconv
    return pl.pallas_call(
        body, name="ffn_bwd_last" if last else "ffn_bwd_first", grid=(ni,),
        in_specs=in_specs, out_specs=out_specs, out_shape=out_shape,
        scratch_shapes=[pltpu.VMEM((SUB, FF_CW), F32), pltpu.VMEM((SUB, FF_CW), F32)],
        compiler_params=_params(1),
    )(*args)


def _final(x, target, nf):
    L = x.shape[0]
    T = _tile(L, 256)

    def body(x_ref, t_ref, nf_ref, dx_ref, acc_ref):
        @pl.when(pl.program_id(0) == 0)
        def _():
            acc_ref[...] = jnp.zeros_like(acc_ref)

        n, r = _rms(x_ref[...])
        w = nf_ref[0:1, :]
        err = n * w - t_ref[...]
        acc_ref[0:1, :] += (0.5 / D) * _sum0(err * err)
        dy = err * (1.0 / D)
        acc_ref[1:2, :] += _sum0(dy * n)
        dx_ref[...] = _rms_bwd(dy * w, n, r)

    row = lambda i: (i, 0)
    return pl.pallas_call(
        body, name="final_norm_loss", grid=(L // T,),
        in_specs=[pl.BlockSpec((T, D), row), pl.BlockSpec((T, D), row), _full((SUB, D))],
        out_specs=[pl.BlockSpec((T, D), row), _full((SUB, D))],
        out_shape=[jax.ShapeDtypeStruct((L, D), F32), jax.ShapeDtypeStruct((SUB, D), F32)],
        compiler_params=_params(1),
    )(x, target, nf)


def _post_bwd(dx2, y, o, p, modrows, sp, w_out):
    L = dx2.shape[0]
    T = _tile(L, 256)

    def body(dx2_ref, y_ref, o_ref, z_ref, mod_ref, sp_ref, w_ref, dy_ref, do_ref, dz_ref, dya_ref, accv_ref, accs_ref):
        @pl.when(pl.program_id(0) == 0)
        def _():
            accv_ref[...] = jnp.zeros_like(accv_ref)
            accs_ref[...] = jnp.zeros_like(accs_ref)

        dx2v = dx2_ref[...]
        accv_ref[0:1, :] += _sum0(dx2v * y_ref[...])
        dyb = (mod_ref[2:3, :] * dx2v).astype(MXU)
        dy_ref[...] = dyb
        dyc = _dot(dyb, w_ref[...], NT)
        dya_ref[...] = dyc[:, 0:AW]
        ndw = sp_ref[2:3, :]
        z = z_ref[...]
        sgz = _sig(z)
        dsz = sgz * (1.0 + z * (1.0 - sgz))
        dndw = jnp.zeros((1, HD), F32)
        for h in range(H):
            sl = slice(h * HD, (h + 1) * HD)
            n, r = _rms(o_ref[:, sl])
            dyh = dyc[:, AW + h * HD:AW + (h + 1) * HD]
            zh = z[:, sl]
            don = dyh * (zh * sgz[:, sl])
            dz_ref[:, sl] = dyh * (n * ndw) * dsz[:, sl]
            dndw = dndw + _sum0(don * n)
            do_ref[:, sl] = _rms_bwd(don * ndw, n, r)
        accs_ref[0:1, :] += dndw

    row = lambda i: (i, 0)
    zcol = (3 * AW + 3 * H * HD) // (H * HD)
    return pl.pallas_call(
        body, name="post_bwd", grid=(L // T,),
        in_specs=[pl.BlockSpec((T, D), row), pl.BlockSpec((T, D), row), pl.BlockSpec((T, H * HD), row),
                  pl.BlockSpec((T, H * HD), lambda i: (i, zcol)), _full((SUB, D)), _full((SUB, LANES)), _full((D, D))],
        out_specs=[pl.BlockSpec((T, D), row)] + [pl.BlockSpec((T, H * HD), row)] * 3 + [_full((SUB, D)), _full((SUB, LANES))],
        out_shape=[jax.ShapeDtypeStruct((L, D), MXU)] + [jax.ShapeDtypeStruct((L, H * HD), F32)] * 3
        + [jax.ShapeDtypeStruct((SUB, D), F32), jax.ShapeDtypeStruct((SUB, LANES), F32)],
        compiler_params=_params(1),
    )(dx2, y, o, p, modrows, sp, w_out)


def _pre_bwd(p, dqn, dkn, dvs, dya, dz, dgb, pa, cq, sp):
    L = p.shape[0]
    T = _tile(L, 256)
    ni = L // T
    scale = HD ** -0.5
    w3 = 3 * AW + 3 * H * HD
    hb_per_t = T // SUB

    def body(pm_ref, ph_ref, ps_ref, dq_ref, dk_ref, dv_ref, dya_ref, dz_ref, dgb_ref, pa_ref, cq_ref, sp_ref,
             dp_ref, dpa_ref, dcq_ref, dsp_ref, carry_u, carry_q):
        i = pl.program_id(0)
        ri = ni - 1 - i

        @pl.when(i == 0)
        def _():
            dpa_ref[...] = jnp.zeros_like(dpa_ref)
            dcq_ref[...] = jnp.zeros_like(dcq_ref)
            dsp_ref[...] = jnp.zeros_like(dsp_ref)
            carry_u[...] = jnp.zeros_like(carry_u)
            carry_q[...] = jnp.zeros_like(carry_q)

        keep = jnp.where(ri == 0, 0.0, 1.0)
        a_b, a_c, a_x = pm_ref[:, 0:AW], pm_ref[:, AW:2 * AW], pm_ref[:, 2 * AW:3 * AW]
        u = a_c * a_x
        hu = ph_ref[:, AW:2 * AW] * ph_ref[:, 2 * AW:3 * AW] * keep
        cu, ush = _conv_fwd(u, pa_ref, 3, hu)
        yp = a_b * cu
        bd = _blockdiag_mean(AW, A_GROUP)
        ra = lax.rsqrt(_dot_f32(yp * yp, bd, NN, exact="b") + EPS)
        na = yp * ra
        dya = dya_ref[...]
        dpa_ref[3:4, :] += _sum0(dya * na)
        dna = dya * pa_ref[3:4, :]
        dyp = ra * (dna - na * _dot_f32(dna * na, bd, NN, exact="b"))
        dcu = dyp * a_b
        for s in range(3):
            dpa_ref[2 - s:3 - s, :] += _sum0(dcu * ush[s])
        du = _conv_bwd_in(dcu, pa_ref, 3, carry_u[...])
        carry_u[...] = dcu[0:SUB, :]
        dp_ref[:, 0:AW] = (dyp * cu).astype(MXU)
        dp_ref[:, AW:2 * AW] = (du * a_x).astype(MXU)
        dp_ref[:, 2 * AW:3 * AW] = (du * a_c).astype(MXU)

        qkv = pm_ref[:, 3 * AW:w3]
        qc, qsh = _conv_fwd(qkv, cq_ref, 4, ph_ref[:, 3 * AW:w3] * keep)
        sg = _sig(qc)
        qs = qc * sg
        parts = []
        for h in range(H):
            q = qs[:, h * HD:(h + 1) * HD]
            rq = lax.rsqrt(jnp.sum(q * q, axis=-1, keepdims=True) + EPS)
            parts.append(_l2_bwd(dq_ref[:, h * HD:(h + 1) * HD] * scale, q * rq, rq))
        for h in range(H):
            k = qs[:, (H + h) * HD:(H + h + 1) * HD]
            rk = lax.rsqrt(jnp.sum(k * k, axis=-1, keepdims=True) + EPS)
            parts.append(_l2_bwd(dk_ref[:, h * HD:(h + 1) * HD], k * rk, rk))
        parts.append(dv_ref[...])
        dqc = jnp.concatenate(parts, axis=-1) * (sg * (1.0 + qc * (1.0 - sg)))
        for s in range(4):
            dcq_ref[3 - s:4 - s, :] += _sum0(dqc * qsh[s])
        dp_ref[:, 3 * AW:w3] = _conv_bwd_in(dqc, cq_ref, 4, carry_q[...]).astype(MXU)
        carry_q[...] = dqc[0:SUB, :]
        dp_ref[:, w3:w3 + H * HD] = dz_ref[...].astype(MXU)

        lane, a, xb, beta, g = _gate_small(ps_ref[...], sp_ref)
        dgb = dgb_ref[...]
        dbeta = jnp.where(lane < H, dgb, 0.0)
        dg = jnp.where((lane >= H) & (lane < 2 * H), dgb, 0.0)
        dalpha = dg * a * _sig(xb)
        dsp_ref[0:1, :] += _sum0(dg * g)
        dsp_ref[1:2, :] += _sum0(dalpha)
        dp_ref[:, w3 + H * HD:P_PAD] = (dbeta * beta * (1.0 - beta) + dalpha).astype(MXU)

    row = lambda i: (ni - 1 - i, 0)
    halo = lambda i: (jnp.maximum((ni - 1 - i) * hb_per_t - 1, 0), 0)
    hrow = pl.BlockSpec((T, H * HD), row)
    return pl.pallas_call(
        body, name="pre_bwd", grid=(ni,),
        in_specs=[pl.BlockSpec((T, w3), row), pl.BlockSpec((SUB, w3), halo),
                  pl.BlockSpec((T, LANES), lambda i: (ni - 1 - i, (P_PAD - LANES) // LANES)),
                  hrow, hrow, hrow, pl.BlockSpec((T, AW), row), hrow,
                  pl.BlockSpec((T, LANES), row),
                  _full((SUB, AW)), _full((SUB, 3 * H * HD)), _full((SUB, LANES))],
        out_specs=[pl.BlockSpec((T, P_PAD), row), _full((SUB, AW)), _full((SUB, 3 * H * HD)), _full((SUB, LANES))],
        out_shape=[jax.ShapeDtypeStruct((L, P_PAD), MXU), jax.ShapeDtypeStruct((SUB, AW), F32),
                   jax.ShapeDtypeStruct((SUB, 3 * H * HD), F32), jax.ShapeDtypeStruct((SUB, LANES), F32)],
        scratch_shapes=[pltpu.VMEM((SUB, AW), F32), pltpu.VMEM((SUB, 3 * H * HD), F32)],
        compiler_params=_params(1),
    )(p, p, p, dqn, dkn, dvs, dya, dz, dgb, pa, cq, sp)


def _in_bwd(dp, w_in, x, dx2, modrows, vec):
    L = x.shape[0]
    T = _tile(L, 256)

    def body(dp_ref, w_ref, x_ref, dx2_ref, mod_ref, vec_ref, dx_ref, accv_ref):
        @pl.when(pl.program_id(0) == 0)
        def _():
            accv_ref[...] = jnp.zeros_like(accv_ref)

        dh = _dot(dp_ref[...], w_ref[...], NT)
        n, r = _rms(x_ref[...])
        nw, sc = vec_ref[0:1, :], mod_ref[1:2, :]
        accv_ref[0:1, :] += _sum0(dh)
        accv_ref[1:2, :] += _sum0(dh * n * nw)
        accv_ref[2:3, :] += _sum0(dh * n * (1.0 + sc))
        dx_ref[...] = _rms_bwd(dh * nw * (1.0 + sc), n, r) + dx2_ref[...]

    row = lambda i: (i, 0)
    return pl.pallas_call(
        body, name="in_bwd", grid=(L // T,),
        in_specs=[pl.BlockSpec((T, P_PAD), row), _full((D, P_PAD)), pl.BlockSpec((T, D), row),
                  pl.BlockSpec((T, D), row), _full((SUB, D)), _full((SUB, D))],
        out_specs=[pl.BlockSpec((T, D), row), _full((SUB, D))],
        out_shape=[jax.ShapeDtypeStruct((L, D), F32), jax.ShapeDtypeStruct((SUB, D), F32)],
        compiler_params=_params(1),
    )(dp, w_in, x, dx2, modrows, vec)


def _wgrad(a, b, tm, tn, name):
    L, m = a.shape
    n = b.shape[1]
    tl = _tile(L, 512)
    tm, tn = _tile(m, tm), _tile(n, tn)
    nl = L // tl

    def body(a_ref, b_ref, o_ref):
        @pl.when(pl.program_id(2) == 0)
        def _():
            o_ref[...] = jnp.zeros_like(o_ref)

        o_ref[...] += _dot(a_ref[...], b_ref[...], TN)

    return pl.pallas_call(
        body, name=name, grid=(m // tm, n // tn, nl),
        in_specs=[pl.BlockSpec((tl, tm), lambda i, j, l: (l, i)), pl.BlockSpec((tl, tn), lambda i, j, l: (l, j))],
        out_specs=pl.BlockSpec((tm, tn), lambda i, j, l: (i, j)),
        out_shape=jax.ShapeDtypeStruct((m, n), F32), compiler_params=_params(3),
    )(a, b)


def _wgrad_cols(a, b, tm, n_shard, wpad, count, name):
    L, m = a.shape
    n = b.shape[1]
    tl = _tile(L, 512)
    tm = _tile(m, tm)
    nl = L // tl
    wins = _shard_windows(n_shard, count)
    assert all(a_ * LANES + win <= n for a_, _, win in wins), (wins, n)

    def body(a_ref, b_ref, o_ref, acc):
        @pl.when(pl.program_id(1) == 0)
        def _():
            acc[...] = jnp.zeros_like(acc)

        acc[...] += _dot(a_ref[...], b_ref[...], TN)

        @pl.when(pl.program_id(1) == nl - 1)
        def _():
            for k, (a_, s, win) in enumerate(wins):
                xk = acc[:, a_ * LANES:a_ * LANES + win]
                if s:
                    xk = pltpu.roll(xk, win - s, 1)
                o_ref[k] = _fit_lanes(xk, wpad)

    return pl.pallas_call(
        body, name=name, grid=(m // tm, nl),
        in_specs=[pl.BlockSpec((tl, tm), lambda i, l: (l, i)), pl.BlockSpec((tl, n), lambda i, l: (l, 0))],
        out_specs=pl.BlockSpec((count, tm, wpad), lambda i, l: (0, i, 0)),
        out_shape=jax.ShapeDtypeStruct((count, m, wpad), F32),
        scratch_shapes=[pltpu.VMEM((tm, n), F32)],
        compiler_params=_params(2),
    )(a, b)


def _adamw(w, g, m, v, name):
    r, n = w.shape
    tr = _tile(r, 512)
    bc1 = 1.0 - ADAM_B1 ** ADAM_STEP
    bc2 = 1.0 - ADAM_B2 ** ADAM_STEP

    def body(w_ref, g_ref, m_ref, v_ref, d_ref, nm_ref, nv_ref):
        gv = g_ref[...]
        nm = ADAM_B1 * m_ref[...] + (1.0 - ADAM_B1) * gv
        nv = ADAM_B2 * v_ref[...] + (1.0 - ADAM_B2) * (gv * gv)
        nm_ref[...] = nm
        nv_ref[...] = nv
        d_ref[...] = -ADAM_LR * ((nm / bc1) / (jnp.sqrt(nv / bc2) + ADAM_EPS) + ADAM_WD * w_ref[...])

    spec = pl.BlockSpec((tr, n), lambda i: (i, 0))
    return pl.pallas_call(
        body, name=name, grid=(r // tr,), in_specs=[spec] * 4, out_specs=[spec] * 3,
        out_shape=[jax.ShapeDtypeStruct((r, n), F32)] * 3, compiler_params=_params(1),
    )(w, g, m, v)


def _rows8(rows, width):
    out = jnp.zeros((SUB, width), F32)
    for r, vrow in enumerate(rows):
        out = out.at[r, :vrow.shape[0]].set(vrow)
    return out


def _at_lanes(v4, start):
    return jnp.zeros((LANES,), F32).at[start:start + v4.shape[0]].set(v4)


def _pad_rows(flat, mult):
    n = flat.shape[0]
    pad = (-n) % mult
    return jnp.pad(flat, (0, pad)) if pad else flat


IN_PAD = 512
UP_PAD = 768


def _local_fwd_bwd(x, target, mod_full, small_w, full_w):
    norm1_w, norm2_w, norm_a_w, a_log, dt_bias, norm_dn_w, norm_f_w = small_w
    w_in_f, w_out_f, w_up_f, w_down_f, conv_a_f, conv_q_f, conv_f_f = full_w

    def layer_params(i):
        modrows = jnp.concatenate([mod_full[i], jnp.zeros((SUB - N_MOD, D), F32)], axis=0)
        vec = _rows8([norm1_w[i], norm2_w[i]], D)
        pa = _rows8([conv_a_f[i, 0], conv_a_f[i, 1], conv_a_f[i, 2], norm_a_w[i]], AW)
        cq = _rows8([conv_q_f[i, k] for k in range(4)], 3 * H * HD)
        sp = _rows8([_at_lanes(a_log[i], H), _at_lanes(dt_bias[i], H), norm_dn_w[i]], LANES)
        cff = _rows8([conv_f_f[i, k] for k in range(3)], 2 * DFF)
        return modrows, vec, pa, cq, sp, cff

    saved = []
    xi = x
    for i in range(DEPTH):
        modrows, vec, pa, cq, sp, cff = layer_params(i)
        p, h1 = _in_proj(xi, modrows, vec, w_in_f[i])
        qn, kn, vs, gb, ya = _pre_fwd(p, pa, cq, sp)
        o, states, tinvs = _gdr_fwd(qn, kn, vs, gb)
        y, x2, yb = _post_fwd(o, p, ya, xi, modrows, sp, w_out_f[i])
        h2, gp0, up0, f0, d0 = _ffn_fwd_half(x2, modrows, vec, w_up_f[i], cff, w_down_f[i], 0, None)
        gp1, up1, f1, dff, x3 = _ffn_fwd_half(x2, modrows, vec, w_up_f[i], cff, w_down_f[i], 1, d0)
        saved.append(dict(x=xi, p=p, h1=h1, qn=qn, kn=kn, vs=vs, gb=gb, ya=ya, o=o, states=states, tinvs=tinvs, y=y, x2=x2, yb=yb,
                          h2=h2, gpre=(gp0, gp1), upre=(up0, up1), f=(f0, f1), d=dff))
        xi = x3

    dx, facc = _final(xi, target, _rows8([norm_f_w], D))
    loss_local = jnp.sum(facc[0])
    d_norm_f = facc[1]

    gw_in, gw_out, gw_up, gw_down = [None] * DEPTH, [None] * DEPTH, [None] * DEPTH, [None] * DEPTH
    g_small = [None] * DEPTH
    for i in reversed(range(DEPTH)):
        s = saved[i]
        modrows, vec, pa, cq, sp, cff = layer_params(i)
        dd, dgp0, dup0, dh0, dcg0, dcu0 = _ffn_bwd_half(dx, modrows, s["gpre"][0], s["upre"][0], cff,
                                                        w_down_f[i], w_up_f[i], 0, None)
        dgp1, dup1, dx2, accf, dcg1, dcu1 = _ffn_bwd_half(dx, modrows, s["gpre"][1], s["upre"][1], cff,
                                                          w_down_f[i], w_up_f[i], 1, (s["d"], s["x2"], vec, dh0))
        n_up, up_pad = 2 * DFF // N_DEV, UP_PAD
        gw_up[i] = jnp.concatenate([_wgrad_cols(s["h2"], t, 1024, n_up, up_pad, FF_CW // n_up, "wgrad_up")
                                    for t in (dgp0, dgp1, dup0, dup1)], axis=0)
        gw_down[i] = jnp.concatenate([_wgrad(s["f"][0], dd, FF_CW, 1024, "wgrad_down"),
                                      _wgrad(s["f"][1], dd, FF_CW, 1024, "wgrad_down")],
                                     axis=0).reshape(N_DEV, DFF // N_DEV, D)
        dy, do, dz, dya, accp, accs = _post_bwd(dx2, s["y"], s["o"], s["p"], modrows, sp, w_out_f[i])
        gw_out[i] = jnp.concatenate([_wgrad(s["ya"], dy, 512, 1024, "wgrad_out"),
                                     _wgrad(s["yb"], dy, 512, 1024, "wgrad_out")], axis=0).reshape(N_DEV, D // N_DEV, D)
        dqn, dkn, dvs, dgb = _gdr_bwd(s["qn"], s["kn"], s["vs"], s["gb"], s["states"], s["tinvs"], do)
        dp, dpa, dcq, dsp = _pre_bwd(s["p"], dqn, dkn, dvs, dya, dz, dgb, pa, cq, sp)
        gw_in[i] = _wgrad_cols(s["h1"], dp, 512, P_IN // N_DEV, IN_PAD, N_DEV, "wgrad_in")
        dx, acci = _in_bwd(dp, w_in_f[i], s["x"], dx2, modrows, vec)
        dconv_ff = jnp.concatenate([dcg0, dcg1, dcu0, dcu1], axis=1)[0:3]
        dmod = jnp.stack([acci[0], acci[1], accp[0], accf[1], accf[2], accf[0]])
        g_small[i] = dict(norm1=acci[2], norm2=accf[3], norm_a=dpa[3], a_log=dsp[0, H:2 * H], dt_bias=dsp[1, H:2 * H],
                          norm_dn=accs[0], conv_a=dpa[0:3], conv_qkv=dcq[0:4], conv_ff=dconv_ff, dmod=dmod.reshape(-1))
    return loss_local, dx, gw_in, gw_out, gw_up, gw_down, g_small, d_norm_f


def kernel(x, c, ada_w, ada_b, norm1_w, w_in, conv_a_w, norm_a_w, conv_qkv_w, a_log, dt_bias, norm_dn_w, w_out, norm2_w, w_up, conv_ff_w, w_down, norm_f_w, loss_target, m_ada_w, m_ada_b, m_norm1_w, m_w_in, m_conv_a_w, m_norm_a_w, m_conv_qkv_w, m_a_log, m_dt_bias, m_norm_dn_w, m_w_out, m_norm2_w, m_w_up, m_conv_ff_w, m_w_down, m_norm_f_w, v_ada_w, v_ada_b, v_norm1_w, v_w_in, v_conv_a_w, v_norm_a_w, v_conv_qkv_w, v_a_log, v_dt_bias, v_norm_dn_w, v_w_out, v_norm2_w, v_w_up, v_conv_ff_w, v_w_down, v_norm_f_w):
    ax, ay, ac = lax.axis_index("x"), lax.axis_index("y"), lax.axis_index("c")
    me = 4 * ax + 2 * ay + ac
    x = x[0]
    target = loss_target[0]
    n_in, n_up = P_IN // N_DEV, 2 * DFF // N_DEV

    def lane_pad(t, width):
        return jnp.pad(t.astype(MXU), ((0, 0), (0, 0), (0, width - t.shape[-1])))

    conv_blob = _pad_rows(jnp.concatenate([t.reshape(-1) for t in (conv_a_w, conv_qkv_w, conv_ff_w)]),
                          SUB * LANES).reshape(-1, LANES)
    c_rows = jnp.zeros((SUB, D), F32).at[0].set(c[0])
    send = [lane_pad(w_in, IN_PAD), w_out.astype(MXU), lane_pad(w_up, UP_PAD), w_down.astype(MXU)]
    got = [None] * DEPTH
    *got[0], g_conv, g_c = _all_gather([t[0] for t in send] + [conv_blob, c_rows], "gather_weights", in_vmem=False)
    for i in range(1, DEPTH):
        got[i] = _all_gather_async([t[i] for t in send], "gather_weights_l%d" % i, collective_id=i)
    w_in_f = [_interleave_cols(g[0][:, None], n_in, P_PAD, "interleave_w_in")[0] for g in got]
    w_up_f = [_interleave_cols(g[2][:, None], n_up, 2 * DFF, "interleave_w_up")[0] for g in got]
    w_out_f = [g[1].reshape(D, D) for g in got]
    w_down_f = [g[3].reshape(DFF, D) for g in got]
    sg = g_conv.reshape(N_DEV, -1)
    o1 = conv_a_w.size
    o2 = o1 + conv_qkv_w.size
    o3 = o2 + conv_ff_w.size
    conv_a_f = sg[:, 0:o1].reshape(N_DEV, DEPTH, 3, AW // N_DEV).transpose(1, 2, 0, 3).reshape(DEPTH, 3, AW)
    conv_q_f = sg[:, o1:o2].reshape(N_DEV, DEPTH, 4, 3 * H * HD // N_DEV).transpose(1, 2, 0, 3).reshape(DEPTH, 4, 3 * H * HD)
    conv_f_f = sg[:, o2:o3].reshape(N_DEV, DEPTH, 3, n_up).transpose(1, 2, 0, 3).reshape(DEPTH, 3, 2 * DFF)

    c_all = jnp.concatenate([g_c[:, 0], jnp.zeros((16 - N_DEV, D), F32)], axis=0)
    n_ada = N_MOD * D // N_DEV
    ada_b_cols = lax.dynamic_slice_in_dim(ada_b, me * n_ada, n_ada, axis=1)[:, None, :]
    mod_sh = _mod_fwd(c_all, ada_w, ada_b_cols)
    mod_all = _all_gather([mod_sh.reshape(DEPTH * 16, n_ada)], "gather_mod", in_vmem=True)[0]
    mod_all = mod_all.reshape(N_DEV, DEPTH, 16, n_ada)
    mod_mine = lax.dynamic_index_in_dim(mod_all, me, axis=2, keepdims=False)
    mod_full = mod_mine.transpose(1, 0, 2).reshape(DEPTH, N_MOD, D)

    loss_local, dx, gw_in, gw_out, gw_up, gw_down, g_small, d_norm_f = _local_fwd_bwd(
        x, target, mod_full, (norm1_w, norm2_w, norm_a_w, a_log, dt_bias, norm_dn_w, norm_f_w),
        (w_in_f, w_out_f, w_up_f, w_down_f, conv_a_f, conv_q_f, conv_f_f))
    loss = lax.psum(loss_local, ("x", "y", "c"))
    grad_x = dx[None]

    keys = ["dmod", "norm1", "norm2", "norm_a", "a_log", "dt_bias", "norm_dn", "conv_a", "conv_qkv", "conv_ff"]
    stacked = {k: jnp.stack([g_small[i][k] for i in range(DEPTH)]) for k in keys}
    flat_parts = [stacked[k].reshape(-1) for k in keys] + [d_norm_f]
    sizes = [int(t.shape[0]) for t in flat_parts]
    sflat = _pad_rows(jnp.concatenate(flat_parts), SUB * LANES).reshape(-1, LANES)
    sall = _all_gather([sflat], "gather_small_grads", in_vmem=True)[0]
    ssum = _sum_devices(sall).reshape(-1)
    so = [0]
    for sz in sizes:
        so.append(so[-1] + sz)
    red = {k: ssum[so[n]:so[n + 1]].reshape(stacked[k].shape) for n, k in enumerate(keys)}
    g_norm_f = ssum[so[len(keys)]:so[len(keys) + 1]]
    dmod_all = sall[:, 0:sizes[0] // LANES, :].reshape(N_DEV, DEPTH, N_MOD * D)

    g_ada_b = red["dmod"].reshape(DEPTH, N_MOD * D)
    dmod_cols = lax.dynamic_slice_in_dim(dmod_all, me * n_ada, n_ada, axis=2).transpose(1, 0, 2)
    dmod_cols = jnp.concatenate([dmod_cols, jnp.zeros((DEPTH, 16 - N_DEV, n_ada), F32)], axis=1)
    g_ada_w = _mod_bwd(c_all, dmod_cols)
    g_conv_a = lax.dynamic_slice_in_dim(red["conv_a"], me * (AW // N_DEV), AW // N_DEV, axis=2)
    g_conv_qkv = lax.dynamic_slice_in_dim(red["conv_qkv"], me * (3 * H * HD // N_DEV), 3 * H * HD // N_DEV, axis=2)
    g_conv_ff = lax.dynamic_slice_in_dim(red["conv_ff"], me * n_up, n_up, axis=2)

    tags = ["w_in", "w_out", "w_up", "w_down"]
    gs = [jnp.stack(t, axis=1) for t in (gw_in, gw_out, gw_up, gw_down)]
    my_c = jnp.reshape(ac, (1,)).astype(jnp.int32)
    my_chip = jnp.reshape(2 * ax + ay, (1,)).astype(jnp.int32)
    recv1 = _rs_sibling(gs)
    pairs = [_rs_add_pairs(g, r, my_c, "rs_add_pairs_" + t) for g, r, t in zip(gs, recv1, tags)]
    recv2 = _rs_chips([pb for _, pb in pairs])
    mine = [_rs_add_chips(pf, r, my_chip, "rs_add_chips_" + t) for (pf, _), r, t in zip(pairs, recv2, tags)]
    g_w_in = mine[0][:, :, :n_in]
    g_w_out = mine[1]
    g_w_up = mine[2][:, :, :n_up]
    g_w_down = mine[3]

    grads = dict(ada_w=g_ada_w, ada_b=g_ada_b, norm1_w=red["norm1"], w_in=g_w_in, conv_a_w=g_conv_a,
                 norm_a_w=red["norm_a"], conv_qkv_w=g_conv_qkv, a_log=red["a_log"], dt_bias=red["dt_bias"],
                 norm_dn_w=red["norm_dn"], w_out=g_w_out, norm2_w=red["norm2"], w_up=g_w_up, conv_ff_w=g_conv_ff,
                 w_down=g_w_down, norm_f_w=g_norm_f)
    weights = dict(ada_w=ada_w, ada_b=ada_b, norm1_w=norm1_w, w_in=w_in, conv_a_w=conv_a_w, norm_a_w=norm_a_w,
                   conv_qkv_w=conv_qkv_w, a_log=a_log, dt_bias=dt_bias, norm_dn_w=norm_dn_w, w_out=w_out,
                   norm2_w=norm2_w, w_up=w_up, conv_ff_w=conv_ff_w, w_down=w_down, norm_f_w=norm_f_w)
    ms = dict(ada_w=m_ada_w, ada_b=m_ada_b, norm1_w=m_norm1_w, w_in=m_w_in, conv_a_w=m_conv_a_w, norm_a_w=m_norm_a_w,
              conv_qkv_w=m_conv_qkv_w, a_log=m_a_log, dt_bias=m_dt_bias, norm_dn_w=m_norm_dn_w, w_out=m_w_out,
              norm2_w=m_norm2_w, w_up=m_w_up, conv_ff_w=m_conv_ff_w, w_down=m_w_down, norm_f_w=m_norm_f_w)
    vs_ = dict(ada_w=v_ada_w, ada_b=v_ada_b, norm1_w=v_norm1_w, w_in=v_w_in, conv_a_w=v_conv_a_w, norm_a_w=v_norm_a_w,
               conv_qkv_w=v_conv_qkv_w, a_log=v_a_log, dt_bias=v_dt_bias, norm_dn_w=v_norm_dn_w, w_out=v_w_out,
               norm2_w=v_norm2_w, w_up=v_w_up, conv_ff_w=v_conv_ff_w, w_down=v_w_down, norm_f_w=v_norm_f_w)
    names = list(weights)
    big_names = ["ada_w", "w_in", "w_out", "w_up", "w_down"]
    delta, new_m, new_v = {}, {}, {}
    for n in big_names:
        shp = weights[n].shape
        two = lambda t: t.reshape(-1, shp[-1])
        dl, nm, nv = _adamw(two(weights[n]), two(grads[n]), two(ms[n]), two(vs_[n]), "adamw_" + n)
        delta[n], new_m[n], new_v[n] = dl.reshape(shp), nm.reshape(shp), nv.reshape(shp)
    small_names = [n for n in names if n not in big_names]

    def pack(dct):
        return _pad_rows(jnp.concatenate([dct[n].reshape(-1) for n in small_names]), SUB * LANES).reshape(-1, LANES)

    dl, nm, nv = _adamw(pack(weights), pack(grads), pack(ms), pack(vs_), "adamw_small")
    off = 0
    for n in small_names:
        sz, shp = weights[n].size, weights[n].shape
        delta[n] = dl.reshape(-1)[off:off + sz].reshape(shp)
        new_m[n] = nm.reshape(-1)[off:off + sz].reshape(shp)
        new_v[n] = nv.reshape(-1)[off:off + sz].reshape(shp)
        off += sz

    return (loss, grad_x, *[grads[n] for n in names], *[delta[n] for n in names],
            *[new_m[n] for n in names], *[new_v[n] for n in names])
```

```python
import functools
import math

import jax
import jax.numpy as jnp
from jax import lax
from jax.experimental import pallas as pl
from jax.experimental.pallas import tpu as pltpu
from jax.experimental.pallas import tpu_sc as plsc

F32 = jnp.float32
MXU = jnp.bfloat16

D = 1024
DEPTH = 4
N_MOD = 6
AW = 512
A_GROUP = 64
H = 4
HD = 128
CK = 64
DFF = 2816
P_IN = 3592
P_PAD = 3712
EPS = 1e-6
N_DEV = 8
LANES = 128
SUB = 8
VMEM_LIMIT = 56 * 1024 * 1024

ADAM_LR, ADAM_B1, ADAM_B2, ADAM_EPS, ADAM_WD, ADAM_STEP = 0.001, 0.9, 0.999, 1e-08, 0.01, 10

NN = ((1,), (0,))
NT = ((1,), (1,))
TN = ((0,), (0,))
HI = lax.Precision.HIGHEST
MESH = pl.DeviceIdType.MESH


def _dot(a, b, dims, prec=None):
    if prec is None:
        a = a.astype(MXU) if a.dtype == F32 else a
        b = b.astype(MXU) if b.dtype == F32 else b
    return lax.dot_general(a, b, (dims, ((), ())), precision=prec, preferred_element_type=F32)


def _params(n_grid=0, limit=VMEM_LIMIT):
    sem = ("arbitrary",) * n_grid if n_grid else None
    return pltpu.CompilerParams(dimension_semantics=sem, vmem_limit_bytes=limit)


def _tile(n, want):
    if n <= want:
        return n
    t = want - want % SUB
    while n % t:
        t -= SUB
    assert t > 0, (n, want)
    return t


def _full(shape):
    nd = len(shape)
    return pl.BlockSpec(shape, lambda *_: (0,) * nd)


def _sig(x):
    return jax.nn.sigmoid(x)


def _rms(x):
    r = lax.rsqrt(jnp.mean(x * x, axis=-1, keepdims=True) + EPS)
    return x * r, r


def _rms_bwd(dn, n, r):
    return r * (dn - n * jnp.mean(dn * n, axis=-1, keepdims=True))


def _l2_bwd(dn, n, r):
    return r * (dn - n * jnp.sum(dn * n, axis=-1, keepdims=True))


def _sum0(x):
    return jnp.sum(x, axis=0, keepdims=True)


def _shift_down(x, s, halo):
    ext = jnp.concatenate([halo, x], axis=0)
    return pltpu.roll(ext, s, 0)[SUB:, :]


def _shift_up(x, s, halo):
    t = x.shape[0]
    ext = jnp.concatenate([x, halo], axis=0)
    return pltpu.roll(ext, t + SUB - s, 0)[:t, :]


def _conv_fwd(x, w_ref, width, halo):
    sh = [x] + [_shift_down(x, s, halo) for s in range(1, width)]
    out = w_ref[width - 1:width, :] * sh[0]
    for s in range(1, width):
        out = out + w_ref[width - 1 - s:width - s, :] * sh[s]
    return out, sh


def _conv_bwd_in(dout, w_ref, width, halo_next):
    dx = w_ref[width - 1:width, :] * dout
    for s in range(1, width):
        dx = dx + w_ref[width - 1 - s:width - s, :] * _shift_up(dout, s, halo_next)
    return dx


def _blockdiag_mean(n, group):
    r = lax.shift_right_logical(lax.broadcasted_iota(jnp.int32, (n, n), 0), int(math.log2(group)))
    c = lax.shift_right_logical(lax.broadcasted_iota(jnp.int32, (n, n), 1), int(math.log2(group)))
    return jnp.where(r == c, 1.0 / group, 0.0).astype(F32)


def _softplus(x):
    return jnp.maximum(x, 0.0) + jnp.log(1.0 + jnp.exp(-jnp.abs(x)))


def _my_place():
    return lax.axis_index("x"), lax.axis_index("y"), lax.axis_index("c")


def _all_gather(shards, name, in_vmem):
    nt = len(shards)

    def body(*refs):
        x_refs, out_refs = refs[:nt], refs[nt:2 * nt]
        send_sems, recv_sems, local_sems = refs[2 * nt:]
        x, y, c = _my_place()
        me, sibling = (x, y, c), (x, y, 1 - c)
        chips = [(1 - x, y), (x, 1 - y), (1 - x, 1 - y)]
        everything = []
        for t in range(nt):
            x_ref, out_ref = x_refs[t], out_refs[t]

            def blk(px, py, pc, out_ref=out_ref):
                return out_ref.at[4 * px + 2 * py + pc]

            def copy(k, block, to, src=None, t=t, blk=blk):
                return pltpu.make_async_remote_copy(
                    src_ref=blk(*block) if src is None else src, dst_ref=blk(*block),
                    send_sem=send_sems.at[7 * t + k], recv_sem=recv_sems.at[7 * t + k], device_id=to, device_id_type=MESH)

            mine = pltpu.make_async_copy(x_ref, blk(*me), local_sems.at[t])
            mine.start()
            first = [copy(0, me, sibling, src=x_ref)]
            first += [copy(1 + j, me, (*chip, c), src=x_ref) for j, chip in enumerate(chips)]
            for cp in first:
                cp.start()
            everything.append((copy, mine, first))
        sends = []
        for copy, mine, first in everything:
            passed = [copy(4 + j, (*chip, c), sibling) for j, chip in enumerate(chips)]
            for j, chip in enumerate(chips):
                copy(1 + j, (*chip, c), me).wait_recv()
                passed[j].start()
            sends += first + passed
        for copy, mine, first in everything:
            copy(0, sibling, me).wait_recv()
            for j, chip in enumerate(chips):
                copy(4 + j, (*chip, 1 - c), me).wait_recv()
        for cp in sends:
            cp.wait_send()
        for copy, mine, first in everything:
            mine.wait()

    space = pltpu.VMEM if in_vmem else pl.ANY
    return pl.pallas_call(
        body, name=name,
        out_shape=[jax.ShapeDtypeStruct((N_DEV,) + s.shape, s.dtype) for s in shards],
        in_specs=[pl.BlockSpec(memory_space=space)] * nt,
        out_specs=[pl.BlockSpec(memory_space=space)] * nt,
        scratch_shapes=[pltpu.SemaphoreType.DMA((7 * nt,)), pltpu.SemaphoreType.DMA((7 * nt,)),
                        pltpu.SemaphoreType.DMA((nt,))],
        compiler_params=pltpu.CompilerParams(vmem_limit_bytes=VMEM_LIMIT),
    )(*shards)


def _all_gather_async(shards, name, collective_id):
    nt = len(shards)
    hbm = pltpu.MemorySpace.HBM
    x_refs = [jax.new_ref(s, memory_space=hbm) for s in shards]
    out_refs = [jax.empty_ref(jax.ShapeDtypeStruct((N_DEV,) + s.shape, s.dtype), memory_space=hbm) for s in shards]

    @pl.kernel(mesh=plsc.ScalarSubcoreMesh(axis_name="sequencer", num_cores=1), name=name,
               scratch_types=(pltpu.SemaphoreType.DMA((7 * nt,)), pltpu.SemaphoreType.DMA((7 * nt,)),
                              pltpu.SemaphoreType.DMA((nt,))),
               compiler_params=pltpu.CompilerParams(collective_id=collective_id))
    def launch(send_sems, recv_sems, local_sems):
        x, y, c = _my_place()
        me, sibling = (x, y, c), (x, y, 1 - c)
        chips = [(1 - x, y), (x, 1 - y), (1 - x, 1 - y)]
        barrier = pltpu.get_barrier_semaphore()
        for peer in [sibling] + [(*chip, c) for chip in chips]:
            pl.semaphore_signal(barrier, inc=1, device_id=peer, device_id_type=MESH)
        pl.semaphore_wait(barrier, 4)
        everything = []
        for t in range(nt):
            x_ref, out_ref = x_refs[t], out_refs[t]

            def blk(px, py, pc, out_ref=out_ref):
                return out_ref.at[4 * px + 2 * py + pc]

            def copy(k, block, to, src=None, t=t, blk=blk):
                return pltpu.make_async_remote_copy(
                    src_ref=blk(*block) if src is None else src, dst_ref=blk(*block),
                    send_sem=send_sems.at[7 * t + k], recv_sem=recv_sems.at[7 * t + k], device_id=to, device_id_type=MESH)

            mine = pltpu.make_async_copy(x_ref, blk(*me), local_sems.at[t])
            mine.start()
            first = [copy(0, me, sibling, src=x_ref)]
            first += [copy(1 + j, me, (*chip, c), src=x_ref) for j, chip in enumerate(chips)]
            for cp in first:
                cp.start()
            everything.append((copy, mine, first))
        sends = []
        for copy, mine, first in everything:
            passed = [copy(4 + j, (*chip, c), sibling) for j, chip in enumerate(chips)]
            for j, chip in enumerate(chips):
                copy(1 + j, (*chip, c), me).wait_recv()
                passed[j].start()
            sends += first + passed
        for copy, mine, first in everything:
            copy(0, sibling, me).wait_recv()
            for j, chip in enumerate(chips):
                copy(4 + j, (*chip, 1 - c), me).wait_recv()
        for cp in sends:
            cp.wait_send()
        for copy, mine, first in everything:
            mine.wait()

    launch()
    return [r[...] for r in out_refs]


def _rs_sibling(gs):
    nt = len(gs)

    def body(*refs):
        g_refs, recv_refs = refs[:nt], refs[nt:2 * nt]
        send_sems, recv_sems = refs[2 * nt:]
        x, y, c = _my_place()
        copies = [pltpu.make_async_remote_copy(
            src_ref=g_refs[t].at[2 * j + (1 - c)], dst_ref=recv_refs[t].at[j],
            send_sem=send_sems.at[4 * t + j], recv_sem=recv_sems.at[4 * t + j],
            device_id=(x, y, 1 - c), device_id_type=MESH) for t in range(nt) for j in range(4)]
        for cp in copies:
            cp.start()
        for cp in copies:
            cp.wait()

    return pl.pallas_call(
        body, name="rs_sibling",
        out_shape=[jax.ShapeDtypeStruct((4,) + g.shape[1:], g.dtype) for g in gs],
        in_specs=[pl.BlockSpec(memory_space=pl.ANY)] * nt, out_specs=[pl.BlockSpec(memory_space=pl.ANY)] * nt,
        scratch_shapes=[pltpu.SemaphoreType.DMA((4 * nt,)), pltpu.SemaphoreType.DMA((4 * nt,))],
    )(*gs)


def _rs_chips(pbs):
    nt = len(pbs)

    def body(*refs):
        p_refs, recv_refs = refs[:nt], refs[nt:2 * nt]
        send_sems, recv_sems = refs[2 * nt:]
        x, y, c = _my_place()
        chips = [(1 - x, y), (x, 1 - y), (1 - x, 1 - y)]
        copies = [pltpu.make_async_remote_copy(
            src_ref=p_refs[t].at[2 * px + py], dst_ref=recv_refs[t].at[s],
            send_sem=send_sems.at[3 * t + s], recv_sem=recv_sems.at[3 * t + s],
            device_id=(px, py, c), device_id_type=MESH) for t in range(nt) for s, (px, py) in enumerate(chips)]
        for cp in copies:
            cp.start()
        for cp in copies:
            cp.wait()

    return pl.pallas_call(
        body, name="rs_chips",
        out_shape=[jax.ShapeDtypeStruct((3,) + p.shape[1:], p.dtype) for p in pbs],
        in_specs=[pl.BlockSpec(memory_space=pl.ANY)] * nt, out_specs=[pl.BlockSpec(memory_space=pl.ANY)] * nt,
        scratch_shapes=[pltpu.SemaphoreType.DMA((3 * nt,)), pltpu.SemaphoreType.DMA((3 * nt,))],
    )(*pbs)


def _rs_async(srcs, n_recv, to_sibling, name, collective_id):
    nt = len(srcs)
    hbm = pltpu.MemorySpace.HBM
    src_refs = [jax.new_ref(s, memory_space=hbm) for s in srcs]
    out_refs = [jax.empty_ref(jax.ShapeDtypeStruct((n_recv,) + s.shape[1:], s.dtype), memory_space=hbm) for s in srcs]

    @pl.kernel(mesh=plsc.ScalarSubcoreMesh(axis_name="sequencer", num_cores=1), name=name,
               scratch_types=(pltpu.SemaphoreType.DMA((n_recv * nt,)), pltpu.SemaphoreType.DMA((n_recv * nt,))),
               compiler_params=pltpu.CompilerParams(collective_id=collective_id))
    def launch(send_sems, recv_sems):
        x, y, c = _my_place()
        if to_sibling:
            peers = [(x, y, 1 - c)]
            plan = [(2 * j + (1 - c), j, peers[0]) for j in range(4)]
        else:
            peers = [(1 - x, y, c), (x, 1 - y, c), (1 - x, 1 - y, c)]
            plan = [(2 * px + py, s, (px, py, pc)) for s, (px, py, pc) in enumerate(peers)]
        barrier = pltpu.get_barrier_semaphore()
        for peer in peers:
            pl.semaphore_signal(barrier, inc=1, device_id=peer, device_id_type=MESH)
        pl.semaphore_wait(barrier, len(peers))
        copies = [pltpu.make_async_remote_copy(
            src_ref=src_refs[t].at[blk], dst_ref=out_refs[t].at[slot],
            send_sem=send_sems.at[n_recv * t + slot], recv_sem=recv_sems.at[n_recv * t + slot],
            device_id=to, device_id_type=MESH) for t in range(nt) for blk, slot, to in plan]
        for cp in copies:
            cp.start()
        for cp in copies:
            cp.wait()

    launch()
    return [r[...] for r in out_refs]


def _rs_add_pairs(g, recv, my_c, name):
    _, nl, r, n = g.shape
    tr = _tile(r, 512)

    def body(c_ref, g_ref, r_ref, pf_ref, pb_ref):
        s = g_ref[...] + r_ref[...]
        pf_ref[...] = s
        pb_ref[...] = s.astype(MXU)

    spec_j = pl.BlockSpec((None, None, tr, n), lambda j, l, i, c_ref: (j, l, i, 0))
    return pl.pallas_call(
        body, name=name,
        grid_spec=pltpu.PrefetchScalarGridSpec(
            num_scalar_prefetch=1, grid=(4, nl, r // tr),
            in_specs=[pl.BlockSpec((None, None, tr, n), lambda j, l, i, c_ref: (2 * j + c_ref[0], l, i, 0)), spec_j],
            out_specs=[spec_j, spec_j]),
        out_shape=[jax.ShapeDtypeStruct((4, nl, r, n), F32), jax.ShapeDtypeStruct((4, nl, r, n), MXU)],
        compiler_params=_params(3),
    )(my_c, g, recv)


def _rs_add_chips(pf, recv, my_chip, name):
    _, nl, r, n = pf.shape
    tr = _tile(r, 512)

    def body(j_ref, p_ref, r_ref, o_ref):
        s = p_ref[...]
        for t in range(3):
            s = s + r_ref[t].astype(F32)
        o_ref[...] = s

    return pl.pallas_call(
        body, name=name,
        grid_spec=pltpu.PrefetchScalarGridSpec(
            num_scalar_prefetch=1, grid=(nl, r // tr),
            in_specs=[pl.BlockSpec((None, None, tr, n), lambda l, i, j_ref: (j_ref[0], l, i, 0)),
                      pl.BlockSpec((3, None, tr, n), lambda l, i, j_ref: (0, l, i, 0))],
            out_specs=pl.BlockSpec((None, tr, n), lambda l, i, j_ref: (l, i, 0))),
        out_shape=jax.ShapeDtypeStruct((nl, r, n), F32),
        compiler_params=_params(2),
    )(my_chip, pf, recv)


def _shard_windows(n_shard, count, first=0):
    out = []
    for k in range(first, first + count):
        off = n_shard * k
        a, s = off // LANES, off % LANES
        out.append((a, s, -(-(s + n_shard) // LANES) * LANES))
    return out


def _fit_lanes(x, width):
    have = x.shape[1]
    if have < width:
        return jnp.concatenate([x, jnp.zeros((x.shape[0], width - have), x.dtype)], axis=-1)
    return x[:, :width]


def _interleave_cols(g, n_shard, w_out, name):
    nd, nl, rows, wpad = g.shape
    rb = _tile(rows, 256)
    wins = _shard_windows(n_shard, nd)

    def body(g_ref, o_ref, acc):
        acc[...] = jnp.zeros_like(acc)
        for k, (a, s, win) in enumerate(wins):
            xk = _fit_lanes(g_ref[k].astype(F32), win)
            if s:
                xk = pltpu.roll(xk, s, 1)
            acc[:, a * LANES:a * LANES + win] += xk
        o_ref[...] = acc[...].astype(o_ref.dtype)

    return pl.pallas_call(
        body, name=name, grid=(nl, rows // rb),
        in_specs=[pl.BlockSpec((nd, None, rb, wpad), lambda l, i: (0, l, i, 0))],
        out_specs=pl.BlockSpec((None, rb, w_out), lambda l, i: (l, i, 0)),
        out_shape=jax.ShapeDtypeStruct((nl, rows, w_out), g.dtype),
        scratch_shapes=[pltpu.VMEM((rb, w_out), F32)],
        compiler_params=_params(2),
    )(g)


def _sum_devices(g):
    _, r, n = g.shape

    def body(g_ref, o_ref):
        s = g_ref[0]
        for t in range(1, N_DEV):
            s = s + g_ref[t]
        o_ref[...] = s

    return pl.pallas_call(
        body, name="sum_devices", out_shape=jax.ShapeDtypeStruct((r, n), F32),
        in_specs=[pl.BlockSpec(memory_space=pltpu.VMEM)], out_specs=pl.BlockSpec(memory_space=pltpu.VMEM),
        compiler_params=pltpu.CompilerParams(vmem_limit_bytes=VMEM_LIMIT),
    )(g)


def _mod_fwd(c_all, ada_w, ada_b_cols):
    nl, _, nc = ada_w.shape

    def body(c_ref, w_ref, b_ref, o_ref):
        cv = c_ref[...]
        act = (cv * _sig(cv)).astype(MXU)
        o_ref[...] = _dot(act, w_ref[...].astype(MXU), NN) + b_ref[...]

    return pl.pallas_call(
        body, name="mod_fwd", grid=(nl,),
        in_specs=[_full((16, D)), pl.BlockSpec((None, D, nc), lambda i: (i, 0, 0)),
                  pl.BlockSpec((None, 1, nc), lambda i: (i, 0, 0))],
        out_specs=pl.BlockSpec((None, 16, nc), lambda i: (i, 0, 0)),
        out_shape=jax.ShapeDtypeStruct((nl, 16, nc), F32), compiler_params=_params(1),
    )(c_all, ada_w, ada_b_cols)


def _mod_bwd(c_all, dmod_cols):
    nl, _, nc = dmod_cols.shape

    def body(c_ref, d_ref, o_ref):
        cv = c_ref[...]
        act = (cv * _sig(cv)).astype(MXU)
        o_ref[...] = _dot(act, d_ref[...].astype(MXU), TN)

    return pl.pallas_call(
        body, name="mod_bwd", grid=(nl,),
        in_specs=[_full((16, D)), pl.BlockSpec((None, 16, nc), lambda i: (i, 0, 0))],
        out_specs=pl.BlockSpec((None, D, nc), lambda i: (i, 0, 0)),
        out_shape=jax.ShapeDtypeStruct((nl, D, nc), F32), compiler_params=_params(1),
    )(c_all, dmod_cols)


def _in_proj(x, modrows, vec, w_in):
    L = x.shape[0]
    T = _tile(L, 256)

    def body(x_ref, mod_ref, vec_ref, w_ref, p_ref, h_ref):
        n, _ = _rms(x_ref[...])
        h = n * vec_ref[0:1, :] * (1.0 + mod_ref[1:2, :]) + mod_ref[0:1, :]
        hb = h.astype(MXU)
        h_ref[...] = hb
        p_ref[...] = _dot(hb, w_ref[...], NN)

    return pl.pallas_call(
        body, name="in_proj", grid=(L // T,),
        in_specs=[pl.BlockSpec((T, D), lambda i: (i, 0)), _full((SUB, D)), _full((SUB, D)), _full((D, P_PAD))],
        out_specs=[pl.BlockSpec((T, P_PAD), lambda i: (i, 0)), pl.BlockSpec((T, D), lambda i: (i, 0))],
        out_shape=[jax.ShapeDtypeStruct((L, P_PAD), F32), jax.ShapeDtypeStruct((L, D), MXU)],
        compiler_params=_params(1),
    )(x, modrows, vec, w_in)


def _gate_small(s, sp_ref):
    lane = lax.broadcasted_iota(jnp.int32, s.shape, 1)
    a = -jnp.exp(sp_ref[0:1, :])
    xb = s + sp_ref[1:2, :]
    beta = _sig(s)
    g = a * _softplus(xb)
    return lane, a, xb, beta, g


def _pre_fwd(p, pa, cq, sp):
    L = p.shape[0]
    T = _tile(L, 256)
    scale = HD ** -0.5

    def body(pm_ref, ps_ref, pa_ref, cq_ref, sp_ref, qn_ref, kn_ref, vs_ref, gb_ref, ya_ref, u_carry, q_carry):
        @pl.when(pl.program_id(0) == 0)
        def _():
            u_carry[...] = jnp.zeros_like(u_carry)
            q_carry[...] = jnp.zeros_like(q_carry)

        a_b = pm_ref[:, 0:AW]
        u = pm_ref[:, AW:2 * AW] * pm_ref[:, 2 * AW:3 * AW]
        cu, _ = _conv_fwd(u, pa_ref, 3, u_carry[...])
        u_carry[...] = u[T - SUB:T, :]
        yp = a_b * cu
        ms = _dot_f32(yp * yp, _blockdiag_mean(AW, A_GROUP), NN, exact="b")
        ya_ref[...] = (yp * lax.rsqrt(ms + EPS) * pa_ref[3:4, :]).astype(MXU)

        qkv = pm_ref[:, 3 * AW:3 * AW + 3 * H * HD]
        qc, _ = _conv_fwd(qkv, cq_ref, 4, q_carry[...])
        q_carry[...] = qkv[T - SUB:T, :]
        qs = qc * _sig(qc)
        for h in range(H):
            q = qs[:, h * HD:(h + 1) * HD]
            qn_ref[:, h * HD:(h + 1) * HD] = q * (lax.rsqrt(jnp.sum(q * q, axis=-1, keepdims=True) + EPS) * scale)
            k = qs[:, (H + h) * HD:(H + h + 1) * HD]
            kn_ref[:, h * HD:(h + 1) * HD] = k * lax.rsqrt(jnp.sum(k * k, axis=-1, keepdims=True) + EPS)
        vs_ref[...] = qs[:, 2 * H * HD:3 * H * HD]

        lane, _, _, beta, g = _gate_small(ps_ref[...], sp_ref)
        gb_ref[...] = jnp.where(lane < H, beta, jnp.where(lane < 2 * H, g, 0.0))

    w3 = 3 * AW + 3 * H * HD
    row = lambda i: (i, 0)
    return pl.pallas_call(
        body, name="pre_fwd", grid=(L // T,),
        in_specs=[pl.BlockSpec((T, w3), row), pl.BlockSpec((T, LANES), lambda i: (i, (P_PAD - LANES) // LANES)),
                  _full((SUB, AW)), _full((SUB, 3 * H * HD)), _full((SUB, LANES))],
        out_specs=[pl.BlockSpec((T, H * HD), row)] * 3 + [pl.BlockSpec((T, LANES), row), pl.BlockSpec((T, AW), row)],
        out_shape=[jax.ShapeDtypeStruct((L, H * HD), F32)] * 3
        + [jax.ShapeDtypeStruct((L, LANES), F32), jax.ShapeDtypeStruct((L, AW), MXU)],
        scratch_shapes=[pltpu.VMEM((SUB, AW), F32), pltpu.VMEM((SUB, 3 * H * HD), F32)],
        compiler_params=_params(1),
    )(p, p, pa, cq, sp)


def _gdr_masks():
    r = lax.broadcasted_iota(jnp.int32, (CK, CK), 0)
    c = lax.broadcasted_iota(jnp.int32, (CK, CK), 1)
    return r >= c, r > c


def _head_cols(gbt, h):
    return gbt[:, h:h + 1], gbt[:, H + h:H + h + 1]


def _split(x, parts):
    out = []
    for _ in range(parts):
        hi = x.astype(jnp.bfloat16)
        out.append(hi)
        x = x - hi.astype(F32)
    return out


def _dot_f32(a, b, dims, exact=None):
    if exact == "a":
        ab = a.astype(jnp.bfloat16)
        return sum(_dot(ab, t, dims) for t in _split(b, 3))
    if exact == "b":
        bb = b.astype(jnp.bfloat16)
        return sum(_dot(t, bb, dims) for t in _split(a, 3))
    ah, al = _split(a, 2)
    bh, bl = _split(b, 2)
    return _dot(ah, bh, dims) + _dot(ah, bl, dims) + _dot(al, bh, dims)


def _gdr_consts():
    causal, strict = _gdr_masks()
    return dict(causal=causal, strict=strict, tril=jnp.where(causal, 1.0, 0.0).astype(F32),
                eye=jnp.where(causal & jnp.logical_not(strict), 1.0, 0.0).astype(F32),
                bcast=jnp.full((CK, HD), 1.0 / HD, F32))


def _dots(a, b, dims):
    return [_dot(x, y, dims) for x, y in zip(a, b)]


def _dots_f32(a, b, dims, exact=None):
    n = len(a)
    if exact == "a":
        lhs = [[x.astype(jnp.bfloat16)] * 3 for x in a]
        rhs = [_split(y, 3) for y in b]
    elif exact == "b":
        lhs = [_split(x, 3) for x in a]
        rhs = [[y.astype(jnp.bfloat16)] * 3 for y in b]
    else:
        sa = [_split(x, 2) for x in a]
        sb = [_split(y, 2) for y in b]
        lhs = [[s[0], s[0], s[1]] for s in sa]
        rhs = [[s[0], s[1], s[0]] for s in sb]
    terms = [[_dot(lhs[i][t], rhs[i][t], dims) for i in range(n)] for t in range(3)]
    return [terms[0][i] + terms[1][i] + terms[2][i] for i in range(n)]


def _gdr_local(q, k, v, beta, g, cst, tinv=None):
    n = len(q)
    R = range(n)
    causal, strict = cst["causal"], cst["strict"]
    gc = _dots_f32([cst["tril"]] * n, [jnp.broadcast_to(g[i], (CK, HD)) for i in R], NN, exact="a")
    g_row = _dots_f32([cst["bcast"]] * n, gc, NT, exact="a")
    decay = [jnp.where(causal, jnp.exp(jnp.where(causal, gc[i][:, 0:CK] - g_row[i], 0.0)), 0.0) for i in R]
    eg = [jnp.exp(gc[i]) for i in R]
    gl = [gc[i][CK - 1:CK, :] for i in R]
    ek = [jnp.exp(gl[i] - gc[i]) for i in R]
    cd = [jnp.exp(gl[i]) for i in R]
    kb = [k[i] * beta[i] for i in R]
    pk = _dots(kb, k, NT)
    if tinv is None:
        xp = [-jnp.where(strict, pk[i] * decay[i], 0.0) for i in R]
        tinv = [cst["eye"] + xp[i] for i in R]
        for _ in range(5):
            xp = _dots_f32(xp, xp, NN)
            tx = _dots_f32(tinv, xp, NN)
            tinv = [tinv[i] + tx[i] for i in R]
    u = _dots(tinv, [v[i] * beta[i] for i in R], NN)
    w = _dots(tinv, [kb[i] * eg[i] for i in R], NN)
    qk = _dots(q, k, NT)
    intra = [jnp.where(causal, qk[i] * decay[i], 0.0) for i in R]
    return dict(decay=decay, eg=eg, ek=ek, cd=cd, kb=kb, pk=pk, tinv=tinv, u=u, w=w, qk=qk, intra=intra,
                q_dec=[q[i] * eg[i] for i in R], k_dec=[k[i] * ek[i] for i in R])


GDR_SUB = 4


def _gdr_fwd(qn, kn, vs, gb):
    L = qn.shape[0]
    nc = L // CK
    cb = min(8, nc)
    rb = cb * CK
    nb = nc // cb
    nsub = GDR_SUB if cb % GDR_SUB == 0 else 1

    def body(q_ref, k_ref, v_ref, gb_ref, o_ref, st_ref, ti_ref, s_ref):
        @pl.when(pl.program_id(0) == 0)
        def _():
            s_ref[...] = jnp.zeros_like(s_ref)

        cst = _gdr_consts()
        heads = range(H)

        def group(gi, carry):
            rows = [pl.ds(pl.multiple_of((gi * nsub + j) * CK, CK), CK) for j in range(nsub)]
            chains = [(j, h) for j in range(nsub) for h in heads]
            gbt = [gb_ref[rows[j], :] for j in range(nsub)]
            cols = lambda h: slice(h * HD, (h + 1) * HD)
            t = _gdr_local([q_ref[rows[j], cols(h)] for j, h in chains], [k_ref[rows[j], cols(h)] for j, h in chains],
                           [v_ref[rows[j], cols(h)] for j, h in chains],
                           [_head_cols(gbt[j], h)[0] for j, h in chains], [_head_cols(gbt[j], h)[1] for j, h in chains], cst)
            s = [s_ref[h] for h in heads]
            for j in range(nsub):
                at = lambda key: [t[key][j * H + h] for h in heads]
                for h in heads:
                    st_ref[h, gi * nsub + j] = s[h]
                    ti_ref[h, gi * nsub + j] = t["tinv"][j * H + h]
                ws = _dots(at("w"), s, NN)
                v_new = [u_h - ws_h for u_h, ws_h in zip(at("u"), ws)]
                o_s = _dots(at("q_dec"), s, NN)
                o_v = _dots(at("intra"), v_new, NN)
                kv = _dots(at("k_dec"), v_new, TN)
                cd = at("cd")
                for h in heads:
                    o_ref[rows[j], cols(h)] = o_s[h] + o_v[h]
                s = [s[h] * cd[h] + kv[h] for h in heads]
            for h in heads:
                s_ref[h] = s[h]
            return carry

        lax.fori_loop(0, cb // nsub, group, 0)

    blk = pl.BlockSpec((rb, H * HD), lambda b: (b, 0))
    return pl.pallas_call(
        body, name="gdr_fwd", grid=(nb,),
        in_specs=[blk, blk, blk, pl.BlockSpec((rb, LANES), lambda b: (b, 0))],
        out_specs=[blk, pl.BlockSpec((H, cb, HD, HD), lambda b: (0, b, 0, 0)),
                   pl.BlockSpec((H, cb, CK, CK), lambda b: (0, b, 0, 0))],
        out_shape=[jax.ShapeDtypeStruct((L, H * HD), F32), jax.ShapeDtypeStruct((H, nc, HD, HD), F32),
                   jax.ShapeDtypeStruct((H, nc, CK, CK), F32)],
        scratch_shapes=[pltpu.VMEM((H, HD, HD), F32)],
        compiler_params=_params(1),
    )(qn, kn, vs, gb)


def _gdr_bwd(qn, kn, vs, gb, states, tinvs, do):
    L = qn.shape[0]
    nc = L // CK
    cb = min(8, nc)
    rb = cb * CK
    nb = nc // cb
    nsub = GDR_SUB if cb % GDR_SUB == 0 else 1

    def body(q_ref, k_ref, v_ref, gb_ref, st_ref, ti_ref, do_ref, dq_ref, dk_ref, dv_ref, dgb_ref, ds_ref):
        @pl.when(pl.program_id(0) == 0)
        def _():
            ds_ref[...] = jnp.zeros_like(ds_ref)

        cst = _gdr_consts()
        causal, strict = cst["causal"], cst["strict"]
        ones = jnp.ones((CK, HD), F32)
        row = lax.broadcasted_iota(jnp.int32, (CK, HD), 0)
        lane = lax.broadcasted_iota(jnp.int32, (CK, LANES), 1)

        heads = range(H)
        rsum = lambda x: jnp.sum(x, axis=-1, keepdims=True)

        def group(gj, carry):
            gi = cb // nsub - 1 - gj
            rows = [pl.ds(pl.multiple_of((gi * nsub + j) * CK, CK), CK) for j in range(nsub)]
            chains = [(j, h) for j in range(nsub) for h in heads]
            gbt = [gb_ref[rows[j], :] for j in range(nsub)]
            cols = lambda h: slice(h * HD, (h + 1) * HD)
            q_all = [q_ref[rows[j], cols(h)] for j, h in chains]
            k_all = [k_ref[rows[j], cols(h)] for j, h in chains]
            v_all = [v_ref[rows[j], cols(h)] for j, h in chains]
            beta_all = [_head_cols(gbt[j], h)[0] for j, h in chains]
            t = _gdr_local(q_all, k_all, v_all, beta_all, [_head_cols(gbt[j], h)[1] for j, h in chains], cst,
                           tinv=[ti_ref[h, gi * nsub + j] for j, h in chains])
            ds_out = [ds_ref[h] for h in heads]
            for j in reversed(range(nsub)):
                at = lambda key: [t[key][j * H + h] for h in heads]
                pick = lambda lst: [lst[j * H + h] for h in heads]
                q, k, v, beta = pick(q_all), pick(k_all), pick(v_all), pick(beta_all)
                u, w, tinv, decay = at("u"), at("w"), at("tinv"), at("decay")
                eg, ek, cd, kb = at("eg"), at("ek"), at("cd"), at("kb")
                q_dec, k_dec, intra, pk, qk = at("q_dec"), at("k_dec"), at("intra"), at("pk"), at("qk")
                s = [st_ref[h, gi * nsub + j] for h in heads]
                dout = [do_ref[rows[j], cols(h)] for h in heads]

                ws = _dots(w, s, NN)
                v_new = [u[h] - ws[h] for h in heads]
                dq_dec = _dots(dout, s, NT)
                qd = _dots(q_dec, dout, TN)
                di = _dots(dout, v_new, NT)
                dintra = [jnp.where(causal, di[h], 0.0) for h in heads]
                ido = _dots(intra, dout, TN)
                kds = _dots(k_dec, ds_out, NN)
                dv_new = [ido[h] + kds[h] for h in heads]
                dk_dec = _dots(v_new, ds_out, NT)
                dcd = [jnp.sum(jnp.sum(ds_out[h] * s[h], axis=1, keepdims=True), axis=0, keepdims=True) for h in heads]
                dvs = _dots(dv_new, s, NT)
                dw = [-dvs[h] for h in heads]
                wdv = _dots(w, dv_new, TN)
                ds_new = [qd[h] + ds_out[h] * cd[h] - wdv[h] for h in heads]
                dru = _dots(tinv, dv_new, TN)
                drw = _dots(tinv, dw, TN)
                dl1 = _dots(dru, u, NT)
                dl2 = _dots(drw, w, NT)
                dlower = [-jnp.where(strict, dl1[h] + dl2[h], 0.0) for h in heads]
                dv = [dru[h] * beta[h] for h in heads]
                dbeta = [rsum(dru[h] * v[h]) for h in heads]
                dgc = [rsum(drw[h] * kb[h]) * eg[h] for h in heads]
                dpk = [dlower[h] * decay[h] for h in heads]
                dqk = [dintra[h] * decay[h] for h in heads]
                dpk_k = _dots(dpk, k, NN)
                dkb = [drw[h] * eg[h] + dpk_k[h] for h in heads]
                dk1 = _dots(dpk, kb, TN)
                dq1 = _dots(dqk, k, NN)
                dk2 = _dots(dqk, q, TN)
                m = [(dlower[h] * pk[h] + dintra[h] * qk[h]) * decay[h] for h in heads]
                mcol = _dots_f32(m, [ones] * H, TN, exact="b")
                e = [rsum(dk_dec[h] * k_dec[h]) for h in heads]
                dgl = [jnp.sum(e[h], axis=0, keepdims=True) + dcd[h] * cd[h] for h in heads]
                dgc = [dgc[h] + rsum(m[h]) - mcol[h] + rsum(dq_dec[h] * q_dec[h]) - e[h]
                       + jnp.where(row == CK - 1, dgl[h], 0.0) for h in heads]
                dg = _dots_f32([cst["tril"]] * H, dgc, TN, exact="a")
                dgb = jnp.zeros((CK, LANES), F32)
                for h in heads:
                    dq_ref[rows[j], cols(h)] = dq1[h] + dq_dec[h] * eg[h]
                    dk_ref[rows[j], cols(h)] = dk1[h] + dk2[h] + dk_dec[h] * ek[h] + dkb[h] * beta[h]
                    dv_ref[rows[j], cols(h)] = dv[h]
                    db = dbeta[h] + rsum(dkb[h] * k[h])
                    dgb = dgb + jnp.where(lane == h, db, 0.0) + jnp.where(lane == H + h, dg[h], 0.0)
                dgb_ref[rows[j], :] = dgb
                ds_out = ds_new
            for h in heads:
                ds_ref[h] = ds_out[h]
            return carry

        lax.fori_loop(0, cb // nsub, group, 0)

    blk = pl.BlockSpec((rb, H * HD), lambda b: (nb - 1 - b, 0))
    sblk = pl.BlockSpec((rb, LANES), lambda b: (nb - 1 - b, 0))
    return pl.pallas_call(
        body, name="gdr_bwd", grid=(nb,),
        in_specs=[blk, blk, blk, sblk, pl.BlockSpec((H, cb, HD, HD), lambda b: (0, nb - 1 - b, 0, 0)),
                  pl.BlockSpec((H, cb, CK, CK), lambda b: (0, nb - 1 - b, 0, 0)), blk],
        out_specs=[blk, blk, blk, sblk],
        out_shape=[jax.ShapeDtypeStruct((L, H * HD), F32)] * 3 + [jax.ShapeDtypeStruct((L, LANES), F32)],
        scratch_shapes=[pltpu.VMEM((H, HD, HD), F32)],
        compiler_params=_params(1),
    )(qn, kn, vs, gb, states, tinvs, do)


def _post_fwd(o, p, ya, x, modrows, sp, w_out):
    L = x.shape[0]
    T = _tile(L, 256)

    def body(o_ref, z_ref, ya_ref, x_ref, mod_ref, sp_ref, w_ref, y_ref, x2_ref, yb_ref):
        ndw = sp_ref[2:3, :]
        z = z_ref[...]
        sz = z * _sig(z)
        parts = []
        for h in range(H):
            n, _ = _rms(o_ref[:, h * HD:(h + 1) * HD])
            parts.append(n * ndw * sz[:, h * HD:(h + 1) * HD])
        yb = jnp.concatenate(parts, axis=-1).astype(MXU)
        yb_ref[...] = yb
        y = _dot(ya_ref[...], w_ref[0:AW, :], NN) + _dot(yb, w_ref[AW:2 * AW, :], NN)
        y_ref[...] = y
        x2_ref[...] = x_ref[...] + mod_ref[2:3, :] * y

    row = lambda i: (i, 0)
    zcol = (3 * AW + 3 * H * HD) // (H * HD)
    return pl.pallas_call(
        body, name="post_fwd", grid=(L // T,),
        in_specs=[pl.BlockSpec((T, H * HD), row), pl.BlockSpec((T, H * HD), lambda i: (i, zcol)),
                  pl.BlockSpec((T, AW), row), pl.BlockSpec((T, D), row), _full((SUB, D)), _full((SUB, LANES)),
                  _full((D, D))],
        out_specs=[pl.BlockSpec((T, D), row), pl.BlockSpec((T, D), row), pl.BlockSpec((T, H * HD), row)],
        out_shape=[jax.ShapeDtypeStruct((L, D), F32), jax.ShapeDtypeStruct((L, D), F32),
                   jax.ShapeDtypeStruct((L, H * HD), MXU)],
        compiler_params=_params(1),
    )(o, p, ya, x, modrows, sp, w_out)


FF_COLS = 2
FF_CW = DFF // FF_COLS
FF_ROWS = 256


def _ffn_fwd_half(x2, modrows, vec, w_up, cff, w_down, j, d_prev):
    assert FF_COLS == 2
    L = x2.shape[0]
    T = _tile(L, FF_ROWS)
    nj = FF_COLS
    last = d_prev is not None

    def body(*refs):
        x_ref, mod_ref, vec_ref, wg_ref, wu_ref, cg_ref, cu_ref, wd_ref = refs[:8]
        if last:
            dp_ref, gp_ref, up_ref, f_ref, d_ref, x3_ref, carry_g, carry_u = refs[8:]
        else:
            h_ref, gp_ref, up_ref, f_ref, d_ref, carry_g, carry_u = refs[8:]

        @pl.when(pl.program_id(0) == 0)
        def _():
            carry_g[...] = jnp.zeros_like(carry_g)
            carry_u[...] = jnp.zeros_like(carry_u)

        xv = x_ref[...]
        n, _ = _rms(xv)
        hb = (n * vec_ref[1:2, :] * (1.0 + mod_ref[4:5, :]) + mod_ref[3:4, :]).astype(MXU)
        if not last:
            h_ref[...] = hb
        g = _dot(hb, wg_ref[...], NN)
        u = _dot(hb, wu_ref[...], NN)
        gp_ref[...] = g
        up_ref[...] = u
        gc, _ = _conv_fwd(g, cg_ref, 3, carry_g[...])
        uc, _ = _conv_fwd(u, cu_ref, 3, carry_u[...])
        carry_g[...] = g[T - SUB:T, :]
        carry_u[...] = u[T - SUB:T, :]
        fb = (gc * _sig(gc) * uc).astype(MXU)
        f_ref[...] = fb
        part = _dot(fb, wd_ref[...], NN)
        if last:
            dv = dp_ref[...] + part
            d_ref[...] = dv
            x3_ref[...] = xv + mod_ref[5:6, :] * dv
        else:
            d_ref[...] = part

    row = lambda i: (i, 0)
    rowD = pl.BlockSpec((T, D), row)
    rowC = pl.BlockSpec((T, FF_CW), row)
    in_specs = [rowD, _full((SUB, D)), _full((SUB, D)),
                pl.BlockSpec((D, FF_CW), lambda i: (0, j)), pl.BlockSpec((D, FF_CW), lambda i: (0, nj + j)),
                pl.BlockSpec((SUB, FF_CW), lambda i: (0, j)), pl.BlockSpec((SUB, FF_CW), lambda i: (0, nj + j)),
                pl.BlockSpec((FF_CW, D), lambda i: (j, 0))]
    half = [jax.ShapeDtypeStruct((L, FF_CW), F32), jax.ShapeDtypeStruct((L, FF_CW), F32),
            jax.ShapeDtypeStruct((L, FF_CW), MXU)]
    args = [x2, modrows, vec, w_up, w_up, cff, cff, w_down]
    if last:
        in_specs.append(rowD)
        args.append(d_prev)
        out_specs = [rowC, rowC, rowC, rowD, rowD]
        out_shape = half + [jax.ShapeDtypeStruct((L, D), F32), jax.ShapeDtypeStruct((L, D), F32)]
    else:
        out_specs = [rowD, rowC, rowC, rowC, rowD]
        out_shape = [jax.ShapeDtypeStruct((L, D), MXU)] + half + [jax.ShapeDtypeStruct((L, D), F32)]
    return pl.pallas_call(
        body, name="ffn_fwd_last" if last else "ffn_fwd_first", grid=(L // T,),
        in_specs=in_specs, out_specs=out_specs, out_shape=out_shape,
        scratch_shapes=[pltpu.VMEM((SUB, FF_CW), F32), pltpu.VMEM((SUB, FF_CW), F32)],
        compiler_params=_params(1),
    )(*args)


def _ffn_bwd_half(dx3, modrows, gpre, upre, cff, w_down, w_up, j, tail):
    assert FF_COLS == 2
    L = dx3.shape[0]
    T = _tile(L, FF_ROWS)
    ni, nj = L // T, FF_COLS
    hb_per_t = T // SUB
    last = tail is not None

    def body(*refs):
        dx3_ref, mod_ref, gp_ref, up_ref, gph_ref, uph_ref, cg_ref, cu_ref, wd_ref, wg_ref, wu_ref = refs[:11]
        if last:
            (d_ref, x2_ref, vec_ref, dhp_ref, dgp_ref, dup_ref, dx2_ref, accv_ref, dcg_ref, dcu_ref,
             carry_g, carry_u) = refs[11:]
        else:
            dd_ref, dgp_ref, dup_ref, dh_ref, dcg_ref, dcu_ref, carry_g, carry_u = refs[11:]
        i = pl.program_id(0)
        ri = ni - 1 - i

        @pl.when(i == 0)
        def _():
            carry_g[...] = jnp.zeros_like(carry_g)
            carry_u[...] = jnp.zeros_like(carry_u)
            dcg_ref[...] = jnp.zeros_like(dcg_ref)
            dcu_ref[...] = jnp.zeros_like(dcu_ref)
            if last:
                accv_ref[...] = jnp.zeros_like(accv_ref)

        dx3v = dx3_ref[...]
        ddb = (mod_ref[5:6, :] * dx3v).astype(MXU)
        if not last:
            dd_ref[...] = ddb
        g, u = gp_ref[...], up_ref[...]
        keep = jnp.where(ri == 0, 0.0, 1.0)
        gc, gsh = _conv_fwd(g, cg_ref, 3, gph_ref[...] * keep)
        uc, ush = _conv_fwd(u, cu_ref, 3, uph_ref[...] * keep)
        sg = _sig(gc)
        df = _dot(ddb, wd_ref[...], NT)
        duc = df * (gc * sg)
        dgc = df * uc * (sg * (1.0 + gc * (1.0 - sg)))
        for s in range(3):
            dcg_ref[2 - s:3 - s, :] += _sum0(dgc * gsh[s])
            dcu_ref[2 - s:3 - s, :] += _sum0(duc * ush[s])
        dg = _conv_bwd_in(dgc, cg_ref, 3, carry_g[...]).astype(MXU)
        du = _conv_bwd_in(duc, cu_ref, 3, carry_u[...]).astype(MXU)
        carry_g[...] = dgc[0:SUB, :]
        carry_u[...] = duc[0:SUB, :]
        dgp_ref[...] = dg
        dup_ref[...] = du
        dh = _dot(dg, wg_ref[...], NT) + _dot(du, wu_ref[...], NT)
        if last:
            dh = dh + dhp_ref[...]
            accv_ref[0:1, :] += _sum0(dx3v * d_ref[...])
            n, r = _rms(x2_ref[...])
            nw, sc = vec_ref[1:2, :], mod_ref[4:5, :]
            accv_ref[1:2, :] += _sum0(dh)
            accv_ref[2:3, :] += _sum0(dh * n * nw)
            accv_ref[3:4, :] += _sum0(dh * n * (1.0 + sc))
            dx2_ref[...] = _rms_bwd(dh * nw * (1.0 + sc), n, r) + dx3v
        else:
            dh_ref[...] = dh

    row = lambda i: (ni - 1 - i, 0)
    halo = lambda i: (jnp.maximum((ni - 1 - i) * hb_per_t - 1, 0), 0)
    rowD = pl.BlockSpec((T, D), row)
    rowC = pl.BlockSpec((T, FF_CW), row)
    haloC = pl.BlockSpec((SUB, FF_CW), halo)
    in_specs = [rowD, _full((SUB, D)), rowC, rowC, haloC, haloC,
                pl.BlockSpec((SUB, FF_CW), lambda i: (0, j)), pl.BlockSpec((SUB, FF_CW), lambda i: (0, nj + j)),
                pl.BlockSpec((FF_CW, D), lambda i: (j, 0)),
                pl.BlockSpec((D, FF_CW), lambda i: (0, j)), pl.BlockSpec((D, FF_CW), lambda i: (0, nj + j))]
    args = [dx3, modrows, gpre, upre, gpre, upre, cff, cff, w_down, w_up, w_up]
    halfb = [jax.ShapeDtypeStruct((L, FF_CW), MXU), jax.ShapeDtypeStruct((L, FF_CW), MXU)]
    dconv = [jax.ShapeDtypeStruct((SUB, FF_CW), F32)] * 2
    if last:
        d, x2, vec, dh_prev = tail
        in_specs += [rowD, rowD, _full((SUB, D)), rowD]
        args += [d, x2, vec, dh_prev]
        out_specs = [rowC, rowC, rowD, _full((SUB, D)), _full((SUB, FF_CW)), _full((SUB, FF_CW))]
        out_shape = halfb + [jax.ShapeDtypeStruct((L, D), F32), jax.ShapeDtypeStruct((SUB, D), F32)] + dconv
    else:
        out_specs = [rowD, rowC, rowC, rowD, _full((SUB, FF_CW)), _full((SUB, FF_CW))]
        out_shape = [jax.ShapeDtypeStruct((L, D), MXU)] + halfb + [jax.ShapeDtypeStruct((L, D), F32)] + dconv
    return pl.pallas_call(
        body, name="ffn_bwd_last" if last else "ffn_bwd_first", grid=(ni,),
        in_specs=in_specs, out_specs=out_specs, out_shape=out_shape,
        scratch_shapes=[pltpu.VMEM((SUB, FF_CW), F32), pltpu.VMEM((SUB, FF_CW), F32)],
        compiler_params=_params(1),
    )(*args)


def _final(x, target, nf):
    L = x.shape[0]
    T = _tile(L, 256)

    def body(x_ref, t_ref, nf_ref, dx_ref, acc_ref):
        @pl.when(pl.program_id(0) == 0)
        def _():
            acc_ref[...] = jnp.zeros_like(acc_ref)

        n, r = _rms(x_ref[...])
        w = nf_ref[0:1, :]
        err = n * w - t_ref[...]
        acc_ref[0:1, :] += (0.5 / D) * _sum0(err * err)
        dy = err * (1.0 / D)
        acc_ref[1:2, :] += _sum0(dy * n)
        dx_ref[...] = _rms_bwd(dy * w, n, r)

    row = lambda i: (i, 0)
    return pl.pallas_call(
        body, name="final_norm_loss", grid=(L // T,),
        in_specs=[pl.BlockSpec((T, D), row), pl.BlockSpec((T, D), row), _full((SUB, D))],
        out_specs=[pl.BlockSpec((T, D), row), _full((SUB, D))],
        out_shape=[jax.ShapeDtypeStruct((L, D), F32), jax.ShapeDtypeStruct((SUB, D), F32)],
        compiler_params=_params(1),
    )(x, target, nf)


def _post_bwd(dx2, y, o, p, modrows, sp, w_out):
    L = dx2.shape[0]
    T = _tile(L, 256)

    def body(dx2_ref, y_ref, o_ref, z_ref, mod_ref, sp_ref, w_ref, dy_ref, do_ref, dz_ref, dya_ref, accv_ref, accs_ref):
        @pl.when(pl.program_id(0) == 0)
        def _():
            accv_ref[...] = jnp.zeros_like(accv_ref)
            accs_ref[...] = jnp.zeros_like(accs_ref)

        dx2v = dx2_ref[...]
        accv_ref[0:1, :] += _sum0(dx2v * y_ref[...])
        dyb = (mod_ref[2:3, :] * dx2v).astype(MXU)
        dy_ref[...] = dyb
        dyc = _dot(dyb, w_ref[...], NT)
        dya_ref[...] = dyc[:, 0:AW]
        ndw = sp_ref[2:3, :]
        z = z_ref[...]
        sgz = _sig(z)
        dsz = sgz * (1.0 + z * (1.0 - sgz))
        dndw = jnp.zeros((1, HD), F32)
        for h in range(H):
            sl = slice(h * HD, (h + 1) * HD)
            n, r = _rms(o_ref[:, sl])
            dyh = dyc[:, AW + h * HD:AW + (h + 1) * HD]
            zh = z[:, sl]
            don = dyh * (zh * sgz[:, sl])
            dz_ref[:, sl] = dyh * (n * ndw) * dsz[:, sl]
            dndw = dndw + _sum0(don * n)
            do_ref[:, sl] = _rms_bwd(don * ndw, n, r)
        accs_ref[0:1, :] += dndw

    row = lambda i: (i, 0)
    zcol = (3 * AW + 3 * H * HD) // (H * HD)
    return pl.pallas_call(
        body, name="post_bwd", grid=(L // T,),
        in_specs=[pl.BlockSpec((T, D), row), pl.BlockSpec((T, D), row), pl.BlockSpec((T, H * HD), row),
                  pl.BlockSpec((T, H * HD), lambda i: (i, zcol)), _full((SUB, D)), _full((SUB, LANES)), _full((D, D))],
        out_specs=[pl.BlockSpec((T, D), row)] + [pl.BlockSpec((T, H * HD), row)] * 3 + [_full((SUB, D)), _full((SUB, LANES))],
        out_shape=[jax.ShapeDtypeStruct((L, D), MXU)] + [jax.ShapeDtypeStruct((L, H * HD), F32)] * 3
        + [jax.ShapeDtypeStruct((SUB, D), F32), jax.ShapeDtypeStruct((SUB, LANES), F32)],
        compiler_params=_params(1),
    )(dx2, y, o, p, modrows, sp, w_out)


def _pre_bwd(p, dqn, dkn, dvs, dya, dz, dgb, pa, cq, sp):
    L = p.shape[0]
    T = _tile(L, 256)
    ni = L // T
    scale = HD ** -0.5
    w3 = 3 * AW + 3 * H * HD
    hb_per_t = T // SUB

    def body(pm_ref, ph_ref, ps_ref, dq_ref, dk_ref, dv_ref, dya_ref, dz_ref, dgb_ref, pa_ref, cq_ref, sp_ref,
             dp_ref, dpa_ref, dcq_ref, dsp_ref, carry_u, carry_q):
        i = pl.program_id(0)
        ri = ni - 1 - i

        @pl.when(i == 0)
        def _():
            dpa_ref[...] = jnp.zeros_like(dpa_ref)
            dcq_ref[...] = jnp.zeros_like(dcq_ref)
            dsp_ref[...] = jnp.zeros_like(dsp_ref)
            carry_u[...] = jnp.zeros_like(carry_u)
            carry_q[...] = jnp.zeros_like(carry_q)

        keep = jnp.where(ri == 0, 0.0, 1.0)
        a_b, a_c, a_x = pm_ref[:, 0:AW], pm_ref[:, AW:2 * AW], pm_ref[:, 2 * AW:3 * AW]
        u = a_c * a_x
        hu = ph_ref[:, AW:2 * AW] * ph_ref[:, 2 * AW:3 * AW] * keep
        cu, ush = _conv_fwd(u, pa_ref, 3, hu)
        yp = a_b * cu
        bd = _blockdiag_mean(AW, A_GROUP)
        ra = lax.rsqrt(_dot_f32(yp * yp, bd, NN, exact="b") + EPS)
        na = yp * ra
        dya = dya_ref[...]
        dpa_ref[3:4, :] += _sum0(dya * na)
        dna = dya * pa_ref[3:4, :]
        dyp = ra * (dna - na * _dot_f32(dna * na, bd, NN, exact="b"))
        dcu = dyp * a_b
        for s in range(3):
            dpa_ref[2 - s:3 - s, :] += _sum0(dcu * ush[s])
        du = _conv_bwd_in(dcu, pa_ref, 3, carry_u[...])
        carry_u[...] = dcu[0:SUB, :]
        dp_ref[:, 0:AW] = (dyp * cu).astype(MXU)
        dp_ref[:, AW:2 * AW] = (du * a_x).astype(MXU)
        dp_ref[:, 2 * AW:3 * AW] = (du * a_c).astype(MXU)

        qkv = pm_ref[:, 3 * AW:w3]
        qc, qsh = _conv_fwd(qkv, cq_ref, 4, ph_ref[:, 3 * AW:w3] * keep)
        sg = _sig(qc)
        qs = qc * sg
        parts = []
        for h in range(H):
            q = qs[:, h * HD:(h + 1) * HD]
            rq = lax.rsqrt(jnp.sum(q * q, axis=-1, keepdims=True) + EPS)
            parts.append(_l2_bwd(dq_ref[:, h * HD:(h + 1) * HD] * scale, q * rq, rq))
        for h in range(H):
            k = qs[:, (H + h) * HD:(H + h + 1) * HD]
            rk = lax.rsqrt(jnp.sum(k * k, axis=-1, keepdims=True) + EPS)
            parts.append(_l2_bwd(dk_ref[:, h * HD:(h + 1) * HD], k * rk, rk))
        parts.append(dv_ref[...])
        dqc = jnp.concatenate(parts, axis=-1) * (sg * (1.0 + qc * (1.0 - sg)))
        for s in range(4):
            dcq_ref[3 - s:4 - s, :] += _sum0(dqc * qsh[s])
        dp_ref[:, 3 * AW:w3] = _conv_bwd_in(dqc, cq_ref, 4, carry_q[...]).astype(MXU)
        carry_q[...] = dqc[0:SUB, :]
        dp_ref[:, w3:w3 + H * HD] = dz_ref[...].astype(MXU)

        lane, a, xb, beta, g = _gate_small(ps_ref[...], sp_ref)
        dgb = dgb_ref[...]
        dbeta = jnp.where(lane < H, dgb, 0.0)
        dg = jnp.where((lane >= H) & (lane < 2 * H), dgb, 0.0)
        dalpha = dg * a * _sig(xb)
        dsp_ref[0:1, :] += _sum0(dg * g)
        dsp_ref[1:2, :] += _sum0(dalpha)
        dp_ref[:, w3 + H * HD:P_PAD] = (dbeta * beta * (1.0 - beta) + dalpha).astype(MXU)

    row = lambda i: (ni - 1 - i, 0)
    halo = lambda i: (jnp.maximum((ni - 1 - i) * hb_per_t - 1, 0), 0)
    hrow = pl.BlockSpec((T, H * HD), row)
    return pl.pallas_call(
        body, name="pre_bwd", grid=(ni,),
        in_specs=[pl.BlockSpec((T, w3), row), pl.BlockSpec((SUB, w3), halo),
                  pl.BlockSpec((T, LANES), lambda i: (ni - 1 - i, (P_PAD - LANES) // LANES)),
                  hrow, hrow, hrow, pl.BlockSpec((T, AW), row), hrow,
                  pl.BlockSpec((T, LANES), row),
                  _full((SUB, AW)), _full((SUB, 3 * H * HD)), _full((SUB, LANES))],
        out_specs=[pl.BlockSpec((T, P_PAD), row), _full((SUB, AW)), _full((SUB, 3 * H * HD)), _full((SUB, LANES))],
        out_shape=[jax.ShapeDtypeStruct((L, P_PAD), MXU), jax.ShapeDtypeStruct((SUB, AW), F32),
                   jax.ShapeDtypeStruct((SUB, 3 * H * HD), F32), jax.ShapeDtypeStruct((SUB, LANES), F32)],
        scratch_shapes=[pltpu.VMEM((SUB, AW), F32), pltpu.VMEM((SUB, 3 * H * HD), F32)],
        compiler_params=_params(1),
    )(p, p, p, dqn, dkn, dvs, dya, dz, dgb, pa, cq, sp)


def _in_bwd(dp, w_in, x, dx2, modrows, vec):
    L = x.shape[0]
    T = _tile(L, 256)

    def body(dp_ref, w_ref, x_ref, dx2_ref, mod_ref, vec_ref, dx_ref, accv_ref):
        @pl.when(pl.program_id(0) == 0)
        def _():
            accv_ref[...] = jnp.zeros_like(accv_ref)

        dh = _dot(dp_ref[...], w_ref[...], NT)
        n, r = _rms(x_ref[...])
        nw, sc = vec_ref[0:1, :], mod_ref[1:2, :]
        accv_ref[0:1, :] += _sum0(dh)
        accv_ref[1:2, :] += _sum0(dh * n * nw)
        accv_ref[2:3, :] += _sum0(dh * n * (1.0 + sc))
        dx_ref[...] = _rms_bwd(dh * nw * (1.0 + sc), n, r) + dx2_ref[...]

    row = lambda i: (i, 0)
    return pl.pallas_call(
        body, name="in_bwd", grid=(L // T,),
        in_specs=[pl.BlockSpec((T, P_PAD), row), _full((D, P_PAD)), pl.BlockSpec((T, D), row),
                  pl.BlockSpec((T, D), row), _full((SUB, D)), _full((SUB, D))],
        out_specs=[pl.BlockSpec((T, D), row), _full((SUB, D))],
        out_shape=[jax.ShapeDtypeStruct((L, D), F32), jax.ShapeDtypeStruct((SUB, D), F32)],
        compiler_params=_params(1),
    )(dp, w_in, x, dx2, modrows, vec)


def _wgrad(a, b, tm, tn, name):
    L, m = a.shape
    n = b.shape[1]
    tl = _tile(L, 512)
    tm, tn = _tile(m, tm), _tile(n, tn)
    nl = L // tl

    def body(a_ref, b_ref, o_ref):
        @pl.when(pl.program_id(2) == 0)
        def _():
            o_ref[...] = jnp.zeros_like(o_ref)

        o_ref[...] += _dot(a_ref[...], b_ref[...], TN)

    return pl.pallas_call(
        body, name=name, grid=(m // tm, n // tn, nl),
        in_specs=[pl.BlockSpec((tl, tm), lambda i, j, l: (l, i)), pl.BlockSpec((tl, tn), lambda i, j, l: (l, j))],
        out_specs=pl.BlockSpec((tm, tn), lambda i, j, l: (i, j)),
        out_shape=jax.ShapeDtypeStruct((m, n), F32), compiler_params=_params(3),
    )(a, b)


def _wgrad_cols(a, b, tm, n_shard, wpad, count, name):
    L, m = a.shape
    n = b.shape[1]
    tl = _tile(L, 512)
    tm = _tile(m, tm)
    nl = L // tl
    wins = _shard_windows(n_shard, count)
    assert all(a_ * LANES + win <= n for a_, _, win in wins), (wins, n)

    def body(a_ref, b_ref, o_ref, acc):
        @pl.when(pl.program_id(1) == 0)
        def _():
            acc[...] = jnp.zeros_like(acc)

        acc[...] += _dot(a_ref[...], b_ref[...], TN)

        @pl.when(pl.program_id(1) == nl - 1)
        def _():
            for k, (a_, s, win) in enumerate(wins):
                xk = acc[:, a_ * LANES:a_ * LANES + win]
                if s:
                    xk = pltpu.roll(xk, win - s, 1)
                o_ref[k] = _fit_lanes(xk, wpad)

    return pl.pallas_call(
        body, name=name, grid=(m // tm, nl),
        in_specs=[pl.BlockSpec((tl, tm), lambda i, l: (l, i)), pl.BlockSpec((tl, n), lambda i, l: (l, 0))],
        out_specs=pl.BlockSpec((count, tm, wpad), lambda i, l: (0, i, 0)),
        out_shape=jax.ShapeDtypeStruct((count, m, wpad), F32),
        scratch_shapes=[pltpu.VMEM((tm, n), F32)],
        compiler_params=_params(2),
    )(a, b)


def _adamw(w, g, m, v, name):
    r, n = w.shape
    tr = _tile(r, 512)
    bc1 = 1.0 - ADAM_B1 ** ADAM_STEP
    bc2 = 1.0 - ADAM_B2 ** ADAM_STEP

    def body(w_ref, g_ref, m_ref, v_ref, d_ref, nm_ref, nv_ref):
        gv = g_ref[...]
        nm = ADAM_B1 * m_ref[...] + (1.0 - ADAM_B1) * gv
        nv = ADAM_B2 * v_ref[...] + (1.0 - ADAM_B2) * (gv * gv)
        nm_ref[...] = nm
        nv_ref[...] = nv
        d_ref[...] = -ADAM_LR * ((nm / bc1) / (jnp.sqrt(nv / bc2) + ADAM_EPS) + ADAM_WD * w_ref[...])

    spec = pl.BlockSpec((tr, n), lambda i: (i, 0))
    return pl.pallas_call(
        body, name=name, grid=(r // tr,), in_specs=[spec] * 4, out_specs=[spec] * 3,
        out_shape=[jax.ShapeDtypeStruct((r, n), F32)] * 3, compiler_params=_params(1),
    )(w, g, m, v)


def _rows8(rows, width):
    out = jnp.zeros((SUB, width), F32)
    for r, vrow in enumerate(rows):
        out = out.at[r, :vrow.shape[0]].set(vrow)
    return out


def _at_lanes(v4, start):
    return jnp.zeros((LANES,), F32).at[start:start + v4.shape[0]].set(v4)


def _pad_rows(flat, mult):
    n = flat.shape[0]
    pad = (-n) % mult
    return jnp.pad(flat, (0, pad)) if pad else flat


IN_PAD = 512
UP_PAD = 768


def _local_fwd_bwd(x, target, mod_full, small_w, full_w, on_grads=None):
    norm1_w, norm2_w, norm_a_w, a_log, dt_bias, norm_dn_w, norm_f_w = small_w
    w_in_f, w_out_f, w_up_f, w_down_f, conv_a_f, conv_q_f, conv_f_f = full_w

    def layer_params(i):
        modrows = jnp.concatenate([mod_full[i], jnp.zeros((SUB - N_MOD, D), F32)], axis=0)
        vec = _rows8([norm1_w[i], norm2_w[i]], D)
        pa = _rows8([conv_a_f[i, 0], conv_a_f[i, 1], conv_a_f[i, 2], norm_a_w[i]], AW)
        cq = _rows8([conv_q_f[i, k] for k in range(4)], 3 * H * HD)
        sp = _rows8([_at_lanes(a_log[i], H), _at_lanes(dt_bias[i], H), norm_dn_w[i]], LANES)
        cff = _rows8([conv_f_f[i, k] for k in range(3)], 2 * DFF)
        return modrows, vec, pa, cq, sp, cff

    saved = []
    xi = x
    for i in range(DEPTH):
        modrows, vec, pa, cq, sp, cff = layer_params(i)
        p, h1 = _in_proj(xi, modrows, vec, w_in_f[i])
        qn, kn, vs, gb, ya = _pre_fwd(p, pa, cq, sp)
        o, states, tinvs = _gdr_fwd(qn, kn, vs, gb)
        y, x2, yb = _post_fwd(o, p, ya, xi, modrows, sp, w_out_f[i])
        h2, gp0, up0, f0, d0 = _ffn_fwd_half(x2, modrows, vec, w_up_f[i], cff, w_down_f[i], 0, None)
        gp1, up1, f1, dff, x3 = _ffn_fwd_half(x2, modrows, vec, w_up_f[i], cff, w_down_f[i], 1, d0)
        saved.append(dict(x=xi, p=p, h1=h1, qn=qn, kn=kn, vs=vs, gb=gb, ya=ya, o=o, states=states, tinvs=tinvs, y=y, x2=x2, yb=yb,
                          h2=h2, gpre=(gp0, gp1), upre=(up0, up1), f=(f0, f1), d=dff))
        xi = x3

    dx, facc = _final(xi, target, _rows8([norm_f_w], D))
    loss_local = jnp.sum(facc[0])
    d_norm_f = facc[1]

    gw_in, gw_out, gw_up, gw_down = [None] * DEPTH, [None] * DEPTH, [None] * DEPTH, [None] * DEPTH
    g_small = [None] * DEPTH
    for i in reversed(range(DEPTH)):
        s = saved[i]
        modrows, vec, pa, cq, sp, cff = layer_params(i)
        dd, dgp0, dup0, dh0, dcg0, dcu0 = _ffn_bwd_half(dx, modrows, s["gpre"][0], s["upre"][0], cff,
                                                        w_down_f[i], w_up_f[i], 0, None)
        dgp1, dup1, dx2, accf, dcg1, dcu1 = _ffn_bwd_half(dx, modrows, s["gpre"][1], s["upre"][1], cff,
                                                          w_down_f[i], w_up_f[i], 1, (s["d"], s["x2"], vec, dh0))
        n_up, up_pad = 2 * DFF // N_DEV, UP_PAD
        gw_up[i] = jnp.concatenate([_wgrad_cols(s["h2"], t, 1024, n_up, up_pad, FF_CW // n_up, "wgrad_up")
                                    for t in (dgp0, dgp1, dup0, dup1)], axis=0)
        gw_down[i] = jnp.concatenate([_wgrad(s["f"][0], dd, FF_CW, 1024, "wgrad_down"),
                                      _wgrad(s["f"][1], dd, FF_CW, 1024, "wgrad_down")],
                                     axis=0).reshape(N_DEV, DFF // N_DEV, D)
        dy, do, dz, dya, accp, accs = _post_bwd(dx2, s["y"], s["o"], s["p"], modrows, sp, w_out_f[i])
        gw_out[i] = jnp.concatenate([_wgrad(s["ya"], dy, 512, 1024, "wgrad_out"),
                                     _wgrad(s["yb"], dy, 512, 1024, "wgrad_out")], axis=0).reshape(N_DEV, D // N_DEV, D)
        dqn, dkn, dvs, dgb = _gdr_bwd(s["qn"], s["kn"], s["vs"], s["gb"], s["states"], s["tinvs"], do)
        dp, dpa, dcq, dsp = _pre_bwd(s["p"], dqn, dkn, dvs, dya, dz, dgb, pa, cq, sp)
        gw_in[i] = _wgrad_cols(s["h1"], dp, 512, P_IN // N_DEV, IN_PAD, N_DEV, "wgrad_in")
        dx, acci = _in_bwd(dp, w_in_f[i], s["x"], dx2, modrows, vec)
        dconv_ff = jnp.concatenate([dcg0, dcg1, dcu0, dcu1], axis=1)[0:3]
        dmod = jnp.stack([acci[0], acci[1], accp[0], accf[1], accf[2], accf[0]])
        g_small[i] = dict(norm1=acci[2], norm2=accf[3], norm_a=dpa[3], a_log=dsp[0, H:2 * H], dt_bias=dsp[1, H:2 * H],
                          norm_dn=accs[0], conv_a=dpa[0:3], conv_qkv=dcq[0:4], conv_ff=dconv_ff, dmod=dmod.reshape(-1))
        if on_grads is not None:
            on_grads(i, [gw_in[i], gw_out[i], gw_up[i], gw_down[i]])
    return loss_local, dx, gw_in, gw_out, gw_up, gw_down, g_small, d_norm_f


def kernel(x, c, ada_w, ada_b, norm1_w, w_in, conv_a_w, norm_a_w, conv_qkv_w, a_log, dt_bias, norm_dn_w, w_out, norm2_w, w_up, conv_ff_w, w_down, norm_f_w, loss_target, m_ada_w, m_ada_b, m_norm1_w, m_w_in, m_conv_a_w, m_norm_a_w, m_conv_qkv_w, m_a_log, m_dt_bias, m_norm_dn_w, m_w_out, m_norm2_w, m_w_up, m_conv_ff_w, m_w_down, m_norm_f_w, v_ada_w, v_ada_b, v_norm1_w, v_w_in, v_conv_a_w, v_norm_a_w, v_conv_qkv_w, v_a_log, v_dt_bias, v_norm_dn_w, v_w_out, v_norm2_w, v_w_up, v_conv_ff_w, v_w_down, v_norm_f_w):
    ax, ay, ac = lax.axis_index("x"), lax.axis_index("y"), lax.axis_index("c")
    me = 4 * ax + 2 * ay + ac
    x = x[0]
    target = loss_target[0]
    n_in, n_up = P_IN // N_DEV, 2 * DFF // N_DEV

    def lane_pad(t, width):
        return jnp.pad(t.astype(MXU), ((0, 0), (0, 0), (0, width - t.shape[-1])))

    conv_blob = _pad_rows(jnp.concatenate([t.reshape(-1) for t in (conv_a_w, conv_qkv_w, conv_ff_w)]),
                          SUB * LANES).reshape(-1, LANES)
    c_rows = jnp.zeros((SUB, D), F32).at[0].set(c[0])
    send = [lane_pad(w_in, IN_PAD), w_out.astype(MXU), lane_pad(w_up, UP_PAD), w_down.astype(MXU)]
    got = [None] * DEPTH
    *got[0], g_conv, g_c = _all_gather([t[0] for t in send] + [conv_blob, c_rows], "gather_weights", in_vmem=False)
    for i in range(1, DEPTH):
        shards, _ = lax.optimization_barrier(([t[i] for t in send], g_c))
        got[i] = _all_gather_async(shards, "gather_weights_l%d" % i, collective_id=i)
    w_in_f = [_interleave_cols(g[0][:, None], n_in, P_PAD, "interleave_w_in")[0] for g in got]
    w_up_f = [_interleave_cols(g[2][:, None], n_up, 2 * DFF, "interleave_w_up")[0] for g in got]
    w_out_f = [g[1].reshape(D, D) for g in got]
    w_down_f = [g[3].reshape(DFF, D) for g in got]
    sg = g_conv.reshape(N_DEV, -1)
    o1 = conv_a_w.size
    o2 = o1 + conv_qkv_w.size
    o3 = o2 + conv_ff_w.size
    conv_a_f = sg[:, 0:o1].reshape(N_DEV, DEPTH, 3, AW // N_DEV).transpose(1, 2, 0, 3).reshape(DEPTH, 3, AW)
    conv_q_f = sg[:, o1:o2].reshape(N_DEV, DEPTH, 4, 3 * H * HD // N_DEV).transpose(1, 2, 0, 3).reshape(DEPTH, 4, 3 * H * HD)
    conv_f_f = sg[:, o2:o3].reshape(N_DEV, DEPTH, 3, n_up).transpose(1, 2, 0, 3).reshape(DEPTH, 3, 2 * DFF)

    c_all = jnp.concatenate([g_c[:, 0], jnp.zeros((16 - N_DEV, D), F32)], axis=0)
    n_ada = N_MOD * D // N_DEV
    ada_b_cols = lax.dynamic_slice_in_dim(ada_b, me * n_ada, n_ada, axis=1)[:, None, :]
    mod_sh = _mod_fwd(c_all, ada_w, ada_b_cols)
    mod_all = _all_gather([mod_sh.reshape(DEPTH * 16, n_ada)], "gather_mod", in_vmem=True)[0]
    mod_all = mod_all.reshape(N_DEV, DEPTH, 16, n_ada)
    mod_mine = lax.dynamic_index_in_dim(mod_all, me, axis=2, keepdims=False)
    mod_full = mod_mine.transpose(1, 0, 2).reshape(DEPTH, N_MOD, D)

    tags = ["w_in", "w_out", "w_up", "w_down"]
    my_c = jnp.reshape(ac, (1,)).astype(jnp.int32)
    my_chip = jnp.reshape(2 * ax + ay, (1,)).astype(jnp.int32)
    exchanged, staged = {}, [None] * DEPTH

    def rs_pairs_and_chips(i):
        gs_i, recv1 = exchanged.pop(i)
        pairs = [_rs_add_pairs(g[:, None], r[:, None], my_c, "rs_add_pairs_" + t) for g, r, t in zip(gs_i, recv1, tags)]
        recv2 = _rs_async([pb[:, 0] for _, pb in pairs], 3, False, "rs_chips_l%d" % i, collective_id=2 * DEPTH + i)
        staged[i] = ([pf for pf, _ in pairs], recv2)

    def on_grads(i, gs_i):
        recv1 = _rs_async(gs_i, 4, True, "rs_sibling_l%d" % i, collective_id=DEPTH + i)
        if i + 1 in exchanged:
            rs_pairs_and_chips(i + 1)
        exchanged[i] = (gs_i, recv1)

    loss_local, dx, _, _, _, _, g_small, d_norm_f = _local_fwd_bwd(
        x, target, mod_full, (norm1_w, norm2_w, norm_a_w, a_log, dt_bias, norm_dn_w, norm_f_w),
        (w_in_f, w_out_f, w_up_f, w_down_f, conv_a_f, conv_q_f, conv_f_f), on_grads)
    rs_pairs_and_chips(0)
    loss = lax.psum(loss_local, ("x", "y", "c"))
    grad_x = dx[None]

    keys = ["dmod", "norm1", "norm2", "norm_a", "a_log", "dt_bias", "norm_dn", "conv_a", "conv_qkv", "conv_ff"]
    stacked = {k: jnp.stack([g_small[i][k] for i in range(DEPTH)]) for k in keys}
    flat_parts = [stacked[k].reshape(-1) for k in keys] + [d_norm_f]
    sizes = [int(t.shape[0]) for t in flat_parts]
    sflat = _pad_rows(jnp.concatenate(flat_parts), SUB * LANES).reshape(-1, LANES)
    sall = _all_gather([sflat], "gather_small_grads", in_vmem=True)[0]
    ssum = _sum_devices(sall).reshape(-1)
    so = [0]
    for sz in sizes:
        so.append(so[-1] + sz)
    red = {k: ssum[so[n]:so[n + 1]].reshape(stacked[k].shape) for n, k in enumerate(keys)}
    g_norm_f = ssum[so[len(keys)]:so[len(keys) + 1]]
    dmod_all = sall[:, 0:sizes[0] // LANES, :].reshape(N_DEV, DEPTH, N_MOD * D)

    g_ada_b = red["dmod"].reshape(DEPTH, N_MOD * D)
    dmod_cols = lax.dynamic_slice_in_dim(dmod_all, me * n_ada, n_ada, axis=2).transpose(1, 0, 2)
    dmod_cols = jnp.concatenate([dmod_cols, jnp.zeros((DEPTH, 16 - N_DEV, n_ada), F32)], axis=1)
    g_ada_w = _mod_bwd(c_all, dmod_cols)
    g_conv_a = lax.dynamic_slice_in_dim(red["conv_a"], me * (AW // N_DEV), AW // N_DEV, axis=2)
    g_conv_qkv = lax.dynamic_slice_in_dim(red["conv_qkv"], me * (3 * H * HD // N_DEV), 3 * H * HD // N_DEV, axis=2)
    g_conv_ff = lax.dynamic_slice_in_dim(red["conv_ff"], me * n_up, n_up, axis=2)

    mine = [jnp.concatenate([_rs_add_chips(staged[i][0][k], staged[i][1][k][:, None], my_chip, "rs_add_chips_" + t)
                             for i in range(DEPTH)], axis=0) for k, t in enumerate(tags)]
    g_w_in = mine[0][:, :, :n_in]
    g_w_out = mine[1]
    g_w_up = mine[2][:, :, :n_up]
    g_w_down = mine[3]

    grads = dict(ada_w=g_ada_w, ada_b=g_ada_b, norm1_w=red["norm1"], w_in=g_w_in, conv_a_w=g_conv_a,
                 norm_a_w=red["norm_a"], conv_qkv_w=g_conv_qkv, a_log=red["a_log"], dt_bias=red["dt_bias"],
                 norm_dn_w=red["norm_dn"], w_out=g_w_out, norm2_w=red["norm2"], w_up=g_w_up, conv_ff_w=g_conv_ff,
                 w_down=g_w_down, norm_f_w=g_norm_f)
    weights = dict(ada_w=ada_w, ada_b=ada_b, norm1_w=norm1_w, w_in=w_in, conv_a_w=conv_a_w, norm_a_w=norm_a_w,
                   conv_qkv_w=conv_qkv_w, a_log=a_log, dt_bias=dt_bias, norm_dn_w=norm_dn_w, w_out=w_out,
                   norm2_w=norm2_w, w_up=w_up, conv_ff_w=conv_ff_w, w_down=w_down, norm_f_w=norm_f_w)
    ms = dict(ada_w=m_ada_w, ada_b=m_ada_b, norm1_w=m_norm1_w, w_in=m_w_in, conv_a_w=m_conv_a_w, norm_a_w=m_norm_a_w,
              conv_qkv_w=m_conv_qkv_w, a_log=m_a_log, dt_bias=m_dt_bias, norm_dn_w=m_norm_dn_w, w_out=m_w_out,
              norm2_w=m_norm2_w, w_up=m_w_up, conv_ff_w=m_conv_ff_w, w_down=m_w_down, norm_f_w=m_norm_f_w)
    vs_ = dict(ada_w=v_ada_w, ada_b=v_ada_b, norm1_w=v_norm1_w, w_in=v_w_in, conv_a_w=v_conv_a_w, norm_a_w=v_norm_a_w,
               conv_qkv_w=v_conv_qkv_w, a_log=v_a_log, dt_bias=v_dt_bias, norm_dn_w=v_norm_dn_w, w_out=v_w_out,
               norm2_w=v_norm2_w, w_up=v_w_up, conv_ff_w=v_conv_ff_w, w_down=v_w_down, norm_f_w=v_norm_f_w)
    names = list(weights)
    big_names = ["ada_w", "w_in", "w_out", "w_up", "w_down"]
    delta, new_m, new_v = {}, {}, {}
    for n in big_names:
        shp = weights[n].shape
        two = lambda t: t.reshape(-1, shp[-1])
        dl, nm, nv = _adamw(two(weights[n]), two(grads[n]), two(ms[n]), two(vs_[n]), "adamw_" + n)
        delta[n], new_m[n], new_v[n] = dl.reshape(shp), nm.reshape(shp), nv.reshape(shp)
    small_names = [n for n in names if n not in big_names]

    def pack(dct):
        return _pad_rows(jnp.concatenate([dct[n].reshape(-1) for n in small_names]), SUB * LANES).reshape(-1, LANES)

    dl, nm, nv = _adamw(pack(weights), pack(grads), pack(ms), pack(vs_), "adamw_small")
    off = 0
    for n in small_names:
        sz, shp = weights[n].size, weights[n].shape
        delta[n] = dl.reshape(-1)[off:off + sz].reshape(shp)
        new_m[n] = nm.reshape(-1)[off:off + sz].reshape(shp)
        new_v[n] = nv.reshape(-1)[off:off + sz].reshape(shp)
        off += sz

    return (loss, grad_x, *[grads[n] for n in names], *[delta[n] for n in names],
            *[new_m[n] for n in names], *[new_v[n] for n in names])
```

```python
import functools
import math

import jax
import jax.numpy as jnp
from jax import lax
from jax.experimental import pallas as pl
from jax.experimental.pallas import tpu as pltpu
from jax.experimental.pallas import tpu_sc as plsc

F32 = jnp.float32
MXU = jnp.bfloat16

D = 1024
DEPTH = 4
N_MOD = 6
AW = 512
A_GROUP = 64
H = 4
HD = 128
CK = 64
DFF = 2816
P_IN = 3592
P_PAD = 3712
EPS = 1e-6
N_DEV = 8
LANES = 128
SUB = 8
VMEM_LIMIT = 56 * 1024 * 1024

ADAM_LR, ADAM_B1, ADAM_B2, ADAM_EPS, ADAM_WD, ADAM_STEP = 0.001, 0.9, 0.999, 1e-08, 0.01, 10

NN = ((1,), (0,))
NT = ((1,), (1,))
TN = ((0,), (0,))
HI = lax.Precision.HIGHEST
MESH = pl.DeviceIdType.MESH


def _dot(a, b, dims, prec=None):
    if prec is None:
        a = a.astype(MXU) if a.dtype == F32 else a
        b = b.astype(MXU) if b.dtype == F32 else b
    return lax.dot_general(a, b, (dims, ((), ())), precision=prec, preferred_element_type=F32)


def _params(n_grid=0, limit=VMEM_LIMIT):
    sem = ("arbitrary",) * n_grid if n_grid else None
    return pltpu.CompilerParams(dimension_semantics=sem, vmem_limit_bytes=limit)


def _tile(n, want):
    if n <= want:
        return n
    t = want - want % SUB
    while n % t:
        t -= SUB
    assert t > 0, (n, want)
    return t


def _full(shape):
    nd = len(shape)
    return pl.BlockSpec(shape, lambda *_: (0,) * nd)


def _sig(x):
    return jax.nn.sigmoid(x)


def _rms(x):
    r = lax.rsqrt(jnp.mean(x * x, axis=-1, keepdims=True) + EPS)
    return x * r, r


def _rms_bwd(dn, n, r):
    return r * (dn - n * jnp.mean(dn * n, axis=-1, keepdims=True))


def _l2_bwd(dn, n, r):
    return r * (dn - n * jnp.sum(dn * n, axis=-1, keepdims=True))


def _sum0(x):
    return jnp.sum(x, axis=0, keepdims=True)


def _shift_down(x, s, halo):
    ext = jnp.concatenate([halo, x], axis=0)
    return pltpu.roll(ext, s, 0)[SUB:, :]


def _shift_up(x, s, halo):
    t = x.shape[0]
    ext = jnp.concatenate([x, halo], axis=0)
    return pltpu.roll(ext, t + SUB - s, 0)[:t, :]


def _conv_fwd(x, w_ref, width, halo):
    sh = [x] + [_shift_down(x, s, halo) for s in range(1, width)]
    out = w_ref[width - 1:width, :] * sh[0]
    for s in range(1, width):
        out = out + w_ref[width - 1 - s:width - s, :] * sh[s]
    return out, sh


def _conv_bwd_in(dout, w_ref, width, halo_next):
    dx = w_ref[width - 1:width, :] * dout
    for s in range(1, width):
        dx = dx + w_ref[width - 1 - s:width - s, :] * _shift_up(dout, s, halo_next)
    return dx


def _blockdiag_mean(n, group):
    r = lax.shift_right_logical(lax.broadcasted_iota(jnp.int32, (n, n), 0), int(math.log2(group)))
    c = lax.shift_right_logical(lax.broadcasted_iota(jnp.int32, (n, n), 1), int(math.log2(group)))
    return jnp.where(r == c, 1.0 / group, 0.0).astype(F32)


def _softplus(x):
    return jnp.maximum(x, 0.0) + jnp.log(1.0 + jnp.exp(-jnp.abs(x)))


def _my_place():
    return lax.axis_index("x"), lax.axis_index("y"), lax.axis_index("c")


def _all_gather(shards, name, in_vmem):
    nt = len(shards)

    def body(*refs):
        x_refs, out_refs = refs[:nt], refs[nt:2 * nt]
        send_sems, recv_sems, local_sems = refs[2 * nt:]
        x, y, c = _my_place()
        me, sibling = (x, y, c), (x, y, 1 - c)
        chips = [(1 - x, y), (x, 1 - y), (1 - x, 1 - y)]
        everything = []
        for t in range(nt):
            x_ref, out_ref = x_refs[t], out_refs[t]

            def blk(px, py, pc, out_ref=out_ref):
                return out_ref.at[4 * px + 2 * py + pc]

            def copy(k, block, to, src=None, t=t, blk=blk):
                return pltpu.make_async_remote_copy(
                    src_ref=blk(*block) if src is None else src, dst_ref=blk(*block),
                    send_sem=send_sems.at[7 * t + k], recv_sem=recv_sems.at[7 * t + k], device_id=to, device_id_type=MESH)

            mine = pltpu.make_async_copy(x_ref, blk(*me), local_sems.at[t])
            mine.start()
            first = [copy(0, me, sibling, src=x_ref)]
            first += [copy(1 + j, me, (*chip, c), src=x_ref) for j, chip in enumerate(chips)]
            for cp in first:
                cp.start()
            everything.append((copy, mine, first))
        sends = []
        for copy, mine, first in everything:
            passed = [copy(4 + j, (*chip, c), sibling) for j, chip in enumerate(chips)]
            for j, chip in enumerate(chips):
                copy(1 + j, (*chip, c), me).wait_recv()
                passed[j].start()
            sends += first + passed
        for copy, mine, first in everything:
            copy(0, sibling, me).wait_recv()
            for j, chip in enumerate(chips):
                copy(4 + j, (*chip, 1 - c), me).wait_recv()
        for cp in sends:
            cp.wait_send()
        for copy, mine, first in everything:
            mine.wait()

    space = pltpu.VMEM if in_vmem else pl.ANY
    return pl.pallas_call(
        body, name=name,
        out_shape=[jax.ShapeDtypeStruct((N_DEV,) + s.shape, s.dtype) for s in shards],
        in_specs=[pl.BlockSpec(memory_space=space)] * nt,
        out_specs=[pl.BlockSpec(memory_space=space)] * nt,
        scratch_shapes=[pltpu.SemaphoreType.DMA((7 * nt,)), pltpu.SemaphoreType.DMA((7 * nt,)),
                        pltpu.SemaphoreType.DMA((nt,))],
        compiler_params=pltpu.CompilerParams(vmem_limit_bytes=VMEM_LIMIT),
    )(*shards)


def _all_gather_async(shards, name, collective_id):
    nt = len(shards)
    hbm = pltpu.MemorySpace.HBM
    x_refs = [jax.new_ref(s, memory_space=hbm) for s in shards]
    out_refs = [jax.empty_ref(jax.ShapeDtypeStruct((N_DEV,) + s.shape, s.dtype), memory_space=hbm) for s in shards]

    @pl.kernel(mesh=plsc.ScalarSubcoreMesh(axis_name="sequencer", num_cores=1), name=name,
               scratch_types=(pltpu.SemaphoreType.DMA((7 * nt,)), pltpu.SemaphoreType.DMA((7 * nt,)),
                              pltpu.SemaphoreType.DMA((nt,))),
               compiler_params=pltpu.CompilerParams(collective_id=collective_id))
    def launch(send_sems, recv_sems, local_sems):
        x, y, c = _my_place()
        me, sibling = (x, y, c), (x, y, 1 - c)
        chips = [(1 - x, y), (x, 1 - y), (1 - x, 1 - y)]
        barrier = pltpu.get_barrier_semaphore()
        for peer in [sibling] + [(*chip, c) for chip in chips]:
            pl.semaphore_signal(barrier, inc=1, device_id=peer, device_id_type=MESH)
        pl.semaphore_wait(barrier, 4)
        everything = []
        for t in range(nt):
            x_ref, out_ref = x_refs[t], out_refs[t]

            def blk(px, py, pc, out_ref=out_ref):
                return out_ref.at[4 * px + 2 * py + pc]

            def copy(k, block, to, src=None, t=t, blk=blk):
                return pltpu.make_async_remote_copy(
                    src_ref=blk(*block) if src is None else src, dst_ref=blk(*block),
                    send_sem=send_sems.at[7 * t + k], recv_sem=recv_sems.at[7 * t + k], device_id=to, device_id_type=MESH)

            mine = pltpu.make_async_copy(x_ref, blk(*me), local_sems.at[t])
            mine.start()
            first = [copy(0, me, sibling, src=x_ref)]
            first += [copy(1 + j, me, (*chip, c), src=x_ref) for j, chip in enumerate(chips)]
            for cp in first:
                cp.start()
            everything.append((copy, mine, first))
        sends = []
        for copy, mine, first in everything:
            passed = [copy(4 + j, (*chip, c), sibling) for j, chip in enumerate(chips)]
            for j, chip in enumerate(chips):
                copy(1 + j, (*chip, c), me).wait_recv()
                passed[j].start()
            sends += first + passed
        for copy, mine, first in everything:
            copy(0, sibling, me).wait_recv()
            for j, chip in enumerate(chips):
                copy(4 + j, (*chip, 1 - c), me).wait_recv()
        for cp in sends:
            cp.wait_send()
        for copy, mine, first in everything:
            mine.wait()

    launch()
    return [r[...] for r in out_refs]


def _rs_async(srcs, n_recv, to_sibling, name, collective_id):
    nt = len(srcs)
    hbm = pltpu.MemorySpace.HBM
    src_refs = [jax.new_ref(s, memory_space=hbm) for s in srcs]
    out_refs = [jax.empty_ref(jax.ShapeDtypeStruct((n_recv,) + s.shape[1:], s.dtype), memory_space=hbm) for s in srcs]

    @pl.kernel(mesh=plsc.ScalarSubcoreMesh(axis_name="sequencer", num_cores=1), name=name,
               scratch_types=(pltpu.SemaphoreType.DMA((n_recv * nt,)), pltpu.SemaphoreType.DMA((n_recv * nt,))),
               compiler_params=pltpu.CompilerParams(collective_id=collective_id))
    def launch(send_sems, recv_sems):
        x, y, c = _my_place()
        if to_sibling:
            peers = [(x, y, 1 - c)]
            plan = [(2 * j + (1 - c), j, peers[0]) for j in range(4)]
        else:
            peers = [(1 - x, y, c), (x, 1 - y, c), (1 - x, 1 - y, c)]
            plan = [(2 * px + py, s, (px, py, pc)) for s, (px, py, pc) in enumerate(peers)]
        barrier = pltpu.get_barrier_semaphore()
        for peer in peers:
            pl.semaphore_signal(barrier, inc=1, device_id=peer, device_id_type=MESH)
        pl.semaphore_wait(barrier, len(peers))
        copies = [pltpu.make_async_remote_copy(
            src_ref=src_refs[t].at[blk], dst_ref=out_refs[t].at[slot],
            send_sem=send_sems.at[n_recv * t + slot], recv_sem=recv_sems.at[n_recv * t + slot],
            device_id=to, device_id_type=MESH) for t in range(nt) for blk, slot, to in plan]
        for cp in copies:
            cp.start()
        for cp in copies:
            cp.wait()

    launch()
    return [r[...] for r in out_refs]


def _rs_add_pairs(g, recv, my_c, name):
    _, nl, r, n = g.shape
    tr = _tile(r, 512)

    def body(c_ref, g_ref, r_ref, pf_ref, pb_ref):
        s = g_ref[...] + r_ref[...]
        pf_ref[...] = s
        pb_ref[...] = s.astype(MXU)

    spec_j = pl.BlockSpec((None, None, tr, n), lambda j, l, i, c_ref: (j, l, i, 0))
    return pl.pallas_call(
        body, name=name,
        grid_spec=pltpu.PrefetchScalarGridSpec(
            num_scalar_prefetch=1, grid=(4, nl, r // tr),
            in_specs=[pl.BlockSpec((None, None, tr, n), lambda j, l, i, c_ref: (2 * j + c_ref[0], l, i, 0)), spec_j],
            out_specs=[spec_j, spec_j]),
        out_shape=[jax.ShapeDtypeStruct((4, nl, r, n), F32), jax.ShapeDtypeStruct((4, nl, r, n), MXU)],
        compiler_params=_params(3),
    )(my_c, g, recv)


def _rs_add_chips(pf, recv, my_chip, name):
    _, nl, r, n = pf.shape
    tr = _tile(r, 512)

    def body(j_ref, p_ref, r_ref, o_ref):
        s = p_ref[...]
        for t in range(3):
            s = s + r_ref[t].astype(F32)
        o_ref[...] = s

    return pl.pallas_call(
        body, name=name,
        grid_spec=pltpu.PrefetchScalarGridSpec(
            num_scalar_prefetch=1, grid=(nl, r // tr),
            in_specs=[pl.BlockSpec((None, None, tr, n), lambda l, i, j_ref: (j_ref[0], l, i, 0)),
                      pl.BlockSpec((3, None, tr, n), lambda l, i, j_ref: (0, l, i, 0))],
            out_specs=pl.BlockSpec((None, tr, n), lambda l, i, j_ref: (l, i, 0))),
        out_shape=jax.ShapeDtypeStruct((nl, r, n), F32),
        compiler_params=_params(2),
    )(my_chip, pf, recv)


def _shard_windows(n_shard, count, first=0):
    out = []
    for k in range(first, first + count):
        off = n_shard * k
        a, s = off // LANES, off % LANES
        out.append((a, s, -(-(s + n_shard) // LANES) * LANES))
    return out


def _fit_lanes(x, width):
    have = x.shape[1]
    if have < width:
        return jnp.concatenate([x, jnp.zeros((x.shape[0], width - have), x.dtype)], axis=-1)
    return x[:, :width]


def _interleave_cols(g, n_shard, w_out, name):
    nd, nl, rows, wpad = g.shape
    rb = _tile(rows, 256)
    wins = _shard_windows(n_shard, nd)

    def body(g_ref, o_ref, acc):
        acc[...] = jnp.zeros_like(acc)
        for k, (a, s, win) in enumerate(wins):
            xk = _fit_lanes(g_ref[k].astype(F32), win)
            if s:
                xk = pltpu.roll(xk, s, 1)
            acc[:, a * LANES:a * LANES + win] += xk
        o_ref[...] = acc[...].astype(o_ref.dtype)

    return pl.pallas_call(
        body, name=name, grid=(nl, rows // rb),
        in_specs=[pl.BlockSpec((nd, None, rb, wpad), lambda l, i: (0, l, i, 0))],
        out_specs=pl.BlockSpec((None, rb, w_out), lambda l, i: (l, i, 0)),
        out_shape=jax.ShapeDtypeStruct((nl, rows, w_out), g.dtype),
        scratch_shapes=[pltpu.VMEM((rb, w_out), F32)],
        compiler_params=_params(2),
    )(g)


def _sum_devices(g):
    _, r, n = g.shape

    def body(g_ref, o_ref):
        s = g_ref[0]
        for t in range(1, N_DEV):
            s = s + g_ref[t]
        o_ref[...] = s

    return pl.pallas_call(
        body, name="sum_devices", out_shape=jax.ShapeDtypeStruct((r, n), F32),
        in_specs=[pl.BlockSpec(memory_space=pltpu.VMEM)], out_specs=pl.BlockSpec(memory_space=pltpu.VMEM),
        compiler_params=pltpu.CompilerParams(vmem_limit_bytes=VMEM_LIMIT),
    )(g)


def _mod_fwd(c_all, ada_w, ada_b_cols):
    nl, _, nc = ada_w.shape

    def body(c_ref, w_ref, b_ref, o_ref):
        cv = c_ref[...]
        act = (cv * _sig(cv)).astype(MXU)
        o_ref[...] = _dot(act, w_ref[...].astype(MXU), NN) + b_ref[...]

    return pl.pallas_call(
        body, name="mod_fwd", grid=(nl,),
        in_specs=[_full((16, D)), pl.BlockSpec((None, D, nc), lambda i: (i, 0, 0)),
                  pl.BlockSpec((None, 1, nc), lambda i: (i, 0, 0))],
        out_specs=pl.BlockSpec((None, 16, nc), lambda i: (i, 0, 0)),
        out_shape=jax.ShapeDtypeStruct((nl, 16, nc), F32), compiler_params=_params(1),
    )(c_all, ada_w, ada_b_cols)


def _mod_bwd(c_all, dmod_cols):
    nl, _, nc = dmod_cols.shape

    def body(c_ref, d_ref, o_ref):
        cv = c_ref[...]
        act = (cv * _sig(cv)).astype(MXU)
        o_ref[...] = _dot(act, d_ref[...].astype(MXU), TN)

    return pl.pallas_call(
        body, name="mod_bwd", grid=(nl,),
        in_specs=[_full((16, D)), pl.BlockSpec((None, 16, nc), lambda i: (i, 0, 0))],
        out_specs=pl.BlockSpec((None, D, nc), lambda i: (i, 0, 0)),
        out_shape=jax.ShapeDtypeStruct((nl, D, nc), F32), compiler_params=_params(1),
    )(c_all, dmod_cols)


def _in_proj(x, modrows, vec, w_in):
    L = x.shape[0]
    T = _tile(L, 256)

    def body(x_ref, mod_ref, vec_ref, w_ref, p_ref, h_ref):
        n, _ = _rms(x_ref[...])
        h = n * vec_ref[0:1, :] * (1.0 + mod_ref[1:2, :]) + mod_ref[0:1, :]
        hb = h.astype(MXU)
        h_ref[...] = hb
        p_ref[...] = _dot(hb, w_ref[...], NN)

    return pl.pallas_call(
        body, name="in_proj", grid=(L // T,),
        in_specs=[pl.BlockSpec((T, D), lambda i: (i, 0)), _full((SUB, D)), _full((SUB, D)), _full((D, P_PAD))],
        out_specs=[pl.BlockSpec((T, P_PAD), lambda i: (i, 0)), pl.BlockSpec((T, D), lambda i: (i, 0))],
        out_shape=[jax.ShapeDtypeStruct((L, P_PAD), F32), jax.ShapeDtypeStruct((L, D), MXU)],
        compiler_params=_params(1),
    )(x, modrows, vec, w_in)


def _gate_small(s, sp_ref):
    lane = lax.broadcasted_iota(jnp.int32, s.shape, 1)
    a = -jnp.exp(sp_ref[0:1, :])
    xb = s + sp_ref[1:2, :]
    beta = _sig(s)
    g = a * _softplus(xb)
    return lane, a, xb, beta, g


def _pre_fwd(p, pa, cq, sp):
    L = p.shape[0]
    T = _tile(L, 256)
    scale = HD ** -0.5

    def body(pm_ref, ps_ref, pa_ref, cq_ref, sp_ref, qn_ref, kn_ref, vs_ref, gb_ref, ya_ref, u_carry, q_carry):
        @pl.when(pl.program_id(0) == 0)
        def _():
            u_carry[...] = jnp.zeros_like(u_carry)
            q_carry[...] = jnp.zeros_like(q_carry)

        a_b = pm_ref[:, 0:AW]
        u = pm_ref[:, AW:2 * AW] * pm_ref[:, 2 * AW:3 * AW]
        cu, _ = _conv_fwd(u, pa_ref, 3, u_carry[...])
        u_carry[...] = u[T - SUB:T, :]
        yp = a_b * cu
        ms = _dot_f32(yp * yp, _blockdiag_mean(AW, A_GROUP), NN, exact="b")
        ya_ref[...] = (yp * lax.rsqrt(ms + EPS) * pa_ref[3:4, :]).astype(MXU)

        qkv = pm_ref[:, 3 * AW:3 * AW + 3 * H * HD]
        qc, _ = _conv_fwd(qkv, cq_ref, 4, q_carry[...])
        q_carry[...] = qkv[T - SUB:T, :]
        qs = qc * _sig(qc)
        for h in range(H):
            q = qs[:, h * HD:(h + 1) * HD]
            qn_ref[:, h * HD:(h + 1) * HD] = q * (lax.rsqrt(jnp.sum(q * q, axis=-1, keepdims=True) + EPS) * scale)
            k = qs[:, (H + h) * HD:(H + h + 1) * HD]
            kn_ref[:, h * HD:(h + 1) * HD] = k * lax.rsqrt(jnp.sum(k * k, axis=-1, keepdims=True) + EPS)
        vs_ref[...] = qs[:, 2 * H * HD:3 * H * HD]

        lane, _, _, beta, g = _gate_small(ps_ref[...], sp_ref)
        gb_ref[...] = jnp.where(lane < H, beta, jnp.where(lane < 2 * H, g, 0.0))

    w3 = 3 * AW + 3 * H * HD
    row = lambda i: (i, 0)
    return pl.pallas_call(
        body, name="pre_fwd", grid=(L // T,),
        in_specs=[pl.BlockSpec((T, w3), row), pl.BlockSpec((T, LANES), lambda i: (i, (P_PAD - LANES) // LANES)),
                  _full((SUB, AW)), _full((SUB, 3 * H * HD)), _full((SUB, LANES))],
        out_specs=[pl.BlockSpec((T, H * HD), row)] * 3 + [pl.BlockSpec((T, LANES), row), pl.BlockSpec((T, AW), row)],
        out_shape=[jax.ShapeDtypeStruct((L, H * HD), F32)] * 3
        + [jax.ShapeDtypeStruct((L, LANES), F32), jax.ShapeDtypeStruct((L, AW), MXU)],
        scratch_shapes=[pltpu.VMEM((SUB, AW), F32), pltpu.VMEM((SUB, 3 * H * HD), F32)],
        compiler_params=_params(1),
    )(p, p, pa, cq, sp)


def _gdr_masks():
    r = lax.broadcasted_iota(jnp.int32, (CK, CK), 0)
    c = lax.broadcasted_iota(jnp.int32, (CK, CK), 1)
    return r >= c, r > c


def _head_cols(gbt, h):
    return gbt[:, h:h + 1], gbt[:, H + h:H + h + 1]


def _split(x, parts):
    out = []
    for _ in range(parts):
        hi = x.astype(jnp.bfloat16)
        out.append(hi)
        x = x - hi.astype(F32)
    return out


def _dot_f32(a, b, dims, exact=None):
    if exact == "a":
        ab = a.astype(jnp.bfloat16)
        return sum(_dot(ab, t, dims) for t in _split(b, 3))
    if exact == "b":
        bb = b.astype(jnp.bfloat16)
        return sum(_dot(t, bb, dims) for t in _split(a, 3))
    ah, al = _split(a, 2)
    bh, bl = _split(b, 2)
    return _dot(ah, bh, dims) + _dot(ah, bl, dims) + _dot(al, bh, dims)


def _gdr_consts():
    causal, strict = _gdr_masks()
    return dict(causal=causal, strict=strict, tril=jnp.where(causal, 1.0, 0.0).astype(F32),
                eye=jnp.where(causal & jnp.logical_not(strict), 1.0, 0.0).astype(F32),
                bcast=jnp.full((CK, HD), 1.0 / HD, F32))


def _dots(a, b, dims):
    return [_dot(x, y, dims) for x, y in zip(a, b)]


def _dots_f32(a, b, dims, exact=None):
    n = len(a)
    if exact == "a":
        lhs = [[x.astype(jnp.bfloat16)] * 3 for x in a]
        rhs = [_split(y, 3) for y in b]
    elif exact == "b":
        lhs = [_split(x, 3) for x in a]
        rhs = [[y.astype(jnp.bfloat16)] * 3 for y in b]
    else:
        sa = [_split(x, 2) for x in a]
        sb = [_split(y, 2) for y in b]
        lhs = [[s[0], s[0], s[1]] for s in sa]
        rhs = [[s[0], s[1], s[0]] for s in sb]
    terms = [[_dot(lhs[i][t], rhs[i][t], dims) for i in range(n)] for t in range(3)]
    return [terms[0][i] + terms[1][i] + terms[2][i] for i in range(n)]


def _gdr_local(q, k, v, beta, g, cst, tinv=None):
    n = len(q)
    R = range(n)
    causal, strict = cst["causal"], cst["strict"]
    gc = _dots_f32([cst["tril"]] * n, [jnp.broadcast_to(g[i], (CK, HD)) for i in R], NN, exact="a")
    g_row = _dots_f32([cst["bcast"]] * n, gc, NT, exact="a")
    decay = [jnp.where(causal, jnp.exp(jnp.where(causal, gc[i][:, 0:CK] - g_row[i], 0.0)), 0.0) for i in R]
    eg = [jnp.exp(gc[i]) for i in R]
    gl = [gc[i][CK - 1:CK, :] for i in R]
    ek = [jnp.exp(gl[i] - gc[i]) for i in R]
    cd = [jnp.exp(gl[i]) for i in R]
    kb = [k[i] * beta[i] for i in R]
    pk = _dots(kb, k, NT)
    if tinv is None:
        xp = [-jnp.where(strict, pk[i] * decay[i], 0.0) for i in R]
        tinv = [cst["eye"] + xp[i] for i in R]
        for _ in range(5):
            xp = _dots_f32(xp, xp, NN)
            tx = _dots_f32(tinv, xp, NN)
            tinv = [tinv[i] + tx[i] for i in R]
    u = _dots(tinv, [v[i] * beta[i] for i in R], NN)
    w = _dots(tinv, [kb[i] * eg[i] for i in R], NN)
    qk = _dots(q, k, NT)
    intra = [jnp.where(causal, qk[i] * decay[i], 0.0) for i in R]
    return dict(decay=decay, eg=eg, ek=ek, cd=cd, kb=kb, pk=pk, tinv=tinv, u=u, w=w, qk=qk, intra=intra,
                q_dec=[q[i] * eg[i] for i in R], k_dec=[k[i] * ek[i] for i in R])


GDR_SUB = 4


def _gdr_fwd(qn, kn, vs, gb):
    L = qn.shape[0]
    nc = L // CK
    cb = min(8, nc)
    rb = cb * CK
    nb = nc // cb
    nsub = GDR_SUB if cb % GDR_SUB == 0 else 1

    def body(q_ref, k_ref, v_ref, gb_ref, o_ref, st_ref, ti_ref, s_ref):
        @pl.when(pl.program_id(0) == 0)
        def _():
            s_ref[...] = jnp.zeros_like(s_ref)

        cst = _gdr_consts()
        heads = range(H)

        def group(gi, carry):
            rows = [pl.ds(pl.multiple_of((gi * nsub + j) * CK, CK), CK) for j in range(nsub)]
            chains = [(j, h) for j in range(nsub) for h in heads]
            gbt = [gb_ref[rows[j], :] for j in range(nsub)]
            cols = lambda h: slice(h * HD, (h + 1) * HD)
            t = _gdr_local([q_ref[rows[j], cols(h)] for j, h in chains], [k_ref[rows[j], cols(h)] for j, h in chains],
                           [v_ref[rows[j], cols(h)] for j, h in chains],
                           [_head_cols(gbt[j], h)[0] for j, h in chains], [_head_cols(gbt[j], h)[1] for j, h in chains], cst)
            s = [s_ref[h] for h in heads]
            for j in range(nsub):
                at = lambda key: [t[key][j * H + h] for h in heads]
                for h in heads:
                    st_ref[h, gi * nsub + j] = s[h]
                    ti_ref[h, gi * nsub + j] = t["tinv"][j * H + h]
                ws = _dots(at("w"), s, NN)
                v_new = [u_h - ws_h for u_h, ws_h in zip(at("u"), ws)]
                o_s = _dots(at("q_dec"), s, NN)
                o_v = _dots(at("intra"), v_new, NN)
                kv = _dots(at("k_dec"), v_new, TN)
                cd = at("cd")
                for h in heads:
                    o_ref[rows[j], cols(h)] = o_s[h] + o_v[h]
                s = [s[h] * cd[h] + kv[h] for h in heads]
            for h in heads:
                s_ref[h] = s[h]
            return carry

        lax.fori_loop(0, cb // nsub, group, 0)

    blk = pl.BlockSpec((rb, H * HD), lambda b: (b, 0))
    return pl.pallas_call(
        body, name="gdr_fwd", grid=(nb,),
        in_specs=[blk, blk, blk, pl.BlockSpec((rb, LANES), lambda b: (b, 0))],
        out_specs=[blk, pl.BlockSpec((H, cb, HD, HD), lambda b: (0, b, 0, 0)),
                   pl.BlockSpec((H, cb, CK, CK), lambda b: (0, b, 0, 0))],
        out_shape=[jax.ShapeDtypeStruct((L, H * HD), F32), jax.ShapeDtypeStruct((H, nc, HD, HD), F32),
                   jax.ShapeDtypeStruct((H, nc, CK, CK), F32)],
        scratch_shapes=[pltpu.VMEM((H, HD, HD), F32)],
        compiler_params=_params(1),
    )(qn, kn, vs, gb)


def _gdr_bwd(qn, kn, vs, gb, states, tinvs, do):
    L = qn.shape[0]
    nc = L // CK
    cb = min(8, nc)
    rb = cb * CK
    nb = nc // cb
    nsub = GDR_SUB if cb % GDR_SUB == 0 else 1

    def body(q_ref, k_ref, v_ref, gb_ref, st_ref, ti_ref, do_ref, dq_ref, dk_ref, dv_ref, dgb_ref, ds_ref):
        @pl.when(pl.program_id(0) == 0)
        def _():
            ds_ref[...] = jnp.zeros_like(ds_ref)

        cst = _gdr_consts()
        causal, strict = cst["causal"], cst["strict"]
        ones = jnp.ones((CK, HD), F32)
        row = lax.broadcasted_iota(jnp.int32, (CK, HD), 0)
        lane = lax.broadcasted_iota(jnp.int32, (CK, LANES), 1)

        heads = range(H)
        rsum = lambda x: jnp.sum(x, axis=-1, keepdims=True)

        def group(gj, carry):
            gi = cb // nsub - 1 - gj
            rows = [pl.ds(pl.multiple_of((gi * nsub + j) * CK, CK), CK) for j in range(nsub)]
            chains = [(j, h) for j in range(nsub) for h in heads]
            gbt = [gb_ref[rows[j], :] for j in range(nsub)]
            cols = lambda h: slice(h * HD, (h + 1) * HD)
            q_all = [q_ref[rows[j], cols(h)] for j, h in chains]
            k_all = [k_ref[rows[j], cols(h)] for j, h in chains]
            v_all = [v_ref[rows[j], cols(h)] for j, h in chains]
            beta_all = [_head_cols(gbt[j], h)[0] for j, h in chains]
            t = _gdr_local(q_all, k_all, v_all, beta_all, [_head_cols(gbt[j], h)[1] for j, h in chains], cst,
                           tinv=[ti_ref[h, gi * nsub + j] for j, h in chains])
            ds_out = [ds_ref[h] for h in heads]
            for j in reversed(range(nsub)):
                at = lambda key: [t[key][j * H + h] for h in heads]
                pick = lambda lst: [lst[j * H + h] for h in heads]
                q, k, v, beta = pick(q_all), pick(k_all), pick(v_all), pick(beta_all)
                u, w, tinv, decay = at("u"), at("w"), at("tinv"), at("decay")
                eg, ek, cd, kb = at("eg"), at("ek"), at("cd"), at("kb")
                q_dec, k_dec, intra, pk, qk = at("q_dec"), at("k_dec"), at("intra"), at("pk"), at("qk")
                s = [st_ref[h, gi * nsub + j] for h in heads]
                dout = [do_ref[rows[j], cols(h)] for h in heads]

                ws = _dots(w, s, NN)
                v_new = [u[h] - ws[h] for h in heads]
                dq_dec = _dots(dout, s, NT)
                qd = _dots(q_dec, dout, TN)
                di = _dots(dout, v_new, NT)
                dintra = [jnp.where(causal, di[h], 0.0) for h in heads]
                ido = _dots(intra, dout, TN)
                kds = _dots(k_dec, ds_out, NN)
                dv_new = [ido[h] + kds[h] for h in heads]
                dk_dec = _dots(v_new, ds_out, NT)
                dcd = [jnp.sum(jnp.sum(ds_out[h] * s[h], axis=1, keepdims=True), axis=0, keepdims=True) for h in heads]
                dvs = _dots(dv_new, s, NT)
                dw = [-dvs[h] for h in heads]
                wdv = _dots(w, dv_new, TN)
                ds_new = [qd[h] + ds_out[h] * cd[h] - wdv[h] for h in heads]
                dru = _dots(tinv, dv_new, TN)
                drw = _dots(tinv, dw, TN)
                dl1 = _dots(dru, u, NT)
                dl2 = _dots(drw, w, NT)
                dlower = [-jnp.where(strict, dl1[h] + dl2[h], 0.0) for h in heads]
                dv = [dru[h] * beta[h] for h in heads]
                dbeta = [rsum(dru[h] * v[h]) for h in heads]
                dgc = [rsum(drw[h] * kb[h]) * eg[h] for h in heads]
                dpk = [dlower[h] * decay[h] for h in heads]
                dqk = [dintra[h] * decay[h] for h in heads]
                dpk_k = _dots(dpk, k, NN)
                dkb = [drw[h] * eg[h] + dpk_k[h] for h in heads]
                dk1 = _dots(dpk, kb, TN)
                dq1 = _dots(dqk, k, NN)
                dk2 = _dots(dqk, q, TN)
                m = [(dlower[h] * pk[h] + dintra[h] * qk[h]) * decay[h] for h in heads]
                mcol = _dots_f32(m, [ones] * H, TN, exact="b")
                e = [rsum(dk_dec[h] * k_dec[h]) for h in heads]
                dgl = [jnp.sum(e[h], axis=0, keepdims=True) + dcd[h] * cd[h] for h in heads]
                dgc = [dgc[h] + rsum(m[h]) - mcol[h] + rsum(dq_dec[h] * q_dec[h]) - e[h]
                       + jnp.where(row == CK - 1, dgl[h], 0.0) for h in heads]
                dg = _dots_f32([cst["tril"]] * H, dgc, TN, exact="a")
                dgb = jnp.zeros((CK, LANES), F32)
                for h in heads:
                    dq_ref[rows[j], cols(h)] = dq1[h] + dq_dec[h] * eg[h]
                    dk_ref[rows[j], cols(h)] = dk1[h] + dk2[h] + dk_dec[h] * ek[h] + dkb[h] * beta[h]
                    dv_ref[rows[j], cols(h)] = dv[h]
                    db = dbeta[h] + rsum(dkb[h] * k[h])
                    dgb = dgb + jnp.where(lane == h, db, 0.0) + jnp.where(lane == H + h, dg[h], 0.0)
                dgb_ref[rows[j], :] = dgb
                ds_out = ds_new
            for h in heads:
                ds_ref[h] = ds_out[h]
            return carry

        lax.fori_loop(0, cb // nsub, group, 0)

    blk = pl.BlockSpec((rb, H * HD), lambda b: (nb - 1 - b, 0))
    sblk = pl.BlockSpec((rb, LANES), lambda b: (nb - 1 - b, 0))
    return pl.pallas_call(
        body, name="gdr_bwd", grid=(nb,),
        in_specs=[blk, blk, blk, sblk, pl.BlockSpec((H, cb, HD, HD), lambda b: (0, nb - 1 - b, 0, 0)),
                  pl.BlockSpec((H, cb, CK, CK), lambda b: (0, nb - 1 - b, 0, 0)), blk],
        out_specs=[blk, blk, blk, sblk],
        out_shape=[jax.ShapeDtypeStruct((L, H * HD), F32)] * 3 + [jax.ShapeDtypeStruct((L, LANES), F32)],
        scratch_shapes=[pltpu.VMEM((H, HD, HD), F32)],
        compiler_params=_params(1),
    )(qn, kn, vs, gb, states, tinvs, do)


def _post_fwd(o, p, ya, x, modrows, sp, w_out):
    L = x.shape[0]
    T = _tile(L, 256)

    def body(o_ref, z_ref, ya_ref, x_ref, mod_ref, sp_ref, w_ref, y_ref, x2_ref, yb_ref):
        ndw = sp_ref[2:3, :]
        z = z_ref[...]
        sz = z * _sig(z)
        parts = []
        for h in range(H):
            n, _ = _rms(o_ref[:, h * HD:(h + 1) * HD])
            parts.append(n * ndw * sz[:, h * HD:(h + 1) * HD])
        yb = jnp.concatenate(parts, axis=-1).astype(MXU)
        yb_ref[...] = yb
        y = _dot(ya_ref[...], w_ref[0:AW, :], NN) + _dot(yb, w_ref[AW:2 * AW, :], NN)
        y_ref[...] = y
        x2_ref[...] = x_ref[...] + mod_ref[2:3, :] * y

    row = lambda i: (i, 0)
    zcol = (3 * AW + 3 * H * HD) // (H * HD)
    return pl.pallas_call(
        body, name="post_fwd", grid=(L // T,),
        in_specs=[pl.BlockSpec((T, H * HD), row), pl.BlockSpec((T, H * HD), lambda i: (i, zcol)),
                  pl.BlockSpec((T, AW), row), pl.BlockSpec((T, D), row), _full((SUB, D)), _full((SUB, LANES)),
                  _full((D, D))],
        out_specs=[pl.BlockSpec((T, D), row), pl.BlockSpec((T, D), row), pl.BlockSpec((T, H * HD), row)],
        out_shape=[jax.ShapeDtypeStruct((L, D), F32), jax.ShapeDtypeStruct((L, D), F32),
                   jax.ShapeDtypeStruct((L, H * HD), MXU)],
        compiler_params=_params(1),
    )(o, p, ya, x, modrows, sp, w_out)


FF_COLS = 2
FF_CW = DFF // FF_COLS
FF_ROWS = 256


def _ffn_fwd_half(x2, modrows, vec, w_up, cff, w_down, j, d_prev):
    assert FF_COLS == 2
    L = x2.shape[0]
    T = _tile(L, FF_ROWS)
    nj = FF_COLS
    last = d_prev is not None

    def body(*refs):
        x_ref, mod_ref, vec_ref, wg_ref, wu_ref, cg_ref, cu_ref, wd_ref = refs[:8]
        if last:
            dp_ref, gp_ref, up_ref, f_ref, d_ref, x3_ref, carry_g, carry_u = refs[8:]
        else:
            h_ref, gp_ref, up_ref, f_ref, d_ref, carry_g, carry_u = refs[8:]

        @pl.when(pl.program_id(0) == 0)
        def _():
            carry_g[...] = jnp.zeros_like(carry_g)
            carry_u[...] = jnp.zeros_like(carry_u)

        xv = x_ref[...]
        n, _ = _rms(xv)
        hb = (n * vec_ref[1:2, :] * (1.0 + mod_ref[4:5, :]) + mod_ref[3:4, :]).astype(MXU)
        if not last:
            h_ref[...] = hb
        g = _dot(hb, wg_ref[...], NN)
        u = _dot(hb, wu_ref[...], NN)
        gp_ref[...] = g
        up_ref[...] = u
        gc, _ = _conv_fwd(g, cg_ref, 3, carry_g[...])
        uc, _ = _conv_fwd(u, cu_ref, 3, carry_u[...])
        carry_g[...] = g[T - SUB:T, :]
        carry_u[...] = u[T - SUB:T, :]
        fb = (gc * _sig(gc) * uc).astype(MXU)
        f_ref[...] = fb
        part = _dot(fb, wd_ref[...], NN)
        if last:
            dv = dp_ref[...] + part
            d_ref[...] = dv
            x3_ref[...] = xv + mod_ref[5:6, :] * dv
        else:
            d_ref[...] = part

    row = lambda i: (i, 0)
    rowD = pl.BlockSpec((T, D), row)
    rowC = pl.BlockSpec((T, FF_CW), row)
    in_specs = [rowD, _full((SUB, D)), _full((SUB, D)),
                pl.BlockSpec((D, FF_CW), lambda i: (0, j)), pl.BlockSpec((D, FF_CW), lambda i: (0, nj + j)),
                pl.BlockSpec((SUB, FF_CW), lambda i: (0, j)), pl.BlockSpec((SUB, FF_CW), lambda i: (0, nj + j)),
                pl.BlockSpec((FF_CW, D), lambda i: (j, 0))]
    half = [jax.ShapeDtypeStruct((L, FF_CW), F32), jax.ShapeDtypeStruct((L, FF_CW), F32),
            jax.ShapeDtypeStruct((L, FF_CW), MXU)]
    args = [x2, modrows, vec, w_up, w_up, cff, cff, w_down]
    if last:
        in_specs.append(rowD)
        args.append(d_prev)
        out_specs = [rowC, rowC, rowC, rowD, rowD]
        out_shape = half + [jax.ShapeDtypeStruct((L, D), F32), jax.ShapeDtypeStruct((L, D), F32)]
    else:
        out_specs = [rowD, rowC, rowC, rowC, rowD]
        out_shape = [jax.ShapeDtypeStruct((L, D), MXU)] + half + [jax.ShapeDtypeStruct((L, D), F32)]
    return pl.pallas_call(
        body, name="ffn_fwd_last" if last else "ffn_fwd_first", grid=(L // T,),
        in_specs=in_specs, out_specs=out_specs, out_shape=out_shape,
        scratch_shapes=[pltpu.VMEM((SUB, FF_CW), F32), pltpu.VMEM((SUB, FF_CW), F32)],
        compiler_params=_params(1),
    )(*args)


def _ffn_bwd_half(dx3, modrows, gpre, upre, cff, w_down, w_up, j, tail):
    assert FF_COLS == 2
    L = dx3.shape[0]
    T = _tile(L, FF_ROWS)
    ni, nj = L // T, FF_COLS
    hb_per_t = T // SUB
    last = tail is not None

    def body(*refs):
        dx3_ref, mod_ref, gp_ref, up_ref, gph_ref, uph_ref, cg_ref, cu_ref, wd_ref, wg_ref, wu_ref = refs[:11]
        if last:
            (d_ref, x2_ref, vec_ref, dhp_ref, dgp_ref, dup_ref, dx2_ref, accv_ref, dcg_ref, dcu_ref,
             carry_g, carry_u) = refs[11:]
        else:
            dd_ref, dgp_ref, dup_ref, dh_ref, dcg_ref, dcu_ref, carry_g, carry_u = refs[11:]
        i = pl.program_id(0)
        ri = ni - 1 - i

        @pl.when(i == 0)
        def _():
            carry_g[...] = jnp.zeros_like(carry_g)
            carry_u[...] = jnp.zeros_like(carry_u)
            dcg_ref[...] = jnp.zeros_like(dcg_ref)
            dcu_ref[...] = jnp.zeros_like(dcu_ref)
            if last:
                accv_ref[...] = jnp.zeros_like(accv_ref)

        dx3v = dx3_ref[...]
        ddb = (mod_ref[5:6, :] * dx3v).astype(MXU)
        if not last:
            dd_ref[...] = ddb
        g, u = gp_ref[...], up_ref[...]
        keep = jnp.where(ri == 0, 0.0, 1.0)
        gc, gsh = _conv_fwd(g, cg_ref, 3, gph_ref[...] * keep)
        uc, ush = _conv_fwd(u, cu_ref, 3, uph_ref[...] * keep)
        sg = _sig(gc)
        df = _dot(ddb, wd_ref[...], NT)
        duc = df * (gc * sg)
        dgc = df * uc * (sg * (1.0 + gc * (1.0 - sg)))
        for s in range(3):
            dcg_ref[2 - s:3 - s, :] += _sum0(dgc * gsh[s])
            dcu_ref[2 - s:3 - s, :] += _sum0(duc * ush[s])
        dg = _conv_bwd_in(dgc, cg_ref, 3, carry_g[...]).astype(MXU)
        du = _conv_bwd_in(duc, cu_ref, 3, carry_u[...]).astype(MXU)
        carry_g[...] = dgc[0:SUB, :]
        carry_u[...] = duc[0:SUB, :]
        dgp_ref[...] = dg
        dup_ref[...] = du
        dh = _dot(dg, wg_ref[...], NT) + _dot(du, wu_ref[...], NT)
        if last:
            dh = dh + dhp_ref[...]
            accv_ref[0:1, :] += _sum0(dx3v * d_ref[...])
            n, r = _rms(x2_ref[...])
            nw, sc = vec_ref[1:2, :], mod_ref[4:5, :]
            accv_ref[1:2, :] += _sum0(dh)
            accv_ref[2:3, :] += _sum0(dh * n * nw)
            accv_ref[3:4, :] += _sum0(dh * n * (1.0 + sc))
            dx2_ref[...] = _rms_bwd(dh * nw * (1.0 + sc), n, r) + dx3v
        else:
            dh_ref[...] = dh

    row = lambda i: (ni - 1 - i, 0)
    halo = lambda i: (jnp.maximum((ni - 1 - i) * hb_per_t - 1, 0), 0)
    rowD = pl.BlockSpec((T, D), row)
    rowC = pl.BlockSpec((T, FF_CW), row)
    haloC = pl.BlockSpec((SUB, FF_CW), halo)
    in_specs = [rowD, _full((SUB, D)), rowC, rowC, haloC, haloC,
                pl.BlockSpec((SUB, FF_CW), lambda i: (0, j)), pl.BlockSpec((SUB, FF_CW), lambda i: (0, nj + j)),
                pl.BlockSpec((FF_CW, D), lambda i: (j, 0)),
                pl.BlockSpec((D, FF_CW), lambda i: (0, j)), pl.BlockSpec((D, FF_CW), lambda i: (0, nj + j))]
    args = [dx3, modrows, gpre, upre, gpre, upre, cff, cff, w_down, w_up, w_up]
    halfb = [jax.ShapeDtypeStruct((L, FF_CW), MXU), jax.ShapeDtypeStruct((L, FF_CW), MXU)]
    dconv = [jax.ShapeDtypeStruct((SUB, FF_CW), F32)] * 2
    if last:
        d, x2, vec, dh_prev = tail
        in_specs += [rowD, rowD, _full((SUB, D)), rowD]
        args += [d, x2, vec, dh_prev]
        out_specs = [rowC, rowC, rowD, _full((SUB, D)), _full((SUB, FF_CW)), _full((SUB, FF_CW))]
        out_shape = halfb + [jax.ShapeDtypeStruct((L, D), F32), jax.ShapeDtypeStruct((SUB, D), F32)] + dconv
    else:
        out_specs = [rowD, rowC, rowC, rowD, _full((SUB, FF_CW)), _full((SUB, FF_CW))]
        out_shape = [jax.ShapeDtypeStruct((L, D), MXU)] + halfb + [jax.ShapeDtypeStruct((L, D), F32)] + dconv
    return pl.pallas_call(
        body, name="ffn_bwd_last" if last else "ffn_bwd_first", grid=(ni,),
        in_specs=in_specs, out_specs=out_specs, out_shape=out_shape,
        scratch_shapes=[pltpu.VMEM((SUB, FF_CW), F32), pltpu.VMEM((SUB, FF_CW), F32)],
        compiler_params=_params(1),
    )(*args)


def _final(x, target, nf):
    L = x.shape[0]
    T = _tile(L, 256)

    def body(x_ref, t_ref, nf_ref, dx_ref, acc_ref):
        @pl.when(pl.program_id(0) == 0)
        def _():
            acc_ref[...] = jnp.zeros_like(acc_ref)

        n, r = _rms(x_ref[...])
        w = nf_ref[0:1, :]
        err = n * w - t_ref[...]
        acc_ref[0:1, :] += (0.5 / D) * _sum0(err * err)
        dy = err * (1.0 / D)
        acc_ref[1:2, :] += _sum0(dy * n)
        dx_ref[...] = _rms_bwd(dy * w, n, r)

    row = lambda i: (i, 0)
    return pl.pallas_call(
        body, name="final_norm_loss", grid=(L // T,),
        in_specs=[pl.BlockSpec((T, D), row), pl.BlockSpec((T, D), row), _full((SUB, D))],
        out_specs=[pl.BlockSpec((T, D), row), _full((SUB, D))],
        out_shape=[jax.ShapeDtypeStruct((L, D), F32), jax.ShapeDtypeStruct((SUB, D), F32)],
        compiler_params=_params(1),
    )(x, target, nf)


def _post_bwd(dx2, y, o, p, modrows, sp, w_out):
    L = dx2.shape[0]
    T = _tile(L, 256)

    def body(dx2_ref, y_ref, o_ref, z_ref, mod_ref, sp_ref, w_ref, dy_ref, do_ref, dz_ref, dya_ref, accv_ref, accs_ref):
        @pl.when(pl.program_id(0) == 0)
        def _():
            accv_ref[...] = jnp.zeros_like(accv_ref)
            accs_ref[...] = jnp.zeros_like(accs_ref)

        dx2v = dx2_ref[...]
        accv_ref[0:1, :] += _sum0(dx2v * y_ref[...])
        dyb = (mod_ref[2:3, :] * dx2v).astype(MXU)
        dy_ref[...] = dyb
        dyc = _dot(dyb, w_ref[...], NT)
        dya_ref[...] = dyc[:, 0:AW]
        ndw = sp_ref[2:3, :]
        z = z_ref[...]
        sgz = _sig(z)
        dsz = sgz * (1.0 + z * (1.0 - sgz))
        dndw = jnp.zeros((1, HD), F32)
        for h in range(H):
            sl = slice(h * HD, (h + 1) * HD)
            n, r = _rms(o_ref[:, sl])
            dyh = dyc[:, AW + h * HD:AW + (h + 1) * HD]
            zh = z[:, sl]
            don = dyh * (zh * sgz[:, sl])
            dz_ref[:, sl] = dyh * (n * ndw) * dsz[:, sl]
            dndw = dndw + _sum0(don * n)
            do_ref[:, sl] = _rms_bwd(don * ndw, n, r)
        accs_ref[0:1, :] += dndw

    row = lambda i: (i, 0)
    zcol = (3 * AW + 3 * H * HD) // (H * HD)
    return pl.pallas_call(
        body, name="post_bwd", grid=(L // T,),
        in_specs=[pl.BlockSpec((T, D), row), pl.BlockSpec((T, D), row), pl.BlockSpec((T, H * HD), row),
                  pl.BlockSpec((T, H * HD), lambda i: (i, zcol)), _full((SUB, D)), _full((SUB, LANES)), _full((D, D))],
        out_specs=[pl.BlockSpec((T, D), row)] + [pl.BlockSpec((T, H * HD), row)] * 3 + [_full((SUB, D)), _full((SUB, LANES))],
        out_shape=[jax.ShapeDtypeStruct((L, D), MXU)] + [jax.ShapeDtypeStruct((L, H * HD), F32)] * 3
        + [jax.ShapeDtypeStruct((SUB, D), F32), jax.ShapeDtypeStruct((SUB, LANES), F32)],
        compiler_params=_params(1),
    )(dx2, y, o, p, modrows, sp, w_out)


def _pre_bwd(p, dqn, dkn, dvs, dya, dz, dgb, pa, cq, sp):
    L = p.shape[0]
    T = _tile(L, 256)
    ni = L // T
    scale = HD ** -0.5
    w3 = 3 * AW + 3 * H * HD
    hb_per_t = T // SUB

    def body(pm_ref, ph_ref, ps_ref, dq_ref, dk_ref, dv_ref, dya_ref, dz_ref, dgb_ref, pa_ref, cq_ref, sp_ref,
             dp_ref, dpa_ref, dcq_ref, dsp_ref, carry_u, carry_q):
        i = pl.program_id(0)
        ri = ni - 1 - i

        @pl.when(i == 0)
        def _():
            dpa_ref[...] = jnp.zeros_like(dpa_ref)
            dcq_ref[...] = jnp.zeros_like(dcq_ref)
            dsp_ref[...] = jnp.zeros_like(dsp_ref)
            carry_u[...] = jnp.zeros_like(carry_u)
            carry_q[...] = jnp.zeros_like(carry_q)

        keep = jnp.where(ri == 0, 0.0, 1.0)
        a_b, a_c, a_x = pm_ref[:, 0:AW], pm_ref[:, AW:2 * AW], pm_ref[:, 2 * AW:3 * AW]
        u = a_c * a_x
        hu = ph_ref[:, AW:2 * AW] * ph_ref[:, 2 * AW:3 * AW] * keep
        cu, ush = _conv_fwd(u, pa_ref, 3, hu)
        yp = a_b * cu
        bd = _blockdiag_mean(AW, A_GROUP)
        ra = lax.rsqrt(_dot_f32(yp * yp, bd, NN, exact="b") + EPS)
        na = yp * ra
        dya = dya_ref[...]
        dpa_ref[3:4, :] += _sum0(dya * na)
        dna = dya * pa_ref[3:4, :]
        dyp = ra * (dna - na * _dot_f32(dna * na, bd, NN, exact="b"))
        dcu = dyp * a_b
        for s in range(3):
            dpa_ref[2 - s:3 - s, :] += _sum0(dcu * ush[s])
        du = _conv_bwd_in(dcu, pa_ref, 3, carry_u[...])
        carry_u[...] = dcu[0:SUB, :]
        dp_ref[:, 0:AW] = (dyp * cu).astype(MXU)
        dp_ref[:, AW:2 * AW] = (du * a_x).astype(MXU)
        dp_ref[:, 2 * AW:3 * AW] = (du * a_c).astype(MXU)

        qkv = pm_ref[:, 3 * AW:w3]
        qc, qsh = _conv_fwd(qkv, cq_ref, 4, ph_ref[:, 3 * AW:w3] * keep)
        sg = _sig(qc)
        qs = qc * sg
        parts = []
        for h in range(H):
            q = qs[:, h * HD:(h + 1) * HD]
            rq = lax.rsqrt(jnp.sum(q * q, axis=-1, keepdims=True) + EPS)
            parts.append(_l2_bwd(dq_ref[:, h * HD:(h + 1) * HD] * scale, q * rq, rq))
        for h in range(H):
            k = qs[:, (H + h) * HD:(H + h + 1) * HD]
            rk = lax.rsqrt(jnp.sum(k * k, axis=-1, keepdims=True) + EPS)
            parts.append(_l2_bwd(dk_ref[:, h * HD:(h + 1) * HD], k * rk, rk))
        parts.append(dv_ref[...])
        dqc = jnp.concatenate(parts, axis=-1) * (sg * (1.0 + qc * (1.0 - sg)))
        for s in range(4):
            dcq_ref[3 - s:4 - s, :] += _sum0(dqc * qsh[s])
        dp_ref[:, 3 * AW:w3] = _conv_bwd_in(dqc, cq_ref, 4, carry_q[...]).astype(MXU)
        carry_q[...] = dqc[0:SUB, :]
        dp_ref[:, w3:w3 + H * HD] = dz_ref[...].astype(MXU)

        lane, a, xb, beta, g = _gate_small(ps_ref[...], sp_ref)
        dgb = dgb_ref[...]
        dbeta = jnp.where(lane < H, dgb, 0.0)
        dg = jnp.where((lane >= H) & (lane < 2 * H), dgb, 0.0)
        dalpha = dg * a * _sig(xb)
        dsp_ref[0:1, :] += _sum0(dg * g)
        dsp_ref[1:2, :] += _sum0(dalpha)
        dp_ref[:, w3 + H * HD:P_PAD] = (dbeta * beta * (1.0 - beta) + dalpha).astype(MXU)

    row = lambda i: (ni - 1 - i, 0)
    halo = lambda i: (jnp.maximum((ni - 1 - i) * hb_per_t - 1, 0), 0)
    hrow = pl.BlockSpec((T, H * HD), row)
    return pl.pallas_call(
        body, name="pre_bwd", grid=(ni,),
        in_specs=[pl.BlockSpec((T, w3), row), pl.BlockSpec((SUB, w3), halo),
                  pl.BlockSpec((T, LANES), lambda i: (ni - 1 - i, (P_PAD - LANES) // LANES)),
                  hrow, hrow, hrow, pl.BlockSpec((T, AW), row), hrow,
                  pl.BlockSpec((T, LANES), row),
                  _full((SUB, AW)), _full((SUB, 3 * H * HD)), _full((SUB, LANES))],
        out_specs=[pl.BlockSpec((T, P_PAD), row), _full((SUB, AW)), _full((SUB, 3 * H * HD)), _full((SUB, LANES))],
        out_shape=[jax.ShapeDtypeStruct((L, P_PAD), MXU), jax.ShapeDtypeStruct((SUB, AW), F32),
                   jax.ShapeDtypeStruct((SUB, 3 * H * HD), F32), jax.ShapeDtypeStruct((SUB, LANES), F32)],
        scratch_shapes=[pltpu.VMEM((SUB, AW), F32), pltpu.VMEM((SUB, 3 * H * HD), F32)],
        compiler_params=_params(1),
    )(p, p, p, dqn, dkn, dvs, dya, dz, dgb, pa, cq, sp)


def _in_bwd(dp, w_in, x, dx2, modrows, vec):
    L = x.shape[0]
    T = _tile(L, 256)

    def body(dp_ref, w_ref, x_ref, dx2_ref, mod_ref, vec_ref, dx_ref, accv_ref):
        @pl.when(pl.program_id(0) == 0)
        def _():
            accv_ref[...] = jnp.zeros_like(accv_ref)

        dh = _dot(dp_ref[...], w_ref[...], NT)
        n, r = _rms(x_ref[...])
        nw, sc = vec_ref[0:1, :], mod_ref[1:2, :]
        accv_ref[0:1, :] += _sum0(dh)
        accv_ref[1:2, :] += _sum0(dh * n * nw)
        accv_ref[2:3, :] += _sum0(dh * n * (1.0 + sc))
        dx_ref[...] = _rms_bwd(dh * nw * (1.0 + sc), n, r) + dx2_ref[...]

    row = lambda i: (i, 0)
    return pl.pallas_call(
        body, name="in_bwd", grid=(L // T,),
        in_specs=[pl.BlockSpec((T, P_PAD), row), _full((D, P_PAD)), pl.BlockSpec((T, D), row),
                  pl.BlockSpec((T, D), row), _full((SUB, D)), _full((SUB, D))],
        out_specs=[pl.BlockSpec((T, D), row), _full((SUB, D))],
        out_shape=[jax.ShapeDtypeStruct((L, D), F32), jax.ShapeDtypeStruct((SUB, D), F32)],
        compiler_params=_params(1),
    )(dp, w_in, x, dx2, modrows, vec)


def _wgrad(a, b, tm, tn, name):
    L, m = a.shape
    n = b.shape[1]
    tl = _tile(L, 512)
    tm, tn = _tile(m, tm), _tile(n, tn)
    nl = L // tl

    def body(a_ref, b_ref, o_ref):
        @pl.when(pl.program_id(2) == 0)
        def _():
            o_ref[...] = jnp.zeros_like(o_ref)

        o_ref[...] += _dot(a_ref[...], b_ref[...], TN)

    return pl.pallas_call(
        body, name=name, grid=(m // tm, n // tn, nl),
        in_specs=[pl.BlockSpec((tl, tm), lambda i, j, l: (l, i)), pl.BlockSpec((tl, tn), lambda i, j, l: (l, j))],
        out_specs=pl.BlockSpec((tm, tn), lambda i, j, l: (i, j)),
        out_shape=jax.ShapeDtypeStruct((m, n), F32), compiler_params=_params(3),
    )(a, b)


def _wgrad_cols(a, b, tm, n_shard, wpad, count, name):
    L, m = a.shape
    n = b.shape[1]
    tl = _tile(L, 512)
    tm = _tile(m, tm)
    nl = L // tl
    wins = _shard_windows(n_shard, count)
    assert all(a_ * LANES + win <= n for a_, _, win in wins), (wins, n)

    def body(a_ref, b_ref, o_ref, acc):
        @pl.when(pl.program_id(1) == 0)
        def _():
            acc[...] = jnp.zeros_like(acc)

        acc[...] += _dot(a_ref[...], b_ref[...], TN)

        @pl.when(pl.program_id(1) == nl - 1)
        def _():
            for k, (a_, s, win) in enumerate(wins):
                xk = acc[:, a_ * LANES:a_ * LANES + win]
                if s:
                    xk = pltpu.roll(xk, win - s, 1)
                o_ref[k] = _fit_lanes(xk, wpad)

    return pl.pallas_call(
        body, name=name, grid=(m // tm, nl),
        in_specs=[pl.BlockSpec((tl, tm), lambda i, l: (l, i)), pl.BlockSpec((tl, n), lambda i, l: (l, 0))],
        out_specs=pl.BlockSpec((count, tm, wpad), lambda i, l: (0, i, 0)),
        out_shape=jax.ShapeDtypeStruct((count, m, wpad), F32),
        scratch_shapes=[pltpu.VMEM((tm, n), F32)],
        compiler_params=_params(2),
    )(a, b)


def _adamw(w, g, m, v, name):
    r, n = w.shape
    tr = _tile(r, 512)
    bc1 = 1.0 - ADAM_B1 ** ADAM_STEP
    bc2 = 1.0 - ADAM_B2 ** ADAM_STEP

    def body(w_ref, g_ref, m_ref, v_ref, d_ref, nm_ref, nv_ref):
        gv = g_ref[...]
        nm = ADAM_B1 * m_ref[...] + (1.0 - ADAM_B1) * gv
        nv = ADAM_B2 * v_ref[...] + (1.0 - ADAM_B2) * (gv * gv)
        nm_ref[...] = nm
        nv_ref[...] = nv
        d_ref[...] = -ADAM_LR * ((nm / bc1) / (jnp.sqrt(nv / bc2) + ADAM_EPS) + ADAM_WD * w_ref[...])

    spec = pl.BlockSpec((tr, n), lambda i: (i, 0))
    return pl.pallas_call(
        body, name=name, grid=(r // tr,), in_specs=[spec] * 4, out_specs=[spec] * 3,
        out_shape=[jax.ShapeDtypeStruct((r, n), F32)] * 3, compiler_params=_params(1),
    )(w, g, m, v)


def _rows8(rows, width):
    out = jnp.zeros((SUB, width), F32)
    for r, vrow in enumerate(rows):
        out = out.at[r, :vrow.shape[0]].set(vrow)
    return out


def _at_lanes(v4, start):
    return jnp.zeros((LANES,), F32).at[start:start + v4.shape[0]].set(v4)


def _pad_rows(flat, mult):
    n = flat.shape[0]
    pad = (-n) % mult
    return jnp.pad(flat, (0, pad)) if pad else flat


IN_PAD = 512
UP_PAD = 768


def _local_fwd_bwd(x, target, mod_full, small_w, full_w, on_grads=None):
    norm1_w, norm2_w, norm_a_w, a_log, dt_bias, norm_dn_w, norm_f_w = small_w
    w_in_f, w_out_f, w_up_f, w_down_f, conv_a_f, conv_q_f, conv_f_f = full_w

    def layer_params(i):
        modrows = jnp.concatenate([mod_full[i], jnp.zeros((SUB - N_MOD, D), F32)], axis=0)
        vec = _rows8([norm1_w[i], norm2_w[i]], D)
        pa = _rows8([conv_a_f[i, 0], conv_a_f[i, 1], conv_a_f[i, 2], norm_a_w[i]], AW)
        cq = _rows8([conv_q_f[i, k] for k in range(4)], 3 * H * HD)
        sp = _rows8([_at_lanes(a_log[i], H), _at_lanes(dt_bias[i], H), norm_dn_w[i]], LANES)
        cff = _rows8([conv_f_f[i, k] for k in range(3)], 2 * DFF)
        return modrows, vec, pa, cq, sp, cff

    saved = []
    xi = x
    for i in range(DEPTH):
        modrows, vec, pa, cq, sp, cff = layer_params(i)
        p, h1 = _in_proj(xi, modrows, vec, w_in_f[i])
        qn, kn, vs, gb, ya = _pre_fwd(p, pa, cq, sp)
        o, states, tinvs = _gdr_fwd(qn, kn, vs, gb)
        y, x2, yb = _post_fwd(o, p, ya, xi, modrows, sp, w_out_f[i])
        h2, gp0, up0, f0, d0 = _ffn_fwd_half(x2, modrows, vec, w_up_f[i], cff, w_down_f[i], 0, None)
        gp1, up1, f1, dff, x3 = _ffn_fwd_half(x2, modrows, vec, w_up_f[i], cff, w_down_f[i], 1, d0)
        saved.append(dict(x=xi, p=p, h1=h1, qn=qn, kn=kn, vs=vs, gb=gb, ya=ya, o=o, states=states, tinvs=tinvs, y=y, x2=x2, yb=yb,
                          h2=h2, gpre=(gp0, gp1), upre=(up0, up1), f=(f0, f1), d=dff))
        xi = x3

    dx, facc = _final(xi, target, _rows8([norm_f_w], D))
    loss_local = jnp.sum(facc[0])
    d_norm_f = facc[1]

    gw_in, gw_out, gw_up, gw_down = [None] * DEPTH, [None] * DEPTH, [None] * DEPTH, [None] * DEPTH
    g_small = [None] * DEPTH
    for i in reversed(range(DEPTH)):
        s = saved[i]
        modrows, vec, pa, cq, sp, cff = layer_params(i)
        dd, dgp0, dup0, dh0, dcg0, dcu0 = _ffn_bwd_half(dx, modrows, s["gpre"][0], s["upre"][0], cff,
                                                        w_down_f[i], w_up_f[i], 0, None)
        dgp1, dup1, dx2, accf, dcg1, dcu1 = _ffn_bwd_half(dx, modrows, s["gpre"][1], s["upre"][1], cff,
                                                          w_down_f[i], w_up_f[i], 1, (s["d"], s["x2"], vec, dh0))
        n_up, up_pad = 2 * DFF // N_DEV, UP_PAD
        gw_up[i] = jnp.concatenate([_wgrad_cols(s["h2"], t, 1024, n_up, up_pad, FF_CW // n_up, "wgrad_up")
                                    for t in (dgp0, dgp1, dup0, dup1)], axis=0)
        gw_down[i] = jnp.concatenate([_wgrad(s["f"][0], dd, FF_CW, 1024, "wgrad_down"),
                                      _wgrad(s["f"][1], dd, FF_CW, 1024, "wgrad_down")],
                                     axis=0).reshape(N_DEV, DFF // N_DEV, D)
        dy, do, dz, dya, accp, accs = _post_bwd(dx2, s["y"], s["o"], s["p"], modrows, sp, w_out_f[i])
        gw_out[i] = jnp.concatenate([_wgrad(s["ya"], dy, 512, 1024, "wgrad_out"),
                                     _wgrad(s["yb"], dy, 512, 1024, "wgrad_out")], axis=0).reshape(N_DEV, D // N_DEV, D)
        dqn, dkn, dvs, dgb = _gdr_bwd(s["qn"], s["kn"], s["vs"], s["gb"], s["states"], s["tinvs"], do)
        dp, dpa, dcq, dsp = _pre_bwd(s["p"], dqn, dkn, dvs, dya, dz, dgb, pa, cq, sp)
        gw_in[i] = _wgrad_cols(s["h1"], dp, 512, P_IN // N_DEV, IN_PAD, N_DEV, "wgrad_in")
        dx, acci = _in_bwd(dp, w_in_f[i], s["x"], dx2, modrows, vec)
        dconv_ff = jnp.concatenate([dcg0, dcg1, dcu0, dcu1], axis=1)[0:3]
        dmod = jnp.stack([acci[0], acci[1], accp[0], accf[1], accf[2], accf[0]])
        g_small[i] = dict(norm1=acci[2], norm2=accf[3], norm_a=dpa[3], a_log=dsp[0, H:2 * H], dt_bias=dsp[1, H:2 * H],
                          norm_dn=accs[0], conv_a=dpa[0:3], conv_qkv=dcq[0:4], conv_ff=dconv_ff, dmod=dmod.reshape(-1))
        if on_grads is not None:
            dx = on_grads(i, [gw_in[i], gw_out[i], gw_up[i], gw_down[i]], dx)
    return loss_local, dx, gw_in, gw_out, gw_up, gw_down, g_small, d_norm_f


def kernel(x, c, ada_w, ada_b, norm1_w, w_in, conv_a_w, norm_a_w, conv_qkv_w, a_log, dt_bias, norm_dn_w, w_out, norm2_w, w_up, conv_ff_w, w_down, norm_f_w, loss_target, m_ada_w, m_ada_b, m_norm1_w, m_w_in, m_conv_a_w, m_norm_a_w, m_conv_qkv_w, m_a_log, m_dt_bias, m_norm_dn_w, m_w_out, m_norm2_w, m_w_up, m_conv_ff_w, m_w_down, m_norm_f_w, v_ada_w, v_ada_b, v_norm1_w, v_w_in, v_conv_a_w, v_norm_a_w, v_conv_qkv_w, v_a_log, v_dt_bias, v_norm_dn_w, v_w_out, v_norm2_w, v_w_up, v_conv_ff_w, v_w_down, v_norm_f_w):
    ax, ay, ac = lax.axis_index("x"), lax.axis_index("y"), lax.axis_index("c")
    me = 4 * ax + 2 * ay + ac
    x = x[0]
    target = loss_target[0]
    n_in, n_up = P_IN // N_DEV, 2 * DFF // N_DEV

    def lane_pad(t, width):
        return jnp.pad(t.astype(MXU), ((0, 0), (0, 0), (0, width - t.shape[-1])))

    conv_blob = _pad_rows(jnp.concatenate([t.reshape(-1) for t in (conv_a_w, conv_qkv_w, conv_ff_w)]),
                          SUB * LANES).reshape(-1, LANES)
    c_rows = jnp.zeros((SUB, D), F32).at[0].set(c[0])
    send = [lane_pad(w_in, IN_PAD), w_out.astype(MXU), lane_pad(w_up, UP_PAD), w_down.astype(MXU)]
    got = [None] * DEPTH
    g_in0, g_conv, g_c = _all_gather([send[0][0], conv_blob, c_rows], "gather_weights", in_vmem=False)
    shards, _ = lax.optimization_barrier(([t[0] for t in send[1:]], g_c))
    got[0] = [g_in0] + _all_gather_async(shards, "gather_weights_l0", collective_id=0)
    for i in range(1, DEPTH):
        shards, _ = lax.optimization_barrier(([t[i] for t in send], got[i - 1][-1]))
        got[i] = _all_gather_async(shards, "gather_weights_l%d" % i, collective_id=i)
    w_in_f = [_interleave_cols(g[0][:, None], n_in, P_PAD, "interleave_w_in")[0] for g in got]
    w_up_f = [_interleave_cols(g[2][:, None], n_up, 2 * DFF, "interleave_w_up")[0] for g in got]
    w_out_f = [g[1].reshape(D, D) for g in got]
    w_down_f = [g[3].reshape(DFF, D) for g in got]
    sg = g_conv.reshape(N_DEV, -1)
    o1 = conv_a_w.size
    o2 = o1 + conv_qkv_w.size
    o3 = o2 + conv_ff_w.size
    conv_a_f = sg[:, 0:o1].reshape(N_DEV, DEPTH, 3, AW // N_DEV).transpose(1, 2, 0, 3).reshape(DEPTH, 3, AW)
    conv_q_f = sg[:, o1:o2].reshape(N_DEV, DEPTH, 4, 3 * H * HD // N_DEV).transpose(1, 2, 0, 3).reshape(DEPTH, 4, 3 * H * HD)
    conv_f_f = sg[:, o2:o3].reshape(N_DEV, DEPTH, 3, n_up).transpose(1, 2, 0, 3).reshape(DEPTH, 3, 2 * DFF)

    c_all = jnp.concatenate([g_c[:, 0], jnp.zeros((16 - N_DEV, D), F32)], axis=0)
    n_ada = N_MOD * D // N_DEV
    ada_b_cols = lax.dynamic_slice_in_dim(ada_b, me * n_ada, n_ada, axis=1)[:, None, :]
    mod_sh = _mod_fwd(c_all, ada_w, ada_b_cols)
    mod_all = _all_gather([mod_sh.reshape(DEPTH * 16, n_ada)], "gather_mod", in_vmem=True)[0]
    mod_all = mod_all.reshape(N_DEV, DEPTH, 16, n_ada)
    mod_mine = lax.dynamic_index_in_dim(mod_all, me, axis=2, keepdims=False)
    mod_full = mod_mine.transpose(1, 0, 2).reshape(DEPTH, N_MOD, D)

    tags = ["w_in", "w_out", "w_up", "w_down"]
    my_c = jnp.reshape(ac, (1,)).astype(jnp.int32)
    my_chip = jnp.reshape(2 * ax + ay, (1,)).astype(jnp.int32)
    exchanged, staged = {}, [None] * DEPTH

    def rs_pairs_and_chips(i):
        gs_i, recv1 = exchanged.pop(i)
        pairs = [_rs_add_pairs(g[:, None], r[:, None], my_c, "rs_add_pairs_" + t) for g, r, t in zip(gs_i, recv1, tags)]
        pbs = [pb[:, 0] for _, pb in pairs]
        recv2 = _rs_async(pbs, 3, False, "rs_chips_l%d" % i, collective_id=2 * DEPTH + i)
        staged[i] = ([pf for pf, _ in pairs], recv2)
        return pbs

    def on_grads(i, gs_i, dx):
        recv1 = _rs_async(gs_i, 4, True, "rs_sibling_l%d" % i, collective_id=DEPTH + i)
        if i + 1 in exchanged:
            dx, _ = lax.optimization_barrier((dx, rs_pairs_and_chips(i + 1)))
        exchanged[i] = (gs_i, recv1)
        return dx

    loss_local, dx, _, _, _, _, g_small, d_norm_f = _local_fwd_bwd(
        x, target, mod_full, (norm1_w, norm2_w, norm_a_w, a_log, dt_bias, norm_dn_w, norm_f_w),
        (w_in_f, w_out_f, w_up_f, w_down_f, conv_a_f, conv_q_f, conv_f_f), on_grads)
    rs_pairs_and_chips(0)
    loss = lax.psum(loss_local, ("x", "y", "c"))
    grad_x = dx[None]

    keys = ["dmod", "norm1", "norm2", "norm_a", "a_log", "dt_bias", "norm_dn", "conv_a", "conv_qkv", "conv_ff"]
    stacked = {k: jnp.stack([g_small[i][k] for i in range(DEPTH)]) for k in keys}
    flat_parts = [stacked[k].reshape(-1) for k in keys] + [d_norm_f]
    sizes = [int(t.shape[0]) for t in flat_parts]
    sflat = _pad_rows(jnp.concatenate(flat_parts), SUB * LANES).reshape(-1, LANES)
    sall = _all_gather([sflat], "gather_small_grads", in_vmem=True)[0]
    ssum = _sum_devices(sall).reshape(-1)
    so = [0]
    for sz in sizes:
        so.append(so[-1] + sz)
    red = {k: ssum[so[n]:so[n + 1]].reshape(stacked[k].shape) for n, k in enumerate(keys)}
    g_norm_f = ssum[so[len(keys)]:so[len(keys) + 1]]
    dmod_all = sall[:, 0:sizes[0] // LANES, :].reshape(N_DEV, DEPTH, N_MOD * D)

    g_ada_b = red["dmod"].reshape(DEPTH, N_MOD * D)
    dmod_cols = lax.dynamic_slice_in_dim(dmod_all, me * n_ada, n_ada, axis=2).transpose(1, 0, 2)
    dmod_cols = jnp.concatenate([dmod_cols, jnp.zeros((DEPTH, 16 - N_DEV, n_ada), F32)], axis=1)
    g_ada_w = _mod_bwd(c_all, dmod_cols)
    g_conv_a = lax.dynamic_slice_in_dim(red["conv_a"], me * (AW // N_DEV), AW // N_DEV, axis=2)
    g_conv_qkv = lax.dynamic_slice_in_dim(red["conv_qkv"], me * (3 * H * HD // N_DEV), 3 * H * HD // N_DEV, axis=2)
    g_conv_ff = lax.dynamic_slice_in_dim(red["conv_ff"], me * n_up, n_up, axis=2)

    mine = [jnp.concatenate([_rs_add_chips(staged[i][0][k], staged[i][1][k][:, None], my_chip, "rs_add_chips_" + t)
                             for i in range(DEPTH)], axis=0) for k, t in enumerate(tags)]
    g_w_in = mine[0][:, :, :n_in]
    g_w_out = mine[1]
    g_w_up = mine[2][:, :, :n_up]
    g_w_down = mine[3]

    grads = dict(ada_w=g_ada_w, ada_b=g_ada_b, norm1_w=red["norm1"], w_in=g_w_in, conv_a_w=g_conv_a,
                 norm_a_w=red["norm_a"], conv_qkv_w=g_conv_qkv, a_log=red["a_log"], dt_bias=red["dt_bias"],
                 norm_dn_w=red["norm_dn"], w_out=g_w_out, norm2_w=red["norm2"], w_up=g_w_up, conv_ff_w=g_conv_ff,
                 w_down=g_w_down, norm_f_w=g_norm_f)
    weights = dict(ada_w=ada_w, ada_b=ada_b, norm1_w=norm1_w, w_in=w_in, conv_a_w=conv_a_w, norm_a_w=norm_a_w,
                   conv_qkv_w=conv_qkv_w, a_log=a_log, dt_bias=dt_bias, norm_dn_w=norm_dn_w, w_out=w_out,
                   norm2_w=norm2_w, w_up=w_up, conv_ff_w=conv_ff_w, w_down=w_down, norm_f_w=norm_f_w)
    ms = dict(ada_w=m_ada_w, ada_b=m_ada_b, norm1_w=m_norm1_w, w_in=m_w_in, conv_a_w=m_conv_a_w, norm_a_w=m_norm_a_w,
              conv_qkv_w=m_conv_qkv_w, a_log=m_a_log, dt_bias=m_dt_bias, norm_dn_w=m_norm_dn_w, w_out=m_w_out,
              norm2_w=m_norm2_w, w_up=m_w_up, conv_ff_w=m_conv_ff_w, w_down=m_w_down, norm_f_w=m_norm_f_w)
    vs_ = dict(ada_w=v_ada_w, ada_b=v_ada_b, norm1_w=v_norm1_w, w_in=v_w_in, conv_a_w=v_conv_a_w, norm_a_w=v_norm_a_w,
               conv_qkv_w=v_conv_qkv_w, a_log=v_a_log, dt_bias=v_dt_bias, norm_dn_w=v_norm_dn_w, w_out=v_w_out,
               norm2_w=v_norm2_w, w_up=v_w_up, conv_ff_w=v_conv_ff_w, w_down=v_w_down, norm_f_w=v_norm_f_w)
    names = list(weights)
    big_names = ["ada_w", "w_in", "w_out", "w_up", "w_down"]
    delta, new_m, new_v = {}, {}, {}
    for n in big_names:
        shp = weights[n].shape
        two = lambda t: t.reshape(-1, shp[-1])
        dl, nm, nv = _adamw(two(weights[n]), two(grads[n]), two(ms[n]), two(vs_[n]), "adamw_" + n)
        delta[n], new_m[n], new_v[n] = dl.reshape(shp), nm.reshape(shp), nv.reshape(shp)
    small_names = [n for n in names if n not in big_names]

    def pack(dct):
        return _pad_rows(jnp.concatenate([dct[n].reshape(-1) for n in small_names]), SUB * LANES).reshape(-1, LANES)

    dl, nm, nv = _adamw(pack(weights), pack(grads), pack(ms), pack(vs_), "adamw_small")
    off = 0
    for n in small_names:
        sz, shp = weights[n].size, weights[n].shape
        delta[n] = dl.reshape(-1)[off:off + sz].reshape(shp)
        new_m[n] = nm.reshape(-1)[off:off + sz].reshape(shp)
        new_v[n] = nv.reshape(-1)[off:off + sz].reshape(shp)
        off += sz

    return (loss, grad_x, *[grads[n] for n in names], *[delta[n] for n in names],
            *[new_m[n] for n in names], *[new_v[n] for n in names])
```

```python
import functools
import math

import jax
import jax.numpy as jnp
from jax import lax
from jax.experimental import pallas as pl
from jax.experimental.pallas import tpu as pltpu
from jax.experimental.pallas import tpu_sc as plsc

F32 = jnp.float32
MXU = jnp.bfloat16

D = 1024
DEPTH = 4
N_MOD = 6
AW = 512
A_GROUP = 64
H = 4
HD = 128
CK = 64
DFF = 2816
P_IN = 3592
P_PAD = 3712
EPS = 1e-6
N_DEV = 8
LANES = 128
SUB = 8
VMEM_LIMIT = 56 * 1024 * 1024

ADAM_LR, ADAM_B1, ADAM_B2, ADAM_EPS, ADAM_WD, ADAM_STEP = 0.001, 0.9, 0.999, 1e-08, 0.01, 10

NN = ((1,), (0,))
NT = ((1,), (1,))
TN = ((0,), (0,))
HI = lax.Precision.HIGHEST
MESH = pl.DeviceIdType.MESH


def _dot(a, b, dims, prec=None):
    if prec is None:
        a = a.astype(MXU) if a.dtype == F32 else a
        b = b.astype(MXU) if b.dtype == F32 else b
    return lax.dot_general(a, b, (dims, ((), ())), precision=prec, preferred_element_type=F32)


def _params(n_grid=0, limit=VMEM_LIMIT):
    sem = ("arbitrary",) * n_grid if n_grid else None
    return pltpu.CompilerParams(dimension_semantics=sem, vmem_limit_bytes=limit)


def _tile(n, want):
    if n <= want:
        return n
    t = want - want % SUB
    while n % t:
        t -= SUB
    assert t > 0, (n, want)
    return t


def _full(shape):
    nd = len(shape)
    return pl.BlockSpec(shape, lambda *_: (0,) * nd)


def _sig(x):
    return jax.nn.sigmoid(x)


def _rms(x):
    r = lax.rsqrt(jnp.mean(x * x, axis=-1, keepdims=True) + EPS)
    return x * r, r


def _rms_bwd(dn, n, r):
    return r * (dn - n * jnp.mean(dn * n, axis=-1, keepdims=True))


def _l2_bwd(dn, n, r):
    return r * (dn - n * jnp.sum(dn * n, axis=-1, keepdims=True))


def _sum0(x):
    return jnp.sum(x, axis=0, keepdims=True)


def _shift_down(x, s, halo):
    ext = jnp.concatenate([halo, x], axis=0)
    return pltpu.roll(ext, s, 0)[SUB:, :]


def _shift_up(x, s, halo):
    t = x.shape[0]
    ext = jnp.concatenate([x, halo], axis=0)
    return pltpu.roll(ext, t + SUB - s, 0)[:t, :]


def _conv_fwd(x, w_ref, width, halo):
    sh = [x] + [_shift_down(x, s, halo) for s in range(1, width)]
    out = w_ref[width - 1:width, :] * sh[0]
    for s in range(1, width):
        out = out + w_ref[width - 1 - s:width - s, :] * sh[s]
    return out, sh


def _conv_bwd_in(dout, w_ref, width, halo_next):
    dx = w_ref[width - 1:width, :] * dout
    for s in range(1, width):
        dx = dx + w_ref[width - 1 - s:width - s, :] * _shift_up(dout, s, halo_next)
    return dx


def _blockdiag_mean(n, group):
    r = lax.shift_right_logical(lax.broadcasted_iota(jnp.int32, (n, n), 0), int(math.log2(group)))
    c = lax.shift_right_logical(lax.broadcasted_iota(jnp.int32, (n, n), 1), int(math.log2(group)))
    return jnp.where(r == c, 1.0 / group, 0.0).astype(F32)


def _softplus(x):
    return jnp.maximum(x, 0.0) + jnp.log(1.0 + jnp.exp(-jnp.abs(x)))


def _my_place():
    return lax.axis_index("x"), lax.axis_index("y"), lax.axis_index("c")


def _all_gather(shards, name, in_vmem):
    nt = len(shards)

    def body(*refs):
        x_refs, out_refs = refs[:nt], refs[nt:2 * nt]
        send_sems, recv_sems, local_sems = refs[2 * nt:]
        x, y, c = _my_place()
        me, sibling = (x, y, c), (x, y, 1 - c)
        chips = [(1 - x, y), (x, 1 - y), (1 - x, 1 - y)]
        everything = []
        for t in range(nt):
            x_ref, out_ref = x_refs[t], out_refs[t]

            def blk(px, py, pc, out_ref=out_ref):
                return out_ref.at[4 * px + 2 * py + pc]

            def copy(k, block, to, src=None, t=t, blk=blk):
                return pltpu.make_async_remote_copy(
                    src_ref=blk(*block) if src is None else src, dst_ref=blk(*block),
                    send_sem=send_sems.at[7 * t + k], recv_sem=recv_sems.at[7 * t + k], device_id=to, device_id_type=MESH)

            mine = pltpu.make_async_copy(x_ref, blk(*me), local_sems.at[t])
            mine.start()
            first = [copy(0, me, sibling, src=x_ref)]
            first += [copy(1 + j, me, (*chip, c), src=x_ref) for j, chip in enumerate(chips)]
            for cp in first:
                cp.start()
            everything.append((copy, mine, first))
        sends = []
        for copy, mine, first in everything:
            passed = [copy(4 + j, (*chip, c), sibling) for j, chip in enumerate(chips)]
            for j, chip in enumerate(chips):
                copy(1 + j, (*chip, c), me).wait_recv()
                passed[j].start()
            sends += first + passed
        for copy, mine, first in everything:
            copy(0, sibling, me).wait_recv()
            for j, chip in enumerate(chips):
                copy(4 + j, (*chip, 1 - c), me).wait_recv()
        for cp in sends:
            cp.wait_send()
        for copy, mine, first in everything:
            mine.wait()

    space = pltpu.VMEM if in_vmem else pl.ANY
    return pl.pallas_call(
        body, name=name,
        out_shape=[jax.ShapeDtypeStruct((N_DEV,) + s.shape, s.dtype) for s in shards],
        in_specs=[pl.BlockSpec(memory_space=space)] * nt,
        out_specs=[pl.BlockSpec(memory_space=space)] * nt,
        scratch_shapes=[pltpu.SemaphoreType.DMA((7 * nt,)), pltpu.SemaphoreType.DMA((7 * nt,)),
                        pltpu.SemaphoreType.DMA((nt,))],
        compiler_params=pltpu.CompilerParams(vmem_limit_bytes=VMEM_LIMIT),
    )(*shards)


def _all_gather_async(shards, name, collective_id):
    nt = len(shards)
    hbm = pltpu.MemorySpace.HBM
    x_refs = [jax.new_ref(s, memory_space=hbm) for s in shards]
    out_refs = [jax.empty_ref(jax.ShapeDtypeStruct((N_DEV,) + s.shape, s.dtype), memory_space=hbm) for s in shards]

    @pl.kernel(mesh=plsc.ScalarSubcoreMesh(axis_name="sequencer", num_cores=1), name=name,
               scratch_types=(pltpu.SemaphoreType.DMA((7 * nt,)), pltpu.SemaphoreType.DMA((7 * nt,)),
                              pltpu.SemaphoreType.DMA((nt,))),
               compiler_params=pltpu.CompilerParams(collective_id=collective_id))
    def launch(send_sems, recv_sems, local_sems):
        x, y, c = _my_place()
        me, sibling = (x, y, c), (x, y, 1 - c)
        chips = [(1 - x, y), (x, 1 - y), (1 - x, 1 - y)]
        barrier = pltpu.get_barrier_semaphore()
        for peer in [sibling] + [(*chip, c) for chip in chips]:
            pl.semaphore_signal(barrier, inc=1, device_id=peer, device_id_type=MESH)
        pl.semaphore_wait(barrier, 4)
        everything = []
        for t in range(nt):
            x_ref, out_ref = x_refs[t], out_refs[t]

            def blk(px, py, pc, out_ref=out_ref):
                return out_ref.at[4 * px + 2 * py + pc]

            def copy(k, block, to, src=None, t=t, blk=blk):
                return pltpu.make_async_remote_copy(
                    src_ref=blk(*block) if src is None else src, dst_ref=blk(*block),
                    send_sem=send_sems.at[7 * t + k], recv_sem=recv_sems.at[7 * t + k], device_id=to, device_id_type=MESH)

            mine = pltpu.make_async_copy(x_ref, blk(*me), local_sems.at[t])
            mine.start()
            first = [copy(0, me, sibling, src=x_ref)]
            first += [copy(1 + j, me, (*chip, c), src=x_ref) for j, chip in enumerate(chips)]
            for cp in first:
                cp.start()
            everything.append((copy, mine, first))
        sends = []
        for copy, mine, first in everything:
            passed = [copy(4 + j, (*chip, c), sibling) for j, chip in enumerate(chips)]
            for j, chip in enumerate(chips):
                copy(1 + j, (*chip, c), me).wait_recv()
                passed[j].start()
            sends += first + passed
        for copy, mine, first in everything:
            copy(0, sibling, me).wait_recv()
            for j, chip in enumerate(chips):
                copy(4 + j, (*chip, 1 - c), me).wait_recv()
        for cp in sends:
            cp.wait_send()
        for copy, mine, first in everything:
            mine.wait()

    launch()
    return [r[...] for r in out_refs]


def _rs_exchange_async(srcs, name, collective_id):
    nt = len(srcs)
    hbm = pltpu.MemorySpace.HBM
    src_refs = [jax.new_ref(s, memory_space=hbm) for s in srcs]
    out_refs = [jax.empty_ref(jax.ShapeDtypeStruct(s.shape, s.dtype), memory_space=hbm) for s in srcs]
    flips = [(fx, fy, fc) for fx in (0, 1) for fy in (0, 1) for fc in (0, 1)][1:]

    @pl.kernel(mesh=plsc.ScalarSubcoreMesh(axis_name="sequencer", num_cores=1), name=name,
               scratch_types=(pltpu.SemaphoreType.DMA((7 * nt,)), pltpu.SemaphoreType.DMA((7 * nt,)),
                              pltpu.SemaphoreType.DMA((nt,))),
               compiler_params=pltpu.CompilerParams(collective_id=collective_id))
    def launch(send_sems, recv_sems, local_sems):
        x, y, c = _my_place()
        me = 4 * x + 2 * y + c
        peers = [(1 - x if fx else x, 1 - y if fy else y, 1 - c if fc else c) for fx, fy, fc in flips]
        barrier = pltpu.get_barrier_semaphore()
        for peer in peers:
            pl.semaphore_signal(barrier, inc=1, device_id=peer, device_id_type=MESH)
        pl.semaphore_wait(barrier, len(peers))
        own = [pltpu.make_async_copy(src_refs[t].at[me], out_refs[t].at[me], local_sems.at[t]) for t in range(nt)]
        copies = [pltpu.make_async_remote_copy(
            src_ref=src_refs[t].at[4 * px + 2 * py + pc], dst_ref=out_refs[t].at[me],
            send_sem=send_sems.at[7 * t + f], recv_sem=recv_sems.at[7 * t + f],
            device_id=(px, py, pc), device_id_type=MESH) for t in range(nt) for f, (px, py, pc) in enumerate(peers)]
        for cp in own + copies:
            cp.start()
        for cp in copies + own:
            cp.wait()

    launch()
    return [r[...] for r in out_refs]


def _rs_sum(recv, name):
    _, r, n = recv.shape
    tr = _tile(r, 512)

    def body(r_ref, o_ref):
        s = r_ref[0].astype(F32)
        for k in range(1, N_DEV):
            s = s + r_ref[k].astype(F32)
        o_ref[...] = s

    return pl.pallas_call(
        body, name=name, grid=(r // tr,),
        in_specs=[pl.BlockSpec((N_DEV, tr, n), lambda i: (0, i, 0))],
        out_specs=pl.BlockSpec((tr, n), lambda i: (i, 0)),
        out_shape=jax.ShapeDtypeStruct((r, n), F32), compiler_params=_params(1),
    )(recv)


def _shard_windows(n_shard, count, first=0):
    out = []
    for k in range(first, first + count):
        off = n_shard * k
        a, s = off // LANES, off % LANES
        out.append((a, s, -(-(s + n_shard) // LANES) * LANES))
    return out


def _fit_lanes(x, width):
    have = x.shape[1]
    if have < width:
        return jnp.concatenate([x, jnp.zeros((x.shape[0], width - have), x.dtype)], axis=-1)
    return x[:, :width]


def _interleave_cols(g, n_shard, w_out, name):
    nd, nl, rows, wpad = g.shape
    rb = _tile(rows, 256)
    wins = _shard_windows(n_shard, nd)

    def body(g_ref, o_ref, acc):
        acc[...] = jnp.zeros_like(acc)
        for k, (a, s, win) in enumerate(wins):
            xk = _fit_lanes(g_ref[k].astype(F32), win)
            if s:
                xk = pltpu.roll(xk, s, 1)
            acc[:, a * LANES:a * LANES + win] += xk
        o_ref[...] = acc[...].astype(o_ref.dtype)

    return pl.pallas_call(
        body, name=name, grid=(nl, rows // rb),
        in_specs=[pl.BlockSpec((nd, None, rb, wpad), lambda l, i: (0, l, i, 0))],
        out_specs=pl.BlockSpec((None, rb, w_out), lambda l, i: (l, i, 0)),
        out_shape=jax.ShapeDtypeStruct((nl, rows, w_out), g.dtype),
        scratch_shapes=[pltpu.VMEM((rb, w_out), F32)],
        compiler_params=_params(2),
    )(g)


def _sum_devices(g):
    _, r, n = g.shape

    def body(g_ref, o_ref):
        s = g_ref[0]
        for t in range(1, N_DEV):
            s = s + g_ref[t]
        o_ref[...] = s

    return pl.pallas_call(
        body, name="sum_devices", out_shape=jax.ShapeDtypeStruct((r, n), F32),
        in_specs=[pl.BlockSpec(memory_space=pltpu.VMEM)], out_specs=pl.BlockSpec(memory_space=pltpu.VMEM),
        compiler_params=pltpu.CompilerParams(vmem_limit_bytes=VMEM_LIMIT),
    )(g)


def _mod_fwd(c_all, ada_w, ada_b_cols):
    nl, _, nc = ada_w.shape

    def body(c_ref, w_ref, b_ref, o_ref):
        cv = c_ref[...]
        act = (cv * _sig(cv)).astype(MXU)
        o_ref[...] = _dot(act, w_ref[...].astype(MXU), NN) + b_ref[...]

    return pl.pallas_call(
        body, name="mod_fwd", grid=(nl,),
        in_specs=[_full((16, D)), pl.BlockSpec((None, D, nc), lambda i: (i, 0, 0)),
                  pl.BlockSpec((None, 1, nc), lambda i: (i, 0, 0))],
        out_specs=pl.BlockSpec((None, 16, nc), lambda i: (i, 0, 0)),
        out_shape=jax.ShapeDtypeStruct((nl, 16, nc), F32), compiler_params=_params(1),
    )(c_all, ada_w, ada_b_cols)


def _mod_bwd(c_all, dmod_cols):
    nl, _, nc = dmod_cols.shape

    def body(c_ref, d_ref, o_ref):
        cv = c_ref[...]
        act = (cv * _sig(cv)).astype(MXU)
        o_ref[...] = _dot(act, d_ref[...].astype(MXU), TN)

    return pl.pallas_call(
        body, name="mod_bwd", grid=(nl,),
        in_specs=[_full((16, D)), pl.BlockSpec((None, 16, nc), lambda i: (i, 0, 0))],
        out_specs=pl.BlockSpec((None, D, nc), lambda i: (i, 0, 0)),
        out_shape=jax.ShapeDtypeStruct((nl, D, nc), F32), compiler_params=_params(1),
    )(c_all, dmod_cols)


def _in_proj(x, modrows, vec, w_in):
    L = x.shape[0]
    T = _tile(L, 256)

    def body(x_ref, mod_ref, vec_ref, w_ref, p_ref, h_ref):
        n, _ = _rms(x_ref[...])
        h = n * vec_ref[0:1, :] * (1.0 + mod_ref[1:2, :]) + mod_ref[0:1, :]
        hb = h.astype(MXU)
        h_ref[...] = hb
        p_ref[...] = _dot(hb, w_ref[...], NN)

    return pl.pallas_call(
        body, name="in_proj", grid=(L // T,),
        in_specs=[pl.BlockSpec((T, D), lambda i: (i, 0)), _full((SUB, D)), _full((SUB, D)), _full((D, P_PAD))],
        out_specs=[pl.BlockSpec((T, P_PAD), lambda i: (i, 0)), pl.BlockSpec((T, D), lambda i: (i, 0))],
        out_shape=[jax.ShapeDtypeStruct((L, P_PAD), F32), jax.ShapeDtypeStruct((L, D), MXU)],
        compiler_params=_params(1),
    )(x, modrows, vec, w_in)


def _gate_small(s, sp_ref):
    lane = lax.broadcasted_iota(jnp.int32, s.shape, 1)
    a = -jnp.exp(sp_ref[0:1, :])
    xb = s + sp_ref[1:2, :]
    beta = _sig(s)
    g = a * _softplus(xb)
    return lane, a, xb, beta, g


def _pre_fwd(p, pa, cq, sp):
    L = p.shape[0]
    T = _tile(L, 256)
    scale = HD ** -0.5

    def body(pm_ref, ps_ref, pa_ref, cq_ref, sp_ref, qn_ref, kn_ref, vs_ref, gb_ref, ya_ref, u_carry, q_carry):
        @pl.when(pl.program_id(0) == 0)
        def _():
            u_carry[...] = jnp.zeros_like(u_carry)
            q_carry[...] = jnp.zeros_like(q_carry)

        a_b = pm_ref[:, 0:AW]
        u = pm_ref[:, AW:2 * AW] * pm_ref[:, 2 * AW:3 * AW]
        cu, _ = _conv_fwd(u, pa_ref, 3, u_carry[...])
        u_carry[...] = u[T - SUB:T, :]
        yp = a_b * cu
        ms = _dot_f32(yp * yp, _blockdiag_mean(AW, A_GROUP), NN, exact="b")
        ya_ref[...] = (yp * lax.rsqrt(ms + EPS) * pa_ref[3:4, :]).astype(MXU)

        qkv = pm_ref[:, 3 * AW:3 * AW + 3 * H * HD]
        qc, _ = _conv_fwd(qkv, cq_ref, 4, q_carry[...])
        q_carry[...] = qkv[T - SUB:T, :]
        qs = qc * _sig(qc)
        for h in range(H):
            q = qs[:, h * HD:(h + 1) * HD]
            qn_ref[:, h * HD:(h + 1) * HD] = q * (lax.rsqrt(jnp.sum(q * q, axis=-1, keepdims=True) + EPS) * scale)
            k = qs[:, (H + h) * HD:(H + h + 1) * HD]
            kn_ref[:, h * HD:(h + 1) * HD] = k * lax.rsqrt(jnp.sum(k * k, axis=-1, keepdims=True) + EPS)
        vs_ref[...] = qs[:, 2 * H * HD:3 * H * HD]

        lane, _, _, beta, g = _gate_small(ps_ref[...], sp_ref)
        gb_ref[...] = jnp.where(lane < H, beta, jnp.where(lane < 2 * H, g, 0.0))

    w3 = 3 * AW + 3 * H * HD
    row = lambda i: (i, 0)
    return pl.pallas_call(
        body, name="pre_fwd", grid=(L // T,),
        in_specs=[pl.BlockSpec((T, w3), row), pl.BlockSpec((T, LANES), lambda i: (i, (P_PAD - LANES) // LANES)),
                  _full((SUB, AW)), _full((SUB, 3 * H * HD)), _full((SUB, LANES))],
        out_specs=[pl.BlockSpec((T, H * HD), row)] * 3 + [pl.BlockSpec((T, LANES), row), pl.BlockSpec((T, AW), row)],
        out_shape=[jax.ShapeDtypeStruct((L, H * HD), F32)] * 3
        + [jax.ShapeDtypeStruct((L, LANES), F32), jax.ShapeDtypeStruct((L, AW), MXU)],
        scratch_shapes=[pltpu.VMEM((SUB, AW), F32), pltpu.VMEM((SUB, 3 * H * HD), F32)],
        compiler_params=_params(1),
    )(p, p, pa, cq, sp)


def _gdr_masks():
    r = lax.broadcasted_iota(jnp.int32, (CK, CK), 0)
    c = lax.broadcasted_iota(jnp.int32, (CK, CK), 1)
    return r >= c, r > c


def _head_cols(gbt, h):
    return gbt[:, h:h + 1], gbt[:, H + h:H + h + 1]


def _split(x, parts):
    out = []
    for _ in range(parts):
        hi = x.astype(jnp.bfloat16)
        out.append(hi)
        x = x - hi.astype(F32)
    return out


def _dot_f32(a, b, dims, exact=None):
    if exact == "a":
        ab = a.astype(jnp.bfloat16)
        return sum(_dot(ab, t, dims) for t in _split(b, 3))
    if exact == "b":
        bb = b.astype(jnp.bfloat16)
        return sum(_dot(t, bb, dims) for t in _split(a, 3))
    ah, al = _split(a, 2)
    bh, bl = _split(b, 2)
    return _dot(ah, bh, dims) + _dot(ah, bl, dims) + _dot(al, bh, dims)


def _gdr_consts():
    causal, strict = _gdr_masks()
    return dict(causal=causal, strict=strict, tril=jnp.where(causal, 1.0, 0.0).astype(F32),
                eye=jnp.where(causal & jnp.logical_not(strict), 1.0, 0.0).astype(F32),
                bcast=jnp.full((CK, HD), 1.0 / HD, F32))


def _dots(a, b, dims):
    return [_dot(x, y, dims) for x, y in zip(a, b)]


def _dots_f32(a, b, dims, exact=None):
    n = len(a)
    if exact == "a":
        lhs = [[x.astype(jnp.bfloat16)] * 3 for x in a]
        rhs = [_split(y, 3) for y in b]
    elif exact == "b":
        lhs = [_split(x, 3) for x in a]
        rhs = [[y.astype(jnp.bfloat16)] * 3 for y in b]
    else:
        sa = [_split(x, 2) for x in a]
        sb = [_split(y, 2) for y in b]
        lhs = [[s[0], s[0], s[1]] for s in sa]
        rhs = [[s[0], s[1], s[0]] for s in sb]
    terms = [[_dot(lhs[i][t], rhs[i][t], dims) for i in range(n)] for t in range(3)]
    return [terms[0][i] + terms[1][i] + terms[2][i] for i in range(n)]


def _gdr_local(q, k, v, beta, g, cst, tinv=None):
    n = len(q)
    R = range(n)
    causal, strict = cst["causal"], cst["strict"]
    gc = _dots_f32([cst["tril"]] * n, [jnp.broadcast_to(g[i], (CK, HD)) for i in R], NN, exact="a")
    g_row = _dots_f32([cst["bcast"]] * n, gc, NT, exact="a")
    decay = [jnp.where(causal, jnp.exp(jnp.where(causal, gc[i][:, 0:CK] - g_row[i], 0.0)), 0.0) for i in R]
    eg = [jnp.exp(gc[i]) for i in R]
    gl = [gc[i][CK - 1:CK, :] for i in R]
    ek = [jnp.exp(gl[i] - gc[i]) for i in R]
    cd = [jnp.exp(gl[i]) for i in R]
    kb = [k[i] * beta[i] for i in R]
    pk = _dots(kb, k, NT)
    if tinv is None:
        xp = [-jnp.where(strict, pk[i] * decay[i], 0.0) for i in R]
        tinv = [cst["eye"] + xp[i] for i in R]
        for _ in range(5):
            xp = _dots_f32(xp, xp, NN)
            tx = _dots_f32(tinv, xp, NN)
            tinv = [tinv[i] + tx[i] for i in R]
    u = _dots(tinv, [v[i] * beta[i] for i in R], NN)
    w = _dots(tinv, [kb[i] * eg[i] for i in R], NN)
    qk = _dots(q, k, NT)
    intra = [jnp.where(causal, qk[i] * decay[i], 0.0) for i in R]
    return dict(decay=decay, eg=eg, ek=ek, cd=cd, kb=kb, pk=pk, tinv=tinv, u=u, w=w, qk=qk, intra=intra,
                q_dec=[q[i] * eg[i] for i in R], k_dec=[k[i] * ek[i] for i in R])


GDR_SUB = 4


def _gdr_fwd(qn, kn, vs, gb):
    L = qn.shape[0]
    nc = L // CK
    cb = min(8, nc)
    rb = cb * CK
    nb = nc // cb
    nsub = GDR_SUB if cb % GDR_SUB == 0 else 1

    def body(q_ref, k_ref, v_ref, gb_ref, o_ref, st_ref, ti_ref, s_ref):
        @pl.when(pl.program_id(0) == 0)
        def _():
            s_ref[...] = jnp.zeros_like(s_ref)

        cst = _gdr_consts()
        heads = range(H)

        def group(gi, carry):
            rows = [pl.ds(pl.multiple_of((gi * nsub + j) * CK, CK), CK) for j in range(nsub)]
            chains = [(j, h) for j in range(nsub) for h in heads]
            gbt = [gb_ref[rows[j], :] for j in range(nsub)]
            cols = lambda h: slice(h * HD, (h + 1) * HD)
            t = _gdr_local([q_ref[rows[j], cols(h)] for j, h in chains], [k_ref[rows[j], cols(h)] for j, h in chains],
                           [v_ref[rows[j], cols(h)] for j, h in chains],
                           [_head_cols(gbt[j], h)[0] for j, h in chains], [_head_cols(gbt[j], h)[1] for j, h in chains], cst)
            s = [s_ref[h] for h in heads]
            for j in range(nsub):
                at = lambda key: [t[key][j * H + h] for h in heads]
                for h in heads:
                    st_ref[h, gi * nsub + j] = s[h]
                    ti_ref[h, gi * nsub + j] = t["tinv"][j * H + h]
                ws = _dots(at("w"), s, NN)
                v_new = [u_h - ws_h for u_h, ws_h in zip(at("u"), ws)]
                o_s = _dots(at("q_dec"), s, NN)
                o_v = _dots(at("intra"), v_new, NN)
                kv = _dots(at("k_dec"), v_new, TN)
                cd = at("cd")
                for h in heads:
                    o_ref[rows[j], cols(h)] = o_s[h] + o_v[h]
                s = [s[h] * cd[h] + kv[h] for h in heads]
            for h in heads:
                s_ref[h] = s[h]
            return carry

        lax.fori_loop(0, cb // nsub, group, 0)

    blk = pl.BlockSpec((rb, H * HD), lambda b: (b, 0))
    return pl.pallas_call(
        body, name="gdr_fwd", grid=(nb,),
        in_specs=[blk, blk, blk, pl.BlockSpec((rb, LANES), lambda b: (b, 0))],
        out_specs=[blk, pl.BlockSpec((H, cb, HD, HD), lambda b: (0, b, 0, 0)),
                   pl.BlockSpec((H, cb, CK, CK), lambda b: (0, b, 0, 0))],
        out_shape=[jax.ShapeDtypeStruct((L, H * HD), F32), jax.ShapeDtypeStruct((H, nc, HD, HD), F32),
                   jax.ShapeDtypeStruct((H, nc, CK, CK), F32)],
        scratch_shapes=[pltpu.VMEM((H, HD, HD), F32)],
        compiler_params=_params(1),
    )(qn, kn, vs, gb)


def _gdr_bwd(qn, kn, vs, gb, states, tinvs, do):
    L = qn.shape[0]
    nc = L // CK
    cb = min(8, nc)
    rb = cb * CK
    nb = nc // cb
    nsub = GDR_SUB if cb % GDR_SUB == 0 else 1

    def body(q_ref, k_ref, v_ref, gb_ref, st_ref, ti_ref, do_ref, dq_ref, dk_ref, dv_ref, dgb_ref, ds_ref):
        @pl.when(pl.program_id(0) == 0)
        def _():
            ds_ref[...] = jnp.zeros_like(ds_ref)

        cst = _gdr_consts()
        causal, strict = cst["causal"], cst["strict"]
        ones = jnp.ones((CK, HD), F32)
        row = lax.broadcasted_iota(jnp.int32, (CK, HD), 0)
        lane = lax.broadcasted_iota(jnp.int32, (CK, LANES), 1)

        heads = range(H)
        rsum = lambda x: jnp.sum(x, axis=-1, keepdims=True)

        def group(gj, carry):
            gi = cb // nsub - 1 - gj
            rows = [pl.ds(pl.multiple_of((gi * nsub + j) * CK, CK), CK) for j in range(nsub)]
            chains = [(j, h) for j in range(nsub) for h in heads]
            gbt = [gb_ref[rows[j], :] for j in range(nsub)]
            cols = lambda h: slice(h * HD, (h + 1) * HD)
            q_all = [q_ref[rows[j], cols(h)] for j, h in chains]
            k_all = [k_ref[rows[j], cols(h)] for j, h in chains]
            v_all = [v_ref[rows[j], cols(h)] for j, h in chains]
            beta_all = [_head_cols(gbt[j], h)[0] for j, h in chains]
            t = _gdr_local(q_all, k_all, v_all, beta_all, [_head_cols(gbt[j], h)[1] for j, h in chains], cst,
                           tinv=[ti_ref[h, gi * nsub + j] for j, h in chains])
            ds_out = [ds_ref[h] for h in heads]
            for j in reversed(range(nsub)):
                at = lambda key: [t[key][j * H + h] for h in heads]
                pick = lambda lst: [lst[j * H + h] for h in heads]
                q, k, v, beta = pick(q_all), pick(k_all), pick(v_all), pick(beta_all)
                u, w, tinv, decay = at("u"), at("w"), at("tinv"), at("decay")
                eg, ek, cd, kb = at("eg"), at("ek"), at("cd"), at("kb")
                q_dec, k_dec, intra, pk, qk = at("q_dec"), at("k_dec"), at("intra"), at("pk"), at("qk")
                s = [st_ref[h, gi * nsub + j] for h in heads]
                dout = [do_ref[rows[j], cols(h)] for h in heads]

                ws = _dots(w, s, NN)
                v_new = [u[h] - ws[h] for h in heads]
                dq_dec = _dots(dout, s, NT)
                qd = _dots(q_dec, dout, TN)
                di = _dots(dout, v_new, NT)
                dintra = [jnp.where(causal, di[h], 0.0) for h in heads]
                ido = _dots(intra, dout, TN)
                kds = _dots(k_dec, ds_out, NN)
                dv_new = [ido[h] + kds[h] for h in heads]
                dk_dec = _dots(v_new, ds_out, NT)
                dcd = [jnp.sum(jnp.sum(ds_out[h] * s[h], axis=1, keepdims=True), axis=0, keepdims=True) for h in heads]
                dvs = _dots(dv_new, s, NT)
                dw = [-dvs[h] for h in heads]
                wdv = _dots(w, dv_new, TN)
                ds_new = [qd[h] + ds_out[h] * cd[h] - wdv[h] for h in heads]
                dru = _dots(tinv, dv_new, TN)
                drw = _dots(tinv, dw, TN)
                dl1 = _dots(dru, u, NT)
                dl2 = _dots(drw, w, NT)
                dlower = [-jnp.where(strict, dl1[h] + dl2[h], 0.0) for h in heads]
                dv = [dru[h] * beta[h] for h in heads]
                dbeta = [rsum(dru[h] * v[h]) for h in heads]
                dgc = [rsum(drw[h] * kb[h]) * eg[h] for h in heads]
                dpk = [dlower[h] * decay[h] for h in heads]
                dqk = [dintra[h] * decay[h] for h in heads]
                dpk_k = _dots(dpk, k, NN)
                dkb = [drw[h] * eg[h] + dpk_k[h] for h in heads]
                dk1 = _dots(dpk, kb, TN)
                dq1 = _dots(dqk, k, NN)
                dk2 = _dots(dqk, q, TN)
                m = [(dlower[h] * pk[h] + dintra[h] * qk[h]) * decay[h] for h in heads]
                mcol = _dots_f32(m, [ones] * H, TN, exact="b")
                e = [rsum(dk_dec[h] * k_dec[h]) for h in heads]
                dgl = [jnp.sum(e[h], axis=0, keepdims=True) + dcd[h] * cd[h] for h in heads]
                dgc = [dgc[h] + rsum(m[h]) - mcol[h] + rsum(dq_dec[h] * q_dec[h]) - e[h]
                       + jnp.where(row == CK - 1, dgl[h], 0.0) for h in heads]
                dg = _dots_f32([cst["tril"]] * H, dgc, TN, exact="a")
                dgb = jnp.zeros((CK, LANES), F32)
                for h in heads:
                    dq_ref[rows[j], cols(h)] = dq1[h] + dq_dec[h] * eg[h]
                    dk_ref[rows[j], cols(h)] = dk1[h] + dk2[h] + dk_dec[h] * ek[h] + dkb[h] * beta[h]
                    dv_ref[rows[j], cols(h)] = dv[h]
                    db = dbeta[h] + rsum(dkb[h] * k[h])
                    dgb = dgb + jnp.where(lane == h, db, 0.0) + jnp.where(lane == H + h, dg[h], 0.0)
                dgb_ref[rows[j], :] = dgb
                ds_out = ds_new
            for h in heads:
                ds_ref[h] = ds_out[h]
            return carry

        lax.fori_loop(0, cb // nsub, group, 0)

    blk = pl.BlockSpec((rb, H * HD), lambda b: (nb - 1 - b, 0))
    sblk = pl.BlockSpec((rb, LANES), lambda b: (nb - 1 - b, 0))
    return pl.pallas_call(
        body, name="gdr_bwd", grid=(nb,),
        in_specs=[blk, blk, blk, sblk, pl.BlockSpec((H, cb, HD, HD), lambda b: (0, nb - 1 - b, 0, 0)),
                  pl.BlockSpec((H, cb, CK, CK), lambda b: (0, nb - 1 - b, 0, 0)), blk],
        out_specs=[blk, blk, blk, sblk],
        out_shape=[jax.ShapeDtypeStruct((L, H * HD), F32)] * 3 + [jax.ShapeDtypeStruct((L, LANES), F32)],
        scratch_shapes=[pltpu.VMEM((H, HD, HD), F32)],
        compiler_params=_params(1),
    )(qn, kn, vs, gb, states, tinvs, do)


def _post_fwd(o, p, ya, x, modrows, sp, w_out):
    L = x.shape[0]
    T = _tile(L, 256)

    def body(o_ref, z_ref, ya_ref, x_ref, mod_ref, sp_ref, w_ref, y_ref, x2_ref, yb_ref):
        ndw = sp_ref[2:3, :]
        z = z_ref[...]
        sz = z * _sig(z)
        parts = []
        for h in range(H):
            n, _ = _rms(o_ref[:, h * HD:(h + 1) * HD])
            parts.append(n * ndw * sz[:, h * HD:(h + 1) * HD])
        yb = jnp.concatenate(parts, axis=-1).astype(MXU)
        yb_ref[...] = yb
        y = _dot(ya_ref[...], w_ref[0:AW, :], NN) + _dot(yb, w_ref[AW:2 * AW, :], NN)
        y_ref[...] = y
        x2_ref[...] = x_ref[...] + mod_ref[2:3, :] * y

    row = lambda i: (i, 0)
    zcol = (3 * AW + 3 * H * HD) // (H * HD)
    return pl.pallas_call(
        body, name="post_fwd", grid=(L // T,),
        in_specs=[pl.BlockSpec((T, H * HD), row), pl.BlockSpec((T, H * HD), lambda i: (i, zcol)),
                  pl.BlockSpec((T, AW), row), pl.BlockSpec((T, D), row), _full((SUB, D)), _full((SUB, LANES)),
                  _full((D, D))],
        out_specs=[pl.BlockSpec((T, D), row), pl.BlockSpec((T, D), row), pl.BlockSpec((T, H * HD), row)],
        out_shape=[jax.ShapeDtypeStruct((L, D), F32), jax.ShapeDtypeStruct((L, D), F32),
                   jax.ShapeDtypeStruct((L, H * HD), MXU)],
        compiler_params=_params(1),
    )(o, p, ya, x, modrows, sp, w_out)


FF_COLS = 2
FF_CW = DFF // FF_COLS
FF_ROWS = 256


def _ffn_fwd_half(x2, modrows, vec, w_up, cff, w_down, j, d_prev):
    assert FF_COLS == 2
    L = x2.shape[0]
    T = _tile(L, FF_ROWS)
    nj = FF_COLS
    last = d_prev is not None

    def body(*refs):
        x_ref, mod_ref, vec_ref, wg_ref, wu_ref, cg_ref, cu_ref, wd_ref = refs[:8]
        if last:
            dp_ref, gp_ref, up_ref, f_ref, d_ref, x3_ref, carry_g, carry_u = refs[8:]
        else:
            h_ref, gp_ref, up_ref, f_ref, d_ref, carry_g, carry_u = refs[8:]

        @pl.when(pl.program_id(0) == 0)
        def _():
            carry_g[...] = jnp.zeros_like(carry_g)
            carry_u[...] = jnp.zeros_like(carry_u)

        xv = x_ref[...]
        n, _ = _rms(xv)
        hb = (n * vec_ref[1:2, :] * (1.0 + mod_ref[4:5, :]) + mod_ref[3:4, :]).astype(MXU)
        if not last:
            h_ref[...] = hb
        g = _dot(hb, wg_ref[...], NN)
        u = _dot(hb, wu_ref[...], NN)
        gp_ref[...] = g
        up_ref[...] = u
        gc, _ = _conv_fwd(g, cg_ref, 3, carry_g[...])
        uc, _ = _conv_fwd(u, cu_ref, 3, carry_u[...])
        carry_g[...] = g[T - SUB:T, :]
        carry_u[...] = u[T - SUB:T, :]
        fb = (gc * _sig(gc) * uc).astype(MXU)
        f_ref[...] = fb
        part = _dot(fb, wd_ref[...], NN)
        if last:
            dv = dp_ref[...] + part
            d_ref[...] = dv
            x3_ref[...] = xv + mod_ref[5:6, :] * dv
        else:
            d_ref[...] = part

    row = lambda i: (i, 0)
    rowD = pl.BlockSpec((T, D), row)
    rowC = pl.BlockSpec((T, FF_CW), row)
    in_specs = [rowD, _full((SUB, D)), _full((SUB, D)),
                pl.BlockSpec((D, FF_CW), lambda i: (0, j)), pl.BlockSpec((D, FF_CW), lambda i: (0, nj + j)),
                pl.BlockSpec((SUB, FF_CW), lambda i: (0, j)), pl.BlockSpec((SUB, FF_CW), lambda i: (0, nj + j)),
                pl.BlockSpec((FF_CW, D), lambda i: (j, 0))]
    half = [jax.ShapeDtypeStruct((L, FF_CW), F32), jax.ShapeDtypeStruct((L, FF_CW), F32),
            jax.ShapeDtypeStruct((L, FF_CW), MXU)]
    args = [x2, modrows, vec, w_up, w_up, cff, cff, w_down]
    if last:
        in_specs.append(rowD)
        args.append(d_prev)
        out_specs = [rowC, rowC, rowC, rowD, rowD]
        out_shape = half + [jax.ShapeDtypeStruct((L, D), F32), jax.ShapeDtypeStruct((L, D), F32)]
    else:
        out_specs = [rowD, rowC, rowC, rowC, rowD]
        out_shape = [jax.ShapeDtypeStruct((L, D), MXU)] + half + [jax.ShapeDtypeStruct((L, D), F32)]
    return pl.pallas_call(
        body, name="ffn_fwd_last" if last else "ffn_fwd_first", grid=(L // T,),
        in_specs=in_specs, out_specs=out_specs, out_shape=out_shape,
        scratch_shapes=[pltpu.VMEM((SUB, FF_CW), F32), pltpu.VMEM((SUB, FF_CW), F32)],
        compiler_params=_params(1),
    )(*args)


def _ffn_bwd_half(dx3, modrows, gpre, upre, cff, w_down, w_up, j, tail):
    assert FF_COLS == 2
    L = dx3.shape[0]
    T = _tile(L, FF_ROWS)
    ni, nj = L // T, FF_COLS
    hb_per_t = T // SUB
    last = tail is not None

    def body(*refs):
        dx3_ref, mod_ref, gp_ref, up_ref, gph_ref, uph_ref, cg_ref, cu_ref, wd_ref, wg_ref, wu_ref = refs[:11]
        if last:
            (d_ref, x2_ref, vec_ref, dhp_ref, dgp_ref, dup_ref, dx2_ref, accv_ref, dcg_ref, dcu_ref,
             carry_g, carry_u) = refs[11:]
        else:
            dd_ref, dgp_ref, dup_ref, dh_ref, dcg_ref, dcu_ref, carry_g, carry_u = refs[11:]
        i = pl.program_id(0)
        ri = ni - 1 - i

        @pl.when(i == 0)
        def _():
            carry_g[...] = jnp.zeros_like(carry_g)
            carry_u[...] = jnp.zeros_like(carry_u)
            dcg_ref[...] = jnp.zeros_like(dcg_ref)
            dcu_ref[...] = jnp.zeros_like(dcu_ref)
            if last:
                accv_ref[...] = jnp.zeros_like(accv_ref)

        dx3v = dx3_ref[...]
        ddb = (mod_ref[5:6, :] * dx3v).astype(MXU)
        if not last:
            dd_ref[...] = ddb
        g, u = gp_ref[...], up_ref[...]
        keep = jnp.where(ri == 0, 0.0, 1.0)
        gc, gsh = _conv_fwd(g, cg_ref, 3, gph_ref[...] * keep)
        uc, ush = _conv_fwd(u, cu_ref, 3, uph_ref[...] * keep)
        sg = _sig(gc)
        df = _dot(ddb, wd_ref[...], NT)
        duc = df * (gc * sg)
        dgc = df * uc * (sg * (1.0 + gc * (1.0 - sg)))
        for s in range(3):
            dcg_ref[2 - s:3 - s, :] += _sum0(dgc * gsh[s])
            dcu_ref[2 - s:3 - s, :] += _sum0(duc * ush[s])
        dg = _conv_bwd_in(dgc, cg_ref, 3, carry_g[...]).astype(MXU)
        du = _conv_bwd_in(duc, cu_ref, 3, carry_u[...]).astype(MXU)
        carry_g[...] = dgc[0:SUB, :]
        carry_u[...] = duc[0:SUB, :]
        dgp_ref[...] = dg
        dup_ref[...] = du
        dh = _dot(dg, wg_ref[...], NT) + _dot(du, wu_ref[...], NT)
        if last:
            dh = dh + dhp_ref[...]
            accv_ref[0:1, :] += _sum0(dx3v * d_ref[...])
            n, r = _rms(x2_ref[...])
            nw, sc = vec_ref[1:2, :], mod_ref[4:5, :]
            accv_ref[1:2, :] += _sum0(dh)
            accv_ref[2:3, :] += _sum0(dh * n * nw)
            accv_ref[3:4, :] += _sum0(dh * n * (1.0 + sc))
            dx2_ref[...] = _rms_bwd(dh * nw * (1.0 + sc), n, r) + dx3v
        else:
            dh_ref[...] = dh

    row = lambda i: (ni - 1 - i, 0)
    halo = lambda i: (jnp.maximum((ni - 1 - i) * hb_per_t - 1, 0), 0)
    rowD = pl.BlockSpec((T, D), row)
    rowC = pl.BlockSpec((T, FF_CW), row)
    haloC = pl.BlockSpec((SUB, FF_CW), halo)
    in_specs = [rowD, _full((SUB, D)), rowC, rowC, haloC, haloC,
                pl.BlockSpec((SUB, FF_CW), lambda i: (0, j)), pl.BlockSpec((SUB, FF_CW), lambda i: (0, nj + j)),
                pl.BlockSpec((FF_CW, D), lambda i: (j, 0)),
                pl.BlockSpec((D, FF_CW), lambda i: (0, j)), pl.BlockSpec((D, FF_CW), lambda i: (0, nj + j))]
    args = [dx3, modrows, gpre, upre, gpre, upre, cff, cff, w_down, w_up, w_up]
    halfb = [jax.ShapeDtypeStruct((L, FF_CW), MXU), jax.ShapeDtypeStruct((L, FF_CW), MXU)]
    dconv = [jax.ShapeDtypeStruct((SUB, FF_CW), F32)] * 2
    if last:
        d, x2, vec, dh_prev = tail
        in_specs += [rowD, rowD, _full((SUB, D)), rowD]
        args += [d, x2, vec, dh_prev]
        out_specs = [rowC, rowC, rowD, _full((SUB, D)), _full((SUB, FF_CW)), _full((SUB, FF_CW))]
        out_shape = halfb + [jax.ShapeDtypeStruct((L, D), F32), jax.ShapeDtypeStruct((SUB, D), F32)] + dconv
    else:
        out_specs = [rowD, rowC, rowC, rowD, _full((SUB, FF_CW)), _full((SUB, FF_CW))]
        out_shape = [jax.ShapeDtypeStruct((L, D), MXU)] + halfb + [jax.ShapeDtypeStruct((L, D), F32)] + dconv
    return pl.pallas_call(
        body, name="ffn_bwd_last" if last else "ffn_bwd_first", grid=(ni,),
        in_specs=in_specs, out_specs=out_specs, out_shape=out_shape,
        scratch_shapes=[pltpu.VMEM((SUB, FF_CW), F32), pltpu.VMEM((SUB, FF_CW), F32)],
        compiler_params=_params(1),
    )(*args)


def _final(x, target, nf):
    L = x.shape[0]
    T = _tile(L, 256)

    def body(x_ref, t_ref, nf_ref, dx_ref, acc_ref):
        @pl.when(pl.program_id(0) == 0)
        def _():
            acc_ref[...] = jnp.zeros_like(acc_ref)

        n, r = _rms(x_ref[...])
        w = nf_ref[0:1, :]
        err = n * w - t_ref[...]
        acc_ref[0:1, :] += (0.5 / D) * _sum0(err * err)
        dy = err * (1.0 / D)
        acc_ref[1:2, :] += _sum0(dy * n)
        dx_ref[...] = _rms_bwd(dy * w, n, r)

    row = lambda i: (i, 0)
    return pl.pallas_call(
        body, name="final_norm_loss", grid=(L // T,),
        in_specs=[pl.BlockSpec((T, D), row), pl.BlockSpec((T, D), row), _full((SUB, D))],
        out_specs=[pl.BlockSpec((T, D), row), _full((SUB, D))],
        out_shape=[jax.ShapeDtypeStruct((L, D), F32), jax.ShapeDtypeStruct((SUB, D), F32)],
        compiler_params=_params(1),
    )(x, target, nf)


def _post_bwd(dx2, y, o, p, modrows, sp, w_out):
    L = dx2.shape[0]
    T = _tile(L, 256)

    def body(dx2_ref, y_ref, o_ref, z_ref, mod_ref, sp_ref, w_ref, dy_ref, do_ref, dz_ref, dya_ref, accv_ref, accs_ref):
        @pl.when(pl.program_id(0) == 0)
        def _():
            accv_ref[...] = jnp.zeros_like(accv_ref)
            accs_ref[...] = jnp.zeros_like(accs_ref)

        dx2v = dx2_ref[...]
        accv_ref[0:1, :] += _sum0(dx2v * y_ref[...])
        dyb = (mod_ref[2:3, :] * dx2v).astype(MXU)
        dy_ref[...] = dyb
        dyc = _dot(dyb, w_ref[...], NT)
        dya_ref[...] = dyc[:, 0:AW]
        ndw = sp_ref[2:3, :]
        z = z_ref[...]
        sgz = _sig(z)
        dsz = sgz * (1.0 + z * (1.0 - sgz))
        dndw = jnp.zeros((1, HD), F32)
        for h in range(H):
            sl = slice(h * HD, (h + 1) * HD)
            n, r = _rms(o_ref[:, sl])
            dyh = dyc[:, AW + h * HD:AW + (h + 1) * HD]
            zh = z[:, sl]
            don = dyh * (zh * sgz[:, sl])
            dz_ref[:, sl] = dyh * (n * ndw) * dsz[:, sl]
            dndw = dndw + _sum0(don * n)
            do_ref[:, sl] = _rms_bwd(don * ndw, n, r)
        accs_ref[0:1, :] += dndw

    row = lambda i: (i, 0)
    zcol = (3 * AW + 3 * H * HD) // (H * HD)
    return pl.pallas_call(
        body, name="post_bwd", grid=(L // T,),
        in_specs=[pl.BlockSpec((T, D), row), pl.BlockSpec((T, D), row), pl.BlockSpec((T, H * HD), row),
                  pl.BlockSpec((T, H * HD), lambda i: (i, zcol)), _full((SUB, D)), _full((SUB, LANES)), _full((D, D))],
        out_specs=[pl.BlockSpec((T, D), row)] + [pl.BlockSpec((T, H * HD), row)] * 3 + [_full((SUB, D)), _full((SUB, LANES))],
        out_shape=[jax.ShapeDtypeStruct((L, D), MXU)] + [jax.ShapeDtypeStruct((L, H * HD), F32)] * 3
        + [jax.ShapeDtypeStruct((SUB, D), F32), jax.ShapeDtypeStruct((SUB, LANES), F32)],
        compiler_params=_params(1),
    )(dx2, y, o, p, modrows, sp, w_out)


def _pre_bwd(p, dqn, dkn, dvs, dya, dz, dgb, pa, cq, sp):
    L = p.shape[0]
    T = _tile(L, 256)
    ni = L // T
    scale = HD ** -0.5
    w3 = 3 * AW + 3 * H * HD
    hb_per_t = T // SUB

    def body(pm_ref, ph_ref, ps_ref, dq_ref, dk_ref, dv_ref, dya_ref, dz_ref, dgb_ref, pa_ref, cq_ref, sp_ref,
             dp_ref, dpa_ref, dcq_ref, dsp_ref, carry_u, carry_q):
        i = pl.program_id(0)
        ri = ni - 1 - i

        @pl.when(i == 0)
        def _():
            dpa_ref[...] = jnp.zeros_like(dpa_ref)
            dcq_ref[...] = jnp.zeros_like(dcq_ref)
            dsp_ref[...] = jnp.zeros_like(dsp_ref)
            carry_u[...] = jnp.zeros_like(carry_u)
            carry_q[...] = jnp.zeros_like(carry_q)

        keep = jnp.where(ri == 0, 0.0, 1.0)
        a_b, a_c, a_x = pm_ref[:, 0:AW], pm_ref[:, AW:2 * AW], pm_ref[:, 2 * AW:3 * AW]
        u = a_c * a_x
        hu = ph_ref[:, AW:2 * AW] * ph_ref[:, 2 * AW:3 * AW] * keep
        cu, ush = _conv_fwd(u, pa_ref, 3, hu)
        yp = a_b * cu
        bd = _blockdiag_mean(AW, A_GROUP)
        ra = lax.rsqrt(_dot_f32(yp * yp, bd, NN, exact="b") + EPS)
        na = yp * ra
        dya = dya_ref[...]
        dpa_ref[3:4, :] += _sum0(dya * na)
        dna = dya * pa_ref[3:4, :]
        dyp = ra * (dna - na * _dot_f32(dna * na, bd, NN, exact="b"))
        dcu = dyp * a_b
        for s in range(3):
            dpa_ref[2 - s:3 - s, :] += _sum0(dcu * ush[s])
        du = _conv_bwd_in(dcu, pa_ref, 3, carry_u[...])
        carry_u[...] = dcu[0:SUB, :]
        dp_ref[:, 0:AW] = (dyp * cu).astype(MXU)
        dp_ref[:, AW:2 * AW] = (du * a_x).astype(MXU)
        dp_ref[:, 2 * AW:3 * AW] = (du * a_c).astype(MXU)

        qkv = pm_ref[:, 3 * AW:w3]
        qc, qsh = _conv_fwd(qkv, cq_ref, 4, ph_ref[:, 3 * AW:w3] * keep)
        sg = _sig(qc)
        qs = qc * sg
        parts = []
        for h in range(H):
            q = qs[:, h * HD:(h + 1) * HD]
            rq = lax.rsqrt(jnp.sum(q * q, axis=-1, keepdims=True) + EPS)
            parts.append(_l2_bwd(dq_ref[:, h * HD:(h + 1) * HD] * scale, q * rq, rq))
        for h in range(H):
            k = qs[:, (H + h) * HD:(H + h + 1) * HD]
            rk = lax.rsqrt(jnp.sum(k * k, axis=-1, keepdims=True) + EPS)
            parts.append(_l2_bwd(dk_ref[:, h * HD:(h + 1) * HD], k * rk, rk))
        parts.append(dv_ref[...])
        dqc = jnp.concatenate(parts, axis=-1) * (sg * (1.0 + qc * (1.0 - sg)))
        for s in range(4):
            dcq_ref[3 - s:4 - s, :] += _sum0(dqc * qsh[s])
        dp_ref[:, 3 * AW:w3] = _conv_bwd_in(dqc, cq_ref, 4, carry_q[...]).astype(MXU)
        carry_q[...] = dqc[0:SUB, :]
        dp_ref[:, w3:w3 + H * HD] = dz_ref[...].astype(MXU)

        lane, a, xb, beta, g = _gate_small(ps_ref[...], sp_ref)
        dgb = dgb_ref[...]
        dbeta = jnp.where(lane < H, dgb, 0.0)
        dg = jnp.where((lane >= H) & (lane < 2 * H), dgb, 0.0)
        dalpha = dg * a * _sig(xb)
        dsp_ref[0:1, :] += _sum0(dg * g)
        dsp_ref[1:2, :] += _sum0(dalpha)
        dp_ref[:, w3 + H * HD:P_PAD] = (dbeta * beta * (1.0 - beta) + dalpha).astype(MXU)

    row = lambda i: (ni - 1 - i, 0)
    halo = lambda i: (jnp.maximum((ni - 1 - i) * hb_per_t - 1, 0), 0)
    hrow = pl.BlockSpec((T, H * HD), row)
    return pl.pallas_call(
        body, name="pre_bwd", grid=(ni,),
        in_specs=[pl.BlockSpec((T, w3), row), pl.BlockSpec((SUB, w3), halo),
                  pl.BlockSpec((T, LANES), lambda i: (ni - 1 - i, (P_PAD - LANES) // LANES)),
                  hrow, hrow, hrow, pl.BlockSpec((T, AW), row), hrow,
                  pl.BlockSpec((T, LANES), row),
                  _full((SUB, AW)), _full((SUB, 3 * H * HD)), _full((SUB, LANES))],
        out_specs=[pl.BlockSpec((T, P_PAD), row), _full((SUB, AW)), _full((SUB, 3 * H * HD)), _full((SUB, LANES))],
        out_shape=[jax.ShapeDtypeStruct((L, P_PAD), MXU), jax.ShapeDtypeStruct((SUB, AW), F32),
                   jax.ShapeDtypeStruct((SUB, 3 * H * HD), F32), jax.ShapeDtypeStruct((SUB, LANES), F32)],
        scratch_shapes=[pltpu.VMEM((SUB, AW), F32), pltpu.VMEM((SUB, 3 * H * HD), F32)],
        compiler_params=_params(1),
    )(p, p, p, dqn, dkn, dvs, dya, dz, dgb, pa, cq, sp)


def _in_bwd(dp, w_in, x, dx2, modrows, vec):
    L = x.shape[0]
    T = _tile(L, 256)

    def body(dp_ref, w_ref, x_ref, dx2_ref, mod_ref, vec_ref, dx_ref, accv_ref):
        @pl.when(pl.program_id(0) == 0)
        def _():
            accv_ref[...] = jnp.zeros_like(accv_ref)

        dh = _dot(dp_ref[...], w_ref[...], NT)
        n, r = _rms(x_ref[...])
        nw, sc = vec_ref[0:1, :], mod_ref[1:2, :]
        accv_ref[0:1, :] += _sum0(dh)
        accv_ref[1:2, :] += _sum0(dh * n * nw)
        accv_ref[2:3, :] += _sum0(dh * n * (1.0 + sc))
        dx_ref[...] = _rms_bwd(dh * nw * (1.0 + sc), n, r) + dx2_ref[...]

    row = lambda i: (i, 0)
    return pl.pallas_call(
        body, name="in_bwd", grid=(L // T,),
        in_specs=[pl.BlockSpec((T, P_PAD), row), _full((D, P_PAD)), pl.BlockSpec((T, D), row),
                  pl.BlockSpec((T, D), row), _full((SUB, D)), _full((SUB, D))],
        out_specs=[pl.BlockSpec((T, D), row), _full((SUB, D))],
        out_shape=[jax.ShapeDtypeStruct((L, D), F32), jax.ShapeDtypeStruct((SUB, D), F32)],
        compiler_params=_params(1),
    )(dp, w_in, x, dx2, modrows, vec)


def _wgrad(a, b, tm, tn, name):
    L, m = a.shape
    n = b.shape[1]
    tl = _tile(L, 512)
    tm, tn = _tile(m, tm), _tile(n, tn)
    nl = L // tl

    def body(a_ref, b_ref, o_ref, acc):
        @pl.when(pl.program_id(2) == 0)
        def _():
            acc[...] = jnp.zeros_like(acc)

        acc[...] += _dot(a_ref[...], b_ref[...], TN)

        @pl.when(pl.program_id(2) == nl - 1)
        def _():
            o_ref[...] = acc[...].astype(o_ref.dtype)

    return pl.pallas_call(
        body, name=name, grid=(m // tm, n // tn, nl),
        in_specs=[pl.BlockSpec((tl, tm), lambda i, j, l: (l, i)), pl.BlockSpec((tl, tn), lambda i, j, l: (l, j))],
        out_specs=pl.BlockSpec((tm, tn), lambda i, j, l: (i, j)),
        out_shape=jax.ShapeDtypeStruct((m, n), MXU), scratch_shapes=[pltpu.VMEM((tm, tn), F32)],
        compiler_params=_params(3),
    )(a, b)


def _wgrad_cols(a, b, tm, n_shard, wpad, count, name):
    L, m = a.shape
    n = b.shape[1]
    tl = _tile(L, 512)
    tm = _tile(m, tm)
    nl = L // tl
    wins = _shard_windows(n_shard, count)
    assert all(a_ * LANES + win <= n for a_, _, win in wins), (wins, n)

    def body(a_ref, b_ref, o_ref, acc):
        @pl.when(pl.program_id(1) == 0)
        def _():
            acc[...] = jnp.zeros_like(acc)

        acc[...] += _dot(a_ref[...], b_ref[...], TN)

        @pl.when(pl.program_id(1) == nl - 1)
        def _():
            for k, (a_, s, win) in enumerate(wins):
                xk = acc[:, a_ * LANES:a_ * LANES + win]
                if s:
                    xk = pltpu.roll(xk, win - s, 1)
                o_ref[k] = _fit_lanes(xk, wpad).astype(o_ref.dtype)

    return pl.pallas_call(
        body, name=name, grid=(m // tm, nl),
        in_specs=[pl.BlockSpec((tl, tm), lambda i, l: (l, i)), pl.BlockSpec((tl, n), lambda i, l: (l, 0))],
        out_specs=pl.BlockSpec((count, tm, wpad), lambda i, l: (0, i, 0)),
        out_shape=jax.ShapeDtypeStruct((count, m, wpad), MXU),
        scratch_shapes=[pltpu.VMEM((tm, n), F32)],
        compiler_params=_params(2),
    )(a, b)


def _adamw(w, g, m, v, name):
    r, n = w.shape
    tr = _tile(r, 512)
    bc1 = 1.0 - ADAM_B1 ** ADAM_STEP
    bc2 = 1.0 - ADAM_B2 ** ADAM_STEP

    def body(w_ref, g_ref, m_ref, v_ref, d_ref, nm_ref, nv_ref):
        gv = g_ref[...]
        nm = ADAM_B1 * m_ref[...] + (1.0 - ADAM_B1) * gv
        nv = ADAM_B2 * v_ref[...] + (1.0 - ADAM_B2) * (gv * gv)
        nm_ref[...] = nm
        nv_ref[...] = nv
        d_ref[...] = -ADAM_LR * ((nm / bc1) / (jnp.sqrt(nv / bc2) + ADAM_EPS) + ADAM_WD * w_ref[...])

    spec = pl.BlockSpec((tr, n), lambda i: (i, 0))
    return pl.pallas_call(
        body, name=name, grid=(r // tr,), in_specs=[spec] * 4, out_specs=[spec] * 3,
        out_shape=[jax.ShapeDtypeStruct((r, n), F32)] * 3, compiler_params=_params(1),
    )(w, g, m, v)


def _rows8(rows, width):
    out = jnp.zeros((SUB, width), F32)
    for r, vrow in enumerate(rows):
        out = out.at[r, :vrow.shape[0]].set(vrow)
    return out


def _at_lanes(v4, start):
    return jnp.zeros((LANES,), F32).at[start:start + v4.shape[0]].set(v4)


def _pad_rows(flat, mult):
    n = flat.shape[0]
    pad = (-n) % mult
    return jnp.pad(flat, (0, pad)) if pad else flat


IN_PAD = 512
UP_PAD = 768


def _local_fwd_bwd(x, target, mod_full, small_w, full_w, on_grads=None):
    norm1_w, norm2_w, norm_a_w, a_log, dt_bias, norm_dn_w, norm_f_w = small_w
    w_in_f, w_out_f, w_up_f, w_down_f, conv_a_f, conv_q_f, conv_f_f = full_w

    def layer_params(i):
        modrows = jnp.concatenate([mod_full[i], jnp.zeros((SUB - N_MOD, D), F32)], axis=0)
        vec = _rows8([norm1_w[i], norm2_w[i]], D)
        pa = _rows8([conv_a_f[i, 0], conv_a_f[i, 1], conv_a_f[i, 2], norm_a_w[i]], AW)
        cq = _rows8([conv_q_f[i, k] for k in range(4)], 3 * H * HD)
        sp = _rows8([_at_lanes(a_log[i], H), _at_lanes(dt_bias[i], H), norm_dn_w[i]], LANES)
        cff = _rows8([conv_f_f[i, k] for k in range(3)], 2 * DFF)
        return modrows, vec, pa, cq, sp, cff

    saved = []
    xi = x
    for i in range(DEPTH):
        modrows, vec, pa, cq, sp, cff = layer_params(i)
        p, h1 = _in_proj(xi, modrows, vec, w_in_f[i])
        qn, kn, vs, gb, ya = _pre_fwd(p, pa, cq, sp)
        o, states, tinvs = _gdr_fwd(qn, kn, vs, gb)
        y, x2, yb = _post_fwd(o, p, ya, xi, modrows, sp, w_out_f[i])
        h2, gp0, up0, f0, d0 = _ffn_fwd_half(x2, modrows, vec, w_up_f[i], cff, w_down_f[i], 0, None)
        gp1, up1, f1, dff, x3 = _ffn_fwd_half(x2, modrows, vec, w_up_f[i], cff, w_down_f[i], 1, d0)
        saved.append(dict(x=xi, p=p, h1=h1, qn=qn, kn=kn, vs=vs, gb=gb, ya=ya, o=o, states=states, tinvs=tinvs, y=y, x2=x2, yb=yb,
                          h2=h2, gpre=(gp0, gp1), upre=(up0, up1), f=(f0, f1), d=dff))
        xi = x3

    dx, facc = _final(xi, target, _rows8([norm_f_w], D))
    loss_local = jnp.sum(facc[0])
    d_norm_f = facc[1]

    gw_in, gw_out, gw_up, gw_down = [None] * DEPTH, [None] * DEPTH, [None] * DEPTH, [None] * DEPTH
    g_small = [None] * DEPTH
    for i in reversed(range(DEPTH)):
        s = saved[i]
        modrows, vec, pa, cq, sp, cff = layer_params(i)
        dd, dgp0, dup0, dh0, dcg0, dcu0 = _ffn_bwd_half(dx, modrows, s["gpre"][0], s["upre"][0], cff,
                                                        w_down_f[i], w_up_f[i], 0, None)
        dgp1, dup1, dx2, accf, dcg1, dcu1 = _ffn_bwd_half(dx, modrows, s["gpre"][1], s["upre"][1], cff,
                                                          w_down_f[i], w_up_f[i], 1, (s["d"], s["x2"], vec, dh0))
        n_up, up_pad = 2 * DFF // N_DEV, UP_PAD
        gw_up[i] = jnp.concatenate([_wgrad_cols(s["h2"], t, 1024, n_up, up_pad, FF_CW // n_up, "wgrad_up")
                                    for t in (dgp0, dgp1, dup0, dup1)], axis=0)
        gw_down[i] = jnp.concatenate([_wgrad(s["f"][0], dd, FF_CW, 1024, "wgrad_down"),
                                      _wgrad(s["f"][1], dd, FF_CW, 1024, "wgrad_down")],
                                     axis=0).reshape(N_DEV, DFF // N_DEV, D)
        dy, do, dz, dya, accp, accs = _post_bwd(dx2, s["y"], s["o"], s["p"], modrows, sp, w_out_f[i])
        gw_out[i] = jnp.concatenate([_wgrad(s["ya"], dy, 512, 1024, "wgrad_out"),
                                     _wgrad(s["yb"], dy, 512, 1024, "wgrad_out")], axis=0).reshape(N_DEV, D // N_DEV, D)
        dqn, dkn, dvs, dgb = _gdr_bwd(s["qn"], s["kn"], s["vs"], s["gb"], s["states"], s["tinvs"], do)
        dp, dpa, dcq, dsp = _pre_bwd(s["p"], dqn, dkn, dvs, dya, dz, dgb, pa, cq, sp)
        gw_in[i] = _wgrad_cols(s["h1"], dp, 512, P_IN // N_DEV, IN_PAD, N_DEV, "wgrad_in")
        dx, acci = _in_bwd(dp, w_in_f[i], s["x"], dx2, modrows, vec)
        dconv_ff = jnp.concatenate([dcg0, dcg1, dcu0, dcu1], axis=1)[0:3]
        dmod = jnp.stack([acci[0], acci[1], accp[0], accf[1], accf[2], accf[0]])
        g_small[i] = dict(norm1=acci[2], norm2=accf[3], norm_a=dpa[3], a_log=dsp[0, H:2 * H], dt_bias=dsp[1, H:2 * H],
                          norm_dn=accs[0], conv_a=dpa[0:3], conv_qkv=dcq[0:4], conv_ff=dconv_ff, dmod=dmod.reshape(-1))
        if on_grads is not None:
            dx = on_grads(i, [gw_in[i], gw_out[i], gw_up[i], gw_down[i]], dx)
    return loss_local, dx, gw_in, gw_out, gw_up, gw_down, g_small, d_norm_f


def kernel(x, c, ada_w, ada_b, norm1_w, w_in, conv_a_w, norm_a_w, conv_qkv_w, a_log, dt_bias, norm_dn_w, w_out, norm2_w, w_up, conv_ff_w, w_down, norm_f_w, loss_target, m_ada_w, m_ada_b, m_norm1_w, m_w_in, m_conv_a_w, m_norm_a_w, m_conv_qkv_w, m_a_log, m_dt_bias, m_norm_dn_w, m_w_out, m_norm2_w, m_w_up, m_conv_ff_w, m_w_down, m_norm_f_w, v_ada_w, v_ada_b, v_norm1_w, v_w_in, v_conv_a_w, v_norm_a_w, v_conv_qkv_w, v_a_log, v_dt_bias, v_norm_dn_w, v_w_out, v_norm2_w, v_w_up, v_conv_ff_w, v_w_down, v_norm_f_w):
    ax, ay, ac = lax.axis_index("x"), lax.axis_index("y"), lax.axis_index("c")
    me = 4 * ax + 2 * ay + ac
    x = x[0]
    target = loss_target[0]
    n_in, n_up = P_IN // N_DEV, 2 * DFF // N_DEV

    def lane_pad(t, width):
        return jnp.pad(t.astype(MXU), ((0, 0), (0, 0), (0, width - t.shape[-1])))

    conv_blob = _pad_rows(jnp.concatenate([t.reshape(-1) for t in (conv_a_w, conv_qkv_w, conv_ff_w)]),
                          SUB * LANES).reshape(-1, LANES)
    c_rows = jnp.zeros((SUB, D), F32).at[0].set(c[0])
    send = [lane_pad(w_in, IN_PAD), w_out.astype(MXU), lane_pad(w_up, UP_PAD), w_down.astype(MXU)]
    got = [None] * DEPTH
    *got[0], g_conv, g_c = _all_gather([t[0] for t in send] + [conv_blob, c_rows], "gather_weights", in_vmem=False)
    for i in range(1, DEPTH):
        shards, _ = lax.optimization_barrier(([t[i] for t in send], g_c))
        got[i] = _all_gather_async(shards, "gather_weights_l%d" % i, collective_id=i)
    w_in_f = [_interleave_cols(g[0][:, None], n_in, P_PAD, "interleave_w_in")[0] for g in got]
    w_up_f = [_interleave_cols(g[2][:, None], n_up, 2 * DFF, "interleave_w_up")[0] for g in got]
    w_out_f = [g[1].reshape(D, D) for g in got]
    w_down_f = [g[3].reshape(DFF, D) for g in got]
    sg = g_conv.reshape(N_DEV, -1)
    o1 = conv_a_w.size
    o2 = o1 + conv_qkv_w.size
    o3 = o2 + conv_ff_w.size
    conv_a_f = sg[:, 0:o1].reshape(N_DEV, DEPTH, 3, AW // N_DEV).transpose(1, 2, 0, 3).reshape(DEPTH, 3, AW)
    conv_q_f = sg[:, o1:o2].reshape(N_DEV, DEPTH, 4, 3 * H * HD // N_DEV).transpose(1, 2, 0, 3).reshape(DEPTH, 4, 3 * H * HD)
    conv_f_f = sg[:, o2:o3].reshape(N_DEV, DEPTH, 3, n_up).transpose(1, 2, 0, 3).reshape(DEPTH, 3, 2 * DFF)

    c_all = jnp.concatenate([g_c[:, 0], jnp.zeros((16 - N_DEV, D), F32)], axis=0)
    n_ada = N_MOD * D // N_DEV
    ada_b_cols = lax.dynamic_slice_in_dim(ada_b, me * n_ada, n_ada, axis=1)[:, None, :]
    mod_sh = _mod_fwd(c_all, ada_w, ada_b_cols)
    mod_all = _all_gather([mod_sh.reshape(DEPTH * 16, n_ada)], "gather_mod", in_vmem=True)[0]
    mod_all = mod_all.reshape(N_DEV, DEPTH, 16, n_ada)
    mod_mine = lax.dynamic_index_in_dim(mod_all, me, axis=2, keepdims=False)
    mod_full = mod_mine.transpose(1, 0, 2).reshape(DEPTH, N_MOD, D)

    tags = ["w_in", "w_out", "w_up", "w_down"]
    received = [None] * DEPTH

    def on_grads(i, gs_i, dx):
        received[i] = _rs_exchange_async(gs_i, "rs_exchange_l%d" % i, collective_id=DEPTH + i)
        return dx

    loss_local, dx, _, _, _, _, g_small, d_norm_f = _local_fwd_bwd(
        x, target, mod_full, (norm1_w, norm2_w, norm_a_w, a_log, dt_bias, norm_dn_w, norm_f_w),
        (w_in_f, w_out_f, w_up_f, w_down_f, conv_a_f, conv_q_f, conv_f_f), on_grads)
    loss = lax.psum(loss_local, ("x", "y", "c"))
    grad_x = dx[None]

    keys = ["dmod", "norm1", "norm2", "norm_a", "a_log", "dt_bias", "norm_dn", "conv_a", "conv_qkv", "conv_ff"]
    stacked = {k: jnp.stack([g_small[i][k] for i in range(DEPTH)]) for k in keys}
    flat_parts = [stacked[k].reshape(-1) for k in keys] + [d_norm_f]
    sizes = [int(t.shape[0]) for t in flat_parts]
    sflat = _pad_rows(jnp.concatenate(flat_parts), SUB * LANES).reshape(-1, LANES)
    sall = _all_gather([sflat], "gather_small_grads", in_vmem=True)[0]
    ssum = _sum_devices(sall).reshape(-1)
    so = [0]
    for sz in sizes:
        so.append(so[-1] + sz)
    red = {k: ssum[so[n]:so[n + 1]].reshape(stacked[k].shape) for n, k in enumerate(keys)}
    g_norm_f = ssum[so[len(keys)]:so[len(keys) + 1]]
    dmod_all = sall[:, 0:sizes[0] // LANES, :].reshape(N_DEV, DEPTH, N_MOD * D)

    g_ada_b = red["dmod"].reshape(DEPTH, N_MOD * D)
    dmod_cols = lax.dynamic_slice_in_dim(dmod_all, me * n_ada, n_ada, axis=2).transpose(1, 0, 2)
    dmod_cols = jnp.concatenate([dmod_cols, jnp.zeros((DEPTH, 16 - N_DEV, n_ada), F32)], axis=1)
    g_ada_w = _mod_bwd(c_all, dmod_cols)
    g_conv_a = lax.dynamic_slice_in_dim(red["conv_a"], me * (AW // N_DEV), AW // N_DEV, axis=2)
    g_conv_qkv = lax.dynamic_slice_in_dim(red["conv_qkv"], me * (3 * H * HD // N_DEV), 3 * H * HD // N_DEV, axis=2)
    g_conv_ff = lax.dynamic_slice_in_dim(red["conv_ff"], me * n_up, n_up, axis=2)

    mine = [jnp.stack([_rs_sum(received[i][k], "rs_sum_" + t) for i in range(DEPTH)]) for k, t in enumerate(tags)]
    g_w_in = mine[0][:, :, :n_in]
    g_w_out = mine[1]
    g_w_up = mine[2][:, :, :n_up]
    g_w_down = mine[3]

    grads = dict(ada_w=g_ada_w, ada_b=g_ada_b, norm1_w=red["norm1"], w_in=g_w_in, conv_a_w=g_conv_a,
                 norm_a_w=red["norm_a"], conv_qkv_w=g_conv_qkv, a_log=red["a_log"], dt_bias=red["dt_bias"],
                 norm_dn_w=red["norm_dn"], w_out=g_w_out, norm2_w=red["norm2"], w_up=g_w_up, conv_ff_w=g_conv_ff,
                 w_down=g_w_down, norm_f_w=g_norm_f)
    weights = dict(ada_w=ada_w, ada_b=ada_b, norm1_w=norm1_w, w_in=w_in, conv_a_w=conv_a_w, norm_a_w=norm_a_w,
                   conv_qkv_w=conv_qkv_w, a_log=a_log, dt_bias=dt_bias, norm_dn_w=norm_dn_w, w_out=w_out,
                   norm2_w=norm2_w, w_up=w_up, conv_ff_w=conv_ff_w, w_down=w_down, norm_f_w=norm_f_w)
    ms = dict(ada_w=m_ada_w, ada_b=m_ada_b, norm1_w=m_norm1_w, w_in=m_w_in, conv_a_w=m_conv_a_w, norm_a_w=m_norm_a_w,
              conv_qkv_w=m_conv_qkv_w, a_log=m_a_log, dt_bias=m_dt_bias, norm_dn_w=m_norm_dn_w, w_out=m_w_out,
              norm2_w=m_norm2_w, w_up=m_w_up, conv_ff_w=m_conv_ff_w, w_down=m_w_down, norm_f_w=m_norm_f_w)
    vs_ = dict(ada_w=v_ada_w, ada_b=v_ada_b, norm1_w=v_norm1_w, w_in=v_w_in, conv_a_w=v_conv_a_w, norm_a_w=v_norm_a_w,
               conv_qkv_w=v_conv_qkv_w, a_log=v_a_log, dt_bias=v_dt_bias, norm_dn_w=v_norm_dn_w, w_out=v_w_out,
               norm2_w=v_norm2_w, w_up=v_w_up, conv_ff_w=v_conv_ff_w, w_down=v_w_down, norm_f_w=v_norm_f_w)
    names = list(weights)
    big_names = ["ada_w", "w_in", "w_out", "w_up", "w_down"]
    delta, new_m, new_v = {}, {}, {}
    for n in big_names:
        shp = weights[n].shape
        two = lambda t: t.reshape(-1, shp[-1])
        dl, nm, nv = _adamw(two(weights[n]), two(grads[n]), two(ms[n]), two(vs_[n]), "adamw_" + n)
        delta[n], new_m[n], new_v[n] = dl.reshape(shp), nm.reshape(shp), nv.reshape(shp)
    small_names = [n for n in names if n not in big_names]

    def pack(dct):
        return _pad_rows(jnp.concatenate([dct[n].reshape(-1) for n in small_names]), SUB * LANES).reshape(-1, LANES)

    dl, nm, nv = _adamw(pack(weights), pack(grads), pack(ms), pack(vs_), "adamw_small")
    off = 0
    for n in small_names:
        sz, shp = weights[n].size, weights[n].shape
        delta[n] = dl.reshape(-1)[off:off + sz].reshape(shp)
        new_m[n] = nm.reshape(-1)[off:off + sz].reshape(shp)
        new_v[n] = nv.reshape(-1)[off:off + sz].reshape(shp)
        off += sz

    return (loss, grad_x, *[grads[n] for n in names], *[delta[n] for n in names],
            *[new_m[n] for n in names], *[new_v[n] for n in names])
```

```python
import functools
import math

import jax
import jax.numpy as jnp
from jax import lax
from jax.experimental import pallas as pl
from jax.experimental.pallas import tpu as pltpu
from jax.experimental.pallas import tpu_sc as plsc

F32 = jnp.float32
MXU = jnp.bfloat16

D = 1024
DEPTH = 4
N_MOD = 6
AW = 512
A_GROUP = 64
H = 4
HD = 128
CK = 64
DFF = 2816
P_IN = 3592
P_PAD = 3712
EPS = 1e-6
N_DEV = 8
LANES = 128
SUB = 8
VMEM_LIMIT = 56 * 1024 * 1024

ADAM_LR, ADAM_B1, ADAM_B2, ADAM_EPS, ADAM_WD, ADAM_STEP = 0.001, 0.9, 0.999, 1e-08, 0.01, 10

NN = ((1,), (0,))
NT = ((1,), (1,))
TN = ((0,), (0,))
HI = lax.Precision.HIGHEST
MESH = pl.DeviceIdType.MESH


def _dot(a, b, dims, prec=None):
    if prec is None:
        a = a.astype(MXU) if a.dtype == F32 else a
        b = b.astype(MXU) if b.dtype == F32 else b
    return lax.dot_general(a, b, (dims, ((), ())), precision=prec, preferred_element_type=F32)


def _params(n_grid=0, limit=VMEM_LIMIT):
    sem = ("arbitrary",) * n_grid if n_grid else None
    return pltpu.CompilerParams(dimension_semantics=sem, vmem_limit_bytes=limit)


def _tile(n, want):
    if n <= want:
        return n
    t = want - want % SUB
    while n % t:
        t -= SUB
    assert t > 0, (n, want)
    return t


def _full(shape):
    nd = len(shape)
    return pl.BlockSpec(shape, lambda *_: (0,) * nd)


def _sig(x):
    return jax.nn.sigmoid(x)


def _rms(x):
    r = lax.rsqrt(jnp.mean(x * x, axis=-1, keepdims=True) + EPS)
    return x * r, r


def _rms_bwd(dn, n, r):
    return r * (dn - n * jnp.mean(dn * n, axis=-1, keepdims=True))


def _l2_bwd(dn, n, r):
    return r * (dn - n * jnp.sum(dn * n, axis=-1, keepdims=True))


def _sum0(x):
    return jnp.sum(x, axis=0, keepdims=True)


def _shift_down(x, s, halo):
    ext = jnp.concatenate([halo, x], axis=0)
    return pltpu.roll(ext, s, 0)[SUB:, :]


def _shift_up(x, s, halo):
    t = x.shape[0]
    ext = jnp.concatenate([x, halo], axis=0)
    return pltpu.roll(ext, t + SUB - s, 0)[:t, :]


def _conv_fwd(x, w_ref, width, halo):
    sh = [x] + [_shift_down(x, s, halo) for s in range(1, width)]
    out = w_ref[width - 1:width, :] * sh[0]
    for s in range(1, width):
        out = out + w_ref[width - 1 - s:width - s, :] * sh[s]
    return out, sh


def _conv_bwd_in(dout, w_ref, width, halo_next):
    dx = w_ref[width - 1:width, :] * dout
    for s in range(1, width):
        dx = dx + w_ref[width - 1 - s:width - s, :] * _shift_up(dout, s, halo_next)
    return dx


def _blockdiag_mean(n, group):
    r = lax.shift_right_logical(lax.broadcasted_iota(jnp.int32, (n, n), 0), int(math.log2(group)))
    c = lax.shift_right_logical(lax.broadcasted_iota(jnp.int32, (n, n), 1), int(math.log2(group)))
    return jnp.where(r == c, 1.0 / group, 0.0).astype(F32)


def _softplus(x):
    return jnp.maximum(x, 0.0) + jnp.log(1.0 + jnp.exp(-jnp.abs(x)))


def _my_place():
    return lax.axis_index("x"), lax.axis_index("y"), lax.axis_index("c")


def _all_gather(shards, name, in_vmem):
    nt = len(shards)

    def body(*refs):
        x_refs, out_refs = refs[:nt], refs[nt:2 * nt]
        send_sems, recv_sems, local_sems = refs[2 * nt:]
        x, y, c = _my_place()
        me, sibling = (x, y, c), (x, y, 1 - c)
        chips = [(1 - x, y), (x, 1 - y), (1 - x, 1 - y)]
        everything = []
        for t in range(nt):
            x_ref, out_ref = x_refs[t], out_refs[t]

            def blk(px, py, pc, out_ref=out_ref):
                return out_ref.at[4 * px + 2 * py + pc]

            def copy(k, block, to, src=None, t=t, blk=blk):
                return pltpu.make_async_remote_copy(
                    src_ref=blk(*block) if src is None else src, dst_ref=blk(*block),
                    send_sem=send_sems.at[7 * t + k], recv_sem=recv_sems.at[7 * t + k], device_id=to, device_id_type=MESH)

            mine = pltpu.make_async_copy(x_ref, blk(*me), local_sems.at[t])
            mine.start()
            first = [copy(0, me, sibling, src=x_ref)]
            first += [copy(1 + j, me, (*chip, c), src=x_ref) for j, chip in enumerate(chips)]
            for cp in first:
                cp.start()
            everything.append((copy, mine, first))
        sends = []
        for copy, mine, first in everything:
            passed = [copy(4 + j, (*chip, c), sibling) for j, chip in enumerate(chips)]
            for j, chip in enumerate(chips):
                copy(1 + j, (*chip, c), me).wait_recv()
                passed[j].start()
            sends += first + passed
        for copy, mine, first in everything:
            copy(0, sibling, me).wait_recv()
            for j, chip in enumerate(chips):
                copy(4 + j, (*chip, 1 - c), me).wait_recv()
        for cp in sends:
            cp.wait_send()
        for copy, mine, first in everything:
            mine.wait()

    space = pltpu.VMEM if in_vmem else pl.ANY
    return pl.pallas_call(
        body, name=name,
        out_shape=[jax.ShapeDtypeStruct((N_DEV,) + s.shape, s.dtype) for s in shards],
        in_specs=[pl.BlockSpec(memory_space=space)] * nt,
        out_specs=[pl.BlockSpec(memory_space=space)] * nt,
        scratch_shapes=[pltpu.SemaphoreType.DMA((7 * nt,)), pltpu.SemaphoreType.DMA((7 * nt,)),
                        pltpu.SemaphoreType.DMA((nt,))],
        compiler_params=pltpu.CompilerParams(vmem_limit_bytes=VMEM_LIMIT),
    )(*shards)


def _all_gather_async(shards, name, collective_id):
    nt = len(shards)
    hbm = pltpu.MemorySpace.HBM
    x_refs = [jax.new_ref(s, memory_space=hbm) for s in shards]
    out_refs = [jax.empty_ref(jax.ShapeDtypeStruct((N_DEV,) + s.shape, s.dtype), memory_space=hbm) for s in shards]

    @pl.kernel(mesh=plsc.ScalarSubcoreMesh(axis_name="sequencer", num_cores=1), name=name,
               scratch_types=(pltpu.SemaphoreType.DMA((7 * nt,)), pltpu.SemaphoreType.DMA((7 * nt,)),
                              pltpu.SemaphoreType.DMA((nt,))),
               compiler_params=pltpu.CompilerParams(collective_id=collective_id))
    def launch(send_sems, recv_sems, local_sems):
        x, y, c = _my_place()
        me, sibling = (x, y, c), (x, y, 1 - c)
        chips = [(1 - x, y), (x, 1 - y), (1 - x, 1 - y)]
        barrier = pltpu.get_barrier_semaphore()
        for peer in [sibling] + [(*chip, c) for chip in chips]:
            pl.semaphore_signal(barrier, inc=1, device_id=peer, device_id_type=MESH)
        pl.semaphore_wait(barrier, 4)
        everything = []
        for t in range(nt):
            x_ref, out_ref = x_refs[t], out_refs[t]

            def blk(px, py, pc, out_ref=out_ref):
                return out_ref.at[4 * px + 2 * py + pc]

            def copy(k, block, to, src=None, t=t, blk=blk):
                return pltpu.make_async_remote_copy(
                    src_ref=blk(*block) if src is None else src, dst_ref=blk(*block),
                    send_sem=send_sems.at[7 * t + k], recv_sem=recv_sems.at[7 * t + k], device_id=to, device_id_type=MESH)

            mine = pltpu.make_async_copy(x_ref, blk(*me), local_sems.at[t])
            mine.start()
            first = [copy(0, me, sibling, src=x_ref)]
            first += [copy(1 + j, me, (*chip, c), src=x_ref) for j, chip in enumerate(chips)]
            for cp in first:
                cp.start()
            everything.append((copy, mine, first))
        sends = []
        for copy, mine, first in everything:
            passed = [copy(4 + j, (*chip, c), sibling) for j, chip in enumerate(chips)]
            for j, chip in enumerate(chips):
                copy(1 + j, (*chip, c), me).wait_recv()
                passed[j].start()
            sends += first + passed
        for copy, mine, first in everything:
            copy(0, sibling, me).wait_recv()
            for j, chip in enumerate(chips):
                copy(4 + j, (*chip, 1 - c), me).wait_recv()
        for cp in sends:
            cp.wait_send()
        for copy, mine, first in everything:
            mine.wait()

    launch()
    return [r[...] for r in out_refs]


def _rs_exchange_async(srcs, name, collective_id):
    nt = len(srcs)
    hbm = pltpu.MemorySpace.HBM
    src_refs = [jax.new_ref(s, memory_space=hbm) for s in srcs]
    out_refs = [jax.empty_ref(jax.ShapeDtypeStruct(s.shape, s.dtype), memory_space=hbm) for s in srcs]
    flips = [(fx, fy, fc) for fx in (0, 1) for fy in (0, 1) for fc in (0, 1)][1:]

    @pl.kernel(mesh=plsc.ScalarSubcoreMesh(axis_name="sequencer", num_cores=1), name=name,
               scratch_types=(pltpu.SemaphoreType.DMA((7 * nt,)), pltpu.SemaphoreType.DMA((7 * nt,)),
                              pltpu.SemaphoreType.DMA((nt,))),
               compiler_params=pltpu.CompilerParams(collective_id=collective_id))
    def launch(send_sems, recv_sems, local_sems):
        x, y, c = _my_place()
        me = 4 * x + 2 * y + c
        peers = [(1 - x if fx else x, 1 - y if fy else y, 1 - c if fc else c) for fx, fy, fc in flips]
        barrier = pltpu.get_barrier_semaphore()
        for peer in peers:
            pl.semaphore_signal(barrier, inc=1, device_id=peer, device_id_type=MESH)
        pl.semaphore_wait(barrier, len(peers))
        own = [pltpu.make_async_copy(src_refs[t].at[me], out_refs[t].at[me], local_sems.at[t]) for t in range(nt)]
        copies = [pltpu.make_async_remote_copy(
            src_ref=src_refs[t].at[4 * px + 2 * py + pc], dst_ref=out_refs[t].at[me],
            send_sem=send_sems.at[7 * t + f], recv_sem=recv_sems.at[7 * t + f],
            device_id=(px, py, pc), device_id_type=MESH) for t in range(nt) for f, (px, py, pc) in enumerate(peers)]
        for cp in own + copies:
            cp.start()
        for cp in copies + own:
            cp.wait()

    launch()
    return [r[...] for r in out_refs]


def _rs_sum(recv, name):
    _, r, n = recv.shape
    tr = _tile(r, 512)

    def body(r_ref, o_ref):
        s = r_ref[0].astype(F32)
        for k in range(1, N_DEV):
            s = s + r_ref[k].astype(F32)
        o_ref[...] = s

    return pl.pallas_call(
        body, name=name, grid=(r // tr,),
        in_specs=[pl.BlockSpec((N_DEV, tr, n), lambda i: (0, i, 0))],
        out_specs=pl.BlockSpec((tr, n), lambda i: (i, 0)),
        out_shape=jax.ShapeDtypeStruct((r, n), F32), compiler_params=_params(1),
    )(recv)


def _shard_windows(n_shard, count, first=0):
    out = []
    for k in range(first, first + count):
        off = n_shard * k
        a, s = off // LANES, off % LANES
        out.append((a, s, -(-(s + n_shard) // LANES) * LANES))
    return out


def _fit_lanes(x, width):
    have = x.shape[1]
    if have < width:
        return jnp.concatenate([x, jnp.zeros((x.shape[0], width - have), x.dtype)], axis=-1)
    return x[:, :width]


def _interleave_cols(g, n_shard, w_out, name):
    nd, nl, rows, wpad = g.shape
    rb = _tile(rows, 256)
    wins = _shard_windows(n_shard, nd)

    def body(g_ref, o_ref, acc):
        acc[...] = jnp.zeros_like(acc)
        for k, (a, s, win) in enumerate(wins):
            xk = _fit_lanes(g_ref[k].astype(F32), win)
            if s:
                xk = pltpu.roll(xk, s, 1)
            acc[:, a * LANES:a * LANES + win] += xk
        o_ref[...] = acc[...].astype(o_ref.dtype)

    return pl.pallas_call(
        body, name=name, grid=(nl, rows // rb),
        in_specs=[pl.BlockSpec((nd, None, rb, wpad), lambda l, i: (0, l, i, 0))],
        out_specs=pl.BlockSpec((None, rb, w_out), lambda l, i: (l, i, 0)),
        out_shape=jax.ShapeDtypeStruct((nl, rows, w_out), g.dtype),
        scratch_shapes=[pltpu.VMEM((rb, w_out), F32)],
        compiler_params=_params(2),
    )(g)


def _sum_devices(g):
    _, r, n = g.shape

    def body(g_ref, o_ref):
        s = g_ref[0]
        for t in range(1, N_DEV):
            s = s + g_ref[t]
        o_ref[...] = s

    return pl.pallas_call(
        body, name="sum_devices", out_shape=jax.ShapeDtypeStruct((r, n), F32),
        in_specs=[pl.BlockSpec(memory_space=pltpu.VMEM)], out_specs=pl.BlockSpec(memory_space=pltpu.VMEM),
        compiler_params=pltpu.CompilerParams(vmem_limit_bytes=VMEM_LIMIT),
    )(g)


def _mod_fwd(c_all, ada_w, ada_b_cols):
    nl, _, nc = ada_w.shape

    def body(c_ref, w_ref, b_ref, o_ref):
        cv = c_ref[...]
        act = (cv * _sig(cv)).astype(MXU)
        o_ref[...] = _dot(act, w_ref[...].astype(MXU), NN) + b_ref[...]

    return pl.pallas_call(
        body, name="mod_fwd", grid=(nl,),
        in_specs=[_full((16, D)), pl.BlockSpec((None, D, nc), lambda i: (i, 0, 0)),
                  pl.BlockSpec((None, 1, nc), lambda i: (i, 0, 0))],
        out_specs=pl.BlockSpec((None, 16, nc), lambda i: (i, 0, 0)),
        out_shape=jax.ShapeDtypeStruct((nl, 16, nc), F32), compiler_params=_params(1),
    )(c_all, ada_w, ada_b_cols)


def _mod_bwd(c_all, dmod_cols):
    nl, _, nc = dmod_cols.shape

    def body(c_ref, d_ref, o_ref):
        cv = c_ref[...]
        act = (cv * _sig(cv)).astype(MXU)
        o_ref[...] = _dot(act, d_ref[...].astype(MXU), TN)

    return pl.pallas_call(
        body, name="mod_bwd", grid=(nl,),
        in_specs=[_full((16, D)), pl.BlockSpec((None, 16, nc), lambda i: (i, 0, 0))],
        out_specs=pl.BlockSpec((None, D, nc), lambda i: (i, 0, 0)),
        out_shape=jax.ShapeDtypeStruct((nl, D, nc), F32), compiler_params=_params(1),
    )(c_all, dmod_cols)


def _in_proj(x, modrows, vec, w_in):
    L = x.shape[0]
    T = _tile(L, 256)

    def body(x_ref, mod_ref, vec_ref, w_ref, p_ref, h_ref):
        n, _ = _rms(x_ref[...])
        h = n * vec_ref[0:1, :] * (1.0 + mod_ref[1:2, :]) + mod_ref[0:1, :]
        hb = h.astype(MXU)
        h_ref[...] = hb
        p_ref[...] = _dot(hb, w_ref[...], NN)

    return pl.pallas_call(
        body, name="in_proj", grid=(L // T,),
        in_specs=[pl.BlockSpec((T, D), lambda i: (i, 0)), _full((SUB, D)), _full((SUB, D)), _full((D, P_PAD))],
        out_specs=[pl.BlockSpec((T, P_PAD), lambda i: (i, 0)), pl.BlockSpec((T, D), lambda i: (i, 0))],
        out_shape=[jax.ShapeDtypeStruct((L, P_PAD), F32), jax.ShapeDtypeStruct((L, D), MXU)],
        compiler_params=_params(1),
    )(x, modrows, vec, w_in)


def _gate_small(s, sp_ref):
    lane = lax.broadcasted_iota(jnp.int32, s.shape, 1)
    a = -jnp.exp(sp_ref[0:1, :])
    xb = s + sp_ref[1:2, :]
    beta = _sig(s)
    g = a * _softplus(xb)
    return lane, a, xb, beta, g


def _pre_fwd(p, pa, cq, sp):
    L = p.shape[0]
    T = _tile(L, 256)
    scale = HD ** -0.5

    def body(pm_ref, ps_ref, pa_ref, cq_ref, sp_ref, qn_ref, kn_ref, vs_ref, gb_ref, ya_ref, u_carry, q_carry):
        @pl.when(pl.program_id(0) == 0)
        def _():
            u_carry[...] = jnp.zeros_like(u_carry)
            q_carry[...] = jnp.zeros_like(q_carry)

        a_b = pm_ref[:, 0:AW]
        u = pm_ref[:, AW:2 * AW] * pm_ref[:, 2 * AW:3 * AW]
        cu, _ = _conv_fwd(u, pa_ref, 3, u_carry[...])
        u_carry[...] = u[T - SUB:T, :]
        yp = a_b * cu
        ms = _dot_f32(yp * yp, _blockdiag_mean(AW, A_GROUP), NN, exact="b")
        ya_ref[...] = (yp * lax.rsqrt(ms + EPS) * pa_ref[3:4, :]).astype(MXU)

        qkv = pm_ref[:, 3 * AW:3 * AW + 3 * H * HD]
        qc, _ = _conv_fwd(qkv, cq_ref, 4, q_carry[...])
        q_carry[...] = qkv[T - SUB:T, :]
        qs = qc * _sig(qc)
        for h in range(H):
            q = qs[:, h * HD:(h + 1) * HD]
            qn_ref[:, h * HD:(h + 1) * HD] = q * (lax.rsqrt(jnp.sum(q * q, axis=-1, keepdims=True) + EPS) * scale)
            k = qs[:, (H + h) * HD:(H + h + 1) * HD]
            kn_ref[:, h * HD:(h + 1) * HD] = k * lax.rsqrt(jnp.sum(k * k, axis=-1, keepdims=True) + EPS)
        vs_ref[...] = qs[:, 2 * H * HD:3 * H * HD]

        lane, _, _, beta, g = _gate_small(ps_ref[...], sp_ref)
        gb_ref[...] = jnp.where(lane < H, beta, jnp.where(lane < 2 * H, g, 0.0))

    w3 = 3 * AW + 3 * H * HD
    row = lambda i: (i, 0)
    return pl.pallas_call(
        body, name="pre_fwd", grid=(L // T,),
        in_specs=[pl.BlockSpec((T, w3), row), pl.BlockSpec((T, LANES), lambda i: (i, (P_PAD - LANES) // LANES)),
                  _full((SUB, AW)), _full((SUB, 3 * H * HD)), _full((SUB, LANES))],
        out_specs=[pl.BlockSpec((T, H * HD), row)] * 3 + [pl.BlockSpec((T, LANES), row), pl.BlockSpec((T, AW), row)],
        out_shape=[jax.ShapeDtypeStruct((L, H * HD), F32)] * 3
        + [jax.ShapeDtypeStruct((L, LANES), F32), jax.ShapeDtypeStruct((L, AW), MXU)],
        scratch_shapes=[pltpu.VMEM((SUB, AW), F32), pltpu.VMEM((SUB, 3 * H * HD), F32)],
        compiler_params=_params(1),
    )(p, p, pa, cq, sp)


def _gdr_masks():
    r = lax.broadcasted_iota(jnp.int32, (CK, CK), 0)
    c = lax.broadcasted_iota(jnp.int32, (CK, CK), 1)
    return r >= c, r > c


def _head_cols(gbt, h):
    return gbt[:, h:h + 1], gbt[:, H + h:H + h + 1]


def _split(x, parts):
    out = []
    for _ in range(parts):
        hi = x.astype(jnp.bfloat16)
        out.append(hi)
        x = x - hi.astype(F32)
    return out


def _dot_f32(a, b, dims, exact=None):
    if exact == "a":
        ab = a.astype(jnp.bfloat16)
        return sum(_dot(ab, t, dims) for t in _split(b, 3))
    if exact == "b":
        bb = b.astype(jnp.bfloat16)
        return sum(_dot(t, bb, dims) for t in _split(a, 3))
    ah, al = _split(a, 2)
    bh, bl = _split(b, 2)
    return _dot(ah, bh, dims) + _dot(ah, bl, dims) + _dot(al, bh, dims)


def _gdr_consts():
    causal, strict = _gdr_masks()
    return dict(causal=causal, strict=strict, tril=jnp.where(causal, 1.0, 0.0).astype(F32),
                eye=jnp.where(causal & jnp.logical_not(strict), 1.0, 0.0).astype(F32),
                bcast=jnp.full((CK, HD), 1.0 / HD, F32))


def _dots(a, b, dims):
    return [_dot(x, y, dims) for x, y in zip(a, b)]


def _dots_f32(a, b, dims, exact=None):
    n = len(a)
    if exact == "a":
        lhs = [[x.astype(jnp.bfloat16)] * 3 for x in a]
        rhs = [_split(y, 3) for y in b]
    elif exact == "b":
        lhs = [_split(x, 3) for x in a]
        rhs = [[y.astype(jnp.bfloat16)] * 3 for y in b]
    else:
        sa = [_split(x, 2) for x in a]
        sb = [_split(y, 2) for y in b]
        lhs = [[s[0], s[0], s[1]] for s in sa]
        rhs = [[s[0], s[1], s[0]] for s in sb]
    terms = [[_dot(lhs[i][t], rhs[i][t], dims) for i in range(n)] for t in range(3)]
    return [terms[0][i] + terms[1][i] + terms[2][i] for i in range(n)]


def _gdr_local(q, k, v, beta, g, cst, tinv=None):
    n = len(q)
    R = range(n)
    causal, strict = cst["causal"], cst["strict"]
    gc = _dots_f32([cst["tril"]] * n, [jnp.broadcast_to(g[i], (CK, HD)) for i in R], NN, exact="a")
    g_row = _dots_f32([cst["bcast"]] * n, gc, NT, exact="a")
    decay = [jnp.where(causal, jnp.exp(jnp.where(causal, gc[i][:, 0:CK] - g_row[i], 0.0)), 0.0) for i in R]
    eg = [jnp.exp(gc[i]) for i in R]
    gl = [gc[i][CK - 1:CK, :] for i in R]
    ek = [jnp.exp(gl[i] - gc[i]) for i in R]
    cd = [jnp.exp(gl[i]) for i in R]
    kb = [k[i] * beta[i] for i in R]
    pk = _dots(kb, k, NT)
    if tinv is None:
        xp = [-jnp.where(strict, pk[i] * decay[i], 0.0) for i in R]
        tinv = [cst["eye"] + xp[i] for i in R]
        for _ in range(5):
            xp = _dots_f32(xp, xp, NN)
            tx = _dots_f32(tinv, xp, NN)
            tinv = [tinv[i] + tx[i] for i in R]
    u = _dots(tinv, [v[i] * beta[i] for i in R], NN)
    w = _dots(tinv, [kb[i] * eg[i] for i in R], NN)
    qk = _dots(q, k, NT)
    intra = [jnp.where(causal, qk[i] * decay[i], 0.0) for i in R]
    return dict(decay=decay, eg=eg, ek=ek, cd=cd, kb=kb, pk=pk, tinv=tinv, u=u, w=w, qk=qk, intra=intra,
                q_dec=[q[i] * eg[i] for i in R], k_dec=[k[i] * ek[i] for i in R])


GDR_SUB = 4


def _gdr_fwd(qn, kn, vs, gb):
    L = qn.shape[0]
    nc = L // CK
    cb = min(8, nc)
    rb = cb * CK
    nb = nc // cb
    nsub = GDR_SUB if cb % GDR_SUB == 0 else 1

    def body(q_ref, k_ref, v_ref, gb_ref, o_ref, st_ref, ti_ref, s_ref):
        @pl.when(pl.program_id(0) == 0)
        def _():
            s_ref[...] = jnp.zeros_like(s_ref)

        cst = _gdr_consts()
        heads = range(H)

        def group(gi, carry):
            rows = [pl.ds(pl.multiple_of((gi * nsub + j) * CK, CK), CK) for j in range(nsub)]
            chains = [(j, h) for j in range(nsub) for h in heads]
            gbt = [gb_ref[rows[j], :] for j in range(nsub)]
            cols = lambda h: slice(h * HD, (h + 1) * HD)
            t = _gdr_local([q_ref[rows[j], cols(h)] for j, h in chains], [k_ref[rows[j], cols(h)] for j, h in chains],
                           [v_ref[rows[j], cols(h)] for j, h in chains],
                           [_head_cols(gbt[j], h)[0] for j, h in chains], [_head_cols(gbt[j], h)[1] for j, h in chains], cst)
            s = [s_ref[h] for h in heads]
            for j in range(nsub):
                at = lambda key: [t[key][j * H + h] for h in heads]
                for h in heads:
                    st_ref[h, gi * nsub + j] = s[h]
                    ti_ref[h, gi * nsub + j] = t["tinv"][j * H + h]
                ws = _dots(at("w"), s, NN)
                v_new = [u_h - ws_h for u_h, ws_h in zip(at("u"), ws)]
                o_s = _dots(at("q_dec"), s, NN)
                o_v = _dots(at("intra"), v_new, NN)
                kv = _dots(at("k_dec"), v_new, TN)
                cd = at("cd")
                for h in heads:
                    o_ref[rows[j], cols(h)] = o_s[h] + o_v[h]
                s = [s[h] * cd[h] + kv[h] for h in heads]
            for h in heads:
                s_ref[h] = s[h]
            return carry

        lax.fori_loop(0, cb // nsub, group, 0)

    blk = pl.BlockSpec((rb, H * HD), lambda b: (b, 0))
    return pl.pallas_call(
        body, name="gdr_fwd", grid=(nb,),
        in_specs=[blk, blk, blk, pl.BlockSpec((rb, LANES), lambda b: (b, 0))],
        out_specs=[blk, pl.BlockSpec((H, cb, HD, HD), lambda b: (0, b, 0, 0)),
                   pl.BlockSpec((H, cb, CK, CK), lambda b: (0, b, 0, 0))],
        out_shape=[jax.ShapeDtypeStruct((L, H * HD), F32), jax.ShapeDtypeStruct((H, nc, HD, HD), F32),
                   jax.ShapeDtypeStruct((H, nc, CK, CK), F32)],
        scratch_shapes=[pltpu.VMEM((H, HD, HD), F32)],
        compiler_params=_params(1),
    )(qn, kn, vs, gb)


def _gdr_bwd(qn, kn, vs, gb, states, tinvs, do):
    L = qn.shape[0]
    nc = L // CK
    cb = min(8, nc)
    rb = cb * CK
    nb = nc // cb
    nsub = GDR_SUB if cb % GDR_SUB == 0 else 1

    def body(q_ref, k_ref, v_ref, gb_ref, st_ref, ti_ref, do_ref, dq_ref, dk_ref, dv_ref, dgb_ref, ds_ref):
        @pl.when(pl.program_id(0) == 0)
        def _():
            ds_ref[...] = jnp.zeros_like(ds_ref)

        cst = _gdr_consts()
        causal, strict = cst["causal"], cst["strict"]
        ones = jnp.ones((CK, HD), F32)
        row = lax.broadcasted_iota(jnp.int32, (CK, HD), 0)
        lane = lax.broadcasted_iota(jnp.int32, (CK, LANES), 1)

        heads = range(H)
        rsum = lambda x: jnp.sum(x, axis=-1, keepdims=True)

        def group(gj, carry):
            gi = cb // nsub - 1 - gj
            rows = [pl.ds(pl.multiple_of((gi * nsub + j) * CK, CK), CK) for j in range(nsub)]
            chains = [(j, h) for j in range(nsub) for h in heads]
            gbt = [gb_ref[rows[j], :] for j in range(nsub)]
            cols = lambda h: slice(h * HD, (h + 1) * HD)
            q_all = [q_ref[rows[j], cols(h)] for j, h in chains]
            k_all = [k_ref[rows[j], cols(h)] for j, h in chains]
            v_all = [v_ref[rows[j], cols(h)] for j, h in chains]
            beta_all = [_head_cols(gbt[j], h)[0] for j, h in chains]
            t = _gdr_local(q_all, k_all, v_all, beta_all, [_head_cols(gbt[j], h)[1] for j, h in chains], cst,
                           tinv=[ti_ref[h, gi * nsub + j] for j, h in chains])
            ds_out = [ds_ref[h] for h in heads]
            for j in reversed(range(nsub)):
                at = lambda key: [t[key][j * H + h] for h in heads]
                pick = lambda lst: [lst[j * H + h] for h in heads]
                q, k, v, beta = pick(q_all), pick(k_all), pick(v_all), pick(beta_all)
                u, w, tinv, decay = at("u"), at("w"), at("tinv"), at("decay")
                eg, ek, cd, kb = at("eg"), at("ek"), at("cd"), at("kb")
                q_dec, k_dec, intra, pk, qk = at("q_dec"), at("k_dec"), at("intra"), at("pk"), at("qk")
                s = [st_ref[h, gi * nsub + j] for h in heads]
                dout = [do_ref[rows[j], cols(h)] for h in heads]

                ws = _dots(w, s, NN)
                v_new = [u[h] - ws[h] for h in heads]
                dq_dec = _dots(dout, s, NT)
                qd = _dots(q_dec, dout, TN)
                di = _dots(dout, v_new, NT)
                dintra = [jnp.where(causal, di[h], 0.0) for h in heads]
                ido = _dots(intra, dout, TN)
                kds = _dots(k_dec, ds_out, NN)
                dv_new = [ido[h] + kds[h] for h in heads]
                dk_dec = _dots(v_new, ds_out, NT)
                dcd = [jnp.sum(jnp.sum(ds_out[h] * s[h], axis=1, keepdims=True), axis=0, keepdims=True) for h in heads]
                dvs = _dots(dv_new, s, NT)
                dw = [-dvs[h] for h in heads]
                wdv = _dots(w, dv_new, TN)
                ds_new = [qd[h] + ds_out[h] * cd[h] - wdv[h] for h in heads]
                dru = _dots(tinv, dv_new, TN)
                drw = _dots(tinv, dw, TN)
                dl1 = _dots(dru, u, NT)
                dl2 = _dots(drw, w, NT)
                dlower = [-jnp.where(strict, dl1[h] + dl2[h], 0.0) for h in heads]
                dv = [dru[h] * beta[h] for h in heads]
                dbeta = [rsum(dru[h] * v[h]) for h in heads]
                dgc = [rsum(drw[h] * kb[h]) * eg[h] for h in heads]
                dpk = [dlower[h] * decay[h] for h in heads]
                dqk = [dintra[h] * decay[h] for h in heads]
                dpk_k = _dots(dpk, k, NN)
                dkb = [drw[h] * eg[h] + dpk_k[h] for h in heads]
                dk1 = _dots(dpk, kb, TN)
                dq1 = _dots(dqk, k, NN)
                dk2 = _dots(dqk, q, TN)
                m = [(dlower[h] * pk[h] + dintra[h] * qk[h]) * decay[h] for h in heads]
                mcol = _dots_f32(m, [ones] * H, TN, exact="b")
                e = [rsum(dk_dec[h] * k_dec[h]) for h in heads]
                dgl = [jnp.sum(e[h], axis=0, keepdims=True) + dcd[h] * cd[h] for h in heads]
                dgc = [dgc[h] + rsum(m[h]) - mcol[h] + rsum(dq_dec[h] * q_dec[h]) - e[h]
                       + jnp.where(row == CK - 1, dgl[h], 0.0) for h in heads]
                dg = _dots_f32([cst["tril"]] * H, dgc, TN, exact="a")
                dgb = jnp.zeros((CK, LANES), F32)
                for h in heads:
                    dq_ref[rows[j], cols(h)] = dq1[h] + dq_dec[h] * eg[h]
                    dk_ref[rows[j], cols(h)] = dk1[h] + dk2[h] + dk_dec[h] * ek[h] + dkb[h] * beta[h]
                    dv_ref[rows[j], cols(h)] = dv[h]
                    db = dbeta[h] + rsum(dkb[h] * k[h])
                    dgb = dgb + jnp.where(lane == h, db, 0.0) + jnp.where(lane == H + h, dg[h], 0.0)
                dgb_ref[rows[j], :] = dgb
                ds_out = ds_new
            for h in heads:
                ds_ref[h] = ds_out[h]
            return carry

        lax.fori_loop(0, cb // nsub, group, 0)

    blk = pl.BlockSpec((rb, H * HD), lambda b: (nb - 1 - b, 0))
    sblk = pl.BlockSpec((rb, LANES), lambda b: (nb - 1 - b, 0))
    return pl.pallas_call(
        body, name="gdr_bwd", grid=(nb,),
        in_specs=[blk, blk, blk, sblk, pl.BlockSpec((H, cb, HD, HD), lambda b: (0, nb - 1 - b, 0, 0)),
                  pl.BlockSpec((H, cb, CK, CK), lambda b: (0, nb - 1 - b, 0, 0)), blk],
        out_specs=[blk, blk, blk, sblk],
        out_shape=[jax.ShapeDtypeStruct((L, H * HD), F32)] * 3 + [jax.ShapeDtypeStruct((L, LANES), F32)],
        scratch_shapes=[pltpu.VMEM((H, HD, HD), F32)],
        compiler_params=_params(1),
    )(qn, kn, vs, gb, states, tinvs, do)


def _post_fwd(o, p, ya, x, modrows, sp, w_out):
    L = x.shape[0]
    T = _tile(L, 256)

    def body(o_ref, z_ref, ya_ref, x_ref, mod_ref, sp_ref, w_ref, y_ref, x2_ref, yb_ref):
        ndw = sp_ref[2:3, :]
        z = z_ref[...]
        sz = z * _sig(z)
        parts = []
        for h in range(H):
            n, _ = _rms(o_ref[:, h * HD:(h + 1) * HD])
            parts.append(n * ndw * sz[:, h * HD:(h + 1) * HD])
        yb = jnp.concatenate(parts, axis=-1).astype(MXU)
        yb_ref[...] = yb
        y = _dot(ya_ref[...], w_ref[0:AW, :], NN) + _dot(yb, w_ref[AW:2 * AW, :], NN)
        y_ref[...] = y
        x2_ref[...] = x_ref[...] + mod_ref[2:3, :] * y

    row = lambda i: (i, 0)
    zcol = (3 * AW + 3 * H * HD) // (H * HD)
    return pl.pallas_call(
        body, name="post_fwd", grid=(L // T,),
        in_specs=[pl.BlockSpec((T, H * HD), row), pl.BlockSpec((T, H * HD), lambda i: (i, zcol)),
                  pl.BlockSpec((T, AW), row), pl.BlockSpec((T, D), row), _full((SUB, D)), _full((SUB, LANES)),
                  _full((D, D))],
        out_specs=[pl.BlockSpec((T, D), row), pl.BlockSpec((T, D), row), pl.BlockSpec((T, H * HD), row)],
        out_shape=[jax.ShapeDtypeStruct((L, D), F32), jax.ShapeDtypeStruct((L, D), F32),
                   jax.ShapeDtypeStruct((L, H * HD), MXU)],
        compiler_params=_params(1),
    )(o, p, ya, x, modrows, sp, w_out)


FF_COLS = 2
FF_CW = DFF // FF_COLS
FF_ROWS = 256


def _ffn_fwd_half(x2, modrows, vec, w_up, cff, w_down, j, d_prev):
    assert FF_COLS == 2
    L = x2.shape[0]
    T = _tile(L, FF_ROWS)
    nj = FF_COLS
    last = d_prev is not None

    def body(*refs):
        x_ref, mod_ref, vec_ref, wg_ref, wu_ref, cg_ref, cu_ref, wd_ref = refs[:8]
        if last:
            dp_ref, gp_ref, up_ref, gc_ref, uc_ref, f_ref, d_ref, x3_ref, carry_g, carry_u = refs[8:]
        else:
            h_ref, gp_ref, up_ref, gc_ref, uc_ref, f_ref, d_ref, carry_g, carry_u = refs[8:]

        @pl.when(pl.program_id(0) == 0)
        def _():
            carry_g[...] = jnp.zeros_like(carry_g)
            carry_u[...] = jnp.zeros_like(carry_u)

        xv = x_ref[...]
        n, _ = _rms(xv)
        hb = (n * vec_ref[1:2, :] * (1.0 + mod_ref[4:5, :]) + mod_ref[3:4, :]).astype(MXU)
        if not last:
            h_ref[...] = hb
        g = _dot(hb, wg_ref[...], NN)
        u = _dot(hb, wu_ref[...], NN)
        gp_ref[...] = g
        up_ref[...] = u
        gc, _ = _conv_fwd(g, cg_ref, 3, carry_g[...])
        uc, _ = _conv_fwd(u, cu_ref, 3, carry_u[...])
        carry_g[...] = g[T - SUB:T, :]
        carry_u[...] = u[T - SUB:T, :]
        gc_ref[...] = gc.astype(MXU)
        uc_ref[...] = uc.astype(MXU)
        fb = (gc * _sig(gc) * uc).astype(MXU)
        f_ref[...] = fb
        part = _dot(fb, wd_ref[...], NN)
        if last:
            dv = dp_ref[...] + part
            d_ref[...] = dv
            x3_ref[...] = xv + mod_ref[5:6, :] * dv
        else:
            d_ref[...] = part

    row = lambda i: (i, 0)
    rowD = pl.BlockSpec((T, D), row)
    rowC = pl.BlockSpec((T, FF_CW), row)
    in_specs = [rowD, _full((SUB, D)), _full((SUB, D)),
                pl.BlockSpec((D, FF_CW), lambda i: (0, j)), pl.BlockSpec((D, FF_CW), lambda i: (0, nj + j)),
                pl.BlockSpec((SUB, FF_CW), lambda i: (0, j)), pl.BlockSpec((SUB, FF_CW), lambda i: (0, nj + j)),
                pl.BlockSpec((FF_CW, D), lambda i: (j, 0))]
    half = [jax.ShapeDtypeStruct((L, FF_CW), F32), jax.ShapeDtypeStruct((L, FF_CW), F32),
            jax.ShapeDtypeStruct((L, FF_CW), MXU), jax.ShapeDtypeStruct((L, FF_CW), MXU),
            jax.ShapeDtypeStruct((L, FF_CW), MXU)]
    args = [x2, modrows, vec, w_up, w_up, cff, cff, w_down]
    if last:
        in_specs.append(rowD)
        args.append(d_prev)
        out_specs = [rowC] * 5 + [rowD, rowD]
        out_shape = half + [jax.ShapeDtypeStruct((L, D), F32), jax.ShapeDtypeStruct((L, D), F32)]
    else:
        out_specs = [rowD] + [rowC] * 5 + [rowD]
        out_shape = [jax.ShapeDtypeStruct((L, D), MXU)] + half + [jax.ShapeDtypeStruct((L, D), F32)]
    return pl.pallas_call(
        body, name="ffn_fwd_last" if last else "ffn_fwd_first", grid=(L // T,),
        in_specs=in_specs, out_specs=out_specs, out_shape=out_shape,
        scratch_shapes=[pltpu.VMEM((SUB, FF_CW), F32), pltpu.VMEM((SUB, FF_CW), F32)],
        compiler_params=_params(1),
    )(*args)


def _ffn_bwd_half(dx3, modrows, gpre, upre, gcv, ucv, cff, w_down, w_up, j, tail):
    assert FF_COLS == 2
    L = dx3.shape[0]
    T = _tile(L, FF_ROWS)
    ni, nj = L // T, FF_COLS
    last = tail is not None

    def body(*refs):
        dx3_ref, mod_ref, gp_ref, up_ref, gc_ref, uc_ref, cg_ref, cu_ref, wd_ref, wg_ref, wu_ref = refs[:11]
        if last:
            (d_ref, x2_ref, vec_ref, dhp_ref, dgp_ref, dup_ref, dx2_ref, accv_ref, dcg_ref, dcu_ref,
             carry_g, carry_u) = refs[11:]
        else:
            dd_ref, dgp_ref, dup_ref, dh_ref, dcg_ref, dcu_ref, carry_g, carry_u = refs[11:]
        i = pl.program_id(0)

        @pl.when(i == 0)
        def _():
            carry_g[...] = jnp.zeros_like(carry_g)
            carry_u[...] = jnp.zeros_like(carry_u)
            dcg_ref[...] = jnp.zeros_like(dcg_ref)
            dcu_ref[...] = jnp.zeros_like(dcu_ref)
            if last:
                accv_ref[...] = jnp.zeros_like(accv_ref)

        dx3v = dx3_ref[...]
        ddb = (mod_ref[5:6, :] * dx3v).astype(MXU)
        if not last:
            dd_ref[...] = ddb
        g, u = gp_ref[...], up_ref[...]
        gc, uc = gc_ref[...].astype(F32), uc_ref[...].astype(F32)
        sg = _sig(gc)
        df = _dot(ddb, wd_ref[...], NT)
        duc = df * (gc * sg)
        dgc = df * uc * (sg * (1.0 + gc * (1.0 - sg)))
        dgs = [dgc] + [_shift_up(dgc, s, carry_g[...]) for s in (1, 2)]
        dus = [duc] + [_shift_up(duc, s, carry_u[...]) for s in (1, 2)]
        for s in range(3):
            dcg_ref[2 - s:3 - s, :] += _sum0(dgs[s] * g)
            dcu_ref[2 - s:3 - s, :] += _sum0(dus[s] * u)
        dg = (cg_ref[2:3, :] * dgs[0] + cg_ref[1:2, :] * dgs[1] + cg_ref[0:1, :] * dgs[2]).astype(MXU)
        du = (cu_ref[2:3, :] * dus[0] + cu_ref[1:2, :] * dus[1] + cu_ref[0:1, :] * dus[2]).astype(MXU)
        carry_g[...] = dgc[0:SUB, :]
        carry_u[...] = duc[0:SUB, :]
        dgp_ref[...] = dg
        dup_ref[...] = du
        dh = _dot(dg, wg_ref[...], NT) + _dot(du, wu_ref[...], NT)
        if last:
            dh = dh + dhp_ref[...]
            accv_ref[0:1, :] += _sum0(dx3v * d_ref[...])
            n, r = _rms(x2_ref[...])
            nw, sc = vec_ref[1:2, :], mod_ref[4:5, :]
            accv_ref[1:2, :] += _sum0(dh)
            accv_ref[2:3, :] += _sum0(dh * n * nw)
            accv_ref[3:4, :] += _sum0(dh * n * (1.0 + sc))
            dx2_ref[...] = _rms_bwd(dh * nw * (1.0 + sc), n, r) + dx3v
        else:
            dh_ref[...] = dh

    row = lambda i: (ni - 1 - i, 0)
    rowD = pl.BlockSpec((T, D), row)
    rowC = pl.BlockSpec((T, FF_CW), row)
    in_specs = [rowD, _full((SUB, D)), rowC, rowC, rowC, rowC,
                pl.BlockSpec((SUB, FF_CW), lambda i: (0, j)), pl.BlockSpec((SUB, FF_CW), lambda i: (0, nj + j)),
                pl.BlockSpec((FF_CW, D), lambda i: (j, 0)),
                pl.BlockSpec((D, FF_CW), lambda i: (0, j)), pl.BlockSpec((D, FF_CW), lambda i: (0, nj + j))]
    args = [dx3, modrows, gpre, upre, gcv, ucv, cff, cff, w_down, w_up, w_up]
    halfb = [jax.ShapeDtypeStruct((L, FF_CW), MXU), jax.ShapeDtypeStruct((L, FF_CW), MXU)]
    dconv = [jax.ShapeDtypeStruct((SUB, FF_CW), F32)] * 2
    if last:
        d, x2, vec, dh_prev = tail
        in_specs += [rowD, rowD, _full((SUB, D)), rowD]
        args += [d, x2, vec, dh_prev]
        out_specs = [rowC, rowC, rowD, _full((SUB, D)), _full((SUB, FF_CW)), _full((SUB, FF_CW))]
        out_shape = halfb + [jax.ShapeDtypeStruct((L, D), F32), jax.ShapeDtypeStruct((SUB, D), F32)] + dconv
    else:
        out_specs = [rowD, rowC, rowC, rowD, _full((SUB, FF_CW)), _full((SUB, FF_CW))]
        out_shape = [jax.ShapeDtypeStruct((L, D), MXU)] + halfb + [jax.ShapeDtypeStruct((L, D), F32)] + dconv
    return pl.pallas_call(
        body, name="ffn_bwd_last" if last else "ffn_bwd_first", grid=(ni,),
        in_specs=in_specs, out_specs=out_specs, out_shape=out_shape,
        scratch_shapes=[pltpu.VMEM((SUB, FF_CW), F32), pltpu.VMEM((SUB, FF_CW), F32)],
        compiler_params=_params(1),
    )(*args)


def _final(x, target, nf):
    L = x.shape[0]
    T = _tile(L, 256)

    def body(x_ref, t_ref, nf_ref, dx_ref, acc_ref):
        @pl.when(pl.program_id(0) == 0)
        def _():
            acc_ref[...] = jnp.zeros_like(acc_ref)

        n, r = _rms(x_ref[...])
        w = nf_ref[0:1, :]
        err = n * w - t_ref[...]
        acc_ref[0:1, :] += (0.5 / D) * _sum0(err * err)
        dy = err * (1.0 / D)
        acc_ref[1:2, :] += _sum0(dy * n)
        dx_ref[...] = _rms_bwd(dy * w, n, r)

    row = lambda i: (i, 0)
    return pl.pallas_call(
        body, name="final_norm_loss", grid=(L // T,),
        in_specs=[pl.BlockSpec((T, D), row), pl.BlockSpec((T, D), row), _full((SUB, D))],
        out_specs=[pl.BlockSpec((T, D), row), _full((SUB, D))],
        out_shape=[jax.ShapeDtypeStruct((L, D), F32), jax.ShapeDtypeStruct((SUB, D), F32)],
        compiler_params=_params(1),
    )(x, target, nf)


def _post_bwd(dx2, y, o, p, modrows, sp, w_out):
    L = dx2.shape[0]
    T = _tile(L, 256)

    def body(dx2_ref, y_ref, o_ref, z_ref, mod_ref, sp_ref, w_ref, dy_ref, do_ref, dz_ref, dya_ref, accv_ref, accs_ref):
        @pl.when(pl.program_id(0) == 0)
        def _():
            accv_ref[...] = jnp.zeros_like(accv_ref)
            accs_ref[...] = jnp.zeros_like(accs_ref)

        dx2v = dx2_ref[...]
        accv_ref[0:1, :] += _sum0(dx2v * y_ref[...])
        dyb = (mod_ref[2:3, :] * dx2v).astype(MXU)
        dy_ref[...] = dyb
        dyc = _dot(dyb, w_ref[...], NT)
        dya_ref[...] = dyc[:, 0:AW]
        ndw = sp_ref[2:3, :]
        z = z_ref[...]
        sgz = _sig(z)
        dsz = sgz * (1.0 + z * (1.0 - sgz))
        dndw = jnp.zeros((1, HD), F32)
        for h in range(H):
            sl = slice(h * HD, (h + 1) * HD)
            n, r = _rms(o_ref[:, sl])
            dyh = dyc[:, AW + h * HD:AW + (h + 1) * HD]
            zh = z[:, sl]
            don = dyh * (zh * sgz[:, sl])
            dz_ref[:, sl] = dyh * (n * ndw) * dsz[:, sl]
            dndw = dndw + _sum0(don * n)
            do_ref[:, sl] = _rms_bwd(don * ndw, n, r)
        accs_ref[0:1, :] += dndw

    row = lambda i: (i, 0)
    zcol = (3 * AW + 3 * H * HD) // (H * HD)
    return pl.pallas_call(
        body, name="post_bwd", grid=(L // T,),
        in_specs=[pl.BlockSpec((T, D), row), pl.BlockSpec((T, D), row), pl.BlockSpec((T, H * HD), row),
                  pl.BlockSpec((T, H * HD), lambda i: (i, zcol)), _full((SUB, D)), _full((SUB, LANES)), _full((D, D))],
        out_specs=[pl.BlockSpec((T, D), row)] + [pl.BlockSpec((T, H * HD), row)] * 3 + [_full((SUB, D)), _full((SUB, LANES))],
        out_shape=[jax.ShapeDtypeStruct((L, D), MXU)] + [jax.ShapeDtypeStruct((L, H * HD), F32)] * 3
        + [jax.ShapeDtypeStruct((SUB, D), F32), jax.ShapeDtypeStruct((SUB, LANES), F32)],
        compiler_params=_params(1),
    )(dx2, y, o, p, modrows, sp, w_out)


def _pre_bwd(p, dqn, dkn, dvs, dya, dz, dgb, pa, cq, sp):
    L = p.shape[0]
    T = _tile(L, 256)
    ni = L // T
    scale = HD ** -0.5
    w3 = 3 * AW + 3 * H * HD
    hb_per_t = T // SUB

    def body(pm_ref, ph_ref, ps_ref, dq_ref, dk_ref, dv_ref, dya_ref, dz_ref, dgb_ref, pa_ref, cq_ref, sp_ref,
             dp_ref, dpa_ref, dcq_ref, dsp_ref, carry_u, carry_q):
        i = pl.program_id(0)
        ri = ni - 1 - i

        @pl.when(i == 0)
        def _():
            dpa_ref[...] = jnp.zeros_like(dpa_ref)
            dcq_ref[...] = jnp.zeros_like(dcq_ref)
            dsp_ref[...] = jnp.zeros_like(dsp_ref)
            carry_u[...] = jnp.zeros_like(carry_u)
            carry_q[...] = jnp.zeros_like(carry_q)

        keep = jnp.where(ri == 0, 0.0, 1.0)
        a_b, a_c, a_x = pm_ref[:, 0:AW], pm_ref[:, AW:2 * AW], pm_ref[:, 2 * AW:3 * AW]
        u = a_c * a_x
        hu = ph_ref[:, AW:2 * AW] * ph_ref[:, 2 * AW:3 * AW] * keep
        cu, ush = _conv_fwd(u, pa_ref, 3, hu)
        yp = a_b * cu
        bd = _blockdiag_mean(AW, A_GROUP)
        ra = lax.rsqrt(_dot_f32(yp * yp, bd, NN, exact="b") + EPS)
        na = yp * ra
        dya = dya_ref[...]
        dpa_ref[3:4, :] += _sum0(dya * na)
        dna = dya * pa_ref[3:4, :]
        dyp = ra * (dna - na * _dot_f32(dna * na, bd, NN, exact="b"))
        dcu = dyp * a_b
        for s in range(3):
            dpa_ref[2 - s:3 - s, :] += _sum0(dcu * ush[s])
        du = _conv_bwd_in(dcu, pa_ref, 3, carry_u[...])
        carry_u[...] = dcu[0:SUB, :]
        dp_ref[:, 0:AW] = (dyp * cu).astype(MXU)
        dp_ref[:, AW:2 * AW] = (du * a_x).astype(MXU)
        dp_ref[:, 2 * AW:3 * AW] = (du * a_c).astype(MXU)

        qkv = pm_ref[:, 3 * AW:w3]
        qc, qsh = _conv_fwd(qkv, cq_ref, 4, ph_ref[:, 3 * AW:w3] * keep)
        sg = _sig(qc)
        qs = qc * sg
        parts = []
        for h in range(H):
            q = qs[:, h * HD:(h + 1) * HD]
            rq = lax.rsqrt(jnp.sum(q * q, axis=-1, keepdims=True) + EPS)
            parts.append(_l2_bwd(dq_ref[:, h * HD:(h + 1) * HD] * scale, q * rq, rq))
        for h in range(H):
            k = qs[:, (H + h) * HD:(H + h + 1) * HD]
            rk = lax.rsqrt(jnp.sum(k * k, axis=-1, keepdims=True) + EPS)
            parts.append(_l2_bwd(dk_ref[:, h * HD:(h + 1) * HD], k * rk, rk))
        parts.append(dv_ref[...])
        dqc = jnp.concatenate(parts, axis=-1) * (sg * (1.0 + qc * (1.0 - sg)))
        for s in range(4):
            dcq_ref[3 - s:4 - s, :] += _sum0(dqc * qsh[s])
        dp_ref[:, 3 * AW:w3] = _conv_bwd_in(dqc, cq_ref, 4, carry_q[...]).astype(MXU)
        carry_q[...] = dqc[0:SUB, :]
        dp_ref[:, w3:w3 + H * HD] = dz_ref[...].astype(MXU)

        lane, a, xb, beta, g = _gate_small(ps_ref[...], sp_ref)
        dgb = dgb_ref[...]
        dbeta = jnp.where(lane < H, dgb, 0.0)
        dg = jnp.where((lane >= H) & (lane < 2 * H), dgb, 0.0)
        dalpha = dg * a * _sig(xb)
        dsp_ref[0:1, :] += _sum0(dg * g)
        dsp_ref[1:2, :] += _sum0(dalpha)
        dp_ref[:, w3 + H * HD:P_PAD] = (dbeta * beta * (1.0 - beta) + dalpha).astype(MXU)

    row = lambda i: (ni - 1 - i, 0)
    halo = lambda i: (jnp.maximum((ni - 1 - i) * hb_per_t - 1, 0), 0)
    hrow = pl.BlockSpec((T, H * HD), row)
    return pl.pallas_call(
        body, name="pre_bwd", grid=(ni,),
        in_specs=[pl.BlockSpec((T, w3), row), pl.BlockSpec((SUB, w3), halo),
                  pl.BlockSpec((T, LANES), lambda i: (ni - 1 - i, (P_PAD - LANES) // LANES)),
                  hrow, hrow, hrow, pl.BlockSpec((T, AW), row), hrow,
                  pl.BlockSpec((T, LANES), row),
                  _full((SUB, AW)), _full((SUB, 3 * H * HD)), _full((SUB, LANES))],
        out_specs=[pl.BlockSpec((T, P_PAD), row), _full((SUB, AW)), _full((SUB, 3 * H * HD)), _full((SUB, LANES))],
        out_shape=[jax.ShapeDtypeStruct((L, P_PAD), MXU), jax.ShapeDtypeStruct((SUB, AW), F32),
                   jax.ShapeDtypeStruct((SUB, 3 * H * HD), F32), jax.ShapeDtypeStruct((SUB, LANES), F32)],
        scratch_shapes=[pltpu.VMEM((SUB, AW), F32), pltpu.VMEM((SUB, 3 * H * HD), F32)],
        compiler_params=_params(1),
    )(p, p, p, dqn, dkn, dvs, dya, dz, dgb, pa, cq, sp)


def _in_bwd(dp, w_in, x, dx2, modrows, vec):
    L = x.shape[0]
    T = _tile(L, 256)

    def body(dp_ref, w_ref, x_ref, dx2_ref, mod_ref, vec_ref, dx_ref, accv_ref):
        @pl.when(pl.program_id(0) == 0)
        def _():
            accv_ref[...] = jnp.zeros_like(accv_ref)

        dh = _dot(dp_ref[...], w_ref[...], NT)
        n, r = _rms(x_ref[...])
        nw, sc = vec_ref[0:1, :], mod_ref[1:2, :]
        accv_ref[0:1, :] += _sum0(dh)
        accv_ref[1:2, :] += _sum0(dh * n * nw)
        accv_ref[2:3, :] += _sum0(dh * n * (1.0 + sc))
        dx_ref[...] = _rms_bwd(dh * nw * (1.0 + sc), n, r) + dx2_ref[...]

    row = lambda i: (i, 0)
    return pl.pallas_call(
        body, name="in_bwd", grid=(L // T,),
        in_specs=[pl.BlockSpec((T, P_PAD), row), _full((D, P_PAD)), pl.BlockSpec((T, D), row),
                  pl.BlockSpec((T, D), row), _full((SUB, D)), _full((SUB, D))],
        out_specs=[pl.BlockSpec((T, D), row), _full((SUB, D))],
        out_shape=[jax.ShapeDtypeStruct((L, D), F32), jax.ShapeDtypeStruct((SUB, D), F32)],
        compiler_params=_params(1),
    )(dp, w_in, x, dx2, modrows, vec)


def _wgrad(a, b, tm, tn, name):
    L, m = a.shape
    n = b.shape[1]
    tl = _tile(L, 512)
    tm, tn = _tile(m, tm), _tile(n, tn)
    nl = L // tl

    def body(a_ref, b_ref, o_ref, acc):
        @pl.when(pl.program_id(2) == 0)
        def _():
            acc[...] = jnp.zeros_like(acc)

        acc[...] += _dot(a_ref[...], b_ref[...], TN)

        @pl.when(pl.program_id(2) == nl - 1)
        def _():
            o_ref[...] = acc[...].astype(o_ref.dtype)

    return pl.pallas_call(
        body, name=name, grid=(m // tm, n // tn, nl),
        in_specs=[pl.BlockSpec((tl, tm), lambda i, j, l: (l, i)), pl.BlockSpec((tl, tn), lambda i, j, l: (l, j))],
        out_specs=pl.BlockSpec((tm, tn), lambda i, j, l: (i, j)),
        out_shape=jax.ShapeDtypeStruct((m, n), MXU), scratch_shapes=[pltpu.VMEM((tm, tn), F32)],
        compiler_params=_params(3),
    )(a, b)


def _wgrad_cols(a, b, tm, n_shard, wpad, count, name):
    L, m = a.shape
    n = b.shape[1]
    tl = _tile(L, 512)
    tm = _tile(m, tm)
    nl = L // tl
    wins = _shard_windows(n_shard, count)
    assert all(a_ * LANES + win <= n for a_, _, win in wins), (wins, n)

    def body(a_ref, b_ref, o_ref, acc):
        @pl.when(pl.program_id(1) == 0)
        def _():
            acc[...] = jnp.zeros_like(acc)

        acc[...] += _dot(a_ref[...], b_ref[...], TN)

        @pl.when(pl.program_id(1) == nl - 1)
        def _():
            for k, (a_, s, win) in enumerate(wins):
                xk = acc[:, a_ * LANES:a_ * LANES + win]
                if s:
                    xk = pltpu.roll(xk, win - s, 1)
                o_ref[k] = _fit_lanes(xk, wpad).astype(o_ref.dtype)

    return pl.pallas_call(
        body, name=name, grid=(m // tm, nl),
        in_specs=[pl.BlockSpec((tl, tm), lambda i, l: (l, i)), pl.BlockSpec((tl, n), lambda i, l: (l, 0))],
        out_specs=pl.BlockSpec((count, tm, wpad), lambda i, l: (0, i, 0)),
        out_shape=jax.ShapeDtypeStruct((count, m, wpad), MXU),
        scratch_shapes=[pltpu.VMEM((tm, n), F32)],
        compiler_params=_params(2),
    )(a, b)


def _adamw(w, g, m, v, name):
    r, n = w.shape
    tr = _tile(r, 512)
    bc1 = 1.0 - ADAM_B1 ** ADAM_STEP
    bc2 = 1.0 - ADAM_B2 ** ADAM_STEP

    def body(w_ref, g_ref, m_ref, v_ref, d_ref, nm_ref, nv_ref):
        gv = g_ref[...]
        nm = ADAM_B1 * m_ref[...] + (1.0 - ADAM_B1) * gv
        nv = ADAM_B2 * v_ref[...] + (1.0 - ADAM_B2) * (gv * gv)
        nm_ref[...] = nm
        nv_ref[...] = nv
        d_ref[...] = -ADAM_LR * ((nm / bc1) / (jnp.sqrt(nv / bc2) + ADAM_EPS) + ADAM_WD * w_ref[...])

    spec = pl.BlockSpec((tr, n), lambda i: (i, 0))
    return pl.pallas_call(
        body, name=name, grid=(r // tr,), in_specs=[spec] * 4, out_specs=[spec] * 3,
        out_shape=[jax.ShapeDtypeStruct((r, n), F32)] * 3, compiler_params=_params(1),
    )(w, g, m, v)


def _rows8(rows, width):
    out = jnp.zeros((SUB, width), F32)
    for r, vrow in enumerate(rows):
        out = out.at[r, :vrow.shape[0]].set(vrow)
    return out


def _at_lanes(v4, start):
    return jnp.zeros((LANES,), F32).at[start:start + v4.shape[0]].set(v4)


def _pad_rows(flat, mult):
    n = flat.shape[0]
    pad = (-n) % mult
    return jnp.pad(flat, (0, pad)) if pad else flat


IN_PAD = 512
UP_PAD = 768


def _local_fwd_bwd(x, target, mod_full, small_w, full_w, on_grads=None):
    norm1_w, norm2_w, norm_a_w, a_log, dt_bias, norm_dn_w, norm_f_w = small_w
    w_in_f, w_out_f, w_up_f, w_down_f, conv_a_f, conv_q_f, conv_f_f = full_w

    def layer_params(i):
        modrows = jnp.concatenate([mod_full[i], jnp.zeros((SUB - N_MOD, D), F32)], axis=0)
        vec = _rows8([norm1_w[i], norm2_w[i]], D)
        pa = _rows8([conv_a_f[i, 0], conv_a_f[i, 1], conv_a_f[i, 2], norm_a_w[i]], AW)
        cq = _rows8([conv_q_f[i, k] for k in range(4)], 3 * H * HD)
        sp = _rows8([_at_lanes(a_log[i], H), _at_lanes(dt_bias[i], H), norm_dn_w[i]], LANES)
        cff = _rows8([conv_f_f[i, k] for k in range(3)], 2 * DFF)
        return modrows, vec, pa, cq, sp, cff

    saved = []
    xi = x
    for i in range(DEPTH):
        modrows, vec, pa, cq, sp, cff = layer_params(i)
        p, h1 = _in_proj(xi, modrows, vec, w_in_f[i])
        qn, kn, vs, gb, ya = _pre_fwd(p, pa, cq, sp)
        o, states, tinvs = _gdr_fwd(qn, kn, vs, gb)
        y, x2, yb = _post_fwd(o, p, ya, xi, modrows, sp, w_out_f[i])
        h2, gp0, up0, gc0, uc0, f0, d0 = _ffn_fwd_half(x2, modrows, vec, w_up_f[i], cff, w_down_f[i], 0, None)
        gp1, up1, gc1, uc1, f1, dff, x3 = _ffn_fwd_half(x2, modrows, vec, w_up_f[i], cff, w_down_f[i], 1, d0)
        saved.append(dict(x=xi, p=p, h1=h1, qn=qn, kn=kn, vs=vs, gb=gb, ya=ya, o=o, states=states, tinvs=tinvs, y=y, x2=x2, yb=yb,
                          h2=h2, gpre=(gp0, gp1), upre=(up0, up1), gc=(gc0, gc1), uc=(uc0, uc1), f=(f0, f1), d=dff))
        xi = x3

    dx, facc = _final(xi, target, _rows8([norm_f_w], D))
    loss_local = jnp.sum(facc[0])
    d_norm_f = facc[1]

    gw_in, gw_out, gw_up, gw_down = [None] * DEPTH, [None] * DEPTH, [None] * DEPTH, [None] * DEPTH
    g_small = [None] * DEPTH
    for i in reversed(range(DEPTH)):
        s = saved[i]
        modrows, vec, pa, cq, sp, cff = layer_params(i)
        dd, dgp0, dup0, dh0, dcg0, dcu0 = _ffn_bwd_half(dx, modrows, s["gpre"][0], s["upre"][0], s["gc"][0], s["uc"][0],
                                                        cff, w_down_f[i], w_up_f[i], 0, None)
        dgp1, dup1, dx2, accf, dcg1, dcu1 = _ffn_bwd_half(dx, modrows, s["gpre"][1], s["upre"][1], s["gc"][1], s["uc"][1],
                                                          cff, w_down_f[i], w_up_f[i], 1, (s["d"], s["x2"], vec, dh0))
        n_up, up_pad = 2 * DFF // N_DEV, UP_PAD
        gw_up[i] = jnp.concatenate([_wgrad_cols(s["h2"], t, 1024, n_up, up_pad, FF_CW // n_up, "wgrad_up")
                                    for t in (dgp0, dgp1, dup0, dup1)], axis=0)
        gw_down[i] = jnp.concatenate([_wgrad(s["f"][0], dd, FF_CW, 1024, "wgrad_down"),
                                      _wgrad(s["f"][1], dd, FF_CW, 1024, "wgrad_down")],
                                     axis=0).reshape(N_DEV, DFF // N_DEV, D)
        dy, do, dz, dya, accp, accs = _post_bwd(dx2, s["y"], s["o"], s["p"], modrows, sp, w_out_f[i])
        gw_out[i] = jnp.concatenate([_wgrad(s["ya"], dy, 512, 1024, "wgrad_out"),
                                     _wgrad(s["yb"], dy, 512, 1024, "wgrad_out")], axis=0).reshape(N_DEV, D // N_DEV, D)
        dqn, dkn, dvs, dgb = _gdr_bwd(s["qn"], s["kn"], s["vs"], s["gb"], s["states"], s["tinvs"], do)
        dp, dpa, dcq, dsp = _pre_bwd(s["p"], dqn, dkn, dvs, dya, dz, dgb, pa, cq, sp)
        gw_in[i] = _wgrad_cols(s["h1"], dp, 512, P_IN // N_DEV, IN_PAD, N_DEV, "wgrad_in")
        dx, acci = _in_bwd(dp, w_in_f[i], s["x"], dx2, modrows, vec)
        dconv_ff = jnp.concatenate([dcg0, dcg1, dcu0, dcu1], axis=1)[0:3]
        dmod = jnp.stack([acci[0], acci[1], accp[0], accf[1], accf[2], accf[0]])
        g_small[i] = dict(norm1=acci[2], norm2=accf[3], norm_a=dpa[3], a_log=dsp[0, H:2 * H], dt_bias=dsp[1, H:2 * H],
                          norm_dn=accs[0], conv_a=dpa[0:3], conv_qkv=dcq[0:4], conv_ff=dconv_ff, dmod=dmod.reshape(-1))
        if on_grads is not None:
            dx = on_grads(i, [gw_in[i], gw_out[i], gw_up[i], gw_down[i]], dx)
    return loss_local, dx, gw_in, gw_out, gw_up, gw_down, g_small, d_norm_f


def kernel(x, c, ada_w, ada_b, norm1_w, w_in, conv_a_w, norm_a_w, conv_qkv_w, a_log, dt_bias, norm_dn_w, w_out, norm2_w, w_up, conv_ff_w, w_down, norm_f_w, loss_target, m_ada_w, m_ada_b, m_norm1_w, m_w_in, m_conv_a_w, m_norm_a_w, m_conv_qkv_w, m_a_log, m_dt_bias, m_norm_dn_w, m_w_out, m_norm2_w, m_w_up, m_conv_ff_w, m_w_down, m_norm_f_w, v_ada_w, v_ada_b, v_norm1_w, v_w_in, v_conv_a_w, v_norm_a_w, v_conv_qkv_w, v_a_log, v_dt_bias, v_norm_dn_w, v_w_out, v_norm2_w, v_w_up, v_conv_ff_w, v_w_down, v_norm_f_w):
    ax, ay, ac = lax.axis_index("x"), lax.axis_index("y"), lax.axis_index("c")
    me = 4 * ax + 2 * ay + ac
    x = x[0]
    target = loss_target[0]
    n_in, n_up = P_IN // N_DEV, 2 * DFF // N_DEV

    def lane_pad(t, width):
        return jnp.pad(t.astype(MXU), ((0, 0), (0, 0), (0, width - t.shape[-1])))

    conv_blob = _pad_rows(jnp.concatenate([t.reshape(-1) for t in (conv_a_w, conv_qkv_w, conv_ff_w)]),
                          SUB * LANES).reshape(-1, LANES)
    c_rows = jnp.zeros((SUB, D), F32).at[0].set(c[0])
    send = [lane_pad(w_in, IN_PAD), w_out.astype(MXU), lane_pad(w_up, UP_PAD), w_down.astype(MXU)]
    got = [None] * DEPTH
    *got[0], g_conv, g_c = _all_gather([t[0] for t in send] + [conv_blob, c_rows], "gather_weights", in_vmem=False)
    for i in range(1, DEPTH):
        shards, _ = lax.optimization_barrier(([t[i] for t in send], g_c))
        got[i] = _all_gather_async(shards, "gather_weights_l%d" % i, collective_id=i)
    w_in_f = [_interleave_cols(g[0][:, None], n_in, P_PAD, "interleave_w_in")[0] for g in got]
    w_up_f = [_interleave_cols(g[2][:, None], n_up, 2 * DFF, "interleave_w_up")[0] for g in got]
    w_out_f = [g[1].reshape(D, D) for g in got]
    w_down_f = [g[3].reshape(DFF, D) for g in got]
    sg = g_conv.reshape(N_DEV, -1)
    o1 = conv_a_w.size
    o2 = o1 + conv_qkv_w.size
    o3 = o2 + conv_ff_w.size
    conv_a_f = sg[:, 0:o1].reshape(N_DEV, DEPTH, 3, AW // N_DEV).transpose(1, 2, 0, 3).reshape(DEPTH, 3, AW)
    conv_q_f = sg[:, o1:o2].reshape(N_DEV, DEPTH, 4, 3 * H * HD // N_DEV).transpose(1, 2, 0, 3).reshape(DEPTH, 4, 3 * H * HD)
    conv_f_f = sg[:, o2:o3].reshape(N_DEV, DEPTH, 3, n_up).transpose(1, 2, 0, 3).reshape(DEPTH, 3, 2 * DFF)

    c_all = jnp.concatenate([g_c[:, 0], jnp.zeros((16 - N_DEV, D), F32)], axis=0)
    n_ada = N_MOD * D // N_DEV
    ada_b_cols = lax.dynamic_slice_in_dim(ada_b, me * n_ada, n_ada, axis=1)[:, None, :]
    mod_sh = _mod_fwd(c_all, ada_w, ada_b_cols)
    mod_all = _all_gather([mod_sh.reshape(DEPTH * 16, n_ada)], "gather_mod", in_vmem=True)[0]
    mod_all = mod_all.reshape(N_DEV, DEPTH, 16, n_ada)
    mod_mine = lax.dynamic_index_in_dim(mod_all, me, axis=2, keepdims=False)
    mod_full = mod_mine.transpose(1, 0, 2).reshape(DEPTH, N_MOD, D)

    tags = ["w_in", "w_out", "w_up", "w_down"]
    received = [None] * DEPTH

    def on_grads(i, gs_i, dx):
        received[i] = _rs_exchange_async(gs_i, "rs_exchange_l%d" % i, collective_id=DEPTH + i)
        return dx

    loss_local, dx, _, _, _, _, g_small, d_norm_f = _local_fwd_bwd(
        x, target, mod_full, (norm1_w, norm2_w, norm_a_w, a_log, dt_bias, norm_dn_w, norm_f_w),
        (w_in_f, w_out_f, w_up_f, w_down_f, conv_a_f, conv_q_f, conv_f_f), on_grads)
    loss = lax.psum(loss_local, ("x", "y", "c"))
    grad_x = dx[None]

    keys = ["dmod", "norm1", "norm2", "norm_a", "a_log", "dt_bias", "norm_dn", "conv_a", "conv_qkv", "conv_ff"]
    stacked = {k: jnp.stack([g_small[i][k] for i in range(DEPTH)]) for k in keys}
    flat_parts = [stacked[k].reshape(-1) for k in keys] + [d_norm_f]
    sizes = [int(t.shape[0]) for t in flat_parts]
    sflat = _pad_rows(jnp.concatenate(flat_parts), SUB * LANES).reshape(-1, LANES)
    sall = _all_gather([sflat], "gather_small_grads", in_vmem=True)[0]
    ssum = _sum_devices(sall).reshape(-1)
    so = [0]
    for sz in sizes:
        so.append(so[-1] + sz)
    red = {k: ssum[so[n]:so[n + 1]].reshape(stacked[k].shape) for n, k in enumerate(keys)}
    g_norm_f = ssum[so[len(keys)]:so[len(keys) + 1]]
    dmod_all = sall[:, 0:sizes[0] // LANES, :].reshape(N_DEV, DEPTH, N_MOD * D)

    g_ada_b = red["dmod"].reshape(DEPTH, N_MOD * D)
    dmod_cols = lax.dynamic_slice_in_dim(dmod_all, me * n_ada, n_ada, axis=2).transpose(1, 0, 2)
    dmod_cols = jnp.concatenate([dmod_cols, jnp.zeros((DEPTH, 16 - N_DEV, n_ada), F32)], axis=1)
    g_ada_w = _mod_bwd(c_all, dmod_cols)
    g_conv_a = lax.dynamic_slice_in_dim(red["conv_a"], me * (AW // N_DEV), AW // N_DEV, axis=2)
    g_conv_qkv = lax.dynamic_slice_in_dim(red["conv_qkv"], me * (3 * H * HD // N_DEV), 3 * H * HD // N_DEV, axis=2)
    g_conv_ff = lax.dynamic_slice_in_dim(red["conv_ff"], me * n_up, n_up, axis=2)

    mine = [jnp.stack([_rs_sum(received[i][k], "rs_sum_" + t) for i in range(DEPTH)]) for k, t in enumerate(tags)]
    g_w_in = mine[0][:, :, :n_in]
    g_w_out = mine[1]
    g_w_up = mine[2][:, :, :n_up]
    g_w_down = mine[3]

    grads = dict(ada_w=g_ada_w, ada_b=g_ada_b, norm1_w=red["norm1"], w_in=g_w_in, conv_a_w=g_conv_a,
                 norm_a_w=red["norm_a"], conv_qkv_w=g_conv_qkv, a_log=red["a_log"], dt_bias=red["dt_bias"],
                 norm_dn_w=red["norm_dn"], w_out=g_w_out, norm2_w=red["norm2"], w_up=g_w_up, conv_ff_w=g_conv_ff,
                 w_down=g_w_down, norm_f_w=g_norm_f)
    weights = dict(ada_w=ada_w, ada_b=ada_b, norm1_w=norm1_w, w_in=w_in, conv_a_w=conv_a_w, norm_a_w=norm_a_w,
                   conv_qkv_w=conv_qkv_w, a_log=a_log, dt_bias=dt_bias, norm_dn_w=norm_dn_w, w_out=w_out,
                   norm2_w=norm2_w, w_up=w_up, conv_ff_w=conv_ff_w, w_down=w_down, norm_f_w=norm_f_w)
    ms = dict(ada_w=m_ada_w, ada_b=m_ada_b, norm1_w=m_norm1_w, w_in=m_w_in, conv_a_w=m_conv_a_w, norm_a_w=m_norm_a_w,
              conv_qkv_w=m_conv_qkv_w, a_log=m_a_log, dt_bias=m_dt_bias, norm_dn_w=m_norm_dn_w, w_out=m_w_out,
              norm2_w=m_norm2_w, w_up=m_w_up, conv_ff_w=m_conv_ff_w, w_down=m_w_down, norm_f_w=m_norm_f_w)
    vs_ = dict(ada_w=v_ada_w, ada_b=v_ada_b, norm1_w=v_norm1_w, w_in=v_w_in, conv_a_w=v_conv_a_w, norm_a_w=v_norm_a_w,
               conv_qkv_w=v_conv_qkv_w, a_log=v_a_log, dt_bias=v_dt_bias, norm_dn_w=v_norm_dn_w, w_out=v_w_out,
               norm2_w=v_norm2_w, w_up=v_w_up, conv_ff_w=v_conv_ff_w, w_down=v_w_down, norm_f_w=v_norm_f_w)
    names = list(weights)
    big_names = ["ada_w", "w_in", "w_out", "w_up", "w_down"]
    delta, new_m, new_v = {}, {}, {}
    for n in big_names:
        shp = weights[n].shape
        two = lambda t: t.reshape(-1, shp[-1])
        dl, nm, nv = _adamw(two(weights[n]), two(grads[n]), two(ms[n]), two(vs_[n]), "adamw_" + n)
        delta[n], new_m[n], new_v[n] = dl.reshape(shp), nm.reshape(shp), nv.reshape(shp)
    small_names = [n for n in names if n not in big_names]

    def pack(dct):
        return _pad_rows(jnp.concatenate([dct[n].reshape(-1) for n in small_names]), SUB * LANES).reshape(-1, LANES)

    dl, nm, nv = _adamw(pack(weights), pack(grads), pack(ms), pack(vs_), "adamw_small")
    off = 0
    for n in small_names:
        sz, shp = weights[n].size, weights[n].shape
        delta[n] = dl.reshape(-1)[off:off + sz].reshape(shp)
        new_m[n] = nm.reshape(-1)[off:off + sz].reshape(shp)
        new_v[n] = nv.reshape(-1)[off:off + sz].reshape(shp)
        off += sz

    return (loss, grad_x, *[grads[n] for n in names], *[delta[n] for n in names],
            *[new_m[n] for n in names], *[new_v[n] for n in names])
```

```python
import functools
import math

import jax
import jax.numpy as jnp
from jax import lax
from jax.experimental import pallas as pl
from jax.experimental.pallas import tpu as pltpu
from jax.experimental.pallas import tpu_sc as plsc

F32 = jnp.float32
MXU = jnp.bfloat16

D = 1024
DEPTH = 4
N_MOD = 6
AW = 512
A_GROUP = 64
H = 4
HD = 128
CK = 64
DFF = 2816
P_IN = 3592
P_PAD = 3712
EPS = 1e-6
N_DEV = 8
LANES = 128
SUB = 8
VMEM_LIMIT = 56 * 1024 * 1024

ADAM_LR, ADAM_B1, ADAM_B2, ADAM_EPS, ADAM_WD, ADAM_STEP = 0.001, 0.9, 0.999, 1e-08, 0.01, 10

NN = ((1,), (0,))
NT = ((1,), (1,))
TN = ((0,), (0,))
HI = lax.Precision.HIGHEST
MESH = pl.DeviceIdType.MESH


def _dot(a, b, dims, prec=None):
    if prec is None:
        a = a.astype(MXU) if a.dtype == F32 else a
        b = b.astype(MXU) if b.dtype == F32 else b
    return lax.dot_general(a, b, (dims, ((), ())), precision=prec, preferred_element_type=F32)


def _params(n_grid=0, limit=VMEM_LIMIT):
    sem = ("arbitrary",) * n_grid if n_grid else None
    return pltpu.CompilerParams(dimension_semantics=sem, vmem_limit_bytes=limit)


def _tile(n, want):
    if n <= want:
        return n
    t = want - want % SUB
    while n % t:
        t -= SUB
    assert t > 0, (n, want)
    return t


def _full(shape):
    nd = len(shape)
    return pl.BlockSpec(shape, lambda *_: (0,) * nd)


def _sig(x):
    return jax.nn.sigmoid(x)


def _rms(x):
    r = lax.rsqrt(jnp.mean(x * x, axis=-1, keepdims=True) + EPS)
    return x * r, r


def _rms_bwd(dn, n, r):
    return r * (dn - n * jnp.mean(dn * n, axis=-1, keepdims=True))


def _l2_bwd(dn, n, r):
    return r * (dn - n * jnp.sum(dn * n, axis=-1, keepdims=True))


def _sum0(x):
    return jnp.sum(x, axis=0, keepdims=True)


def _shift_down(x, s, halo):
    ext = jnp.concatenate([halo, x], axis=0)
    return pltpu.roll(ext, s, 0)[SUB:, :]


def _shift_up(x, s, halo):
    t = x.shape[0]
    ext = jnp.concatenate([x, halo], axis=0)
    return pltpu.roll(ext, t + SUB - s, 0)[:t, :]


def _conv_fwd(x, w_ref, width, halo):
    sh = [x] + [_shift_down(x, s, halo) for s in range(1, width)]
    out = w_ref[width - 1:width, :] * sh[0]
    for s in range(1, width):
        out = out + w_ref[width - 1 - s:width - s, :] * sh[s]
    return out, sh


def _conv_bwd_in(dout, w_ref, width, halo_next):
    dx = w_ref[width - 1:width, :] * dout
    for s in range(1, width):
        dx = dx + w_ref[width - 1 - s:width - s, :] * _shift_up(dout, s, halo_next)
    return dx


def _blockdiag_mean(n, group):
    r = lax.shift_right_logical(lax.broadcasted_iota(jnp.int32, (n, n), 0), int(math.log2(group)))
    c = lax.shift_right_logical(lax.broadcasted_iota(jnp.int32, (n, n), 1), int(math.log2(group)))
    return jnp.where(r == c, 1.0 / group, 0.0).astype(F32)


def _softplus(x):
    return jnp.maximum(x, 0.0) + jnp.log(1.0 + jnp.exp(-jnp.abs(x)))


def _my_place():
    return lax.axis_index("x"), lax.axis_index("y"), lax.axis_index("c")


def _all_gather(shards, name, in_vmem):
    nt = len(shards)

    def body(*refs):
        x_refs, out_refs = refs[:nt], refs[nt:2 * nt]
        send_sems, recv_sems, local_sems = refs[2 * nt:]
        x, y, c = _my_place()
        me, sibling = (x, y, c), (x, y, 1 - c)
        chips = [(1 - x, y), (x, 1 - y), (1 - x, 1 - y)]
        everything = []
        for t in range(nt):
            x_ref, out_ref = x_refs[t], out_refs[t]

            def blk(px, py, pc, out_ref=out_ref):
                return out_ref.at[4 * px + 2 * py + pc]

            def copy(k, block, to, src=None, t=t, blk=blk):
                return pltpu.make_async_remote_copy(
                    src_ref=blk(*block) if src is None else src, dst_ref=blk(*block),
                    send_sem=send_sems.at[7 * t + k], recv_sem=recv_sems.at[7 * t + k], device_id=to, device_id_type=MESH)

            mine = pltpu.make_async_copy(x_ref, blk(*me), local_sems.at[t])
            mine.start()
            first = [copy(0, me, sibling, src=x_ref)]
            first += [copy(1 + j, me, (*chip, c), src=x_ref) for j, chip in enumerate(chips)]
            for cp in first:
                cp.start()
            everything.append((copy, mine, first))
        sends = []
        for copy, mine, first in everything:
            passed = [copy(4 + j, (*chip, c), sibling) for j, chip in enumerate(chips)]
            for j, chip in enumerate(chips):
                copy(1 + j, (*chip, c), me).wait_recv()
                passed[j].start()
            sends += first + passed
        for copy, mine, first in everything:
            copy(0, sibling, me).wait_recv()
            for j, chip in enumerate(chips):
                copy(4 + j, (*chip, 1 - c), me).wait_recv()
        for cp in sends:
            cp.wait_send()
        for copy, mine, first in everything:
            mine.wait()

    space = pltpu.VMEM if in_vmem else pl.ANY
    return pl.pallas_call(
        body, name=name,
        out_shape=[jax.ShapeDtypeStruct((N_DEV,) + s.shape, s.dtype) for s in shards],
        in_specs=[pl.BlockSpec(memory_space=space)] * nt,
        out_specs=[pl.BlockSpec(memory_space=space)] * nt,
        scratch_shapes=[pltpu.SemaphoreType.DMA((7 * nt,)), pltpu.SemaphoreType.DMA((7 * nt,)),
                        pltpu.SemaphoreType.DMA((nt,))],
        compiler_params=pltpu.CompilerParams(vmem_limit_bytes=VMEM_LIMIT),
    )(*shards)


def _all_gather_async(shards, name, collective_id):
    nt = len(shards)
    hbm = pltpu.MemorySpace.HBM
    x_refs = [jax.new_ref(s, memory_space=hbm) for s in shards]
    out_refs = [jax.empty_ref(jax.ShapeDtypeStruct((N_DEV,) + s.shape, s.dtype), memory_space=hbm) for s in shards]

    @pl.kernel(mesh=plsc.ScalarSubcoreMesh(axis_name="sequencer", num_cores=1), name=name,
               scratch_types=(pltpu.SemaphoreType.DMA((7 * nt,)), pltpu.SemaphoreType.DMA((7 * nt,)),
                              pltpu.SemaphoreType.DMA((nt,))),
               compiler_params=pltpu.CompilerParams(collective_id=collective_id))
    def launch(send_sems, recv_sems, local_sems):
        x, y, c = _my_place()
        me, sibling = (x, y, c), (x, y, 1 - c)
        chips = [(1 - x, y), (x, 1 - y), (1 - x, 1 - y)]
        barrier = pltpu.get_barrier_semaphore()
        for peer in [sibling] + [(*chip, c) for chip in chips]:
            pl.semaphore_signal(barrier, inc=1, device_id=peer, device_id_type=MESH)
        pl.semaphore_wait(barrier, 4)
        everything = []
        for t in range(nt):
            x_ref, out_ref = x_refs[t], out_refs[t]

            def blk(px, py, pc, out_ref=out_ref):
                return out_ref.at[4 * px + 2 * py + pc]

            def copy(k, block, to, src=None, t=t, blk=blk):
                return pltpu.make_async_remote_copy(
                    src_ref=blk(*block) if src is None else src, dst_ref=blk(*block),
                    send_sem=send_sems.at[7 * t + k], recv_sem=recv_sems.at[7 * t + k], device_id=to, device_id_type=MESH)

            mine = pltpu.make_async_copy(x_ref, blk(*me), local_sems.at[t])
            mine.start()
            first = [copy(0, me, sibling, src=x_ref)]
            first += [copy(1 + j, me, (*chip, c), src=x_ref) for j, chip in enumerate(chips)]
            for cp in first:
                cp.start()
            everything.append((copy, mine, first))
        sends = []
        for copy, mine, first in everything:
            passed = [copy(4 + j, (*chip, c), sibling) for j, chip in enumerate(chips)]
            for j, chip in enumerate(chips):
                copy(1 + j, (*chip, c), me).wait_recv()
                passed[j].start()
            sends += first + passed
        for copy, mine, first in everything:
            copy(0, sibling, me).wait_recv()
            for j, chip in enumerate(chips):
                copy(4 + j, (*chip, 1 - c), me).wait_recv()
        for cp in sends:
            cp.wait_send()
        for copy, mine, first in everything:
            mine.wait()

    launch()
    return [r[...] for r in out_refs]


def _rs_exchange_async(srcs, name, collective_id):
    nt = len(srcs)
    hbm = pltpu.MemorySpace.HBM
    src_refs = [jax.new_ref(s, memory_space=hbm) for s in srcs]
    out_refs = [jax.empty_ref(jax.ShapeDtypeStruct(s.shape, s.dtype), memory_space=hbm) for s in srcs]
    flips = [(fx, fy, fc) for fx in (0, 1) for fy in (0, 1) for fc in (0, 1)][1:]

    @pl.kernel(mesh=plsc.ScalarSubcoreMesh(axis_name="sequencer", num_cores=1), name=name,
               scratch_types=(pltpu.SemaphoreType.DMA((7 * nt,)), pltpu.SemaphoreType.DMA((7 * nt,)),
                              pltpu.SemaphoreType.DMA((nt,))),
               compiler_params=pltpu.CompilerParams(collective_id=collective_id))
    def launch(send_sems, recv_sems, local_sems):
        x, y, c = _my_place()
        me = 4 * x + 2 * y + c
        peers = [(1 - x if fx else x, 1 - y if fy else y, 1 - c if fc else c) for fx, fy, fc in flips]
        barrier = pltpu.get_barrier_semaphore()
        for peer in peers:
            pl.semaphore_signal(barrier, inc=1, device_id=peer, device_id_type=MESH)
        pl.semaphore_wait(barrier, len(peers))
        own = [pltpu.make_async_copy(src_refs[t].at[me], out_refs[t].at[me], local_sems.at[t]) for t in range(nt)]
        copies = [pltpu.make_async_remote_copy(
            src_ref=src_refs[t].at[4 * px + 2 * py + pc], dst_ref=out_refs[t].at[me],
            send_sem=send_sems.at[7 * t + f], recv_sem=recv_sems.at[7 * t + f],
            device_id=(px, py, pc), device_id_type=MESH) for t in range(nt) for f, (px, py, pc) in enumerate(peers)]
        for cp in own + copies:
            cp.start()
        for cp in copies + own:
            cp.wait()

    launch()
    return [r[...] for r in out_refs]


def _rs_sum(recv, name):
    _, r, n = recv.shape
    tr = _tile(r, 512)

    def body(r_ref, o_ref):
        s = r_ref[0].astype(F32)
        for k in range(1, N_DEV):
            s = s + r_ref[k].astype(F32)
        o_ref[...] = s

    return pl.pallas_call(
        body, name=name, grid=(r // tr,),
        in_specs=[pl.BlockSpec((N_DEV, tr, n), lambda i: (0, i, 0))],
        out_specs=pl.BlockSpec((tr, n), lambda i: (i, 0)),
        out_shape=jax.ShapeDtypeStruct((r, n), F32), compiler_params=_params(1),
    )(recv)


def _shard_windows(n_shard, count, first=0):
    out = []
    for k in range(first, first + count):
        off = n_shard * k
        a, s = off // LANES, off % LANES
        out.append((a, s, -(-(s + n_shard) // LANES) * LANES))
    return out


def _fit_lanes(x, width):
    have = x.shape[1]
    if have < width:
        return jnp.concatenate([x, jnp.zeros((x.shape[0], width - have), x.dtype)], axis=-1)
    return x[:, :width]


def _interleave_cols(g, n_shard, w_out, name):
    nd, nl, rows, wpad = g.shape
    rb = _tile(rows, 256)
    wins = _shard_windows(n_shard, nd)

    def body(g_ref, o_ref, acc):
        acc[...] = jnp.zeros_like(acc)
        for k, (a, s, win) in enumerate(wins):
            xk = _fit_lanes(g_ref[k].astype(F32), win)
            if s:
                xk = pltpu.roll(xk, s, 1)
            acc[:, a * LANES:a * LANES + win] += xk
        o_ref[...] = acc[...].astype(o_ref.dtype)

    return pl.pallas_call(
        body, name=name, grid=(nl, rows // rb),
        in_specs=[pl.BlockSpec((nd, None, rb, wpad), lambda l, i: (0, l, i, 0))],
        out_specs=pl.BlockSpec((None, rb, w_out), lambda l, i: (l, i, 0)),
        out_shape=jax.ShapeDtypeStruct((nl, rows, w_out), g.dtype),
        scratch_shapes=[pltpu.VMEM((rb, w_out), F32)],
        compiler_params=_params(2),
    )(g)


def _sum_devices(g):
    _, r, n = g.shape

    def body(g_ref, o_ref):
        s = g_ref[0]
        for t in range(1, N_DEV):
            s = s + g_ref[t]
        o_ref[...] = s

    return pl.pallas_call(
        body, name="sum_devices", out_shape=jax.ShapeDtypeStruct((r, n), F32),
        in_specs=[pl.BlockSpec(memory_space=pltpu.VMEM)], out_specs=pl.BlockSpec(memory_space=pltpu.VMEM),
        compiler_params=pltpu.CompilerParams(vmem_limit_bytes=VMEM_LIMIT),
    )(g)


def _mod_fwd(c_all, ada_w, ada_b_cols):
    nl, _, nc = ada_w.shape

    def body(c_ref, w_ref, b_ref, o_ref):
        cv = c_ref[...]
        act = (cv * _sig(cv)).astype(MXU)
        o_ref[...] = _dot(act, w_ref[...].astype(MXU), NN) + b_ref[...]

    return pl.pallas_call(
        body, name="mod_fwd", grid=(nl,),
        in_specs=[_full((16, D)), pl.BlockSpec((None, D, nc), lambda i: (i, 0, 0)),
                  pl.BlockSpec((None, 1, nc), lambda i: (i, 0, 0))],
        out_specs=pl.BlockSpec((None, 16, nc), lambda i: (i, 0, 0)),
        out_shape=jax.ShapeDtypeStruct((nl, 16, nc), F32), compiler_params=_params(1),
    )(c_all, ada_w, ada_b_cols)


def _mod_bwd(c_all, dmod_cols):
    nl, _, nc = dmod_cols.shape

    def body(c_ref, d_ref, o_ref):
        cv = c_ref[...]
        act = (cv * _sig(cv)).astype(MXU)
        o_ref[...] = _dot(act, d_ref[...].astype(MXU), TN)

    return pl.pallas_call(
        body, name="mod_bwd", grid=(nl,),
        in_specs=[_full((16, D)), pl.BlockSpec((None, 16, nc), lambda i: (i, 0, 0))],
        out_specs=pl.BlockSpec((None, D, nc), lambda i: (i, 0, 0)),
        out_shape=jax.ShapeDtypeStruct((nl, D, nc), F32), compiler_params=_params(1),
    )(c_all, dmod_cols)


def _in_proj(x, modrows, vec, w_in):
    L = x.shape[0]
    T = _tile(L, 256)

    def body(x_ref, mod_ref, vec_ref, w_ref, p_ref, h_ref):
        n, _ = _rms(x_ref[...])
        h = n * vec_ref[0:1, :] * (1.0 + mod_ref[1:2, :]) + mod_ref[0:1, :]
        hb = h.astype(MXU)
        h_ref[...] = hb
        p_ref[...] = _dot(hb, w_ref[...], NN)

    return pl.pallas_call(
        body, name="in_proj", grid=(L // T,),
        in_specs=[pl.BlockSpec((T, D), lambda i: (i, 0)), _full((SUB, D)), _full((SUB, D)), _full((D, P_PAD))],
        out_specs=[pl.BlockSpec((T, P_PAD), lambda i: (i, 0)), pl.BlockSpec((T, D), lambda i: (i, 0))],
        out_shape=[jax.ShapeDtypeStruct((L, P_PAD), F32), jax.ShapeDtypeStruct((L, D), MXU)],
        compiler_params=_params(1),
    )(x, modrows, vec, w_in)


def _gate_small(s, sp_ref):
    lane = lax.broadcasted_iota(jnp.int32, s.shape, 1)
    a = -jnp.exp(sp_ref[0:1, :])
    xb = s + sp_ref[1:2, :]
    beta = _sig(s)
    g = a * _softplus(xb)
    return lane, a, xb, beta, g


def _pre_fwd(p, pa, cq, sp):
    L = p.shape[0]
    T = _tile(L, 256)
    scale = HD ** -0.5

    def body(pm_ref, ps_ref, pa_ref, cq_ref, sp_ref, qn_ref, kn_ref, vs_ref, gb_ref, ya_ref, cu_ref, qc_ref,
             u_carry, q_carry):
        @pl.when(pl.program_id(0) == 0)
        def _():
            u_carry[...] = jnp.zeros_like(u_carry)
            q_carry[...] = jnp.zeros_like(q_carry)

        a_b = pm_ref[:, 0:AW]
        u = pm_ref[:, AW:2 * AW] * pm_ref[:, 2 * AW:3 * AW]
        cu, _ = _conv_fwd(u, pa_ref, 3, u_carry[...])
        cu_ref[...] = cu.astype(MXU)
        u_carry[...] = u[T - SUB:T, :]
        yp = a_b * cu
        ms = _dot_f32(yp * yp, _blockdiag_mean(AW, A_GROUP), NN, exact="b")
        ya_ref[...] = (yp * lax.rsqrt(ms + EPS) * pa_ref[3:4, :]).astype(MXU)

        qkv = pm_ref[:, 3 * AW:3 * AW + 3 * H * HD]
        qc, _ = _conv_fwd(qkv, cq_ref, 4, q_carry[...])
        qc_ref[...] = qc.astype(MXU)
        q_carry[...] = qkv[T - SUB:T, :]
        qs = qc * _sig(qc)
        for h in range(H):
            q = qs[:, h * HD:(h + 1) * HD]
            qn_ref[:, h * HD:(h + 1) * HD] = q * (lax.rsqrt(jnp.sum(q * q, axis=-1, keepdims=True) + EPS) * scale)
            k = qs[:, (H + h) * HD:(H + h + 1) * HD]
            kn_ref[:, h * HD:(h + 1) * HD] = k * lax.rsqrt(jnp.sum(k * k, axis=-1, keepdims=True) + EPS)
        vs_ref[...] = qs[:, 2 * H * HD:3 * H * HD]

        lane, _, _, beta, g = _gate_small(ps_ref[...], sp_ref)
        gb_ref[...] = jnp.where(lane < H, beta, jnp.where(lane < 2 * H, g, 0.0))

    w3 = 3 * AW + 3 * H * HD
    row = lambda i: (i, 0)
    return pl.pallas_call(
        body, name="pre_fwd", grid=(L // T,),
        in_specs=[pl.BlockSpec((T, w3), row), pl.BlockSpec((T, LANES), lambda i: (i, (P_PAD - LANES) // LANES)),
                  _full((SUB, AW)), _full((SUB, 3 * H * HD)), _full((SUB, LANES))],
        out_specs=[pl.BlockSpec((T, H * HD), row)] * 3 + [pl.BlockSpec((T, LANES), row), pl.BlockSpec((T, AW), row),
                                                          pl.BlockSpec((T, AW), row), pl.BlockSpec((T, 3 * H * HD), row)],
        out_shape=[jax.ShapeDtypeStruct((L, H * HD), F32)] * 3
        + [jax.ShapeDtypeStruct((L, LANES), F32), jax.ShapeDtypeStruct((L, AW), MXU),
           jax.ShapeDtypeStruct((L, AW), MXU), jax.ShapeDtypeStruct((L, 3 * H * HD), MXU)],
        scratch_shapes=[pltpu.VMEM((SUB, AW), F32), pltpu.VMEM((SUB, 3 * H * HD), F32)],
        compiler_params=_params(1),
    )(p, p, pa, cq, sp)


def _gdr_masks():
    r = lax.broadcasted_iota(jnp.int32, (CK, CK), 0)
    c = lax.broadcasted_iota(jnp.int32, (CK, CK), 1)
    return r >= c, r > c


def _head_cols(gbt, h):
    return gbt[:, h:h + 1], gbt[:, H + h:H + h + 1]


def _split(x, parts):
    out = []
    for _ in range(parts):
        hi = x.astype(jnp.bfloat16)
        out.append(hi)
        x = x - hi.astype(F32)
    return out


def _dot_f32(a, b, dims, exact=None):
    if exact == "a":
        ab = a.astype(jnp.bfloat16)
        return sum(_dot(ab, t, dims) for t in _split(b, 3))
    if exact == "b":
        bb = b.astype(jnp.bfloat16)
        return sum(_dot(t, bb, dims) for t in _split(a, 3))
    ah, al = _split(a, 2)
    bh, bl = _split(b, 2)
    return _dot(ah, bh, dims) + _dot(ah, bl, dims) + _dot(al, bh, dims)


def _gdr_consts():
    causal, strict = _gdr_masks()
    return dict(causal=causal, strict=strict, tril=jnp.where(causal, 1.0, 0.0).astype(F32),
                eye=jnp.where(causal & jnp.logical_not(strict), 1.0, 0.0).astype(F32),
                bcast=jnp.full((CK, HD), 1.0 / HD, F32))


def _dots(a, b, dims):
    return [_dot(x, y, dims) for x, y in zip(a, b)]


def _dots_f32(a, b, dims, exact=None):
    n = len(a)
    if exact == "a":
        lhs = [[x.astype(jnp.bfloat16)] * 3 for x in a]
        rhs = [_split(y, 3) for y in b]
    elif exact == "b":
        lhs = [_split(x, 3) for x in a]
        rhs = [[y.astype(jnp.bfloat16)] * 3 for y in b]
    else:
        sa = [_split(x, 2) for x in a]
        sb = [_split(y, 2) for y in b]
        lhs = [[s[0], s[0], s[1]] for s in sa]
        rhs = [[s[0], s[1], s[0]] for s in sb]
    terms = [[_dot(lhs[i][t], rhs[i][t], dims) for i in range(n)] for t in range(3)]
    return [terms[0][i] + terms[1][i] + terms[2][i] for i in range(n)]


def _gdr_local(q, k, v, beta, g, cst, tinv=None):
    n = len(q)
    R = range(n)
    causal, strict = cst["causal"], cst["strict"]
    gc = _dots_f32([cst["tril"]] * n, [jnp.broadcast_to(g[i], (CK, HD)) for i in R], NN, exact="a")
    g_row = _dots_f32([cst["bcast"]] * n, gc, NT, exact="a")
    decay = [jnp.where(causal, jnp.exp(jnp.where(causal, gc[i][:, 0:CK] - g_row[i], 0.0)), 0.0) for i in R]
    eg = [jnp.exp(gc[i]) for i in R]
    gl = [gc[i][CK - 1:CK, :] for i in R]
    ek = [jnp.exp(gl[i] - gc[i]) for i in R]
    cd = [jnp.exp(gl[i]) for i in R]
    kb = [k[i] * beta[i] for i in R]
    pk = _dots(kb, k, NT)
    if tinv is None:
        xp = [-jnp.where(strict, pk[i] * decay[i], 0.0) for i in R]
        tinv = [cst["eye"] + xp[i] for i in R]
        for _ in range(5):
            xp = _dots_f32(xp, xp, NN)
            tx = _dots_f32(tinv, xp, NN)
            tinv = [tinv[i] + tx[i] for i in R]
    u = _dots(tinv, [v[i] * beta[i] for i in R], NN)
    w = _dots(tinv, [kb[i] * eg[i] for i in R], NN)
    qk = _dots(q, k, NT)
    intra = [jnp.where(causal, qk[i] * decay[i], 0.0) for i in R]
    return dict(decay=decay, eg=eg, ek=ek, cd=cd, kb=kb, pk=pk, tinv=tinv, u=u, w=w, qk=qk, intra=intra,
                q_dec=[q[i] * eg[i] for i in R], k_dec=[k[i] * ek[i] for i in R])


GDR_SUB = 4


def _gdr_fwd(qn, kn, vs, gb):
    L = qn.shape[0]
    nc = L // CK
    cb = min(8, nc)
    rb = cb * CK
    nb = nc // cb
    nsub = GDR_SUB if cb % GDR_SUB == 0 else 1

    def body(q_ref, k_ref, v_ref, gb_ref, o_ref, st_ref, ti_ref, s_ref):
        @pl.when(pl.program_id(0) == 0)
        def _():
            s_ref[...] = jnp.zeros_like(s_ref)

        cst = _gdr_consts()
        heads = range(H)

        def group(gi, carry):
            rows = [pl.ds(pl.multiple_of((gi * nsub + j) * CK, CK), CK) for j in range(nsub)]
            chains = [(j, h) for j in range(nsub) for h in heads]
            gbt = [gb_ref[rows[j], :] for j in range(nsub)]
            cols = lambda h: slice(h * HD, (h + 1) * HD)
            t = _gdr_local([q_ref[rows[j], cols(h)] for j, h in chains], [k_ref[rows[j], cols(h)] for j, h in chains],
                           [v_ref[rows[j], cols(h)] for j, h in chains],
                           [_head_cols(gbt[j], h)[0] for j, h in chains], [_head_cols(gbt[j], h)[1] for j, h in chains], cst)
            s = [s_ref[h] for h in heads]
            for j in range(nsub):
                at = lambda key: [t[key][j * H + h] for h in heads]
                for h in heads:
                    st_ref[h, gi * nsub + j] = s[h]
                    ti_ref[h, gi * nsub + j] = t["tinv"][j * H + h]
                ws = _dots(at("w"), s, NN)
                v_new = [u_h - ws_h for u_h, ws_h in zip(at("u"), ws)]
                o_s = _dots(at("q_dec"), s, NN)
                o_v = _dots(at("intra"), v_new, NN)
                kv = _dots(at("k_dec"), v_new, TN)
                cd = at("cd")
                for h in heads:
                    o_ref[rows[j], cols(h)] = o_s[h] + o_v[h]
                s = [s[h] * cd[h] + kv[h] for h in heads]
            for h in heads:
                s_ref[h] = s[h]
            return carry

        lax.fori_loop(0, cb // nsub, group, 0)

    blk = pl.BlockSpec((rb, H * HD), lambda b: (b, 0))
    return pl.pallas_call(
        body, name="gdr_fwd", grid=(nb,),
        in_specs=[blk, blk, blk, pl.BlockSpec((rb, LANES), lambda b: (b, 0))],
        out_specs=[blk, pl.BlockSpec((H, cb, HD, HD), lambda b: (0, b, 0, 0)),
                   pl.BlockSpec((H, cb, CK, CK), lambda b: (0, b, 0, 0))],
        out_shape=[jax.ShapeDtypeStruct((L, H * HD), F32), jax.ShapeDtypeStruct((H, nc, HD, HD), F32),
                   jax.ShapeDtypeStruct((H, nc, CK, CK), F32)],
        scratch_shapes=[pltpu.VMEM((H, HD, HD), F32)],
        compiler_params=_params(1),
    )(qn, kn, vs, gb)


def _gdr_bwd(qn, kn, vs, gb, states, tinvs, do):
    L = qn.shape[0]
    nc = L // CK
    cb = min(8, nc)
    rb = cb * CK
    nb = nc // cb
    nsub = GDR_SUB if cb % GDR_SUB == 0 else 1

    def body(q_ref, k_ref, v_ref, gb_ref, st_ref, ti_ref, do_ref, dq_ref, dk_ref, dv_ref, dgb_ref, ds_ref):
        @pl.when(pl.program_id(0) == 0)
        def _():
            ds_ref[...] = jnp.zeros_like(ds_ref)

        cst = _gdr_consts()
        causal, strict = cst["causal"], cst["strict"]
        ones = jnp.ones((CK, HD), F32)
        row = lax.broadcasted_iota(jnp.int32, (CK, HD), 0)
        lane = lax.broadcasted_iota(jnp.int32, (CK, LANES), 1)

        heads = range(H)
        rsum = lambda x: jnp.sum(x, axis=-1, keepdims=True)

        def group(gj, carry):
            gi = cb // nsub - 1 - gj
            rows = [pl.ds(pl.multiple_of((gi * nsub + j) * CK, CK), CK) for j in range(nsub)]
            chains = [(j, h) for j in range(nsub) for h in heads]
            gbt = [gb_ref[rows[j], :] for j in range(nsub)]
            cols = lambda h: slice(h * HD, (h + 1) * HD)
            q_all = [q_ref[rows[j], cols(h)] for j, h in chains]
            k_all = [k_ref[rows[j], cols(h)] for j, h in chains]
            v_all = [v_ref[rows[j], cols(h)] for j, h in chains]
            beta_all = [_head_cols(gbt[j], h)[0] for j, h in chains]
            t = _gdr_local(q_all, k_all, v_all, beta_all, [_head_cols(gbt[j], h)[1] for j, h in chains], cst,
                           tinv=[ti_ref[h, gi * nsub + j] for j, h in chains])
            ds_out = [ds_ref[h] for h in heads]
            for j in reversed(range(nsub)):
                at = lambda key: [t[key][j * H + h] for h in heads]
                pick = lambda lst: [lst[j * H + h] for h in heads]
                q, k, v, beta = pick(q_all), pick(k_all), pick(v_all), pick(beta_all)
                u, w, tinv, decay = at("u"), at("w"), at("tinv"), at("decay")
                eg, ek, cd, kb = at("eg"), at("ek"), at("cd"), at("kb")
                q_dec, k_dec, intra, pk, qk = at("q_dec"), at("k_dec"), at("intra"), at("pk"), at("qk")
                s = [st_ref[h, gi * nsub + j] for h in heads]
                dout = [do_ref[rows[j], cols(h)] for h in heads]

                ws = _dots(w, s, NN)
                v_new = [u[h] - ws[h] for h in heads]
                dq_dec = _dots(dout, s, NT)
                qd = _dots(q_dec, dout, TN)
                di = _dots(dout, v_new, NT)
                dintra = [jnp.where(causal, di[h], 0.0) for h in heads]
                ido = _dots(intra, dout, TN)
                kds = _dots(k_dec, ds_out, NN)
                dv_new = [ido[h] + kds[h] for h in heads]
                dk_dec = _dots(v_new, ds_out, NT)
                dcd = [jnp.sum(jnp.sum(ds_out[h] * s[h], axis=1, keepdims=True), axis=0, keepdims=True) for h in heads]
                dvs = _dots(dv_new, s, NT)
                dw = [-dvs[h] for h in heads]
                wdv = _dots(w, dv_new, TN)
                ds_new = [qd[h] + ds_out[h] * cd[h] - wdv[h] for h in heads]
                dru = _dots(tinv, dv_new, TN)
                drw = _dots(tinv, dw, TN)
                dl1 = _dots(dru, u, NT)
                dl2 = _dots(drw, w, NT)
                dlower = [-jnp.where(strict, dl1[h] + dl2[h], 0.0) for h in heads]
                dv = [dru[h] * beta[h] for h in heads]
                dbeta = [rsum(dru[h] * v[h]) for h in heads]
                dgc = [rsum(drw[h] * kb[h]) * eg[h] for h in heads]
                dpk = [dlower[h] * decay[h] for h in heads]
                dqk = [dintra[h] * decay[h] for h in heads]
                dpk_k = _dots(dpk, k, NN)
                dkb = [drw[h] * eg[h] + dpk_k[h] for h in heads]
                dk1 = _dots(dpk, kb, TN)
                dq1 = _dots(dqk, k, NN)
                dk2 = _dots(dqk, q, TN)
                m = [(dlower[h] * pk[h] + dintra[h] * qk[h]) * decay[h] for h in heads]
                mcol = _dots_f32(m, [ones] * H, TN, exact="b")
                e = [rsum(dk_dec[h] * k_dec[h]) for h in heads]
                dgl = [jnp.sum(e[h], axis=0, keepdims=True) + dcd[h] * cd[h] for h in heads]
                dgc = [dgc[h] + rsum(m[h]) - mcol[h] + rsum(dq_dec[h] * q_dec[h]) - e[h]
                       + jnp.where(row == CK - 1, dgl[h], 0.0) for h in heads]
                dg = _dots_f32([cst["tril"]] * H, dgc, TN, exact="a")
                dgb = jnp.zeros((CK, LANES), F32)
                for h in heads:
                    dq_ref[rows[j], cols(h)] = dq1[h] + dq_dec[h] * eg[h]
                    dk_ref[rows[j], cols(h)] = dk1[h] + dk2[h] + dk_dec[h] * ek[h] + dkb[h] * beta[h]
                    dv_ref[rows[j], cols(h)] = dv[h]
                    db = dbeta[h] + rsum(dkb[h] * k[h])
                    dgb = dgb + jnp.where(lane == h, db, 0.0) + jnp.where(lane == H + h, dg[h], 0.0)
                dgb_ref[rows[j], :] = dgb
                ds_out = ds_new
            for h in heads:
                ds_ref[h] = ds_out[h]
            return carry

        lax.fori_loop(0, cb // nsub, group, 0)

    blk = pl.BlockSpec((rb, H * HD), lambda b: (nb - 1 - b, 0))
    sblk = pl.BlockSpec((rb, LANES), lambda b: (nb - 1 - b, 0))
    return pl.pallas_call(
        body, name="gdr_bwd", grid=(nb,),
        in_specs=[blk, blk, blk, sblk, pl.BlockSpec((H, cb, HD, HD), lambda b: (0, nb - 1 - b, 0, 0)),
                  pl.BlockSpec((H, cb, CK, CK), lambda b: (0, nb - 1 - b, 0, 0)), blk],
        out_specs=[blk, blk, blk, sblk],
        out_shape=[jax.ShapeDtypeStruct((L, H * HD), F32)] * 3 + [jax.ShapeDtypeStruct((L, LANES), F32)],
        scratch_shapes=[pltpu.VMEM((H, HD, HD), F32)],
        compiler_params=_params(1),
    )(qn, kn, vs, gb, states, tinvs, do)


def _post_fwd(o, p, ya, x, modrows, sp, w_out):
    L = x.shape[0]
    T = _tile(L, 256)

    def body(o_ref, z_ref, ya_ref, x_ref, mod_ref, sp_ref, w_ref, y_ref, x2_ref, yb_ref):
        ndw = sp_ref[2:3, :]
        z = z_ref[...]
        sz = z * _sig(z)
        parts = []
        for h in range(H):
            n, _ = _rms(o_ref[:, h * HD:(h + 1) * HD])
            parts.append(n * ndw * sz[:, h * HD:(h + 1) * HD])
        yb = jnp.concatenate(parts, axis=-1).astype(MXU)
        yb_ref[...] = yb
        y = _dot(ya_ref[...], w_ref[0:AW, :], NN) + _dot(yb, w_ref[AW:2 * AW, :], NN)
        y_ref[...] = y
        x2_ref[...] = x_ref[...] + mod_ref[2:3, :] * y

    row = lambda i: (i, 0)
    zcol = (3 * AW + 3 * H * HD) // (H * HD)
    return pl.pallas_call(
        body, name="post_fwd", grid=(L // T,),
        in_specs=[pl.BlockSpec((T, H * HD), row), pl.BlockSpec((T, H * HD), lambda i: (i, zcol)),
                  pl.BlockSpec((T, AW), row), pl.BlockSpec((T, D), row), _full((SUB, D)), _full((SUB, LANES)),
                  _full((D, D))],
        out_specs=[pl.BlockSpec((T, D), row), pl.BlockSpec((T, D), row), pl.BlockSpec((T, H * HD), row)],
        out_shape=[jax.ShapeDtypeStruct((L, D), F32), jax.ShapeDtypeStruct((L, D), F32),
                   jax.ShapeDtypeStruct((L, H * HD), MXU)],
        compiler_params=_params(1),
    )(o, p, ya, x, modrows, sp, w_out)


FF_COLS = 2
FF_CW = DFF // FF_COLS
FF_ROWS = 256


def _ffn_fwd_half(x2, modrows, vec, w_up, cff, w_down, j, d_prev):
    assert FF_COLS == 2
    L = x2.shape[0]
    T = _tile(L, FF_ROWS)
    nj = FF_COLS
    last = d_prev is not None

    def body(*refs):
        x_ref, mod_ref, vec_ref, wg_ref, wu_ref, cg_ref, cu_ref, wd_ref = refs[:8]
        if last:
            dp_ref, gp_ref, up_ref, gc_ref, uc_ref, f_ref, d_ref, x3_ref, carry_g, carry_u = refs[8:]
        else:
            h_ref, gp_ref, up_ref, gc_ref, uc_ref, f_ref, d_ref, carry_g, carry_u = refs[8:]

        @pl.when(pl.program_id(0) == 0)
        def _():
            carry_g[...] = jnp.zeros_like(carry_g)
            carry_u[...] = jnp.zeros_like(carry_u)

        xv = x_ref[...]
        n, _ = _rms(xv)
        hb = (n * vec_ref[1:2, :] * (1.0 + mod_ref[4:5, :]) + mod_ref[3:4, :]).astype(MXU)
        if not last:
            h_ref[...] = hb
        g = _dot(hb, wg_ref[...], NN)
        u = _dot(hb, wu_ref[...], NN)
        gp_ref[...] = g
        up_ref[...] = u
        gc, _ = _conv_fwd(g, cg_ref, 3, carry_g[...])
        uc, _ = _conv_fwd(u, cu_ref, 3, carry_u[...])
        carry_g[...] = g[T - SUB:T, :]
        carry_u[...] = u[T - SUB:T, :]
        gc_ref[...] = gc.astype(MXU)
        uc_ref[...] = uc.astype(MXU)
        fb = (gc * _sig(gc) * uc).astype(MXU)
        f_ref[...] = fb
        part = _dot(fb, wd_ref[...], NN)
        if last:
            dv = dp_ref[...] + part
            d_ref[...] = dv
            x3_ref[...] = xv + mod_ref[5:6, :] * dv
        else:
            d_ref[...] = part

    row = lambda i: (i, 0)
    rowD = pl.BlockSpec((T, D), row)
    rowC = pl.BlockSpec((T, FF_CW), row)
    in_specs = [rowD, _full((SUB, D)), _full((SUB, D)),
                pl.BlockSpec((D, FF_CW), lambda i: (0, j)), pl.BlockSpec((D, FF_CW), lambda i: (0, nj + j)),
                pl.BlockSpec((SUB, FF_CW), lambda i: (0, j)), pl.BlockSpec((SUB, FF_CW), lambda i: (0, nj + j)),
                pl.BlockSpec((FF_CW, D), lambda i: (j, 0))]
    half = [jax.ShapeDtypeStruct((L, FF_CW), F32), jax.ShapeDtypeStruct((L, FF_CW), F32),
            jax.ShapeDtypeStruct((L, FF_CW), MXU), jax.ShapeDtypeStruct((L, FF_CW), MXU),
            jax.ShapeDtypeStruct((L, FF_CW), MXU)]
    args = [x2, modrows, vec, w_up, w_up, cff, cff, w_down]
    if last:
        in_specs.append(rowD)
        args.append(d_prev)
        out_specs = [rowC] * 5 + [rowD, rowD]
        out_shape = half + [jax.ShapeDtypeStruct((L, D), F32), jax.ShapeDtypeStruct((L, D), F32)]
    else:
        out_specs = [rowD] + [rowC] * 5 + [rowD]
        out_shape = [jax.ShapeDtypeStruct((L, D), MXU)] + half + [jax.ShapeDtypeStruct((L, D), F32)]
    return pl.pallas_call(
        body, name="ffn_fwd_last" if last else "ffn_fwd_first", grid=(L // T,),
        in_specs=in_specs, out_specs=out_specs, out_shape=out_shape,
        scratch_shapes=[pltpu.VMEM((SUB, FF_CW), F32), pltpu.VMEM((SUB, FF_CW), F32)],
        compiler_params=_params(1),
    )(*args)


def _ffn_bwd_half(dx3, modrows, gpre, upre, gcv, ucv, cff, w_down, w_up, j, tail):
    assert FF_COLS == 2
    L = dx3.shape[0]
    T = _tile(L, FF_ROWS)
    ni, nj = L // T, FF_COLS
    last = tail is not None

    def body(*refs):
        dx3_ref, mod_ref, gp_ref, up_ref, gc_ref, uc_ref, cg_ref, cu_ref, wd_ref, wg_ref, wu_ref = refs[:11]
        if last:
            (d_ref, x2_ref, vec_ref, dhp_ref, dgp_ref, dup_ref, dx2_ref, accv_ref, dcg_ref, dcu_ref,
             carry_g, carry_u) = refs[11:]
        else:
            dd_ref, dgp_ref, dup_ref, dh_ref, dcg_ref, dcu_ref, carry_g, carry_u = refs[11:]
        i = pl.program_id(0)

        @pl.when(i == 0)
        def _():
            carry_g[...] = jnp.zeros_like(carry_g)
            carry_u[...] = jnp.zeros_like(carry_u)
            dcg_ref[...] = jnp.zeros_like(dcg_ref)
            dcu_ref[...] = jnp.zeros_like(dcu_ref)
            if last:
                accv_ref[...] = jnp.zeros_like(accv_ref)

        dx3v = dx3_ref[...]
        ddb = (mod_ref[5:6, :] * dx3v).astype(MXU)
        if not last:
            dd_ref[...] = ddb
        g, u = gp_ref[...], up_ref[...]
        gc, uc = gc_ref[...].astype(F32), uc_ref[...].astype(F32)
        sg = _sig(gc)
        df = _dot(ddb, wd_ref[...], NT)
        duc = df * (gc * sg)
        dgc = df * uc * (sg * (1.0 + gc * (1.0 - sg)))
        dgs = [dgc] + [_shift_up(dgc, s, carry_g[...]) for s in (1, 2)]
        dus = [duc] + [_shift_up(duc, s, carry_u[...]) for s in (1, 2)]
        for s in range(3):
            dcg_ref[2 - s:3 - s, :] += _sum0(dgs[s] * g)
            dcu_ref[2 - s:3 - s, :] += _sum0(dus[s] * u)
        dg = (cg_ref[2:3, :] * dgs[0] + cg_ref[1:2, :] * dgs[1] + cg_ref[0:1, :] * dgs[2]).astype(MXU)
        du = (cu_ref[2:3, :] * dus[0] + cu_ref[1:2, :] * dus[1] + cu_ref[0:1, :] * dus[2]).astype(MXU)
        carry_g[...] = dgc[0:SUB, :]
        carry_u[...] = duc[0:SUB, :]
        dgp_ref[...] = dg
        dup_ref[...] = du
        dh = _dot(dg, wg_ref[...], NT) + _dot(du, wu_ref[...], NT)
        if last:
            dh = dh + dhp_ref[...]
            accv_ref[0:1, :] += _sum0(dx3v * d_ref[...])
            n, r = _rms(x2_ref[...])
            nw, sc = vec_ref[1:2, :], mod_ref[4:5, :]
            accv_ref[1:2, :] += _sum0(dh)
            accv_ref[2:3, :] += _sum0(dh * n * nw)
            accv_ref[3:4, :] += _sum0(dh * n * (1.0 + sc))
            dx2_ref[...] = _rms_bwd(dh * nw * (1.0 + sc), n, r) + dx3v
        else:
            dh_ref[...] = dh

    row = lambda i: (ni - 1 - i, 0)
    rowD = pl.BlockSpec((T, D), row)
    rowC = pl.BlockSpec((T, FF_CW), row)
    in_specs = [rowD, _full((SUB, D)), rowC, rowC, rowC, rowC,
                pl.BlockSpec((SUB, FF_CW), lambda i: (0, j)), pl.BlockSpec((SUB, FF_CW), lambda i: (0, nj + j)),
                pl.BlockSpec((FF_CW, D), lambda i: (j, 0)),
                pl.BlockSpec((D, FF_CW), lambda i: (0, j)), pl.BlockSpec((D, FF_CW), lambda i: (0, nj + j))]
    args = [dx3, modrows, gpre, upre, gcv, ucv, cff, cff, w_down, w_up, w_up]
    halfb = [jax.ShapeDtypeStruct((L, FF_CW), MXU), jax.ShapeDtypeStruct((L, FF_CW), MXU)]
    dconv = [jax.ShapeDtypeStruct((SUB, FF_CW), F32)] * 2
    if last:
        d, x2, vec, dh_prev = tail
        in_specs += [rowD, rowD, _full((SUB, D)), rowD]
        args += [d, x2, vec, dh_prev]
        out_specs = [rowC, rowC, rowD, _full((SUB, D)), _full((SUB, FF_CW)), _full((SUB, FF_CW))]
        out_shape = halfb + [jax.ShapeDtypeStruct((L, D), F32), jax.ShapeDtypeStruct((SUB, D), F32)] + dconv
    else:
        out_specs = [rowD, rowC, rowC, rowD, _full((SUB, FF_CW)), _full((SUB, FF_CW))]
        out_shape = [jax.ShapeDtypeStruct((L, D), MXU)] + halfb + [jax.ShapeDtypeStruct((L, D), F32)] + dconv
    return pl.pallas_call(
        body, name="ffn_bwd_last" if last else "ffn_bwd_first", grid=(ni,),
        in_specs=in_specs, out_specs=out_specs, out_shape=out_shape,
        scratch_shapes=[pltpu.VMEM((SUB, FF_CW), F32), pltpu.VMEM((SUB, FF_CW), F32)],
        compiler_params=_params(1),
    )(*args)


def _final(x, target, nf):
    L = x.shape[0]
    T = _tile(L, 256)

    def body(x_ref, t_ref, nf_ref, dx_ref, acc_ref):
        @pl.when(pl.program_id(0) == 0)
        def _():
            acc_ref[...] = jnp.zeros_like(acc_ref)

        n, r = _rms(x_ref[...])
        w = nf_ref[0:1, :]
        err = n * w - t_ref[...]
        acc_ref[0:1, :] += (0.5 / D) * _sum0(err * err)
        dy = err * (1.0 / D)
        acc_ref[1:2, :] += _sum0(dy * n)
        dx_ref[...] = _rms_bwd(dy * w, n, r)

    row = lambda i: (i, 0)
    return pl.pallas_call(
        body, name="final_norm_loss", grid=(L // T,),
        in_specs=[pl.BlockSpec((T, D), row), pl.BlockSpec((T, D), row), _full((SUB, D))],
        out_specs=[pl.BlockSpec((T, D), row), _full((SUB, D))],
        out_shape=[jax.ShapeDtypeStruct((L, D), F32), jax.ShapeDtypeStruct((SUB, D), F32)],
        compiler_params=_params(1),
    )(x, target, nf)


def _post_bwd(dx2, y, o, p, modrows, sp, w_out):
    L = dx2.shape[0]
    T = _tile(L, 256)

    def body(dx2_ref, y_ref, o_ref, z_ref, mod_ref, sp_ref, w_ref, dy_ref, do_ref, dz_ref, dya_ref, accv_ref, accs_ref):
        @pl.when(pl.program_id(0) == 0)
        def _():
            accv_ref[...] = jnp.zeros_like(accv_ref)
            accs_ref[...] = jnp.zeros_like(accs_ref)

        dx2v = dx2_ref[...]
        accv_ref[0:1, :] += _sum0(dx2v * y_ref[...])
        dyb = (mod_ref[2:3, :] * dx2v).astype(MXU)
        dy_ref[...] = dyb
        dyc = _dot(dyb, w_ref[...], NT)
        dya_ref[...] = dyc[:, 0:AW]
        ndw = sp_ref[2:3, :]
        z = z_ref[...]
        sgz = _sig(z)
        dsz = sgz * (1.0 + z * (1.0 - sgz))
        dndw = jnp.zeros((1, HD), F32)
        for h in range(H):
            sl = slice(h * HD, (h + 1) * HD)
            n, r = _rms(o_ref[:, sl])
            dyh = dyc[:, AW + h * HD:AW + (h + 1) * HD]
            zh = z[:, sl]
            don = dyh * (zh * sgz[:, sl])
            dz_ref[:, sl] = dyh * (n * ndw) * dsz[:, sl]
            dndw = dndw + _sum0(don * n)
            do_ref[:, sl] = _rms_bwd(don * ndw, n, r)
        accs_ref[0:1, :] += dndw

    row = lambda i: (i, 0)
    zcol = (3 * AW + 3 * H * HD) // (H * HD)
    return pl.pallas_call(
        body, name="post_bwd", grid=(L // T,),
        in_specs=[pl.BlockSpec((T, D), row), pl.BlockSpec((T, D), row), pl.BlockSpec((T, H * HD), row),
                  pl.BlockSpec((T, H * HD), lambda i: (i, zcol)), _full((SUB, D)), _full((SUB, LANES)), _full((D, D))],
        out_specs=[pl.BlockSpec((T, D), row)] + [pl.BlockSpec((T, H * HD), row)] * 3 + [_full((SUB, D)), _full((SUB, LANES))],
        out_shape=[jax.ShapeDtypeStruct((L, D), MXU)] + [jax.ShapeDtypeStruct((L, H * HD), F32)] * 3
        + [jax.ShapeDtypeStruct((SUB, D), F32), jax.ShapeDtypeStruct((SUB, LANES), F32)],
        compiler_params=_params(1),
    )(dx2, y, o, p, modrows, sp, w_out)


def _pre_bwd(p, cub, qcb, dqn, dkn, dvs, dya, dz, dgb, pa, cq, sp):
    L = p.shape[0]
    T = _tile(L, 256)
    ni = L // T
    scale = HD ** -0.5
    w3 = 3 * AW + 3 * H * HD

    def body(pm_ref, cu_ref, qc_ref, ps_ref, dq_ref, dk_ref, dv_ref, dya_ref, dz_ref, dgb_ref, pa_ref, cq_ref, sp_ref,
             dp_ref, dpa_ref, dcq_ref, dsp_ref, carry_u, carry_q):
        i = pl.program_id(0)

        @pl.when(i == 0)
        def _():
            dpa_ref[...] = jnp.zeros_like(dpa_ref)
            dcq_ref[...] = jnp.zeros_like(dcq_ref)
            dsp_ref[...] = jnp.zeros_like(dsp_ref)
            carry_u[...] = jnp.zeros_like(carry_u)
            carry_q[...] = jnp.zeros_like(carry_q)

        a_b, a_c, a_x = pm_ref[:, 0:AW], pm_ref[:, AW:2 * AW], pm_ref[:, 2 * AW:3 * AW]
        u = a_c * a_x
        cu = cu_ref[...].astype(F32)
        yp = a_b * cu
        bd = _blockdiag_mean(AW, A_GROUP)
        ra = lax.rsqrt(_dot_f32(yp * yp, bd, NN, exact="b") + EPS)
        na = yp * ra
        dya = dya_ref[...]
        dpa_ref[3:4, :] += _sum0(dya * na)
        dna = dya * pa_ref[3:4, :]
        dyp = ra * (dna - na * _dot_f32(dna * na, bd, NN, exact="b"))
        dcu = dyp * a_b
        dcs = [dcu] + [_shift_up(dcu, s, carry_u[...]) for s in (1, 2)]
        du = pa_ref[2:3, :] * dcs[0]
        for s in range(3):
            dpa_ref[2 - s:3 - s, :] += _sum0(dcs[s] * u)
            if s:
                du = du + pa_ref[2 - s:3 - s, :] * dcs[s]
        carry_u[...] = dcu[0:SUB, :]
        dp_ref[:, 0:AW] = (dyp * cu).astype(MXU)
        dp_ref[:, AW:2 * AW] = (du * a_x).astype(MXU)
        dp_ref[:, 2 * AW:3 * AW] = (du * a_c).astype(MXU)

        qkv = pm_ref[:, 3 * AW:w3]
        qc = qc_ref[...].astype(F32)
        sg = _sig(qc)
        qs = qc * sg
        parts = []
        for h in range(H):
            q = qs[:, h * HD:(h + 1) * HD]
            rq = lax.rsqrt(jnp.sum(q * q, axis=-1, keepdims=True) + EPS)
            parts.append(_l2_bwd(dq_ref[:, h * HD:(h + 1) * HD] * scale, q * rq, rq))
        for h in range(H):
            k = qs[:, (H + h) * HD:(H + h + 1) * HD]
            rk = lax.rsqrt(jnp.sum(k * k, axis=-1, keepdims=True) + EPS)
            parts.append(_l2_bwd(dk_ref[:, h * HD:(h + 1) * HD], k * rk, rk))
        parts.append(dv_ref[...])
        dqc = jnp.concatenate(parts, axis=-1) * (sg * (1.0 + qc * (1.0 - sg)))
        dqs = [dqc] + [_shift_up(dqc, s, carry_q[...]) for s in (1, 2, 3)]
        dqkv = cq_ref[3:4, :] * dqs[0]
        for s in range(4):
            dcq_ref[3 - s:4 - s, :] += _sum0(dqs[s] * qkv)
            if s:
                dqkv = dqkv + cq_ref[3 - s:4 - s, :] * dqs[s]
        dp_ref[:, 3 * AW:w3] = dqkv.astype(MXU)
        carry_q[...] = dqc[0:SUB, :]
        dp_ref[:, w3:w3 + H * HD] = dz_ref[...].astype(MXU)

        lane, a, xb, beta, g = _gate_small(ps_ref[...], sp_ref)
        dgb = dgb_ref[...]
        dbeta = jnp.where(lane < H, dgb, 0.0)
        dg = jnp.where((lane >= H) & (lane < 2 * H), dgb, 0.0)
        dalpha = dg * a * _sig(xb)
        dsp_ref[0:1, :] += _sum0(dg * g)
        dsp_ref[1:2, :] += _sum0(dalpha)
        dp_ref[:, w3 + H * HD:P_PAD] = (dbeta * beta * (1.0 - beta) + dalpha).astype(MXU)

    row = lambda i: (ni - 1 - i, 0)
    hrow = pl.BlockSpec((T, H * HD), row)
    return pl.pallas_call(
        body, name="pre_bwd", grid=(ni,),
        in_specs=[pl.BlockSpec((T, w3), row), pl.BlockSpec((T, AW), row), pl.BlockSpec((T, 3 * H * HD), row),
                  pl.BlockSpec((T, LANES), lambda i: (ni - 1 - i, (P_PAD - LANES) // LANES)),
                  hrow, hrow, hrow, pl.BlockSpec((T, AW), row), hrow,
                  pl.BlockSpec((T, LANES), row),
                  _full((SUB, AW)), _full((SUB, 3 * H * HD)), _full((SUB, LANES))],
        out_specs=[pl.BlockSpec((T, P_PAD), row), _full((SUB, AW)), _full((SUB, 3 * H * HD)), _full((SUB, LANES))],
        out_shape=[jax.ShapeDtypeStruct((L, P_PAD), MXU), jax.ShapeDtypeStruct((SUB, AW), F32),
                   jax.ShapeDtypeStruct((SUB, 3 * H * HD), F32), jax.ShapeDtypeStruct((SUB, LANES), F32)],
        scratch_shapes=[pltpu.VMEM((SUB, AW), F32), pltpu.VMEM((SUB, 3 * H * HD), F32)],
        compiler_params=_params(1),
    )(p, cub, qcb, p, dqn, dkn, dvs, dya, dz, dgb, pa, cq, sp)


def _in_bwd(dp, w_in, x, dx2, modrows, vec):
    L = x.shape[0]
    T = _tile(L, 256)

    def body(dp_ref, w_ref, x_ref, dx2_ref, mod_ref, vec_ref, dx_ref, accv_ref):
        @pl.when(pl.program_id(0) == 0)
        def _():
            accv_ref[...] = jnp.zeros_like(accv_ref)

        dh = _dot(dp_ref[...], w_ref[...], NT)
        n, r = _rms(x_ref[...])
        nw, sc = vec_ref[0:1, :], mod_ref[1:2, :]
        accv_ref[0:1, :] += _sum0(dh)
        accv_ref[1:2, :] += _sum0(dh * n * nw)
        accv_ref[2:3, :] += _sum0(dh * n * (1.0 + sc))
        dx_ref[...] = _rms_bwd(dh * nw * (1.0 + sc), n, r) + dx2_ref[...]

    row = lambda i: (i, 0)
    return pl.pallas_call(
        body, name="in_bwd", grid=(L // T,),
        in_specs=[pl.BlockSpec((T, P_PAD), row), _full((D, P_PAD)), pl.BlockSpec((T, D), row),
                  pl.BlockSpec((T, D), row), _full((SUB, D)), _full((SUB, D))],
        out_specs=[pl.BlockSpec((T, D), row), _full((SUB, D))],
        out_shape=[jax.ShapeDtypeStruct((L, D), F32), jax.ShapeDtypeStruct((SUB, D), F32)],
        compiler_params=_params(1),
    )(dp, w_in, x, dx2, modrows, vec)


def _wgrad(a, b, tm, tn, name):
    L, m = a.shape
    n = b.shape[1]
    tl = _tile(L, 512)
    tm, tn = _tile(m, tm), _tile(n, tn)
    nl = L // tl

    def body(a_ref, b_ref, o_ref, acc):
        @pl.when(pl.program_id(2) == 0)
        def _():
            acc[...] = jnp.zeros_like(acc)

        acc[...] += _dot(a_ref[...], b_ref[...], TN)

        @pl.when(pl.program_id(2) == nl - 1)
        def _():
            o_ref[...] = acc[...].astype(o_ref.dtype)

    return pl.pallas_call(
        body, name=name, grid=(m // tm, n // tn, nl),
        in_specs=[pl.BlockSpec((tl, tm), lambda i, j, l: (l, i)), pl.BlockSpec((tl, tn), lambda i, j, l: (l, j))],
        out_specs=pl.BlockSpec((tm, tn), lambda i, j, l: (i, j)),
        out_shape=jax.ShapeDtypeStruct((m, n), MXU), scratch_shapes=[pltpu.VMEM((tm, tn), F32)],
        compiler_params=_params(3),
    )(a, b)


def _wgrad_cols(a, b, tm, n_shard, wpad, count, name):
    L, m = a.shape
    n = b.shape[1]
    tl = _tile(L, 512)
    tm = _tile(m, tm)
    nl = L // tl
    wins = _shard_windows(n_shard, count)
    assert all(a_ * LANES + win <= n for a_, _, win in wins), (wins, n)

    def body(a_ref, b_ref, o_ref, acc):
        @pl.when(pl.program_id(1) == 0)
        def _():
            acc[...] = jnp.zeros_like(acc)

        acc[...] += _dot(a_ref[...], b_ref[...], TN)

        @pl.when(pl.program_id(1) == nl - 1)
        def _():
            for k, (a_, s, win) in enumerate(wins):
                xk = acc[:, a_ * LANES:a_ * LANES + win]
                if s:
                    xk = pltpu.roll(xk, win - s, 1)
                o_ref[k] = _fit_lanes(xk, wpad).astype(o_ref.dtype)

    return pl.pallas_call(
        body, name=name, grid=(m // tm, nl),
        in_specs=[pl.BlockSpec((tl, tm), lambda i, l: (l, i)), pl.BlockSpec((tl, n), lambda i, l: (l, 0))],
        out_specs=pl.BlockSpec((count, tm, wpad), lambda i, l: (0, i, 0)),
        out_shape=jax.ShapeDtypeStruct((count, m, wpad), MXU),
        scratch_shapes=[pltpu.VMEM((tm, n), F32)],
        compiler_params=_params(2),
    )(a, b)


def _adamw(w, g, m, v, name):
    r, n = w.shape
    tr = _tile(r, 512)
    bc1 = 1.0 - ADAM_B1 ** ADAM_STEP
    bc2 = 1.0 - ADAM_B2 ** ADAM_STEP

    def body(w_ref, g_ref, m_ref, v_ref, d_ref, nm_ref, nv_ref):
        gv = g_ref[...]
        nm = ADAM_B1 * m_ref[...] + (1.0 - ADAM_B1) * gv
        nv = ADAM_B2 * v_ref[...] + (1.0 - ADAM_B2) * (gv * gv)
        nm_ref[...] = nm
        nv_ref[...] = nv
        d_ref[...] = -ADAM_LR * ((nm / bc1) / (jnp.sqrt(nv / bc2) + ADAM_EPS) + ADAM_WD * w_ref[...])

    spec = pl.BlockSpec((tr, n), lambda i: (i, 0))
    return pl.pallas_call(
        body, name=name, grid=(r // tr,), in_specs=[spec] * 4, out_specs=[spec] * 3,
        out_shape=[jax.ShapeDtypeStruct((r, n), F32)] * 3, compiler_params=_params(1),
    )(w, g, m, v)


def _rows8(rows, width):
    out = jnp.zeros((SUB, width), F32)
    for r, vrow in enumerate(rows):
        out = out.at[r, :vrow.shape[0]].set(vrow)
    return out


def _at_lanes(v4, start):
    return jnp.zeros((LANES,), F32).at[start:start + v4.shape[0]].set(v4)


def _pad_rows(flat, mult):
    n = flat.shape[0]
    pad = (-n) % mult
    return jnp.pad(flat, (0, pad)) if pad else flat


IN_PAD = 512
UP_PAD = 768


def _local_fwd_bwd(x, target, mod_full, small_w, full_w, on_grads=None):
    norm1_w, norm2_w, norm_a_w, a_log, dt_bias, norm_dn_w, norm_f_w = small_w
    w_in_f, w_out_f, w_up_f, w_down_f, conv_a_f, conv_q_f, conv_f_f = full_w

    def layer_params(i):
        modrows = jnp.concatenate([mod_full[i], jnp.zeros((SUB - N_MOD, D), F32)], axis=0)
        vec = _rows8([norm1_w[i], norm2_w[i]], D)
        pa = _rows8([conv_a_f[i, 0], conv_a_f[i, 1], conv_a_f[i, 2], norm_a_w[i]], AW)
        cq = _rows8([conv_q_f[i, k] for k in range(4)], 3 * H * HD)
        sp = _rows8([_at_lanes(a_log[i], H), _at_lanes(dt_bias[i], H), norm_dn_w[i]], LANES)
        cff = _rows8([conv_f_f[i, k] for k in range(3)], 2 * DFF)
        return modrows, vec, pa, cq, sp, cff

    saved = []
    xi = x
    for i in range(DEPTH):
        modrows, vec, pa, cq, sp, cff = layer_params(i)
        p, h1 = _in_proj(xi, modrows, vec, w_in_f[i])
        qn, kn, vs, gb, ya, cub, qcb = _pre_fwd(p, pa, cq, sp)
        o, states, tinvs = _gdr_fwd(qn, kn, vs, gb)
        y, x2, yb = _post_fwd(o, p, ya, xi, modrows, sp, w_out_f[i])
        h2, gp0, up0, gc0, uc0, f0, d0 = _ffn_fwd_half(x2, modrows, vec, w_up_f[i], cff, w_down_f[i], 0, None)
        gp1, up1, gc1, uc1, f1, dff, x3 = _ffn_fwd_half(x2, modrows, vec, w_up_f[i], cff, w_down_f[i], 1, d0)
        saved.append(dict(x=xi, p=p, h1=h1, qn=qn, kn=kn, vs=vs, gb=gb, ya=ya, cub=cub, qcb=qcb, o=o, states=states,
                          tinvs=tinvs, y=y, x2=x2, yb=yb,
                          h2=h2, gpre=(gp0, gp1), upre=(up0, up1), gc=(gc0, gc1), uc=(uc0, uc1), f=(f0, f1), d=dff))
        xi = x3

    dx, facc = _final(xi, target, _rows8([norm_f_w], D))
    loss_local = jnp.sum(facc[0])
    d_norm_f = facc[1]

    gw_in, gw_out, gw_up, gw_down = [None] * DEPTH, [None] * DEPTH, [None] * DEPTH, [None] * DEPTH
    g_small = [None] * DEPTH
    for i in reversed(range(DEPTH)):
        s = saved[i]
        modrows, vec, pa, cq, sp, cff = layer_params(i)
        dd, dgp0, dup0, dh0, dcg0, dcu0 = _ffn_bwd_half(dx, modrows, s["gpre"][0], s["upre"][0], s["gc"][0], s["uc"][0],
                                                        cff, w_down_f[i], w_up_f[i], 0, None)
        dgp1, dup1, dx2, accf, dcg1, dcu1 = _ffn_bwd_half(dx, modrows, s["gpre"][1], s["upre"][1], s["gc"][1], s["uc"][1],
                                                          cff, w_down_f[i], w_up_f[i], 1, (s["d"], s["x2"], vec, dh0))
        n_up, up_pad = 2 * DFF // N_DEV, UP_PAD
        gw_up[i] = jnp.concatenate([_wgrad_cols(s["h2"], t, 1024, n_up, up_pad, FF_CW // n_up, "wgrad_up")
                                    for t in (dgp0, dgp1, dup0, dup1)], axis=0)
        gw_down[i] = jnp.concatenate([_wgrad(s["f"][0], dd, FF_CW, 1024, "wgrad_down"),
                                      _wgrad(s["f"][1], dd, FF_CW, 1024, "wgrad_down")],
                                     axis=0).reshape(N_DEV, DFF // N_DEV, D)
        dy, do, dz, dya, accp, accs = _post_bwd(dx2, s["y"], s["o"], s["p"], modrows, sp, w_out_f[i])
        gw_out[i] = jnp.concatenate([_wgrad(s["ya"], dy, 512, 1024, "wgrad_out"),
                                     _wgrad(s["yb"], dy, 512, 1024, "wgrad_out")], axis=0).reshape(N_DEV, D // N_DEV, D)
        dqn, dkn, dvs, dgb = _gdr_bwd(s["qn"], s["kn"], s["vs"], s["gb"], s["states"], s["tinvs"], do)
        dp, dpa, dcq, dsp = _pre_bwd(s["p"], s["cub"], s["qcb"], dqn, dkn, dvs, dya, dz, dgb, pa, cq, sp)
        gw_in[i] = _wgrad_cols(s["h1"], dp, 1024, P_IN // N_DEV, IN_PAD, N_DEV, "wgrad_in")
        dx, acci = _in_bwd(dp, w_in_f[i], s["x"], dx2, modrows, vec)
        dconv_ff = jnp.concatenate([dcg0, dcg1, dcu0, dcu1], axis=1)[0:3]
        dmod = jnp.stack([acci[0], acci[1], accp[0], accf[1], accf[2], accf[0]])
        g_small[i] = dict(norm1=acci[2], norm2=accf[3], norm_a=dpa[3], a_log=dsp[0, H:2 * H], dt_bias=dsp[1, H:2 * H],
                          norm_dn=accs[0], conv_a=dpa[0:3], conv_qkv=dcq[0:4], conv_ff=dconv_ff, dmod=dmod.reshape(-1))
        if on_grads is not None:
            dx = on_grads(i, [gw_in[i], gw_out[i], gw_up[i], gw_down[i]], dx)
    return loss_local, dx, gw_in, gw_out, gw_up, gw_down, g_small, d_norm_f


def kernel(x, c, ada_w, ada_b, norm1_w, w_in, conv_a_w, norm_a_w, conv_qkv_w, a_log, dt_bias, norm_dn_w, w_out, norm2_w, w_up, conv_ff_w, w_down, norm_f_w, loss_target, m_ada_w, m_ada_b, m_norm1_w, m_w_in, m_conv_a_w, m_norm_a_w, m_conv_qkv_w, m_a_log, m_dt_bias, m_norm_dn_w, m_w_out, m_norm2_w, m_w_up, m_conv_ff_w, m_w_down, m_norm_f_w, v_ada_w, v_ada_b, v_norm1_w, v_w_in, v_conv_a_w, v_norm_a_w, v_conv_qkv_w, v_a_log, v_dt_bias, v_norm_dn_w, v_w_out, v_norm2_w, v_w_up, v_conv_ff_w, v_w_down, v_norm_f_w):
    ax, ay, ac = lax.axis_index("x"), lax.axis_index("y"), lax.axis_index("c")
    me = 4 * ax + 2 * ay + ac
    x = x[0]
    target = loss_target[0]
    n_in, n_up = P_IN // N_DEV, 2 * DFF // N_DEV

    def lane_pad(t, width):
        return jnp.pad(t.astype(MXU), ((0, 0), (0, 0), (0, width - t.shape[-1])))

    conv_blob = _pad_rows(jnp.concatenate([t.reshape(-1) for t in (conv_a_w, conv_qkv_w, conv_ff_w)]),
                          SUB * LANES).reshape(-1, LANES)
    c_rows = jnp.zeros((SUB, D), F32).at[0].set(c[0])
    send = [lane_pad(w_in, IN_PAD), w_out.astype(MXU), lane_pad(w_up, UP_PAD), w_down.astype(MXU)]
    got = [None] * DEPTH
    g_in0, g_conv, g_c = _all_gather([send[0][0], conv_blob, c_rows], "gather_weights", in_vmem=False)
    shards, _ = lax.optimization_barrier(([t[0] for t in send[1:]], g_c))
    got[0] = [g_in0] + _all_gather_async(shards, "gather_weights_l0", collective_id=0)
    for i in range(1, DEPTH):
        shards, _ = lax.optimization_barrier(([t[i] for t in send], g_c))
        got[i] = _all_gather_async(shards, "gather_weights_l%d" % i, collective_id=i)
    w_in_f = [_interleave_cols(g[0][:, None], n_in, P_PAD, "interleave_w_in")[0] for g in got]
    w_up_f = [_interleave_cols(g[2][:, None], n_up, 2 * DFF, "interleave_w_up")[0] for g in got]
    w_out_f = [g[1].reshape(D, D) for g in got]
    w_down_f = [g[3].reshape(DFF, D) for g in got]
    sg = g_conv.reshape(N_DEV, -1)
    o1 = conv_a_w.size
    o2 = o1 + conv_qkv_w.size
    o3 = o2 + conv_ff_w.size
    conv_a_f = sg[:, 0:o1].reshape(N_DEV, DEPTH, 3, AW // N_DEV).transpose(1, 2, 0, 3).reshape(DEPTH, 3, AW)
    conv_q_f = sg[:, o1:o2].reshape(N_DEV, DEPTH, 4, 3 * H * HD // N_DEV).transpose(1, 2, 0, 3).reshape(DEPTH, 4, 3 * H * HD)
    conv_f_f = sg[:, o2:o3].reshape(N_DEV, DEPTH, 3, n_up).transpose(1, 2, 0, 3).reshape(DEPTH, 3, 2 * DFF)

    c_all = jnp.concatenate([g_c[:, 0], jnp.zeros((16 - N_DEV, D), F32)], axis=0)
    n_ada = N_MOD * D // N_DEV
    ada_b_cols = lax.dynamic_slice_in_dim(ada_b, me * n_ada, n_ada, axis=1)[:, None, :]
    mod_sh = _mod_fwd(c_all, ada_w, ada_b_cols)
    mod_all = _all_gather([mod_sh.reshape(DEPTH * 16, n_ada)], "gather_mod", in_vmem=True)[0]
    mod_all = mod_all.reshape(N_DEV, DEPTH, 16, n_ada)
    mod_mine = lax.dynamic_index_in_dim(mod_all, me, axis=2, keepdims=False)
    mod_full = mod_mine.transpose(1, 0, 2).reshape(DEPTH, N_MOD, D)

    tags = ["w_in", "w_out", "w_up", "w_down"]
    received = [None] * DEPTH

    def on_grads(i, gs_i, dx):
        received[i] = _rs_exchange_async(gs_i, "rs_exchange_l%d" % i, collective_id=DEPTH + i)
        return dx

    loss_local, dx, _, _, _, _, g_small, d_norm_f = _local_fwd_bwd(
        x, target, mod_full, (norm1_w, norm2_w, norm_a_w, a_log, dt_bias, norm_dn_w, norm_f_w),
        (w_in_f, w_out_f, w_up_f, w_down_f, conv_a_f, conv_q_f, conv_f_f), on_grads)
    loss = lax.psum(loss_local, ("x", "y", "c"))
    grad_x = dx[None]

    keys = ["dmod", "norm1", "norm2", "norm_a", "a_log", "dt_bias", "norm_dn", "conv_a", "conv_qkv", "conv_ff"]
    stacked = {k: jnp.stack([g_small[i][k] for i in range(DEPTH)]) for k in keys}
    flat_parts = [stacked[k].reshape(-1) for k in keys] + [d_norm_f]
    sizes = [int(t.shape[0]) for t in flat_parts]
    sflat = _pad_rows(jnp.concatenate(flat_parts), SUB * LANES).reshape(-1, LANES)
    sall = _all_gather([sflat], "gather_small_grads", in_vmem=True)[0]
    ssum = _sum_devices(sall).reshape(-1)
    so = [0]
    for sz in sizes:
        so.append(so[-1] + sz)
    red = {k: ssum[so[n]:so[n + 1]].reshape(stacked[k].shape) for n, k in enumerate(keys)}
    g_norm_f = ssum[so[len(keys)]:so[len(keys) + 1]]
    dmod_all = sall[:, 0:sizes[0] // LANES, :].reshape(N_DEV, DEPTH, N_MOD * D)

    g_ada_b = red["dmod"].reshape(DEPTH, N_MOD * D)
    dmod_cols = lax.dynamic_slice_in_dim(dmod_all, me * n_ada, n_ada, axis=2).transpose(1, 0, 2)
    dmod_cols = jnp.concatenate([dmod_cols, jnp.zeros((DEPTH, 16 - N_DEV, n_ada), F32)], axis=1)
    g_ada_w = _mod_bwd(c_all, dmod_cols)
    g_conv_a = lax.dynamic_slice_in_dim(red["conv_a"], me * (AW // N_DEV), AW // N_DEV, axis=2)
    g_conv_qkv = lax.dynamic_slice_in_dim(red["conv_qkv"], me * (3 * H * HD // N_DEV), 3 * H * HD // N_DEV, axis=2)
    g_conv_ff = lax.dynamic_slice_in_dim(red["conv_ff"], me * n_up, n_up, axis=2)

    mine = [jnp.stack([_rs_sum(received[i][k], "rs_sum_" + t) for i in range(DEPTH)]) for k, t in enumerate(tags)]
    g_w_in = mine[0][:, :, :n_in]
    g_w_out = mine[1]
    g_w_up = mine[2][:, :, :n_up]
    g_w_down = mine[3]

    grads = dict(ada_w=g_ada_w, ada_b=g_ada_b, norm1_w=red["norm1"], w_in=g_w_in, conv_a_w=g_conv_a,
                 norm_a_w=red["norm_a"], conv_qkv_w=g_conv_qkv, a_log=red["a_log"], dt_bias=red["dt_bias"],
                 norm_dn_w=red["norm_dn"], w_out=g_w_out, norm2_w=red["norm2"], w_up=g_w_up, conv_ff_w=g_conv_ff,
                 w_down=g_w_down, norm_f_w=g_norm_f)
    weights = dict(ada_w=ada_w, ada_b=ada_b, norm1_w=norm1_w, w_in=w_in, conv_a_w=conv_a_w, norm_a_w=norm_a_w,
                   conv_qkv_w=conv_qkv_w, a_log=a_log, dt_bias=dt_bias, norm_dn_w=norm_dn_w, w_out=w_out,
                   norm2_w=norm2_w, w_up=w_up, conv_ff_w=conv_ff_w, w_down=w_down, norm_f_w=norm_f_w)
    ms = dict(ada_w=m_ada_w, ada_b=m_ada_b, norm1_w=m_norm1_w, w_in=m_w_in, conv_a_w=m_conv_a_w, norm_a_w=m_norm_a_w,
              conv_qkv_w=m_conv_qkv_w, a_log=m_a_log, dt_bias=m_dt_bias, norm_dn_w=m_norm_dn_w, w_out=m_w_out,
              norm2_w=m_norm2_w, w_up=m_w_up, conv_ff_w=m_conv_ff_w, w_down=m_w_down, norm_f_w=m_norm_f_w)
    vs_ = dict(ada_w=v_ada_w, ada_b=v_ada_b, norm1_w=v_norm1_w, w_in=v_w_in, conv_a_w=v_conv_a_w, norm_a_w=v_norm_a_w,
               conv_qkv_w=v_conv_qkv_w, a_log=v_a_log, dt_bias=v_dt_bias, norm_dn_w=v_norm_dn_w, w_out=v_w_out,
               norm2_w=v_norm2_w, w_up=v_w_up, conv_ff_w=v_conv_ff_w, w_down=v_w_down, norm_f_w=v_norm_f_w)
    names = list(weights)
    big_names = ["ada_w", "w_in", "w_out", "w_up", "w_down"]
    delta, new_m, new_v = {}, {}, {}
    for n in big_names:
        shp = weights[n].shape
        two = lambda t: t.reshape(-1, shp[-1])
        dl, nm, nv = _adamw(two(weights[n]), two(grads[n]), two(ms[n]), two(vs_[n]), "adamw_" + n)
        delta[n], new_m[n], new_v[n] = dl.reshape(shp), nm.reshape(shp), nv.reshape(shp)
    small_names = [n for n in names if n not in big_names]

    def pack(dct):
        return _pad_rows(jnp.concatenate([dct[n].reshape(-1) for n in small_names]), SUB * LANES).reshape(-1, LANES)

    dl, nm, nv = _adamw(pack(weights), pack(grads), pack(ms), pack(vs_), "adamw_small")
    off = 0
    for n in small_names:
        sz, shp = weights[n].size, weights[n].shape
        delta[n] = dl.reshape(-1)[off:off + sz].reshape(shp)
        new_m[n] = nm.reshape(-1)[off:off + sz].reshape(shp)
        new_v[n] = nv.reshape(-1)[off:off + sz].reshape(shp)
        off += sz

    return (loss, grad_x, *[grads[n] for n in names], *[delta[n] for n in names],
            *[new_m[n] for n in names], *[new_v[n] for n in names])
```

```python
import functools
import math

import jax
import jax.numpy as jnp
from jax import lax
from jax.experimental import pallas as pl
from jax.experimental.pallas import tpu as pltpu
from jax.experimental.pallas import tpu_sc as plsc

F32 = jnp.float32
MXU = jnp.bfloat16

D = 1024
DEPTH = 4
N_MOD = 6
AW = 512
A_GROUP = 64
H = 4
HD = 128
CK = 64
DFF = 2816
P_IN = 3592
P_PAD = 3712
EPS = 1e-6
N_DEV = 8
LANES = 128
SUB = 8
VMEM_LIMIT = 56 * 1024 * 1024

ADAM_LR, ADAM_B1, ADAM_B2, ADAM_EPS, ADAM_WD, ADAM_STEP = 0.001, 0.9, 0.999, 1e-08, 0.01, 10

NN = ((1,), (0,))
NT = ((1,), (1,))
TN = ((0,), (0,))
HI = lax.Precision.HIGHEST
MESH = pl.DeviceIdType.MESH


def _dot(a, b, dims, prec=None):
    if prec is None:
        a = a.astype(MXU) if a.dtype == F32 else a
        b = b.astype(MXU) if b.dtype == F32 else b
    return lax.dot_general(a, b, (dims, ((), ())), precision=prec, preferred_element_type=F32)


def _params(n_grid=0, limit=VMEM_LIMIT):
    sem = ("arbitrary",) * n_grid if n_grid else None
    return pltpu.CompilerParams(dimension_semantics=sem, vmem_limit_bytes=limit)


def _tile(n, want):
    if n <= want:
        return n
    t = want - want % SUB
    while n % t:
        t -= SUB
    assert t > 0, (n, want)
    return t


def _full(shape):
    nd = len(shape)
    return pl.BlockSpec(shape, lambda *_: (0,) * nd)


def _sig(x):
    return jax.nn.sigmoid(x)


def _rms(x):
    r = lax.rsqrt(jnp.mean(x * x, axis=-1, keepdims=True) + EPS)
    return x * r, r


def _rms_bwd(dn, n, r):
    return r * (dn - n * jnp.mean(dn * n, axis=-1, keepdims=True))


def _l2_bwd(dn, n, r):
    return r * (dn - n * jnp.sum(dn * n, axis=-1, keepdims=True))


def _sum0(x):
    return jnp.sum(x, axis=0, keepdims=True)


def _shift_down(x, s, halo):
    ext = jnp.concatenate([halo, x], axis=0)
    return pltpu.roll(ext, s, 0)[SUB:, :]


def _shift_up(x, s, halo):
    t = x.shape[0]
    ext = jnp.concatenate([x, halo], axis=0)
    return pltpu.roll(ext, t + SUB - s, 0)[:t, :]


def _conv_fwd(x, w_ref, width, halo):
    sh = [x] + [_shift_down(x, s, halo) for s in range(1, width)]
    out = w_ref[width - 1:width, :] * sh[0]
    for s in range(1, width):
        out = out + w_ref[width - 1 - s:width - s, :] * sh[s]
    return out, sh


def _blockdiag_mean(n, group):
    r = lax.shift_right_logical(lax.broadcasted_iota(jnp.int32, (n, n), 0), int(math.log2(group)))
    c = lax.shift_right_logical(lax.broadcasted_iota(jnp.int32, (n, n), 1), int(math.log2(group)))
    return jnp.where(r == c, 1.0 / group, 0.0).astype(F32)


def _softplus(x):
    return jnp.maximum(x, 0.0) + jnp.log(1.0 + jnp.exp(-jnp.abs(x)))


def _my_place():
    return lax.axis_index("x"), lax.axis_index("y"), lax.axis_index("c")


def _all_gather(shards, name, in_vmem):
    nt = len(shards)

    def body(*refs):
        x_refs, out_refs = refs[:nt], refs[nt:2 * nt]
        send_sems, recv_sems, local_sems = refs[2 * nt:]
        x, y, c = _my_place()
        me, sibling = (x, y, c), (x, y, 1 - c)
        chips = [(1 - x, y), (x, 1 - y), (1 - x, 1 - y)]
        everything = []
        for t in range(nt):
            x_ref, out_ref = x_refs[t], out_refs[t]

            def blk(px, py, pc, out_ref=out_ref):
                return out_ref.at[4 * px + 2 * py + pc]

            def copy(k, block, to, src=None, t=t, blk=blk):
                return pltpu.make_async_remote_copy(
                    src_ref=blk(*block) if src is None else src, dst_ref=blk(*block),
                    send_sem=send_sems.at[7 * t + k], recv_sem=recv_sems.at[7 * t + k], device_id=to, device_id_type=MESH)

            mine = pltpu.make_async_copy(x_ref, blk(*me), local_sems.at[t])
            mine.start()
            first = [copy(0, me, sibling, src=x_ref)]
            first += [copy(1 + j, me, (*chip, c), src=x_ref) for j, chip in enumerate(chips)]
            for cp in first:
                cp.start()
            everything.append((copy, mine, first))
        sends = []
        for copy, mine, first in everything:
            passed = [copy(4 + j, (*chip, c), sibling) for j, chip in enumerate(chips)]
            for j, chip in enumerate(chips):
                copy(1 + j, (*chip, c), me).wait_recv()
                passed[j].start()
            sends += first + passed
        for copy, mine, first in everything:
            copy(0, sibling, me).wait_recv()
            for j, chip in enumerate(chips):
                copy(4 + j, (*chip, 1 - c), me).wait_recv()
        for cp in sends:
            cp.wait_send()
        for copy, mine, first in everything:
            mine.wait()

    space = pltpu.VMEM if in_vmem else pl.ANY
    return pl.pallas_call(
        body, name=name,
        out_shape=[jax.ShapeDtypeStruct((N_DEV,) + s.shape, s.dtype) for s in shards],
        in_specs=[pl.BlockSpec(memory_space=space)] * nt,
        out_specs=[pl.BlockSpec(memory_space=space)] * nt,
        scratch_shapes=[pltpu.SemaphoreType.DMA((7 * nt,)), pltpu.SemaphoreType.DMA((7 * nt,)),
                        pltpu.SemaphoreType.DMA((nt,))],
        compiler_params=pltpu.CompilerParams(vmem_limit_bytes=VMEM_LIMIT),
    )(*shards)


def _all_gather_async(shards, name, collective_id):
    nt = len(shards)
    hbm = pltpu.MemorySpace.HBM
    x_refs = [jax.new_ref(s, memory_space=hbm) for s in shards]
    out_refs = [jax.empty_ref(jax.ShapeDtypeStruct((N_DEV,) + s.shape, s.dtype), memory_space=hbm) for s in shards]

    @pl.kernel(mesh=plsc.ScalarSubcoreMesh(axis_name="sequencer", num_cores=1), name=name,
               scratch_types=(pltpu.SemaphoreType.DMA((7 * nt,)), pltpu.SemaphoreType.DMA((7 * nt,)),
                              pltpu.SemaphoreType.DMA((nt,))),
               compiler_params=pltpu.CompilerParams(collective_id=collective_id))
    def launch(send_sems, recv_sems, local_sems):
        x, y, c = _my_place()
        me, sibling = (x, y, c), (x, y, 1 - c)
        chips = [(1 - x, y), (x, 1 - y), (1 - x, 1 - y)]
        barrier = pltpu.get_barrier_semaphore()
        for peer in [sibling] + [(*chip, c) for chip in chips]:
            pl.semaphore_signal(barrier, inc=1, device_id=peer, device_id_type=MESH)
        pl.semaphore_wait(barrier, 4)
        everything = []
        for t in range(nt):
            x_ref, out_ref = x_refs[t], out_refs[t]

            def blk(px, py, pc, out_ref=out_ref):
                return out_ref.at[4 * px + 2 * py + pc]

            def copy(k, block, to, src=None, t=t, blk=blk):
                return pltpu.make_async_remote_copy(
                    src_ref=blk(*block) if src is None else src, dst_ref=blk(*block),
                    send_sem=send_sems.at[7 * t + k], recv_sem=recv_sems.at[7 * t + k], device_id=to, device_id_type=MESH)

            mine = pltpu.make_async_copy(x_ref, blk(*me), local_sems.at[t])
            mine.start()
            first = [copy(0, me, sibling, src=x_ref)]
            first += [copy(1 + j, me, (*chip, c), src=x_ref) for j, chip in enumerate(chips)]
            for cp in first:
                cp.start()
            everything.append((copy, mine, first))
        sends = []
        for copy, mine, first in everything:
            passed = [copy(4 + j, (*chip, c), sibling) for j, chip in enumerate(chips)]
            for j, chip in enumerate(chips):
                copy(1 + j, (*chip, c), me).wait_recv()
                passed[j].start()
            sends += first + passed
        for copy, mine, first in everything:
            copy(0, sibling, me).wait_recv()
            for j, chip in enumerate(chips):
                copy(4 + j, (*chip, 1 - c), me).wait_recv()
        for cp in sends:
            cp.wait_send()
        for copy, mine, first in everything:
            mine.wait()

    launch()
    return [r[...] for r in out_refs]


def _rs_exchange_async(srcs, name, collective_id):
    nt = len(srcs)
    hbm = pltpu.MemorySpace.HBM
    src_refs = [jax.new_ref(s, memory_space=hbm) for s in srcs]
    out_refs = [jax.empty_ref(jax.ShapeDtypeStruct(s.shape, s.dtype), memory_space=hbm) for s in srcs]
    flips = [(fx, fy, fc) for fx in (0, 1) for fy in (0, 1) for fc in (0, 1)][1:]

    @pl.kernel(mesh=plsc.ScalarSubcoreMesh(axis_name="sequencer", num_cores=1), name=name,
               scratch_types=(pltpu.SemaphoreType.DMA((7 * nt,)), pltpu.SemaphoreType.DMA((7 * nt,)),
                              pltpu.SemaphoreType.DMA((nt,))),
               compiler_params=pltpu.CompilerParams(collective_id=collective_id))
    def launch(send_sems, recv_sems, local_sems):
        x, y, c = _my_place()
        me = 4 * x + 2 * y + c
        peers = [(1 - x if fx else x, 1 - y if fy else y, 1 - c if fc else c) for fx, fy, fc in flips]
        barrier = pltpu.get_barrier_semaphore()
        for peer in peers:
            pl.semaphore_signal(barrier, inc=1, device_id=peer, device_id_type=MESH)
        pl.semaphore_wait(barrier, len(peers))
        own = [pltpu.make_async_copy(src_refs[t].at[me], out_refs[t].at[me], local_sems.at[t]) for t in range(nt)]
        copies = [pltpu.make_async_remote_copy(
            src_ref=src_refs[t].at[4 * px + 2 * py + pc], dst_ref=out_refs[t].at[me],
            send_sem=send_sems.at[7 * t + f], recv_sem=recv_sems.at[7 * t + f],
            device_id=(px, py, pc), device_id_type=MESH) for t in range(nt) for f, (px, py, pc) in enumerate(peers)]
        for cp in own + copies:
            cp.start()
        for cp in copies + own:
            cp.wait()

    launch()
    return [r[...] for r in out_refs]


def _rs_sum(recv, name):
    _, r, n = recv.shape
    tr = _tile(r, 512)

    def body(r_ref, o_ref):
        s = r_ref[0].astype(F32)
        for k in range(1, N_DEV):
            s = s + r_ref[k].astype(F32)
        o_ref[...] = s

    return pl.pallas_call(
        body, name=name, grid=(r // tr,),
        in_specs=[pl.BlockSpec((N_DEV, tr, n), lambda i: (0, i, 0))],
        out_specs=pl.BlockSpec((tr, n), lambda i: (i, 0)),
        out_shape=jax.ShapeDtypeStruct((r, n), F32), compiler_params=_params(1),
    )(recv)


def _shard_windows(n_shard, count, first=0):
    out = []
    for k in range(first, first + count):
        off = n_shard * k
        a, s = off // LANES, off % LANES
        out.append((a, s, -(-(s + n_shard) // LANES) * LANES))
    return out


def _fit_lanes(x, width):
    have = x.shape[1]
    if have < width:
        return jnp.concatenate([x, jnp.zeros((x.shape[0], width - have), x.dtype)], axis=-1)
    return x[:, :width]


def _interleave_cols(g, n_shard, w_out, name):
    nd, nl, rows, wpad = g.shape
    rb = _tile(rows, 256)
    wins = _shard_windows(n_shard, nd)

    def body(g_ref, o_ref, acc):
        acc[...] = jnp.zeros_like(acc)
        for k, (a, s, win) in enumerate(wins):
            xk = _fit_lanes(g_ref[k].astype(F32), win)
            if s:
                xk = pltpu.roll(xk, s, 1)
            acc[:, a * LANES:a * LANES + win] += xk
        o_ref[...] = acc[...].astype(o_ref.dtype)

    return pl.pallas_call(
        body, name=name, grid=(nl, rows // rb),
        in_specs=[pl.BlockSpec((nd, None, rb, wpad), lambda l, i: (0, l, i, 0))],
        out_specs=pl.BlockSpec((None, rb, w_out), lambda l, i: (l, i, 0)),
        out_shape=jax.ShapeDtypeStruct((nl, rows, w_out), g.dtype),
        scratch_shapes=[pltpu.VMEM((rb, w_out), F32)],
        compiler_params=_params(2),
    )(g)


def _sum_devices(g):
    _, r, n = g.shape

    def body(g_ref, o_ref):
        s = g_ref[0]
        for t in range(1, N_DEV):
            s = s + g_ref[t]
        o_ref[...] = s

    return pl.pallas_call(
        body, name="sum_devices", out_shape=jax.ShapeDtypeStruct((r, n), F32),
        in_specs=[pl.BlockSpec(memory_space=pltpu.VMEM)], out_specs=pl.BlockSpec(memory_space=pltpu.VMEM),
        compiler_params=pltpu.CompilerParams(vmem_limit_bytes=VMEM_LIMIT),
    )(g)


def _mod_fwd(c_all, ada_w, ada_b_cols):
    nl, _, nc = ada_w.shape

    def body(c_ref, w_ref, b_ref, o_ref):
        cv = c_ref[...]
        act = (cv * _sig(cv)).astype(MXU)
        o_ref[...] = _dot(act, w_ref[...].astype(MXU), NN) + b_ref[...]

    return pl.pallas_call(
        body, name="mod_fwd", grid=(nl,),
        in_specs=[_full((16, D)), pl.BlockSpec((None, D, nc), lambda i: (i, 0, 0)),
                  pl.BlockSpec((None, 1, nc), lambda i: (i, 0, 0))],
        out_specs=pl.BlockSpec((None, 16, nc), lambda i: (i, 0, 0)),
        out_shape=jax.ShapeDtypeStruct((nl, 16, nc), F32), compiler_params=_params(1),
    )(c_all, ada_w, ada_b_cols)


def _mod_bwd(c_all, dmod_cols):
    nl, _, nc = dmod_cols.shape

    def body(c_ref, d_ref, o_ref):
        cv = c_ref[...]
        act = (cv * _sig(cv)).astype(MXU)
        o_ref[...] = _dot(act, d_ref[...].astype(MXU), TN)

    return pl.pallas_call(
        body, name="mod_bwd", grid=(nl,),
        in_specs=[_full((16, D)), pl.BlockSpec((None, 16, nc), lambda i: (i, 0, 0))],
        out_specs=pl.BlockSpec((None, D, nc), lambda i: (i, 0, 0)),
        out_shape=jax.ShapeDtypeStruct((nl, D, nc), F32), compiler_params=_params(1),
    )(c_all, dmod_cols)


def _in_proj(x, modrows, vec, w_in):
    L = x.shape[0]
    T = _tile(L, 512)

    def body(x_ref, mod_ref, vec_ref, w_ref, p_ref, h_ref):
        n, _ = _rms(x_ref[...])
        h = n * vec_ref[0:1, :] * (1.0 + mod_ref[1:2, :]) + mod_ref[0:1, :]
        hb = h.astype(MXU)
        h_ref[...] = hb
        p_ref[...] = _dot(hb, w_ref[...], NN)

    return pl.pallas_call(
        body, name="in_proj", grid=(L // T,),
        in_specs=[pl.BlockSpec((T, D), lambda i: (i, 0)), _full((SUB, D)), _full((SUB, D)), _full((D, P_PAD))],
        out_specs=[pl.BlockSpec((T, P_PAD), lambda i: (i, 0)), pl.BlockSpec((T, D), lambda i: (i, 0))],
        out_shape=[jax.ShapeDtypeStruct((L, P_PAD), F32), jax.ShapeDtypeStruct((L, D), MXU)],
        compiler_params=_params(1),
    )(x, modrows, vec, w_in)


def _gate_small(s, sp_ref):
    lane = lax.broadcasted_iota(jnp.int32, s.shape, 1)
    a = -jnp.exp(sp_ref[0:1, :])
    xb = s + sp_ref[1:2, :]
    beta = _sig(s)
    g = a * _softplus(xb)
    return lane, a, xb, beta, g


def _pre_fwd(p, pa, cq, sp):
    L = p.shape[0]
    T = _tile(L, 256)
    scale = HD ** -0.5

    def body(pm_ref, ps_ref, pa_ref, cq_ref, sp_ref, qn_ref, kn_ref, vs_ref, gb_ref, ya_ref, cu_ref, qc_ref,
             u_carry, q_carry):
        @pl.when(pl.program_id(0) == 0)
        def _():
            u_carry[...] = jnp.zeros_like(u_carry)
            q_carry[...] = jnp.zeros_like(q_carry)

        a_b = pm_ref[:, 0:AW]
        u = pm_ref[:, AW:2 * AW] * pm_ref[:, 2 * AW:3 * AW]
        cu, _ = _conv_fwd(u, pa_ref, 3, u_carry[...])
        cu_ref[...] = cu.astype(MXU)
        u_carry[...] = u[T - SUB:T, :]
        yp = a_b * cu
        ms = _dot_f32(yp * yp, _blockdiag_mean(AW, A_GROUP), NN, exact="b")
        ya_ref[...] = (yp * lax.rsqrt(ms + EPS) * pa_ref[3:4, :]).astype(MXU)

        qkv = pm_ref[:, 3 * AW:3 * AW + 3 * H * HD]
        qc, _ = _conv_fwd(qkv, cq_ref, 4, q_carry[...])
        qc_ref[...] = qc.astype(MXU)
        q_carry[...] = qkv[T - SUB:T, :]
        qs = qc * _sig(qc)
        for h in range(H):
            q = qs[:, h * HD:(h + 1) * HD]
            qn_ref[:, h * HD:(h + 1) * HD] = q * (lax.rsqrt(jnp.sum(q * q, axis=-1, keepdims=True) + EPS) * scale)
            k = qs[:, (H + h) * HD:(H + h + 1) * HD]
            kn_ref[:, h * HD:(h + 1) * HD] = k * lax.rsqrt(jnp.sum(k * k, axis=-1, keepdims=True) + EPS)
        vs_ref[...] = qs[:, 2 * H * HD:3 * H * HD]

        lane, _, _, beta, g = _gate_small(ps_ref[...], sp_ref)
        gb_ref[...] = jnp.where(lane < H, beta, jnp.where(lane < 2 * H, g, 0.0))

    w3 = 3 * AW + 3 * H * HD
    row = lambda i: (i, 0)
    return pl.pallas_call(
        body, name="pre_fwd", grid=(L // T,),
        in_specs=[pl.BlockSpec((T, w3), row), pl.BlockSpec((T, LANES), lambda i: (i, (P_PAD - LANES) // LANES)),
                  _full((SUB, AW)), _full((SUB, 3 * H * HD)), _full((SUB, LANES))],
        out_specs=[pl.BlockSpec((T, H * HD), row)] * 3 + [pl.BlockSpec((T, LANES), row), pl.BlockSpec((T, AW), row),
                                                          pl.BlockSpec((T, AW), row), pl.BlockSpec((T, 3 * H * HD), row)],
        out_shape=[jax.ShapeDtypeStruct((L, H * HD), F32)] * 3
        + [jax.ShapeDtypeStruct((L, LANES), F32), jax.ShapeDtypeStruct((L, AW), MXU),
           jax.ShapeDtypeStruct((L, AW), MXU), jax.ShapeDtypeStruct((L, 3 * H * HD), MXU)],
        scratch_shapes=[pltpu.VMEM((SUB, AW), F32), pltpu.VMEM((SUB, 3 * H * HD), F32)],
        compiler_params=_params(1),
    )(p, p, pa, cq, sp)


def _gdr_masks():
    r = lax.broadcasted_iota(jnp.int32, (CK, CK), 0)
    c = lax.broadcasted_iota(jnp.int32, (CK, CK), 1)
    return r >= c, r > c


def _head_cols(gbt, h):
    return gbt[:, h:h + 1], gbt[:, H + h:H + h + 1]


def _split(x, parts):
    out = []
    for _ in range(parts):
        hi = x.astype(jnp.bfloat16)
        out.append(hi)
        x = x - hi.astype(F32)
    return out


def _dot_f32(a, b, dims, exact=None):
    if exact == "a":
        ab = a.astype(jnp.bfloat16)
        return sum(_dot(ab, t, dims) for t in _split(b, 3))
    if exact == "b":
        bb = b.astype(jnp.bfloat16)
        return sum(_dot(t, bb, dims) for t in _split(a, 3))
    ah, al = _split(a, 2)
    bh, bl = _split(b, 2)
    return _dot(ah, bh, dims) + _dot(ah, bl, dims) + _dot(al, bh, dims)


def _gdr_consts():
    causal, strict = _gdr_masks()
    return dict(causal=causal, strict=strict, tril=jnp.where(causal, 1.0, 0.0).astype(F32),
                eye=jnp.where(causal & jnp.logical_not(strict), 1.0, 0.0).astype(F32),
                bcast=jnp.full((CK, HD), 1.0 / HD, F32))


def _dots(a, b, dims):
    return [_dot(x, y, dims) for x, y in zip(a, b)]


def _dots_f32(a, b, dims, exact=None):
    n = len(a)
    if exact == "a":
        lhs = [[x.astype(jnp.bfloat16)] * 3 for x in a]
        rhs = [_split(y, 3) for y in b]
    elif exact == "b":
        lhs = [_split(x, 3) for x in a]
        rhs = [[y.astype(jnp.bfloat16)] * 3 for y in b]
    else:
        sa = [_split(x, 2) for x in a]
        sb = [_split(y, 2) for y in b]
        lhs = [[s[0], s[0], s[1]] for s in sa]
        rhs = [[s[0], s[1], s[0]] for s in sb]
    terms = [[_dot(lhs[i][t], rhs[i][t], dims) for i in range(n)] for t in range(3)]
    return [terms[0][i] + terms[1][i] + terms[2][i] for i in range(n)]


def _gdr_local(q, k, v, beta, g, cst, tinv=None):
    n = len(q)
    R = range(n)
    causal, strict = cst["causal"], cst["strict"]
    gc = _dots_f32([cst["tril"]] * n, [jnp.broadcast_to(g[i], (CK, HD)) for i in R], NN, exact="a")
    g_row = _dots_f32([cst["bcast"]] * n, gc, NT, exact="a")
    decay = [jnp.where(causal, jnp.exp(jnp.where(causal, gc[i][:, 0:CK] - g_row[i], 0.0)), 0.0) for i in R]
    eg = [jnp.exp(gc[i]) for i in R]
    gl = [gc[i][CK - 1:CK, :] for i in R]
    ek = [jnp.exp(gl[i] - gc[i]) for i in R]
    cd = [jnp.exp(gl[i]) for i in R]
    kb = [k[i] * beta[i] for i in R]
    pk = _dots(kb, k, NT)
    if tinv is None:
        xp = [-jnp.where(strict, pk[i] * decay[i], 0.0) for i in R]
        tinv = [cst["eye"] + xp[i] for i in R]
        for _ in range(5):
            xp = _dots_f32(xp, xp, NN)
            tx = _dots_f32(tinv, xp, NN)
            tinv = [tinv[i] + tx[i] for i in R]
    u = _dots(tinv, [v[i] * beta[i] for i in R], NN)
    w = _dots(tinv, [kb[i] * eg[i] for i in R], NN)
    qk = _dots(q, k, NT)
    intra = [jnp.where(causal, qk[i] * decay[i], 0.0) for i in R]
    return dict(decay=decay, eg=eg, ek=ek, cd=cd, kb=kb, pk=pk, tinv=tinv, u=u, w=w, qk=qk, intra=intra,
                q_dec=[q[i] * eg[i] for i in R], k_dec=[k[i] * ek[i] for i in R])


GDR_SUB = 8


def _gdr_fwd(qn, kn, vs, gb):
    L = qn.shape[0]
    nc = L // CK
    cb = min(8, nc)
    rb = cb * CK
    nb = nc // cb
    nsub = GDR_SUB if cb % GDR_SUB == 0 else 1

    def body(q_ref, k_ref, v_ref, gb_ref, o_ref, st_ref, ti_ref, s_ref):
        @pl.when(pl.program_id(0) == 0)
        def _():
            s_ref[...] = jnp.zeros_like(s_ref)

        cst = _gdr_consts()
        heads = range(H)

        def group(gi, carry):
            rows = [pl.ds(pl.multiple_of((gi * nsub + j) * CK, CK), CK) for j in range(nsub)]
            chains = [(j, h) for j in range(nsub) for h in heads]
            gbt = [gb_ref[rows[j], :] for j in range(nsub)]
            cols = lambda h: slice(h * HD, (h + 1) * HD)
            t = _gdr_local([q_ref[rows[j], cols(h)] for j, h in chains], [k_ref[rows[j], cols(h)] for j, h in chains],
                           [v_ref[rows[j], cols(h)] for j, h in chains],
                           [_head_cols(gbt[j], h)[0] for j, h in chains], [_head_cols(gbt[j], h)[1] for j, h in chains], cst)
            s = [s_ref[h] for h in heads]
            for j in range(nsub):
                at = lambda key: [t[key][j * H + h] for h in heads]
                for h in heads:
                    st_ref[h, gi * nsub + j] = s[h]
                    ti_ref[h, gi * nsub + j] = t["tinv"][j * H + h]
                ws = _dots(at("w"), s, NN)
                v_new = [u_h - ws_h for u_h, ws_h in zip(at("u"), ws)]
                o_s = _dots(at("q_dec"), s, NN)
                o_v = _dots(at("intra"), v_new, NN)
                kv = _dots(at("k_dec"), v_new, TN)
                cd = at("cd")
                for h in heads:
                    o_ref[rows[j], cols(h)] = o_s[h] + o_v[h]
                s = [s[h] * cd[h] + kv[h] for h in heads]
            for h in heads:
                s_ref[h] = s[h]
            return carry

        lax.fori_loop(0, cb // nsub, group, 0)

    blk = pl.BlockSpec((rb, H * HD), lambda b: (b, 0))
    return pl.pallas_call(
        body, name="gdr_fwd", grid=(nb,),
        in_specs=[blk, blk, blk, pl.BlockSpec((rb, LANES), lambda b: (b, 0))],
        out_specs=[blk, pl.BlockSpec((H, cb, HD, HD), lambda b: (0, b, 0, 0)),
                   pl.BlockSpec((H, cb, CK, CK), lambda b: (0, b, 0, 0))],
        out_shape=[jax.ShapeDtypeStruct((L, H * HD), F32), jax.ShapeDtypeStruct((H, nc, HD, HD), F32),
                   jax.ShapeDtypeStruct((H, nc, CK, CK), F32)],
        scratch_shapes=[pltpu.VMEM((H, HD, HD), F32)],
        compiler_params=_params(1),
    )(qn, kn, vs, gb)


def _gdr_bwd(qn, kn, vs, gb, states, tinvs, do):
    L = qn.shape[0]
    nc = L // CK
    cb = min(8, nc)
    rb = cb * CK
    nb = nc // cb
    nsub = GDR_SUB if cb % GDR_SUB == 0 else 1

    def body(q_ref, k_ref, v_ref, gb_ref, st_ref, ti_ref, do_ref, dq_ref, dk_ref, dv_ref, dgb_ref, ds_ref):
        @pl.when(pl.program_id(0) == 0)
        def _():
            ds_ref[...] = jnp.zeros_like(ds_ref)

        cst = _gdr_consts()
        causal, strict = cst["causal"], cst["strict"]
        ones = jnp.ones((CK, HD), F32)
        row = lax.broadcasted_iota(jnp.int32, (CK, HD), 0)
        lane = lax.broadcasted_iota(jnp.int32, (CK, LANES), 1)

        heads = range(H)
        rsum = lambda x: jnp.sum(x, axis=-1, keepdims=True)

        def group(gj, carry):
            gi = cb // nsub - 1 - gj
            rows = [pl.ds(pl.multiple_of((gi * nsub + j) * CK, CK), CK) for j in range(nsub)]
            chains = [(j, h) for j in range(nsub) for h in heads]
            gbt = [gb_ref[rows[j], :] for j in range(nsub)]
            cols = lambda h: slice(h * HD, (h + 1) * HD)
            q_all = [q_ref[rows[j], cols(h)] for j, h in chains]
            k_all = [k_ref[rows[j], cols(h)] for j, h in chains]
            v_all = [v_ref[rows[j], cols(h)] for j, h in chains]
            beta_all = [_head_cols(gbt[j], h)[0] for j, h in chains]
            t = _gdr_local(q_all, k_all, v_all, beta_all, [_head_cols(gbt[j], h)[1] for j, h in chains], cst,
                           tinv=[ti_ref[h, gi * nsub + j] for j, h in chains])
            ds_out = [ds_ref[h] for h in heads]
            for j in reversed(range(nsub)):
                at = lambda key: [t[key][j * H + h] for h in heads]
                pick = lambda lst: [lst[j * H + h] for h in heads]
                q, k, v, beta = pick(q_all), pick(k_all), pick(v_all), pick(beta_all)
                u, w, tinv, decay = at("u"), at("w"), at("tinv"), at("decay")
                eg, ek, cd, kb = at("eg"), at("ek"), at("cd"), at("kb")
                q_dec, k_dec, intra, pk, qk = at("q_dec"), at("k_dec"), at("intra"), at("pk"), at("qk")
                s = [st_ref[h, gi * nsub + j] for h in heads]
                dout = [do_ref[rows[j], cols(h)] for h in heads]

                ws = _dots(w, s, NN)
                v_new = [u[h] - ws[h] for h in heads]
                dq_dec = _dots(dout, s, NT)
                qd = _dots(q_dec, dout, TN)
                di = _dots(dout, v_new, NT)
                dintra = [jnp.where(causal, di[h], 0.0) for h in heads]
                ido = _dots(intra, dout, TN)
                kds = _dots(k_dec, ds_out, NN)
                dv_new = [ido[h] + kds[h] for h in heads]
                dk_dec = _dots(v_new, ds_out, NT)
                dcd = [jnp.sum(jnp.sum(ds_out[h] * s[h], axis=1, keepdims=True), axis=0, keepdims=True) for h in heads]
                dvs = _dots(dv_new, s, NT)
                dw = [-dvs[h] for h in heads]
                wdv = _dots(w, dv_new, TN)
                ds_new = [qd[h] + ds_out[h] * cd[h] - wdv[h] for h in heads]
                dru = _dots(tinv, dv_new, TN)
                drw = _dots(tinv, dw, TN)
                dl1 = _dots(dru, u, NT)
                dl2 = _dots(drw, w, NT)
                dlower = [-jnp.where(strict, dl1[h] + dl2[h], 0.0) for h in heads]
                dv = [dru[h] * beta[h] for h in heads]
                dbeta = [rsum(dru[h] * v[h]) for h in heads]
                dgc = [rsum(drw[h] * kb[h]) * eg[h] for h in heads]
                dpk = [dlower[h] * decay[h] for h in heads]
                dqk = [dintra[h] * decay[h] for h in heads]
                dpk_k = _dots(dpk, k, NN)
                dkb = [drw[h] * eg[h] + dpk_k[h] for h in heads]
                dk1 = _dots(dpk, kb, TN)
                dq1 = _dots(dqk, k, NN)
                dk2 = _dots(dqk, q, TN)
                m = [(dlower[h] * pk[h] + dintra[h] * qk[h]) * decay[h] for h in heads]
                mcol = _dots_f32(m, [ones] * H, TN, exact="b")
                e = [rsum(dk_dec[h] * k_dec[h]) for h in heads]
                dgl = [jnp.sum(e[h], axis=0, keepdims=True) + dcd[h] * cd[h] for h in heads]
                dgc = [dgc[h] + rsum(m[h]) - mcol[h] + rsum(dq_dec[h] * q_dec[h]) - e[h]
                       + jnp.where(row == CK - 1, dgl[h], 0.0) for h in heads]
                dg = _dots_f32([cst["tril"]] * H, dgc, TN, exact="a")
                dgb = jnp.zeros((CK, LANES), F32)
                for h in heads:
                    dq_ref[rows[j], cols(h)] = dq1[h] + dq_dec[h] * eg[h]
                    dk_ref[rows[j], cols(h)] = dk1[h] + dk2[h] + dk_dec[h] * ek[h] + dkb[h] * beta[h]
                    dv_ref[rows[j], cols(h)] = dv[h]
                    db = dbeta[h] + rsum(dkb[h] * k[h])
                    dgb = dgb + jnp.where(lane == h, db, 0.0) + jnp.where(lane == H + h, dg[h], 0.0)
                dgb_ref[rows[j], :] = dgb
                ds_out = ds_new
            for h in heads:
                ds_ref[h] = ds_out[h]
            return carry

        lax.fori_loop(0, cb // nsub, group, 0)

    blk = pl.BlockSpec((rb, H * HD), lambda b: (nb - 1 - b, 0))
    sblk = pl.BlockSpec((rb, LANES), lambda b: (nb - 1 - b, 0))
    return pl.pallas_call(
        body, name="gdr_bwd", grid=(nb,),
        in_specs=[blk, blk, blk, sblk, pl.BlockSpec((H, cb, HD, HD), lambda b: (0, nb - 1 - b, 0, 0)),
                  pl.BlockSpec((H, cb, CK, CK), lambda b: (0, nb - 1 - b, 0, 0)), blk],
        out_specs=[blk, blk, blk, sblk],
        out_shape=[jax.ShapeDtypeStruct((L, H * HD), F32)] * 3 + [jax.ShapeDtypeStruct((L, LANES), F32)],
        scratch_shapes=[pltpu.VMEM((H, HD, HD), F32)],
        compiler_params=_params(1),
    )(qn, kn, vs, gb, states, tinvs, do)


def _post_fwd(o, p, ya, x, modrows, sp, w_out):
    L = x.shape[0]
    T = _tile(L, 256)

    def body(o_ref, z_ref, ya_ref, x_ref, mod_ref, sp_ref, w_ref, y_ref, x2_ref, yb_ref):
        ndw = sp_ref[2:3, :]
        z = z_ref[...]
        sz = z * _sig(z)
        parts = []
        for h in range(H):
            n, _ = _rms(o_ref[:, h * HD:(h + 1) * HD])
            parts.append(n * ndw * sz[:, h * HD:(h + 1) * HD])
        yb = jnp.concatenate(parts, axis=-1).astype(MXU)
        yb_ref[...] = yb
        y = _dot(ya_ref[...], w_ref[0:AW, :], NN) + _dot(yb, w_ref[AW:2 * AW, :], NN)
        y_ref[...] = y
        x2_ref[...] = x_ref[...] + mod_ref[2:3, :] * y

    row = lambda i: (i, 0)
    zcol = (3 * AW + 3 * H * HD) // (H * HD)
    return pl.pallas_call(
        body, name="post_fwd", grid=(L // T,),
        in_specs=[pl.BlockSpec((T, H * HD), row), pl.BlockSpec((T, H * HD), lambda i: (i, zcol)),
                  pl.BlockSpec((T, AW), row), pl.BlockSpec((T, D), row), _full((SUB, D)), _full((SUB, LANES)),
                  _full((D, D))],
        out_specs=[pl.BlockSpec((T, D), row), pl.BlockSpec((T, D), row), pl.BlockSpec((T, H * HD), row)],
        out_shape=[jax.ShapeDtypeStruct((L, D), F32), jax.ShapeDtypeStruct((L, D), F32),
                   jax.ShapeDtypeStruct((L, H * HD), MXU)],
        compiler_params=_params(1),
    )(o, p, ya, x, modrows, sp, w_out)


FF_COLS = 2
FF_CW = DFF // FF_COLS
FF_ROWS = 256


def _ffn_fwd_half(x2, modrows, vec, w_up, cff, w_down, j, d_prev):
    assert FF_COLS == 2
    L = x2.shape[0]
    T = _tile(L, FF_ROWS)
    nj = FF_COLS
    last = d_prev is not None

    def body(*refs):
        x_ref, mod_ref, vec_ref, wg_ref, wu_ref, cg_ref, cu_ref, wd_ref = refs[:8]
        if last:
            dp_ref, gp_ref, up_ref, gc_ref, uc_ref, f_ref, d_ref, x3_ref, carry_g, carry_u = refs[8:]
        else:
            h_ref, gp_ref, up_ref, gc_ref, uc_ref, f_ref, d_ref, carry_g, carry_u = refs[8:]

        @pl.when(pl.program_id(0) == 0)
        def _():
            carry_g[...] = jnp.zeros_like(carry_g)
            carry_u[...] = jnp.zeros_like(carry_u)

        xv = x_ref[...]
        n, _ = _rms(xv)
        hb = (n * vec_ref[1:2, :] * (1.0 + mod_ref[4:5, :]) + mod_ref[3:4, :]).astype(MXU)
        if not last:
            h_ref[...] = hb
        g = _dot(hb, wg_ref[...], NN)
        u = _dot(hb, wu_ref[...], NN)
        gp_ref[...] = g
        up_ref[...] = u
        gc, _ = _conv_fwd(g, cg_ref, 3, carry_g[...])
        uc, _ = _conv_fwd(u, cu_ref, 3, carry_u[...])
        carry_g[...] = g[T - SUB:T, :]
        carry_u[...] = u[T - SUB:T, :]
        gc_ref[...] = gc.astype(MXU)
        uc_ref[...] = uc.astype(MXU)
        fb = (gc * _sig(gc) * uc).astype(MXU)
        f_ref[...] = fb
        part = _dot(fb, wd_ref[...], NN)
        if last:
            dv = dp_ref[...] + part
            d_ref[...] = dv
            x3_ref[...] = xv + mod_ref[5:6, :] * dv
        else:
            d_ref[...] = part

    row = lambda i: (i, 0)
    rowD = pl.BlockSpec((T, D), row)
    rowC = pl.BlockSpec((T, FF_CW), row)
    in_specs = [rowD, _full((SUB, D)), _full((SUB, D)),
                pl.BlockSpec((D, FF_CW), lambda i: (0, j)), pl.BlockSpec((D, FF_CW), lambda i: (0, nj + j)),
                pl.BlockSpec((SUB, FF_CW), lambda i: (0, j)), pl.BlockSpec((SUB, FF_CW), lambda i: (0, nj + j)),
                pl.BlockSpec((FF_CW, D), lambda i: (j, 0))]
    half = [jax.ShapeDtypeStruct((L, FF_CW), F32), jax.ShapeDtypeStruct((L, FF_CW), F32),
            jax.ShapeDtypeStruct((L, FF_CW), MXU), jax.ShapeDtypeStruct((L, FF_CW), MXU),
            jax.ShapeDtypeStruct((L, FF_CW), MXU)]
    args = [x2, modrows, vec, w_up, w_up, cff, cff, w_down]
    if last:
        in_specs.append(rowD)
        args.append(d_prev)
        out_specs = [rowC] * 5 + [rowD, rowD]
        out_shape = half + [jax.ShapeDtypeStruct((L, D), F32), jax.ShapeDtypeStruct((L, D), F32)]
    else:
        out_specs = [rowD] + [rowC] * 5 + [rowD]
        out_shape = [jax.ShapeDtypeStruct((L, D), MXU)] + half + [jax.ShapeDtypeStruct((L, D), F32)]
    return pl.pallas_call(
        body, name="ffn_fwd_last" if last else "ffn_fwd_first", grid=(L // T,),
        in_specs=in_specs, out_specs=out_specs, out_shape=out_shape,
        scratch_shapes=[pltpu.VMEM((SUB, FF_CW), F32), pltpu.VMEM((SUB, FF_CW), F32)],
        compiler_params=_params(1),
    )(*args)


def _ffn_bwd_half(dx3, modrows, gpre, upre, gcv, ucv, cff, w_down, w_up, j, tail):
    assert FF_COLS == 2
    L = dx3.shape[0]
    T = _tile(L, FF_ROWS)
    ni, nj = L // T, FF_COLS
    last = tail is not None

    def body(*refs):
        dx3_ref, mod_ref, gp_ref, up_ref, gc_ref, uc_ref, cg_ref, cu_ref, wd_ref, wg_ref, wu_ref = refs[:11]
        if last:
            (d_ref, x2_ref, vec_ref, dhp_ref, dgp_ref, dup_ref, dx2_ref, accv_ref, dcg_ref, dcu_ref,
             carry_g, carry_u) = refs[11:]
        else:
            dd_ref, dgp_ref, dup_ref, dh_ref, dcg_ref, dcu_ref, carry_g, carry_u = refs[11:]
        i = pl.program_id(0)

        @pl.when(i == 0)
        def _():
            carry_g[...] = jnp.zeros_like(carry_g)
            carry_u[...] = jnp.zeros_like(carry_u)
            dcg_ref[...] = jnp.zeros_like(dcg_ref)
            dcu_ref[...] = jnp.zeros_like(dcu_ref)
            if last:
                accv_ref[...] = jnp.zeros_like(accv_ref)

        dx3v = dx3_ref[...]
        ddb = (mod_ref[5:6, :] * dx3v).astype(MXU)
        if not last:
            dd_ref[...] = ddb
        g, u = gp_ref[...], up_ref[...]
        gc, uc = gc_ref[...].astype(F32), uc_ref[...].astype(F32)
        sg = _sig(gc)
        df = _dot(ddb, wd_ref[...], NT)
        duc = df * (gc * sg)
        dgc = df * uc * (sg * (1.0 + gc * (1.0 - sg)))
        dgs = [dgc] + [_shift_up(dgc, s, carry_g[...]) for s in (1, 2)]
        dus = [duc] + [_shift_up(duc, s, carry_u[...]) for s in (1, 2)]
        for s in range(3):
            dcg_ref[2 - s:3 - s, :] += _sum0(dgs[s] * g)
            dcu_ref[2 - s:3 - s, :] += _sum0(dus[s] * u)
        dg = (cg_ref[2:3, :] * dgs[0] + cg_ref[1:2, :] * dgs[1] + cg_ref[0:1, :] * dgs[2]).astype(MXU)
        du = (cu_ref[2:3, :] * dus[0] + cu_ref[1:2, :] * dus[1] + cu_ref[0:1, :] * dus[2]).astype(MXU)
        carry_g[...] = dgc[0:SUB, :]
        carry_u[...] = duc[0:SUB, :]
        dgp_ref[...] = dg
        dup_ref[...] = du
        dh = _dot(dg, wg_ref[...], NT) + _dot(du, wu_ref[...], NT)
        if last:
            dh = dh + dhp_ref[...]
            accv_ref[0:1, :] += _sum0(dx3v * d_ref[...])
            n, r = _rms(x2_ref[...])
            nw, sc = vec_ref[1:2, :], mod_ref[4:5, :]
            accv_ref[1:2, :] += _sum0(dh)
            accv_ref[2:3, :] += _sum0(dh * n * nw)
            accv_ref[3:4, :] += _sum0(dh * n * (1.0 + sc))
            dx2_ref[...] = _rms_bwd(dh * nw * (1.0 + sc), n, r) + dx3v
        else:
            dh_ref[...] = dh

    row = lambda i: (ni - 1 - i, 0)
    rowD = pl.BlockSpec((T, D), row)
    rowC = pl.BlockSpec((T, FF_CW), row)
    in_specs = [rowD, _full((SUB, D)), rowC, rowC, rowC, rowC,
                pl.BlockSpec((SUB, FF_CW), lambda i: (0, j)), pl.BlockSpec((SUB, FF_CW), lambda i: (0, nj + j)),
                pl.BlockSpec((FF_CW, D), lambda i: (j, 0)),
                pl.BlockSpec((D, FF_CW), lambda i: (0, j)), pl.BlockSpec((D, FF_CW), lambda i: (0, nj + j))]
    args = [dx3, modrows, gpre, upre, gcv, ucv, cff, cff, w_down, w_up, w_up]
    halfb = [jax.ShapeDtypeStruct((L, FF_CW), MXU), jax.ShapeDtypeStruct((L, FF_CW), MXU)]
    dconv = [jax.ShapeDtypeStruct((SUB, FF_CW), F32)] * 2
    if last:
        d, x2, vec, dh_prev = tail
        in_specs += [rowD, rowD, _full((SUB, D)), rowD]
        args += [d, x2, vec, dh_prev]
        out_specs = [rowC, rowC, rowD, _full((SUB, D)), _full((SUB, FF_CW)), _full((SUB, FF_CW))]
        out_shape = halfb + [jax.ShapeDtypeStruct((L, D), F32), jax.ShapeDtypeStruct((SUB, D), F32)] + dconv
    else:
        out_specs = [rowD, rowC, rowC, rowD, _full((SUB, FF_CW)), _full((SUB, FF_CW))]
        out_shape = [jax.ShapeDtypeStruct((L, D), MXU)] + halfb + [jax.ShapeDtypeStruct((L, D), F32)] + dconv
    return pl.pallas_call(
        body, name="ffn_bwd_last" if last else "ffn_bwd_first", grid=(ni,),
        in_specs=in_specs, out_specs=out_specs, out_shape=out_shape,
        scratch_shapes=[pltpu.VMEM((SUB, FF_CW), F32), pltpu.VMEM((SUB, FF_CW), F32)],
        compiler_params=_params(1),
    )(*args)


def _final(x, target, nf):
    L = x.shape[0]
    T = _tile(L, 256)

    def body(x_ref, t_ref, nf_ref, dx_ref, acc_ref):
        @pl.when(pl.program_id(0) == 0)
        def _():
            acc_ref[...] = jnp.zeros_like(acc_ref)

        n, r = _rms(x_ref[...])
        w = nf_ref[0:1, :]
        err = n * w - t_ref[...]
        acc_ref[0:1, :] += (0.5 / D) * _sum0(err * err)
        dy = err * (1.0 / D)
        acc_ref[1:2, :] += _sum0(dy * n)
        dx_ref[...] = _rms_bwd(dy * w, n, r)

    row = lambda i: (i, 0)
    return pl.pallas_call(
        body, name="final_norm_loss", grid=(L // T,),
        in_specs=[pl.BlockSpec((T, D), row), pl.BlockSpec((T, D), row), _full((SUB, D))],
        out_specs=[pl.BlockSpec((T, D), row), _full((SUB, D))],
        out_shape=[jax.ShapeDtypeStruct((L, D), F32), jax.ShapeDtypeStruct((SUB, D), F32)],
        compiler_params=_params(1),
    )(x, target, nf)


def _post_bwd(dx2, y, o, p, modrows, sp, w_out):
    L = dx2.shape[0]
    T = _tile(L, 256)

    def body(dx2_ref, y_ref, o_ref, z_ref, mod_ref, sp_ref, w_ref, dy_ref, do_ref, dz_ref, dya_ref, accv_ref, accs_ref):
        @pl.when(pl.program_id(0) == 0)
        def _():
            accv_ref[...] = jnp.zeros_like(accv_ref)
            accs_ref[...] = jnp.zeros_like(accs_ref)

        dx2v = dx2_ref[...]
        accv_ref[0:1, :] += _sum0(dx2v * y_ref[...])
        dyb = (mod_ref[2:3, :] * dx2v).astype(MXU)
        dy_ref[...] = dyb
        dyc = _dot(dyb, w_ref[...], NT)
        dya_ref[...] = dyc[:, 0:AW]
        ndw = sp_ref[2:3, :]
        z = z_ref[...]
        sgz = _sig(z)
        dsz = sgz * (1.0 + z * (1.0 - sgz))
        dndw = jnp.zeros((1, HD), F32)
        for h in range(H):
            sl = slice(h * HD, (h + 1) * HD)
            n, r = _rms(o_ref[:, sl])
            dyh = dyc[:, AW + h * HD:AW + (h + 1) * HD]
            zh = z[:, sl]
            don = dyh * (zh * sgz[:, sl])
            dz_ref[:, sl] = dyh * (n * ndw) * dsz[:, sl]
            dndw = dndw + _sum0(don * n)
            do_ref[:, sl] = _rms_bwd(don * ndw, n, r)
        accs_ref[0:1, :] += dndw

    row = lambda i: (i, 0)
    zcol = (3 * AW + 3 * H * HD) // (H * HD)
    return pl.pallas_call(
        body, name="post_bwd", grid=(L // T,),
        in_specs=[pl.BlockSpec((T, D), row), pl.BlockSpec((T, D), row), pl.BlockSpec((T, H * HD), row),
                  pl.BlockSpec((T, H * HD), lambda i: (i, zcol)), _full((SUB, D)), _full((SUB, LANES)), _full((D, D))],
        out_specs=[pl.BlockSpec((T, D), row)] + [pl.BlockSpec((T, H * HD), row)] * 3 + [_full((SUB, D)), _full((SUB, LANES))],
        out_shape=[jax.ShapeDtypeStruct((L, D), MXU)] + [jax.ShapeDtypeStruct((L, H * HD), F32)] * 3
        + [jax.ShapeDtypeStruct((SUB, D), F32), jax.ShapeDtypeStruct((SUB, LANES), F32)],
        compiler_params=_params(1),
    )(dx2, y, o, p, modrows, sp, w_out)


def _pre_bwd(p, cub, qcb, dqn, dkn, dvs, dya, dz, dgb, pa, cq, sp):
    L = p.shape[0]
    T = _tile(L, 256)
    ni = L // T
    scale = HD ** -0.5
    w3 = 3 * AW + 3 * H * HD

    def body(pm_ref, cu_ref, qc_ref, ps_ref, dq_ref, dk_ref, dv_ref, dya_ref, dz_ref, dgb_ref, pa_ref, cq_ref, sp_ref,
             dp_ref, dpa_ref, dcq_ref, dsp_ref, carry_u, carry_q):
        i = pl.program_id(0)

        @pl.when(i == 0)
        def _():
            dpa_ref[...] = jnp.zeros_like(dpa_ref)
            dcq_ref[...] = jnp.zeros_like(dcq_ref)
            dsp_ref[...] = jnp.zeros_like(dsp_ref)
            carry_u[...] = jnp.zeros_like(carry_u)
            carry_q[...] = jnp.zeros_like(carry_q)

        a_b, a_c, a_x = pm_ref[:, 0:AW], pm_ref[:, AW:2 * AW], pm_ref[:, 2 * AW:3 * AW]
        u = a_c * a_x
        cu = cu_ref[...].astype(F32)
        yp = a_b * cu
        bd = _blockdiag_mean(AW, A_GROUP)
        ra = lax.rsqrt(_dot_f32(yp * yp, bd, NN, exact="b") + EPS)
        na = yp * ra
        dya = dya_ref[...]
        dpa_ref[3:4, :] += _sum0(dya * na)
        dna = dya * pa_ref[3:4, :]
        dyp = ra * (dna - na * _dot_f32(dna * na, bd, NN, exact="b"))
        dcu = dyp * a_b
        dcs = [dcu] + [_shift_up(dcu, s, carry_u[...]) for s in (1, 2)]
        du = pa_ref[2:3, :] * dcs[0]
        for s in range(3):
            dpa_ref[2 - s:3 - s, :] += _sum0(dcs[s] * u)
            if s:
                du = du + pa_ref[2 - s:3 - s, :] * dcs[s]
        carry_u[...] = dcu[0:SUB, :]
        dp_ref[:, 0:AW] = (dyp * cu).astype(MXU)
        dp_ref[:, AW:2 * AW] = (du * a_x).astype(MXU)
        dp_ref[:, 2 * AW:3 * AW] = (du * a_c).astype(MXU)

        qkv = pm_ref[:, 3 * AW:w3]
        qc = qc_ref[...].astype(F32)
        sg = _sig(qc)
        qs = qc * sg
        parts = []
        for h in range(H):
            q = qs[:, h * HD:(h + 1) * HD]
            rq = lax.rsqrt(jnp.sum(q * q, axis=-1, keepdims=True) + EPS)
            parts.append(_l2_bwd(dq_ref[:, h * HD:(h + 1) * HD] * scale, q * rq, rq))
        for h in range(H):
            k = qs[:, (H + h) * HD:(H + h + 1) * HD]
            rk = lax.rsqrt(jnp.sum(k * k, axis=-1, keepdims=True) + EPS)
            parts.append(_l2_bwd(dk_ref[:, h * HD:(h + 1) * HD], k * rk, rk))
        parts.append(dv_ref[...])
        dqc = jnp.concatenate(parts, axis=-1) * (sg * (1.0 + qc * (1.0 - sg)))
        dqs = [dqc] + [_shift_up(dqc, s, carry_q[...]) for s in (1, 2, 3)]
        dqkv = cq_ref[3:4, :] * dqs[0]
        for s in range(4):
            dcq_ref[3 - s:4 - s, :] += _sum0(dqs[s] * qkv)
            if s:
                dqkv = dqkv + cq_ref[3 - s:4 - s, :] * dqs[s]
        dp_ref[:, 3 * AW:w3] = dqkv.astype(MXU)
        carry_q[...] = dqc[0:SUB, :]
        dp_ref[:, w3:w3 + H * HD] = dz_ref[...].astype(MXU)

        lane, a, xb, beta, g = _gate_small(ps_ref[...], sp_ref)
        dgb = dgb_ref[...]
        dbeta = jnp.where(lane < H, dgb, 0.0)
        dg = jnp.where((lane >= H) & (lane < 2 * H), dgb, 0.0)
        dalpha = dg * a * _sig(xb)
        dsp_ref[0:1, :] += _sum0(dg * g)
        dsp_ref[1:2, :] += _sum0(dalpha)
        dp_ref[:, w3 + H * HD:P_PAD] = (dbeta * beta * (1.0 - beta) + dalpha).astype(MXU)

    row = lambda i: (ni - 1 - i, 0)
    hrow = pl.BlockSpec((T, H * HD), row)
    return pl.pallas_call(
        body, name="pre_bwd", grid=(ni,),
        in_specs=[pl.BlockSpec((T, w3), row), pl.BlockSpec((T, AW), row), pl.BlockSpec((T, 3 * H * HD), row),
                  pl.BlockSpec((T, LANES), lambda i: (ni - 1 - i, (P_PAD - LANES) // LANES)),
                  hrow, hrow, hrow, pl.BlockSpec((T, AW), row), hrow,
                  pl.BlockSpec((T, LANES), row),
                  _full((SUB, AW)), _full((SUB, 3 * H * HD)), _full((SUB, LANES))],
        out_specs=[pl.BlockSpec((T, P_PAD), row), _full((SUB, AW)), _full((SUB, 3 * H * HD)), _full((SUB, LANES))],
        out_shape=[jax.ShapeDtypeStruct((L, P_PAD), MXU), jax.ShapeDtypeStruct((SUB, AW), F32),
                   jax.ShapeDtypeStruct((SUB, 3 * H * HD), F32), jax.ShapeDtypeStruct((SUB, LANES), F32)],
        scratch_shapes=[pltpu.VMEM((SUB, AW), F32), pltpu.VMEM((SUB, 3 * H * HD), F32)],
        compiler_params=_params(1),
    )(p, cub, qcb, p, dqn, dkn, dvs, dya, dz, dgb, pa, cq, sp)


def _in_bwd(dp, w_in, x, dx2, modrows, vec):
    L = x.shape[0]
    T = _tile(L, 512)

    def body(dp_ref, w_ref, x_ref, dx2_ref, mod_ref, vec_ref, dx_ref, accv_ref):
        @pl.when(pl.program_id(0) == 0)
        def _():
            accv_ref[...] = jnp.zeros_like(accv_ref)

        dh = _dot(dp_ref[...], w_ref[...], NT)
        n, r = _rms(x_ref[...])
        nw, sc = vec_ref[0:1, :], mod_ref[1:2, :]
        accv_ref[0:1, :] += _sum0(dh)
        accv_ref[1:2, :] += _sum0(dh * n * nw)
        accv_ref[2:3, :] += _sum0(dh * n * (1.0 + sc))
        dx_ref[...] = _rms_bwd(dh * nw * (1.0 + sc), n, r) + dx2_ref[...]

    row = lambda i: (i, 0)
    return pl.pallas_call(
        body, name="in_bwd", grid=(L // T,),
        in_specs=[pl.BlockSpec((T, P_PAD), row), _full((D, P_PAD)), pl.BlockSpec((T, D), row),
                  pl.BlockSpec((T, D), row), _full((SUB, D)), _full((SUB, D))],
        out_specs=[pl.BlockSpec((T, D), row), _full((SUB, D))],
        out_shape=[jax.ShapeDtypeStruct((L, D), F32), jax.ShapeDtypeStruct((SUB, D), F32)],
        compiler_params=_params(1),
    )(dp, w_in, x, dx2, modrows, vec)


def _wgrad(a, b, tm, tn, name):
    L, m = a.shape
    n = b.shape[1]
    tl = _tile(L, 512)
    tm, tn = _tile(m, tm), _tile(n, tn)
    nl = L // tl

    def body(a_ref, b_ref, o_ref, acc):
        @pl.when(pl.program_id(2) == 0)
        def _():
            acc[...] = jnp.zeros_like(acc)

        acc[...] += _dot(a_ref[...], b_ref[...], TN)

        @pl.when(pl.program_id(2) == nl - 1)
        def _():
            o_ref[...] = acc[...].astype(o_ref.dtype)

    return pl.pallas_call(
        body, name=name, grid=(m // tm, n // tn, nl),
        in_specs=[pl.BlockSpec((tl, tm), lambda i, j, l: (l, i)), pl.BlockSpec((tl, tn), lambda i, j, l: (l, j))],
        out_specs=pl.BlockSpec((tm, tn), lambda i, j, l: (i, j)),
        out_shape=jax.ShapeDtypeStruct((m, n), MXU), scratch_shapes=[pltpu.VMEM((tm, tn), F32)],
        compiler_params=_params(3),
    )(a, b)


def _wgrad_cols(a, b, tm, n_shard, wpad, count, name):
    L, m = a.shape
    n = b.shape[1]
    tl = _tile(L, 512)
    tm = _tile(m, tm)
    nl = L // tl
    wins = _shard_windows(n_shard, count)
    assert all(a_ * LANES + win <= n for a_, _, win in wins), (wins, n)

    def body(a_ref, b_ref, o_ref, acc):
        @pl.when(pl.program_id(1) == 0)
        def _():
            acc[...] = jnp.zeros_like(acc)

        acc[...] += _dot(a_ref[...], b_ref[...], TN)

        @pl.when(pl.program_id(1) == nl - 1)
        def _():
            for k, (a_, s, win) in enumerate(wins):
                xk = acc[:, a_ * LANES:a_ * LANES + win]
                if s:
                    xk = pltpu.roll(xk, win - s, 1)
                o_ref[k] = _fit_lanes(xk, wpad).astype(o_ref.dtype)

    return pl.pallas_call(
        body, name=name, grid=(m // tm, nl),
        in_specs=[pl.BlockSpec((tl, tm), lambda i, l: (l, i)), pl.BlockSpec((tl, n), lambda i, l: (l, 0))],
        out_specs=pl.BlockSpec((count, tm, wpad), lambda i, l: (0, i, 0)),
        out_shape=jax.ShapeDtypeStruct((count, m, wpad), MXU),
        scratch_shapes=[pltpu.VMEM((tm, n), F32)],
        compiler_params=_params(2),
    )(a, b)


def _adamw(w, g, m, v, name):
    r, n = w.shape
    tr = _tile(r, 512)
    bc1 = 1.0 - ADAM_B1 ** ADAM_STEP
    bc2 = 1.0 - ADAM_B2 ** ADAM_STEP

    def body(w_ref, g_ref, m_ref, v_ref, d_ref, nm_ref, nv_ref):
        gv = g_ref[...]
        nm = ADAM_B1 * m_ref[...] + (1.0 - ADAM_B1) * gv
        nv = ADAM_B2 * v_ref[...] + (1.0 - ADAM_B2) * (gv * gv)
        nm_ref[...] = nm
        nv_ref[...] = nv
        d_ref[...] = -ADAM_LR * ((nm / bc1) / (jnp.sqrt(nv / bc2) + ADAM_EPS) + ADAM_WD * w_ref[...])

    spec = pl.BlockSpec((tr, n), lambda i: (i, 0))
    return pl.pallas_call(
        body, name=name, grid=(r // tr,), in_specs=[spec] * 4, out_specs=[spec] * 3,
        out_shape=[jax.ShapeDtypeStruct((r, n), F32)] * 3, compiler_params=_params(1),
    )(w, g, m, v)


def _rows8(rows, width):
    out = jnp.zeros((SUB, width), F32)
    for r, vrow in enumerate(rows):
        out = out.at[r, :vrow.shape[0]].set(vrow)
    return out


def _at_lanes(v4, start):
    return jnp.zeros((LANES,), F32).at[start:start + v4.shape[0]].set(v4)


def _pad_rows(flat, mult):
    n = flat.shape[0]
    pad = (-n) % mult
    return jnp.pad(flat, (0, pad)) if pad else flat


IN_PAD = 512
UP_PAD = 768


def _local_fwd_bwd(x, target, mod_full, small_w, full_w, on_grads=None):
    norm1_w, norm2_w, norm_a_w, a_log, dt_bias, norm_dn_w, norm_f_w = small_w
    w_in_f, w_out_f, w_up_f, w_down_f, conv_a_f, conv_q_f, conv_f_f = full_w

    def layer_params(i):
        modrows = jnp.concatenate([mod_full[i], jnp.zeros((SUB - N_MOD, D), F32)], axis=0)
        vec = _rows8([norm1_w[i], norm2_w[i]], D)
        pa = _rows8([conv_a_f[i, 0], conv_a_f[i, 1], conv_a_f[i, 2], norm_a_w[i]], AW)
        cq = _rows8([conv_q_f[i, k] for k in range(4)], 3 * H * HD)
        sp = _rows8([_at_lanes(a_log[i], H), _at_lanes(dt_bias[i], H), norm_dn_w[i]], LANES)
        cff = _rows8([conv_f_f[i, k] for k in range(3)], 2 * DFF)
        return modrows, vec, pa, cq, sp, cff

    saved = []
    xi = x
    for i in range(DEPTH):
        modrows, vec, pa, cq, sp, cff = layer_params(i)
        p, h1 = _in_proj(xi, modrows, vec, w_in_f[i])
        qn, kn, vs, gb, ya, cub, qcb = _pre_fwd(p, pa, cq, sp)
        o, states, tinvs = _gdr_fwd(qn, kn, vs, gb)
        y, x2, yb = _post_fwd(o, p, ya, xi, modrows, sp, w_out_f[i])
        h2, gp0, up0, gc0, uc0, f0, d0 = _ffn_fwd_half(x2, modrows, vec, w_up_f[i], cff, w_down_f[i], 0, None)
        gp1, up1, gc1, uc1, f1, dff, x3 = _ffn_fwd_half(x2, modrows, vec, w_up_f[i], cff, w_down_f[i], 1, d0)
        saved.append(dict(x=xi, p=p, h1=h1, qn=qn, kn=kn, vs=vs, gb=gb, ya=ya, cub=cub, qcb=qcb, o=o, states=states,
                          tinvs=tinvs, y=y, x2=x2, yb=yb,
                          h2=h2, gpre=(gp0, gp1), upre=(up0, up1), gc=(gc0, gc1), uc=(uc0, uc1), f=(f0, f1), d=dff))
        xi = x3

    dx, facc = _final(xi, target, _rows8([norm_f_w], D))
    loss_local = jnp.sum(facc[0])
    d_norm_f = facc[1]

    gw_in, gw_out, gw_up, gw_down = [None] * DEPTH, [None] * DEPTH, [None] * DEPTH, [None] * DEPTH
    g_small = [None] * DEPTH
    for i in reversed(range(DEPTH)):
        s = saved[i]
        modrows, vec, pa, cq, sp, cff = layer_params(i)
        dd, dgp0, dup0, dh0, dcg0, dcu0 = _ffn_bwd_half(dx, modrows, s["gpre"][0], s["upre"][0], s["gc"][0], s["uc"][0],
                                                        cff, w_down_f[i], w_up_f[i], 0, None)
        dgp1, dup1, dx2, accf, dcg1, dcu1 = _ffn_bwd_half(dx, modrows, s["gpre"][1], s["upre"][1], s["gc"][1], s["uc"][1],
                                                          cff, w_down_f[i], w_up_f[i], 1, (s["d"], s["x2"], vec, dh0))
        n_up, up_pad = 2 * DFF // N_DEV, UP_PAD
        gw_up[i] = jnp.concatenate([_wgrad_cols(s["h2"], t, 1024, n_up, up_pad, FF_CW // n_up, "wgrad_up")
                                    for t in (dgp0, dgp1, dup0, dup1)], axis=0)
        gw_down[i] = jnp.concatenate([_wgrad(s["f"][0], dd, FF_CW, 1024, "wgrad_down"),
                                      _wgrad(s["f"][1], dd, FF_CW, 1024, "wgrad_down")],
                                     axis=0).reshape(N_DEV, DFF // N_DEV, D)
        if on_grads is not None:
            on_grads(i, "ffn", [gw_up[i], gw_down[i]])
        dy, do, dz, dya, accp, accs = _post_bwd(dx2, s["y"], s["o"], s["p"], modrows, sp, w_out_f[i])
        gw_out[i] = jnp.concatenate([_wgrad(s["ya"], dy, 512, 1024, "wgrad_out"),
                                     _wgrad(s["yb"], dy, 512, 1024, "wgrad_out")], axis=0).reshape(N_DEV, D // N_DEV, D)
        dqn, dkn, dvs, dgb = _gdr_bwd(s["qn"], s["kn"], s["vs"], s["gb"], s["states"], s["tinvs"], do)
        dp, dpa, dcq, dsp = _pre_bwd(s["p"], s["cub"], s["qcb"], dqn, dkn, dvs, dya, dz, dgb, pa, cq, sp)
        gw_in[i] = _wgrad_cols(s["h1"], dp, 1024, P_IN // N_DEV, IN_PAD, N_DEV, "wgrad_in")
        dx, acci = _in_bwd(dp, w_in_f[i], s["x"], dx2, modrows, vec)
        dconv_ff = jnp.concatenate([dcg0, dcg1, dcu0, dcu1], axis=1)[0:3]
        dmod = jnp.stack([acci[0], acci[1], accp[0], accf[1], accf[2], accf[0]])
        g_small[i] = dict(norm1=acci[2], norm2=accf[3], norm_a=dpa[3], a_log=dsp[0, H:2 * H], dt_bias=dsp[1, H:2 * H],
                          norm_dn=accs[0], conv_a=dpa[0:3], conv_qkv=dcq[0:4], conv_ff=dconv_ff, dmod=dmod.reshape(-1))
        if on_grads is not None:
            on_grads(i, "mix", [gw_in[i], gw_out[i]])
    return loss_local, dx, gw_in, gw_out, gw_up, gw_down, g_small, d_norm_f


def kernel(x, c, ada_w, ada_b, norm1_w, w_in, conv_a_w, norm_a_w, conv_qkv_w, a_log, dt_bias, norm_dn_w, w_out, norm2_w, w_up, conv_ff_w, w_down, norm_f_w, loss_target, m_ada_w, m_ada_b, m_norm1_w, m_w_in, m_conv_a_w, m_norm_a_w, m_conv_qkv_w, m_a_log, m_dt_bias, m_norm_dn_w, m_w_out, m_norm2_w, m_w_up, m_conv_ff_w, m_w_down, m_norm_f_w, v_ada_w, v_ada_b, v_norm1_w, v_w_in, v_conv_a_w, v_norm_a_w, v_conv_qkv_w, v_a_log, v_dt_bias, v_norm_dn_w, v_w_out, v_norm2_w, v_w_up, v_conv_ff_w, v_w_down, v_norm_f_w):
    ax, ay, ac = lax.axis_index("x"), lax.axis_index("y"), lax.axis_index("c")
    me = 4 * ax + 2 * ay + ac
    x = x[0]
    target = loss_target[0]
    n_in, n_up = P_IN // N_DEV, 2 * DFF // N_DEV

    def lane_pad(t, width):
        return jnp.pad(t.astype(MXU), ((0, 0), (0, 0), (0, width - t.shape[-1])))

    conv_blob = _pad_rows(jnp.concatenate([t.reshape(-1) for t in (conv_a_w, conv_qkv_w, conv_ff_w)]),
                          SUB * LANES).reshape(-1, LANES)
    c_rows = jnp.zeros((SUB, D), F32).at[0].set(c[0])
    send = [lane_pad(w_in, IN_PAD), w_out.astype(MXU), lane_pad(w_up, UP_PAD), w_down.astype(MXU)]
    got = [None] * DEPTH
    g_in0, g_conv, g_c = _all_gather([send[0][0], conv_blob, c_rows], "gather_weights", in_vmem=False)
    shards, _ = lax.optimization_barrier(([t[0] for t in send[1:]], g_c))
    got[0] = [g_in0] + _all_gather_async(shards, "gather_weights_l0", collective_id=0)
    for i in range(1, DEPTH):
        shards, _ = lax.optimization_barrier(([t[i] for t in send], g_c))
        got[i] = _all_gather_async(shards, "gather_weights_l%d" % i, collective_id=i)
    w_in_f = [_interleave_cols(g[0][:, None], n_in, P_PAD, "interleave_w_in")[0] for g in got]
    w_up_f = [_interleave_cols(g[2][:, None], n_up, 2 * DFF, "interleave_w_up")[0] for g in got]
    w_out_f = [g[1].reshape(D, D) for g in got]
    w_down_f = [g[3].reshape(DFF, D) for g in got]
    sg = g_conv.reshape(N_DEV, -1)
    o1 = conv_a_w.size
    o2 = o1 + conv_qkv_w.size
    o3 = o2 + conv_ff_w.size
    conv_a_f = sg[:, 0:o1].reshape(N_DEV, DEPTH, 3, AW // N_DEV).transpose(1, 2, 0, 3).reshape(DEPTH, 3, AW)
    conv_q_f = sg[:, o1:o2].reshape(N_DEV, DEPTH, 4, 3 * H * HD // N_DEV).transpose(1, 2, 0, 3).reshape(DEPTH, 4, 3 * H * HD)
    conv_f_f = sg[:, o2:o3].reshape(N_DEV, DEPTH, 3, n_up).transpose(1, 2, 0, 3).reshape(DEPTH, 3, 2 * DFF)

    c_all = jnp.concatenate([g_c[:, 0], jnp.zeros((16 - N_DEV, D), F32)], axis=0)
    n_ada = N_MOD * D // N_DEV
    ada_b_cols = lax.dynamic_slice_in_dim(ada_b, me * n_ada, n_ada, axis=1)[:, None, :]
    mod_sh = _mod_fwd(c_all, ada_w, ada_b_cols)
    mod_all = _all_gather([mod_sh.reshape(DEPTH * 16, n_ada)], "gather_mod", in_vmem=True)[0]
    mod_all = mod_all.reshape(N_DEV, DEPTH, 16, n_ada)
    mod_mine = lax.dynamic_index_in_dim(mod_all, me, axis=2, keepdims=False)
    mod_full = mod_mine.transpose(1, 0, 2).reshape(DEPTH, N_MOD, D)

    tags = ["w_in", "w_out", "w_up", "w_down"]
    received = [dict() for _ in range(DEPTH)]

    def on_grads(i, part, gs_i):
        first_id = DEPTH if part == "ffn" else 2 * DEPTH
        got_i = _rs_exchange_async(gs_i, "rs_exchange_%s_l%d" % (part, i), collective_id=first_id + i)
        received[i].update(zip(("w_up", "w_down") if part == "ffn" else ("w_in", "w_out"), got_i))

    loss_local, dx, _, _, _, _, g_small, d_norm_f = _local_fwd_bwd(
        x, target, mod_full, (norm1_w, norm2_w, norm_a_w, a_log, dt_bias, norm_dn_w, norm_f_w),
        (w_in_f, w_out_f, w_up_f, w_down_f, conv_a_f, conv_q_f, conv_f_f), on_grads)
    loss = lax.psum(loss_local, ("x", "y", "c"))
    grad_x = dx[None]

    keys = ["dmod", "norm1", "norm2", "norm_a", "a_log", "dt_bias", "norm_dn", "conv_a", "conv_qkv", "conv_ff"]
    stacked = {k: jnp.stack([g_small[i][k] for i in range(DEPTH)]) for k in keys}
    flat_parts = [stacked[k].reshape(-1) for k in keys] + [d_norm_f]
    sizes = [int(t.shape[0]) for t in flat_parts]
    sflat = _pad_rows(jnp.concatenate(flat_parts), SUB * LANES).reshape(-1, LANES)
    sall = _all_gather([sflat], "gather_small_grads", in_vmem=True)[0]
    ssum = _sum_devices(sall).reshape(-1)
    so = [0]
    for sz in sizes:
        so.append(so[-1] + sz)
    red = {k: ssum[so[n]:so[n + 1]].reshape(stacked[k].shape) for n, k in enumerate(keys)}
    g_norm_f = ssum[so[len(keys)]:so[len(keys) + 1]]
    dmod_all = sall[:, 0:sizes[0] // LANES, :].reshape(N_DEV, DEPTH, N_MOD * D)

    g_ada_b = red["dmod"].reshape(DEPTH, N_MOD * D)
    dmod_cols = lax.dynamic_slice_in_dim(dmod_all, me * n_ada, n_ada, axis=2).transpose(1, 0, 2)
    dmod_cols = jnp.concatenate([dmod_cols, jnp.zeros((DEPTH, 16 - N_DEV, n_ada), F32)], axis=1)
    g_ada_w = _mod_bwd(c_all, dmod_cols)
    g_conv_a = lax.dynamic_slice_in_dim(red["conv_a"], me * (AW // N_DEV), AW // N_DEV, axis=2)
    g_conv_qkv = lax.dynamic_slice_in_dim(red["conv_qkv"], me * (3 * H * HD // N_DEV), 3 * H * HD // N_DEV, axis=2)
    g_conv_ff = lax.dynamic_slice_in_dim(red["conv_ff"], me * n_up, n_up, axis=2)

    mine = [jnp.stack([_rs_sum(received[i][t], "rs_sum_" + t) for i in range(DEPTH)]) for t in tags]
    g_w_in = mine[0][:, :, :n_in]
    g_w_out = mine[1]
    g_w_up = mine[2][:, :, :n_up]
    g_w_down = mine[3]

    grads = dict(ada_w=g_ada_w, ada_b=g_ada_b, norm1_w=red["norm1"], w_in=g_w_in, conv_a_w=g_conv_a,
                 norm_a_w=red["norm_a"], conv_qkv_w=g_conv_qkv, a_log=red["a_log"], dt_bias=red["dt_bias"],
                 norm_dn_w=red["norm_dn"], w_out=g_w_out, norm2_w=red["norm2"], w_up=g_w_up, conv_ff_w=g_conv_ff,
                 w_down=g_w_down, norm_f_w=g_norm_f)
    weights = dict(ada_w=ada_w, ada_b=ada_b, norm1_w=norm1_w, w_in=w_in, conv_a_w=conv_a_w, norm_a_w=norm_a_w,
                   conv_qkv_w=conv_qkv_w, a_log=a_log, dt_bias=dt_bias, norm_dn_w=norm_dn_w, w_out=w_out,
                   norm2_w=norm2_w, w_up=w_up, conv_ff_w=conv_ff_w, w_down=w_down, norm_f_w=norm_f_w)
    ms = dict(ada_w=m_ada_w, ada_b=m_ada_b, norm1_w=m_norm1_w, w_in=m_w_in, conv_a_w=m_conv_a_w, norm_a_w=m_norm_a_w,
              conv_qkv_w=m_conv_qkv_w, a_log=m_a_log, dt_bias=m_dt_bias, norm_dn_w=m_norm_dn_w, w_out=m_w_out,
              norm2_w=m_norm2_w, w_up=m_w_up, conv_ff_w=m_conv_ff_w, w_down=m_w_down, norm_f_w=m_norm_f_w)
    vs_ = dict(ada_w=v_ada_w, ada_b=v_ada_b, norm1_w=v_norm1_w, w_in=v_w_in, conv_a_w=v_conv_a_w, norm_a_w=v_norm_a_w,
               conv_qkv_w=v_conv_qkv_w, a_log=v_a_log, dt_bias=v_dt_bias, norm_dn_w=v_norm_dn_w, w_out=v_w_out,
               norm2_w=v_norm2_w, w_up=v_w_up, conv_ff_w=v_conv_ff_w, w_down=v_w_down, norm_f_w=v_norm_f_w)
    names = list(weights)
    big_names = ["ada_w", "w_in", "w_out", "w_up", "w_down"]
    delta, new_m, new_v = {}, {}, {}
    for n in big_names:
        shp = weights[n].shape
        two = lambda t: t.reshape(-1, shp[-1])
        dl, nm, nv = _adamw(two(weights[n]), two(grads[n]), two(ms[n]), two(vs_[n]), "adamw_" + n)
        delta[n], new_m[n], new_v[n] = dl.reshape(shp), nm.reshape(shp), nv.reshape(shp)
    small_names = [n for n in names if n not in big_names]

    def pack(dct):
        return _pad_rows(jnp.concatenate([dct[n].reshape(-1) for n in small_names]), SUB * LANES).reshape(-1, LANES)

    dl, nm, nv = _adamw(pack(weights), pack(grads), pack(ms), pack(vs_), "adamw_small")
    off = 0
    for n in small_names:
        sz, shp = weights[n].size, weights[n].shape
        delta[n] = dl.reshape(-1)[off:off + sz].reshape(shp)
        new_m[n] = nm.reshape(-1)[off:off + sz].reshape(shp)
        new_v[n] = nv.reshape(-1)[off:off + sz].reshape(shp)
        off += sz

    return (loss, grad_x, *[grads[n] for n in names], *[delta[n] for n in names],
            *[new_m[n] for n in names], *[new_v[n] for n in names])
```

```python
import functools
import math

import jax
import jax.numpy as jnp
from jax import lax
from jax.experimental import pallas as pl
from jax.experimental.pallas import tpu as pltpu
from jax.experimental.pallas import tpu_sc as plsc

F32 = jnp.float32
MXU = jnp.bfloat16

D = 1024
DEPTH = 4
N_MOD = 6
AW = 512
A_GROUP = 64
H = 4
HD = 128
CK = 64
DFF = 2816
P_IN = 3592
P_PAD = 3712
EPS = 1e-6
N_DEV = 8
LANES = 128
SUB = 8
VMEM_LIMIT = 56 * 1024 * 1024

ADAM_LR, ADAM_B1, ADAM_B2, ADAM_EPS, ADAM_WD, ADAM_STEP = 0.001, 0.9, 0.999, 1e-08, 0.01, 10

NN = ((1,), (0,))
NT = ((1,), (1,))
TN = ((0,), (0,))
HI = lax.Precision.HIGHEST
MESH = pl.DeviceIdType.MESH


def _dot(a, b, dims, prec=None):
    if prec is None:
        a = a.astype(MXU) if a.dtype == F32 else a
        b = b.astype(MXU) if b.dtype == F32 else b
    return lax.dot_general(a, b, (dims, ((), ())), precision=prec, preferred_element_type=F32)


def _params(n_grid=0, limit=VMEM_LIMIT):
    sem = ("arbitrary",) * n_grid if n_grid else None
    return pltpu.CompilerParams(dimension_semantics=sem, vmem_limit_bytes=limit)


def _tile(n, want):
    if n <= want:
        return n
    t = want - want % SUB
    while n % t:
        t -= SUB
    assert t > 0, (n, want)
    return t


def _full(shape):
    nd = len(shape)
    return pl.BlockSpec(shape, lambda *_: (0,) * nd)


def _sig(x):
    return jax.nn.sigmoid(x)


def _rms(x):
    r = lax.rsqrt(jnp.mean(x * x, axis=-1, keepdims=True) + EPS)
    return x * r, r


def _rms_bwd(dn, n, r):
    return r * (dn - n * jnp.mean(dn * n, axis=-1, keepdims=True))


def _l2_bwd(dn, n, r):
    return r * (dn - n * jnp.sum(dn * n, axis=-1, keepdims=True))


def _sum0(x):
    return jnp.sum(x, axis=0, keepdims=True)


def _shift_down(x, s, halo):
    ext = jnp.concatenate([halo, x], axis=0)
    return pltpu.roll(ext, s, 0)[SUB:, :]


def _shift_up(x, s, halo):
    t = x.shape[0]
    ext = jnp.concatenate([x, halo], axis=0)
    return pltpu.roll(ext, t + SUB - s, 0)[:t, :]


def _conv_fwd(x, w_ref, width, halo):
    sh = [x] + [_shift_down(x, s, halo) for s in range(1, width)]
    out = w_ref[width - 1:width, :] * sh[0]
    for s in range(1, width):
        out = out + w_ref[width - 1 - s:width - s, :] * sh[s]
    return out, sh


def _blockdiag_mean(n, group):
    r = lax.shift_right_logical(lax.broadcasted_iota(jnp.int32, (n, n), 0), int(math.log2(group)))
    c = lax.shift_right_logical(lax.broadcasted_iota(jnp.int32, (n, n), 1), int(math.log2(group)))
    return jnp.where(r == c, 1.0 / group, 0.0).astype(F32)


def _softplus(x):
    return jnp.maximum(x, 0.0) + jnp.log(1.0 + jnp.exp(-jnp.abs(x)))


def _my_place():
    return lax.axis_index("x"), lax.axis_index("y"), lax.axis_index("c")


def _all_gather(shards, name, in_vmem):
    nt = len(shards)

    def body(*refs):
        x_refs, out_refs = refs[:nt], refs[nt:2 * nt]
        send_sems, recv_sems, local_sems = refs[2 * nt:]
        x, y, c = _my_place()
        me, sibling = (x, y, c), (x, y, 1 - c)
        chips = [(1 - x, y), (x, 1 - y), (1 - x, 1 - y)]
        everything = []
        for t in range(nt):
            x_ref, out_ref = x_refs[t], out_refs[t]

            def blk(px, py, pc, out_ref=out_ref):
                return out_ref.at[4 * px + 2 * py + pc]

            def copy(k, block, to, src=None, t=t, blk=blk):
                return pltpu.make_async_remote_copy(
                    src_ref=blk(*block) if src is None else src, dst_ref=blk(*block),
                    send_sem=send_sems.at[7 * t + k], recv_sem=recv_sems.at[7 * t + k], device_id=to, device_id_type=MESH)

            mine = pltpu.make_async_copy(x_ref, blk(*me), local_sems.at[t])
            mine.start()
            first = [copy(0, me, sibling, src=x_ref)]
            first += [copy(1 + j, me, (*chip, c), src=x_ref) for j, chip in enumerate(chips)]
            for cp in first:
                cp.start()
            everything.append((copy, mine, first))
        sends = []
        for copy, mine, first in everything:
            passed = [copy(4 + j, (*chip, c), sibling) for j, chip in enumerate(chips)]
            for j, chip in enumerate(chips):
                copy(1 + j, (*chip, c), me).wait_recv()
                passed[j].start()
            sends += first + passed
        for copy, mine, first in everything:
            copy(0, sibling, me).wait_recv()
            for j, chip in enumerate(chips):
                copy(4 + j, (*chip, 1 - c), me).wait_recv()
        for cp in sends:
            cp.wait_send()
        for copy, mine, first in everything:
            mine.wait()

    space = pltpu.VMEM if in_vmem else pl.ANY
    return pl.pallas_call(
        body, name=name,
        out_shape=[jax.ShapeDtypeStruct((N_DEV,) + s.shape, s.dtype) for s in shards],
        in_specs=[pl.BlockSpec(memory_space=space)] * nt,
        out_specs=[pl.BlockSpec(memory_space=space)] * nt,
        scratch_shapes=[pltpu.SemaphoreType.DMA((7 * nt,)), pltpu.SemaphoreType.DMA((7 * nt,)),
                        pltpu.SemaphoreType.DMA((nt,))],
        compiler_params=pltpu.CompilerParams(vmem_limit_bytes=VMEM_LIMIT),
    )(*shards)


def _all_gather_async(shards, name, collective_id):
    nt = len(shards)
    hbm = pltpu.MemorySpace.HBM
    x_refs = [jax.new_ref(s, memory_space=hbm) for s in shards]
    out_refs = [jax.empty_ref(jax.ShapeDtypeStruct((N_DEV,) + s.shape, s.dtype), memory_space=hbm) for s in shards]

    @pl.kernel(mesh=plsc.ScalarSubcoreMesh(axis_name="sequencer", num_cores=1), name=name,
               scratch_types=(pltpu.SemaphoreType.DMA((7 * nt,)), pltpu.SemaphoreType.DMA((7 * nt,)),
                              pltpu.SemaphoreType.DMA((nt,))),
               compiler_params=pltpu.CompilerParams(collective_id=collective_id))
    def launch(send_sems, recv_sems, local_sems):
        x, y, c = _my_place()
        me, sibling = (x, y, c), (x, y, 1 - c)
        chips = [(1 - x, y), (x, 1 - y), (1 - x, 1 - y)]
        barrier = pltpu.get_barrier_semaphore()
        for peer in [sibling] + [(*chip, c) for chip in chips]:
            pl.semaphore_signal(barrier, inc=1, device_id=peer, device_id_type=MESH)
        pl.semaphore_wait(barrier, 4)
        everything = []
        for t in range(nt):
            x_ref, out_ref = x_refs[t], out_refs[t]

            def blk(px, py, pc, out_ref=out_ref):
                return out_ref.at[4 * px + 2 * py + pc]

            def copy(k, block, to, src=None, t=t, blk=blk):
                return pltpu.make_async_remote_copy(
                    src_ref=blk(*block) if src is None else src, dst_ref=blk(*block),
                    send_sem=send_sems.at[7 * t + k], recv_sem=recv_sems.at[7 * t + k], device_id=to, device_id_type=MESH)

            mine = pltpu.make_async_copy(x_ref, blk(*me), local_sems.at[t])
            mine.start()
            first = [copy(0, me, sibling, src=x_ref)]
            first += [copy(1 + j, me, (*chip, c), src=x_ref) for j, chip in enumerate(chips)]
            for cp in first:
                cp.start()
            everything.append((copy, mine, first))
        sends = []
        for copy, mine, first in everything:
            passed = [copy(4 + j, (*chip, c), sibling) for j, chip in enumerate(chips)]
            for j, chip in enumerate(chips):
                copy(1 + j, (*chip, c), me).wait_recv()
                passed[j].start()
            sends += first + passed
        for copy, mine, first in everything:
            copy(0, sibling, me).wait_recv()
            for j, chip in enumerate(chips):
                copy(4 + j, (*chip, 1 - c), me).wait_recv()
        for cp in sends:
            cp.wait_send()
        for copy, mine, first in everything:
            mine.wait()

    launch()
    return [r[...] for r in out_refs]


def _rs_exchange_async(srcs, name, collective_id):
    nt = len(srcs)
    hbm = pltpu.MemorySpace.HBM
    src_refs = [jax.new_ref(s, memory_space=hbm) for s in srcs]
    out_refs = [jax.empty_ref(jax.ShapeDtypeStruct(s.shape, s.dtype), memory_space=hbm) for s in srcs]
    flips = [(fx, fy, fc) for fx in (0, 1) for fy in (0, 1) for fc in (0, 1)][1:]

    @pl.kernel(mesh=plsc.ScalarSubcoreMesh(axis_name="sequencer", num_cores=1), name=name,
               scratch_types=(pltpu.SemaphoreType.DMA((7 * nt,)), pltpu.SemaphoreType.DMA((7 * nt,)),
                              pltpu.SemaphoreType.DMA((nt,))),
               compiler_params=pltpu.CompilerParams(collective_id=collective_id))
    def launch(send_sems, recv_sems, local_sems):
        x, y, c = _my_place()
        me = 4 * x + 2 * y + c
        peers = [(1 - x if fx else x, 1 - y if fy else y, 1 - c if fc else c) for fx, fy, fc in flips]
        barrier = pltpu.get_barrier_semaphore()
        for peer in peers:
            pl.semaphore_signal(barrier, inc=1, device_id=peer, device_id_type=MESH)
        pl.semaphore_wait(barrier, len(peers))
        own = [pltpu.make_async_copy(src_refs[t].at[me], out_refs[t].at[me], local_sems.at[t]) for t in range(nt)]
        copies = [pltpu.make_async_remote_copy(
            src_ref=src_refs[t].at[4 * px + 2 * py + pc], dst_ref=out_refs[t].at[me],
            send_sem=send_sems.at[7 * t + f], recv_sem=recv_sems.at[7 * t + f],
            device_id=(px, py, pc), device_id_type=MESH) for t in range(nt) for f, (px, py, pc) in enumerate(peers)]
        for cp in own + copies:
            cp.start()
        for cp in copies + own:
            cp.wait()

    launch()
    return [r[...] for r in out_refs]


def _rs_sum(recv, name):
    _, r, n = recv.shape
    tr = _tile(r, 512)

    def body(r_ref, o_ref):
        s = r_ref[0].astype(F32)
        for k in range(1, N_DEV):
            s = s + r_ref[k].astype(F32)
        o_ref[...] = s

    return pl.pallas_call(
        body, name=name, grid=(r // tr,),
        in_specs=[pl.BlockSpec((N_DEV, tr, n), lambda i: (0, i, 0))],
        out_specs=pl.BlockSpec((tr, n), lambda i: (i, 0)),
        out_shape=jax.ShapeDtypeStruct((r, n), F32), compiler_params=_params(1),
    )(recv)


def _shard_windows(n_shard, count, first=0):
    out = []
    for k in range(first, first + count):
        off = n_shard * k
        a, s = off // LANES, off % LANES
        out.append((a, s, -(-(s + n_shard) // LANES) * LANES))
    return out


def _fit_lanes(x, width):
    have = x.shape[1]
    if have < width:
        return jnp.concatenate([x, jnp.zeros((x.shape[0], width - have), x.dtype)], axis=-1)
    return x[:, :width]


def _interleave_cols(g, n_shard, w_out, name):
    nd, nl, rows, wpad = g.shape
    rb = _tile(rows, 256)
    wins = _shard_windows(n_shard, nd)

    def body(g_ref, o_ref, acc):
        acc[...] = jnp.zeros_like(acc)
        for k, (a, s, win) in enumerate(wins):
            xk = _fit_lanes(g_ref[k].astype(F32), win)
            if s:
                xk = pltpu.roll(xk, s, 1)
            acc[:, a * LANES:a * LANES + win] += xk
        o_ref[...] = acc[...].astype(o_ref.dtype)

    return pl.pallas_call(
        body, name=name, grid=(nl, rows // rb),
        in_specs=[pl.BlockSpec((nd, None, rb, wpad), lambda l, i: (0, l, i, 0))],
        out_specs=pl.BlockSpec((None, rb, w_out), lambda l, i: (l, i, 0)),
        out_shape=jax.ShapeDtypeStruct((nl, rows, w_out), g.dtype),
        scratch_shapes=[pltpu.VMEM((rb, w_out), F32)],
        compiler_params=_params(2),
    )(g)


def _sum_devices(g):
    _, r, n = g.shape

    def body(g_ref, o_ref):
        s = g_ref[0]
        for t in range(1, N_DEV):
            s = s + g_ref[t]
        o_ref[...] = s

    return pl.pallas_call(
        body, name="sum_devices", out_shape=jax.ShapeDtypeStruct((r, n), F32),
        in_specs=[pl.BlockSpec(memory_space=pltpu.VMEM)], out_specs=pl.BlockSpec(memory_space=pltpu.VMEM),
        compiler_params=pltpu.CompilerParams(vmem_limit_bytes=VMEM_LIMIT),
    )(g)


def _mod_fwd(c_all, ada_w, ada_b_cols):
    nl, _, nc = ada_w.shape

    def body(c_ref, w_ref, b_ref, o_ref):
        cv = c_ref[...]
        act = (cv * _sig(cv)).astype(MXU)
        o_ref[...] = _dot(act, w_ref[...].astype(MXU), NN) + b_ref[...]

    return pl.pallas_call(
        body, name="mod_fwd", grid=(nl,),
        in_specs=[_full((16, D)), pl.BlockSpec((None, D, nc), lambda i: (i, 0, 0)),
                  pl.BlockSpec((None, 1, nc), lambda i: (i, 0, 0))],
        out_specs=pl.BlockSpec((None, 16, nc), lambda i: (i, 0, 0)),
        out_shape=jax.ShapeDtypeStruct((nl, 16, nc), F32), compiler_params=_params(1),
    )(c_all, ada_w, ada_b_cols)


def _mod_bwd(c_all, dmod_cols):
    nl, _, nc = dmod_cols.shape

    def body(c_ref, d_ref, o_ref):
        cv = c_ref[...]
        act = (cv * _sig(cv)).astype(MXU)
        o_ref[...] = _dot(act, d_ref[...].astype(MXU), TN)

    return pl.pallas_call(
        body, name="mod_bwd", grid=(nl,),
        in_specs=[_full((16, D)), pl.BlockSpec((None, 16, nc), lambda i: (i, 0, 0))],
        out_specs=pl.BlockSpec((None, D, nc), lambda i: (i, 0, 0)),
        out_shape=jax.ShapeDtypeStruct((nl, D, nc), F32), compiler_params=_params(1),
    )(c_all, dmod_cols)


def _in_proj(x, modrows, vec, w_in):
    L = x.shape[0]
    T = _tile(L, 512)

    def body(x_ref, mod_ref, vec_ref, w_ref, p_ref, h_ref):
        n, _ = _rms(x_ref[...])
        h = n * vec_ref[0:1, :] * (1.0 + mod_ref[1:2, :]) + mod_ref[0:1, :]
        hb = h.astype(MXU)
        h_ref[...] = hb
        p_ref[...] = _dot(hb, w_ref[...], NN)

    return pl.pallas_call(
        body, name="in_proj", grid=(L // T,),
        in_specs=[pl.BlockSpec((T, D), lambda i: (i, 0)), _full((SUB, D)), _full((SUB, D)), _full((D, P_PAD))],
        out_specs=[pl.BlockSpec((T, P_PAD), lambda i: (i, 0)), pl.BlockSpec((T, D), lambda i: (i, 0))],
        out_shape=[jax.ShapeDtypeStruct((L, P_PAD), F32), jax.ShapeDtypeStruct((L, D), MXU)],
        compiler_params=_params(1),
    )(x, modrows, vec, w_in)


def _gate_small(s, sp_ref):
    lane = lax.broadcasted_iota(jnp.int32, s.shape, 1)
    a = -jnp.exp(sp_ref[0:1, :])
    xb = s + sp_ref[1:2, :]
    beta = _sig(s)
    g = a * _softplus(xb)
    return lane, a, xb, beta, g


def _pre_fwd(p, pa, cq, sp):
    L = p.shape[0]
    T = _tile(L, 256)
    scale = HD ** -0.5

    def body(pm_ref, ps_ref, pa_ref, cq_ref, sp_ref, qn_ref, kn_ref, vs_ref, gb_ref, ya_ref, cu_ref, qc_ref,
             u_carry, q_carry):
        @pl.when(pl.program_id(0) == 0)
        def _():
            u_carry[...] = jnp.zeros_like(u_carry)
            q_carry[...] = jnp.zeros_like(q_carry)

        a_b = pm_ref[:, 0:AW]
        u = pm_ref[:, AW:2 * AW] * pm_ref[:, 2 * AW:3 * AW]
        cu, _ = _conv_fwd(u, pa_ref, 3, u_carry[...])
        cu_ref[...] = cu.astype(MXU)
        u_carry[...] = u[T - SUB:T, :]
        yp = a_b * cu
        ms = _dot_f32(yp * yp, _blockdiag_mean(AW, A_GROUP), NN, exact="b")
        ya_ref[...] = (yp * lax.rsqrt(ms + EPS) * pa_ref[3:4, :]).astype(MXU)

        qkv = pm_ref[:, 3 * AW:3 * AW + 3 * H * HD]
        qc, _ = _conv_fwd(qkv, cq_ref, 4, q_carry[...])
        qc_ref[...] = qc.astype(MXU)
        q_carry[...] = qkv[T - SUB:T, :]
        qs = qc * _sig(qc)
        for h in range(H):
            q = qs[:, h * HD:(h + 1) * HD]
            qn_ref[:, h * HD:(h + 1) * HD] = q * (lax.rsqrt(jnp.sum(q * q, axis=-1, keepdims=True) + EPS) * scale)
            k = qs[:, (H + h) * HD:(H + h + 1) * HD]
            kn_ref[:, h * HD:(h + 1) * HD] = k * lax.rsqrt(jnp.sum(k * k, axis=-1, keepdims=True) + EPS)
        vs_ref[...] = qs[:, 2 * H * HD:3 * H * HD]

        lane, _, _, beta, g = _gate_small(ps_ref[...], sp_ref)
        gb_ref[...] = jnp.where(lane < H, beta, jnp.where(lane < 2 * H, g, 0.0))

    w3 = 3 * AW + 3 * H * HD
    row = lambda i: (i, 0)
    return pl.pallas_call(
        body, name="pre_fwd", grid=(L // T,),
        in_specs=[pl.BlockSpec((T, w3), row), pl.BlockSpec((T, LANES), lambda i: (i, (P_PAD - LANES) // LANES)),
                  _full((SUB, AW)), _full((SUB, 3 * H * HD)), _full((SUB, LANES))],
        out_specs=[pl.BlockSpec((T, H * HD), row)] * 3 + [pl.BlockSpec((T, LANES), row), pl.BlockSpec((T, AW), row),
                                                          pl.BlockSpec((T, AW), row), pl.BlockSpec((T, 3 * H * HD), row)],
        out_shape=[jax.ShapeDtypeStruct((L, H * HD), F32)] * 3
        + [jax.ShapeDtypeStruct((L, LANES), F32), jax.ShapeDtypeStruct((L, AW), MXU),
           jax.ShapeDtypeStruct((L, AW), MXU), jax.ShapeDtypeStruct((L, 3 * H * HD), MXU)],
        scratch_shapes=[pltpu.VMEM((SUB, AW), F32), pltpu.VMEM((SUB, 3 * H * HD), F32)],
        compiler_params=_params(1),
    )(p, p, pa, cq, sp)


def _gdr_masks():
    r = lax.broadcasted_iota(jnp.int32, (CK, CK), 0)
    c = lax.broadcasted_iota(jnp.int32, (CK, CK), 1)
    return r >= c, r > c


def _head_cols(gbt, h):
    return gbt[:, h:h + 1], gbt[:, H + h:H + h + 1]


def _split(x, parts):
    out = []
    for _ in range(parts):
        hi = x.astype(jnp.bfloat16)
        out.append(hi)
        x = x - hi.astype(F32)
    return out


def _dot_f32(a, b, dims, exact=None):
    if exact == "a":
        ab = a.astype(jnp.bfloat16)
        return sum(_dot(ab, t, dims) for t in _split(b, 3))
    if exact == "b":
        bb = b.astype(jnp.bfloat16)
        return sum(_dot(t, bb, dims) for t in _split(a, 3))
    ah, al = _split(a, 2)
    bh, bl = _split(b, 2)
    return _dot(ah, bh, dims) + _dot(ah, bl, dims) + _dot(al, bh, dims)


def _gdr_consts():
    causal, strict = _gdr_masks()
    return dict(causal=causal, strict=strict, tril=jnp.where(causal, 1.0, 0.0).astype(F32),
                eye=jnp.where(causal & jnp.logical_not(strict), 1.0, 0.0).astype(F32),
                bcast=jnp.full((CK, HD), 1.0 / HD, F32))


def _dots(a, b, dims):
    return [_dot(x, y, dims) for x, y in zip(a, b)]


def _dots_f32(a, b, dims, exact=None):
    n = len(a)
    if exact == "a":
        lhs = [[x.astype(jnp.bfloat16)] * 3 for x in a]
        rhs = [_split(y, 3) for y in b]
    elif exact == "b":
        lhs = [_split(x, 3) for x in a]
        rhs = [[y.astype(jnp.bfloat16)] * 3 for y in b]
    else:
        sa = [_split(x, 2) for x in a]
        sb = [_split(y, 2) for y in b]
        lhs = [[s[0], s[0], s[1]] for s in sa]
        rhs = [[s[0], s[1], s[0]] for s in sb]
    terms = [[_dot(lhs[i][t], rhs[i][t], dims) for i in range(n)] for t in range(3)]
    return [terms[0][i] + terms[1][i] + terms[2][i] for i in range(n)]


def _gdr_local(q, k, v, beta, g, cst, tinv=None):
    n = len(q)
    R = range(n)
    causal, strict = cst["causal"], cst["strict"]
    gc = _dots_f32([cst["tril"]] * n, [jnp.broadcast_to(g[i], (CK, HD)) for i in R], NN, exact="a")
    g_row = _dots_f32([cst["bcast"]] * n, gc, NT, exact="a")
    decay = [jnp.where(causal, jnp.exp(jnp.where(causal, gc[i][:, 0:CK] - g_row[i], 0.0)), 0.0) for i in R]
    eg = [jnp.exp(gc[i]) for i in R]
    gl = [gc[i][CK - 1:CK, :] for i in R]
    ek = [jnp.exp(gl[i] - gc[i]) for i in R]
    cd = [jnp.exp(gl[i]) for i in R]
    kb = [k[i] * beta[i] for i in R]
    pk = _dots(kb, k, NT)
    if tinv is None:
        xp = [-jnp.where(strict, pk[i] * decay[i], 0.0) for i in R]
        tinv = [cst["eye"] + xp[i] for i in R]
        for _ in range(5):
            xp = _dots_f32(xp, xp, NN)
            tx = _dots_f32(tinv, xp, NN)
            tinv = [tinv[i] + tx[i] for i in R]
    u = _dots(tinv, [v[i] * beta[i] for i in R], NN)
    w = _dots(tinv, [kb[i] * eg[i] for i in R], NN)
    qk = _dots(q, k, NT)
    intra = [jnp.where(causal, qk[i] * decay[i], 0.0) for i in R]
    return dict(decay=decay, eg=eg, ek=ek, cd=cd, kb=kb, pk=pk, tinv=tinv, u=u, w=w, qk=qk, intra=intra,
                q_dec=[q[i] * eg[i] for i in R], k_dec=[k[i] * ek[i] for i in R])


GDR_SUB = 8


def _gdr_fwd(qn, kn, vs, gb):
    L = qn.shape[0]
    nc = L // CK
    cb = min(8, nc)
    rb = cb * CK
    nb = nc // cb
    nsub = GDR_SUB if cb % GDR_SUB == 0 else 1

    def body(q_ref, k_ref, v_ref, gb_ref, o_ref, st_ref, ti_ref, s_ref):
        @pl.when(pl.program_id(0) == 0)
        def _():
            s_ref[...] = jnp.zeros_like(s_ref)

        cst = _gdr_consts()
        heads = range(H)

        def group(gi, carry):
            rows = [pl.ds(pl.multiple_of((gi * nsub + j) * CK, CK), CK) for j in range(nsub)]
            chains = [(j, h) for j in range(nsub) for h in heads]
            gbt = [gb_ref[rows[j], :] for j in range(nsub)]
            cols = lambda h: slice(h * HD, (h + 1) * HD)
            t = _gdr_local([q_ref[rows[j], cols(h)] for j, h in chains], [k_ref[rows[j], cols(h)] for j, h in chains],
                           [v_ref[rows[j], cols(h)] for j, h in chains],
                           [_head_cols(gbt[j], h)[0] for j, h in chains], [_head_cols(gbt[j], h)[1] for j, h in chains], cst)
            s = [s_ref[h] for h in heads]
            for j in range(nsub):
                at = lambda key: [t[key][j * H + h] for h in heads]
                for h in heads:
                    st_ref[h, gi * nsub + j] = s[h]
                    ti_ref[h, gi * nsub + j] = t["tinv"][j * H + h]
                ws = _dots(at("w"), s, NN)
                v_new = [u_h - ws_h for u_h, ws_h in zip(at("u"), ws)]
                o_s = _dots(at("q_dec"), s, NN)
                o_v = _dots(at("intra"), v_new, NN)
                kv = _dots(at("k_dec"), v_new, TN)
                cd = at("cd")
                for h in heads:
                    o_ref[rows[j], cols(h)] = o_s[h] + o_v[h]
                s = [s[h] * cd[h] + kv[h] for h in heads]
            for h in heads:
                s_ref[h] = s[h]
            return carry

        lax.fori_loop(0, cb // nsub, group, 0)

    blk = pl.BlockSpec((rb, H * HD), lambda b: (b, 0))
    return pl.pallas_call(
        body, name="gdr_fwd", grid=(nb,),
        in_specs=[blk, blk, blk, pl.BlockSpec((rb, LANES), lambda b: (b, 0))],
        out_specs=[blk, pl.BlockSpec((H, cb, HD, HD), lambda b: (0, b, 0, 0)),
                   pl.BlockSpec((H, cb, CK, CK), lambda b: (0, b, 0, 0))],
        out_shape=[jax.ShapeDtypeStruct((L, H * HD), F32), jax.ShapeDtypeStruct((H, nc, HD, HD), F32),
                   jax.ShapeDtypeStruct((H, nc, CK, CK), F32)],
        scratch_shapes=[pltpu.VMEM((H, HD, HD), F32)],
        compiler_params=_params(1),
    )(qn, kn, vs, gb)


def _gdr_bwd(qn, kn, vs, gb, states, tinvs, do):
    L = qn.shape[0]
    nc = L // CK
    cb = min(8, nc)
    rb = cb * CK
    nb = nc // cb
    nsub = GDR_SUB if cb % GDR_SUB == 0 else 1

    def body(q_ref, k_ref, v_ref, gb_ref, st_ref, ti_ref, do_ref, dq_ref, dk_ref, dv_ref, dgb_ref, ds_ref):
        @pl.when(pl.program_id(0) == 0)
        def _():
            ds_ref[...] = jnp.zeros_like(ds_ref)

        cst = _gdr_consts()
        causal, strict = cst["causal"], cst["strict"]
        ones = jnp.ones((CK, HD), F32)
        row = lax.broadcasted_iota(jnp.int32, (CK, HD), 0)
        lane = lax.broadcasted_iota(jnp.int32, (CK, LANES), 1)

        heads = range(H)
        rsum = lambda x: jnp.sum(x, axis=-1, keepdims=True)

        def group(gj, carry):
            gi = cb // nsub - 1 - gj
            rows = [pl.ds(pl.multiple_of((gi * nsub + j) * CK, CK), CK) for j in range(nsub)]
            chains = [(j, h) for j in range(nsub) for h in heads]
            gbt = [gb_ref[rows[j], :] for j in range(nsub)]
            cols = lambda h: slice(h * HD, (h + 1) * HD)
            q_all = [q_ref[rows[j], cols(h)] for j, h in chains]
            k_all = [k_ref[rows[j], cols(h)] for j, h in chains]
            v_all = [v_ref[rows[j], cols(h)] for j, h in chains]
            beta_all = [_head_cols(gbt[j], h)[0] for j, h in chains]
            t = _gdr_local(q_all, k_all, v_all, beta_all, [_head_cols(gbt[j], h)[1] for j, h in chains], cst,
                           tinv=[ti_ref[h, gi * nsub + j] for j, h in chains])
            ds_out = [ds_ref[h] for h in heads]
            for j in reversed(range(nsub)):
                at = lambda key: [t[key][j * H + h] for h in heads]
                pick = lambda lst: [lst[j * H + h] for h in heads]
                q, k, v, beta = pick(q_all), pick(k_all), pick(v_all), pick(beta_all)
                u, w, tinv, decay = at("u"), at("w"), at("tinv"), at("decay")
                eg, ek, cd, kb = at("eg"), at("ek"), at("cd"), at("kb")
                q_dec, k_dec, intra, pk, qk = at("q_dec"), at("k_dec"), at("intra"), at("pk"), at("qk")
                s = [st_ref[h, gi * nsub + j] for h in heads]
                dout = [do_ref[rows[j], cols(h)] for h in heads]

                ws = _dots(w, s, NN)
                v_new = [u[h] - ws[h] for h in heads]
                dq_dec = _dots(dout, s, NT)
                qd = _dots(q_dec, dout, TN)
                di = _dots(dout, v_new, NT)
                dintra = [jnp.where(causal, di[h], 0.0) for h in heads]
                ido = _dots(intra, dout, TN)
                kds = _dots(k_dec, ds_out, NN)
                dv_new = [ido[h] + kds[h] for h in heads]
                dk_dec = _dots(v_new, ds_out, NT)
                dcd = [jnp.sum(jnp.sum(ds_out[h] * s[h], axis=1, keepdims=True), axis=0, keepdims=True) for h in heads]
                dvs = _dots(dv_new, s, NT)
                dw = [-dvs[h] for h in heads]
                wdv = _dots(w, dv_new, TN)
                ds_new = [qd[h] + ds_out[h] * cd[h] - wdv[h] for h in heads]
                dru = _dots(tinv, dv_new, TN)
                drw = _dots(tinv, dw, TN)
                dl1 = _dots(dru, u, NT)
                dl2 = _dots(drw, w, NT)
                dlower = [-jnp.where(strict, dl1[h] + dl2[h], 0.0) for h in heads]
                dv = [dru[h] * beta[h] for h in heads]
                dbeta = [rsum(dru[h] * v[h]) for h in heads]
                dgc = [rsum(drw[h] * kb[h]) * eg[h] for h in heads]
                dpk = [dlower[h] * decay[h] for h in heads]
                dqk = [dintra[h] * decay[h] for h in heads]
                dpk_k = _dots(dpk, k, NN)
                dkb = [drw[h] * eg[h] + dpk_k[h] for h in heads]
                dk1 = _dots(dpk, kb, TN)
                dq1 = _dots(dqk, k, NN)
                dk2 = _dots(dqk, q, TN)
                m = [(dlower[h] * pk[h] + dintra[h] * qk[h]) * decay[h] for h in heads]
                mcol = _dots_f32(m, [ones] * H, TN, exact="b")
                e = [rsum(dk_dec[h] * k_dec[h]) for h in heads]
                dgl = [jnp.sum(e[h], axis=0, keepdims=True) + dcd[h] * cd[h] for h in heads]
                dgc = [dgc[h] + rsum(m[h]) - mcol[h] + rsum(dq_dec[h] * q_dec[h]) - e[h]
                       + jnp.where(row == CK - 1, dgl[h], 0.0) for h in heads]
                dg = _dots_f32([cst["tril"]] * H, dgc, TN, exact="a")
                dgb = jnp.zeros((CK, LANES), F32)
                for h in heads:
                    dq_ref[rows[j], cols(h)] = dq1[h] + dq_dec[h] * eg[h]
                    dk_ref[rows[j], cols(h)] = dk1[h] + dk2[h] + dk_dec[h] * ek[h] + dkb[h] * beta[h]
                    dv_ref[rows[j], cols(h)] = dv[h]
                    db = dbeta[h] + rsum(dkb[h] * k[h])
                    dgb = dgb + jnp.where(lane == h, db, 0.0) + jnp.where(lane == H + h, dg[h], 0.0)
                dgb_ref[rows[j], :] = dgb
                ds_out = ds_new
            for h in heads:
                ds_ref[h] = ds_out[h]
            return carry

        lax.fori_loop(0, cb // nsub, group, 0)

    blk = pl.BlockSpec((rb, H * HD), lambda b: (nb - 1 - b, 0))
    sblk = pl.BlockSpec((rb, LANES), lambda b: (nb - 1 - b, 0))
    return pl.pallas_call(
        body, name="gdr_bwd", grid=(nb,),
        in_specs=[blk, blk, blk, sblk, pl.BlockSpec((H, cb, HD, HD), lambda b: (0, nb - 1 - b, 0, 0)),
                  pl.BlockSpec((H, cb, CK, CK), lambda b: (0, nb - 1 - b, 0, 0)), blk],
        out_specs=[blk, blk, blk, sblk],
        out_shape=[jax.ShapeDtypeStruct((L, H * HD), F32)] * 3 + [jax.ShapeDtypeStruct((L, LANES), F32)],
        scratch_shapes=[pltpu.VMEM((H, HD, HD), F32)],
        compiler_params=_params(1),
    )(qn, kn, vs, gb, states, tinvs, do)


def _post_fwd(o, p, ya, x, modrows, sp, w_out):
    L = x.shape[0]
    T = _tile(L, 512)

    def body(o_ref, z_ref, ya_ref, x_ref, mod_ref, sp_ref, w_ref, y_ref, x2_ref, yb_ref):
        ndw = sp_ref[2:3, :]
        z = z_ref[...]
        sz = z * _sig(z)
        parts = []
        for h in range(H):
            n, _ = _rms(o_ref[:, h * HD:(h + 1) * HD])
            parts.append(n * ndw * sz[:, h * HD:(h + 1) * HD])
        yb = jnp.concatenate(parts, axis=-1).astype(MXU)
        yb_ref[...] = yb
        y = _dot(ya_ref[...], w_ref[0:AW, :], NN) + _dot(yb, w_ref[AW:2 * AW, :], NN)
        y_ref[...] = y
        x2_ref[...] = x_ref[...] + mod_ref[2:3, :] * y

    row = lambda i: (i, 0)
    zcol = (3 * AW + 3 * H * HD) // (H * HD)
    return pl.pallas_call(
        body, name="post_fwd", grid=(L // T,),
        in_specs=[pl.BlockSpec((T, H * HD), row), pl.BlockSpec((T, H * HD), lambda i: (i, zcol)),
                  pl.BlockSpec((T, AW), row), pl.BlockSpec((T, D), row), _full((SUB, D)), _full((SUB, LANES)),
                  _full((D, D))],
        out_specs=[pl.BlockSpec((T, D), row), pl.BlockSpec((T, D), row), pl.BlockSpec((T, H * HD), row)],
        out_shape=[jax.ShapeDtypeStruct((L, D), F32), jax.ShapeDtypeStruct((L, D), F32),
                   jax.ShapeDtypeStruct((L, H * HD), MXU)],
        compiler_params=_params(1),
    )(o, p, ya, x, modrows, sp, w_out)


FF_COLS = 2
FF_CW = DFF // FF_COLS
FF_ROWS = 256


def _ffn_fwd_half(x2, modrows, vec, w_up, cff, w_down, j, d_prev):
    assert FF_COLS == 2
    L = x2.shape[0]
    T = _tile(L, FF_ROWS)
    nj = FF_COLS
    last = d_prev is not None

    def body(*refs):
        x_ref, mod_ref, vec_ref, wg_ref, wu_ref, cg_ref, cu_ref, wd_ref = refs[:8]
        if last:
            dp_ref, gp_ref, up_ref, gc_ref, uc_ref, f_ref, d_ref, x3_ref, carry_g, carry_u = refs[8:]
        else:
            h_ref, gp_ref, up_ref, gc_ref, uc_ref, f_ref, d_ref, carry_g, carry_u = refs[8:]

        @pl.when(pl.program_id(0) == 0)
        def _():
            carry_g[...] = jnp.zeros_like(carry_g)
            carry_u[...] = jnp.zeros_like(carry_u)

        xv = x_ref[...]
        n, _ = _rms(xv)
        hb = (n * vec_ref[1:2, :] * (1.0 + mod_ref[4:5, :]) + mod_ref[3:4, :]).astype(MXU)
        if not last:
            h_ref[...] = hb
        g = _dot(hb, wg_ref[...], NN)
        u = _dot(hb, wu_ref[...], NN)
        gp_ref[...] = g.astype(MXU)
        up_ref[...] = u.astype(MXU)
        gc, _ = _conv_fwd(g, cg_ref, 3, carry_g[...])
        uc, _ = _conv_fwd(u, cu_ref, 3, carry_u[...])
        carry_g[...] = g[T - SUB:T, :]
        carry_u[...] = u[T - SUB:T, :]
        gc_ref[...] = gc.astype(MXU)
        uc_ref[...] = uc.astype(MXU)
        fb = (gc * _sig(gc) * uc).astype(MXU)
        f_ref[...] = fb
        part = _dot(fb, wd_ref[...], NN)
        if last:
            dv = dp_ref[...] + part
            d_ref[...] = dv
            x3_ref[...] = xv + mod_ref[5:6, :] * dv
        else:
            d_ref[...] = part

    row = lambda i: (i, 0)
    rowD = pl.BlockSpec((T, D), row)
    rowC = pl.BlockSpec((T, FF_CW), row)
    in_specs = [rowD, _full((SUB, D)), _full((SUB, D)),
                pl.BlockSpec((D, FF_CW), lambda i: (0, j)), pl.BlockSpec((D, FF_CW), lambda i: (0, nj + j)),
                pl.BlockSpec((SUB, FF_CW), lambda i: (0, j)), pl.BlockSpec((SUB, FF_CW), lambda i: (0, nj + j)),
                pl.BlockSpec((FF_CW, D), lambda i: (j, 0))]
    half = [jax.ShapeDtypeStruct((L, FF_CW), MXU)] * 5
    args = [x2, modrows, vec, w_up, w_up, cff, cff, w_down]
    if last:
        in_specs.append(rowD)
        args.append(d_prev)
        out_specs = [rowC] * 5 + [rowD, rowD]
        out_shape = half + [jax.ShapeDtypeStruct((L, D), F32), jax.ShapeDtypeStruct((L, D), F32)]
    else:
        out_specs = [rowD] + [rowC] * 5 + [rowD]
        out_shape = [jax.ShapeDtypeStruct((L, D), MXU)] + half + [jax.ShapeDtypeStruct((L, D), F32)]
    return pl.pallas_call(
        body, name="ffn_fwd_last" if last else "ffn_fwd_first", grid=(L // T,),
        in_specs=in_specs, out_specs=out_specs, out_shape=out_shape,
        scratch_shapes=[pltpu.VMEM((SUB, FF_CW), F32), pltpu.VMEM((SUB, FF_CW), F32)],
        compiler_params=_params(1),
    )(*args)


def _ffn_bwd_half(dx3, modrows, gpre, upre, gcv, ucv, cff, w_down, w_up, j, tail):
    assert FF_COLS == 2
    L = dx3.shape[0]
    T = _tile(L, FF_ROWS)
    ni, nj = L // T, FF_COLS
    last = tail is not None

    def body(*refs):
        dx3_ref, mod_ref, gp_ref, up_ref, gc_ref, uc_ref, cg_ref, cu_ref, wd_ref, wg_ref, wu_ref = refs[:11]
        if last:
            (d_ref, x2_ref, vec_ref, dhp_ref, dgp_ref, dup_ref, dx2_ref, accv_ref, dcg_ref, dcu_ref,
             carry_g, carry_u) = refs[11:]
        else:
            dd_ref, dgp_ref, dup_ref, dh_ref, dcg_ref, dcu_ref, carry_g, carry_u = refs[11:]
        i = pl.program_id(0)

        @pl.when(i == 0)
        def _():
            carry_g[...] = jnp.zeros_like(carry_g)
            carry_u[...] = jnp.zeros_like(carry_u)
            dcg_ref[...] = jnp.zeros_like(dcg_ref)
            dcu_ref[...] = jnp.zeros_like(dcu_ref)
            if last:
                accv_ref[...] = jnp.zeros_like(accv_ref)

        dx3v = dx3_ref[...]
        ddb = (mod_ref[5:6, :] * dx3v).astype(MXU)
        if not last:
            dd_ref[...] = ddb
        g, u = gp_ref[...].astype(F32), up_ref[...].astype(F32)
        gc, uc = gc_ref[...].astype(F32), uc_ref[...].astype(F32)
        sg = _sig(gc)
        df = _dot(ddb, wd_ref[...], NT)
        duc = df * (gc * sg)
        dgc = df * uc * (sg * (1.0 + gc * (1.0 - sg)))
        dgs = [dgc] + [_shift_up(dgc, s, carry_g[...]) for s in (1, 2)]
        dus = [duc] + [_shift_up(duc, s, carry_u[...]) for s in (1, 2)]
        for s in range(3):
            dcg_ref[2 - s:3 - s, :] += _sum0(dgs[s] * g)
            dcu_ref[2 - s:3 - s, :] += _sum0(dus[s] * u)
        dg = (cg_ref[2:3, :] * dgs[0] + cg_ref[1:2, :] * dgs[1] + cg_ref[0:1, :] * dgs[2]).astype(MXU)
        du = (cu_ref[2:3, :] * dus[0] + cu_ref[1:2, :] * dus[1] + cu_ref[0:1, :] * dus[2]).astype(MXU)
        carry_g[...] = dgc[0:SUB, :]
        carry_u[...] = duc[0:SUB, :]
        dgp_ref[...] = dg
        dup_ref[...] = du
        dh = _dot(dg, wg_ref[...], NT) + _dot(du, wu_ref[...], NT)
        if last:
            dh = dh + dhp_ref[...]
            accv_ref[0:1, :] += _sum0(dx3v * d_ref[...])
            n, r = _rms(x2_ref[...])
            nw, sc = vec_ref[1:2, :], mod_ref[4:5, :]
            accv_ref[1:2, :] += _sum0(dh)
            accv_ref[2:3, :] += _sum0(dh * n * nw)
            accv_ref[3:4, :] += _sum0(dh * n * (1.0 + sc))
            dx2_ref[...] = _rms_bwd(dh * nw * (1.0 + sc), n, r) + dx3v
        else:
            dh_ref[...] = dh

    row = lambda i: (ni - 1 - i, 0)
    rowD = pl.BlockSpec((T, D), row)
    rowC = pl.BlockSpec((T, FF_CW), row)
    in_specs = [rowD, _full((SUB, D)), rowC, rowC, rowC, rowC,
                pl.BlockSpec((SUB, FF_CW), lambda i: (0, j)), pl.BlockSpec((SUB, FF_CW), lambda i: (0, nj + j)),
                pl.BlockSpec((FF_CW, D), lambda i: (j, 0)),
                pl.BlockSpec((D, FF_CW), lambda i: (0, j)), pl.BlockSpec((D, FF_CW), lambda i: (0, nj + j))]
    args = [dx3, modrows, gpre, upre, gcv, ucv, cff, cff, w_down, w_up, w_up]
    halfb = [jax.ShapeDtypeStruct((L, FF_CW), MXU), jax.ShapeDtypeStruct((L, FF_CW), MXU)]
    dconv = [jax.ShapeDtypeStruct((SUB, FF_CW), F32)] * 2
    if last:
        d, x2, vec, dh_prev = tail
        in_specs += [rowD, rowD, _full((SUB, D)), rowD]
        args += [d, x2, vec, dh_prev]
        out_specs = [rowC, rowC, rowD, _full((SUB, D)), _full((SUB, FF_CW)), _full((SUB, FF_CW))]
        out_shape = halfb + [jax.ShapeDtypeStruct((L, D), F32), jax.ShapeDtypeStruct((SUB, D), F32)] + dconv
    else:
        out_specs = [rowD, rowC, rowC, rowD, _full((SUB, FF_CW)), _full((SUB, FF_CW))]
        out_shape = [jax.ShapeDtypeStruct((L, D), MXU)] + halfb + [jax.ShapeDtypeStruct((L, D), F32)] + dconv
    return pl.pallas_call(
        body, name="ffn_bwd_last" if last else "ffn_bwd_first", grid=(ni,),
        in_specs=in_specs, out_specs=out_specs, out_shape=out_shape,
        scratch_shapes=[pltpu.VMEM((SUB, FF_CW), F32), pltpu.VMEM((SUB, FF_CW), F32)],
        compiler_params=_params(1),
    )(*args)


def _final(x, target, nf):
    L = x.shape[0]
    T = _tile(L, 256)

    def body(x_ref, t_ref, nf_ref, dx_ref, acc_ref):
        @pl.when(pl.program_id(0) == 0)
        def _():
            acc_ref[...] = jnp.zeros_like(acc_ref)

        n, r = _rms(x_ref[...])
        w = nf_ref[0:1, :]
        err = n * w - t_ref[...]
        acc_ref[0:1, :] += (0.5 / D) * _sum0(err * err)
        dy = err * (1.0 / D)
        acc_ref[1:2, :] += _sum0(dy * n)
        dx_ref[...] = _rms_bwd(dy * w, n, r)

    row = lambda i: (i, 0)
    return pl.pallas_call(
        body, name="final_norm_loss", grid=(L // T,),
        in_specs=[pl.BlockSpec((T, D), row), pl.BlockSpec((T, D), row), _full((SUB, D))],
        out_specs=[pl.BlockSpec((T, D), row), _full((SUB, D))],
        out_shape=[jax.ShapeDtypeStruct((L, D), F32), jax.ShapeDtypeStruct((SUB, D), F32)],
        compiler_params=_params(1),
    )(x, target, nf)


def _post_bwd(dx2, y, o, p, modrows, sp, w_out):
    L = dx2.shape[0]
    T = _tile(L, 512)

    def body(dx2_ref, y_ref, o_ref, z_ref, mod_ref, sp_ref, w_ref, dy_ref, do_ref, dz_ref, dya_ref, accv_ref, accs_ref):
        @pl.when(pl.program_id(0) == 0)
        def _():
            accv_ref[...] = jnp.zeros_like(accv_ref)
            accs_ref[...] = jnp.zeros_like(accs_ref)

        dx2v = dx2_ref[...]
        accv_ref[0:1, :] += _sum0(dx2v * y_ref[...])
        dyb = (mod_ref[2:3, :] * dx2v).astype(MXU)
        dy_ref[...] = dyb
        dyc = _dot(dyb, w_ref[...], NT)
        dya_ref[...] = dyc[:, 0:AW]
        ndw = sp_ref[2:3, :]
        z = z_ref[...]
        sgz = _sig(z)
        dsz = sgz * (1.0 + z * (1.0 - sgz))
        dndw = jnp.zeros((1, HD), F32)
        for h in range(H):
            sl = slice(h * HD, (h + 1) * HD)
            n, r = _rms(o_ref[:, sl])
            dyh = dyc[:, AW + h * HD:AW + (h + 1) * HD]
            zh = z[:, sl]
            don = dyh * (zh * sgz[:, sl])
            dz_ref[:, sl] = dyh * (n * ndw) * dsz[:, sl]
            dndw = dndw + _sum0(don * n)
            do_ref[:, sl] = _rms_bwd(don * ndw, n, r)
        accs_ref[0:1, :] += dndw

    row = lambda i: (i, 0)
    zcol = (3 * AW + 3 * H * HD) // (H * HD)
    return pl.pallas_call(
        body, name="post_bwd", grid=(L // T,),
        in_specs=[pl.BlockSpec((T, D), row), pl.BlockSpec((T, D), row), pl.BlockSpec((T, H * HD), row),
                  pl.BlockSpec((T, H * HD), lambda i: (i, zcol)), _full((SUB, D)), _full((SUB, LANES)), _full((D, D))],
        out_specs=[pl.BlockSpec((T, D), row)] + [pl.BlockSpec((T, H * HD), row)] * 3 + [_full((SUB, D)), _full((SUB, LANES))],
        out_shape=[jax.ShapeDtypeStruct((L, D), MXU)] + [jax.ShapeDtypeStruct((L, H * HD), F32)] * 3
        + [jax.ShapeDtypeStruct((SUB, D), F32), jax.ShapeDtypeStruct((SUB, LANES), F32)],
        compiler_params=_params(1),
    )(dx2, y, o, p, modrows, sp, w_out)


def _pre_bwd(p, cub, qcb, dqn, dkn, dvs, dya, dz, dgb, pa, cq, sp):
    L = p.shape[0]
    T = _tile(L, 256)
    ni = L // T
    scale = HD ** -0.5
    w3 = 3 * AW + 3 * H * HD

    def body(pm_ref, cu_ref, qc_ref, ps_ref, dq_ref, dk_ref, dv_ref, dya_ref, dz_ref, dgb_ref, pa_ref, cq_ref, sp_ref,
             dp_ref, dpa_ref, dcq_ref, dsp_ref, carry_u, carry_q):
        i = pl.program_id(0)

        @pl.when(i == 0)
        def _():
            dpa_ref[...] = jnp.zeros_like(dpa_ref)
            dcq_ref[...] = jnp.zeros_like(dcq_ref)
            dsp_ref[...] = jnp.zeros_like(dsp_ref)
            carry_u[...] = jnp.zeros_like(carry_u)
            carry_q[...] = jnp.zeros_like(carry_q)

        a_b, a_c, a_x = pm_ref[:, 0:AW], pm_ref[:, AW:2 * AW], pm_ref[:, 2 * AW:3 * AW]
        u = a_c * a_x
        cu = cu_ref[...].astype(F32)
        yp = a_b * cu
        bd = _blockdiag_mean(AW, A_GROUP)
        ra = lax.rsqrt(_dot_f32(yp * yp, bd, NN, exact="b") + EPS)
        na = yp * ra
        dya = dya_ref[...]
        dpa_ref[3:4, :] += _sum0(dya * na)
        dna = dya * pa_ref[3:4, :]
        dyp = ra * (dna - na * _dot_f32(dna * na, bd, NN, exact="b"))
        dcu = dyp * a_b
        dcs = [dcu] + [_shift_up(dcu, s, carry_u[...]) for s in (1, 2)]
        du = pa_ref[2:3, :] * dcs[0]
        for s in range(3):
            dpa_ref[2 - s:3 - s, :] += _sum0(dcs[s] * u)
            if s:
                du = du + pa_ref[2 - s:3 - s, :] * dcs[s]
        carry_u[...] = dcu[0:SUB, :]
        dp_ref[:, 0:AW] = (dyp * cu).astype(MXU)
        dp_ref[:, AW:2 * AW] = (du * a_x).astype(MXU)
        dp_ref[:, 2 * AW:3 * AW] = (du * a_c).astype(MXU)

        qkv = pm_ref[:, 3 * AW:w3]
        qc = qc_ref[...].astype(F32)
        sg = _sig(qc)
        qs = qc * sg
        parts = []
        for h in range(H):
            q = qs[:, h * HD:(h + 1) * HD]
            rq = lax.rsqrt(jnp.sum(q * q, axis=-1, keepdims=True) + EPS)
            parts.append(_l2_bwd(dq_ref[:, h * HD:(h + 1) * HD] * scale, q * rq, rq))
        for h in range(H):
            k = qs[:, (H + h) * HD:(H + h + 1) * HD]
            rk = lax.rsqrt(jnp.sum(k * k, axis=-1, keepdims=True) + EPS)
            parts.append(_l2_bwd(dk_ref[:, h * HD:(h + 1) * HD], k * rk, rk))
        parts.append(dv_ref[...])
        dqc = jnp.concatenate(parts, axis=-1) * (sg * (1.0 + qc * (1.0 - sg)))
        dqs = [dqc] + [_shift_up(dqc, s, carry_q[...]) for s in (1, 2, 3)]
        dqkv = cq_ref[3:4, :] * dqs[0]
        for s in range(4):
            dcq_ref[3 - s:4 - s, :] += _sum0(dqs[s] * qkv)
            if s:
                dqkv = dqkv + cq_ref[3 - s:4 - s, :] * dqs[s]
        dp_ref[:, 3 * AW:w3] = dqkv.astype(MXU)
        carry_q[...] = dqc[0:SUB, :]
        dp_ref[:, w3:w3 + H * HD] = dz_ref[...].astype(MXU)

        lane, a, xb, beta, g = _gate_small(ps_ref[...], sp_ref)
        dgb = dgb_ref[...]
        dbeta = jnp.where(lane < H, dgb, 0.0)
        dg = jnp.where((lane >= H) & (lane < 2 * H), dgb, 0.0)
        dalpha = dg * a * _sig(xb)
        dsp_ref[0:1, :] += _sum0(dg * g)
        dsp_ref[1:2, :] += _sum0(dalpha)
        dp_ref[:, w3 + H * HD:P_PAD] = (dbeta * beta * (1.0 - beta) + dalpha).astype(MXU)

    row = lambda i: (ni - 1 - i, 0)
    hrow = pl.BlockSpec((T, H * HD), row)
    return pl.pallas_call(
        body, name="pre_bwd", grid=(ni,),
        in_specs=[pl.BlockSpec((T, w3), row), pl.BlockSpec((T, AW), row), pl.BlockSpec((T, 3 * H * HD), row),
                  pl.BlockSpec((T, LANES), lambda i: (ni - 1 - i, (P_PAD - LANES) // LANES)),
                  hrow, hrow, hrow, pl.BlockSpec((T, AW), row), hrow,
                  pl.BlockSpec((T, LANES), row),
                  _full((SUB, AW)), _full((SUB, 3 * H * HD)), _full((SUB, LANES))],
        out_specs=[pl.BlockSpec((T, P_PAD), row), _full((SUB, AW)), _full((SUB, 3 * H * HD)), _full((SUB, LANES))],
        out_shape=[jax.ShapeDtypeStruct((L, P_PAD), MXU), jax.ShapeDtypeStruct((SUB, AW), F32),
                   jax.ShapeDtypeStruct((SUB, 3 * H * HD), F32), jax.ShapeDtypeStruct((SUB, LANES), F32)],
        scratch_shapes=[pltpu.VMEM((SUB, AW), F32), pltpu.VMEM((SUB, 3 * H * HD), F32)],
        compiler_params=_params(1),
    )(p, cub, qcb, p, dqn, dkn, dvs, dya, dz, dgb, pa, cq, sp)


def _in_bwd(dp, w_in, x, dx2, modrows, vec):
    L = x.shape[0]
    T = _tile(L, 512)

    def body(dp_ref, w_ref, x_ref, dx2_ref, mod_ref, vec_ref, dx_ref, accv_ref):
        @pl.when(pl.program_id(0) == 0)
        def _():
            accv_ref[...] = jnp.zeros_like(accv_ref)

        dh = _dot(dp_ref[...], w_ref[...], NT)
        n, r = _rms(x_ref[...])
        nw, sc = vec_ref[0:1, :], mod_ref[1:2, :]
        accv_ref[0:1, :] += _sum0(dh)
        accv_ref[1:2, :] += _sum0(dh * n * nw)
        accv_ref[2:3, :] += _sum0(dh * n * (1.0 + sc))
        dx_ref[...] = _rms_bwd(dh * nw * (1.0 + sc), n, r) + dx2_ref[...]

    row = lambda i: (i, 0)
    return pl.pallas_call(
        body, name="in_bwd", grid=(L // T,),
        in_specs=[pl.BlockSpec((T, P_PAD), row), _full((D, P_PAD)), pl.BlockSpec((T, D), row),
                  pl.BlockSpec((T, D), row), _full((SUB, D)), _full((SUB, D))],
        out_specs=[pl.BlockSpec((T, D), row), _full((SUB, D))],
        out_shape=[jax.ShapeDtypeStruct((L, D), F32), jax.ShapeDtypeStruct((SUB, D), F32)],
        compiler_params=_params(1),
    )(dp, w_in, x, dx2, modrows, vec)


def _wgrad(a, b, tm, tn, name):
    L, m = a.shape
    n = b.shape[1]
    tl = _tile(L, 512)
    tm, tn = _tile(m, tm), _tile(n, tn)
    nl = L // tl

    def body(a_ref, b_ref, o_ref, acc):
        @pl.when(pl.program_id(2) == 0)
        def _():
            acc[...] = jnp.zeros_like(acc)

        acc[...] += _dot(a_ref[...], b_ref[...], TN)

        @pl.when(pl.program_id(2) == nl - 1)
        def _():
            o_ref[...] = acc[...].astype(o_ref.dtype)

    return pl.pallas_call(
        body, name=name, grid=(m // tm, n // tn, nl),
        in_specs=[pl.BlockSpec((tl, tm), lambda i, j, l: (l, i)), pl.BlockSpec((tl, tn), lambda i, j, l: (l, j))],
        out_specs=pl.BlockSpec((tm, tn), lambda i, j, l: (i, j)),
        out_shape=jax.ShapeDtypeStruct((m, n), MXU), scratch_shapes=[pltpu.VMEM((tm, tn), F32)],
        compiler_params=_params(3),
    )(a, b)


def _wgrad_cols(a, b, tm, n_shard, wpad, count, name):
    L, m = a.shape
    n = b.shape[1]
    tl = _tile(L, 512)
    tm = _tile(m, tm)
    nl = L // tl
    wins = _shard_windows(n_shard, count)
    assert all(a_ * LANES + win <= n for a_, _, win in wins), (wins, n)

    def body(a_ref, b_ref, o_ref, acc):
        @pl.when(pl.program_id(1) == 0)
        def _():
            acc[...] = jnp.zeros_like(acc)

        acc[...] += _dot(a_ref[...], b_ref[...], TN)

        @pl.when(pl.program_id(1) == nl - 1)
        def _():
            for k, (a_, s, win) in enumerate(wins):
                xk = acc[:, a_ * LANES:a_ * LANES + win]
                if s:
                    xk = pltpu.roll(xk, win - s, 1)
                o_ref[k] = _fit_lanes(xk, wpad).astype(o_ref.dtype)

    return pl.pallas_call(
        body, name=name, grid=(m // tm, nl),
        in_specs=[pl.BlockSpec((tl, tm), lambda i, l: (l, i)), pl.BlockSpec((tl, n), lambda i, l: (l, 0))],
        out_specs=pl.BlockSpec((count, tm, wpad), lambda i, l: (0, i, 0)),
        out_shape=jax.ShapeDtypeStruct((count, m, wpad), MXU),
        scratch_shapes=[pltpu.VMEM((tm, n), F32)],
        compiler_params=_params(2),
    )(a, b)


def _adamw(w, g, m, v, name):
    r, n = w.shape
    tr = _tile(r, 512)
    bc1 = 1.0 - ADAM_B1 ** ADAM_STEP
    bc2 = 1.0 - ADAM_B2 ** ADAM_STEP

    def body(w_ref, g_ref, m_ref, v_ref, d_ref, nm_ref, nv_ref):
        gv = g_ref[...]
        nm = ADAM_B1 * m_ref[...] + (1.0 - ADAM_B1) * gv
        nv = ADAM_B2 * v_ref[...] + (1.0 - ADAM_B2) * (gv * gv)
        nm_ref[...] = nm
        nv_ref[...] = nv
        d_ref[...] = -ADAM_LR * ((nm / bc1) / (jnp.sqrt(nv / bc2) + ADAM_EPS) + ADAM_WD * w_ref[...])

    spec = pl.BlockSpec((tr, n), lambda i: (i, 0))
    return pl.pallas_call(
        body, name=name, grid=(r // tr,), in_specs=[spec] * 4, out_specs=[spec] * 3,
        out_shape=[jax.ShapeDtypeStruct((r, n), F32)] * 3, compiler_params=_params(1),
    )(w, g, m, v)


def _rows8(rows, width):
    out = jnp.zeros((SUB, width), F32)
    for r, vrow in enumerate(rows):
        out = out.at[r, :vrow.shape[0]].set(vrow)
    return out


def _at_lanes(v4, start):
    return jnp.zeros((LANES,), F32).at[start:start + v4.shape[0]].set(v4)


def _pad_rows(flat, mult):
    n = flat.shape[0]
    pad = (-n) % mult
    return jnp.pad(flat, (0, pad)) if pad else flat


IN_PAD = 512
UP_PAD = 768


def _local_fwd_bwd(x, target, mod_full, small_w, full_w, on_grads=None):
    norm1_w, norm2_w, norm_a_w, a_log, dt_bias, norm_dn_w, norm_f_w = small_w
    w_in_f, w_out_f, w_up_f, w_down_f, conv_a_f, conv_q_f, conv_f_f = full_w

    def layer_params(i):
        modrows = jnp.concatenate([mod_full[i], jnp.zeros((SUB - N_MOD, D), F32)], axis=0)
        vec = _rows8([norm1_w[i], norm2_w[i]], D)
        pa = _rows8([conv_a_f[i, 0], conv_a_f[i, 1], conv_a_f[i, 2], norm_a_w[i]], AW)
        cq = _rows8([conv_q_f[i, k] for k in range(4)], 3 * H * HD)
        sp = _rows8([_at_lanes(a_log[i], H), _at_lanes(dt_bias[i], H), norm_dn_w[i]], LANES)
        cff = _rows8([conv_f_f[i, k] for k in range(3)], 2 * DFF)
        return modrows, vec, pa, cq, sp, cff

    saved = []
    xi = x
    for i in range(DEPTH):
        modrows, vec, pa, cq, sp, cff = layer_params(i)
        p, h1 = _in_proj(xi, modrows, vec, w_in_f[i])
        qn, kn, vs, gb, ya, cub, qcb = _pre_fwd(p, pa, cq, sp)
        o, states, tinvs = _gdr_fwd(qn, kn, vs, gb)
        y, x2, yb = _post_fwd(o, p, ya, xi, modrows, sp, w_out_f[i])
        h2, gp0, up0, gc0, uc0, f0, d0 = _ffn_fwd_half(x2, modrows, vec, w_up_f[i], cff, w_down_f[i], 0, None)
        gp1, up1, gc1, uc1, f1, dff, x3 = _ffn_fwd_half(x2, modrows, vec, w_up_f[i], cff, w_down_f[i], 1, d0)
        saved.append(dict(x=xi, p=p, h1=h1, qn=qn, kn=kn, vs=vs, gb=gb, ya=ya, cub=cub, qcb=qcb, o=o, states=states,
                          tinvs=tinvs, y=y, x2=x2, yb=yb,
                          h2=h2, gpre=(gp0, gp1), upre=(up0, up1), gc=(gc0, gc1), uc=(uc0, uc1), f=(f0, f1), d=dff))
        xi = x3

    dx, facc = _final(xi, target, _rows8([norm_f_w], D))
    loss_local = jnp.sum(facc[0])
    d_norm_f = facc[1]

    gw_in, gw_out, gw_up, gw_down = [None] * DEPTH, [None] * DEPTH, [None] * DEPTH, [None] * DEPTH
    g_small = [None] * DEPTH
    for i in reversed(range(DEPTH)):
        s = saved[i]
        modrows, vec, pa, cq, sp, cff = layer_params(i)
        dd, dgp0, dup0, dh0, dcg0, dcu0 = _ffn_bwd_half(dx, modrows, s["gpre"][0], s["upre"][0], s["gc"][0], s["uc"][0],
                                                        cff, w_down_f[i], w_up_f[i], 0, None)
        dgp1, dup1, dx2, accf, dcg1, dcu1 = _ffn_bwd_half(dx, modrows, s["gpre"][1], s["upre"][1], s["gc"][1], s["uc"][1],
                                                          cff, w_down_f[i], w_up_f[i], 1, (s["d"], s["x2"], vec, dh0))
        n_up, up_pad = 2 * DFF // N_DEV, UP_PAD
        gw_up[i] = jnp.concatenate([_wgrad_cols(s["h2"], t, 1024, n_up, up_pad, FF_CW // n_up, "wgrad_up")
                                    for t in (dgp0, dgp1, dup0, dup1)], axis=0)
        gw_down[i] = jnp.concatenate([_wgrad(s["f"][0], dd, FF_CW, 1024, "wgrad_down"),
                                      _wgrad(s["f"][1], dd, FF_CW, 1024, "wgrad_down")],
                                     axis=0).reshape(N_DEV, DFF // N_DEV, D)
        if on_grads is not None:
            on_grads(i, "ffn", [gw_up[i], gw_down[i]])
        dy, do, dz, dya, accp, accs = _post_bwd(dx2, s["y"], s["o"], s["p"], modrows, sp, w_out_f[i])
        gw_out[i] = jnp.concatenate([_wgrad(s["ya"], dy, 512, 1024, "wgrad_out"),
                                     _wgrad(s["yb"], dy, 512, 1024, "wgrad_out")], axis=0).reshape(N_DEV, D // N_DEV, D)
        dqn, dkn, dvs, dgb = _gdr_bwd(s["qn"], s["kn"], s["vs"], s["gb"], s["states"], s["tinvs"], do)
        dp, dpa, dcq, dsp = _pre_bwd(s["p"], s["cub"], s["qcb"], dqn, dkn, dvs, dya, dz, dgb, pa, cq, sp)
        gw_in[i] = _wgrad_cols(s["h1"], dp, 1024, P_IN // N_DEV, IN_PAD, N_DEV, "wgrad_in")
        dx, acci = _in_bwd(dp, w_in_f[i], s["x"], dx2, modrows, vec)
        dconv_ff = jnp.concatenate([dcg0, dcg1, dcu0, dcu1], axis=1)[0:3]
        dmod = jnp.stack([acci[0], acci[1], accp[0], accf[1], accf[2], accf[0]])
        g_small[i] = dict(norm1=acci[2], norm2=accf[3], norm_a=dpa[3], a_log=dsp[0, H:2 * H], dt_bias=dsp[1, H:2 * H],
                          norm_dn=accs[0], conv_a=dpa[0:3], conv_qkv=dcq[0:4], conv_ff=dconv_ff, dmod=dmod.reshape(-1))
        if on_grads is not None:
            on_grads(i, "mix", [gw_in[i], gw_out[i]])
    return loss_local, dx, gw_in, gw_out, gw_up, gw_down, g_small, d_norm_f


def kernel(x, c, ada_w, ada_b, norm1_w, w_in, conv_a_w, norm_a_w, conv_qkv_w, a_log, dt_bias, norm_dn_w, w_out, norm2_w, w_up, conv_ff_w, w_down, norm_f_w, loss_target, m_ada_w, m_ada_b, m_norm1_w, m_w_in, m_conv_a_w, m_norm_a_w, m_conv_qkv_w, m_a_log, m_dt_bias, m_norm_dn_w, m_w_out, m_norm2_w, m_w_up, m_conv_ff_w, m_w_down, m_norm_f_w, v_ada_w, v_ada_b, v_norm1_w, v_w_in, v_conv_a_w, v_norm_a_w, v_conv_qkv_w, v_a_log, v_dt_bias, v_norm_dn_w, v_w_out, v_norm2_w, v_w_up, v_conv_ff_w, v_w_down, v_norm_f_w):
    ax, ay, ac = lax.axis_index("x"), lax.axis_index("y"), lax.axis_index("c")
    me = 4 * ax + 2 * ay + ac
    x = x[0]
    target = loss_target[0]
    n_in, n_up = P_IN // N_DEV, 2 * DFF // N_DEV

    def lane_pad(t, width):
        return jnp.pad(t.astype(MXU), ((0, 0), (0, 0), (0, width - t.shape[-1])))

    conv_blob = _pad_rows(jnp.concatenate([t.reshape(-1) for t in (conv_a_w, conv_qkv_w, conv_ff_w)]),
                          SUB * LANES).reshape(-1, LANES)
    c_rows = jnp.zeros((SUB, D), F32).at[0].set(c[0])
    send = [lane_pad(w_in, IN_PAD), w_out.astype(MXU), lane_pad(w_up, UP_PAD), w_down.astype(MXU)]
    got = [None] * DEPTH
    g_in0, g_conv, g_c = _all_gather([send[0][0], conv_blob, c_rows], "gather_weights", in_vmem=False)
    shards, _ = lax.optimization_barrier(([t[0] for t in send[1:]], g_c))
    got[0] = [g_in0] + _all_gather_async(shards, "gather_weights_l0", collective_id=0)
    for i in range(1, DEPTH):
        shards, _ = lax.optimization_barrier(([t[i] for t in send], g_c))
        got[i] = _all_gather_async(shards, "gather_weights_l%d" % i, collective_id=i)
    w_in_f = [_interleave_cols(g[0][:, None], n_in, P_PAD, "interleave_w_in")[0] for g in got]
    w_up_f = [_interleave_cols(g[2][:, None], n_up, 2 * DFF, "interleave_w_up")[0] for g in got]
    w_out_f = [g[1].reshape(D, D) for g in got]
    w_down_f = [g[3].reshape(DFF, D) for g in got]
    sg = g_conv.reshape(N_DEV, -1)
    o1 = conv_a_w.size
    o2 = o1 + conv_qkv_w.size
    o3 = o2 + conv_ff_w.size
    conv_a_f = sg[:, 0:o1].reshape(N_DEV, DEPTH, 3, AW // N_DEV).transpose(1, 2, 0, 3).reshape(DEPTH, 3, AW)
    conv_q_f = sg[:, o1:o2].reshape(N_DEV, DEPTH, 4, 3 * H * HD // N_DEV).transpose(1, 2, 0, 3).reshape(DEPTH, 4, 3 * H * HD)
    conv_f_f = sg[:, o2:o3].reshape(N_DEV, DEPTH, 3, n_up).transpose(1, 2, 0, 3).reshape(DEPTH, 3, 2 * DFF)

    c_all = jnp.concatenate([g_c[:, 0], jnp.zeros((16 - N_DEV, D), F32)], axis=0)
    n_ada = N_MOD * D // N_DEV
    ada_b_cols = lax.dynamic_slice_in_dim(ada_b, me * n_ada, n_ada, axis=1)[:, None, :]
    mod_sh = _mod_fwd(c_all, ada_w, ada_b_cols)
    mod_all = _all_gather([mod_sh.reshape(DEPTH * 16, n_ada)], "gather_mod", in_vmem=True)[0]
    mod_all = mod_all.reshape(N_DEV, DEPTH, 16, n_ada)
    mod_mine = lax.dynamic_index_in_dim(mod_all, me, axis=2, keepdims=False)
    mod_full = mod_mine.transpose(1, 0, 2).reshape(DEPTH, N_MOD, D)

    tags = ["w_in", "w_out", "w_up", "w_down"]
    received = [dict() for _ in range(DEPTH)]

    def on_grads(i, part, gs_i):
        first_id = DEPTH if part == "ffn" else 2 * DEPTH
        got_i = _rs_exchange_async(gs_i, "rs_exchange_%s_l%d" % (part, i), collective_id=first_id + i)
        received[i].update(zip(("w_up", "w_down") if part == "ffn" else ("w_in", "w_out"), got_i))

    loss_local, dx, _, _, _, _, g_small, d_norm_f = _local_fwd_bwd(
        x, target, mod_full, (norm1_w, norm2_w, norm_a_w, a_log, dt_bias, norm_dn_w, norm_f_w),
        (w_in_f, w_out_f, w_up_f, w_down_f, conv_a_f, conv_q_f, conv_f_f), on_grads)
    loss = lax.psum(loss_local, ("x", "y", "c"))
    grad_x = dx[None]

    keys = ["dmod", "norm1", "norm2", "norm_a", "a_log", "dt_bias", "norm_dn", "conv_a", "conv_qkv", "conv_ff"]
    stacked = {k: jnp.stack([g_small[i][k] for i in range(DEPTH)]) for k in keys}
    flat_parts = [stacked[k].reshape(-1) for k in keys] + [d_norm_f]
    sizes = [int(t.shape[0]) for t in flat_parts]
    sflat = _pad_rows(jnp.concatenate(flat_parts), SUB * LANES).reshape(-1, LANES)
    sall = _all_gather([sflat], "gather_small_grads", in_vmem=True)[0]
    ssum = _sum_devices(sall).reshape(-1)
    so = [0]
    for sz in sizes:
        so.append(so[-1] + sz)
    red = {k: ssum[so[n]:so[n + 1]].reshape(stacked[k].shape) for n, k in enumerate(keys)}
    g_norm_f = ssum[so[len(keys)]:so[len(keys) + 1]]
    dmod_all = sall[:, 0:sizes[0] // LANES, :].reshape(N_DEV, DEPTH, N_MOD * D)

    g_ada_b = red["dmod"].reshape(DEPTH, N_MOD * D)
    dmod_cols = lax.dynamic_slice_in_dim(dmod_all, me * n_ada, n_ada, axis=2).transpose(1, 0, 2)
    dmod_cols = jnp.concatenate([dmod_cols, jnp.zeros((DEPTH, 16 - N_DEV, n_ada), F32)], axis=1)
    g_ada_w = _mod_bwd(c_all, dmod_cols)
    g_conv_a = lax.dynamic_slice_in_dim(red["conv_a"], me * (AW // N_DEV), AW // N_DEV, axis=2)
    g_conv_qkv = lax.dynamic_slice_in_dim(red["conv_qkv"], me * (3 * H * HD // N_DEV), 3 * H * HD // N_DEV, axis=2)
    g_conv_ff = lax.dynamic_slice_in_dim(red["conv_ff"], me * n_up, n_up, axis=2)

    mine = [jnp.stack([_rs_sum(received[i][t], "rs_sum_" + t) for i in range(DEPTH)]) for t in tags]
    g_w_in = mine[0][:, :, :n_in]
    g_w_out = mine[1]
    g_w_up = mine[2][:, :, :n_up]
    g_w_down = mine[3]

    grads = dict(ada_w=g_ada_w, ada_b=g_ada_b, norm1_w=red["norm1"], w_in=g_w_in, conv_a_w=g_conv_a,
                 norm_a_w=red["norm_a"], conv_qkv_w=g_conv_qkv, a_log=red["a_log"], dt_bias=red["dt_bias"],
                 norm_dn_w=red["norm_dn"], w_out=g_w_out, norm2_w=red["norm2"], w_up=g_w_up, conv_ff_w=g_conv_ff,
                 w_down=g_w_down, norm_f_w=g_norm_f)
    weights = dict(ada_w=ada_w, ada_b=ada_b, norm1_w=norm1_w, w_in=w_in, conv_a_w=conv_a_w, norm_a_w=norm_a_w,
                   conv_qkv_w=conv_qkv_w, a_log=a_log, dt_bias=dt_bias, norm_dn_w=norm_dn_w, w_out=w_out,
                   norm2_w=norm2_w, w_up=w_up, conv_ff_w=conv_ff_w, w_down=w_down, norm_f_w=norm_f_w)
    ms = dict(ada_w=m_ada_w, ada_b=m_ada_b, norm1_w=m_norm1_w, w_in=m_w_in, conv_a_w=m_conv_a_w, norm_a_w=m_norm_a_w,
              conv_qkv_w=m_conv_qkv_w, a_log=m_a_log, dt_bias=m_dt_bias, norm_dn_w=m_norm_dn_w, w_out=m_w_out,
              norm2_w=m_norm2_w, w_up=m_w_up, conv_ff_w=m_conv_ff_w, w_down=m_w_down, norm_f_w=m_norm_f_w)
    vs_ = dict(ada_w=v_ada_w, ada_b=v_ada_b, norm1_w=v_norm1_w, w_in=v_w_in, conv_a_w=v_conv_a_w, norm_a_w=v_norm_a_w,
               conv_qkv_w=v_conv_qkv_w, a_log=v_a_log, dt_bias=v_dt_bias, norm_dn_w=v_norm_dn_w, w_out=v_w_out,
               norm2_w=v_norm2_w, w_up=v_w_up, conv_ff_w=v_conv_ff_w, w_down=v_w_down, norm_f_w=v_norm_f_w)
    names = list(weights)
    big_names = ["ada_w", "w_in", "w_out", "w_up", "w_down"]
    delta, new_m, new_v = {}, {}, {}
    for n in big_names:
        shp = weights[n].shape
        two = lambda t: t.reshape(-1, shp[-1])
        dl, nm, nv = _adamw(two(weights[n]), two(grads[n]), two(ms[n]), two(vs_[n]), "adamw_" + n)
        delta[n], new_m[n], new_v[n] = dl.reshape(shp), nm.reshape(shp), nv.reshape(shp)
    small_names = [n for n in names if n not in big_names]

    def pack(dct):
        return _pad_rows(jnp.concatenate([dct[n].reshape(-1) for n in small_names]), SUB * LANES).reshape(-1, LANES)

    dl, nm, nv = _adamw(pack(weights), pack(grads), pack(ms), pack(vs_), "adamw_small")
    off = 0
    for n in small_names:
        sz, shp = weights[n].size, weights[n].shape
        delta[n] = dl.reshape(-1)[off:off + sz].reshape(shp)
        new_m[n] = nm.reshape(-1)[off:off + sz].reshape(shp)
        new_v[n] = nv.reshape(-1)[off:off + sz].reshape(shp)
        off += sz

    return (loss, grad_x, *[grads[n] for n in names], *[delta[n] for n in names],
            *[new_m[n] for n in names], *[new_v[n] for n in names])
```

```python
import functools
import math

import jax
import jax.numpy as jnp
from jax import lax
from jax.experimental import pallas as pl
from jax.experimental.pallas import tpu as pltpu
from jax.experimental.pallas import tpu_sc as plsc

F32 = jnp.float32
MXU = jnp.bfloat16

D = 1024
DEPTH = 4
N_MOD = 6
AW = 512
A_GROUP = 64
H = 4
HD = 128
CK = 64
DFF = 2816
P_IN = 3592
P_PAD = 3712
EPS = 1e-6
N_DEV = 8
LANES = 128
SUB = 8
VMEM_LIMIT = 56 * 1024 * 1024

ADAM_LR, ADAM_B1, ADAM_B2, ADAM_EPS, ADAM_WD, ADAM_STEP = 0.001, 0.9, 0.999, 1e-08, 0.01, 10

NN = ((1,), (0,))
NT = ((1,), (1,))
TN = ((0,), (0,))
HI = lax.Precision.HIGHEST
MESH = pl.DeviceIdType.MESH


def _dot(a, b, dims, prec=None):
    if prec is None:
        a = a.astype(MXU) if a.dtype == F32 else a
        b = b.astype(MXU) if b.dtype == F32 else b
    return lax.dot_general(a, b, (dims, ((), ())), precision=prec, preferred_element_type=F32)


def _params(n_grid=0, limit=VMEM_LIMIT):
    sem = ("arbitrary",) * n_grid if n_grid else None
    return pltpu.CompilerParams(dimension_semantics=sem, vmem_limit_bytes=limit)


def _tile(n, want):
    if n <= want:
        return n
    t = want - want % SUB
    while n % t:
        t -= SUB
    assert t > 0, (n, want)
    return t


def _full(shape):
    nd = len(shape)
    return pl.BlockSpec(shape, lambda *_: (0,) * nd)


def _sig(x):
    return jax.nn.sigmoid(x)


def _rms(x):
    r = lax.rsqrt(jnp.mean(x * x, axis=-1, keepdims=True) + EPS)
    return x * r, r


def _rms_bwd(dn, n, r):
    return r * (dn - n * jnp.mean(dn * n, axis=-1, keepdims=True))


def _l2_bwd(dn, n, r):
    return r * (dn - n * jnp.sum(dn * n, axis=-1, keepdims=True))


def _sum0(x):
    return jnp.sum(x, axis=0, keepdims=True)


def _shift_down(x, s, halo):
    ext = jnp.concatenate([halo, x], axis=0)
    return pltpu.roll(ext, s, 0)[SUB:, :]


def _shift_up(x, s, halo):
    t = x.shape[0]
    ext = jnp.concatenate([x, halo], axis=0)
    return pltpu.roll(ext, t + SUB - s, 0)[:t, :]


def _conv_fwd(x, w_ref, width, halo):
    sh = [x] + [_shift_down(x, s, halo) for s in range(1, width)]
    out = w_ref[width - 1:width, :] * sh[0]
    for s in range(1, width):
        out = out + w_ref[width - 1 - s:width - s, :] * sh[s]
    return out, sh


def _blockdiag_mean(n, group):
    r = lax.shift_right_logical(lax.broadcasted_iota(jnp.int32, (n, n), 0), int(math.log2(group)))
    c = lax.shift_right_logical(lax.broadcasted_iota(jnp.int32, (n, n), 1), int(math.log2(group)))
    return jnp.where(r == c, 1.0 / group, 0.0).astype(F32)


def _softplus(x):
    return jnp.maximum(x, 0.0) + jnp.log(1.0 + jnp.exp(-jnp.abs(x)))


def _my_place():
    return lax.axis_index("x"), lax.axis_index("y"), lax.axis_index("c")


def _all_gather(shards, name, in_vmem):
    nt = len(shards)

    def body(*refs):
        x_refs, out_refs = refs[:nt], refs[nt:2 * nt]
        send_sems, recv_sems, local_sems = refs[2 * nt:]
        x, y, c = _my_place()
        me, sibling = (x, y, c), (x, y, 1 - c)
        chips = [(1 - x, y), (x, 1 - y), (1 - x, 1 - y)]
        everything = []
        for t in range(nt):
            x_ref, out_ref = x_refs[t], out_refs[t]

            def blk(px, py, pc, out_ref=out_ref):
                return out_ref.at[4 * px + 2 * py + pc]

            def copy(k, block, to, src=None, t=t, blk=blk):
                return pltpu.make_async_remote_copy(
                    src_ref=blk(*block) if src is None else src, dst_ref=blk(*block),
                    send_sem=send_sems.at[7 * t + k], recv_sem=recv_sems.at[7 * t + k], device_id=to, device_id_type=MESH)

            mine = pltpu.make_async_copy(x_ref, blk(*me), local_sems.at[t])
            mine.start()
            first = [copy(0, me, sibling, src=x_ref)]
            first += [copy(1 + j, me, (*chip, c), src=x_ref) for j, chip in enumerate(chips)]
            for cp in first:
                cp.start()
            everything.append((copy, mine, first))
        sends = []
        for copy, mine, first in everything:
            passed = [copy(4 + j, (*chip, c), sibling) for j, chip in enumerate(chips)]
            for j, chip in enumerate(chips):
                copy(1 + j, (*chip, c), me).wait_recv()
                passed[j].start()
            sends += first + passed
        for copy, mine, first in everything:
            copy(0, sibling, me).wait_recv()
            for j, chip in enumerate(chips):
                copy(4 + j, (*chip, 1 - c), me).wait_recv()
        for cp in sends:
            cp.wait_send()
        for copy, mine, first in everything:
            mine.wait()

    space = pltpu.VMEM if in_vmem else pl.ANY
    return pl.pallas_call(
        body, name=name,
        out_shape=[jax.ShapeDtypeStruct((N_DEV,) + s.shape, s.dtype) for s in shards],
        in_specs=[pl.BlockSpec(memory_space=space)] * nt,
        out_specs=[pl.BlockSpec(memory_space=space)] * nt,
        scratch_shapes=[pltpu.SemaphoreType.DMA((7 * nt,)), pltpu.SemaphoreType.DMA((7 * nt,)),
                        pltpu.SemaphoreType.DMA((nt,))],
        compiler_params=pltpu.CompilerParams(vmem_limit_bytes=VMEM_LIMIT),
    )(*shards)


def _all_gather_async(shards, name, collective_id):
    nt = len(shards)
    hbm = pltpu.MemorySpace.HBM
    x_refs = [jax.new_ref(s, memory_space=hbm) for s in shards]
    out_refs = [jax.empty_ref(jax.ShapeDtypeStruct((N_DEV,) + s.shape, s.dtype), memory_space=hbm) for s in shards]

    @pl.kernel(mesh=plsc.ScalarSubcoreMesh(axis_name="sequencer", num_cores=1), name=name,
               scratch_types=(pltpu.SemaphoreType.DMA((7 * nt,)), pltpu.SemaphoreType.DMA((7 * nt,)),
                              pltpu.SemaphoreType.DMA((nt,))),
               compiler_params=pltpu.CompilerParams(collective_id=collective_id))
    def launch(send_sems, recv_sems, local_sems):
        x, y, c = _my_place()
        me, sibling = (x, y, c), (x, y, 1 - c)
        chips = [(1 - x, y), (x, 1 - y), (1 - x, 1 - y)]
        barrier = pltpu.get_barrier_semaphore()
        for peer in [sibling] + [(*chip, c) for chip in chips]:
            pl.semaphore_signal(barrier, inc=1, device_id=peer, device_id_type=MESH)
        pl.semaphore_wait(barrier, 4)
        everything = []
        for t in range(nt):
            x_ref, out_ref = x_refs[t], out_refs[t]

            def blk(px, py, pc, out_ref=out_ref):
                return out_ref.at[4 * px + 2 * py + pc]

            def copy(k, block, to, src=None, t=t, blk=blk):
                return pltpu.make_async_remote_copy(
                    src_ref=blk(*block) if src is None else src, dst_ref=blk(*block),
                    send_sem=send_sems.at[7 * t + k], recv_sem=recv_sems.at[7 * t + k], device_id=to, device_id_type=MESH)

            mine = pltpu.make_async_copy(x_ref, blk(*me), local_sems.at[t])
            mine.start()
            first = [copy(0, me, sibling, src=x_ref)]
            first += [copy(1 + j, me, (*chip, c), src=x_ref) for j, chip in enumerate(chips)]
            for cp in first:
                cp.start()
            everything.append((copy, mine, first))
        sends = []
        for copy, mine, first in everything:
            passed = [copy(4 + j, (*chip, c), sibling) for j, chip in enumerate(chips)]
            for j, chip in enumerate(chips):
                copy(1 + j, (*chip, c), me).wait_recv()
                passed[j].start()
            sends += first + passed
        for copy, mine, first in everything:
            copy(0, sibling, me).wait_recv()
            for j, chip in enumerate(chips):
                copy(4 + j, (*chip, 1 - c), me).wait_recv()
        for cp in sends:
            cp.wait_send()
        for copy, mine, first in everything:
            mine.wait()

    launch()
    return [r[...] for r in out_refs]


def _rs_exchange_async(srcs, name, collective_id):
    nt = len(srcs)
    hbm = pltpu.MemorySpace.HBM
    src_refs = [jax.new_ref(s, memory_space=hbm) for s in srcs]
    out_refs = [jax.empty_ref(jax.ShapeDtypeStruct(s.shape, s.dtype), memory_space=hbm) for s in srcs]
    flips = [(fx, fy, fc) for fx in (0, 1) for fy in (0, 1) for fc in (0, 1)][1:]

    @pl.kernel(mesh=plsc.ScalarSubcoreMesh(axis_name="sequencer", num_cores=1), name=name,
               scratch_types=(pltpu.SemaphoreType.DMA((7 * nt,)), pltpu.SemaphoreType.DMA((7 * nt,)),
                              pltpu.SemaphoreType.DMA((nt,))),
               compiler_params=pltpu.CompilerParams(collective_id=collective_id))
    def launch(send_sems, recv_sems, local_sems):
        x, y, c = _my_place()
        me = 4 * x + 2 * y + c
        peers = [(1 - x if fx else x, 1 - y if fy else y, 1 - c if fc else c) for fx, fy, fc in flips]
        barrier = pltpu.get_barrier_semaphore()
        for peer in peers:
            pl.semaphore_signal(barrier, inc=1, device_id=peer, device_id_type=MESH)
        pl.semaphore_wait(barrier, len(peers))
        own = [pltpu.make_async_copy(src_refs[t].at[me], out_refs[t].at[me], local_sems.at[t]) for t in range(nt)]
        copies = [pltpu.make_async_remote_copy(
            src_ref=src_refs[t].at[4 * px + 2 * py + pc], dst_ref=out_refs[t].at[me],
            send_sem=send_sems.at[7 * t + f], recv_sem=recv_sems.at[7 * t + f],
            device_id=(px, py, pc), device_id_type=MESH) for t in range(nt) for f, (px, py, pc) in enumerate(peers)]
        for cp in own + copies:
            cp.start()
        for cp in copies + own:
            cp.wait()

    launch()
    return [r[...] for r in out_refs]


def _rs_sum(recv, name):
    _, r, n = recv.shape
    tr = _tile(r, 512)

    def body(r_ref, o_ref):
        s = r_ref[0].astype(F32)
        for k in range(1, N_DEV):
            s = s + r_ref[k].astype(F32)
        o_ref[...] = s

    return pl.pallas_call(
        body, name=name, grid=(r // tr,),
        in_specs=[pl.BlockSpec((N_DEV, tr, n), lambda i: (0, i, 0))],
        out_specs=pl.BlockSpec((tr, n), lambda i: (i, 0)),
        out_shape=jax.ShapeDtypeStruct((r, n), F32), compiler_params=_params(1),
    )(recv)


def _shard_windows(n_shard, count, first=0):
    out = []
    for k in range(first, first + count):
        off = n_shard * k
        a, s = off // LANES, off % LANES
        out.append((a, s, -(-(s + n_shard) // LANES) * LANES))
    return out


def _fit_lanes(x, width):
    have = x.shape[1]
    if have < width:
        return jnp.concatenate([x, jnp.zeros((x.shape[0], width - have), x.dtype)], axis=-1)
    return x[:, :width]


def _interleave_cols(g, n_shard, w_out, name):
    nd, nl, rows, wpad = g.shape
    rb = _tile(rows, 256)
    wins = _shard_windows(n_shard, nd)

    def body(g_ref, o_ref, acc):
        acc[...] = jnp.zeros_like(acc)
        for k, (a, s, win) in enumerate(wins):
            xk = _fit_lanes(g_ref[k].astype(F32), win)
            if s:
                xk = pltpu.roll(xk, s, 1)
            acc[:, a * LANES:a * LANES + win] += xk
        o_ref[...] = acc[...].astype(o_ref.dtype)

    return pl.pallas_call(
        body, name=name, grid=(nl, rows // rb),
        in_specs=[pl.BlockSpec((nd, None, rb, wpad), lambda l, i: (0, l, i, 0))],
        out_specs=pl.BlockSpec((None, rb, w_out), lambda l, i: (l, i, 0)),
        out_shape=jax.ShapeDtypeStruct((nl, rows, w_out), g.dtype),
        scratch_shapes=[pltpu.VMEM((rb, w_out), F32)],
        compiler_params=_params(2),
    )(g)


def _sum_devices(g):
    _, r, n = g.shape

    def body(g_ref, o_ref):
        s = g_ref[0]
        for t in range(1, N_DEV):
            s = s + g_ref[t]
        o_ref[...] = s

    return pl.pallas_call(
        body, name="sum_devices", out_shape=jax.ShapeDtypeStruct((r, n), F32),
        in_specs=[pl.BlockSpec(memory_space=pltpu.VMEM)], out_specs=pl.BlockSpec(memory_space=pltpu.VMEM),
        compiler_params=pltpu.CompilerParams(vmem_limit_bytes=VMEM_LIMIT),
    )(g)


def _mod_fwd(c_all, ada_w, ada_b_cols):
    nl, _, nc = ada_w.shape

    def body(c_ref, w_ref, b_ref, o_ref):
        cv = c_ref[...]
        act = (cv * _sig(cv)).astype(MXU)
        o_ref[...] = _dot(act, w_ref[...].astype(MXU), NN) + b_ref[...]

    return pl.pallas_call(
        body, name="mod_fwd", grid=(nl,),
        in_specs=[_full((16, D)), pl.BlockSpec((None, D, nc), lambda i: (i, 0, 0)),
                  pl.BlockSpec((None, 1, nc), lambda i: (i, 0, 0))],
        out_specs=pl.BlockSpec((None, 16, nc), lambda i: (i, 0, 0)),
        out_shape=jax.ShapeDtypeStruct((nl, 16, nc), F32), compiler_params=_params(1),
    )(c_all, ada_w, ada_b_cols)


def _mod_bwd(c_all, dmod_cols):
    nl, _, nc = dmod_cols.shape

    def body(c_ref, d_ref, o_ref):
        cv = c_ref[...]
        act = (cv * _sig(cv)).astype(MXU)
        o_ref[...] = _dot(act, d_ref[...].astype(MXU), TN)

    return pl.pallas_call(
        body, name="mod_bwd", grid=(nl,),
        in_specs=[_full((16, D)), pl.BlockSpec((None, 16, nc), lambda i: (i, 0, 0))],
        out_specs=pl.BlockSpec((None, D, nc), lambda i: (i, 0, 0)),
        out_shape=jax.ShapeDtypeStruct((nl, D, nc), F32), compiler_params=_params(1),
    )(c_all, dmod_cols)


def _gate_small(s, sp_ref):
    lane = lax.broadcasted_iota(jnp.int32, s.shape, 1)
    a = -jnp.exp(sp_ref[0:1, :])
    xb = s + sp_ref[1:2, :]
    beta = _sig(s)
    g = a * _softplus(xb)
    return lane, a, xb, beta, g


def _in_pre_fwd(x, modrows, vec, w_in, pa, cq, sp):
    L = x.shape[0]
    T = _tile(L, 256)
    scale = HD ** -0.5
    w3 = 3 * AW + 3 * H * HD

    def body(x_ref, mod_ref, vec_ref, w_ref, pa_ref, cq_ref, sp_ref,
             p_ref, h_ref, qn_ref, kn_ref, vs_ref, gb_ref, ya_ref, cu_ref, qc_ref, u_carry, q_carry):
        @pl.when(pl.program_id(0) == 0)
        def _():
            u_carry[...] = jnp.zeros_like(u_carry)
            q_carry[...] = jnp.zeros_like(q_carry)

        n, _ = _rms(x_ref[...])
        hb = (n * vec_ref[0:1, :] * (1.0 + mod_ref[1:2, :]) + mod_ref[0:1, :]).astype(MXU)
        h_ref[...] = hb
        pm_a = _dot(hb, w_ref[:, 0:3 * AW], NN)
        p_ref[:, 0:3 * AW] = pm_a
        pm_q = _dot(hb, w_ref[:, 3 * AW:w3], NN)
        p_ref[:, 3 * AW:w3] = pm_q

        a_b = pm_a[:, 0:AW]
        u = pm_a[:, AW:2 * AW] * pm_a[:, 2 * AW:3 * AW]
        cu, _ = _conv_fwd(u, pa_ref, 3, u_carry[...])
        cu_ref[...] = cu.astype(MXU)
        u_carry[...] = u[T - SUB:T, :]
        yp = a_b * cu
        ms = _dot_f32(yp * yp, _blockdiag_mean(AW, A_GROUP), NN, exact="b")
        ya_ref[...] = (yp * lax.rsqrt(ms + EPS) * pa_ref[3:4, :]).astype(MXU)

        pm_z = _dot(hb, w_ref[:, w3:P_PAD], NN)
        p_ref[:, w3:P_PAD] = pm_z
        qkv = pm_q
        qc, _ = _conv_fwd(qkv, cq_ref, 4, q_carry[...])
        qc_ref[...] = qc.astype(MXU)
        q_carry[...] = qkv[T - SUB:T, :]
        qs = qc * _sig(qc)
        for h in range(H):
            q = qs[:, h * HD:(h + 1) * HD]
            qn_ref[:, h * HD:(h + 1) * HD] = q * (lax.rsqrt(jnp.sum(q * q, axis=-1, keepdims=True) + EPS) * scale)
            k = qs[:, (H + h) * HD:(H + h + 1) * HD]
            kn_ref[:, h * HD:(h + 1) * HD] = k * lax.rsqrt(jnp.sum(k * k, axis=-1, keepdims=True) + EPS)
        vs_ref[...] = qs[:, 2 * H * HD:3 * H * HD]

        lane, _, _, beta, g = _gate_small(pm_z[:, H * HD:H * HD + LANES], sp_ref)
        gb_ref[...] = jnp.where(lane < H, beta, jnp.where(lane < 2 * H, g, 0.0))

    row = lambda i: (i, 0)
    return pl.pallas_call(
        body, name="in_pre_fwd", grid=(L // T,),
        in_specs=[pl.BlockSpec((T, D), row), _full((SUB, D)), _full((SUB, D)), _full((D, P_PAD)),
                  _full((SUB, AW)), _full((SUB, 3 * H * HD)), _full((SUB, LANES))],
        out_specs=[pl.BlockSpec((T, P_PAD), row), pl.BlockSpec((T, D), row)]
        + [pl.BlockSpec((T, H * HD), row)] * 3 + [pl.BlockSpec((T, LANES), row), pl.BlockSpec((T, AW), row),
                                                  pl.BlockSpec((T, AW), row), pl.BlockSpec((T, 3 * H * HD), row)],
        out_shape=[jax.ShapeDtypeStruct((L, P_PAD), F32), jax.ShapeDtypeStruct((L, D), MXU)]
        + [jax.ShapeDtypeStruct((L, H * HD), F32)] * 3
        + [jax.ShapeDtypeStruct((L, LANES), F32), jax.ShapeDtypeStruct((L, AW), MXU),
           jax.ShapeDtypeStruct((L, AW), MXU), jax.ShapeDtypeStruct((L, 3 * H * HD), MXU)],
        scratch_shapes=[pltpu.VMEM((SUB, AW), F32), pltpu.VMEM((SUB, 3 * H * HD), F32)],
        compiler_params=_params(1),
    )(x, modrows, vec, w_in, pa, cq, sp)


def _gdr_masks():
    r = lax.broadcasted_iota(jnp.int32, (CK, CK), 0)
    c = lax.broadcasted_iota(jnp.int32, (CK, CK), 1)
    return r >= c, r > c


def _head_cols(gbt, h):
    return gbt[:, h:h + 1], gbt[:, H + h:H + h + 1]


def _split(x, parts):
    out = []
    for _ in range(parts):
        hi = x.astype(jnp.bfloat16)
        out.append(hi)
        x = x - hi.astype(F32)
    return out


def _dot_f32(a, b, dims, exact=None):
    if exact == "a":
        ab = a.astype(jnp.bfloat16)
        return sum(_dot(ab, t, dims) for t in _split(b, 3))
    if exact == "b":
        bb = b.astype(jnp.bfloat16)
        return sum(_dot(t, bb, dims) for t in _split(a, 3))
    ah, al = _split(a, 2)
    bh, bl = _split(b, 2)
    return _dot(ah, bh, dims) + _dot(ah, bl, dims) + _dot(al, bh, dims)


def _gdr_consts():
    causal, strict = _gdr_masks()
    return dict(causal=causal, strict=strict, tril=jnp.where(causal, 1.0, 0.0).astype(F32),
                eye=jnp.where(causal & jnp.logical_not(strict), 1.0, 0.0).astype(F32),
                bcast=jnp.full((CK, HD), 1.0 / HD, F32))


def _dots(a, b, dims):
    return [_dot(x, y, dims) for x, y in zip(a, b)]


def _dots_f32(a, b, dims, exact=None):
    n = len(a)
    if exact == "a":
        lhs = [[x.astype(jnp.bfloat16)] * 3 for x in a]
        rhs = [_split(y, 3) for y in b]
    elif exact == "b":
        lhs = [_split(x, 3) for x in a]
        rhs = [[y.astype(jnp.bfloat16)] * 3 for y in b]
    else:
        sa = [_split(x, 2) for x in a]
        sb = [_split(y, 2) for y in b]
        lhs = [[s[0], s[0], s[1]] for s in sa]
        rhs = [[s[0], s[1], s[0]] for s in sb]
    terms = [[_dot(lhs[i][t], rhs[i][t], dims) for i in range(n)] for t in range(3)]
    return [terms[0][i] + terms[1][i] + terms[2][i] for i in range(n)]


def _gdr_local(q, k, v, beta, g, cst, tinv=None):
    n = len(q)
    R = range(n)
    causal, strict = cst["causal"], cst["strict"]
    gc = _dots_f32([cst["tril"]] * n, [jnp.broadcast_to(g[i], (CK, HD)) for i in R], NN, exact="a")
    g_row = _dots_f32([cst["bcast"]] * n, gc, NT, exact="a")
    decay = [jnp.where(causal, jnp.exp(jnp.where(causal, gc[i][:, 0:CK] - g_row[i], 0.0)), 0.0) for i in R]
    eg = [jnp.exp(gc[i]) for i in R]
    gl = [gc[i][CK - 1:CK, :] for i in R]
    ek = [jnp.exp(gl[i] - gc[i]) for i in R]
    cd = [jnp.exp(gl[i]) for i in R]
    kb = [k[i] * beta[i] for i in R]
    pk = _dots(kb, k, NT)
    if tinv is None:
        xp = [-jnp.where(strict, pk[i] * decay[i], 0.0) for i in R]
        tinv = [cst["eye"] + xp[i] for i in R]
        for _ in range(5):
            xp = _dots_f32(xp, xp, NN)
            tx = _dots_f32(tinv, xp, NN)
            tinv = [tinv[i] + tx[i] for i in R]
    u = _dots(tinv, [v[i] * beta[i] for i in R], NN)
    w = _dots(tinv, [kb[i] * eg[i] for i in R], NN)
    qk = _dots(q, k, NT)
    intra = [jnp.where(causal, qk[i] * decay[i], 0.0) for i in R]
    return dict(decay=decay, eg=eg, ek=ek, cd=cd, kb=kb, pk=pk, tinv=tinv, u=u, w=w, qk=qk, intra=intra,
                q_dec=[q[i] * eg[i] for i in R], k_dec=[k[i] * ek[i] for i in R])


GDR_SUB = 8


def _gdr_fwd(qn, kn, vs, gb):
    L = qn.shape[0]
    nc = L // CK
    cb = min(8, nc)
    rb = cb * CK
    nb = nc // cb
    nsub = GDR_SUB if cb % GDR_SUB == 0 else 1

    def body(q_ref, k_ref, v_ref, gb_ref, o_ref, st_ref, ti_ref, s_ref):
        @pl.when(pl.program_id(0) == 0)
        def _():
            s_ref[...] = jnp.zeros_like(s_ref)

        cst = _gdr_consts()
        heads = range(H)

        def group(gi, carry):
            rows = [pl.ds(pl.multiple_of((gi * nsub + j) * CK, CK), CK) for j in range(nsub)]
            chains = [(j, h) for j in range(nsub) for h in heads]
            gbt = [gb_ref[rows[j], :] for j in range(nsub)]
            cols = lambda h: slice(h * HD, (h + 1) * HD)
            t = _gdr_local([q_ref[rows[j], cols(h)] for j, h in chains], [k_ref[rows[j], cols(h)] for j, h in chains],
                           [v_ref[rows[j], cols(h)] for j, h in chains],
                           [_head_cols(gbt[j], h)[0] for j, h in chains], [_head_cols(gbt[j], h)[1] for j, h in chains], cst)
            s = [s_ref[h] for h in heads]
            for j in range(nsub):
                at = lambda key: [t[key][j * H + h] for h in heads]
                for h in heads:
                    st_ref[h, gi * nsub + j] = s[h]
                    ti_ref[h, gi * nsub + j] = t["tinv"][j * H + h]
                ws = _dots(at("w"), s, NN)
                v_new = [u_h - ws_h for u_h, ws_h in zip(at("u"), ws)]
                o_s = _dots(at("q_dec"), s, NN)
                o_v = _dots(at("intra"), v_new, NN)
                kv = _dots(at("k_dec"), v_new, TN)
                cd = at("cd")
                for h in heads:
                    o_ref[rows[j], cols(h)] = o_s[h] + o_v[h]
                s = [s[h] * cd[h] + kv[h] for h in heads]
            for h in heads:
                s_ref[h] = s[h]
            return carry

        lax.fori_loop(0, cb // nsub, group, 0)

    blk = pl.BlockSpec((rb, H * HD), lambda b: (b, 0))
    return pl.pallas_call(
        body, name="gdr_fwd", grid=(nb,),
        in_specs=[blk, blk, blk, pl.BlockSpec((rb, LANES), lambda b: (b, 0))],
        out_specs=[blk, pl.BlockSpec((H, cb, HD, HD), lambda b: (0, b, 0, 0)),
                   pl.BlockSpec((H, cb, CK, CK), lambda b: (0, b, 0, 0))],
        out_shape=[jax.ShapeDtypeStruct((L, H * HD), F32), jax.ShapeDtypeStruct((H, nc, HD, HD), F32),
                   jax.ShapeDtypeStruct((H, nc, CK, CK), F32)],
        scratch_shapes=[pltpu.VMEM((H, HD, HD), F32)],
        compiler_params=_params(1),
    )(qn, kn, vs, gb)


def _gdr_bwd(qn, kn, vs, gb, states, tinvs, do):
    L = qn.shape[0]
    nc = L // CK
    cb = min(8, nc)
    rb = cb * CK
    nb = nc // cb
    nsub = GDR_SUB if cb % GDR_SUB == 0 else 1

    def body(q_ref, k_ref, v_ref, gb_ref, st_ref, ti_ref, do_ref, dq_ref, dk_ref, dv_ref, dgb_ref, ds_ref):
        @pl.when(pl.program_id(0) == 0)
        def _():
            ds_ref[...] = jnp.zeros_like(ds_ref)

        cst = _gdr_consts()
        causal, strict = cst["causal"], cst["strict"]
        ones = jnp.ones((CK, HD), F32)
        row = lax.broadcasted_iota(jnp.int32, (CK, HD), 0)
        lane = lax.broadcasted_iota(jnp.int32, (CK, LANES), 1)

        heads = range(H)
        rsum = lambda x: jnp.sum(x, axis=-1, keepdims=True)

        def group(gj, carry):
            gi = cb // nsub - 1 - gj
            rows = [pl.ds(pl.multiple_of((gi * nsub + j) * CK, CK), CK) for j in range(nsub)]
            chains = [(j, h) for j in range(nsub) for h in heads]
            gbt = [gb_ref[rows[j], :] for j in range(nsub)]
            cols = lambda h: slice(h * HD, (h + 1) * HD)
            q_all = [q_ref[rows[j], cols(h)] for j, h in chains]
            k_all = [k_ref[rows[j], cols(h)] for j, h in chains]
            v_all = [v_ref[rows[j], cols(h)] for j, h in chains]
            beta_all = [_head_cols(gbt[j], h)[0] for j, h in chains]
            t = _gdr_local(q_all, k_all, v_all, beta_all, [_head_cols(gbt[j], h)[1] for j, h in chains], cst,
                           tinv=[ti_ref[h, gi * nsub + j] for j, h in chains])
            ds_out = [ds_ref[h] for h in heads]
            for j in reversed(range(nsub)):
                at = lambda key: [t[key][j * H + h] for h in heads]
                pick = lambda lst: [lst[j * H + h] for h in heads]
                q, k, v, beta = pick(q_all), pick(k_all), pick(v_all), pick(beta_all)
                u, w, tinv, decay = at("u"), at("w"), at("tinv"), at("decay")
                eg, ek, cd, kb = at("eg"), at("ek"), at("cd"), at("kb")
                q_dec, k_dec, intra, pk, qk = at("q_dec"), at("k_dec"), at("intra"), at("pk"), at("qk")
                s = [st_ref[h, gi * nsub + j] for h in heads]
                dout = [do_ref[rows[j], cols(h)] for h in heads]

                ws = _dots(w, s, NN)
                v_new = [u[h] - ws[h] for h in heads]
                dq_dec = _dots(dout, s, NT)
                qd = _dots(q_dec, dout, TN)
                di = _dots(dout, v_new, NT)
                dintra = [jnp.where(causal, di[h], 0.0) for h in heads]
                ido = _dots(intra, dout, TN)
                kds = _dots(k_dec, ds_out, NN)
                dv_new = [ido[h] + kds[h] for h in heads]
                dk_dec = _dots(v_new, ds_out, NT)
                dcd = [jnp.sum(jnp.sum(ds_out[h] * s[h], axis=1, keepdims=True), axis=0, keepdims=True) for h in heads]
                dvs = _dots(dv_new, s, NT)
                dw = [-dvs[h] for h in heads]
                wdv = _dots(w, dv_new, TN)
                ds_new = [qd[h] + ds_out[h] * cd[h] - wdv[h] for h in heads]
                dru = _dots(tinv, dv_new, TN)
                drw = _dots(tinv, dw, TN)
                dl1 = _dots(dru, u, NT)
                dl2 = _dots(drw, w, NT)
                dlower = [-jnp.where(strict, dl1[h] + dl2[h], 0.0) for h in heads]
                dv = [dru[h] * beta[h] for h in heads]
                dbeta = [rsum(dru[h] * v[h]) for h in heads]
                dgc = [rsum(drw[h] * kb[h]) * eg[h] for h in heads]
                dpk = [dlower[h] * decay[h] for h in heads]
                dqk = [dintra[h] * decay[h] for h in heads]
                dpk_k = _dots(dpk, k, NN)
                dkb = [drw[h] * eg[h] + dpk_k[h] for h in heads]
                dk1 = _dots(dpk, kb, TN)
                dq1 = _dots(dqk, k, NN)
                dk2 = _dots(dqk, q, TN)
                m = [(dlower[h] * pk[h] + dintra[h] * qk[h]) * decay[h] for h in heads]
                mcol = _dots_f32(m, [ones] * H, TN, exact="b")
                e = [rsum(dk_dec[h] * k_dec[h]) for h in heads]
                dgl = [jnp.sum(e[h], axis=0, keepdims=True) + dcd[h] * cd[h] for h in heads]
                dgc = [dgc[h] + rsum(m[h]) - mcol[h] + rsum(dq_dec[h] * q_dec[h]) - e[h]
                       + jnp.where(row == CK - 1, dgl[h], 0.0) for h in heads]
                dg = _dots_f32([cst["tril"]] * H, dgc, TN, exact="a")
                dgb = jnp.zeros((CK, LANES), F32)
                for h in heads:
                    dq_ref[rows[j], cols(h)] = dq1[h] + dq_dec[h] * eg[h]
                    dk_ref[rows[j], cols(h)] = dk1[h] + dk2[h] + dk_dec[h] * ek[h] + dkb[h] * beta[h]
                    dv_ref[rows[j], cols(h)] = dv[h]
                    db = dbeta[h] + rsum(dkb[h] * k[h])
                    dgb = dgb + jnp.where(lane == h, db, 0.0) + jnp.where(lane == H + h, dg[h], 0.0)
                dgb_ref[rows[j], :] = dgb
                ds_out = ds_new
            for h in heads:
                ds_ref[h] = ds_out[h]
            return carry

        lax.fori_loop(0, cb // nsub, group, 0)

    blk = pl.BlockSpec((rb, H * HD), lambda b: (nb - 1 - b, 0))
    sblk = pl.BlockSpec((rb, LANES), lambda b: (nb - 1 - b, 0))
    return pl.pallas_call(
        body, name="gdr_bwd", grid=(nb,),
        in_specs=[blk, blk, blk, sblk, pl.BlockSpec((H, cb, HD, HD), lambda b: (0, nb - 1 - b, 0, 0)),
                  pl.BlockSpec((H, cb, CK, CK), lambda b: (0, nb - 1 - b, 0, 0)), blk],
        out_specs=[blk, blk, blk, sblk],
        out_shape=[jax.ShapeDtypeStruct((L, H * HD), F32)] * 3 + [jax.ShapeDtypeStruct((L, LANES), F32)],
        scratch_shapes=[pltpu.VMEM((H, HD, HD), F32)],
        compiler_params=_params(1),
    )(qn, kn, vs, gb, states, tinvs, do)


def _post_fwd(o, p, ya, x, modrows, sp, w_out):
    L = x.shape[0]
    T = _tile(L, 512)

    def body(o_ref, z_ref, ya_ref, x_ref, mod_ref, sp_ref, w_ref, y_ref, x2_ref, yb_ref):
        ndw = sp_ref[2:3, :]
        z = z_ref[...]
        sz = z * _sig(z)
        parts = []
        for h in range(H):
            n, _ = _rms(o_ref[:, h * HD:(h + 1) * HD])
            parts.append(n * ndw * sz[:, h * HD:(h + 1) * HD])
        yb = jnp.concatenate(parts, axis=-1).astype(MXU)
        yb_ref[...] = yb
        y = _dot(ya_ref[...], w_ref[0:AW, :], NN) + _dot(yb, w_ref[AW:2 * AW, :], NN)
        y_ref[...] = y
        x2_ref[...] = x_ref[...] + mod_ref[2:3, :] * y

    row = lambda i: (i, 0)
    zcol = (3 * AW + 3 * H * HD) // (H * HD)
    return pl.pallas_call(
        body, name="post_fwd", grid=(L // T,),
        in_specs=[pl.BlockSpec((T, H * HD), row), pl.BlockSpec((T, H * HD), lambda i: (i, zcol)),
                  pl.BlockSpec((T, AW), row), pl.BlockSpec((T, D), row), _full((SUB, D)), _full((SUB, LANES)),
                  _full((D, D))],
        out_specs=[pl.BlockSpec((T, D), row), pl.BlockSpec((T, D), row), pl.BlockSpec((T, H * HD), row)],
        out_shape=[jax.ShapeDtypeStruct((L, D), F32), jax.ShapeDtypeStruct((L, D), F32),
                   jax.ShapeDtypeStruct((L, H * HD), MXU)],
        compiler_params=_params(1),
    )(o, p, ya, x, modrows, sp, w_out)


FF_COLS = 2
FF_CW = DFF // FF_COLS
FF_ROWS = 256


def _ffn_fwd_half(x2, modrows, vec, w_up, cff, w_down, j, d_prev):
    assert FF_COLS == 2
    L = x2.shape[0]
    T = _tile(L, FF_ROWS)
    nj = FF_COLS
    last = d_prev is not None

    def body(*refs):
        x_ref, mod_ref, vec_ref, wg_ref, wu_ref, cg_ref, cu_ref, wd_ref = refs[:8]
        if last:
            dp_ref, gp_ref, up_ref, gc_ref, uc_ref, f_ref, d_ref, x3_ref, carry_g, carry_u = refs[8:]
        else:
            h_ref, gp_ref, up_ref, gc_ref, uc_ref, f_ref, d_ref, carry_g, carry_u = refs[8:]

        @pl.when(pl.program_id(0) == 0)
        def _():
            carry_g[...] = jnp.zeros_like(carry_g)
            carry_u[...] = jnp.zeros_like(carry_u)

        xv = x_ref[...]
        n, _ = _rms(xv)
        hb = (n * vec_ref[1:2, :] * (1.0 + mod_ref[4:5, :]) + mod_ref[3:4, :]).astype(MXU)
        if not last:
            h_ref[...] = hb
        g = _dot(hb, wg_ref[...], NN)
        u = _dot(hb, wu_ref[...], NN)
        gp_ref[...] = g.astype(MXU)
        up_ref[...] = u.astype(MXU)
        gc, _ = _conv_fwd(g, cg_ref, 3, carry_g[...])
        uc, _ = _conv_fwd(u, cu_ref, 3, carry_u[...])
        carry_g[...] = g[T - SUB:T, :]
        carry_u[...] = u[T - SUB:T, :]
        gc_ref[...] = gc.astype(MXU)
        uc_ref[...] = uc.astype(MXU)
        fb = (gc * _sig(gc) * uc).astype(MXU)
        f_ref[...] = fb
        part = _dot(fb, wd_ref[...], NN)
        if last:
            dv = dp_ref[...] + part
            d_ref[...] = dv
            x3_ref[...] = xv + mod_ref[5:6, :] * dv
        else:
            d_ref[...] = part

    row = lambda i: (i, 0)
    rowD = pl.BlockSpec((T, D), row)
    rowC = pl.BlockSpec((T, FF_CW), row)
    in_specs = [rowD, _full((SUB, D)), _full((SUB, D)),
                pl.BlockSpec((D, FF_CW), lambda i: (0, j)), pl.BlockSpec((D, FF_CW), lambda i: (0, nj + j)),
                pl.BlockSpec((SUB, FF_CW), lambda i: (0, j)), pl.BlockSpec((SUB, FF_CW), lambda i: (0, nj + j)),
                pl.BlockSpec((FF_CW, D), lambda i: (j, 0))]
    half = [jax.ShapeDtypeStruct((L, FF_CW), MXU)] * 5
    args = [x2, modrows, vec, w_up, w_up, cff, cff, w_down]
    if last:
        in_specs.append(rowD)
        args.append(d_prev)
        out_specs = [rowC] * 5 + [rowD, rowD]
        out_shape = half + [jax.ShapeDtypeStruct((L, D), F32), jax.ShapeDtypeStruct((L, D), F32)]
    else:
        out_specs = [rowD] + [rowC] * 5 + [rowD]
        out_shape = [jax.ShapeDtypeStruct((L, D), MXU)] + half + [jax.ShapeDtypeStruct((L, D), F32)]
    return pl.pallas_call(
        body, name="ffn_fwd_last" if last else "ffn_fwd_first", grid=(L // T,),
        in_specs=in_specs, out_specs=out_specs, out_shape=out_shape,
        scratch_shapes=[pltpu.VMEM((SUB, FF_CW), F32), pltpu.VMEM((SUB, FF_CW), F32)],
        compiler_params=_params(1),
    )(*args)


def _ffn_bwd_half(dx3, modrows, gpre, upre, gcv, ucv, cff, w_down, w_up, j, tail):
    assert FF_COLS == 2
    L = dx3.shape[0]
    T = _tile(L, FF_ROWS)
    ni, nj = L // T, FF_COLS
    last = tail is not None

    def body(*refs):
        dx3_ref, mod_ref, gp_ref, up_ref, gc_ref, uc_ref, cg_ref, cu_ref, wd_ref, wg_ref, wu_ref = refs[:11]
        if last:
            (d_ref, x2_ref, vec_ref, dhp_ref, dgp_ref, dup_ref, dx2_ref, accv_ref, dcg_ref, dcu_ref,
             carry_g, carry_u) = refs[11:]
        else:
            dd_ref, dgp_ref, dup_ref, dh_ref, dcg_ref, dcu_ref, carry_g, carry_u = refs[11:]
        i = pl.program_id(0)

        @pl.when(i == 0)
        def _():
            carry_g[...] = jnp.zeros_like(carry_g)
            carry_u[...] = jnp.zeros_like(carry_u)
            dcg_ref[...] = jnp.zeros_like(dcg_ref)
            dcu_ref[...] = jnp.zeros_like(dcu_ref)
            if last:
                accv_ref[...] = jnp.zeros_like(accv_ref)

        dx3v = dx3_ref[...]
        ddb = (mod_ref[5:6, :] * dx3v).astype(MXU)
        if not last:
            dd_ref[...] = ddb
        g, u = gp_ref[...].astype(F32), up_ref[...].astype(F32)
        gc, uc = gc_ref[...].astype(F32), uc_ref[...].astype(F32)
        sg = _sig(gc)
        df = _dot(ddb, wd_ref[...], NT)
        duc = df * (gc * sg)
        dgc = df * uc * (sg * (1.0 + gc * (1.0 - sg)))
        dgs = [dgc] + [_shift_up(dgc, s, carry_g[...]) for s in (1, 2)]
        dus = [duc] + [_shift_up(duc, s, carry_u[...]) for s in (1, 2)]
        for s in range(3):
            dcg_ref[2 - s:3 - s, :] += _sum0(dgs[s] * g)
            dcu_ref[2 - s:3 - s, :] += _sum0(dus[s] * u)
        dg = (cg_ref[2:3, :] * dgs[0] + cg_ref[1:2, :] * dgs[1] + cg_ref[0:1, :] * dgs[2]).astype(MXU)
        du = (cu_ref[2:3, :] * dus[0] + cu_ref[1:2, :] * dus[1] + cu_ref[0:1, :] * dus[2]).astype(MXU)
        carry_g[...] = dgc[0:SUB, :]
        carry_u[...] = duc[0:SUB, :]
        dgp_ref[...] = dg
        dup_ref[...] = du
        dh = _dot(dg, wg_ref[...], NT) + _dot(du, wu_ref[...], NT)
        if last:
            dh = dh + dhp_ref[...]
            accv_ref[0:1, :] += _sum0(dx3v * d_ref[...])
            n, r = _rms(x2_ref[...])
            nw, sc = vec_ref[1:2, :], mod_ref[4:5, :]
            accv_ref[1:2, :] += _sum0(dh)
            accv_ref[2:3, :] += _sum0(dh * n * nw)
            accv_ref[3:4, :] += _sum0(dh * n * (1.0 + sc))
            dx2_ref[...] = _rms_bwd(dh * nw * (1.0 + sc), n, r) + dx3v
        else:
            dh_ref[...] = dh

    row = lambda i: (ni - 1 - i, 0)
    rowD = pl.BlockSpec((T, D), row)
    rowC = pl.BlockSpec((T, FF_CW), row)
    in_specs = [rowD, _full((SUB, D)), rowC, rowC, rowC, rowC,
                pl.BlockSpec((SUB, FF_CW), lambda i: (0, j)), pl.BlockSpec((SUB, FF_CW), lambda i: (0, nj + j)),
                pl.BlockSpec((FF_CW, D), lambda i: (j, 0)),
                pl.BlockSpec((D, FF_CW), lambda i: (0, j)), pl.BlockSpec((D, FF_CW), lambda i: (0, nj + j))]
    args = [dx3, modrows, gpre, upre, gcv, ucv, cff, cff, w_down, w_up, w_up]
    halfb = [jax.ShapeDtypeStruct((L, FF_CW), MXU), jax.ShapeDtypeStruct((L, FF_CW), MXU)]
    dconv = [jax.ShapeDtypeStruct((SUB, FF_CW), F32)] * 2
    if last:
        d, x2, vec, dh_prev = tail
        in_specs += [rowD, rowD, _full((SUB, D)), rowD]
        args += [d, x2, vec, dh_prev]
        out_specs = [rowC, rowC, rowD, _full((SUB, D)), _full((SUB, FF_CW)), _full((SUB, FF_CW))]
        out_shape = halfb + [jax.ShapeDtypeStruct((L, D), F32), jax.ShapeDtypeStruct((SUB, D), F32)] + dconv
    else:
        out_specs = [rowD, rowC, rowC, rowD, _full((SUB, FF_CW)), _full((SUB, FF_CW))]
        out_shape = [jax.ShapeDtypeStruct((L, D), MXU)] + halfb + [jax.ShapeDtypeStruct((L, D), F32)] + dconv
    return pl.pallas_call(
        body, name="ffn_bwd_last" if last else "ffn_bwd_first", grid=(ni,),
        in_specs=in_specs, out_specs=out_specs, out_shape=out_shape,
        scratch_shapes=[pltpu.VMEM((SUB, FF_CW), F32), pltpu.VMEM((SUB, FF_CW), F32)],
        compiler_params=_params(1),
    )(*args)


def _final(x, target, nf):
    L = x.shape[0]
    T = _tile(L, 256)

    def body(x_ref, t_ref, nf_ref, dx_ref, acc_ref):
        @pl.when(pl.program_id(0) == 0)
        def _():
            acc_ref[...] = jnp.zeros_like(acc_ref)

        n, r = _rms(x_ref[...])
        w = nf_ref[0:1, :]
        err = n * w - t_ref[...]
        acc_ref[0:1, :] += (0.5 / D) * _sum0(err * err)
        dy = err * (1.0 / D)
        acc_ref[1:2, :] += _sum0(dy * n)
        dx_ref[...] = _rms_bwd(dy * w, n, r)

    row = lambda i: (i, 0)
    return pl.pallas_call(
        body, name="final_norm_loss", grid=(L // T,),
        in_specs=[pl.BlockSpec((T, D), row), pl.BlockSpec((T, D), row), _full((SUB, D))],
        out_specs=[pl.BlockSpec((T, D), row), _full((SUB, D))],
        out_shape=[jax.ShapeDtypeStruct((L, D), F32), jax.ShapeDtypeStruct((SUB, D), F32)],
        compiler_params=_params(1),
    )(x, target, nf)


def _post_bwd(dx2, y, o, p, modrows, sp, w_out):
    L = dx2.shape[0]
    T = _tile(L, 512)

    def body(dx2_ref, y_ref, o_ref, z_ref, mod_ref, sp_ref, w_ref, dy_ref, do_ref, dz_ref, dya_ref, accv_ref, accs_ref):
        @pl.when(pl.program_id(0) == 0)
        def _():
            accv_ref[...] = jnp.zeros_like(accv_ref)
            accs_ref[...] = jnp.zeros_like(accs_ref)

        dx2v = dx2_ref[...]
        accv_ref[0:1, :] += _sum0(dx2v * y_ref[...])
        dyb = (mod_ref[2:3, :] * dx2v).astype(MXU)
        dy_ref[...] = dyb
        dyc = _dot(dyb, w_ref[...], NT)
        dya_ref[...] = dyc[:, 0:AW]
        ndw = sp_ref[2:3, :]
        z = z_ref[...]
        sgz = _sig(z)
        dsz = sgz * (1.0 + z * (1.0 - sgz))
        dndw = jnp.zeros((1, HD), F32)
        for h in range(H):
            sl = slice(h * HD, (h + 1) * HD)
            n, r = _rms(o_ref[:, sl])
            dyh = dyc[:, AW + h * HD:AW + (h + 1) * HD]
            zh = z[:, sl]
            don = dyh * (zh * sgz[:, sl])
            dz_ref[:, sl] = dyh * (n * ndw) * dsz[:, sl]
            dndw = dndw + _sum0(don * n)
            do_ref[:, sl] = _rms_bwd(don * ndw, n, r)
        accs_ref[0:1, :] += dndw

    row = lambda i: (i, 0)
    zcol = (3 * AW + 3 * H * HD) // (H * HD)
    return pl.pallas_call(
        body, name="post_bwd", grid=(L // T,),
        in_specs=[pl.BlockSpec((T, D), row), pl.BlockSpec((T, D), row), pl.BlockSpec((T, H * HD), row),
                  pl.BlockSpec((T, H * HD), lambda i: (i, zcol)), _full((SUB, D)), _full((SUB, LANES)), _full((D, D))],
        out_specs=[pl.BlockSpec((T, D), row)] + [pl.BlockSpec((T, H * HD), row)] * 3 + [_full((SUB, D)), _full((SUB, LANES))],
        out_shape=[jax.ShapeDtypeStruct((L, D), MXU)] + [jax.ShapeDtypeStruct((L, H * HD), F32)] * 3
        + [jax.ShapeDtypeStruct((SUB, D), F32), jax.ShapeDtypeStruct((SUB, LANES), F32)],
        compiler_params=_params(1),
    )(dx2, y, o, p, modrows, sp, w_out)


def _pre_in_bwd(p, cub, qcb, dqn, dkn, dvs, dya, dz, dgb, pa, cq, sp, w_in, x, dx2, modrows, vec):
    L = p.shape[0]
    T = _tile(L, 256)
    ni = L // T
    scale = HD ** -0.5
    w3 = 3 * AW + 3 * H * HD

    def body(pm_ref, cu_ref, qc_ref, ps_ref, dq_ref, dk_ref, dv_ref, dya_ref, dz_ref, dgb_ref, pa_ref, cq_ref, sp_ref,
             w_ref, x_ref, dx2_ref, mod_ref, vec_ref,
             dp_ref, dx_ref, dpa_ref, dcq_ref, dsp_ref, accv_ref, carry_u, carry_q):
        i = pl.program_id(0)

        @pl.when(i == 0)
        def _():
            dpa_ref[...] = jnp.zeros_like(dpa_ref)
            dcq_ref[...] = jnp.zeros_like(dcq_ref)
            dsp_ref[...] = jnp.zeros_like(dsp_ref)
            accv_ref[...] = jnp.zeros_like(accv_ref)
            carry_u[...] = jnp.zeros_like(carry_u)
            carry_q[...] = jnp.zeros_like(carry_q)

        a_b, a_c, a_x = pm_ref[:, 0:AW], pm_ref[:, AW:2 * AW], pm_ref[:, 2 * AW:3 * AW]
        u = a_c * a_x
        cu = cu_ref[...].astype(F32)
        yp = a_b * cu
        bd = _blockdiag_mean(AW, A_GROUP)
        ra = lax.rsqrt(_dot_f32(yp * yp, bd, NN, exact="b") + EPS)
        na = yp * ra
        dya = dya_ref[...]
        dpa_ref[3:4, :] += _sum0(dya * na)
        dna = dya * pa_ref[3:4, :]
        dyp = ra * (dna - na * _dot_f32(dna * na, bd, NN, exact="b"))
        dcu = dyp * a_b
        dcs = [dcu] + [_shift_up(dcu, s, carry_u[...]) for s in (1, 2)]
        du = pa_ref[2:3, :] * dcs[0]
        for s in range(3):
            dpa_ref[2 - s:3 - s, :] += _sum0(dcs[s] * u)
            if s:
                du = du + pa_ref[2 - s:3 - s, :] * dcs[s]
        carry_u[...] = dcu[0:SUB, :]
        dp_a = jnp.concatenate([dyp * cu, du * a_x, du * a_c], axis=-1).astype(MXU)
        dp_ref[:, 0:3 * AW] = dp_a
        dh = _dot(dp_a, w_ref[:, 0:3 * AW], NT)

        qkv = pm_ref[:, 3 * AW:w3]
        qc = qc_ref[...].astype(F32)
        sg = _sig(qc)
        qs = qc * sg
        parts = []
        for h in range(H):
            q = qs[:, h * HD:(h + 1) * HD]
            rq = lax.rsqrt(jnp.sum(q * q, axis=-1, keepdims=True) + EPS)
            parts.append(_l2_bwd(dq_ref[:, h * HD:(h + 1) * HD] * scale, q * rq, rq))
        for h in range(H):
            k = qs[:, (H + h) * HD:(H + h + 1) * HD]
            rk = lax.rsqrt(jnp.sum(k * k, axis=-1, keepdims=True) + EPS)
            parts.append(_l2_bwd(dk_ref[:, h * HD:(h + 1) * HD], k * rk, rk))
        parts.append(dv_ref[...])
        dqc = jnp.concatenate(parts, axis=-1) * (sg * (1.0 + qc * (1.0 - sg)))
        dqs = [dqc] + [_shift_up(dqc, s, carry_q[...]) for s in (1, 2, 3)]
        dqkv = cq_ref[3:4, :] * dqs[0]
        for s in range(4):
            dcq_ref[3 - s:4 - s, :] += _sum0(dqs[s] * qkv)
            if s:
                dqkv = dqkv + cq_ref[3 - s:4 - s, :] * dqs[s]
        dp_q = dqkv.astype(MXU)
        dp_ref[:, 3 * AW:w3] = dp_q
        dh = dh + _dot(dp_q, w_ref[:, 3 * AW:w3], NT)
        carry_q[...] = dqc[0:SUB, :]

        lane, a, xb, beta, g = _gate_small(ps_ref[...], sp_ref)
        dgb = dgb_ref[...]
        dbeta = jnp.where(lane < H, dgb, 0.0)
        dg = jnp.where((lane >= H) & (lane < 2 * H), dgb, 0.0)
        dalpha = dg * a * _sig(xb)
        dsp_ref[0:1, :] += _sum0(dg * g)
        dsp_ref[1:2, :] += _sum0(dalpha)
        dp_z = jnp.concatenate([dz_ref[...], dbeta * beta * (1.0 - beta) + dalpha], axis=-1).astype(MXU)
        dp_ref[:, w3:P_PAD] = dp_z
        dh = dh + _dot(dp_z, w_ref[:, w3:P_PAD], NT)

        n, r = _rms(x_ref[...])
        nw, sc = vec_ref[0:1, :], mod_ref[1:2, :]
        accv_ref[0:1, :] += _sum0(dh)
        accv_ref[1:2, :] += _sum0(dh * n * nw)
        accv_ref[2:3, :] += _sum0(dh * n * (1.0 + sc))
        dx_ref[...] = _rms_bwd(dh * nw * (1.0 + sc), n, r) + dx2_ref[...]

    row = lambda i: (ni - 1 - i, 0)
    hrow = pl.BlockSpec((T, H * HD), row)
    rowD = pl.BlockSpec((T, D), row)
    return pl.pallas_call(
        body, name="pre_in_bwd", grid=(ni,),
        in_specs=[pl.BlockSpec((T, w3), row), pl.BlockSpec((T, AW), row), pl.BlockSpec((T, 3 * H * HD), row),
                  pl.BlockSpec((T, LANES), lambda i: (ni - 1 - i, (P_PAD - LANES) // LANES)),
                  hrow, hrow, hrow, pl.BlockSpec((T, AW), row), hrow,
                  pl.BlockSpec((T, LANES), row),
                  _full((SUB, AW)), _full((SUB, 3 * H * HD)), _full((SUB, LANES)),
                  _full((D, P_PAD)), rowD, rowD, _full((SUB, D)), _full((SUB, D))],
        out_specs=[pl.BlockSpec((T, P_PAD), row), rowD, _full((SUB, AW)), _full((SUB, 3 * H * HD)), _full((SUB, LANES)),
                   _full((SUB, D))],
        out_shape=[jax.ShapeDtypeStruct((L, P_PAD), MXU), jax.ShapeDtypeStruct((L, D), F32),
                   jax.ShapeDtypeStruct((SUB, AW), F32), jax.ShapeDtypeStruct((SUB, 3 * H * HD), F32),
                   jax.ShapeDtypeStruct((SUB, LANES), F32), jax.ShapeDtypeStruct((SUB, D), F32)],
        scratch_shapes=[pltpu.VMEM((SUB, AW), F32), pltpu.VMEM((SUB, 3 * H * HD), F32)],
        compiler_params=_params(1),
    )(p, cub, qcb, p, dqn, dkn, dvs, dya, dz, dgb, pa, cq, sp, w_in, x, dx2, modrows, vec)


def _wgrad(a, b, tm, tn, name):
    L, m = a.shape
    n = b.shape[1]
    tl = _tile(L, 512)
    tm, tn = _tile(m, tm), _tile(n, tn)
    nl = L // tl

    def body(a_ref, b_ref, o_ref, acc):
        @pl.when(pl.program_id(2) == 0)
        def _():
            acc[...] = jnp.zeros_like(acc)

        acc[...] += _dot(a_ref[...], b_ref[...], TN)

        @pl.when(pl.program_id(2) == nl - 1)
        def _():
            o_ref[...] = acc[...].astype(o_ref.dtype)

    return pl.pallas_call(
        body, name=name, grid=(m // tm, n // tn, nl),
        in_specs=[pl.BlockSpec((tl, tm), lambda i, j, l: (l, i)), pl.BlockSpec((tl, tn), lambda i, j, l: (l, j))],
        out_specs=pl.BlockSpec((tm, tn), lambda i, j, l: (i, j)),
        out_shape=jax.ShapeDtypeStruct((m, n), MXU), scratch_shapes=[pltpu.VMEM((tm, tn), F32)],
        compiler_params=_params(3),
    )(a, b)


def _wgrad_cols(a, b, tm, n_shard, wpad, count, name):
    L, m = a.shape
    n = b.shape[1]
    tl = _tile(L, 512)
    tm = _tile(m, tm)
    nl = L // tl
    wins = _shard_windows(n_shard, count)
    assert all(a_ * LANES + win <= n for a_, _, win in wins), (wins, n)

    def body(a_ref, b_ref, o_ref, acc):
        @pl.when(pl.program_id(1) == 0)
        def _():
            acc[...] = jnp.zeros_like(acc)

        acc[...] += _dot(a_ref[...], b_ref[...], TN)

        @pl.when(pl.program_id(1) == nl - 1)
        def _():
            for k, (a_, s, win) in enumerate(wins):
                xk = acc[:, a_ * LANES:a_ * LANES + win]
                if s:
                    xk = pltpu.roll(xk, win - s, 1)
                o_ref[k] = _fit_lanes(xk, wpad).astype(o_ref.dtype)

    return pl.pallas_call(
        body, name=name, grid=(m // tm, nl),
        in_specs=[pl.BlockSpec((tl, tm), lambda i, l: (l, i)), pl.BlockSpec((tl, n), lambda i, l: (l, 0))],
        out_specs=pl.BlockSpec((count, tm, wpad), lambda i, l: (0, i, 0)),
        out_shape=jax.ShapeDtypeStruct((count, m, wpad), MXU),
        scratch_shapes=[pltpu.VMEM((tm, n), F32)],
        compiler_params=_params(2),
    )(a, b)


def _adamw(w, g, m, v, name):
    r, n = w.shape
    tr = _tile(r, 512)
    bc1 = 1.0 - ADAM_B1 ** ADAM_STEP
    bc2 = 1.0 - ADAM_B2 ** ADAM_STEP

    def body(w_ref, g_ref, m_ref, v_ref, d_ref, nm_ref, nv_ref):
        gv = g_ref[...]
        nm = ADAM_B1 * m_ref[...] + (1.0 - ADAM_B1) * gv
        nv = ADAM_B2 * v_ref[...] + (1.0 - ADAM_B2) * (gv * gv)
        nm_ref[...] = nm
        nv_ref[...] = nv
        d_ref[...] = -ADAM_LR * ((nm / bc1) / (jnp.sqrt(nv / bc2) + ADAM_EPS) + ADAM_WD * w_ref[...])

    spec = pl.BlockSpec((tr, n), lambda i: (i, 0))
    return pl.pallas_call(
        body, name=name, grid=(r // tr,), in_specs=[spec] * 4, out_specs=[spec] * 3,
        out_shape=[jax.ShapeDtypeStruct((r, n), F32)] * 3, compiler_params=_params(1),
    )(w, g, m, v)


def _rows8(rows, width):
    out = jnp.zeros((SUB, width), F32)
    for r, vrow in enumerate(rows):
        out = out.at[r, :vrow.shape[0]].set(vrow)
    return out


def _at_lanes(v4, start):
    return jnp.zeros((LANES,), F32).at[start:start + v4.shape[0]].set(v4)


def _pad_rows(flat, mult):
    n = flat.shape[0]
    pad = (-n) % mult
    return jnp.pad(flat, (0, pad)) if pad else flat


IN_PAD = 512
UP_PAD = 768


def _local_fwd_bwd(x, target, mod_full, small_w, full_w, on_grads=None):
    norm1_w, norm2_w, norm_a_w, a_log, dt_bias, norm_dn_w, norm_f_w = small_w
    w_in_f, w_out_f, w_up_f, w_down_f, conv_a_f, conv_q_f, conv_f_f = full_w

    def layer_params(i):
        modrows = jnp.concatenate([mod_full[i], jnp.zeros((SUB - N_MOD, D), F32)], axis=0)
        vec = _rows8([norm1_w[i], norm2_w[i]], D)
        pa = _rows8([conv_a_f[i, 0], conv_a_f[i, 1], conv_a_f[i, 2], norm_a_w[i]], AW)
        cq = _rows8([conv_q_f[i, k] for k in range(4)], 3 * H * HD)
        sp = _rows8([_at_lanes(a_log[i], H), _at_lanes(dt_bias[i], H), norm_dn_w[i]], LANES)
        cff = _rows8([conv_f_f[i, k] for k in range(3)], 2 * DFF)
        return modrows, vec, pa, cq, sp, cff

    saved = []
    xi = x
    for i in range(DEPTH):
        modrows, vec, pa, cq, sp, cff = layer_params(i)
        p, h1, qn, kn, vs, gb, ya, cub, qcb = _in_pre_fwd(xi, modrows, vec, w_in_f[i], pa, cq, sp)
        o, states, tinvs = _gdr_fwd(qn, kn, vs, gb)
        y, x2, yb = _post_fwd(o, p, ya, xi, modrows, sp, w_out_f[i])
        h2, gp0, up0, gc0, uc0, f0, d0 = _ffn_fwd_half(x2, modrows, vec, w_up_f[i], cff, w_down_f[i], 0, None)
        gp1, up1, gc1, uc1, f1, dff, x3 = _ffn_fwd_half(x2, modrows, vec, w_up_f[i], cff, w_down_f[i], 1, d0)
        saved.append(dict(x=xi, p=p, h1=h1, qn=qn, kn=kn, vs=vs, gb=gb, ya=ya, cub=cub, qcb=qcb, o=o, states=states,
                          tinvs=tinvs, y=y, x2=x2, yb=yb,
                          h2=h2, gpre=(gp0, gp1), upre=(up0, up1), gc=(gc0, gc1), uc=(uc0, uc1), f=(f0, f1), d=dff))
        xi = x3

    dx, facc = _final(xi, target, _rows8([norm_f_w], D))
    loss_local = jnp.sum(facc[0])
    d_norm_f = facc[1]

    gw_in, gw_out, gw_up, gw_down = [None] * DEPTH, [None] * DEPTH, [None] * DEPTH, [None] * DEPTH
    g_small = [None] * DEPTH
    for i in reversed(range(DEPTH)):
        s = saved[i]
        modrows, vec, pa, cq, sp, cff = layer_params(i)
        dd, dgp0, dup0, dh0, dcg0, dcu0 = _ffn_bwd_half(dx, modrows, s["gpre"][0], s["upre"][0], s["gc"][0], s["uc"][0],
                                                        cff, w_down_f[i], w_up_f[i], 0, None)
        dgp1, dup1, dx2, accf, dcg1, dcu1 = _ffn_bwd_half(dx, modrows, s["gpre"][1], s["upre"][1], s["gc"][1], s["uc"][1],
                                                          cff, w_down_f[i], w_up_f[i], 1, (s["d"], s["x2"], vec, dh0))
        n_up, up_pad = 2 * DFF // N_DEV, UP_PAD
        gw_up[i] = jnp.concatenate([_wgrad_cols(s["h2"], t, 1024, n_up, up_pad, FF_CW // n_up, "wgrad_up")
                                    for t in (dgp0, dgp1, dup0, dup1)], axis=0)
        gw_down[i] = jnp.concatenate([_wgrad(s["f"][0], dd, FF_CW, 1024, "wgrad_down"),
                                      _wgrad(s["f"][1], dd, FF_CW, 1024, "wgrad_down")],
                                     axis=0).reshape(N_DEV, DFF // N_DEV, D)
        if on_grads is not None:
            on_grads(i, "ffn", [gw_up[i], gw_down[i]])
        dy, do, dz, dya, accp, accs = _post_bwd(dx2, s["y"], s["o"], s["p"], modrows, sp, w_out_f[i])
        gw_out[i] = jnp.concatenate([_wgrad(s["ya"], dy, 512, 1024, "wgrad_out"),
                                     _wgrad(s["yb"], dy, 512, 1024, "wgrad_out")], axis=0).reshape(N_DEV, D // N_DEV, D)
        dqn, dkn, dvs, dgb = _gdr_bwd(s["qn"], s["kn"], s["vs"], s["gb"], s["states"], s["tinvs"], do)
        dp, dx, dpa, dcq, dsp, acci = _pre_in_bwd(s["p"], s["cub"], s["qcb"], dqn, dkn, dvs, dya, dz, dgb, pa, cq, sp,
                                                  w_in_f[i], s["x"], dx2, modrows, vec)
        gw_in[i] = _wgrad_cols(s["h1"], dp, 1024, P_IN // N_DEV, IN_PAD, N_DEV, "wgrad_in")
        dconv_ff = jnp.concatenate([dcg0, dcg1, dcu0, dcu1], axis=1)[0:3]
        dmod = jnp.stack([acci[0], acci[1], accp[0], accf[1], accf[2], accf[0]])
        g_small[i] = dict(norm1=acci[2], norm2=accf[3], norm_a=dpa[3], a_log=dsp[0, H:2 * H], dt_bias=dsp[1, H:2 * H],
                          norm_dn=accs[0], conv_a=dpa[0:3], conv_qkv=dcq[0:4], conv_ff=dconv_ff, dmod=dmod.reshape(-1))
        if on_grads is not None:
            on_grads(i, "mix", [gw_in[i], gw_out[i]])
    return loss_local, dx, gw_in, gw_out, gw_up, gw_down, g_small, d_norm_f


def kernel(x, c, ada_w, ada_b, norm1_w, w_in, conv_a_w, norm_a_w, conv_qkv_w, a_log, dt_bias, norm_dn_w, w_out, norm2_w, w_up, conv_ff_w, w_down, norm_f_w, loss_target, m_ada_w, m_ada_b, m_norm1_w, m_w_in, m_conv_a_w, m_norm_a_w, m_conv_qkv_w, m_a_log, m_dt_bias, m_norm_dn_w, m_w_out, m_norm2_w, m_w_up, m_conv_ff_w, m_w_down, m_norm_f_w, v_ada_w, v_ada_b, v_norm1_w, v_w_in, v_conv_a_w, v_norm_a_w, v_conv_qkv_w, v_a_log, v_dt_bias, v_norm_dn_w, v_w_out, v_norm2_w, v_w_up, v_conv_ff_w, v_w_down, v_norm_f_w):
    ax, ay, ac = lax.axis_index("x"), lax.axis_index("y"), lax.axis_index("c")
    me = 4 * ax + 2 * ay + ac
    x = x[0]
    target = loss_target[0]
    n_in, n_up = P_IN // N_DEV, 2 * DFF // N_DEV

    def lane_pad(t, width):
        return jnp.pad(t.astype(MXU), ((0, 0), (0, 0), (0, width - t.shape[-1])))

    conv_blob = _pad_rows(jnp.concatenate([t.reshape(-1) for t in (conv_a_w, conv_qkv_w, conv_ff_w)]),
                          SUB * LANES).reshape(-1, LANES)
    c_rows = jnp.zeros((SUB, D), F32).at[0].set(c[0])
    send = [lane_pad(w_in, IN_PAD), w_out.astype(MXU), lane_pad(w_up, UP_PAD), w_down.astype(MXU)]
    got = [None] * DEPTH
    g_in0, g_conv, g_c = _all_gather([send[0][0], conv_blob, c_rows], "gather_weights", in_vmem=False)
    shards, _ = lax.optimization_barrier(([t[0] for t in send[1:]], g_c))
    got[0] = [g_in0] + _all_gather_async(shards, "gather_weights_l0", collective_id=0)
    for i in range(1, DEPTH):
        shards, _ = lax.optimization_barrier(([t[i] for t in send], g_c))
        got[i] = _all_gather_async(shards, "gather_weights_l%d" % i, collective_id=i)
    w_in_f = [_interleave_cols(g[0][:, None], n_in, P_PAD, "interleave_w_in")[0] for g in got]
    w_up_f = [_interleave_cols(g[2][:, None], n_up, 2 * DFF, "interleave_w_up")[0] for g in got]
    w_out_f = [g[1].reshape(D, D) for g in got]
    w_down_f = [g[3].reshape(DFF, D) for g in got]
    sg = g_conv.reshape(N_DEV, -1)
    o1 = conv_a_w.size
    o2 = o1 + conv_qkv_w.size
    o3 = o2 + conv_ff_w.size
    conv_a_f = sg[:, 0:o1].reshape(N_DEV, DEPTH, 3, AW // N_DEV).transpose(1, 2, 0, 3).reshape(DEPTH, 3, AW)
    conv_q_f = sg[:, o1:o2].reshape(N_DEV, DEPTH, 4, 3 * H * HD // N_DEV).transpose(1, 2, 0, 3).reshape(DEPTH, 4, 3 * H * HD)
    conv_f_f = sg[:, o2:o3].reshape(N_DEV, DEPTH, 3, n_up).transpose(1, 2, 0, 3).reshape(DEPTH, 3, 2 * DFF)

    c_all = jnp.concatenate([g_c[:, 0], jnp.zeros((16 - N_DEV, D), F32)], axis=0)
    n_ada = N_MOD * D // N_DEV
    ada_b_cols = lax.dynamic_slice_in_dim(ada_b, me * n_ada, n_ada, axis=1)[:, None, :]
    mod_sh = _mod_fwd(c_all, ada_w, ada_b_cols)
    mod_all = _all_gather([mod_sh.reshape(DEPTH * 16, n_ada)], "gather_mod", in_vmem=True)[0]
    mod_all = mod_all.reshape(N_DEV, DEPTH, 16, n_ada)
    mod_mine = lax.dynamic_index_in_dim(mod_all, me, axis=2, keepdims=False)
    mod_full = mod_mine.transpose(1, 0, 2).reshape(DEPTH, N_MOD, D)

    tags = ["w_in", "w_out", "w_up", "w_down"]
    received = [dict() for _ in range(DEPTH)]

    def on_grads(i, part, gs_i):
        first_id = DEPTH if part == "ffn" else 2 * DEPTH
        got_i = _rs_exchange_async(gs_i, "rs_exchange_%s_l%d" % (part, i), collective_id=first_id + i)
        received[i].update(zip(("w_up", "w_down") if part == "ffn" else ("w_in", "w_out"), got_i))

    loss_local, dx, _, _, _, _, g_small, d_norm_f = _local_fwd_bwd(
        x, target, mod_full, (norm1_w, norm2_w, norm_a_w, a_log, dt_bias, norm_dn_w, norm_f_w),
        (w_in_f, w_out_f, w_up_f, w_down_f, conv_a_f, conv_q_f, conv_f_f), on_grads)
    loss = lax.psum(loss_local, ("x", "y", "c"))
    grad_x = dx[None]

    keys = ["dmod", "norm1", "norm2", "norm_a", "a_log", "dt_bias", "norm_dn", "conv_a", "conv_qkv", "conv_ff"]
    stacked = {k: jnp.stack([g_small[i][k] for i in range(DEPTH)]) for k in keys}
    flat_parts = [stacked[k].reshape(-1) for k in keys] + [d_norm_f]
    sizes = [int(t.shape[0]) for t in flat_parts]
    sflat = _pad_rows(jnp.concatenate(flat_parts), SUB * LANES).reshape(-1, LANES)
    sall = _all_gather([sflat], "gather_small_grads", in_vmem=True)[0]
    ssum = _sum_devices(sall).reshape(-1)
    so = [0]
    for sz in sizes:
        so.append(so[-1] + sz)
    red = {k: ssum[so[n]:so[n + 1]].reshape(stacked[k].shape) for n, k in enumerate(keys)}
    g_norm_f = ssum[so[len(keys)]:so[len(keys) + 1]]
    dmod_all = sall[:, 0:sizes[0] // LANES, :].reshape(N_DEV, DEPTH, N_MOD * D)

    g_ada_b = red["dmod"].reshape(DEPTH, N_MOD * D)
    dmod_cols = lax.dynamic_slice_in_dim(dmod_all, me * n_ada, n_ada, axis=2).transpose(1, 0, 2)
    dmod_cols = jnp.concatenate([dmod_cols, jnp.zeros((DEPTH, 16 - N_DEV, n_ada), F32)], axis=1)
    g_ada_w = _mod_bwd(c_all, dmod_cols)
    g_conv_a = lax.dynamic_slice_in_dim(red["conv_a"], me * (AW // N_DEV), AW // N_DEV, axis=2)
    g_conv_qkv = lax.dynamic_slice_in_dim(red["conv_qkv"], me * (3 * H * HD // N_DEV), 3 * H * HD // N_DEV, axis=2)
    g_conv_ff = lax.dynamic_slice_in_dim(red["conv_ff"], me * n_up, n_up, axis=2)

    mine = [jnp.stack([_rs_sum(received[i][t], "rs_sum_" + t) for i in range(DEPTH)]) for t in tags]
    g_w_in = mine[0][:, :, :n_in]
    g_w_out = mine[1]
    g_w_up = mine[2][:, :, :n_up]
    g_w_down = mine[3]

    grads = dict(ada_w=g_ada_w, ada_b=g_ada_b, norm1_w=red["norm1"], w_in=g_w_in, conv_a_w=g_conv_a,
                 norm_a_w=red["norm_a"], conv_qkv_w=g_conv_qkv, a_log=red["a_log"], dt_bias=red["dt_bias"],
                 norm_dn_w=red["norm_dn"], w_out=g_w_out, norm2_w=red["norm2"], w_up=g_w_up, conv_ff_w=g_conv_ff,
                 w_down=g_w_down, norm_f_w=g_norm_f)
    weights = dict(ada_w=ada_w, ada_b=ada_b, norm1_w=norm1_w, w_in=w_in, conv_a_w=conv_a_w, norm_a_w=norm_a_w,
                   conv_qkv_w=conv_qkv_w, a_log=a_log, dt_bias=dt_bias, norm_dn_w=norm_dn_w, w_out=w_out,
                   norm2_w=norm2_w, w_up=w_up, conv_ff_w=conv_ff_w, w_down=w_down, norm_f_w=norm_f_w)
    ms = dict(ada_w=m_ada_w, ada_b=m_ada_b, norm1_w=m_norm1_w, w_in=m_w_in, conv_a_w=m_conv_a_w, norm_a_w=m_norm_a_w,
              conv_qkv_w=m_conv_qkv_w, a_log=m_a_log, dt_bias=m_dt_bias, norm_dn_w=m_norm_dn_w, w_out=m_w_out,
              norm2_w=m_norm2_w, w_up=m_w_up, conv_ff_w=m_conv_ff_w, w_down=m_w_down, norm_f_w=m_norm_f_w)
    vs_ = dict(ada_w=v_ada_w, ada_b=v_ada_b, norm1_w=v_norm1_w, w_in=v_w_in, conv_a_w=v_conv_a_w, norm_a_w=v_norm_a_w,
               conv_qkv_w=v_conv_qkv_w, a_log=v_a_log, dt_bias=v_dt_bias, norm_dn_w=v_norm_dn_w, w_out=v_w_out,
               norm2_w=v_norm2_w, w_up=v_w_up, conv_ff_w=v_conv_ff_w, w_down=v_w_down, norm_f_w=v_norm_f_w)
    names = list(weights)
    big_names = ["ada_w", "w_in", "w_out", "w_up", "w_down"]
    delta, new_m, new_v = {}, {}, {}
    for n in big_names:
        shp = weights[n].shape
        two = lambda t: t.reshape(-1, shp[-1])
        dl, nm, nv = _adamw(two(weights[n]), two(grads[n]), two(ms[n]), two(vs_[n]), "adamw_" + n)
        delta[n], new_m[n], new_v[n] = dl.reshape(shp), nm.reshape(shp), nv.reshape(shp)
    small_names = [n for n in names if n not in big_names]

    def pack(dct):
        return _pad_rows(jnp.concatenate([dct[n].reshape(-1) for n in small_names]), SUB * LANES).reshape(-1, LANES)

    dl, nm, nv = _adamw(pack(weights), pack(grads), pack(ms), pack(vs_), "adamw_small")
    off = 0
    for n in small_names:
        sz, shp = weights[n].size, weights[n].shape
        delta[n] = dl.reshape(-1)[off:off + sz].reshape(shp)
        new_m[n] = nm.reshape(-1)[off:off + sz].reshape(shp)
        new_v[n] = nv.reshape(-1)[off:off + sz].reshape(shp)
        off += sz

    return (loss, grad_x, *[grads[n] for n in names], *[delta[n] for n in names],
            *[new_m[n] for n in names], *[new_v[n] for n in names])
```

```python
import functools
import math

import jax
import jax.numpy as jnp
from jax import lax
from jax.experimental import pallas as pl
from jax.experimental.pallas import tpu as pltpu
from jax.experimental.pallas import tpu_sc as plsc

F32 = jnp.float32
MXU = jnp.bfloat16

D = 1024
DEPTH = 4
N_MOD = 6
AW = 512
A_GROUP = 64
H = 4
HD = 128
CK = 64
DFF = 2816
P_IN = 3592
P_PAD = 3712
EPS = 1e-6
N_DEV = 8
LANES = 128
SUB = 8
VMEM_LIMIT = 56 * 1024 * 1024

ADAM_LR, ADAM_B1, ADAM_B2, ADAM_EPS, ADAM_WD, ADAM_STEP = 0.001, 0.9, 0.999, 1e-08, 0.01, 10

NN = ((1,), (0,))
NT = ((1,), (1,))
TN = ((0,), (0,))
HI = lax.Precision.HIGHEST
MESH = pl.DeviceIdType.MESH


def _dot(a, b, dims, prec=None):
    if prec is None:
        a = a.astype(MXU) if a.dtype == F32 else a
        b = b.astype(MXU) if b.dtype == F32 else b
    return lax.dot_general(a, b, (dims, ((), ())), precision=prec, preferred_element_type=F32)


def _params(n_grid=0, limit=VMEM_LIMIT):
    sem = ("arbitrary",) * n_grid if n_grid else None
    return pltpu.CompilerParams(dimension_semantics=sem, vmem_limit_bytes=limit)


def _tile(n, want):
    if n <= want:
        return n
    t = want - want % SUB
    while n % t:
        t -= SUB
    assert t > 0, (n, want)
    return t


def _full(shape):
    nd = len(shape)
    return pl.BlockSpec(shape, lambda *_: (0,) * nd)


def _sig(x):
    return jax.nn.sigmoid(x)


def _rms(x):
    r = lax.rsqrt(jnp.mean(x * x, axis=-1, keepdims=True) + EPS)
    return x * r, r


def _rms_bwd(dn, n, r):
    return r * (dn - n * jnp.mean(dn * n, axis=-1, keepdims=True))


def _l2_bwd(dn, n, r):
    return r * (dn - n * jnp.sum(dn * n, axis=-1, keepdims=True))


def _sum0(x):
    return jnp.sum(x, axis=0, keepdims=True)


def _shift_down(x, s, halo):
    ext = jnp.concatenate([halo, x], axis=0)
    return pltpu.roll(ext, s, 0)[SUB:, :]


def _shift_up(x, s, halo):
    t = x.shape[0]
    ext = jnp.concatenate([x, halo], axis=0)
    return pltpu.roll(ext, t + SUB - s, 0)[:t, :]


def _conv_fwd(x, w_ref, width, halo):
    sh = [x] + [_shift_down(x, s, halo) for s in range(1, width)]
    out = w_ref[width - 1:width, :] * sh[0]
    for s in range(1, width):
        out = out + w_ref[width - 1 - s:width - s, :] * sh[s]
    return out, sh


def _blockdiag_mean(n, group):
    r = lax.shift_right_logical(lax.broadcasted_iota(jnp.int32, (n, n), 0), int(math.log2(group)))
    c = lax.shift_right_logical(lax.broadcasted_iota(jnp.int32, (n, n), 1), int(math.log2(group)))
    return jnp.where(r == c, 1.0 / group, 0.0).astype(F32)


def _softplus(x):
    return jnp.maximum(x, 0.0) + jnp.log(1.0 + jnp.exp(-jnp.abs(x)))


def _my_place():
    return lax.axis_index("x"), lax.axis_index("y"), lax.axis_index("c")


def _all_gather(shards, name, in_vmem):
    nt = len(shards)

    def body(*refs):
        x_refs, out_refs = refs[:nt], refs[nt:2 * nt]
        send_sems, recv_sems, local_sems = refs[2 * nt:]
        x, y, c = _my_place()
        me, sibling = (x, y, c), (x, y, 1 - c)
        chips = [(1 - x, y), (x, 1 - y), (1 - x, 1 - y)]
        everything = []
        for t in range(nt):
            x_ref, out_ref = x_refs[t], out_refs[t]

            def blk(px, py, pc, out_ref=out_ref):
                return out_ref.at[4 * px + 2 * py + pc]

            def copy(k, block, to, src=None, t=t, blk=blk):
                return pltpu.make_async_remote_copy(
                    src_ref=blk(*block) if src is None else src, dst_ref=blk(*block),
                    send_sem=send_sems.at[7 * t + k], recv_sem=recv_sems.at[7 * t + k], device_id=to, device_id_type=MESH)

            mine = pltpu.make_async_copy(x_ref, blk(*me), local_sems.at[t])
            mine.start()
            first = [copy(0, me, sibling, src=x_ref)]
            first += [copy(1 + j, me, (*chip, c), src=x_ref) for j, chip in enumerate(chips)]
            for cp in first:
                cp.start()
            everything.append((copy, mine, first))
        sends = []
        for copy, mine, first in everything:
            passed = [copy(4 + j, (*chip, c), sibling) for j, chip in enumerate(chips)]
            for j, chip in enumerate(chips):
                copy(1 + j, (*chip, c), me).wait_recv()
                passed[j].start()
            sends += first + passed
        for copy, mine, first in everything:
            copy(0, sibling, me).wait_recv()
            for j, chip in enumerate(chips):
                copy(4 + j, (*chip, 1 - c), me).wait_recv()
        for cp in sends:
            cp.wait_send()
        for copy, mine, first in everything:
            mine.wait()

    space = pltpu.VMEM if in_vmem else pl.ANY
    return pl.pallas_call(
        body, name=name,
        out_shape=[jax.ShapeDtypeStruct((N_DEV,) + s.shape, s.dtype) for s in shards],
        in_specs=[pl.BlockSpec(memory_space=space)] * nt,
        out_specs=[pl.BlockSpec(memory_space=space)] * nt,
        scratch_shapes=[pltpu.SemaphoreType.DMA((7 * nt,)), pltpu.SemaphoreType.DMA((7 * nt,)),
                        pltpu.SemaphoreType.DMA((nt,))],
        compiler_params=pltpu.CompilerParams(vmem_limit_bytes=VMEM_LIMIT),
    )(*shards)


def _all_gather_async(shards, name, collective_id):
    nt = len(shards)
    hbm = pltpu.MemorySpace.HBM
    x_refs = [jax.new_ref(s, memory_space=hbm) for s in shards]
    out_refs = [jax.empty_ref(jax.ShapeDtypeStruct((N_DEV,) + s.shape, s.dtype), memory_space=hbm) for s in shards]

    @pl.kernel(mesh=plsc.ScalarSubcoreMesh(axis_name="sequencer", num_cores=1), name=name,
               scratch_types=(pltpu.SemaphoreType.DMA((7 * nt,)), pltpu.SemaphoreType.DMA((7 * nt,)),
                              pltpu.SemaphoreType.DMA((nt,))),
               compiler_params=pltpu.CompilerParams(collective_id=collective_id))
    def launch(send_sems, recv_sems, local_sems):
        x, y, c = _my_place()
        me, sibling = (x, y, c), (x, y, 1 - c)
        chips = [(1 - x, y), (x, 1 - y), (1 - x, 1 - y)]
        barrier = pltpu.get_barrier_semaphore()
        for peer in [sibling] + [(*chip, c) for chip in chips]:
            pl.semaphore_signal(barrier, inc=1, device_id=peer, device_id_type=MESH)
        pl.semaphore_wait(barrier, 4)
        everything = []
        for t in range(nt):
            x_ref, out_ref = x_refs[t], out_refs[t]

            def blk(px, py, pc, out_ref=out_ref):
                return out_ref.at[4 * px + 2 * py + pc]

            def copy(k, block, to, src=None, t=t, blk=blk):
                return pltpu.make_async_remote_copy(
                    src_ref=blk(*block) if src is None else src, dst_ref=blk(*block),
                    send_sem=send_sems.at[7 * t + k], recv_sem=recv_sems.at[7 * t + k], device_id=to, device_id_type=MESH)

            mine = pltpu.make_async_copy(x_ref, blk(*me), local_sems.at[t])
            mine.start()
            first = [copy(0, me, sibling, src=x_ref)]
            first += [copy(1 + j, me, (*chip, c), src=x_ref) for j, chip in enumerate(chips)]
            for cp in first:
                cp.start()
            everything.append((copy, mine, first))
        sends = []
        for copy, mine, first in everything:
            passed = [copy(4 + j, (*chip, c), sibling) for j, chip in enumerate(chips)]
            for j, chip in enumerate(chips):
                copy(1 + j, (*chip, c), me).wait_recv()
                passed[j].start()
            sends += first + passed
        for copy, mine, first in everything:
            copy(0, sibling, me).wait_recv()
            for j, chip in enumerate(chips):
                copy(4 + j, (*chip, 1 - c), me).wait_recv()
        for cp in sends:
            cp.wait_send()
        for copy, mine, first in everything:
            mine.wait()

    launch()
    return [r[...] for r in out_refs]


def _rs_exchange_async(srcs, name, collective_id):
    nt = len(srcs)
    hbm = pltpu.MemorySpace.HBM
    src_refs = [jax.new_ref(s, memory_space=hbm) for s in srcs]
    out_refs = [jax.empty_ref(jax.ShapeDtypeStruct(s.shape, s.dtype), memory_space=hbm) for s in srcs]
    flips = [(fx, fy, fc) for fx in (0, 1) for fy in (0, 1) for fc in (0, 1)][1:]

    @pl.kernel(mesh=plsc.ScalarSubcoreMesh(axis_name="sequencer", num_cores=1), name=name,
               scratch_types=(pltpu.SemaphoreType.DMA((7 * nt,)), pltpu.SemaphoreType.DMA((7 * nt,)),
                              pltpu.SemaphoreType.DMA((nt,))),
               compiler_params=pltpu.CompilerParams(collective_id=collective_id))
    def launch(send_sems, recv_sems, local_sems):
        x, y, c = _my_place()
        me = 4 * x + 2 * y + c
        peers = [(1 - x if fx else x, 1 - y if fy else y, 1 - c if fc else c) for fx, fy, fc in flips]
        barrier = pltpu.get_barrier_semaphore()
        for peer in peers:
            pl.semaphore_signal(barrier, inc=1, device_id=peer, device_id_type=MESH)
        pl.semaphore_wait(barrier, len(peers))
        own = [pltpu.make_async_copy(src_refs[t].at[me], out_refs[t].at[me], local_sems.at[t]) for t in range(nt)]
        copies = [pltpu.make_async_remote_copy(
            src_ref=src_refs[t].at[4 * px + 2 * py + pc], dst_ref=out_refs[t].at[me],
            send_sem=send_sems.at[7 * t + f], recv_sem=recv_sems.at[7 * t + f],
            device_id=(px, py, pc), device_id_type=MESH) for t in range(nt) for f, (px, py, pc) in enumerate(peers)]
        for cp in own + copies:
            cp.start()
        for cp in copies + own:
            cp.wait()

    launch()
    return [r[...] for r in out_refs]


def _rs_sum(recv, name):
    _, r, n = recv.shape
    tr = _tile(r, 512)

    def body(r_ref, o_ref):
        s = r_ref[0].astype(F32)
        for k in range(1, N_DEV):
            s = s + r_ref[k].astype(F32)
        o_ref[...] = s

    return pl.pallas_call(
        body, name=name, grid=(r // tr,),
        in_specs=[pl.BlockSpec((N_DEV, tr, n), lambda i: (0, i, 0))],
        out_specs=pl.BlockSpec((tr, n), lambda i: (i, 0)),
        out_shape=jax.ShapeDtypeStruct((r, n), F32), compiler_params=_params(1),
    )(recv)


def _shard_windows(n_shard, count, first=0):
    out = []
    for k in range(first, first + count):
        off = n_shard * k
        a, s = off // LANES, off % LANES
        out.append((a, s, -(-(s + n_shard) // LANES) * LANES))
    return out


def _fit_lanes(x, width):
    have = x.shape[1]
    if have < width:
        return jnp.concatenate([x, jnp.zeros((x.shape[0], width - have), x.dtype)], axis=-1)
    return x[:, :width]


def _interleave_cols(g, n_shard, w_out, name):
    nd, nl, rows, wpad = g.shape
    rb = _tile(rows, 256)
    wins = _shard_windows(n_shard, nd)

    def body(g_ref, o_ref, acc):
        acc[...] = jnp.zeros_like(acc)
        for k, (a, s, win) in enumerate(wins):
            xk = _fit_lanes(g_ref[k].astype(F32), win)
            if s:
                xk = pltpu.roll(xk, s, 1)
            acc[:, a * LANES:a * LANES + win] += xk
        o_ref[...] = acc[...].astype(o_ref.dtype)

    return pl.pallas_call(
        body, name=name, grid=(nl, rows // rb),
        in_specs=[pl.BlockSpec((nd, None, rb, wpad), lambda l, i: (0, l, i, 0))],
        out_specs=pl.BlockSpec((None, rb, w_out), lambda l, i: (l, i, 0)),
        out_shape=jax.ShapeDtypeStruct((nl, rows, w_out), g.dtype),
        scratch_shapes=[pltpu.VMEM((rb, w_out), F32)],
        compiler_params=_params(2),
    )(g)


def _sum_devices(g):
    _, r, n = g.shape

    def body(g_ref, o_ref):
        s = g_ref[0]
        for t in range(1, N_DEV):
            s = s + g_ref[t]
        o_ref[...] = s

    return pl.pallas_call(
        body, name="sum_devices", out_shape=jax.ShapeDtypeStruct((r, n), F32),
        in_specs=[pl.BlockSpec(memory_space=pltpu.VMEM)], out_specs=pl.BlockSpec(memory_space=pltpu.VMEM),
        compiler_params=pltpu.CompilerParams(vmem_limit_bytes=VMEM_LIMIT),
    )(g)


def _mod_fwd(c_all, ada_w, ada_b_cols):
    nl, _, nc = ada_w.shape

    def body(c_ref, w_ref, b_ref, o_ref):
        cv = c_ref[...]
        act = (cv * _sig(cv)).astype(MXU)
        o_ref[...] = _dot(act, w_ref[...].astype(MXU), NN) + b_ref[...]

    return pl.pallas_call(
        body, name="mod_fwd", grid=(nl,),
        in_specs=[_full((16, D)), pl.BlockSpec((None, D, nc), lambda i: (i, 0, 0)),
                  pl.BlockSpec((None, 1, nc), lambda i: (i, 0, 0))],
        out_specs=pl.BlockSpec((None, 16, nc), lambda i: (i, 0, 0)),
        out_shape=jax.ShapeDtypeStruct((nl, 16, nc), F32), compiler_params=_params(1),
    )(c_all, ada_w, ada_b_cols)


def _mod_bwd(c_all, dmod_cols):
    nl, _, nc = dmod_cols.shape

    def body(c_ref, d_ref, o_ref):
        cv = c_ref[...]
        act = (cv * _sig(cv)).astype(MXU)
        o_ref[...] = _dot(act, d_ref[...].astype(MXU), TN)

    return pl.pallas_call(
        body, name="mod_bwd", grid=(nl,),
        in_specs=[_full((16, D)), pl.BlockSpec((None, 16, nc), lambda i: (i, 0, 0))],
        out_specs=pl.BlockSpec((None, D, nc), lambda i: (i, 0, 0)),
        out_shape=jax.ShapeDtypeStruct((nl, D, nc), F32), compiler_params=_params(1),
    )(c_all, dmod_cols)


def _gate_small(s, sp_ref):
    lane = lax.broadcasted_iota(jnp.int32, s.shape, 1)
    a = -jnp.exp(sp_ref[0:1, :])
    xb = s + sp_ref[1:2, :]
    beta = _sig(s)
    g = a * _softplus(xb)
    return lane, a, xb, beta, g


def _in_pre_fwd(x, modrows, vec, w_in, pa, cq, sp):
    L = x.shape[0]
    T = _tile(L, 256)
    scale = HD ** -0.5
    w3 = 3 * AW + 3 * H * HD

    def body(x_ref, mod_ref, vec_ref, w_ref, pa_ref, cq_ref, sp_ref,
             p_ref, h_ref, qn_ref, kn_ref, vs_ref, gb_ref, ya_ref, cu_ref, qc_ref, u_carry, q_carry):
        @pl.when(pl.program_id(0) == 0)
        def _():
            u_carry[...] = jnp.zeros_like(u_carry)
            q_carry[...] = jnp.zeros_like(q_carry)

        n, _ = _rms(x_ref[...])
        hb = (n * vec_ref[0:1, :] * (1.0 + mod_ref[1:2, :]) + mod_ref[0:1, :]).astype(MXU)
        h_ref[...] = hb
        pm_a = _dot(hb, w_ref[:, 0:3 * AW], NN)
        p_ref[:, 0:3 * AW] = pm_a
        pm_q = _dot(hb, w_ref[:, 3 * AW:w3], NN)
        p_ref[:, 3 * AW:w3] = pm_q

        a_b = pm_a[:, 0:AW]
        u = pm_a[:, AW:2 * AW] * pm_a[:, 2 * AW:3 * AW]
        cu, _ = _conv_fwd(u, pa_ref, 3, u_carry[...])
        cu_ref[...] = cu.astype(MXU)
        u_carry[...] = u[T - SUB:T, :]
        yp = a_b * cu
        ms = _dot_f32(yp * yp, _blockdiag_mean(AW, A_GROUP), NN, exact="b")
        ya_ref[...] = (yp * lax.rsqrt(ms + EPS) * pa_ref[3:4, :]).astype(MXU)

        pm_z = _dot(hb, w_ref[:, w3:P_PAD], NN)
        p_ref[:, w3:P_PAD] = pm_z
        qkv = pm_q
        qc, _ = _conv_fwd(qkv, cq_ref, 4, q_carry[...])
        qc_ref[...] = qc.astype(MXU)
        q_carry[...] = qkv[T - SUB:T, :]
        qs = qc * _sig(qc)
        for h in range(H):
            q = qs[:, h * HD:(h + 1) * HD]
            qn_ref[:, h * HD:(h + 1) * HD] = q * (lax.rsqrt(jnp.sum(q * q, axis=-1, keepdims=True) + EPS) * scale)
            k = qs[:, (H + h) * HD:(H + h + 1) * HD]
            kn_ref[:, h * HD:(h + 1) * HD] = k * lax.rsqrt(jnp.sum(k * k, axis=-1, keepdims=True) + EPS)
        vs_ref[...] = qs[:, 2 * H * HD:3 * H * HD]

        lane, _, _, beta, g = _gate_small(pm_z[:, H * HD:H * HD + LANES], sp_ref)
        gb_ref[...] = jnp.where(lane < H, beta, jnp.where(lane < 2 * H, g, 0.0))

    row = lambda i: (i, 0)
    return pl.pallas_call(
        body, name="in_pre_fwd", grid=(L // T,),
        in_specs=[pl.BlockSpec((T, D), row), _full((SUB, D)), _full((SUB, D)), _full((D, P_PAD)),
                  _full((SUB, AW)), _full((SUB, 3 * H * HD)), _full((SUB, LANES))],
        out_specs=[pl.BlockSpec((T, P_PAD), row), pl.BlockSpec((T, D), row)]
        + [pl.BlockSpec((T, H * HD), row)] * 3 + [pl.BlockSpec((T, LANES), row), pl.BlockSpec((T, AW), row),
                                                  pl.BlockSpec((T, AW), row), pl.BlockSpec((T, 3 * H * HD), row)],
        out_shape=[jax.ShapeDtypeStruct((L, P_PAD), F32), jax.ShapeDtypeStruct((L, D), MXU)]
        + [jax.ShapeDtypeStruct((L, H * HD), F32)] * 3
        + [jax.ShapeDtypeStruct((L, LANES), F32), jax.ShapeDtypeStruct((L, AW), MXU),
           jax.ShapeDtypeStruct((L, AW), MXU), jax.ShapeDtypeStruct((L, 3 * H * HD), MXU)],
        scratch_shapes=[pltpu.VMEM((SUB, AW), F32), pltpu.VMEM((SUB, 3 * H * HD), F32)],
        compiler_params=_params(1),
    )(x, modrows, vec, w_in, pa, cq, sp)


def _gdr_masks():
    r = lax.broadcasted_iota(jnp.int32, (CK, CK), 0)
    c = lax.broadcasted_iota(jnp.int32, (CK, CK), 1)
    return r >= c, r > c


def _head_cols(gbt, h):
    return gbt[:, h:h + 1], gbt[:, H + h:H + h + 1]


def _split(x, parts):
    out = []
    for _ in range(parts):
        hi = x.astype(jnp.bfloat16)
        out.append(hi)
        x = x - hi.astype(F32)
    return out


def _dot_f32(a, b, dims, exact=None):
    if exact == "a":
        ab = a.astype(jnp.bfloat16)
        return sum(_dot(ab, t, dims) for t in _split(b, 3))
    if exact == "b":
        bb = b.astype(jnp.bfloat16)
        return sum(_dot(t, bb, dims) for t in _split(a, 3))
    ah, al = _split(a, 2)
    bh, bl = _split(b, 2)
    return _dot(ah, bh, dims) + _dot(ah, bl, dims) + _dot(al, bh, dims)


def _gdr_consts():
    causal, strict = _gdr_masks()
    return dict(causal=causal, strict=strict, tril=jnp.where(causal, 1.0, 0.0).astype(F32),
                eye=jnp.where(causal & jnp.logical_not(strict), 1.0, 0.0).astype(F32),
                bcast=jnp.full((CK, HD), 1.0 / HD, F32))


def _dots(a, b, dims):
    return [_dot(x, y, dims) for x, y in zip(a, b)]


def _dots_f32(a, b, dims, exact=None):
    n = len(a)
    if exact == "a":
        lhs = [[x.astype(jnp.bfloat16)] * 3 for x in a]
        rhs = [_split(y, 3) for y in b]
    elif exact == "b":
        lhs = [_split(x, 3) for x in a]
        rhs = [[y.astype(jnp.bfloat16)] * 3 for y in b]
    else:
        sa = [_split(x, 2) for x in a]
        sb = [_split(y, 2) for y in b]
        lhs = [[s[0], s[0], s[1]] for s in sa]
        rhs = [[s[0], s[1], s[0]] for s in sb]
    terms = [[_dot(lhs[i][t], rhs[i][t], dims) for i in range(n)] for t in range(3)]
    return [terms[0][i] + terms[1][i] + terms[2][i] for i in range(n)]


def _gdr_local(q, k, v, beta, g, cst, tinv=None):
    n = len(q)
    R = range(n)
    causal, strict = cst["causal"], cst["strict"]
    gc = _dots_f32([cst["tril"]] * n, [jnp.broadcast_to(g[i], (CK, HD)) for i in R], NN, exact="a")
    g_row = _dots_f32([cst["bcast"]] * n, gc, NT, exact="a")
    decay = [jnp.where(causal, jnp.exp(jnp.where(causal, gc[i][:, 0:CK] - g_row[i], 0.0)), 0.0) for i in R]
    eg = [jnp.exp(gc[i]) for i in R]
    gl = [gc[i][CK - 1:CK, :] for i in R]
    ek = [jnp.exp(gl[i] - gc[i]) for i in R]
    cd = [jnp.exp(gl[i]) for i in R]
    kb = [k[i] * beta[i] for i in R]
    pk = _dots(kb, k, NT)
    if tinv is None:
        xp = [-jnp.where(strict, pk[i] * decay[i], 0.0) for i in R]
        tinv = [cst["eye"] + xp[i] for i in R]
        for _ in range(5):
            xp = _dots_f32(xp, xp, NN)
            tx = _dots_f32(tinv, xp, NN)
            tinv = [tinv[i] + tx[i] for i in R]
    u = _dots(tinv, [v[i] * beta[i] for i in R], NN)
    w = _dots(tinv, [kb[i] * eg[i] for i in R], NN)
    qk = _dots(q, k, NT)
    intra = [jnp.where(causal, qk[i] * decay[i], 0.0) for i in R]
    return dict(decay=decay, eg=eg, ek=ek, cd=cd, kb=kb, pk=pk, tinv=tinv, u=u, w=w, qk=qk, intra=intra,
                q_dec=[q[i] * eg[i] for i in R], k_dec=[k[i] * ek[i] for i in R])


GDR_SUB = 8


def _gdr_fwd(qn, kn, vs, gb):
    L = qn.shape[0]
    nc = L // CK
    cb = min(8, nc)
    rb = cb * CK
    nb = nc // cb
    nsub = GDR_SUB if cb % GDR_SUB == 0 else 1

    def body(q_ref, k_ref, v_ref, gb_ref, o_ref, st_ref, ti_ref, s_ref):
        @pl.when(pl.program_id(0) == 0)
        def _():
            s_ref[...] = jnp.zeros_like(s_ref)

        cst = _gdr_consts()
        heads = range(H)

        def group(gi, carry):
            rows = [pl.ds(pl.multiple_of((gi * nsub + j) * CK, CK), CK) for j in range(nsub)]
            chains = [(j, h) for j in range(nsub) for h in heads]
            gbt = [gb_ref[rows[j], :] for j in range(nsub)]
            cols = lambda h: slice(h * HD, (h + 1) * HD)
            t = _gdr_local([q_ref[rows[j], cols(h)] for j, h in chains], [k_ref[rows[j], cols(h)] for j, h in chains],
                           [v_ref[rows[j], cols(h)] for j, h in chains],
                           [_head_cols(gbt[j], h)[0] for j, h in chains], [_head_cols(gbt[j], h)[1] for j, h in chains], cst)
            s = [s_ref[h] for h in heads]
            for j in range(nsub):
                at = lambda key: [t[key][j * H + h] for h in heads]
                for h in heads:
                    st_ref[h, gi * nsub + j] = s[h]
                    ti_ref[h, gi * nsub + j] = t["tinv"][j * H + h]
                ws = _dots(at("w"), s, NN)
                v_new = [u_h - ws_h for u_h, ws_h in zip(at("u"), ws)]
                o_s = _dots(at("q_dec"), s, NN)
                o_v = _dots(at("intra"), v_new, NN)
                kv = _dots(at("k_dec"), v_new, TN)
                cd = at("cd")
                for h in heads:
                    o_ref[rows[j], cols(h)] = o_s[h] + o_v[h]
                s = [s[h] * cd[h] + kv[h] for h in heads]
            for h in heads:
                s_ref[h] = s[h]
            return carry

        lax.fori_loop(0, cb // nsub, group, 0)

    blk = pl.BlockSpec((rb, H * HD), lambda b: (b, 0))
    return pl.pallas_call(
        body, name="gdr_fwd", grid=(nb,),
        in_specs=[blk, blk, blk, pl.BlockSpec((rb, LANES), lambda b: (b, 0))],
        out_specs=[blk, pl.BlockSpec((H, cb, HD, HD), lambda b: (0, b, 0, 0)),
                   pl.BlockSpec((H, cb, CK, CK), lambda b: (0, b, 0, 0))],
        out_shape=[jax.ShapeDtypeStruct((L, H * HD), F32), jax.ShapeDtypeStruct((H, nc, HD, HD), F32),
                   jax.ShapeDtypeStruct((H, nc, CK, CK), F32)],
        scratch_shapes=[pltpu.VMEM((H, HD, HD), F32)],
        compiler_params=_params(1),
    )(qn, kn, vs, gb)


def _gdr_bwd(qn, kn, vs, gb, states, tinvs, do):
    L = qn.shape[0]
    nc = L // CK
    cb = min(8, nc)
    rb = cb * CK
    nb = nc // cb
    nsub = GDR_SUB if cb % GDR_SUB == 0 else 1

    def body(q_ref, k_ref, v_ref, gb_ref, st_ref, ti_ref, do_ref, dq_ref, dk_ref, dv_ref, dgb_ref, ds_ref):
        @pl.when(pl.program_id(0) == 0)
        def _():
            ds_ref[...] = jnp.zeros_like(ds_ref)

        cst = _gdr_consts()
        causal, strict = cst["causal"], cst["strict"]
        ones = jnp.ones((CK, HD), F32)
        row = lax.broadcasted_iota(jnp.int32, (CK, HD), 0)
        lane = lax.broadcasted_iota(jnp.int32, (CK, LANES), 1)

        heads = range(H)
        rsum = lambda x: jnp.sum(x, axis=-1, keepdims=True)

        def group(gj, carry):
            gi = cb // nsub - 1 - gj
            rows = [pl.ds(pl.multiple_of((gi * nsub + j) * CK, CK), CK) for j in range(nsub)]
            chains = [(j, h) for j in range(nsub) for h in heads]
            gbt = [gb_ref[rows[j], :] for j in range(nsub)]
            cols = lambda h: slice(h * HD, (h + 1) * HD)
            q_all = [q_ref[rows[j], cols(h)] for j, h in chains]
            k_all = [k_ref[rows[j], cols(h)] for j, h in chains]
            v_all = [v_ref[rows[j], cols(h)] for j, h in chains]
            beta_all = [_head_cols(gbt[j], h)[0] for j, h in chains]
            t = _gdr_local(q_all, k_all, v_all, beta_all, [_head_cols(gbt[j], h)[1] for j, h in chains], cst,
                           tinv=[ti_ref[h, gi * nsub + j] for j, h in chains])
            ds_out = [ds_ref[h] for h in heads]
            for j in reversed(range(nsub)):
                at = lambda key: [t[key][j * H + h] for h in heads]
                pick = lambda lst: [lst[j * H + h] for h in heads]
                q, k, v, beta = pick(q_all), pick(k_all), pick(v_all), pick(beta_all)
                u, w, tinv, decay = at("u"), at("w"), at("tinv"), at("decay")
                eg, ek, cd, kb = at("eg"), at("ek"), at("cd"), at("kb")
                q_dec, k_dec, intra, pk, qk = at("q_dec"), at("k_dec"), at("intra"), at("pk"), at("qk")
                s = [st_ref[h, gi * nsub + j] for h in heads]
                dout = [do_ref[rows[j], cols(h)] for h in heads]

                ws = _dots(w, s, NN)
                v_new = [u[h] - ws[h] for h in heads]
                dq_dec = _dots(dout, s, NT)
                qd = _dots(q_dec, dout, TN)
                di = _dots(dout, v_new, NT)
                dintra = [jnp.where(causal, di[h], 0.0) for h in heads]
                ido = _dots(intra, dout, TN)
                kds = _dots(k_dec, ds_out, NN)
                dv_new = [ido[h] + kds[h] for h in heads]
                dk_dec = _dots(v_new, ds_out, NT)
                dcd = [jnp.sum(jnp.sum(ds_out[h] * s[h], axis=1, keepdims=True), axis=0, keepdims=True) for h in heads]
                dvs = _dots(dv_new, s, NT)
                dw = [-dvs[h] for h in heads]
                wdv = _dots(w, dv_new, TN)
                ds_new = [qd[h] + ds_out[h] * cd[h] - wdv[h] for h in heads]
                dru = _dots(tinv, dv_new, TN)
                drw = _dots(tinv, dw, TN)
                dl1 = _dots(dru, u, NT)
                dl2 = _dots(drw, w, NT)
                dlower = [-jnp.where(strict, dl1[h] + dl2[h], 0.0) for h in heads]
                dv = [dru[h] * beta[h] for h in heads]
                dbeta = [rsum(dru[h] * v[h]) for h in heads]
                dgc = [rsum(drw[h] * kb[h]) * eg[h] for h in heads]
                dpk = [dlower[h] * decay[h] for h in heads]
                dqk = [dintra[h] * decay[h] for h in heads]
                dpk_k = _dots(dpk, k, NN)
                dkb = [drw[h] * eg[h] + dpk_k[h] for h in heads]
                dk1 = _dots(dpk, kb, TN)
                dq1 = _dots(dqk, k, NN)
                dk2 = _dots(dqk, q, TN)
                m = [(dlower[h] * pk[h] + dintra[h] * qk[h]) * decay[h] for h in heads]
                mcol = _dots_f32(m, [ones] * H, TN, exact="b")
                e = [rsum(dk_dec[h] * k_dec[h]) for h in heads]
                dgl = [jnp.sum(e[h], axis=0, keepdims=True) + dcd[h] * cd[h] for h in heads]
                dgc = [dgc[h] + rsum(m[h]) - mcol[h] + rsum(dq_dec[h] * q_dec[h]) - e[h]
                       + jnp.where(row == CK - 1, dgl[h], 0.0) for h in heads]
                dg = _dots_f32([cst["tril"]] * H, dgc, TN, exact="a")
                dgb = jnp.zeros((CK, LANES), F32)
                for h in heads:
                    dq_ref[rows[j], cols(h)] = dq1[h] + dq_dec[h] * eg[h]
                    dk_ref[rows[j], cols(h)] = dk1[h] + dk2[h] + dk_dec[h] * ek[h] + dkb[h] * beta[h]
                    dv_ref[rows[j], cols(h)] = dv[h]
                    db = dbeta[h] + rsum(dkb[h] * k[h])
                    dgb = dgb + jnp.where(lane == h, db, 0.0) + jnp.where(lane == H + h, dg[h], 0.0)
                dgb_ref[rows[j], :] = dgb
                ds_out = ds_new
            for h in heads:
                ds_ref[h] = ds_out[h]
            return carry

        lax.fori_loop(0, cb // nsub, group, 0)

    blk = pl.BlockSpec((rb, H * HD), lambda b: (nb - 1 - b, 0))
    sblk = pl.BlockSpec((rb, LANES), lambda b: (nb - 1 - b, 0))
    return pl.pallas_call(
        body, name="gdr_bwd", grid=(nb,),
        in_specs=[blk, blk, blk, sblk, pl.BlockSpec((H, cb, HD, HD), lambda b: (0, nb - 1 - b, 0, 0)),
                  pl.BlockSpec((H, cb, CK, CK), lambda b: (0, nb - 1 - b, 0, 0)), blk],
        out_specs=[blk, blk, blk, sblk],
        out_shape=[jax.ShapeDtypeStruct((L, H * HD), F32)] * 3 + [jax.ShapeDtypeStruct((L, LANES), F32)],
        scratch_shapes=[pltpu.VMEM((H, HD, HD), F32)],
        compiler_params=_params(1),
    )(qn, kn, vs, gb, states, tinvs, do)


def _post_fwd(o, p, ya, x, modrows, sp, w_out):
    L = x.shape[0]
    T = _tile(L, 512)

    def body(o_ref, z_ref, ya_ref, x_ref, mod_ref, sp_ref, w_ref, y_ref, x2_ref, yb_ref):
        ndw = sp_ref[2:3, :]
        z = z_ref[...]
        sz = z * _sig(z)
        parts = []
        for h in range(H):
            n, _ = _rms(o_ref[:, h * HD:(h + 1) * HD])
            parts.append(n * ndw * sz[:, h * HD:(h + 1) * HD])
        yb = jnp.concatenate(parts, axis=-1).astype(MXU)
        yb_ref[...] = yb
        y = _dot(ya_ref[...], w_ref[0:AW, :], NN) + _dot(yb, w_ref[AW:2 * AW, :], NN)
        y_ref[...] = y
        x2_ref[...] = x_ref[...] + mod_ref[2:3, :] * y

    row = lambda i: (i, 0)
    zcol = (3 * AW + 3 * H * HD) // (H * HD)
    return pl.pallas_call(
        body, name="post_fwd", grid=(L // T,),
        in_specs=[pl.BlockSpec((T, H * HD), row), pl.BlockSpec((T, H * HD), lambda i: (i, zcol)),
                  pl.BlockSpec((T, AW), row), pl.BlockSpec((T, D), row), _full((SUB, D)), _full((SUB, LANES)),
                  _full((D, D))],
        out_specs=[pl.BlockSpec((T, D), row), pl.BlockSpec((T, D), row), pl.BlockSpec((T, H * HD), row)],
        out_shape=[jax.ShapeDtypeStruct((L, D), F32), jax.ShapeDtypeStruct((L, D), F32),
                   jax.ShapeDtypeStruct((L, H * HD), MXU)],
        compiler_params=_params(1),
    )(o, p, ya, x, modrows, sp, w_out)


FF_COLS = 2
FF_CW = DFF // FF_COLS
FF_ROWS = 512


def _ffn_fwd_half(x2, modrows, vec, w_up, cff, w_down, j, d_prev):
    assert FF_COLS == 2
    L = x2.shape[0]
    T = _tile(L, FF_ROWS)
    nj = FF_COLS
    last = d_prev is not None

    def body(*refs):
        x_ref, mod_ref, vec_ref, wg_ref, wu_ref, cg_ref, cu_ref, wd_ref = refs[:8]
        if last:
            dp_ref, gp_ref, up_ref, gc_ref, uc_ref, f_ref, d_ref, x3_ref, carry_g, carry_u = refs[8:]
        else:
            h_ref, gp_ref, up_ref, gc_ref, uc_ref, f_ref, d_ref, carry_g, carry_u = refs[8:]

        @pl.when(pl.program_id(0) == 0)
        def _():
            carry_g[...] = jnp.zeros_like(carry_g)
            carry_u[...] = jnp.zeros_like(carry_u)

        xv = x_ref[...]
        n, _ = _rms(xv)
        hb = (n * vec_ref[1:2, :] * (1.0 + mod_ref[4:5, :]) + mod_ref[3:4, :]).astype(MXU)
        if not last:
            h_ref[...] = hb
        g = _dot(hb, wg_ref[...], NN)
        u = _dot(hb, wu_ref[...], NN)
        gp_ref[...] = g.astype(MXU)
        up_ref[...] = u.astype(MXU)
        gc, _ = _conv_fwd(g, cg_ref, 3, carry_g[...])
        uc, _ = _conv_fwd(u, cu_ref, 3, carry_u[...])
        carry_g[...] = g[T - SUB:T, :]
        carry_u[...] = u[T - SUB:T, :]
        gc_ref[...] = gc.astype(MXU)
        uc_ref[...] = uc.astype(MXU)
        fb = (gc * _sig(gc) * uc).astype(MXU)
        f_ref[...] = fb
        part = _dot(fb, wd_ref[...], NN)
        if last:
            dv = dp_ref[...] + part
            d_ref[...] = dv
            x3_ref[...] = xv + mod_ref[5:6, :] * dv
        else:
            d_ref[...] = part

    row = lambda i: (i, 0)
    rowD = pl.BlockSpec((T, D), row)
    rowC = pl.BlockSpec((T, FF_CW), row)
    in_specs = [rowD, _full((SUB, D)), _full((SUB, D)),
                pl.BlockSpec((D, FF_CW), lambda i: (0, j)), pl.BlockSpec((D, FF_CW), lambda i: (0, nj + j)),
                pl.BlockSpec((SUB, FF_CW), lambda i: (0, j)), pl.BlockSpec((SUB, FF_CW), lambda i: (0, nj + j)),
                pl.BlockSpec((FF_CW, D), lambda i: (j, 0))]
    half = [jax.ShapeDtypeStruct((L, FF_CW), MXU)] * 5
    args = [x2, modrows, vec, w_up, w_up, cff, cff, w_down]
    if last:
        in_specs.append(rowD)
        args.append(d_prev)
        out_specs = [rowC] * 5 + [rowD, rowD]
        out_shape = half + [jax.ShapeDtypeStruct((L, D), F32), jax.ShapeDtypeStruct((L, D), F32)]
    else:
        out_specs = [rowD] + [rowC] * 5 + [rowD]
        out_shape = [jax.ShapeDtypeStruct((L, D), MXU)] + half + [jax.ShapeDtypeStruct((L, D), F32)]
    return pl.pallas_call(
        body, name="ffn_fwd_last" if last else "ffn_fwd_first", grid=(L // T,),
        in_specs=in_specs, out_specs=out_specs, out_shape=out_shape,
        scratch_shapes=[pltpu.VMEM((SUB, FF_CW), F32), pltpu.VMEM((SUB, FF_CW), F32)],
        compiler_params=_params(1),
    )(*args)


def _ffn_bwd_half(dx3, modrows, gpre, upre, gcv, ucv, cff, w_down, w_up, j, tail):
    assert FF_COLS == 2
    L = dx3.shape[0]
    T = _tile(L, FF_ROWS)
    ni, nj = L // T, FF_COLS
    last = tail is not None

    def body(*refs):
        dx3_ref, mod_ref, gp_ref, up_ref, gc_ref, uc_ref, cg_ref, cu_ref, wd_ref, wg_ref, wu_ref = refs[:11]
        if last:
            (d_ref, x2_ref, vec_ref, dhp_ref, dgp_ref, dup_ref, dx2_ref, accv_ref, dcg_ref, dcu_ref,
             carry_g, carry_u) = refs[11:]
        else:
            dd_ref, dgp_ref, dup_ref, dh_ref, dcg_ref, dcu_ref, carry_g, carry_u = refs[11:]
        i = pl.program_id(0)

        @pl.when(i == 0)
        def _():
            carry_g[...] = jnp.zeros_like(carry_g)
            carry_u[...] = jnp.zeros_like(carry_u)
            dcg_ref[...] = jnp.zeros_like(dcg_ref)
            dcu_ref[...] = jnp.zeros_like(dcu_ref)
            if last:
                accv_ref[...] = jnp.zeros_like(accv_ref)

        dx3v = dx3_ref[...]
        ddb = (mod_ref[5:6, :] * dx3v).astype(MXU)
        if not last:
            dd_ref[...] = ddb
        g, u = gp_ref[...].astype(F32), up_ref[...].astype(F32)
        gc, uc = gc_ref[...].astype(F32), uc_ref[...].astype(F32)
        sg = _sig(gc)
        df = _dot(ddb, wd_ref[...], NT)
        duc = df * (gc * sg)
        dgc = df * uc * (sg * (1.0 + gc * (1.0 - sg)))
        dgs = [dgc] + [_shift_up(dgc, s, carry_g[...]) for s in (1, 2)]
        dus = [duc] + [_shift_up(duc, s, carry_u[...]) for s in (1, 2)]
        for s in range(3):
            dcg_ref[2 - s:3 - s, :] += _sum0(dgs[s] * g)
            dcu_ref[2 - s:3 - s, :] += _sum0(dus[s] * u)
        dg = (cg_ref[2:3, :] * dgs[0] + cg_ref[1:2, :] * dgs[1] + cg_ref[0:1, :] * dgs[2]).astype(MXU)
        du = (cu_ref[2:3, :] * dus[0] + cu_ref[1:2, :] * dus[1] + cu_ref[0:1, :] * dus[2]).astype(MXU)
        carry_g[...] = dgc[0:SUB, :]
        carry_u[...] = duc[0:SUB, :]
        dgp_ref[...] = dg
        dup_ref[...] = du
        dh = _dot(dg, wg_ref[...], NT) + _dot(du, wu_ref[...], NT)
        if last:
            dh = dh + dhp_ref[...]
            accv_ref[0:1, :] += _sum0(dx3v * d_ref[...])
            n, r = _rms(x2_ref[...])
            nw, sc = vec_ref[1:2, :], mod_ref[4:5, :]
            accv_ref[1:2, :] += _sum0(dh)
            accv_ref[2:3, :] += _sum0(dh * n * nw)
            accv_ref[3:4, :] += _sum0(dh * n * (1.0 + sc))
            dx2_ref[...] = _rms_bwd(dh * nw * (1.0 + sc), n, r) + dx3v
        else:
            dh_ref[...] = dh

    row = lambda i: (ni - 1 - i, 0)
    rowD = pl.BlockSpec((T, D), row)
    rowC = pl.BlockSpec((T, FF_CW), row)
    in_specs = [rowD, _full((SUB, D)), rowC, rowC, rowC, rowC,
                pl.BlockSpec((SUB, FF_CW), lambda i: (0, j)), pl.BlockSpec((SUB, FF_CW), lambda i: (0, nj + j)),
                pl.BlockSpec((FF_CW, D), lambda i: (j, 0)),
                pl.BlockSpec((D, FF_CW), lambda i: (0, j)), pl.BlockSpec((D, FF_CW), lambda i: (0, nj + j))]
    args = [dx3, modrows, gpre, upre, gcv, ucv, cff, cff, w_down, w_up, w_up]
    halfb = [jax.ShapeDtypeStruct((L, FF_CW), MXU), jax.ShapeDtypeStruct((L, FF_CW), MXU)]
    dconv = [jax.ShapeDtypeStruct((SUB, FF_CW), F32)] * 2
    if last:
        d, x2, vec, dh_prev = tail
        in_specs += [rowD, rowD, _full((SUB, D)), rowD]
        args += [d, x2, vec, dh_prev]
        out_specs = [rowC, rowC, rowD, _full((SUB, D)), _full((SUB, FF_CW)), _full((SUB, FF_CW))]
        out_shape = halfb + [jax.ShapeDtypeStruct((L, D), F32), jax.ShapeDtypeStruct((SUB, D), F32)] + dconv
    else:
        out_specs = [rowD, rowC, rowC, rowD, _full((SUB, FF_CW)), _full((SUB, FF_CW))]
        out_shape = [jax.ShapeDtypeStruct((L, D), MXU)] + halfb + [jax.ShapeDtypeStruct((L, D), F32)] + dconv
    return pl.pallas_call(
        body, name="ffn_bwd_last" if last else "ffn_bwd_first", grid=(ni,),
        in_specs=in_specs, out_specs=out_specs, out_shape=out_shape,
        scratch_shapes=[pltpu.VMEM((SUB, FF_CW), F32), pltpu.VMEM((SUB, FF_CW), F32)],
        compiler_params=_params(1),
    )(*args)


def _final(x, target, nf):
    L = x.shape[0]
    T = _tile(L, 256)

    def body(x_ref, t_ref, nf_ref, dx_ref, acc_ref):
        @pl.when(pl.program_id(0) == 0)
        def _():
            acc_ref[...] = jnp.zeros_like(acc_ref)

        n, r = _rms(x_ref[...])
        w = nf_ref[0:1, :]
        err = n * w - t_ref[...]
        acc_ref[0:1, :] += (0.5 / D) * _sum0(err * err)
        dy = err * (1.0 / D)
        acc_ref[1:2, :] += _sum0(dy * n)
        dx_ref[...] = _rms_bwd(dy * w, n, r)

    row = lambda i: (i, 0)
    return pl.pallas_call(
        body, name="final_norm_loss", grid=(L // T,),
        in_specs=[pl.BlockSpec((T, D), row), pl.BlockSpec((T, D), row), _full((SUB, D))],
        out_specs=[pl.BlockSpec((T, D), row), _full((SUB, D))],
        out_shape=[jax.ShapeDtypeStruct((L, D), F32), jax.ShapeDtypeStruct((SUB, D), F32)],
        compiler_params=_params(1),
    )(x, target, nf)


def _post_bwd(dx2, y, o, p, modrows, sp, w_out):
    L = dx2.shape[0]
    T = _tile(L, 512)

    def body(dx2_ref, y_ref, o_ref, z_ref, mod_ref, sp_ref, w_ref, dy_ref, do_ref, dz_ref, dya_ref, accv_ref, accs_ref):
        @pl.when(pl.program_id(0) == 0)
        def _():
            accv_ref[...] = jnp.zeros_like(accv_ref)
            accs_ref[...] = jnp.zeros_like(accs_ref)

        dx2v = dx2_ref[...]
        accv_ref[0:1, :] += _sum0(dx2v * y_ref[...])
        dyb = (mod_ref[2:3, :] * dx2v).astype(MXU)
        dy_ref[...] = dyb
        dyc = _dot(dyb, w_ref[...], NT)
        dya_ref[...] = dyc[:, 0:AW]
        ndw = sp_ref[2:3, :]
        z = z_ref[...]
        sgz = _sig(z)
        dsz = sgz * (1.0 + z * (1.0 - sgz))
        dndw = jnp.zeros((1, HD), F32)
        for h in range(H):
            sl = slice(h * HD, (h + 1) * HD)
            n, r = _rms(o_ref[:, sl])
            dyh = dyc[:, AW + h * HD:AW + (h + 1) * HD]
            zh = z[:, sl]
            don = dyh * (zh * sgz[:, sl])
            dz_ref[:, sl] = dyh * (n * ndw) * dsz[:, sl]
            dndw = dndw + _sum0(don * n)
            do_ref[:, sl] = _rms_bwd(don * ndw, n, r)
        accs_ref[0:1, :] += dndw

    row = lambda i: (i, 0)
    zcol = (3 * AW + 3 * H * HD) // (H * HD)
    return pl.pallas_call(
        body, name="post_bwd", grid=(L // T,),
        in_specs=[pl.BlockSpec((T, D), row), pl.BlockSpec((T, D), row), pl.BlockSpec((T, H * HD), row),
                  pl.BlockSpec((T, H * HD), lambda i: (i, zcol)), _full((SUB, D)), _full((SUB, LANES)), _full((D, D))],
        out_specs=[pl.BlockSpec((T, D), row)] + [pl.BlockSpec((T, H * HD), row)] * 3 + [_full((SUB, D)), _full((SUB, LANES))],
        out_shape=[jax.ShapeDtypeStruct((L, D), MXU)] + [jax.ShapeDtypeStruct((L, H * HD), F32)] * 3
        + [jax.ShapeDtypeStruct((SUB, D), F32), jax.ShapeDtypeStruct((SUB, LANES), F32)],
        compiler_params=_params(1),
    )(dx2, y, o, p, modrows, sp, w_out)


def _pre_in_bwd(p, cub, qcb, dqn, dkn, dvs, dya, dz, dgb, pa, cq, sp, w_in, x, dx2, modrows, vec):
    L = p.shape[0]
    T = _tile(L, 256)
    ni = L // T
    scale = HD ** -0.5
    w3 = 3 * AW + 3 * H * HD

    def body(pm_ref, cu_ref, qc_ref, ps_ref, dq_ref, dk_ref, dv_ref, dya_ref, dz_ref, dgb_ref, pa_ref, cq_ref, sp_ref,
             w_ref, x_ref, dx2_ref, mod_ref, vec_ref,
             dp_ref, dx_ref, dpa_ref, dcq_ref, dsp_ref, accv_ref, carry_u, carry_q):
        i = pl.program_id(0)

        @pl.when(i == 0)
        def _():
            dpa_ref[...] = jnp.zeros_like(dpa_ref)
            dcq_ref[...] = jnp.zeros_like(dcq_ref)
            dsp_ref[...] = jnp.zeros_like(dsp_ref)
            accv_ref[...] = jnp.zeros_like(accv_ref)
            carry_u[...] = jnp.zeros_like(carry_u)
            carry_q[...] = jnp.zeros_like(carry_q)

        a_b, a_c, a_x = pm_ref[:, 0:AW], pm_ref[:, AW:2 * AW], pm_ref[:, 2 * AW:3 * AW]
        u = a_c * a_x
        cu = cu_ref[...].astype(F32)
        yp = a_b * cu
        bd = _blockdiag_mean(AW, A_GROUP)
        ra = lax.rsqrt(_dot_f32(yp * yp, bd, NN, exact="b") + EPS)
        na = yp * ra
        dya = dya_ref[...]
        dpa_ref[3:4, :] += _sum0(dya * na)
        dna = dya * pa_ref[3:4, :]
        dyp = ra * (dna - na * _dot_f32(dna * na, bd, NN, exact="b"))
        dcu = dyp * a_b
        dcs = [dcu] + [_shift_up(dcu, s, carry_u[...]) for s in (1, 2)]
        du = pa_ref[2:3, :] * dcs[0]
        for s in range(3):
            dpa_ref[2 - s:3 - s, :] += _sum0(dcs[s] * u)
            if s:
                du = du + pa_ref[2 - s:3 - s, :] * dcs[s]
        carry_u[...] = dcu[0:SUB, :]
        dp_a = jnp.concatenate([dyp * cu, du * a_x, du * a_c], axis=-1).astype(MXU)
        dp_ref[:, 0:3 * AW] = dp_a
        dh = _dot(dp_a, w_ref[:, 0:3 * AW], NT)

        qkv = pm_ref[:, 3 * AW:w3]
        qc = qc_ref[...].astype(F32)
        sg = _sig(qc)
        qs = qc * sg
        parts = []
        for h in range(H):
            q = qs[:, h * HD:(h + 1) * HD]
            rq = lax.rsqrt(jnp.sum(q * q, axis=-1, keepdims=True) + EPS)
            parts.append(_l2_bwd(dq_ref[:, h * HD:(h + 1) * HD] * scale, q * rq, rq))
        for h in range(H):
            k = qs[:, (H + h) * HD:(H + h + 1) * HD]
            rk = lax.rsqrt(jnp.sum(k * k, axis=-1, keepdims=True) + EPS)
            parts.append(_l2_bwd(dk_ref[:, h * HD:(h + 1) * HD], k * rk, rk))
        parts.append(dv_ref[...])
        dqc = jnp.concatenate(parts, axis=-1) * (sg * (1.0 + qc * (1.0 - sg)))
        dqs = [dqc] + [_shift_up(dqc, s, carry_q[...]) for s in (1, 2, 3)]
        dqkv = cq_ref[3:4, :] * dqs[0]
        for s in range(4):
            dcq_ref[3 - s:4 - s, :] += _sum0(dqs[s] * qkv)
            if s:
                dqkv = dqkv + cq_ref[3 - s:4 - s, :] * dqs[s]
        dp_q = dqkv.astype(MXU)
        dp_ref[:, 3 * AW:w3] = dp_q
        dh = dh + _dot(dp_q, w_ref[:, 3 * AW:w3], NT)
        carry_q[...] = dqc[0:SUB, :]

        lane, a, xb, beta, g = _gate_small(ps_ref[...], sp_ref)
        dgb = dgb_ref[...]
        dbeta = jnp.where(lane < H, dgb, 0.0)
        dg = jnp.where((lane >= H) & (lane < 2 * H), dgb, 0.0)
        dalpha = dg * a * _sig(xb)
        dsp_ref[0:1, :] += _sum0(dg * g)
        dsp_ref[1:2, :] += _sum0(dalpha)
        dp_z = jnp.concatenate([dz_ref[...], dbeta * beta * (1.0 - beta) + dalpha], axis=-1).astype(MXU)
        dp_ref[:, w3:P_PAD] = dp_z
        dh = dh + _dot(dp_z, w_ref[:, w3:P_PAD], NT)

        n, r = _rms(x_ref[...])
        nw, sc = vec_ref[0:1, :], mod_ref[1:2, :]
        accv_ref[0:1, :] += _sum0(dh)
        accv_ref[1:2, :] += _sum0(dh * n * nw)
        accv_ref[2:3, :] += _sum0(dh * n * (1.0 + sc))
        dx_ref[...] = _rms_bwd(dh * nw * (1.0 + sc), n, r) + dx2_ref[...]

    row = lambda i: (ni - 1 - i, 0)
    hrow = pl.BlockSpec((T, H * HD), row)
    rowD = pl.BlockSpec((T, D), row)
    return pl.pallas_call(
        body, name="pre_in_bwd", grid=(ni,),
        in_specs=[pl.BlockSpec((T, w3), row), pl.BlockSpec((T, AW), row), pl.BlockSpec((T, 3 * H * HD), row),
                  pl.BlockSpec((T, LANES), lambda i: (ni - 1 - i, (P_PAD - LANES) // LANES)),
                  hrow, hrow, hrow, pl.BlockSpec((T, AW), row), hrow,
                  pl.BlockSpec((T, LANES), row),
                  _full((SUB, AW)), _full((SUB, 3 * H * HD)), _full((SUB, LANES)),
                  _full((D, P_PAD)), rowD, rowD, _full((SUB, D)), _full((SUB, D))],
        out_specs=[pl.BlockSpec((T, P_PAD), row), rowD, _full((SUB, AW)), _full((SUB, 3 * H * HD)), _full((SUB, LANES)),
                   _full((SUB, D))],
        out_shape=[jax.ShapeDtypeStruct((L, P_PAD), MXU), jax.ShapeDtypeStruct((L, D), F32),
                   jax.ShapeDtypeStruct((SUB, AW), F32), jax.ShapeDtypeStruct((SUB, 3 * H * HD), F32),
                   jax.ShapeDtypeStruct((SUB, LANES), F32), jax.ShapeDtypeStruct((SUB, D), F32)],
        scratch_shapes=[pltpu.VMEM((SUB, AW), F32), pltpu.VMEM((SUB, 3 * H * HD), F32)],
        compiler_params=_params(1),
    )(p, cub, qcb, p, dqn, dkn, dvs, dya, dz, dgb, pa, cq, sp, w_in, x, dx2, modrows, vec)


def _wgrad(a, b, tm, tn, name):
    L, m = a.shape
    n = b.shape[1]
    tl = _tile(L, 512)
    tm, tn = _tile(m, tm), _tile(n, tn)
    nl = L // tl

    def body(a_ref, b_ref, o_ref, acc):
        @pl.when(pl.program_id(2) == 0)
        def _():
            acc[...] = jnp.zeros_like(acc)

        acc[...] += _dot(a_ref[...], b_ref[...], TN)

        @pl.when(pl.program_id(2) == nl - 1)
        def _():
            o_ref[...] = acc[...].astype(o_ref.dtype)

    return pl.pallas_call(
        body, name=name, grid=(m // tm, n // tn, nl),
        in_specs=[pl.BlockSpec((tl, tm), lambda i, j, l: (l, i)), pl.BlockSpec((tl, tn), lambda i, j, l: (l, j))],
        out_specs=pl.BlockSpec((tm, tn), lambda i, j, l: (i, j)),
        out_shape=jax.ShapeDtypeStruct((m, n), MXU), scratch_shapes=[pltpu.VMEM((tm, tn), F32)],
        compiler_params=_params(3),
    )(a, b)


def _wgrad_cols(a, b, tm, n_shard, wpad, count, name):
    L, m = a.shape
    n = b.shape[1]
    tl = _tile(L, 512)
    tm = _tile(m, tm)
    nl = L // tl
    wins = _shard_windows(n_shard, count)
    assert all(a_ * LANES + win <= n for a_, _, win in wins), (wins, n)

    def body(a_ref, b_ref, o_ref, acc):
        @pl.when(pl.program_id(1) == 0)
        def _():
            acc[...] = jnp.zeros_like(acc)

        acc[...] += _dot(a_ref[...], b_ref[...], TN)

        @pl.when(pl.program_id(1) == nl - 1)
        def _():
            for k, (a_, s, win) in enumerate(wins):
                xk = acc[:, a_ * LANES:a_ * LANES + win]
                if s:
                    xk = pltpu.roll(xk, win - s, 1)
                o_ref[k] = _fit_lanes(xk, wpad).astype(o_ref.dtype)

    return pl.pallas_call(
        body, name=name, grid=(m // tm, nl),
        in_specs=[pl.BlockSpec((tl, tm), lambda i, l: (l, i)), pl.BlockSpec((tl, n), lambda i, l: (l, 0))],
        out_specs=pl.BlockSpec((count, tm, wpad), lambda i, l: (0, i, 0)),
        out_shape=jax.ShapeDtypeStruct((count, m, wpad), MXU),
        scratch_shapes=[pltpu.VMEM((tm, n), F32)],
        compiler_params=_params(2),
    )(a, b)


def _adamw(w, g, m, v, name):
    r, n = w.shape
    tr = _tile(r, 512)
    bc1 = 1.0 - ADAM_B1 ** ADAM_STEP
    bc2 = 1.0 - ADAM_B2 ** ADAM_STEP

    def body(w_ref, g_ref, m_ref, v_ref, d_ref, nm_ref, nv_ref):
        gv = g_ref[...]
        nm = ADAM_B1 * m_ref[...] + (1.0 - ADAM_B1) * gv
        nv = ADAM_B2 * v_ref[...] + (1.0 - ADAM_B2) * (gv * gv)
        nm_ref[...] = nm
        nv_ref[...] = nv
        d_ref[...] = -ADAM_LR * ((nm / bc1) / (jnp.sqrt(nv / bc2) + ADAM_EPS) + ADAM_WD * w_ref[...])

    spec = pl.BlockSpec((tr, n), lambda i: (i, 0))
    return pl.pallas_call(
        body, name=name, grid=(r // tr,), in_specs=[spec] * 4, out_specs=[spec] * 3,
        out_shape=[jax.ShapeDtypeStruct((r, n), F32)] * 3, compiler_params=_params(1),
    )(w, g, m, v)


def _rows8(rows, width):
    out = jnp.zeros((SUB, width), F32)
    for r, vrow in enumerate(rows):
        out = out.at[r, :vrow.shape[0]].set(vrow)
    return out


def _at_lanes(v4, start):
    return jnp.zeros((LANES,), F32).at[start:start + v4.shape[0]].set(v4)


def _pad_rows(flat, mult):
    n = flat.shape[0]
    pad = (-n) % mult
    return jnp.pad(flat, (0, pad)) if pad else flat


IN_PAD = 512
UP_PAD = 768


def _local_fwd_bwd(x, target, mod_full, small_w, full_w, on_grads=None):
    norm1_w, norm2_w, norm_a_w, a_log, dt_bias, norm_dn_w, norm_f_w = small_w
    w_in_f, w_out_f, w_up_f, w_down_f, conv_a_f, conv_q_f, conv_f_f = full_w

    def layer_params(i):
        modrows = jnp.concatenate([mod_full[i], jnp.zeros((SUB - N_MOD, D), F32)], axis=0)
        vec = _rows8([norm1_w[i], norm2_w[i]], D)
        pa = _rows8([conv_a_f[i, 0], conv_a_f[i, 1], conv_a_f[i, 2], norm_a_w[i]], AW)
        cq = _rows8([conv_q_f[i, k] for k in range(4)], 3 * H * HD)
        sp = _rows8([_at_lanes(a_log[i], H), _at_lanes(dt_bias[i], H), norm_dn_w[i]], LANES)
        cff = _rows8([conv_f_f[i, k] for k in range(3)], 2 * DFF)
        return modrows, vec, pa, cq, sp, cff

    saved = []
    xi = x
    for i in range(DEPTH):
        modrows, vec, pa, cq, sp, cff = layer_params(i)
        p, h1, qn, kn, vs, gb, ya, cub, qcb = _in_pre_fwd(xi, modrows, vec, w_in_f[i], pa, cq, sp)
        o, states, tinvs = _gdr_fwd(qn, kn, vs, gb)
        y, x2, yb = _post_fwd(o, p, ya, xi, modrows, sp, w_out_f[i])
        h2, gp0, up0, gc0, uc0, f0, d0 = _ffn_fwd_half(x2, modrows, vec, w_up_f[i], cff, w_down_f[i], 0, None)
        gp1, up1, gc1, uc1, f1, dff, x3 = _ffn_fwd_half(x2, modrows, vec, w_up_f[i], cff, w_down_f[i], 1, d0)
        saved.append(dict(x=xi, p=p, h1=h1, qn=qn, kn=kn, vs=vs, gb=gb, ya=ya, cub=cub, qcb=qcb, o=o, states=states,
                          tinvs=tinvs, y=y, x2=x2, yb=yb,
                          h2=h2, gpre=(gp0, gp1), upre=(up0, up1), gc=(gc0, gc1), uc=(uc0, uc1), f=(f0, f1), d=dff))
        xi = x3

    dx, facc = _final(xi, target, _rows8([norm_f_w], D))
    loss_local = jnp.sum(facc[0])
    d_norm_f = facc[1]

    gw_in, gw_out, gw_up, gw_down = [None] * DEPTH, [None] * DEPTH, [None] * DEPTH, [None] * DEPTH
    g_small = [None] * DEPTH
    for i in reversed(range(DEPTH)):
        s = saved[i]
        modrows, vec, pa, cq, sp, cff = layer_params(i)
        dd, dgp0, dup0, dh0, dcg0, dcu0 = _ffn_bwd_half(dx, modrows, s["gpre"][0], s["upre"][0], s["gc"][0], s["uc"][0],
                                                        cff, w_down_f[i], w_up_f[i], 0, None)
        dgp1, dup1, dx2, accf, dcg1, dcu1 = _ffn_bwd_half(dx, modrows, s["gpre"][1], s["upre"][1], s["gc"][1], s["uc"][1],
                                                          cff, w_down_f[i], w_up_f[i], 1, (s["d"], s["x2"], vec, dh0))
        n_up, up_pad = 2 * DFF // N_DEV, UP_PAD
        gw_up[i] = jnp.concatenate([_wgrad_cols(s["h2"], t, 1024, n_up, up_pad, FF_CW // n_up, "wgrad_up")
                                    for t in (dgp0, dgp1, dup0, dup1)], axis=0)
        gw_down[i] = jnp.concatenate([_wgrad(s["f"][0], dd, FF_CW, 1024, "wgrad_down"),
                                      _wgrad(s["f"][1], dd, FF_CW, 1024, "wgrad_down")],
                                     axis=0).reshape(N_DEV, DFF // N_DEV, D)
        if on_grads is not None:
            on_grads(i, "ffn", [gw_up[i], gw_down[i]])
        dy, do, dz, dya, accp, accs = _post_bwd(dx2, s["y"], s["o"], s["p"], modrows, sp, w_out_f[i])
        gw_out[i] = jnp.concatenate([_wgrad(s["ya"], dy, 512, 1024, "wgrad_out"),
                                     _wgrad(s["yb"], dy, 512, 1024, "wgrad_out")], axis=0).reshape(N_DEV, D // N_DEV, D)
        dqn, dkn, dvs, dgb = _gdr_bwd(s["qn"], s["kn"], s["vs"], s["gb"], s["states"], s["tinvs"], do)
        dp, dx, dpa, dcq, dsp, acci = _pre_in_bwd(s["p"], s["cub"], s["qcb"], dqn, dkn, dvs, dya, dz, dgb, pa, cq, sp,
                                                  w_in_f[i], s["x"], dx2, modrows, vec)
        gw_in[i] = _wgrad_cols(s["h1"], dp, 1024, P_IN // N_DEV, IN_PAD, N_DEV, "wgrad_in")
        dconv_ff = jnp.concatenate([dcg0, dcg1, dcu0, dcu1], axis=1)[0:3]
        dmod = jnp.stack([acci[0], acci[1], accp[0], accf[1], accf[2], accf[0]])
        g_small[i] = dict(norm1=acci[2], norm2=accf[3], norm_a=dpa[3], a_log=dsp[0, H:2 * H], dt_bias=dsp[1, H:2 * H],
                          norm_dn=accs[0], conv_a=dpa[0:3], conv_qkv=dcq[0:4], conv_ff=dconv_ff, dmod=dmod.reshape(-1))
        if on_grads is not None:
            on_grads(i, "mix", [gw_in[i], gw_out[i]])
    return loss_local, dx, gw_in, gw_out, gw_up, gw_down, g_small, d_norm_f


def kernel(x, c, ada_w, ada_b, norm1_w, w_in, conv_a_w, norm_a_w, conv_qkv_w, a_log, dt_bias, norm_dn_w, w_out, norm2_w, w_up, conv_ff_w, w_down, norm_f_w, loss_target, m_ada_w, m_ada_b, m_norm1_w, m_w_in, m_conv_a_w, m_norm_a_w, m_conv_qkv_w, m_a_log, m_dt_bias, m_norm_dn_w, m_w_out, m_norm2_w, m_w_up, m_conv_ff_w, m_w_down, m_norm_f_w, v_ada_w, v_ada_b, v_norm1_w, v_w_in, v_conv_a_w, v_norm_a_w, v_conv_qkv_w, v_a_log, v_dt_bias, v_norm_dn_w, v_w_out, v_norm2_w, v_w_up, v_conv_ff_w, v_w_down, v_norm_f_w):
    ax, ay, ac = lax.axis_index("x"), lax.axis_index("y"), lax.axis_index("c")
    me = 4 * ax + 2 * ay + ac
    x = x[0]
    target = loss_target[0]
    n_in, n_up = P_IN // N_DEV, 2 * DFF // N_DEV

    def lane_pad(t, width):
        return jnp.pad(t.astype(MXU), ((0, 0), (0, 0), (0, width - t.shape[-1])))

    conv_blob = _pad_rows(jnp.concatenate([t.reshape(-1) for t in (conv_a_w, conv_qkv_w, conv_ff_w)]),
                          SUB * LANES).reshape(-1, LANES)
    c_rows = jnp.zeros((SUB, D), F32).at[0].set(c[0])
    send = [lane_pad(w_in, IN_PAD), w_out.astype(MXU), lane_pad(w_up, UP_PAD), w_down.astype(MXU)]
    got = [None] * DEPTH
    g_in0, g_conv, g_c = _all_gather([send[0][0], conv_blob, c_rows], "gather_weights", in_vmem=False)
    shards, _ = lax.optimization_barrier(([t[0] for t in send[1:]], g_c))
    got[0] = [g_in0] + _all_gather_async(shards, "gather_weights_l0", collective_id=0)
    for i in range(1, DEPTH):
        shards, _ = lax.optimization_barrier(([t[i] for t in send], g_c))
        got[i] = _all_gather_async(shards, "gather_weights_l%d" % i, collective_id=i)
    w_in_f = [_interleave_cols(g[0][:, None], n_in, P_PAD, "interleave_w_in")[0] for g in got]
    w_up_f = [_interleave_cols(g[2][:, None], n_up, 2 * DFF, "interleave_w_up")[0] for g in got]
    w_out_f = [g[1].reshape(D, D) for g in got]
    w_down_f = [g[3].reshape(DFF, D) for g in got]
    sg = g_conv.reshape(N_DEV, -1)
    o1 = conv_a_w.size
    o2 = o1 + conv_qkv_w.size
    o3 = o2 + conv_ff_w.size
    conv_a_f = sg[:, 0:o1].reshape(N_DEV, DEPTH, 3, AW // N_DEV).transpose(1, 2, 0, 3).reshape(DEPTH, 3, AW)
    conv_q_f = sg[:, o1:o2].reshape(N_DEV, DEPTH, 4, 3 * H * HD // N_DEV).transpose(1, 2, 0, 3).reshape(DEPTH, 4, 3 * H * HD)
    conv_f_f = sg[:, o2:o3].reshape(N_DEV, DEPTH, 3, n_up).transpose(1, 2, 0, 3).reshape(DEPTH, 3, 2 * DFF)

    c_all = jnp.concatenate([g_c[:, 0], jnp.zeros((16 - N_DEV, D), F32)], axis=0)
    n_ada = N_MOD * D // N_DEV
    ada_b_cols = lax.dynamic_slice_in_dim(ada_b, me * n_ada, n_ada, axis=1)[:, None, :]
    mod_sh = _mod_fwd(c_all, ada_w, ada_b_cols)
    mod_all = _all_gather([mod_sh.reshape(DEPTH * 16, n_ada)], "gather_mod", in_vmem=True)[0]
    mod_all = mod_all.reshape(N_DEV, DEPTH, 16, n_ada)
    mod_mine = lax.dynamic_index_in_dim(mod_all, me, axis=2, keepdims=False)
    mod_full = mod_mine.transpose(1, 0, 2).reshape(DEPTH, N_MOD, D)

    tags = ["w_in", "w_out", "w_up", "w_down"]
    received = [dict() for _ in range(DEPTH)]

    def on_grads(i, part, gs_i):
        first_id = DEPTH if part == "ffn" else 2 * DEPTH
        got_i = _rs_exchange_async(gs_i, "rs_exchange_%s_l%d" % (part, i), collective_id=first_id + i)
        received[i].update(zip(("w_up", "w_down") if part == "ffn" else ("w_in", "w_out"), got_i))

    loss_local, dx, _, _, _, _, g_small, d_norm_f = _local_fwd_bwd(
        x, target, mod_full, (norm1_w, norm2_w, norm_a_w, a_log, dt_bias, norm_dn_w, norm_f_w),
        (w_in_f, w_out_f, w_up_f, w_down_f, conv_a_f, conv_q_f, conv_f_f), on_grads)
    loss = lax.psum(loss_local, ("x", "y", "c"))
    grad_x = dx[None]

    keys = ["dmod", "norm1", "norm2", "norm_a", "a_log", "dt_bias", "norm_dn", "conv_a", "conv_qkv", "conv_ff"]
    stacked = {k: jnp.stack([g_small[i][k] for i in range(DEPTH)]) for k in keys}
    flat_parts = [stacked[k].reshape(-1) for k in keys] + [d_norm_f]
    sizes = [int(t.shape[0]) for t in flat_parts]
    sflat = _pad_rows(jnp.concatenate(flat_parts), SUB * LANES).reshape(-1, LANES)
    sall = _all_gather([sflat], "gather_small_grads", in_vmem=True)[0]
    ssum = _sum_devices(sall).reshape(-1)
    so = [0]
    for sz in sizes:
        so.append(so[-1] + sz)
    red = {k: ssum[so[n]:so[n + 1]].reshape(stacked[k].shape) for n, k in enumerate(keys)}
    g_norm_f = ssum[so[len(keys)]:so[len(keys) + 1]]
    dmod_all = sall[:, 0:sizes[0] // LANES, :].reshape(N_DEV, DEPTH, N_MOD * D)

    g_ada_b = red["dmod"].reshape(DEPTH, N_MOD * D)
    dmod_cols = lax.dynamic_slice_in_dim(dmod_all, me * n_ada, n_ada, axis=2).transpose(1, 0, 2)
    dmod_cols = jnp.concatenate([dmod_cols, jnp.zeros((DEPTH, 16 - N_DEV, n_ada), F32)], axis=1)
    g_ada_w = _mod_bwd(c_all, dmod_cols)
    g_conv_a = lax.dynamic_slice_in_dim(red["conv_a"], me * (AW // N_DEV), AW // N_DEV, axis=2)
    g_conv_qkv = lax.dynamic_slice_in_dim(red["conv_qkv"], me * (3 * H * HD // N_DEV), 3 * H * HD // N_DEV, axis=2)
    g_conv_ff = lax.dynamic_slice_in_dim(red["conv_ff"], me * n_up, n_up, axis=2)

    mine = [jnp.stack([_rs_sum(received[i][t], "rs_sum_" + t) for i in range(DEPTH)]) for t in tags]
    g_w_in = mine[0][:, :, :n_in]
    g_w_out = mine[1]
    g_w_up = mine[2][:, :, :n_up]
    g_w_down = mine[3]

    grads = dict(ada_w=g_ada_w, ada_b=g_ada_b, norm1_w=red["norm1"], w_in=g_w_in, conv_a_w=g_conv_a,
                 norm_a_w=red["norm_a"], conv_qkv_w=g_conv_qkv, a_log=red["a_log"], dt_bias=red["dt_bias"],
                 norm_dn_w=red["norm_dn"], w_out=g_w_out, norm2_w=red["norm2"], w_up=g_w_up, conv_ff_w=g_conv_ff,
                 w_down=g_w_down, norm_f_w=g_norm_f)
    weights = dict(ada_w=ada_w, ada_b=ada_b, norm1_w=norm1_w, w_in=w_in, conv_a_w=conv_a_w, norm_a_w=norm_a_w,
                   conv_qkv_w=conv_qkv_w, a_log=a_log, dt_bias=dt_bias, norm_dn_w=norm_dn_w, w_out=w_out,
                   norm2_w=norm2_w, w_up=w_up, conv_ff_w=conv_ff_w, w_down=w_down, norm_f_w=norm_f_w)
    ms = dict(ada_w=m_ada_w, ada_b=m_ada_b, norm1_w=m_norm1_w, w_in=m_w_in, conv_a_w=m_conv_a_w, norm_a_w=m_norm_a_w,
              conv_qkv_w=m_conv_qkv_w, a_log=m_a_log, dt_bias=m_dt_bias, norm_dn_w=m_norm_dn_w, w_out=m_w_out,
              norm2_w=m_norm2_w, w_up=m_w_up, conv_ff_w=m_conv_ff_w, w_down=m_w_down, norm_f_w=m_norm_f_w)
    vs_ = dict(ada_w=v_ada_w, ada_b=v_ada_b, norm1_w=v_norm1_w, w_in=v_w_in, conv_a_w=v_conv_a_w, norm_a_w=v_norm_a_w,
               conv_qkv_w=v_conv_qkv_w, a_log=v_a_log, dt_bias=v_dt_bias, norm_dn_w=v_norm_dn_w, w_out=v_w_out,
               norm2_w=v_norm2_w, w_up=v_w_up, conv_ff_w=v_conv_ff_w, w_down=v_w_down, norm_f_w=v_norm_f_w)
    names = list(weights)
    big_names = ["ada_w", "w_in", "w_out", "w_up", "w_down"]
    delta, new_m, new_v = {}, {}, {}
    for n in big_names:
        shp = weights[n].shape
        two = lambda t: t.reshape(-1, shp[-1])
        dl, nm, nv = _adamw(two(weights[n]), two(grads[n]), two(ms[n]), two(vs_[n]), "adamw_" + n)
        delta[n], new_m[n], new_v[n] = dl.reshape(shp), nm.reshape(shp), nv.reshape(shp)
    small_names = [n for n in names if n not in big_names]

    def pack(dct):
        return _pad_rows(jnp.concatenate([dct[n].reshape(-1) for n in small_names]), SUB * LANES).reshape(-1, LANES)

    dl, nm, nv = _adamw(pack(weights), pack(grads), pack(ms), pack(vs_), "adamw_small")
    off = 0
    for n in small_names:
        sz, shp = weights[n].size, weights[n].shape
        delta[n] = dl.reshape(-1)[off:off + sz].reshape(shp)
        new_m[n] = nm.reshape(-1)[off:off + sz].reshape(shp)
        new_v[n] = nv.reshape(-1)[off:off + sz].reshape(shp)
        off += sz

    return (loss, grad_x, *[grads[n] for n in names], *[delta[n] for n in names],
            *[new_m[n] for n in names], *[new_v[n] for n in names])
```

```python
import functools
import math

import jax
import jax.numpy as jnp
from jax import lax
from jax.experimental import pallas as pl
from jax.experimental.pallas import tpu as pltpu
from jax.experimental.pallas import tpu_sc as plsc

F32 = jnp.float32
MXU = jnp.bfloat16

D = 1024
DEPTH = 4
N_MOD = 6
AW = 512
A_GROUP = 64
H = 4
HD = 128
CK = 64
DFF = 2816
P_IN = 3592
P_PAD = 3712
EPS = 1e-6
N_DEV = 8
LANES = 128
SUB = 8
VMEM_LIMIT = 56 * 1024 * 1024

ADAM_LR, ADAM_B1, ADAM_B2, ADAM_EPS, ADAM_WD, ADAM_STEP = 0.001, 0.9, 0.999, 1e-08, 0.01, 10

NN = ((1,), (0,))
NT = ((1,), (1,))
TN = ((0,), (0,))
HI = lax.Precision.HIGHEST
MESH = pl.DeviceIdType.MESH


def _dot(a, b, dims, prec=None):
    if prec is None:
        a = a.astype(MXU) if a.dtype == F32 else a
        b = b.astype(MXU) if b.dtype == F32 else b
    return lax.dot_general(a, b, (dims, ((), ())), precision=prec, preferred_element_type=F32)


def _params(n_grid=0, limit=VMEM_LIMIT):
    sem = ("arbitrary",) * n_grid if n_grid else None
    return pltpu.CompilerParams(dimension_semantics=sem, vmem_limit_bytes=limit)


def _tile(n, want):
    if n <= want:
        return n
    t = want - want % SUB
    while n % t:
        t -= SUB
    assert t > 0, (n, want)
    return t


def _full(shape):
    nd = len(shape)
    return pl.BlockSpec(shape, lambda *_: (0,) * nd)


def _sig(x):
    return jax.nn.sigmoid(x)


def _rms(x):
    r = lax.rsqrt(jnp.mean(x * x, axis=-1, keepdims=True) + EPS)
    return x * r, r


def _rms_bwd(dn, n, r):
    return r * (dn - n * jnp.mean(dn * n, axis=-1, keepdims=True))


def _l2_bwd(dn, n, r):
    return r * (dn - n * jnp.sum(dn * n, axis=-1, keepdims=True))


def _sum0(x):
    return jnp.sum(x, axis=0, keepdims=True)


def _shift_down(x, s, halo):
    ext = jnp.concatenate([halo, x], axis=0)
    return pltpu.roll(ext, s, 0)[SUB:, :]


def _shift_up(x, s, halo):
    t = x.shape[0]
    ext = jnp.concatenate([x, halo], axis=0)
    return pltpu.roll(ext, t + SUB - s, 0)[:t, :]


def _conv_fwd(x, w_ref, width, halo):
    sh = [x] + [_shift_down(x, s, halo) for s in range(1, width)]
    out = w_ref[width - 1:width, :] * sh[0]
    for s in range(1, width):
        out = out + w_ref[width - 1 - s:width - s, :] * sh[s]
    return out, sh


def _blockdiag_mean(n, group):
    r = lax.shift_right_logical(lax.broadcasted_iota(jnp.int32, (n, n), 0), int(math.log2(group)))
    c = lax.shift_right_logical(lax.broadcasted_iota(jnp.int32, (n, n), 1), int(math.log2(group)))
    return jnp.where(r == c, 1.0 / group, 0.0).astype(F32)


def _softplus(x):
    return jnp.maximum(x, 0.0) + jnp.log(1.0 + jnp.exp(-jnp.abs(x)))


def _my_place():
    return lax.axis_index("x"), lax.axis_index("y"), lax.axis_index("c")


def _all_gather(shards, name, in_vmem):
    nt = len(shards)

    def body(*refs):
        x_refs, out_refs = refs[:nt], refs[nt:2 * nt]
        send_sems, recv_sems, local_sems = refs[2 * nt:]
        x, y, c = _my_place()
        me, sibling = (x, y, c), (x, y, 1 - c)
        chips = [(1 - x, y), (x, 1 - y), (1 - x, 1 - y)]
        everything = []
        for t in range(nt):
            x_ref, out_ref = x_refs[t], out_refs[t]

            def blk(px, py, pc, out_ref=out_ref):
                return out_ref.at[4 * px + 2 * py + pc]

            def copy(k, block, to, src=None, t=t, blk=blk):
                return pltpu.make_async_remote_copy(
                    src_ref=blk(*block) if src is None else src, dst_ref=blk(*block),
                    send_sem=send_sems.at[7 * t + k], recv_sem=recv_sems.at[7 * t + k], device_id=to, device_id_type=MESH)

            mine = pltpu.make_async_copy(x_ref, blk(*me), local_sems.at[t])
            mine.start()
            first = [copy(0, me, sibling, src=x_ref)]
            first += [copy(1 + j, me, (*chip, c), src=x_ref) for j, chip in enumerate(chips)]
            for cp in first:
                cp.start()
            everything.append((copy, mine, first))
        sends = []
        for copy, mine, first in everything:
            passed = [copy(4 + j, (*chip, c), sibling) for j, chip in enumerate(chips)]
            for j, chip in enumerate(chips):
                copy(1 + j, (*chip, c), me).wait_recv()
                passed[j].start()
            sends += first + passed
        for copy, mine, first in everything:
            copy(0, sibling, me).wait_recv()
            for j, chip in enumerate(chips):
                copy(4 + j, (*chip, 1 - c), me).wait_recv()
        for cp in sends:
            cp.wait_send()
        for copy, mine, first in everything:
            mine.wait()

    space = pltpu.VMEM if in_vmem else pl.ANY
    return pl.pallas_call(
        body, name=name,
        out_shape=[jax.ShapeDtypeStruct((N_DEV,) + s.shape, s.dtype) for s in shards],
        in_specs=[pl.BlockSpec(memory_space=space)] * nt,
        out_specs=[pl.BlockSpec(memory_space=space)] * nt,
        scratch_shapes=[pltpu.SemaphoreType.DMA((7 * nt,)), pltpu.SemaphoreType.DMA((7 * nt,)),
                        pltpu.SemaphoreType.DMA((nt,))],
        compiler_params=pltpu.CompilerParams(vmem_limit_bytes=VMEM_LIMIT),
    )(*shards)


def _all_gather_async(shards, name, collective_id):
    nt = len(shards)
    hbm = pltpu.MemorySpace.HBM
    x_refs = [jax.new_ref(s, memory_space=hbm) for s in shards]
    out_refs = [jax.empty_ref(jax.ShapeDtypeStruct((N_DEV,) + s.shape, s.dtype), memory_space=hbm) for s in shards]

    @pl.kernel(mesh=plsc.ScalarSubcoreMesh(axis_name="sequencer", num_cores=1), name=name,
               scratch_types=(pltpu.SemaphoreType.DMA((7 * nt,)), pltpu.SemaphoreType.DMA((7 * nt,)),
                              pltpu.SemaphoreType.DMA((nt,))),
               compiler_params=pltpu.CompilerParams(collective_id=collective_id))
    def launch(send_sems, recv_sems, local_sems):
        x, y, c = _my_place()
        me, sibling = (x, y, c), (x, y, 1 - c)
        chips = [(1 - x, y), (x, 1 - y), (1 - x, 1 - y)]
        barrier = pltpu.get_barrier_semaphore()
        for peer in [sibling] + [(*chip, c) for chip in chips]:
            pl.semaphore_signal(barrier, inc=1, device_id=peer, device_id_type=MESH)
        pl.semaphore_wait(barrier, 4)
        everything = []
        for t in range(nt):
            x_ref, out_ref = x_refs[t], out_refs[t]

            def blk(px, py, pc, out_ref=out_ref):
                return out_ref.at[4 * px + 2 * py + pc]

            def copy(k, block, to, src=None, t=t, blk=blk):
                return pltpu.make_async_remote_copy(
                    src_ref=blk(*block) if src is None else src, dst_ref=blk(*block),
                    send_sem=send_sems.at[7 * t + k], recv_sem=recv_sems.at[7 * t + k], device_id=to, device_id_type=MESH)

            mine = pltpu.make_async_copy(x_ref, blk(*me), local_sems.at[t])
            mine.start()
            first = [copy(0, me, sibling, src=x_ref)]
            first += [copy(1 + j, me, (*chip, c), src=x_ref) for j, chip in enumerate(chips)]
            for cp in first:
                cp.start()
            everything.append((copy, mine, first))
        sends = []
        for copy, mine, first in everything:
            passed = [copy(4 + j, (*chip, c), sibling) for j, chip in enumerate(chips)]
            for j, chip in enumerate(chips):
                copy(1 + j, (*chip, c), me).wait_recv()
                passed[j].start()
            sends += first + passed
        for copy, mine, first in everything:
            copy(0, sibling, me).wait_recv()
            for j, chip in enumerate(chips):
                copy(4 + j, (*chip, 1 - c), me).wait_recv()
        for cp in sends:
            cp.wait_send()
        for copy, mine, first in everything:
            mine.wait()

    launch()
    return [r[...] for r in out_refs]


def _rs_exchange_async(srcs, name, collective_id):
    nt = len(srcs)
    hbm = pltpu.MemorySpace.HBM
    src_refs = [jax.new_ref(s, memory_space=hbm) for s in srcs]
    out_refs = [jax.empty_ref(jax.ShapeDtypeStruct(s.shape, s.dtype), memory_space=hbm) for s in srcs]
    flips = [(fx, fy, fc) for fx in (0, 1) for fy in (0, 1) for fc in (0, 1)][1:]

    @pl.kernel(mesh=plsc.ScalarSubcoreMesh(axis_name="sequencer", num_cores=1), name=name,
               scratch_types=(pltpu.SemaphoreType.DMA((7 * nt,)), pltpu.SemaphoreType.DMA((7 * nt,)),
                              pltpu.SemaphoreType.DMA((nt,))),
               compiler_params=pltpu.CompilerParams(collective_id=collective_id))
    def launch(send_sems, recv_sems, local_sems):
        x, y, c = _my_place()
        me = 4 * x + 2 * y + c
        peers = [(1 - x if fx else x, 1 - y if fy else y, 1 - c if fc else c) for fx, fy, fc in flips]
        barrier = pltpu.get_barrier_semaphore()
        for peer in peers:
            pl.semaphore_signal(barrier, inc=1, device_id=peer, device_id_type=MESH)
        pl.semaphore_wait(barrier, len(peers))
        own = [pltpu.make_async_copy(src_refs[t].at[me], out_refs[t].at[me], local_sems.at[t]) for t in range(nt)]
        copies = [pltpu.make_async_remote_copy(
            src_ref=src_refs[t].at[4 * px + 2 * py + pc], dst_ref=out_refs[t].at[me],
            send_sem=send_sems.at[7 * t + f], recv_sem=recv_sems.at[7 * t + f],
            device_id=(px, py, pc), device_id_type=MESH) for t in range(nt) for f, (px, py, pc) in enumerate(peers)]
        for cp in own + copies:
            cp.start()
        for cp in copies + own:
            cp.wait()

    launch()
    return [r[...] for r in out_refs]


def _rs_sum(recv, name):
    _, r, n = recv.shape
    tr = _tile(r, 512)

    def body(r_ref, o_ref):
        s = r_ref[0].astype(F32)
        for k in range(1, N_DEV):
            s = s + r_ref[k].astype(F32)
        o_ref[...] = s

    return pl.pallas_call(
        body, name=name, grid=(r // tr,),
        in_specs=[pl.BlockSpec((N_DEV, tr, n), lambda i: (0, i, 0))],
        out_specs=pl.BlockSpec((tr, n), lambda i: (i, 0)),
        out_shape=jax.ShapeDtypeStruct((r, n), F32), compiler_params=_params(1),
    )(recv)


def _shard_windows(n_shard, count, first=0):
    out = []
    for k in range(first, first + count):
        off = n_shard * k
        a, s = off // LANES, off % LANES
        out.append((a, s, -(-(s + n_shard) // LANES) * LANES))
    return out


def _fit_lanes(x, width):
    have = x.shape[1]
    if have < width:
        return jnp.concatenate([x, jnp.zeros((x.shape[0], width - have), x.dtype)], axis=-1)
    return x[:, :width]


def _interleave_cols(g, n_shard, w_out, name):
    nd, nl, rows, wpad = g.shape
    rb = _tile(rows, 256)
    wins = _shard_windows(n_shard, nd)

    def body(g_ref, o_ref, acc):
        acc[...] = jnp.zeros_like(acc)
        for k, (a, s, win) in enumerate(wins):
            xk = _fit_lanes(g_ref[k].astype(F32), win)
            if s:
                xk = pltpu.roll(xk, s, 1)
            acc[:, a * LANES:a * LANES + win] += xk
        o_ref[...] = acc[...].astype(o_ref.dtype)

    return pl.pallas_call(
        body, name=name, grid=(nl, rows // rb),
        in_specs=[pl.BlockSpec((nd, None, rb, wpad), lambda l, i: (0, l, i, 0))],
        out_specs=pl.BlockSpec((None, rb, w_out), lambda l, i: (l, i, 0)),
        out_shape=jax.ShapeDtypeStruct((nl, rows, w_out), g.dtype),
        scratch_shapes=[pltpu.VMEM((rb, w_out), F32)],
        compiler_params=_params(2),
    )(g)


def _sum_devices(g):
    _, r, n = g.shape

    def body(g_ref, o_ref):
        s = g_ref[0]
        for t in range(1, N_DEV):
            s = s + g_ref[t]
        o_ref[...] = s

    return pl.pallas_call(
        body, name="sum_devices", out_shape=jax.ShapeDtypeStruct((r, n), F32),
        in_specs=[pl.BlockSpec(memory_space=pltpu.VMEM)], out_specs=pl.BlockSpec(memory_space=pltpu.VMEM),
        compiler_params=pltpu.CompilerParams(vmem_limit_bytes=VMEM_LIMIT),
    )(g)


def _mod_fwd(c_all, ada_w, ada_b_cols):
    nl, _, nc = ada_w.shape

    def body(c_ref, w_ref, b_ref, o_ref):
        cv = c_ref[...]
        act = (cv * _sig(cv)).astype(MXU)
        o_ref[...] = _dot(act, w_ref[...].astype(MXU), NN) + b_ref[...]

    return pl.pallas_call(
        body, name="mod_fwd", grid=(nl,),
        in_specs=[_full((16, D)), pl.BlockSpec((None, D, nc), lambda i: (i, 0, 0)),
                  pl.BlockSpec((None, 1, nc), lambda i: (i, 0, 0))],
        out_specs=pl.BlockSpec((None, 16, nc), lambda i: (i, 0, 0)),
        out_shape=jax.ShapeDtypeStruct((nl, 16, nc), F32), compiler_params=_params(1),
    )(c_all, ada_w, ada_b_cols)


def _mod_bwd(c_all, dmod_cols):
    nl, _, nc = dmod_cols.shape

    def body(c_ref, d_ref, o_ref):
        cv = c_ref[...]
        act = (cv * _sig(cv)).astype(MXU)
        o_ref[...] = _dot(act, d_ref[...].astype(MXU), TN)

    return pl.pallas_call(
        body, name="mod_bwd", grid=(nl,),
        in_specs=[_full((16, D)), pl.BlockSpec((None, 16, nc), lambda i: (i, 0, 0))],
        out_specs=pl.BlockSpec((None, D, nc), lambda i: (i, 0, 0)),
        out_shape=jax.ShapeDtypeStruct((nl, D, nc), F32), compiler_params=_params(1),
    )(c_all, dmod_cols)


def _gate_small(s, sp_ref):
    lane = lax.broadcasted_iota(jnp.int32, s.shape, 1)
    a = -jnp.exp(sp_ref[0:1, :])
    xb = s + sp_ref[1:2, :]
    beta = _sig(s)
    g = a * _softplus(xb)
    return lane, a, xb, beta, g


def _in_pre_fwd(x, modrows, vec, w_in, pa, cq, sp):
    L = x.shape[0]
    T = _tile(L, 512)
    scale = HD ** -0.5
    w3 = 3 * AW + 3 * H * HD

    def body(x_ref, mod_ref, vec_ref, w_ref, pa_ref, cq_ref, sp_ref,
             p_ref, h_ref, qn_ref, kn_ref, vs_ref, gb_ref, ya_ref, cu_ref, qc_ref, u_carry, q_carry):
        @pl.when(pl.program_id(0) == 0)
        def _():
            u_carry[...] = jnp.zeros_like(u_carry)
            q_carry[...] = jnp.zeros_like(q_carry)

        n, _ = _rms(x_ref[...])
        hb = (n * vec_ref[0:1, :] * (1.0 + mod_ref[1:2, :]) + mod_ref[0:1, :]).astype(MXU)
        h_ref[...] = hb
        pm_a = _dot(hb, w_ref[:, 0:3 * AW], NN)
        p_ref[:, 0:3 * AW] = pm_a
        pm_q = _dot(hb, w_ref[:, 3 * AW:w3], NN)
        p_ref[:, 3 * AW:w3] = pm_q

        a_b = pm_a[:, 0:AW]
        u = pm_a[:, AW:2 * AW] * pm_a[:, 2 * AW:3 * AW]
        cu, _ = _conv_fwd(u, pa_ref, 3, u_carry[...])
        cu_ref[...] = cu.astype(MXU)
        u_carry[...] = u[T - SUB:T, :]
        yp = a_b * cu
        ms = _dot_f32(yp * yp, _blockdiag_mean(AW, A_GROUP), NN, exact="b")
        ya_ref[...] = (yp * lax.rsqrt(ms + EPS) * pa_ref[3:4, :]).astype(MXU)

        pm_z = _dot(hb, w_ref[:, w3:P_PAD], NN)
        p_ref[:, w3:P_PAD] = pm_z
        qkv = pm_q
        qc, _ = _conv_fwd(qkv, cq_ref, 4, q_carry[...])
        qc_ref[...] = qc.astype(MXU)
        q_carry[...] = qkv[T - SUB:T, :]
        qs = qc * _sig(qc)
        for h in range(H):
            q = qs[:, h * HD:(h + 1) * HD]
            qn_ref[:, h * HD:(h + 1) * HD] = q * (lax.rsqrt(jnp.sum(q * q, axis=-1, keepdims=True) + EPS) * scale)
            k = qs[:, (H + h) * HD:(H + h + 1) * HD]
            kn_ref[:, h * HD:(h + 1) * HD] = k * lax.rsqrt(jnp.sum(k * k, axis=-1, keepdims=True) + EPS)
        vs_ref[...] = qs[:, 2 * H * HD:3 * H * HD]

        lane, _, _, beta, g = _gate_small(pm_z[:, H * HD:H * HD + LANES], sp_ref)
        gb_ref[...] = jnp.where(lane < H, beta, jnp.where(lane < 2 * H, g, 0.0))

    row = lambda i: (i, 0)
    return pl.pallas_call(
        body, name="in_pre_fwd", grid=(L // T,),
        in_specs=[pl.BlockSpec((T, D), row), _full((SUB, D)), _full((SUB, D)), _full((D, P_PAD)),
                  _full((SUB, AW)), _full((SUB, 3 * H * HD)), _full((SUB, LANES))],
        out_specs=[pl.BlockSpec((T, P_PAD), row), pl.BlockSpec((T, D), row)]
        + [pl.BlockSpec((T, H * HD), row)] * 3 + [pl.BlockSpec((T, LANES), row), pl.BlockSpec((T, AW), row),
                                                  pl.BlockSpec((T, AW), row), pl.BlockSpec((T, 3 * H * HD), row)],
        out_shape=[jax.ShapeDtypeStruct((L, P_PAD), F32), jax.ShapeDtypeStruct((L, D), MXU)]
        + [jax.ShapeDtypeStruct((L, H * HD), F32)] * 3
        + [jax.ShapeDtypeStruct((L, LANES), F32), jax.ShapeDtypeStruct((L, AW), MXU),
           jax.ShapeDtypeStruct((L, AW), MXU), jax.ShapeDtypeStruct((L, 3 * H * HD), MXU)],
        scratch_shapes=[pltpu.VMEM((SUB, AW), F32), pltpu.VMEM((SUB, 3 * H * HD), F32)],
        compiler_params=_params(1),
    )(x, modrows, vec, w_in, pa, cq, sp)


def _gdr_masks():
    r = lax.broadcasted_iota(jnp.int32, (CK, CK), 0)
    c = lax.broadcasted_iota(jnp.int32, (CK, CK), 1)
    return r >= c, r > c


def _head_cols(gbt, h):
    return gbt[:, h:h + 1], gbt[:, H + h:H + h + 1]


def _split(x, parts):
    out = []
    for _ in range(parts):
        hi = x.astype(jnp.bfloat16)
        out.append(hi)
        x = x - hi.astype(F32)
    return out


def _dot_f32(a, b, dims, exact=None):
    if exact == "a":
        ab = a.astype(jnp.bfloat16)
        return sum(_dot(ab, t, dims) for t in _split(b, 3))
    if exact == "b":
        bb = b.astype(jnp.bfloat16)
        return sum(_dot(t, bb, dims) for t in _split(a, 3))
    ah, al = _split(a, 2)
    bh, bl = _split(b, 2)
    return _dot(ah, bh, dims) + _dot(ah, bl, dims) + _dot(al, bh, dims)


def _gdr_consts():
    causal, strict = _gdr_masks()
    return dict(causal=causal, strict=strict, tril=jnp.where(causal, 1.0, 0.0).astype(F32),
                eye=jnp.where(causal & jnp.logical_not(strict), 1.0, 0.0).astype(F32),
                bcast=jnp.full((CK, HD), 1.0 / HD, F32))


def _dots(a, b, dims):
    return [_dot(x, y, dims) for x, y in zip(a, b)]


def _dots_f32(a, b, dims, exact=None):
    n = len(a)
    if exact == "a":
        lhs = [[x.astype(jnp.bfloat16)] * 3 for x in a]
        rhs = [_split(y, 3) for y in b]
    elif exact == "b":
        lhs = [_split(x, 3) for x in a]
        rhs = [[y.astype(jnp.bfloat16)] * 3 for y in b]
    else:
        sa = [_split(x, 2) for x in a]
        sb = [_split(y, 2) for y in b]
        lhs = [[s[0], s[0], s[1]] for s in sa]
        rhs = [[s[0], s[1], s[0]] for s in sb]
    terms = [[_dot(lhs[i][t], rhs[i][t], dims) for i in range(n)] for t in range(3)]
    return [terms[0][i] + terms[1][i] + terms[2][i] for i in range(n)]


def _gdr_local(q, k, v, beta, g, cst, tinv=None):
    n = len(q)
    R = range(n)
    causal, strict = cst["causal"], cst["strict"]
    gc = _dots_f32([cst["tril"]] * n, [jnp.broadcast_to(g[i], (CK, HD)) for i in R], NN, exact="a")
    g_row = _dots_f32([cst["bcast"]] * n, gc, NT, exact="a")
    decay = [jnp.where(causal, jnp.exp(jnp.where(causal, gc[i][:, 0:CK] - g_row[i], 0.0)), 0.0) for i in R]
    eg = [jnp.exp(gc[i]) for i in R]
    gl = [gc[i][CK - 1:CK, :] for i in R]
    ek = [jnp.exp(gl[i] - gc[i]) for i in R]
    cd = [jnp.exp(gl[i]) for i in R]
    kb = [k[i] * beta[i] for i in R]
    pk = _dots(kb, k, NT)
    if tinv is None:
        xp = [-jnp.where(strict, pk[i] * decay[i], 0.0) for i in R]
        tinv = [cst["eye"] + xp[i] for i in R]
        for _ in range(5):
            xp = _dots_f32(xp, xp, NN)
            tx = _dots_f32(tinv, xp, NN)
            tinv = [tinv[i] + tx[i] for i in R]
    u = _dots(tinv, [v[i] * beta[i] for i in R], NN)
    w = _dots(tinv, [kb[i] * eg[i] for i in R], NN)
    qk = _dots(q, k, NT)
    intra = [jnp.where(causal, qk[i] * decay[i], 0.0) for i in R]
    return dict(decay=decay, eg=eg, ek=ek, cd=cd, kb=kb, pk=pk, tinv=tinv, u=u, w=w, qk=qk, intra=intra,
                q_dec=[q[i] * eg[i] for i in R], k_dec=[k[i] * ek[i] for i in R])


GDR_SUB = 8


def _gdr_fwd(qn, kn, vs, gb):
    L = qn.shape[0]
    nc = L // CK
    cb = min(8, nc)
    rb = cb * CK
    nb = nc // cb
    nsub = GDR_SUB if cb % GDR_SUB == 0 else 1

    def body(q_ref, k_ref, v_ref, gb_ref, o_ref, st_ref, ti_ref, s_ref):
        @pl.when(pl.program_id(0) == 0)
        def _():
            s_ref[...] = jnp.zeros_like(s_ref)

        cst = _gdr_consts()
        heads = range(H)

        def group(gi, carry):
            rows = [pl.ds(pl.multiple_of((gi * nsub + j) * CK, CK), CK) for j in range(nsub)]
            chains = [(j, h) for j in range(nsub) for h in heads]
            gbt = [gb_ref[rows[j], :] for j in range(nsub)]
            cols = lambda h: slice(h * HD, (h + 1) * HD)
            t = _gdr_local([q_ref[rows[j], cols(h)] for j, h in chains], [k_ref[rows[j], cols(h)] for j, h in chains],
                           [v_ref[rows[j], cols(h)] for j, h in chains],
                           [_head_cols(gbt[j], h)[0] for j, h in chains], [_head_cols(gbt[j], h)[1] for j, h in chains], cst)
            s = [s_ref[h] for h in heads]
            for j in range(nsub):
                at = lambda key: [t[key][j * H + h] for h in heads]
                for h in heads:
                    st_ref[h, gi * nsub + j] = s[h]
                    ti_ref[h, gi * nsub + j] = t["tinv"][j * H + h]
                ws = _dots(at("w"), s, NN)
                v_new = [u_h - ws_h for u_h, ws_h in zip(at("u"), ws)]
                o_s = _dots(at("q_dec"), s, NN)
                o_v = _dots(at("intra"), v_new, NN)
                kv = _dots(at("k_dec"), v_new, TN)
                cd = at("cd")
                for h in heads:
                    o_ref[rows[j], cols(h)] = o_s[h] + o_v[h]
                s = [s[h] * cd[h] + kv[h] for h in heads]
            for h in heads:
                s_ref[h] = s[h]
            return carry

        lax.fori_loop(0, cb // nsub, group, 0)

    blk = pl.BlockSpec((rb, H * HD), lambda b: (b, 0))
    return pl.pallas_call(
        body, name="gdr_fwd", grid=(nb,),
        in_specs=[blk, blk, blk, pl.BlockSpec((rb, LANES), lambda b: (b, 0))],
        out_specs=[blk, pl.BlockSpec((H, cb, HD, HD), lambda b: (0, b, 0, 0)),
                   pl.BlockSpec((H, cb, CK, CK), lambda b: (0, b, 0, 0))],
        out_shape=[jax.ShapeDtypeStruct((L, H * HD), F32), jax.ShapeDtypeStruct((H, nc, HD, HD), F32),
                   jax.ShapeDtypeStruct((H, nc, CK, CK), F32)],
        scratch_shapes=[pltpu.VMEM((H, HD, HD), F32)],
        compiler_params=_params(1),
    )(qn, kn, vs, gb)


def _gdr_bwd(qn, kn, vs, gb, states, tinvs, do):
    L = qn.shape[0]
    nc = L // CK
    cb = min(8, nc)
    rb = cb * CK
    nb = nc // cb
    nsub = GDR_SUB if cb % GDR_SUB == 0 else 1

    def body(q_ref, k_ref, v_ref, gb_ref, st_ref, ti_ref, do_ref, dq_ref, dk_ref, dv_ref, dgb_ref, ds_ref):
        @pl.when(pl.program_id(0) == 0)
        def _():
            ds_ref[...] = jnp.zeros_like(ds_ref)

        cst = _gdr_consts()
        causal, strict = cst["causal"], cst["strict"]
        ones = jnp.ones((CK, HD), F32)
        row = lax.broadcasted_iota(jnp.int32, (CK, HD), 0)
        lane = lax.broadcasted_iota(jnp.int32, (CK, LANES), 1)

        heads = range(H)
        rsum = lambda x: jnp.sum(x, axis=-1, keepdims=True)

        def group(gj, carry):
            gi = cb // nsub - 1 - gj
            rows = [pl.ds(pl.multiple_of((gi * nsub + j) * CK, CK), CK) for j in range(nsub)]
            chains = [(j, h) for j in range(nsub) for h in heads]
            gbt = [gb_ref[rows[j], :] for j in range(nsub)]
            cols = lambda h: slice(h * HD, (h + 1) * HD)
            q_all = [q_ref[rows[j], cols(h)] for j, h in chains]
            k_all = [k_ref[rows[j], cols(h)] for j, h in chains]
            v_all = [v_ref[rows[j], cols(h)] for j, h in chains]
            beta_all = [_head_cols(gbt[j], h)[0] for j, h in chains]
            t = _gdr_local(q_all, k_all, v_all, beta_all, [_head_cols(gbt[j], h)[1] for j, h in chains], cst,
                           tinv=[ti_ref[h, gi * nsub + j] for j, h in chains])
            ds_out = [ds_ref[h] for h in heads]
            for j in reversed(range(nsub)):
                at = lambda key: [t[key][j * H + h] for h in heads]
                pick = lambda lst: [lst[j * H + h] for h in heads]
                q, k, v, beta = pick(q_all), pick(k_all), pick(v_all), pick(beta_all)
                u, w, tinv, decay = at("u"), at("w"), at("tinv"), at("decay")
                eg, ek, cd, kb = at("eg"), at("ek"), at("cd"), at("kb")
                q_dec, k_dec, intra, pk, qk = at("q_dec"), at("k_dec"), at("intra"), at("pk"), at("qk")
                s = [st_ref[h, gi * nsub + j] for h in heads]
                dout = [do_ref[rows[j], cols(h)] for h in heads]

                ws = _dots(w, s, NN)
                v_new = [u[h] - ws[h] for h in heads]
                dq_dec = _dots(dout, s, NT)
                qd = _dots(q_dec, dout, TN)
                di = _dots(dout, v_new, NT)
                dintra = [jnp.where(causal, di[h], 0.0) for h in heads]
                ido = _dots(intra, dout, TN)
                kds = _dots(k_dec, ds_out, NN)
                dv_new = [ido[h] + kds[h] for h in heads]
                dk_dec = _dots(v_new, ds_out, NT)
                dcd = [jnp.sum(jnp.sum(ds_out[h] * s[h], axis=1, keepdims=True), axis=0, keepdims=True) for h in heads]
                dvs = _dots(dv_new, s, NT)
                dw = [-dvs[h] for h in heads]
                wdv = _dots(w, dv_new, TN)
                ds_new = [qd[h] + ds_out[h] * cd[h] - wdv[h] for h in heads]
                dru = _dots(tinv, dv_new, TN)
                drw = _dots(tinv, dw, TN)
                dl1 = _dots(dru, u, NT)
                dl2 = _dots(drw, w, NT)
                dlower = [-jnp.where(strict, dl1[h] + dl2[h], 0.0) for h in heads]
                dv = [dru[h] * beta[h] for h in heads]
                dbeta = [rsum(dru[h] * v[h]) for h in heads]
                dgc = [rsum(drw[h] * kb[h]) * eg[h] for h in heads]
                dpk = [dlower[h] * decay[h] for h in heads]
                dqk = [dintra[h] * decay[h] for h in heads]
                dpk_k = _dots(dpk, k, NN)
                dkb = [drw[h] * eg[h] + dpk_k[h] for h in heads]
                dk1 = _dots(dpk, kb, TN)
                dq1 = _dots(dqk, k, NN)
                dk2 = _dots(dqk, q, TN)
                m = [(dlower[h] * pk[h] + dintra[h] * qk[h]) * decay[h] for h in heads]
                mcol = _dots_f32(m, [ones] * H, TN, exact="b")
                e = [rsum(dk_dec[h] * k_dec[h]) for h in heads]
                dgl = [jnp.sum(e[h], axis=0, keepdims=True) + dcd[h] * cd[h] for h in heads]
                dgc = [dgc[h] + rsum(m[h]) - mcol[h] + rsum(dq_dec[h] * q_dec[h]) - e[h]
                       + jnp.where(row == CK - 1, dgl[h], 0.0) for h in heads]
                dg = _dots_f32([cst["tril"]] * H, dgc, TN, exact="a")
                dgb = jnp.zeros((CK, LANES), F32)
                for h in heads:
                    dq_ref[rows[j], cols(h)] = dq1[h] + dq_dec[h] * eg[h]
                    dk_ref[rows[j], cols(h)] = dk1[h] + dk2[h] + dk_dec[h] * ek[h] + dkb[h] * beta[h]
                    dv_ref[rows[j], cols(h)] = dv[h]
                    db = dbeta[h] + rsum(dkb[h] * k[h])
                    dgb = dgb + jnp.where(lane == h, db, 0.0) + jnp.where(lane == H + h, dg[h], 0.0)
                dgb_ref[rows[j], :] = dgb
                ds_out = ds_new
            for h in heads:
                ds_ref[h] = ds_out[h]
            return carry

        lax.fori_loop(0, cb // nsub, group, 0)

    blk = pl.BlockSpec((rb, H * HD), lambda b: (nb - 1 - b, 0))
    sblk = pl.BlockSpec((rb, LANES), lambda b: (nb - 1 - b, 0))
    return pl.pallas_call(
        body, name="gdr_bwd", grid=(nb,),
        in_specs=[blk, blk, blk, sblk, pl.BlockSpec((H, cb, HD, HD), lambda b: (0, nb - 1 - b, 0, 0)),
                  pl.BlockSpec((H, cb, CK, CK), lambda b: (0, nb - 1 - b, 0, 0)), blk],
        out_specs=[blk, blk, blk, sblk],
        out_shape=[jax.ShapeDtypeStruct((L, H * HD), F32)] * 3 + [jax.ShapeDtypeStruct((L, LANES), F32)],
        scratch_shapes=[pltpu.VMEM((H, HD, HD), F32)],
        compiler_params=_params(1),
    )(qn, kn, vs, gb, states, tinvs, do)


def _post_fwd(o, p, ya, x, modrows, sp, w_out):
    L = x.shape[0]
    T = _tile(L, 512)

    def body(o_ref, z_ref, ya_ref, x_ref, mod_ref, sp_ref, w_ref, y_ref, x2_ref, yb_ref):
        ndw = sp_ref[2:3, :]
        z = z_ref[...]
        sz = z * _sig(z)
        parts = []
        for h in range(H):
            n, _ = _rms(o_ref[:, h * HD:(h + 1) * HD])
            parts.append(n * ndw * sz[:, h * HD:(h + 1) * HD])
        yb = jnp.concatenate(parts, axis=-1).astype(MXU)
        yb_ref[...] = yb
        y = _dot(ya_ref[...], w_ref[0:AW, :], NN) + _dot(yb, w_ref[AW:2 * AW, :], NN)
        y_ref[...] = y
        x2_ref[...] = x_ref[...] + mod_ref[2:3, :] * y

    row = lambda i: (i, 0)
    zcol = (3 * AW + 3 * H * HD) // (H * HD)
    return pl.pallas_call(
        body, name="post_fwd", grid=(L // T,),
        in_specs=[pl.BlockSpec((T, H * HD), row), pl.BlockSpec((T, H * HD), lambda i: (i, zcol)),
                  pl.BlockSpec((T, AW), row), pl.BlockSpec((T, D), row), _full((SUB, D)), _full((SUB, LANES)),
                  _full((D, D))],
        out_specs=[pl.BlockSpec((T, D), row), pl.BlockSpec((T, D), row), pl.BlockSpec((T, H * HD), row)],
        out_shape=[jax.ShapeDtypeStruct((L, D), F32), jax.ShapeDtypeStruct((L, D), F32),
                   jax.ShapeDtypeStruct((L, H * HD), MXU)],
        compiler_params=_params(1),
    )(o, p, ya, x, modrows, sp, w_out)


FF_COLS = 2
FF_CW = DFF // FF_COLS
FF_ROWS = 512


def _ffn_fwd_half(x2, modrows, vec, w_up, cff, w_down, j, d_prev):
    assert FF_COLS == 2
    L = x2.shape[0]
    T = _tile(L, FF_ROWS)
    nj = FF_COLS
    last = d_prev is not None

    def body(*refs):
        x_ref, mod_ref, vec_ref, wg_ref, wu_ref, cg_ref, cu_ref, wd_ref = refs[:8]
        if last:
            dp_ref, gp_ref, up_ref, gc_ref, uc_ref, f_ref, d_ref, x3_ref, carry_g, carry_u = refs[8:]
        else:
            h_ref, gp_ref, up_ref, gc_ref, uc_ref, f_ref, d_ref, carry_g, carry_u = refs[8:]

        @pl.when(pl.program_id(0) == 0)
        def _():
            carry_g[...] = jnp.zeros_like(carry_g)
            carry_u[...] = jnp.zeros_like(carry_u)

        xv = x_ref[...]
        n, _ = _rms(xv)
        hb = (n * vec_ref[1:2, :] * (1.0 + mod_ref[4:5, :]) + mod_ref[3:4, :]).astype(MXU)
        if not last:
            h_ref[...] = hb
        g = _dot(hb, wg_ref[...], NN)
        u = _dot(hb, wu_ref[...], NN)
        gp_ref[...] = g.astype(MXU)
        up_ref[...] = u.astype(MXU)
        gc, _ = _conv_fwd(g, cg_ref, 3, carry_g[...])
        uc, _ = _conv_fwd(u, cu_ref, 3, carry_u[...])
        carry_g[...] = g[T - SUB:T, :]
        carry_u[...] = u[T - SUB:T, :]
        gc_ref[...] = gc.astype(MXU)
        uc_ref[...] = uc.astype(MXU)
        fb = (gc * _sig(gc) * uc).astype(MXU)
        f_ref[...] = fb
        part = _dot(fb, wd_ref[...], NN)
        if last:
            dv = dp_ref[...] + part
            d_ref[...] = dv
            x3_ref[...] = xv + mod_ref[5:6, :] * dv
        else:
            d_ref[...] = part

    row = lambda i: (i, 0)
    rowD = pl.BlockSpec((T, D), row)
    rowC = pl.BlockSpec((T, FF_CW), row)
    in_specs = [rowD, _full((SUB, D)), _full((SUB, D)),
                pl.BlockSpec((D, FF_CW), lambda i: (0, j)), pl.BlockSpec((D, FF_CW), lambda i: (0, nj + j)),
                pl.BlockSpec((SUB, FF_CW), lambda i: (0, j)), pl.BlockSpec((SUB, FF_CW), lambda i: (0, nj + j)),
                pl.BlockSpec((FF_CW, D), lambda i: (j, 0))]
    half = [jax.ShapeDtypeStruct((L, FF_CW), MXU)] * 5
    args = [x2, modrows, vec, w_up, w_up, cff, cff, w_down]
    if last:
        in_specs.append(rowD)
        args.append(d_prev)
        out_specs = [rowC] * 5 + [rowD, rowD]
        out_shape = half + [jax.ShapeDtypeStruct((L, D), F32), jax.ShapeDtypeStruct((L, D), F32)]
    else:
        out_specs = [rowD] + [rowC] * 5 + [rowD]
        out_shape = [jax.ShapeDtypeStruct((L, D), MXU)] + half + [jax.ShapeDtypeStruct((L, D), F32)]
    return pl.pallas_call(
        body, name="ffn_fwd_last" if last else "ffn_fwd_first", grid=(L // T,),
        in_specs=in_specs, out_specs=out_specs, out_shape=out_shape,
        scratch_shapes=[pltpu.VMEM((SUB, FF_CW), F32), pltpu.VMEM((SUB, FF_CW), F32)],
        compiler_params=_params(1),
    )(*args)


def _ffn_bwd_half(dx3, modrows, gpre, upre, gcv, ucv, cff, w_down, w_up, j, tail):
    assert FF_COLS == 2
    L = dx3.shape[0]
    T = _tile(L, FF_ROWS)
    ni, nj = L // T, FF_COLS
    last = tail is not None

    def body(*refs):
        dx3_ref, mod_ref, gp_ref, up_ref, gc_ref, uc_ref, cg_ref, cu_ref, wd_ref, wg_ref, wu_ref = refs[:11]
        if last:
            (d_ref, x2_ref, vec_ref, dhp_ref, dgp_ref, dup_ref, dx2_ref, accv_ref, dcg_ref, dcu_ref,
             carry_g, carry_u) = refs[11:]
        else:
            dd_ref, dgp_ref, dup_ref, dh_ref, dcg_ref, dcu_ref, carry_g, carry_u = refs[11:]
        i = pl.program_id(0)

        @pl.when(i == 0)
        def _():
            carry_g[...] = jnp.zeros_like(carry_g)
            carry_u[...] = jnp.zeros_like(carry_u)
            dcg_ref[...] = jnp.zeros_like(dcg_ref)
            dcu_ref[...] = jnp.zeros_like(dcu_ref)
            if last:
                accv_ref[...] = jnp.zeros_like(accv_ref)

        dx3v = dx3_ref[...]
        ddb = (mod_ref[5:6, :] * dx3v).astype(MXU)
        if not last:
            dd_ref[...] = ddb
        g, u = gp_ref[...].astype(F32), up_ref[...].astype(F32)
        gc, uc = gc_ref[...].astype(F32), uc_ref[...].astype(F32)
        sg = _sig(gc)
        df = _dot(ddb, wd_ref[...], NT)
        duc = df * (gc * sg)
        dgc = df * uc * (sg * (1.0 + gc * (1.0 - sg)))
        dgs = [dgc] + [_shift_up(dgc, s, carry_g[...]) for s in (1, 2)]
        dus = [duc] + [_shift_up(duc, s, carry_u[...]) for s in (1, 2)]
        for s in range(3):
            dcg_ref[2 - s:3 - s, :] += _sum0(dgs[s] * g)
            dcu_ref[2 - s:3 - s, :] += _sum0(dus[s] * u)
        dg = (cg_ref[2:3, :] * dgs[0] + cg_ref[1:2, :] * dgs[1] + cg_ref[0:1, :] * dgs[2]).astype(MXU)
        du = (cu_ref[2:3, :] * dus[0] + cu_ref[1:2, :] * dus[1] + cu_ref[0:1, :] * dus[2]).astype(MXU)
        carry_g[...] = dgc[0:SUB, :]
        carry_u[...] = duc[0:SUB, :]
        dgp_ref[...] = dg
        dup_ref[...] = du
        dh = _dot(dg, wg_ref[...], NT) + _dot(du, wu_ref[...], NT)
        if last:
            dh = dh + dhp_ref[...]
            accv_ref[0:1, :] += _sum0(dx3v * d_ref[...])
            n, r = _rms(x2_ref[...])
            nw, sc = vec_ref[1:2, :], mod_ref[4:5, :]
            accv_ref[1:2, :] += _sum0(dh)
            accv_ref[2:3, :] += _sum0(dh * n * nw)
            accv_ref[3:4, :] += _sum0(dh * n * (1.0 + sc))
            dx2_ref[...] = _rms_bwd(dh * nw * (1.0 + sc), n, r) + dx3v
        else:
            dh_ref[...] = dh

    row = lambda i: (ni - 1 - i, 0)
    rowD = pl.BlockSpec((T, D), row)
    rowC = pl.BlockSpec((T, FF_CW), row)
    in_specs = [rowD, _full((SUB, D)), rowC, rowC, rowC, rowC,
                pl.BlockSpec((SUB, FF_CW), lambda i: (0, j)), pl.BlockSpec((SUB, FF_CW), lambda i: (0, nj + j)),
                pl.BlockSpec((FF_CW, D), lambda i: (j, 0)),
                pl.BlockSpec((D, FF_CW), lambda i: (0, j)), pl.BlockSpec((D, FF_CW), lambda i: (0, nj + j))]
    args = [dx3, modrows, gpre, upre, gcv, ucv, cff, cff, w_down, w_up, w_up]
    halfb = [jax.ShapeDtypeStruct((L, FF_CW), MXU), jax.ShapeDtypeStruct((L, FF_CW), MXU)]
    dconv = [jax.ShapeDtypeStruct((SUB, FF_CW), F32)] * 2
    if last:
        d, x2, vec, dh_prev = tail
        in_specs += [rowD, rowD, _full((SUB, D)), rowD]
        args += [d, x2, vec, dh_prev]
        out_specs = [rowC, rowC, rowD, _full((SUB, D)), _full((SUB, FF_CW)), _full((SUB, FF_CW))]
        out_shape = halfb + [jax.ShapeDtypeStruct((L, D), F32), jax.ShapeDtypeStruct((SUB, D), F32)] + dconv
    else:
        out_specs = [rowD, rowC, rowC, rowD, _full((SUB, FF_CW)), _full((SUB, FF_CW))]
        out_shape = [jax.ShapeDtypeStruct((L, D), MXU)] + halfb + [jax.ShapeDtypeStruct((L, D), F32)] + dconv
    return pl.pallas_call(
        body, name="ffn_bwd_last" if last else "ffn_bwd_first", grid=(ni,),
        in_specs=in_specs, out_specs=out_specs, out_shape=out_shape,
        scratch_shapes=[pltpu.VMEM((SUB, FF_CW), F32), pltpu.VMEM((SUB, FF_CW), F32)],
        compiler_params=_params(1),
    )(*args)


def _final(x, target, nf):
    L = x.shape[0]
    T = _tile(L, 256)

    def body(x_ref, t_ref, nf_ref, dx_ref, acc_ref):
        @pl.when(pl.program_id(0) == 0)
        def _():
            acc_ref[...] = jnp.zeros_like(acc_ref)

        n, r = _rms(x_ref[...])
        w = nf_ref[0:1, :]
        err = n * w - t_ref[...]
        acc_ref[0:1, :] += (0.5 / D) * _sum0(err * err)
        dy = err * (1.0 / D)
        acc_ref[1:2, :] += _sum0(dy * n)
        dx_ref[...] = _rms_bwd(dy * w, n, r)

    row = lambda i: (i, 0)
    return pl.pallas_call(
        body, name="final_norm_loss", grid=(L // T,),
        in_specs=[pl.BlockSpec((T, D), row), pl.BlockSpec((T, D), row), _full((SUB, D))],
        out_specs=[pl.BlockSpec((T, D), row), _full((SUB, D))],
        out_shape=[jax.ShapeDtypeStruct((L, D), F32), jax.ShapeDtypeStruct((SUB, D), F32)],
        compiler_params=_params(1),
    )(x, target, nf)


def _post_bwd(dx2, y, o, p, modrows, sp, w_out):
    L = dx2.shape[0]
    T = _tile(L, 512)

    def body(dx2_ref, y_ref, o_ref, z_ref, mod_ref, sp_ref, w_ref, dy_ref, do_ref, dz_ref, dya_ref, accv_ref, accs_ref):
        @pl.when(pl.program_id(0) == 0)
        def _():
            accv_ref[...] = jnp.zeros_like(accv_ref)
            accs_ref[...] = jnp.zeros_like(accs_ref)

        dx2v = dx2_ref[...]
        accv_ref[0:1, :] += _sum0(dx2v * y_ref[...])
        dyb = (mod_ref[2:3, :] * dx2v).astype(MXU)
        dy_ref[...] = dyb
        dyc = _dot(dyb, w_ref[...], NT)
        dya_ref[...] = dyc[:, 0:AW]
        ndw = sp_ref[2:3, :]
        z = z_ref[...]
        sgz = _sig(z)
        dsz = sgz * (1.0 + z * (1.0 - sgz))
        dndw = jnp.zeros((1, HD), F32)
        for h in range(H):
            sl = slice(h * HD, (h + 1) * HD)
            n, r = _rms(o_ref[:, sl])
            dyh = dyc[:, AW + h * HD:AW + (h + 1) * HD]
            zh = z[:, sl]
            don = dyh * (zh * sgz[:, sl])
            dz_ref[:, sl] = dyh * (n * ndw) * dsz[:, sl]
            dndw = dndw + _sum0(don * n)
            do_ref[:, sl] = _rms_bwd(don * ndw, n, r)
        accs_ref[0:1, :] += dndw

    row = lambda i: (i, 0)
    zcol = (3 * AW + 3 * H * HD) // (H * HD)
    return pl.pallas_call(
        body, name="post_bwd", grid=(L // T,),
        in_specs=[pl.BlockSpec((T, D), row), pl.BlockSpec((T, D), row), pl.BlockSpec((T, H * HD), row),
                  pl.BlockSpec((T, H * HD), lambda i: (i, zcol)), _full((SUB, D)), _full((SUB, LANES)), _full((D, D))],
        out_specs=[pl.BlockSpec((T, D), row)] + [pl.BlockSpec((T, H * HD), row)] * 3 + [_full((SUB, D)), _full((SUB, LANES))],
        out_shape=[jax.ShapeDtypeStruct((L, D), MXU)] + [jax.ShapeDtypeStruct((L, H * HD), F32)] * 3
        + [jax.ShapeDtypeStruct((SUB, D), F32), jax.ShapeDtypeStruct((SUB, LANES), F32)],
        compiler_params=_params(1),
    )(dx2, y, o, p, modrows, sp, w_out)


def _pre_in_bwd(p, cub, qcb, dqn, dkn, dvs, dya, dz, dgb, pa, cq, sp, w_in, x, dx2, modrows, vec):
    L = p.shape[0]
    T = _tile(L, 256)
    ni = L // T
    scale = HD ** -0.5
    w3 = 3 * AW + 3 * H * HD

    def body(pm_ref, cu_ref, qc_ref, ps_ref, dq_ref, dk_ref, dv_ref, dya_ref, dz_ref, dgb_ref, pa_ref, cq_ref, sp_ref,
             w_ref, x_ref, dx2_ref, mod_ref, vec_ref,
             dp_ref, dx_ref, dpa_ref, dcq_ref, dsp_ref, accv_ref, carry_u, carry_q):
        i = pl.program_id(0)

        @pl.when(i == 0)
        def _():
            dpa_ref[...] = jnp.zeros_like(dpa_ref)
            dcq_ref[...] = jnp.zeros_like(dcq_ref)
            dsp_ref[...] = jnp.zeros_like(dsp_ref)
            accv_ref[...] = jnp.zeros_like(accv_ref)
            carry_u[...] = jnp.zeros_like(carry_u)
            carry_q[...] = jnp.zeros_like(carry_q)

        a_b, a_c, a_x = pm_ref[:, 0:AW], pm_ref[:, AW:2 * AW], pm_ref[:, 2 * AW:3 * AW]
        u = a_c * a_x
        cu = cu_ref[...].astype(F32)
        yp = a_b * cu
        bd = _blockdiag_mean(AW, A_GROUP)
        ra = lax.rsqrt(_dot_f32(yp * yp, bd, NN, exact="b") + EPS)
        na = yp * ra
        dya = dya_ref[...]
        dpa_ref[3:4, :] += _sum0(dya * na)
        dna = dya * pa_ref[3:4, :]
        dyp = ra * (dna - na * _dot_f32(dna * na, bd, NN, exact="b"))
        dcu = dyp * a_b
        dcs = [dcu] + [_shift_up(dcu, s, carry_u[...]) for s in (1, 2)]
        du = pa_ref[2:3, :] * dcs[0]
        for s in range(3):
            dpa_ref[2 - s:3 - s, :] += _sum0(dcs[s] * u)
            if s:
                du = du + pa_ref[2 - s:3 - s, :] * dcs[s]
        carry_u[...] = dcu[0:SUB, :]
        dp_a = jnp.concatenate([dyp * cu, du * a_x, du * a_c], axis=-1).astype(MXU)
        dp_ref[:, 0:3 * AW] = dp_a
        dh = _dot(dp_a, w_ref[:, 0:3 * AW], NT)

        qkv = pm_ref[:, 3 * AW:w3]
        qc = qc_ref[...].astype(F32)
        sg = _sig(qc)
        qs = qc * sg
        parts = []
        for h in range(H):
            q = qs[:, h * HD:(h + 1) * HD]
            rq = lax.rsqrt(jnp.sum(q * q, axis=-1, keepdims=True) + EPS)
            parts.append(_l2_bwd(dq_ref[:, h * HD:(h + 1) * HD] * scale, q * rq, rq))
        for h in range(H):
            k = qs[:, (H + h) * HD:(H + h + 1) * HD]
            rk = lax.rsqrt(jnp.sum(k * k, axis=-1, keepdims=True) + EPS)
            parts.append(_l2_bwd(dk_ref[:, h * HD:(h + 1) * HD], k * rk, rk))
        parts.append(dv_ref[...])
        dqc = jnp.concatenate(parts, axis=-1) * (sg * (1.0 + qc * (1.0 - sg)))
        dqs = [dqc] + [_shift_up(dqc, s, carry_q[...]) for s in (1, 2, 3)]
        dqkv = cq_ref[3:4, :] * dqs[0]
        for s in range(4):
            dcq_ref[3 - s:4 - s, :] += _sum0(dqs[s] * qkv)
            if s:
                dqkv = dqkv + cq_ref[3 - s:4 - s, :] * dqs[s]
        dp_q = dqkv.astype(MXU)
        dp_ref[:, 3 * AW:w3] = dp_q
        dh = dh + _dot(dp_q, w_ref[:, 3 * AW:w3], NT)
        carry_q[...] = dqc[0:SUB, :]

        lane, a, xb, beta, g = _gate_small(ps_ref[...], sp_ref)
        dgb = dgb_ref[...]
        dbeta = jnp.where(lane < H, dgb, 0.0)
        dg = jnp.where((lane >= H) & (lane < 2 * H), dgb, 0.0)
        dalpha = dg * a * _sig(xb)
        dsp_ref[0:1, :] += _sum0(dg * g)
        dsp_ref[1:2, :] += _sum0(dalpha)
        dp_z = jnp.concatenate([dz_ref[...], dbeta * beta * (1.0 - beta) + dalpha], axis=-1).astype(MXU)
        dp_ref[:, w3:P_PAD] = dp_z
        dh = dh + _dot(dp_z, w_ref[:, w3:P_PAD], NT)

        n, r = _rms(x_ref[...])
        nw, sc = vec_ref[0:1, :], mod_ref[1:2, :]
        accv_ref[0:1, :] += _sum0(dh)
        accv_ref[1:2, :] += _sum0(dh * n * nw)
        accv_ref[2:3, :] += _sum0(dh * n * (1.0 + sc))
        dx_ref[...] = _rms_bwd(dh * nw * (1.0 + sc), n, r) + dx2_ref[...]

    row = lambda i: (ni - 1 - i, 0)
    hrow = pl.BlockSpec((T, H * HD), row)
    rowD = pl.BlockSpec((T, D), row)
    return pl.pallas_call(
        body, name="pre_in_bwd", grid=(ni,),
        in_specs=[pl.BlockSpec((T, w3), row), pl.BlockSpec((T, AW), row), pl.BlockSpec((T, 3 * H * HD), row),
                  pl.BlockSpec((T, LANES), lambda i: (ni - 1 - i, (P_PAD - LANES) // LANES)),
                  hrow, hrow, hrow, pl.BlockSpec((T, AW), row), hrow,
                  pl.BlockSpec((T, LANES), row),
                  _full((SUB, AW)), _full((SUB, 3 * H * HD)), _full((SUB, LANES)),
                  _full((D, P_PAD)), rowD, rowD, _full((SUB, D)), _full((SUB, D))],
        out_specs=[pl.BlockSpec((T, P_PAD), row), rowD, _full((SUB, AW)), _full((SUB, 3 * H * HD)), _full((SUB, LANES)),
                   _full((SUB, D))],
        out_shape=[jax.ShapeDtypeStruct((L, P_PAD), MXU), jax.ShapeDtypeStruct((L, D), F32),
                   jax.ShapeDtypeStruct((SUB, AW), F32), jax.ShapeDtypeStruct((SUB, 3 * H * HD), F32),
                   jax.ShapeDtypeStruct((SUB, LANES), F32), jax.ShapeDtypeStruct((SUB, D), F32)],
        scratch_shapes=[pltpu.VMEM((SUB, AW), F32), pltpu.VMEM((SUB, 3 * H * HD), F32)],
        compiler_params=_params(1),
    )(p, cub, qcb, p, dqn, dkn, dvs, dya, dz, dgb, pa, cq, sp, w_in, x, dx2, modrows, vec)


def _wgrad(a, b, tm, tn, name):
    L, m = a.shape
    n = b.shape[1]
    tl = _tile(L, 1024)
    tm, tn = _tile(m, tm), _tile(n, tn)
    nl = L // tl

    def body(a_ref, b_ref, o_ref, acc):
        @pl.when(pl.program_id(2) == 0)
        def _():
            acc[...] = jnp.zeros_like(acc)

        acc[...] += _dot(a_ref[...], b_ref[...], TN)

        @pl.when(pl.program_id(2) == nl - 1)
        def _():
            o_ref[...] = acc[...].astype(o_ref.dtype)

    return pl.pallas_call(
        body, name=name, grid=(m // tm, n // tn, nl),
        in_specs=[pl.BlockSpec((tl, tm), lambda i, j, l: (l, i)), pl.BlockSpec((tl, tn), lambda i, j, l: (l, j))],
        out_specs=pl.BlockSpec((tm, tn), lambda i, j, l: (i, j)),
        out_shape=jax.ShapeDtypeStruct((m, n), MXU), scratch_shapes=[pltpu.VMEM((tm, tn), F32)],
        compiler_params=_params(3),
    )(a, b)


def _wgrad_cols(a, b, tm, n_shard, wpad, count, name):
    L, m = a.shape
    n = b.shape[1]
    tl = _tile(L, 1024)
    tm = _tile(m, tm)
    nl = L // tl
    wins = _shard_windows(n_shard, count)
    assert all(a_ * LANES + win <= n for a_, _, win in wins), (wins, n)

    def body(a_ref, b_ref, o_ref, acc):
        @pl.when(pl.program_id(1) == 0)
        def _():
            acc[...] = jnp.zeros_like(acc)

        acc[...] += _dot(a_ref[...], b_ref[...], TN)

        @pl.when(pl.program_id(1) == nl - 1)
        def _():
            for k, (a_, s, win) in enumerate(wins):
                xk = acc[:, a_ * LANES:a_ * LANES + win]
                if s:
                    xk = pltpu.roll(xk, win - s, 1)
                o_ref[k] = _fit_lanes(xk, wpad).astype(o_ref.dtype)

    return pl.pallas_call(
        body, name=name, grid=(m // tm, nl),
        in_specs=[pl.BlockSpec((tl, tm), lambda i, l: (l, i)), pl.BlockSpec((tl, n), lambda i, l: (l, 0))],
        out_specs=pl.BlockSpec((count, tm, wpad), lambda i, l: (0, i, 0)),
        out_shape=jax.ShapeDtypeStruct((count, m, wpad), MXU),
        scratch_shapes=[pltpu.VMEM((tm, n), F32)],
        compiler_params=_params(2),
    )(a, b)


def _adamw(w, g, m, v, name):
    r, n = w.shape
    tr = _tile(r, 512)
    bc1 = 1.0 - ADAM_B1 ** ADAM_STEP
    bc2 = 1.0 - ADAM_B2 ** ADAM_STEP

    def body(w_ref, g_ref, m_ref, v_ref, d_ref, nm_ref, nv_ref):
        gv = g_ref[...]
        nm = ADAM_B1 * m_ref[...] + (1.0 - ADAM_B1) * gv
        nv = ADAM_B2 * v_ref[...] + (1.0 - ADAM_B2) * (gv * gv)
        nm_ref[...] = nm
        nv_ref[...] = nv
        d_ref[...] = -ADAM_LR * ((nm / bc1) / (jnp.sqrt(nv / bc2) + ADAM_EPS) + ADAM_WD * w_ref[...])

    spec = pl.BlockSpec((tr, n), lambda i: (i, 0))
    return pl.pallas_call(
        body, name=name, grid=(r // tr,), in_specs=[spec] * 4, out_specs=[spec] * 3,
        out_shape=[jax.ShapeDtypeStruct((r, n), F32)] * 3, compiler_params=_params(1),
    )(w, g, m, v)


def _rows8(rows, width):
    out = jnp.zeros((SUB, width), F32)
    for r, vrow in enumerate(rows):
        out = out.at[r, :vrow.shape[0]].set(vrow)
    return out


def _at_lanes(v4, start):
    return jnp.zeros((LANES,), F32).at[start:start + v4.shape[0]].set(v4)


def _pad_rows(flat, mult):
    n = flat.shape[0]
    pad = (-n) % mult
    return jnp.pad(flat, (0, pad)) if pad else flat


IN_PAD = 512
UP_PAD = 768


def _local_fwd_bwd(x, target, mod_full, small_w, full_w, on_grads=None):
    norm1_w, norm2_w, norm_a_w, a_log, dt_bias, norm_dn_w, norm_f_w = small_w
    w_in_f, w_out_f, w_up_f, w_down_f, conv_a_f, conv_q_f, conv_f_f = full_w

    def layer_params(i):
        modrows = jnp.concatenate([mod_full[i], jnp.zeros((SUB - N_MOD, D), F32)], axis=0)
        vec = _rows8([norm1_w[i], norm2_w[i]], D)
        pa = _rows8([conv_a_f[i, 0], conv_a_f[i, 1], conv_a_f[i, 2], norm_a_w[i]], AW)
        cq = _rows8([conv_q_f[i, k] for k in range(4)], 3 * H * HD)
        sp = _rows8([_at_lanes(a_log[i], H), _at_lanes(dt_bias[i], H), norm_dn_w[i]], LANES)
        cff = _rows8([conv_f_f[i, k] for k in range(3)], 2 * DFF)
        return modrows, vec, pa, cq, sp, cff

    saved = []
    xi = x
    for i in range(DEPTH):
        modrows, vec, pa, cq, sp, cff = layer_params(i)
        p, h1, qn, kn, vs, gb, ya, cub, qcb = _in_pre_fwd(xi, modrows, vec, w_in_f[i], pa, cq, sp)
        o, states, tinvs = _gdr_fwd(qn, kn, vs, gb)
        y, x2, yb = _post_fwd(o, p, ya, xi, modrows, sp, w_out_f[i])
        h2, gp0, up0, gc0, uc0, f0, d0 = _ffn_fwd_half(x2, modrows, vec, w_up_f[i], cff, w_down_f[i], 0, None)
        gp1, up1, gc1, uc1, f1, dff, x3 = _ffn_fwd_half(x2, modrows, vec, w_up_f[i], cff, w_down_f[i], 1, d0)
        saved.append(dict(x=xi, p=p, h1=h1, qn=qn, kn=kn, vs=vs, gb=gb, ya=ya, cub=cub, qcb=qcb, o=o, states=states,
                          tinvs=tinvs, y=y, x2=x2, yb=yb,
                          h2=h2, gpre=(gp0, gp1), upre=(up0, up1), gc=(gc0, gc1), uc=(uc0, uc1), f=(f0, f1), d=dff))
        xi = x3

    dx, facc = _final(xi, target, _rows8([norm_f_w], D))
    loss_local = jnp.sum(facc[0])
    d_norm_f = facc[1]

    gw_in, gw_out, gw_up, gw_down = [None] * DEPTH, [None] * DEPTH, [None] * DEPTH, [None] * DEPTH
    g_small = [None] * DEPTH
    for i in reversed(range(DEPTH)):
        s = saved[i]
        modrows, vec, pa, cq, sp, cff = layer_params(i)
        dd, dgp0, dup0, dh0, dcg0, dcu0 = _ffn_bwd_half(dx, modrows, s["gpre"][0], s["upre"][0], s["gc"][0], s["uc"][0],
                                                        cff, w_down_f[i], w_up_f[i], 0, None)
        dgp1, dup1, dx2, accf, dcg1, dcu1 = _ffn_bwd_half(dx, modrows, s["gpre"][1], s["upre"][1], s["gc"][1], s["uc"][1],
                                                          cff, w_down_f[i], w_up_f[i], 1, (s["d"], s["x2"], vec, dh0))
        n_up, up_pad = 2 * DFF // N_DEV, UP_PAD
        gw_up[i] = jnp.concatenate([_wgrad_cols(s["h2"], t, 1024, n_up, up_pad, FF_CW // n_up, "wgrad_up")
                                    for t in (dgp0, dgp1, dup0, dup1)], axis=0)
        gw_down[i] = jnp.concatenate([_wgrad(s["f"][0], dd, FF_CW, 1024, "wgrad_down"),
                                      _wgrad(s["f"][1], dd, FF_CW, 1024, "wgrad_down")],
                                     axis=0).reshape(N_DEV, DFF // N_DEV, D)
        if on_grads is not None:
            on_grads(i, "ffn", [gw_up[i], gw_down[i]])
        dy, do, dz, dya, accp, accs = _post_bwd(dx2, s["y"], s["o"], s["p"], modrows, sp, w_out_f[i])
        gw_out[i] = jnp.concatenate([_wgrad(s["ya"], dy, 512, 1024, "wgrad_out"),
                                     _wgrad(s["yb"], dy, 512, 1024, "wgrad_out")], axis=0).reshape(N_DEV, D // N_DEV, D)
        dqn, dkn, dvs, dgb = _gdr_bwd(s["qn"], s["kn"], s["vs"], s["gb"], s["states"], s["tinvs"], do)
        dp, dx, dpa, dcq, dsp, acci = _pre_in_bwd(s["p"], s["cub"], s["qcb"], dqn, dkn, dvs, dya, dz, dgb, pa, cq, sp,
                                                  w_in_f[i], s["x"], dx2, modrows, vec)
        gw_in[i] = _wgrad_cols(s["h1"], dp, 1024, P_IN // N_DEV, IN_PAD, N_DEV, "wgrad_in")
        dconv_ff = jnp.concatenate([dcg0, dcg1, dcu0, dcu1], axis=1)[0:3]
        dmod = jnp.stack([acci[0], acci[1], accp[0], accf[1], accf[2], accf[0]])
        g_small[i] = dict(norm1=acci[2], norm2=accf[3], norm_a=dpa[3], a_log=dsp[0, H:2 * H], dt_bias=dsp[1, H:2 * H],
                          norm_dn=accs[0], conv_a=dpa[0:3], conv_qkv=dcq[0:4], conv_ff=dconv_ff, dmod=dmod.reshape(-1))
        if on_grads is not None:
            on_grads(i, "mix", [gw_in[i], gw_out[i]])
    return loss_local, dx, gw_in, gw_out, gw_up, gw_down, g_small, d_norm_f


def kernel(x, c, ada_w, ada_b, norm1_w, w_in, conv_a_w, norm_a_w, conv_qkv_w, a_log, dt_bias, norm_dn_w, w_out, norm2_w, w_up, conv_ff_w, w_down, norm_f_w, loss_target, m_ada_w, m_ada_b, m_norm1_w, m_w_in, m_conv_a_w, m_norm_a_w, m_conv_qkv_w, m_a_log, m_dt_bias, m_norm_dn_w, m_w_out, m_norm2_w, m_w_up, m_conv_ff_w, m_w_down, m_norm_f_w, v_ada_w, v_ada_b, v_norm1_w, v_w_in, v_conv_a_w, v_norm_a_w, v_conv_qkv_w, v_a_log, v_dt_bias, v_norm_dn_w, v_w_out, v_norm2_w, v_w_up, v_conv_ff_w, v_w_down, v_norm_f_w):
    ax, ay, ac = lax.axis_index("x"), lax.axis_index("y"), lax.axis_index("c")
    me = 4 * ax + 2 * ay + ac
    x = x[0]
    target = loss_target[0]
    n_in, n_up = P_IN // N_DEV, 2 * DFF // N_DEV

    def lane_pad(t, width):
        return jnp.pad(t.astype(MXU), ((0, 0), (0, 0), (0, width - t.shape[-1])))

    conv_blob = _pad_rows(jnp.concatenate([t.reshape(-1) for t in (conv_a_w, conv_qkv_w, conv_ff_w)]),
                          SUB * LANES).reshape(-1, LANES)
    c_rows = jnp.zeros((SUB, D), F32).at[0].set(c[0])
    send = [lane_pad(w_in, IN_PAD), w_out.astype(MXU), lane_pad(w_up, UP_PAD), w_down.astype(MXU)]
    got = [None] * DEPTH
    g_in0, g_conv, g_c = _all_gather([send[0][0], conv_blob, c_rows], "gather_weights", in_vmem=False)
    shards, _ = lax.optimization_barrier(([t[0] for t in send[1:]], g_c))
    got[0] = [g_in0] + _all_gather_async(shards, "gather_weights_l0", collective_id=0)
    for i in range(1, DEPTH):
        shards, _ = lax.optimization_barrier(([t[i] for t in send], g_c))
        got[i] = _all_gather_async(shards, "gather_weights_l%d" % i, collective_id=i)
    w_in_f = [_interleave_cols(g[0][:, None], n_in, P_PAD, "interleave_w_in")[0] for g in got]
    w_up_f = [_interleave_cols(g[2][:, None], n_up, 2 * DFF, "interleave_w_up")[0] for g in got]
    w_out_f = [g[1].reshape(D, D) for g in got]
    w_down_f = [g[3].reshape(DFF, D) for g in got]
    sg = g_conv.reshape(N_DEV, -1)
    o1 = conv_a_w.size
    o2 = o1 + conv_qkv_w.size
    o3 = o2 + conv_ff_w.size
    conv_a_f = sg[:, 0:o1].reshape(N_DEV, DEPTH, 3, AW // N_DEV).transpose(1, 2, 0, 3).reshape(DEPTH, 3, AW)
    conv_q_f = sg[:, o1:o2].reshape(N_DEV, DEPTH, 4, 3 * H * HD // N_DEV).transpose(1, 2, 0, 3).reshape(DEPTH, 4, 3 * H * HD)
    conv_f_f = sg[:, o2:o3].reshape(N_DEV, DEPTH, 3, n_up).transpose(1, 2, 0, 3).reshape(DEPTH, 3, 2 * DFF)

    c_all = jnp.concatenate([g_c[:, 0], jnp.zeros((16 - N_DEV, D), F32)], axis=0)
    n_ada = N_MOD * D // N_DEV
    ada_b_cols = lax.dynamic_slice_in_dim(ada_b, me * n_ada, n_ada, axis=1)[:, None, :]
    mod_sh = _mod_fwd(c_all, ada_w, ada_b_cols)
    mod_all = _all_gather([mod_sh.reshape(DEPTH * 16, n_ada)], "gather_mod", in_vmem=True)[0]
    mod_all = mod_all.reshape(N_DEV, DEPTH, 16, n_ada)
    mod_mine = lax.dynamic_index_in_dim(mod_all, me, axis=2, keepdims=False)
    mod_full = mod_mine.transpose(1, 0, 2).reshape(DEPTH, N_MOD, D)

    tags = ["w_in", "w_out", "w_up", "w_down"]
    received = [dict() for _ in range(DEPTH)]

    def on_grads(i, part, gs_i):
        first_id = DEPTH if part == "ffn" else 2 * DEPTH
        got_i = _rs_exchange_async(gs_i, "rs_exchange_%s_l%d" % (part, i), collective_id=first_id + i)
        received[i].update(zip(("w_up", "w_down") if part == "ffn" else ("w_in", "w_out"), got_i))

    loss_local, dx, _, _, _, _, g_small, d_norm_f = _local_fwd_bwd(
        x, target, mod_full, (norm1_w, norm2_w, norm_a_w, a_log, dt_bias, norm_dn_w, norm_f_w),
        (w_in_f, w_out_f, w_up_f, w_down_f, conv_a_f, conv_q_f, conv_f_f), on_grads)
    loss = lax.psum(loss_local, ("x", "y", "c"))
    grad_x = dx[None]

    keys = ["dmod", "norm1", "norm2", "norm_a", "a_log", "dt_bias", "norm_dn", "conv_a", "conv_qkv", "conv_ff"]
    stacked = {k: jnp.stack([g_small[i][k] for i in range(DEPTH)]) for k in keys}
    flat_parts = [stacked[k].reshape(-1) for k in keys] + [d_norm_f]
    sizes = [int(t.shape[0]) for t in flat_parts]
    sflat = _pad_rows(jnp.concatenate(flat_parts), SUB * LANES).reshape(-1, LANES)
    sall = _all_gather([sflat], "gather_small_grads", in_vmem=True)[0]
    ssum = _sum_devices(sall).reshape(-1)
    so = [0]
    for sz in sizes:
        so.append(so[-1] + sz)
    red = {k: ssum[so[n]:so[n + 1]].reshape(stacked[k].shape) for n, k in enumerate(keys)}
    g_norm_f = ssum[so[len(keys)]:so[len(keys) + 1]]
    dmod_all = sall[:, 0:sizes[0] // LANES, :].reshape(N_DEV, DEPTH, N_MOD * D)

    g_ada_b = red["dmod"].reshape(DEPTH, N_MOD * D)
    dmod_cols = lax.dynamic_slice_in_dim(dmod_all, me * n_ada, n_ada, axis=2).transpose(1, 0, 2)
    dmod_cols = jnp.concatenate([dmod_cols, jnp.zeros((DEPTH, 16 - N_DEV, n_ada), F32)], axis=1)
    g_ada_w = _mod_bwd(c_all, dmod_cols)
    g_conv_a = lax.dynamic_slice_in_dim(red["conv_a"], me * (AW // N_DEV), AW // N_DEV, axis=2)
    g_conv_qkv = lax.dynamic_slice_in_dim(red["conv_qkv"], me * (3 * H * HD // N_DEV), 3 * H * HD // N_DEV, axis=2)
    g_conv_ff = lax.dynamic_slice_in_dim(red["conv_ff"], me * n_up, n_up, axis=2)

    mine = [jnp.stack([_rs_sum(received[i][t], "rs_sum_" + t) for i in range(DEPTH)]) for t in tags]
    g_w_in = mine[0][:, :, :n_in]
    g_w_out = mine[1]
    g_w_up = mine[2][:, :, :n_up]
    g_w_down = mine[3]

    grads = dict(ada_w=g_ada_w, ada_b=g_ada_b, norm1_w=red["norm1"], w_in=g_w_in, conv_a_w=g_conv_a,
                 norm_a_w=red["norm_a"], conv_qkv_w=g_conv_qkv, a_log=red["a_log"], dt_bias=red["dt_bias"],
                 norm_dn_w=red["norm_dn"], w_out=g_w_out, norm2_w=red["norm2"], w_up=g_w_up, conv_ff_w=g_conv_ff,
                 w_down=g_w_down, norm_f_w=g_norm_f)
    weights = dict(ada_w=ada_w, ada_b=ada_b, norm1_w=norm1_w, w_in=w_in, conv_a_w=conv_a_w, norm_a_w=norm_a_w,
                   conv_qkv_w=conv_qkv_w, a_log=a_log, dt_bias=dt_bias, norm_dn_w=norm_dn_w, w_out=w_out,
                   norm2_w=norm2_w, w_up=w_up, conv_ff_w=conv_ff_w, w_down=w_down, norm_f_w=norm_f_w)
    ms = dict(ada_w=m_ada_w, ada_b=m_ada_b, norm1_w=m_norm1_w, w_in=m_w_in, conv_a_w=m_conv_a_w, norm_a_w=m_norm_a_w,
              conv_qkv_w=m_conv_qkv_w, a_log=m_a_log, dt_bias=m_dt_bias, norm_dn_w=m_norm_dn_w, w_out=m_w_out,
              norm2_w=m_norm2_w, w_up=m_w_up, conv_ff_w=m_conv_ff_w, w_down=m_w_down, norm_f_w=m_norm_f_w)
    vs_ = dict(ada_w=v_ada_w, ada_b=v_ada_b, norm1_w=v_norm1_w, w_in=v_w_in, conv_a_w=v_conv_a_w, norm_a_w=v_norm_a_w,
               conv_qkv_w=v_conv_qkv_w, a_log=v_a_log, dt_bias=v_dt_bias, norm_dn_w=v_norm_dn_w, w_out=v_w_out,
               norm2_w=v_norm2_w, w_up=v_w_up, conv_ff_w=v_conv_ff_w, w_down=v_w_down, norm_f_w=v_norm_f_w)
    names = list(weights)
    big_names = ["ada_w", "w_in", "w_out", "w_up", "w_down"]
    delta, new_m, new_v = {}, {}, {}
    for n in big_names:
        shp = weights[n].shape
        two = lambda t: t.reshape(-1, shp[-1])
        dl, nm, nv = _adamw(two(weights[n]), two(grads[n]), two(ms[n]), two(vs_[n]), "adamw_" + n)
        delta[n], new_m[n], new_v[n] = dl.reshape(shp), nm.reshape(shp), nv.reshape(shp)
    small_names = [n for n in names if n not in big_names]

    def pack(dct):
        return _pad_rows(jnp.concatenate([dct[n].reshape(-1) for n in small_names]), SUB * LANES).reshape(-1, LANES)

    dl, nm, nv = _adamw(pack(weights), pack(grads), pack(ms), pack(vs_), "adamw_small")
    off = 0
    for n in small_names:
        sz, shp = weights[n].size, weights[n].shape
        delta[n] = dl.reshape(-1)[off:off + sz].reshape(shp)
        new_m[n] = nm.reshape(-1)[off:off + sz].reshape(shp)
        new_v[n] = nv.reshape(-1)[off:off + sz].reshape(shp)
        off += sz

    return (loss, grad_x, *[grads[n] for n in names], *[delta[n] for n in names],
            *[new_m[n] for n in names], *[new_v[n] for n in names])
```

```python
import functools
import math

import jax
import jax.numpy as jnp
from jax import lax
from jax.experimental import pallas as pl
from jax.experimental.pallas import tpu as pltpu
from jax.experimental.pallas import tpu_sc as plsc

F32 = jnp.float32
MXU = jnp.bfloat16

D = 1024
DEPTH = 4
N_MOD = 6
AW = 512
A_GROUP = 64
H = 4
HD = 128
CK = 64
DFF = 2816
P_IN = 3592
P_PAD = 3712
EPS = 1e-6
N_DEV = 8
LANES = 128
SUB = 8
VMEM_LIMIT = 56 * 1024 * 1024

ADAM_LR, ADAM_B1, ADAM_B2, ADAM_EPS, ADAM_WD, ADAM_STEP = 0.001, 0.9, 0.999, 1e-08, 0.01, 10

NN = ((1,), (0,))
NT = ((1,), (1,))
TN = ((0,), (0,))
HI = lax.Precision.HIGHEST
MESH = pl.DeviceIdType.MESH


def _dot(a, b, dims, prec=None):
    if prec is None:
        a = a.astype(MXU) if a.dtype == F32 else a
        b = b.astype(MXU) if b.dtype == F32 else b
    return lax.dot_general(a, b, (dims, ((), ())), precision=prec, preferred_element_type=F32)


def _params(n_grid=0, limit=VMEM_LIMIT):
    sem = ("arbitrary",) * n_grid if n_grid else None
    return pltpu.CompilerParams(dimension_semantics=sem, vmem_limit_bytes=limit)


def _tile(n, want):
    if n <= want:
        return n
    t = want - want % SUB
    while n % t:
        t -= SUB
    assert t > 0, (n, want)
    return t


def _full(shape):
    nd = len(shape)
    return pl.BlockSpec(shape, lambda *_: (0,) * nd)


def _sig(x):
    return jax.nn.sigmoid(x)


def _rms(x):
    r = lax.rsqrt(jnp.mean(x * x, axis=-1, keepdims=True) + EPS)
    return x * r, r


def _rms_bwd(dn, n, r):
    return r * (dn - n * jnp.mean(dn * n, axis=-1, keepdims=True))


def _l2_bwd(dn, n, r):
    return r * (dn - n * jnp.sum(dn * n, axis=-1, keepdims=True))


def _sum0(x):
    return jnp.sum(x, axis=0, keepdims=True)


def _shift_down(x, s, halo):
    ext = jnp.concatenate([halo, x], axis=0)
    return pltpu.roll(ext, s, 0)[SUB:, :]


def _shift_up(x, s, halo):
    t = x.shape[0]
    ext = jnp.concatenate([x, halo], axis=0)
    return pltpu.roll(ext, t + SUB - s, 0)[:t, :]


def _conv_fwd(x, w_ref, width, halo):
    sh = [x] + [_shift_down(x, s, halo) for s in range(1, width)]
    out = w_ref[width - 1:width, :] * sh[0]
    for s in range(1, width):
        out = out + w_ref[width - 1 - s:width - s, :] * sh[s]
    return out, sh


def _blockdiag_mean(n, group):
    r = lax.shift_right_logical(lax.broadcasted_iota(jnp.int32, (n, n), 0), int(math.log2(group)))
    c = lax.shift_right_logical(lax.broadcasted_iota(jnp.int32, (n, n), 1), int(math.log2(group)))
    return jnp.where(r == c, 1.0 / group, 0.0).astype(F32)


def _softplus(x):
    return jnp.maximum(x, 0.0) + jnp.log(1.0 + jnp.exp(-jnp.abs(x)))


def _my_place():
    return lax.axis_index("x"), lax.axis_index("y"), lax.axis_index("c")


def _all_gather(shards, name, in_vmem):
    nt = len(shards)

    def body(*refs):
        x_refs, out_refs = refs[:nt], refs[nt:2 * nt]
        send_sems, recv_sems, local_sems = refs[2 * nt:]
        x, y, c = _my_place()
        me, sibling = (x, y, c), (x, y, 1 - c)
        chips = [(1 - x, y), (x, 1 - y), (1 - x, 1 - y)]
        everything = []
        for t in range(nt):
            x_ref, out_ref = x_refs[t], out_refs[t]

            def blk(px, py, pc, out_ref=out_ref):
                return out_ref.at[4 * px + 2 * py + pc]

            def copy(k, block, to, src=None, t=t, blk=blk):
                return pltpu.make_async_remote_copy(
                    src_ref=blk(*block) if src is None else src, dst_ref=blk(*block),
                    send_sem=send_sems.at[7 * t + k], recv_sem=recv_sems.at[7 * t + k], device_id=to, device_id_type=MESH)

            mine = pltpu.make_async_copy(x_ref, blk(*me), local_sems.at[t])
            mine.start()
            first = [copy(0, me, sibling, src=x_ref)]
            first += [copy(1 + j, me, (*chip, c), src=x_ref) for j, chip in enumerate(chips)]
            for cp in first:
                cp.start()
            everything.append((copy, mine, first))
        sends = []
        for copy, mine, first in everything:
            passed = [copy(4 + j, (*chip, c), sibling) for j, chip in enumerate(chips)]
            for j, chip in enumerate(chips):
                copy(1 + j, (*chip, c), me).wait_recv()
                passed[j].start()
            sends += first + passed
        for copy, mine, first in everything:
            copy(0, sibling, me).wait_recv()
            for j, chip in enumerate(chips):
                copy(4 + j, (*chip, 1 - c), me).wait_recv()
        for cp in sends:
            cp.wait_send()
        for copy, mine, first in everything:
            mine.wait()

    space = pltpu.VMEM if in_vmem else pl.ANY
    return pl.pallas_call(
        body, name=name,
        out_shape=[jax.ShapeDtypeStruct((N_DEV,) + s.shape, s.dtype) for s in shards],
        in_specs=[pl.BlockSpec(memory_space=space)] * nt,
        out_specs=[pl.BlockSpec(memory_space=space)] * nt,
        scratch_shapes=[pltpu.SemaphoreType.DMA((7 * nt,)), pltpu.SemaphoreType.DMA((7 * nt,)),
                        pltpu.SemaphoreType.DMA((nt,))],
        compiler_params=pltpu.CompilerParams(vmem_limit_bytes=VMEM_LIMIT),
    )(*shards)


def _all_gather_async(shards, name, collective_id):
    nt = len(shards)
    hbm = pltpu.MemorySpace.HBM
    x_refs = [jax.new_ref(s, memory_space=hbm) for s in shards]
    out_refs = [jax.empty_ref(jax.ShapeDtypeStruct((N_DEV,) + s.shape, s.dtype), memory_space=hbm) for s in shards]

    @pl.kernel(mesh=plsc.ScalarSubcoreMesh(axis_name="sequencer", num_cores=1), name=name,
               scratch_types=(pltpu.SemaphoreType.DMA((7 * nt,)), pltpu.SemaphoreType.DMA((7 * nt,)),
                              pltpu.SemaphoreType.DMA((nt,))),
               compiler_params=pltpu.CompilerParams(collective_id=collective_id))
    def launch(send_sems, recv_sems, local_sems):
        x, y, c = _my_place()
        me, sibling = (x, y, c), (x, y, 1 - c)
        chips = [(1 - x, y), (x, 1 - y), (1 - x, 1 - y)]
        barrier = pltpu.get_barrier_semaphore()
        for peer in [sibling] + [(*chip, c) for chip in chips]:
            pl.semaphore_signal(barrier, inc=1, device_id=peer, device_id_type=MESH)
        pl.semaphore_wait(barrier, 4)
        everything = []
        for t in range(nt):
            x_ref, out_ref = x_refs[t], out_refs[t]

            def blk(px, py, pc, out_ref=out_ref):
                return out_ref.at[4 * px + 2 * py + pc]

            def copy(k, block, to, src=None, t=t, blk=blk):
                return pltpu.make_async_remote_copy(
                    src_ref=blk(*block) if src is None else src, dst_ref=blk(*block),
                    send_sem=send_sems.at[7 * t + k], recv_sem=recv_sems.at[7 * t + k], device_id=to, device_id_type=MESH)

            mine = pltpu.make_async_copy(x_ref, blk(*me), local_sems.at[t])
            mine.start()
            first = [copy(0, me, sibling, src=x_ref)]
            first += [copy(1 + j, me, (*chip, c), src=x_ref) for j, chip in enumerate(chips)]
            for cp in first:
                cp.start()
            everything.append((copy, mine, first))
        sends = []
        for copy, mine, first in everything:
            passed = [copy(4 + j, (*chip, c), sibling) for j, chip in enumerate(chips)]
            for j, chip in enumerate(chips):
                copy(1 + j, (*chip, c), me).wait_recv()
                passed[j].start()
            sends += first + passed
        for copy, mine, first in everything:
            copy(0, sibling, me).wait_recv()
            for j, chip in enumerate(chips):
                copy(4 + j, (*chip, 1 - c), me).wait_recv()
        for cp in sends:
            cp.wait_send()
        for copy, mine, first in everything:
            mine.wait()

    launch()
    return [r[...] for r in out_refs]


def _rs_exchange_async(srcs, name, collective_id):
    nt = len(srcs)
    hbm = pltpu.MemorySpace.HBM
    src_refs = [jax.new_ref(s, memory_space=hbm) for s in srcs]
    out_refs = [jax.empty_ref(jax.ShapeDtypeStruct(s.shape, s.dtype), memory_space=hbm) for s in srcs]
    flips = [(fx, fy, fc) for fx in (0, 1) for fy in (0, 1) for fc in (0, 1)][1:]

    @pl.kernel(mesh=plsc.ScalarSubcoreMesh(axis_name="sequencer", num_cores=1), name=name,
               scratch_types=(pltpu.SemaphoreType.DMA((7 * nt,)), pltpu.SemaphoreType.DMA((7 * nt,)),
                              pltpu.SemaphoreType.DMA((nt,))),
               compiler_params=pltpu.CompilerParams(collective_id=collective_id))
    def launch(send_sems, recv_sems, local_sems):
        x, y, c = _my_place()
        me = 4 * x + 2 * y + c
        peers = [(1 - x if fx else x, 1 - y if fy else y, 1 - c if fc else c) for fx, fy, fc in flips]
        barrier = pltpu.get_barrier_semaphore()
        for peer in peers:
            pl.semaphore_signal(barrier, inc=1, device_id=peer, device_id_type=MESH)
        pl.semaphore_wait(barrier, len(peers))
        own = [pltpu.make_async_copy(src_refs[t].at[me], out_refs[t].at[me], local_sems.at[t]) for t in range(nt)]
        copies = [pltpu.make_async_remote_copy(
            src_ref=src_refs[t].at[4 * px + 2 * py + pc], dst_ref=out_refs[t].at[me],
            send_sem=send_sems.at[7 * t + f], recv_sem=recv_sems.at[7 * t + f],
            device_id=(px, py, pc), device_id_type=MESH) for t in range(nt) for f, (px, py, pc) in enumerate(peers)]
        for cp in own + copies:
            cp.start()
        for cp in copies + own:
            cp.wait()

    launch()
    return [r[...] for r in out_refs]


def _rs_sum(recv, n_out, name):
    _, r, n = recv.shape
    tr = _tile(r, 512)

    def body(r_ref, o_ref):
        s = r_ref[0].astype(F32)
        for k in range(1, N_DEV):
            s = s + r_ref[k].astype(F32)
        o_ref[...] = s[:, :n_out]

    return pl.pallas_call(
        body, name=name, grid=(r // tr,),
        in_specs=[pl.BlockSpec((N_DEV, tr, n), lambda i: (0, i, 0))],
        out_specs=pl.BlockSpec((tr, n_out), lambda i: (i, 0)),
        out_shape=jax.ShapeDtypeStruct((r, n_out), F32), compiler_params=_params(1),
    )(recv)


def _shard_windows(n_shard, count, first=0):
    out = []
    for k in range(first, first + count):
        off = n_shard * k
        a, s = off // LANES, off % LANES
        out.append((a, s, -(-(s + n_shard) // LANES) * LANES))
    return out


def _fit_lanes(x, width):
    have = x.shape[1]
    if have < width:
        return jnp.concatenate([x, jnp.zeros((x.shape[0], width - have), x.dtype)], axis=-1)
    return x[:, :width]


def _interleave_cols(g, n_shard, w_out, name):
    nd, nl, rows, wpad = g.shape
    rb = _tile(rows, 256)
    wins = _shard_windows(n_shard, nd)

    def body(g_ref, o_ref, acc):
        acc[...] = jnp.zeros_like(acc)
        for k, (a, s, win) in enumerate(wins):
            xk = _fit_lanes(g_ref[k].astype(F32), win)
            if s:
                xk = pltpu.roll(xk, s, 1)
            acc[:, a * LANES:a * LANES + win] += xk
        o_ref[...] = acc[...].astype(o_ref.dtype)

    return pl.pallas_call(
        body, name=name, grid=(nl, rows // rb),
        in_specs=[pl.BlockSpec((nd, None, rb, wpad), lambda l, i: (0, l, i, 0))],
        out_specs=pl.BlockSpec((None, rb, w_out), lambda l, i: (l, i, 0)),
        out_shape=jax.ShapeDtypeStruct((nl, rows, w_out), g.dtype),
        scratch_shapes=[pltpu.VMEM((rb, w_out), F32)],
        compiler_params=_params(2),
    )(g)


def _sum_devices(g):
    _, r, n = g.shape

    def body(g_ref, o_ref):
        s = g_ref[0]
        for t in range(1, N_DEV):
            s = s + g_ref[t]
        o_ref[...] = s

    return pl.pallas_call(
        body, name="sum_devices", out_shape=jax.ShapeDtypeStruct((r, n), F32),
        in_specs=[pl.BlockSpec(memory_space=pltpu.VMEM)], out_specs=pl.BlockSpec(memory_space=pltpu.VMEM),
        compiler_params=pltpu.CompilerParams(vmem_limit_bytes=VMEM_LIMIT),
    )(g)


def _mod_fwd(c_all, ada_w, ada_b_cols):
    nl, _, nc = ada_w.shape

    def body(c_ref, w_ref, b_ref, o_ref):
        cv = c_ref[...]
        act = (cv * _sig(cv)).astype(MXU)
        o_ref[...] = _dot(act, w_ref[...].astype(MXU), NN) + b_ref[...]

    return pl.pallas_call(
        body, name="mod_fwd", grid=(nl,),
        in_specs=[_full((16, D)), pl.BlockSpec((None, D, nc), lambda i: (i, 0, 0)),
                  pl.BlockSpec((None, 1, nc), lambda i: (i, 0, 0))],
        out_specs=pl.BlockSpec((None, 16, nc), lambda i: (i, 0, 0)),
        out_shape=jax.ShapeDtypeStruct((nl, 16, nc), F32), compiler_params=_params(1),
    )(c_all, ada_w, ada_b_cols)


def _mod_bwd(c_all, dmod_cols):
    nl, _, nc = dmod_cols.shape

    def body(c_ref, d_ref, o_ref):
        cv = c_ref[...]
        act = (cv * _sig(cv)).astype(MXU)
        o_ref[...] = _dot(act, d_ref[...].astype(MXU), TN)

    return pl.pallas_call(
        body, name="mod_bwd", grid=(nl,),
        in_specs=[_full((16, D)), pl.BlockSpec((None, 16, nc), lambda i: (i, 0, 0))],
        out_specs=pl.BlockSpec((None, D, nc), lambda i: (i, 0, 0)),
        out_shape=jax.ShapeDtypeStruct((nl, D, nc), F32), compiler_params=_params(1),
    )(c_all, dmod_cols)


def _gate_small(s, sp_ref):
    lane = lax.broadcasted_iota(jnp.int32, s.shape, 1)
    a = -jnp.exp(sp_ref[0:1, :])
    xb = s + sp_ref[1:2, :]
    beta = _sig(s)
    g = a * _softplus(xb)
    return lane, a, xb, beta, g


def _in_pre_fwd(x, modrows, vec, w_in, pa, cq, sp):
    L = x.shape[0]
    T = _tile(L, 512)
    scale = HD ** -0.5
    w3 = 3 * AW + 3 * H * HD

    def body(x_ref, mod_ref, vec_ref, w_ref, pa_ref, cq_ref, sp_ref,
             p_ref, h_ref, qn_ref, kn_ref, vs_ref, gb_ref, ya_ref, cu_ref, qc_ref, u_carry, q_carry):
        @pl.when(pl.program_id(0) == 0)
        def _():
            u_carry[...] = jnp.zeros_like(u_carry)
            q_carry[...] = jnp.zeros_like(q_carry)

        n, _ = _rms(x_ref[...])
        hb = (n * vec_ref[0:1, :] * (1.0 + mod_ref[1:2, :]) + mod_ref[0:1, :]).astype(MXU)
        h_ref[...] = hb
        pm_a = _dot(hb, w_ref[:, 0:3 * AW], NN)
        p_ref[:, 0:3 * AW] = pm_a
        pm_q = _dot(hb, w_ref[:, 3 * AW:w3], NN)
        p_ref[:, 3 * AW:w3] = pm_q

        a_b = pm_a[:, 0:AW]
        u = pm_a[:, AW:2 * AW] * pm_a[:, 2 * AW:3 * AW]
        cu, _ = _conv_fwd(u, pa_ref, 3, u_carry[...])
        cu_ref[...] = cu.astype(MXU)
        u_carry[...] = u[T - SUB:T, :]
        yp = a_b * cu
        ms = _dot_f32(yp * yp, _blockdiag_mean(AW, A_GROUP), NN, exact="b")
        ya_ref[...] = (yp * lax.rsqrt(ms + EPS) * pa_ref[3:4, :]).astype(MXU)

        pm_z = _dot(hb, w_ref[:, w3:P_PAD], NN)
        p_ref[:, w3:P_PAD] = pm_z
        qkv = pm_q
        qc, _ = _conv_fwd(qkv, cq_ref, 4, q_carry[...])
        qc_ref[...] = qc.astype(MXU)
        q_carry[...] = qkv[T - SUB:T, :]
        qs = qc * _sig(qc)
        for h in range(H):
            q = qs[:, h * HD:(h + 1) * HD]
            qn_ref[:, h * HD:(h + 1) * HD] = q * (lax.rsqrt(jnp.sum(q * q, axis=-1, keepdims=True) + EPS) * scale)
            k = qs[:, (H + h) * HD:(H + h + 1) * HD]
            kn_ref[:, h * HD:(h + 1) * HD] = k * lax.rsqrt(jnp.sum(k * k, axis=-1, keepdims=True) + EPS)
        vs_ref[...] = qs[:, 2 * H * HD:3 * H * HD]

        lane, _, _, beta, g = _gate_small(pm_z[:, H * HD:H * HD + LANES], sp_ref)
        gb_ref[...] = jnp.where(lane < H, beta, jnp.where(lane < 2 * H, g, 0.0))

    row = lambda i: (i, 0)
    return pl.pallas_call(
        body, name="in_pre_fwd", grid=(L // T,),
        in_specs=[pl.BlockSpec((T, D), row), _full((SUB, D)), _full((SUB, D)), _full((D, P_PAD)),
                  _full((SUB, AW)), _full((SUB, 3 * H * HD)), _full((SUB, LANES))],
        out_specs=[pl.BlockSpec((T, P_PAD), row), pl.BlockSpec((T, D), row)]
        + [pl.BlockSpec((T, H * HD), row)] * 3 + [pl.BlockSpec((T, LANES), row), pl.BlockSpec((T, AW), row),
                                                  pl.BlockSpec((T, AW), row), pl.BlockSpec((T, 3 * H * HD), row)],
        out_shape=[jax.ShapeDtypeStruct((L, P_PAD), F32), jax.ShapeDtypeStruct((L, D), MXU)]
        + [jax.ShapeDtypeStruct((L, H * HD), F32)] * 3
        + [jax.ShapeDtypeStruct((L, LANES), F32), jax.ShapeDtypeStruct((L, AW), MXU),
           jax.ShapeDtypeStruct((L, AW), MXU), jax.ShapeDtypeStruct((L, 3 * H * HD), MXU)],
        scratch_shapes=[pltpu.VMEM((SUB, AW), F32), pltpu.VMEM((SUB, 3 * H * HD), F32)],
        compiler_params=_params(1),
    )(x, modrows, vec, w_in, pa, cq, sp)


def _gdr_masks():
    r = lax.broadcasted_iota(jnp.int32, (CK, CK), 0)
    c = lax.broadcasted_iota(jnp.int32, (CK, CK), 1)
    return r >= c, r > c


def _head_cols(gbt, h):
    return gbt[:, h:h + 1], gbt[:, H + h:H + h + 1]


def _split(x, parts):
    out = []
    for _ in range(parts):
        hi = x.astype(jnp.bfloat16)
        out.append(hi)
        x = x - hi.astype(F32)
    return out


def _dot_f32(a, b, dims, exact=None):
    if exact == "a":
        ab = a.astype(jnp.bfloat16)
        return sum(_dot(ab, t, dims) for t in _split(b, 3))
    if exact == "b":
        bb = b.astype(jnp.bfloat16)
        return sum(_dot(t, bb, dims) for t in _split(a, 3))
    ah, al = _split(a, 2)
    bh, bl = _split(b, 2)
    return _dot(ah, bh, dims) + _dot(ah, bl, dims) + _dot(al, bh, dims)


def _gdr_consts():
    causal, strict = _gdr_masks()
    return dict(causal=causal, strict=strict, tril=jnp.where(causal, 1.0, 0.0).astype(F32),
                eye=jnp.where(causal & jnp.logical_not(strict), 1.0, 0.0).astype(F32),
                bcast=jnp.full((CK, HD), 1.0 / HD, F32))


def _dots(a, b, dims):
    return [_dot(x, y, dims) for x, y in zip(a, b)]


def _dots_f32(a, b, dims, exact=None):
    n = len(a)
    if exact == "a":
        lhs = [[x.astype(jnp.bfloat16)] * 3 for x in a]
        rhs = [_split(y, 3) for y in b]
    elif exact == "b":
        lhs = [_split(x, 3) for x in a]
        rhs = [[y.astype(jnp.bfloat16)] * 3 for y in b]
    else:
        sa = [_split(x, 2) for x in a]
        sb = [_split(y, 2) for y in b]
        lhs = [[s[0], s[0], s[1]] for s in sa]
        rhs = [[s[0], s[1], s[0]] for s in sb]
    terms = [[_dot(lhs[i][t], rhs[i][t], dims) for i in range(n)] for t in range(3)]
    return [terms[0][i] + terms[1][i] + terms[2][i] for i in range(n)]


def _gdr_local(q, k, v, beta, g, cst, tinv=None):
    n = len(q)
    R = range(n)
    causal, strict = cst["causal"], cst["strict"]
    gc = _dots_f32([cst["tril"]] * n, [jnp.broadcast_to(g[i], (CK, HD)) for i in R], NN, exact="a")
    g_row = _dots_f32([cst["bcast"]] * n, gc, NT, exact="a")
    decay = [jnp.where(causal, jnp.exp(jnp.where(causal, gc[i][:, 0:CK] - g_row[i], 0.0)), 0.0) for i in R]
    eg = [jnp.exp(gc[i]) for i in R]
    gl = [gc[i][CK - 1:CK, :] for i in R]
    ek = [jnp.exp(gl[i] - gc[i]) for i in R]
    cd = [jnp.exp(gl[i]) for i in R]
    kb = [k[i] * beta[i] for i in R]
    pk = _dots(kb, k, NT)
    if tinv is None:
        xp = [-jnp.where(strict, pk[i] * decay[i], 0.0) for i in R]
        tinv = [cst["eye"] + xp[i] for i in R]
        for _ in range(5):
            xp = _dots_f32(xp, xp, NN)
            tx = _dots_f32(tinv, xp, NN)
            tinv = [tinv[i] + tx[i] for i in R]
    u = _dots(tinv, [v[i] * beta[i] for i in R], NN)
    w = _dots(tinv, [kb[i] * eg[i] for i in R], NN)
    qk = _dots(q, k, NT)
    intra = [jnp.where(causal, qk[i] * decay[i], 0.0) for i in R]
    return dict(decay=decay, eg=eg, ek=ek, cd=cd, kb=kb, pk=pk, tinv=tinv, u=u, w=w, qk=qk, intra=intra,
                q_dec=[q[i] * eg[i] for i in R], k_dec=[k[i] * ek[i] for i in R])


GDR_SUB = 8


def _gdr_fwd(qn, kn, vs, gb):
    L = qn.shape[0]
    nc = L // CK
    cb = min(8, nc)
    rb = cb * CK
    nb = nc // cb
    nsub = GDR_SUB if cb % GDR_SUB == 0 else 1

    def body(q_ref, k_ref, v_ref, gb_ref, o_ref, st_ref, ti_ref, s_ref):
        @pl.when(pl.program_id(0) == 0)
        def _():
            s_ref[...] = jnp.zeros_like(s_ref)

        cst = _gdr_consts()
        heads = range(H)

        def group(gi, carry):
            rows = [pl.ds(pl.multiple_of((gi * nsub + j) * CK, CK), CK) for j in range(nsub)]
            chains = [(j, h) for j in range(nsub) for h in heads]
            gbt = [gb_ref[rows[j], :] for j in range(nsub)]
            cols = lambda h: slice(h * HD, (h + 1) * HD)
            t = _gdr_local([q_ref[rows[j], cols(h)] for j, h in chains], [k_ref[rows[j], cols(h)] for j, h in chains],
                           [v_ref[rows[j], cols(h)] for j, h in chains],
                           [_head_cols(gbt[j], h)[0] for j, h in chains], [_head_cols(gbt[j], h)[1] for j, h in chains], cst)
            s = [s_ref[h] for h in heads]
            for j in range(nsub):
                at = lambda key: [t[key][j * H + h] for h in heads]
                for h in heads:
                    st_ref[h, gi * nsub + j] = s[h]
                    ti_ref[h, gi * nsub + j] = t["tinv"][j * H + h]
                ws = _dots(at("w"), s, NN)
                v_new = [u_h - ws_h for u_h, ws_h in zip(at("u"), ws)]
                o_s = _dots(at("q_dec"), s, NN)
                o_v = _dots(at("intra"), v_new, NN)
                kv = _dots(at("k_dec"), v_new, TN)
                cd = at("cd")
                for h in heads:
                    o_ref[rows[j], cols(h)] = o_s[h] + o_v[h]
                s = [s[h] * cd[h] + kv[h] for h in heads]
            for h in heads:
                s_ref[h] = s[h]
            return carry

        lax.fori_loop(0, cb // nsub, group, 0)

    blk = pl.BlockSpec((rb, H * HD), lambda b: (b, 0))
    return pl.pallas_call(
        body, name="gdr_fwd", grid=(nb,),
        in_specs=[blk, blk, blk, pl.BlockSpec((rb, LANES), lambda b: (b, 0))],
        out_specs=[blk, pl.BlockSpec((H, cb, HD, HD), lambda b: (0, b, 0, 0)),
                   pl.BlockSpec((H, cb, CK, CK), lambda b: (0, b, 0, 0))],
        out_shape=[jax.ShapeDtypeStruct((L, H * HD), F32), jax.ShapeDtypeStruct((H, nc, HD, HD), F32),
                   jax.ShapeDtypeStruct((H, nc, CK, CK), F32)],
        scratch_shapes=[pltpu.VMEM((H, HD, HD), F32)],
        compiler_params=_params(1),
    )(qn, kn, vs, gb)


def _gdr_bwd(qn, kn, vs, gb, states, tinvs, do):
    L = qn.shape[0]
    nc = L // CK
    cb = min(8, nc)
    rb = cb * CK
    nb = nc // cb
    nsub = GDR_SUB if cb % GDR_SUB == 0 else 1

    def body(q_ref, k_ref, v_ref, gb_ref, st_ref, ti_ref, do_ref, dq_ref, dk_ref, dv_ref, dgb_ref, ds_ref):
        @pl.when(pl.program_id(0) == 0)
        def _():
            ds_ref[...] = jnp.zeros_like(ds_ref)

        cst = _gdr_consts()
        causal, strict = cst["causal"], cst["strict"]
        ones = jnp.ones((CK, HD), F32)
        row = lax.broadcasted_iota(jnp.int32, (CK, HD), 0)
        lane = lax.broadcasted_iota(jnp.int32, (CK, LANES), 1)

        heads = range(H)
        rsum = lambda x: jnp.sum(x, axis=-1, keepdims=True)

        def group(gj, carry):
            gi = cb // nsub - 1 - gj
            rows = [pl.ds(pl.multiple_of((gi * nsub + j) * CK, CK), CK) for j in range(nsub)]
            chains = [(j, h) for j in range(nsub) for h in heads]
            gbt = [gb_ref[rows[j], :] for j in range(nsub)]
            cols = lambda h: slice(h * HD, (h + 1) * HD)
            q_all = [q_ref[rows[j], cols(h)] for j, h in chains]
            k_all = [k_ref[rows[j], cols(h)] for j, h in chains]
            v_all = [v_ref[rows[j], cols(h)] for j, h in chains]
            beta_all = [_head_cols(gbt[j], h)[0] for j, h in chains]
            t = _gdr_local(q_all, k_all, v_all, beta_all, [_head_cols(gbt[j], h)[1] for j, h in chains], cst,
                           tinv=[ti_ref[h, gi * nsub + j] for j, h in chains])
            ds_out = [ds_ref[h] for h in heads]
            for j in reversed(range(nsub)):
                at = lambda key: [t[key][j * H + h] for h in heads]
                pick = lambda lst: [lst[j * H + h] for h in heads]
                q, k, v, beta = pick(q_all), pick(k_all), pick(v_all), pick(beta_all)
                u, w, tinv, decay = at("u"), at("w"), at("tinv"), at("decay")
                eg, ek, cd, kb = at("eg"), at("ek"), at("cd"), at("kb")
                q_dec, k_dec, intra, pk, qk = at("q_dec"), at("k_dec"), at("intra"), at("pk"), at("qk")
                s = [st_ref[h, gi * nsub + j] for h in heads]
                dout = [do_ref[rows[j], cols(h)] for h in heads]

                ws = _dots(w, s, NN)
                v_new = [u[h] - ws[h] for h in heads]
                dq_dec = _dots(dout, s, NT)
                qd = _dots(q_dec, dout, TN)
                di = _dots(dout, v_new, NT)
                dintra = [jnp.where(causal, di[h], 0.0) for h in heads]
                ido = _dots(intra, dout, TN)
                kds = _dots(k_dec, ds_out, NN)
                dv_new = [ido[h] + kds[h] for h in heads]
                dk_dec = _dots(v_new, ds_out, NT)
                dcd = [jnp.sum(jnp.sum(ds_out[h] * s[h], axis=1, keepdims=True), axis=0, keepdims=True) for h in heads]
                dvs = _dots(dv_new, s, NT)
                dw = [-dvs[h] for h in heads]
                wdv = _dots(w, dv_new, TN)
                ds_new = [qd[h] + ds_out[h] * cd[h] - wdv[h] for h in heads]
                dru = _dots(tinv, dv_new, TN)
                drw = _dots(tinv, dw, TN)
                dl1 = _dots(dru, u, NT)
                dl2 = _dots(drw, w, NT)
                dlower = [-jnp.where(strict, dl1[h] + dl2[h], 0.0) for h in heads]
                dv = [dru[h] * beta[h] for h in heads]
                dbeta = [rsum(dru[h] * v[h]) for h in heads]
                dgc = [rsum(drw[h] * kb[h]) * eg[h] for h in heads]
                dpk = [dlower[h] * decay[h] for h in heads]
                dqk = [dintra[h] * decay[h] for h in heads]
                dpk_k = _dots(dpk, k, NN)
                dkb = [drw[h] * eg[h] + dpk_k[h] for h in heads]
                dk1 = _dots(dpk, kb, TN)
                dq1 = _dots(dqk, k, NN)
                dk2 = _dots(dqk, q, TN)
                m = [(dlower[h] * pk[h] + dintra[h] * qk[h]) * decay[h] for h in heads]
                mcol = _dots_f32(m, [ones] * H, TN, exact="b")
                e = [rsum(dk_dec[h] * k_dec[h]) for h in heads]
                dgl = [jnp.sum(e[h], axis=0, keepdims=True) + dcd[h] * cd[h] for h in heads]
                dgc = [dgc[h] + rsum(m[h]) - mcol[h] + rsum(dq_dec[h] * q_dec[h]) - e[h]
                       + jnp.where(row == CK - 1, dgl[h], 0.0) for h in heads]
                dg = _dots_f32([cst["tril"]] * H, dgc, TN, exact="a")
                dgb = jnp.zeros((CK, LANES), F32)
                for h in heads:
                    dq_ref[rows[j], cols(h)] = dq1[h] + dq_dec[h] * eg[h]
                    dk_ref[rows[j], cols(h)] = dk1[h] + dk2[h] + dk_dec[h] * ek[h] + dkb[h] * beta[h]
                    dv_ref[rows[j], cols(h)] = dv[h]
                    db = dbeta[h] + rsum(dkb[h] * k[h])
                    dgb = dgb + jnp.where(lane == h, db, 0.0) + jnp.where(lane == H + h, dg[h], 0.0)
                dgb_ref[rows[j], :] = dgb
                ds_out = ds_new
            for h in heads:
                ds_ref[h] = ds_out[h]
            return carry

        lax.fori_loop(0, cb // nsub, group, 0)

    blk = pl.BlockSpec((rb, H * HD), lambda b: (nb - 1 - b, 0))
    sblk = pl.BlockSpec((rb, LANES), lambda b: (nb - 1 - b, 0))
    return pl.pallas_call(
        body, name="gdr_bwd", grid=(nb,),
        in_specs=[blk, blk, blk, sblk, pl.BlockSpec((H, cb, HD, HD), lambda b: (0, nb - 1 - b, 0, 0)),
                  pl.BlockSpec((H, cb, CK, CK), lambda b: (0, nb - 1 - b, 0, 0)), blk],
        out_specs=[blk, blk, blk, sblk],
        out_shape=[jax.ShapeDtypeStruct((L, H * HD), F32)] * 3 + [jax.ShapeDtypeStruct((L, LANES), F32)],
        scratch_shapes=[pltpu.VMEM((H, HD, HD), F32)],
        compiler_params=_params(1),
    )(qn, kn, vs, gb, states, tinvs, do)


def _post_fwd(o, p, ya, x, modrows, sp, w_out):
    L = x.shape[0]
    T = _tile(L, 512)

    def body(o_ref, z_ref, ya_ref, x_ref, mod_ref, sp_ref, w_ref, y_ref, x2_ref, yb_ref):
        ndw = sp_ref[2:3, :]
        z = z_ref[...]
        sz = z * _sig(z)
        parts = []
        for h in range(H):
            n, _ = _rms(o_ref[:, h * HD:(h + 1) * HD])
            parts.append(n * ndw * sz[:, h * HD:(h + 1) * HD])
        yb = jnp.concatenate(parts, axis=-1).astype(MXU)
        yb_ref[...] = yb
        y = _dot(ya_ref[...], w_ref[0:AW, :], NN) + _dot(yb, w_ref[AW:2 * AW, :], NN)
        y_ref[...] = y
        x2_ref[...] = x_ref[...] + mod_ref[2:3, :] * y

    row = lambda i: (i, 0)
    zcol = (3 * AW + 3 * H * HD) // (H * HD)
    return pl.pallas_call(
        body, name="post_fwd", grid=(L // T,),
        in_specs=[pl.BlockSpec((T, H * HD), row), pl.BlockSpec((T, H * HD), lambda i: (i, zcol)),
                  pl.BlockSpec((T, AW), row), pl.BlockSpec((T, D), row), _full((SUB, D)), _full((SUB, LANES)),
                  _full((D, D))],
        out_specs=[pl.BlockSpec((T, D), row), pl.BlockSpec((T, D), row), pl.BlockSpec((T, H * HD), row)],
        out_shape=[jax.ShapeDtypeStruct((L, D), F32), jax.ShapeDtypeStruct((L, D), F32),
                   jax.ShapeDtypeStruct((L, H * HD), MXU)],
        compiler_params=_params(1),
    )(o, p, ya, x, modrows, sp, w_out)


FF_COLS = 2
FF_CW = DFF // FF_COLS
FF_ROWS = 512


def _ffn_fwd_half(x2, modrows, vec, w_up, cff, w_down, j, d_prev):
    assert FF_COLS == 2
    L = x2.shape[0]
    T = _tile(L, FF_ROWS)
    nj = FF_COLS
    last = d_prev is not None

    def body(*refs):
        x_ref, mod_ref, vec_ref, wg_ref, wu_ref, cg_ref, cu_ref, wd_ref = refs[:8]
        if last:
            dp_ref, gp_ref, up_ref, gc_ref, uc_ref, f_ref, d_ref, x3_ref, carry_g, carry_u = refs[8:]
        else:
            h_ref, gp_ref, up_ref, gc_ref, uc_ref, f_ref, d_ref, carry_g, carry_u = refs[8:]

        @pl.when(pl.program_id(0) == 0)
        def _():
            carry_g[...] = jnp.zeros_like(carry_g)
            carry_u[...] = jnp.zeros_like(carry_u)

        xv = x_ref[...]
        n, _ = _rms(xv)
        hb = (n * vec_ref[1:2, :] * (1.0 + mod_ref[4:5, :]) + mod_ref[3:4, :]).astype(MXU)
        if not last:
            h_ref[...] = hb
        g = _dot(hb, wg_ref[...], NN)
        u = _dot(hb, wu_ref[...], NN)
        gp_ref[...] = g.astype(MXU)
        up_ref[...] = u.astype(MXU)
        gc, _ = _conv_fwd(g, cg_ref, 3, carry_g[...])
        uc, _ = _conv_fwd(u, cu_ref, 3, carry_u[...])
        carry_g[...] = g[T - SUB:T, :]
        carry_u[...] = u[T - SUB:T, :]
        gc_ref[...] = gc.astype(MXU)
        uc_ref[...] = uc.astype(MXU)
        fb = (gc * _sig(gc) * uc).astype(MXU)
        f_ref[...] = fb
        part = _dot(fb, wd_ref[...], NN)
        if last:
            dv = dp_ref[...] + part
            d_ref[...] = dv
            x3_ref[...] = xv + mod_ref[5:6, :] * dv
        else:
            d_ref[...] = part

    row = lambda i: (i, 0)
    rowD = pl.BlockSpec((T, D), row)
    rowC = pl.BlockSpec((T, FF_CW), row)
    in_specs = [rowD, _full((SUB, D)), _full((SUB, D)),
                pl.BlockSpec((D, FF_CW), lambda i: (0, j)), pl.BlockSpec((D, FF_CW), lambda i: (0, nj + j)),
                pl.BlockSpec((SUB, FF_CW), lambda i: (0, j)), pl.BlockSpec((SUB, FF_CW), lambda i: (0, nj + j)),
                pl.BlockSpec((FF_CW, D), lambda i: (j, 0))]
    half = [jax.ShapeDtypeStruct((L, FF_CW), MXU)] * 5
    args = [x2, modrows, vec, w_up, w_up, cff, cff, w_down]
    if last:
        in_specs.append(rowD)
        args.append(d_prev)
        out_specs = [rowC] * 5 + [rowD, rowD]
        out_shape = half + [jax.ShapeDtypeStruct((L, D), F32), jax.ShapeDtypeStruct((L, D), F32)]
    else:
        out_specs = [rowD] + [rowC] * 5 + [rowD]
        out_shape = [jax.ShapeDtypeStruct((L, D), MXU)] + half + [jax.ShapeDtypeStruct((L, D), F32)]
    return pl.pallas_call(
        body, name="ffn_fwd_last" if last else "ffn_fwd_first", grid=(L // T,),
        in_specs=in_specs, out_specs=out_specs, out_shape=out_shape,
        scratch_shapes=[pltpu.VMEM((SUB, FF_CW), F32), pltpu.VMEM((SUB, FF_CW), F32)],
        compiler_params=_params(1),
    )(*args)


def _ffn_bwd_half(dx3, modrows, gpre, upre, gcv, ucv, cff, w_down, w_up, j, tail):
    assert FF_COLS == 2
    L = dx3.shape[0]
    T = _tile(L, FF_ROWS)
    ni, nj = L // T, FF_COLS
    last = tail is not None

    def body(*refs):
        dx3_ref, mod_ref, gp_ref, up_ref, gc_ref, uc_ref, cg_ref, cu_ref, wd_ref, wg_ref, wu_ref = refs[:11]
        if last:
            (d_ref, x2_ref, vec_ref, dhp_ref, dgp_ref, dup_ref, dx2_ref, accv_ref, dcg_ref, dcu_ref,
             carry_g, carry_u) = refs[11:]
        else:
            dd_ref, dgp_ref, dup_ref, dh_ref, dcg_ref, dcu_ref, carry_g, carry_u = refs[11:]
        i = pl.program_id(0)

        @pl.when(i == 0)
        def _():
            carry_g[...] = jnp.zeros_like(carry_g)
            carry_u[...] = jnp.zeros_like(carry_u)
            dcg_ref[...] = jnp.zeros_like(dcg_ref)
            dcu_ref[...] = jnp.zeros_like(dcu_ref)
            if last:
                accv_ref[...] = jnp.zeros_like(accv_ref)

        dx3v = dx3_ref[...]
        ddb = (mod_ref[5:6, :] * dx3v).astype(MXU)
        if not last:
            dd_ref[...] = ddb
        g, u = gp_ref[...].astype(F32), up_ref[...].astype(F32)
        gc, uc = gc_ref[...].astype(F32), uc_ref[...].astype(F32)
        sg = _sig(gc)
        df = _dot(ddb, wd_ref[...], NT)
        duc = df * (gc * sg)
        dgc = df * uc * (sg * (1.0 + gc * (1.0 - sg)))
        dgs = [dgc] + [_shift_up(dgc, s, carry_g[...]) for s in (1, 2)]
        dus = [duc] + [_shift_up(duc, s, carry_u[...]) for s in (1, 2)]
        for s in range(3):
            dcg_ref[2 - s:3 - s, :] += _sum0(dgs[s] * g)
            dcu_ref[2 - s:3 - s, :] += _sum0(dus[s] * u)
        dg = (cg_ref[2:3, :] * dgs[0] + cg_ref[1:2, :] * dgs[1] + cg_ref[0:1, :] * dgs[2]).astype(MXU)
        du = (cu_ref[2:3, :] * dus[0] + cu_ref[1:2, :] * dus[1] + cu_ref[0:1, :] * dus[2]).astype(MXU)
        carry_g[...] = dgc[0:SUB, :]
        carry_u[...] = duc[0:SUB, :]
        dgp_ref[...] = dg
        dup_ref[...] = du
        dh = _dot(dg, wg_ref[...], NT) + _dot(du, wu_ref[...], NT)
        if last:
            dh = dh + dhp_ref[...]
            accv_ref[0:1, :] += _sum0(dx3v * d_ref[...])
            n, r = _rms(x2_ref[...])
            nw, sc = vec_ref[1:2, :], mod_ref[4:5, :]
            accv_ref[1:2, :] += _sum0(dh)
            accv_ref[2:3, :] += _sum0(dh * n * nw)
            accv_ref[3:4, :] += _sum0(dh * n * (1.0 + sc))
            dx2_ref[...] = _rms_bwd(dh * nw * (1.0 + sc), n, r) + dx3v
        else:
            dh_ref[...] = dh

    row = lambda i: (ni - 1 - i, 0)
    rowD = pl.BlockSpec((T, D), row)
    rowC = pl.BlockSpec((T, FF_CW), row)
    in_specs = [rowD, _full((SUB, D)), rowC, rowC, rowC, rowC,
                pl.BlockSpec((SUB, FF_CW), lambda i: (0, j)), pl.BlockSpec((SUB, FF_CW), lambda i: (0, nj + j)),
                pl.BlockSpec((FF_CW, D), lambda i: (j, 0)),
                pl.BlockSpec((D, FF_CW), lambda i: (0, j)), pl.BlockSpec((D, FF_CW), lambda i: (0, nj + j))]
    args = [dx3, modrows, gpre, upre, gcv, ucv, cff, cff, w_down, w_up, w_up]
    halfb = [jax.ShapeDtypeStruct((L, FF_CW), MXU), jax.ShapeDtypeStruct((L, FF_CW), MXU)]
    dconv = [jax.ShapeDtypeStruct((SUB, FF_CW), F32)] * 2
    if last:
        d, x2, vec, dh_prev = tail
        in_specs += [rowD, rowD, _full((SUB, D)), rowD]
        args += [d, x2, vec, dh_prev]
        out_specs = [rowC, rowC, rowD, _full((SUB, D)), _full((SUB, FF_CW)), _full((SUB, FF_CW))]
        out_shape = halfb + [jax.ShapeDtypeStruct((L, D), F32), jax.ShapeDtypeStruct((SUB, D), F32)] + dconv
    else:
        out_specs = [rowD, rowC, rowC, rowD, _full((SUB, FF_CW)), _full((SUB, FF_CW))]
        out_shape = [jax.ShapeDtypeStruct((L, D), MXU)] + halfb + [jax.ShapeDtypeStruct((L, D), F32)] + dconv
    return pl.pallas_call(
        body, name="ffn_bwd_last" if last else "ffn_bwd_first", grid=(ni,),
        in_specs=in_specs, out_specs=out_specs, out_shape=out_shape,
        scratch_shapes=[pltpu.VMEM((SUB, FF_CW), F32), pltpu.VMEM((SUB, FF_CW), F32)],
        compiler_params=_params(1),
    )(*args)


def _final(x, target, nf):
    L = x.shape[0]
    T = _tile(L, 256)

    def body(x_ref, t_ref, nf_ref, dx_ref, acc_ref):
        @pl.when(pl.program_id(0) == 0)
        def _():
            acc_ref[...] = jnp.zeros_like(acc_ref)

        n, r = _rms(x_ref[...])
        w = nf_ref[0:1, :]
        err = n * w - t_ref[...]
        acc_ref[0:1, :] += (0.5 / D) * _sum0(err * err)
        dy = err * (1.0 / D)
        acc_ref[1:2, :] += _sum0(dy * n)
        dx_ref[...] = _rms_bwd(dy * w, n, r)

    row = lambda i: (i, 0)
    return pl.pallas_call(
        body, name="final_norm_loss", grid=(L // T,),
        in_specs=[pl.BlockSpec((T, D), row), pl.BlockSpec((T, D), row), _full((SUB, D))],
        out_specs=[pl.BlockSpec((T, D), row), _full((SUB, D))],
        out_shape=[jax.ShapeDtypeStruct((L, D), F32), jax.ShapeDtypeStruct((SUB, D), F32)],
        compiler_params=_params(1),
    )(x, target, nf)


def _post_bwd(dx2, y, o, p, modrows, sp, w_out):
    L = dx2.shape[0]
    T = _tile(L, 512)

    def body(dx2_ref, y_ref, o_ref, z_ref, mod_ref, sp_ref, w_ref, dy_ref, do_ref, dz_ref, dya_ref, accv_ref, accs_ref):
        @pl.when(pl.program_id(0) == 0)
        def _():
            accv_ref[...] = jnp.zeros_like(accv_ref)
            accs_ref[...] = jnp.zeros_like(accs_ref)

        dx2v = dx2_ref[...]
        accv_ref[0:1, :] += _sum0(dx2v * y_ref[...])
        dyb = (mod_ref[2:3, :] * dx2v).astype(MXU)
        dy_ref[...] = dyb
        dyc = _dot(dyb, w_ref[...], NT)
        dya_ref[...] = dyc[:, 0:AW]
        ndw = sp_ref[2:3, :]
        z = z_ref[...]
        sgz = _sig(z)
        dsz = sgz * (1.0 + z * (1.0 - sgz))
        dndw = jnp.zeros((1, HD), F32)
        for h in range(H):
            sl = slice(h * HD, (h + 1) * HD)
            n, r = _rms(o_ref[:, sl])
            dyh = dyc[:, AW + h * HD:AW + (h + 1) * HD]
            zh = z[:, sl]
            don = dyh * (zh * sgz[:, sl])
            dz_ref[:, sl] = dyh * (n * ndw) * dsz[:, sl]
            dndw = dndw + _sum0(don * n)
            do_ref[:, sl] = _rms_bwd(don * ndw, n, r)
        accs_ref[0:1, :] += dndw

    row = lambda i: (i, 0)
    zcol = (3 * AW + 3 * H * HD) // (H * HD)
    return pl.pallas_call(
        body, name="post_bwd", grid=(L // T,),
        in_specs=[pl.BlockSpec((T, D), row), pl.BlockSpec((T, D), row), pl.BlockSpec((T, H * HD), row),
                  pl.BlockSpec((T, H * HD), lambda i: (i, zcol)), _full((SUB, D)), _full((SUB, LANES)), _full((D, D))],
        out_specs=[pl.BlockSpec((T, D), row)] + [pl.BlockSpec((T, H * HD), row)] * 3 + [_full((SUB, D)), _full((SUB, LANES))],
        out_shape=[jax.ShapeDtypeStruct((L, D), MXU)] + [jax.ShapeDtypeStruct((L, H * HD), F32)] * 3
        + [jax.ShapeDtypeStruct((SUB, D), F32), jax.ShapeDtypeStruct((SUB, LANES), F32)],
        compiler_params=_params(1),
    )(dx2, y, o, p, modrows, sp, w_out)


def _pre_in_bwd(p, cub, qcb, dqn, dkn, dvs, dya, dz, dgb, pa, cq, sp, w_in, x, dx2, modrows, vec):
    L = p.shape[0]
    T = _tile(L, 256)
    ni = L // T
    scale = HD ** -0.5
    w3 = 3 * AW + 3 * H * HD

    def body(pm_ref, cu_ref, qc_ref, ps_ref, dq_ref, dk_ref, dv_ref, dya_ref, dz_ref, dgb_ref, pa_ref, cq_ref, sp_ref,
             w_ref, x_ref, dx2_ref, mod_ref, vec_ref,
             dp_ref, dx_ref, dpa_ref, dcq_ref, dsp_ref, accv_ref, carry_u, carry_q):
        i = pl.program_id(0)

        @pl.when(i == 0)
        def _():
            dpa_ref[...] = jnp.zeros_like(dpa_ref)
            dcq_ref[...] = jnp.zeros_like(dcq_ref)
            dsp_ref[...] = jnp.zeros_like(dsp_ref)
            accv_ref[...] = jnp.zeros_like(accv_ref)
            carry_u[...] = jnp.zeros_like(carry_u)
            carry_q[...] = jnp.zeros_like(carry_q)

        a_b, a_c, a_x = pm_ref[:, 0:AW], pm_ref[:, AW:2 * AW], pm_ref[:, 2 * AW:3 * AW]
        u = a_c * a_x
        cu = cu_ref[...].astype(F32)
        yp = a_b * cu
        bd = _blockdiag_mean(AW, A_GROUP)
        ra = lax.rsqrt(_dot_f32(yp * yp, bd, NN, exact="b") + EPS)
        na = yp * ra
        dya = dya_ref[...]
        dpa_ref[3:4, :] += _sum0(dya * na)
        dna = dya * pa_ref[3:4, :]
        dyp = ra * (dna - na * _dot_f32(dna * na, bd, NN, exact="b"))
        dcu = dyp * a_b
        dcs = [dcu] + [_shift_up(dcu, s, carry_u[...]) for s in (1, 2)]
        du = pa_ref[2:3, :] * dcs[0]
        for s in range(3):
            dpa_ref[2 - s:3 - s, :] += _sum0(dcs[s] * u)
            if s:
                du = du + pa_ref[2 - s:3 - s, :] * dcs[s]
        carry_u[...] = dcu[0:SUB, :]
        dp_a = jnp.concatenate([dyp * cu, du * a_x, du * a_c], axis=-1).astype(MXU)
        dp_ref[:, 0:3 * AW] = dp_a
        dh = _dot(dp_a, w_ref[:, 0:3 * AW], NT)

        qkv = pm_ref[:, 3 * AW:w3]
        qc = qc_ref[...].astype(F32)
        sg = _sig(qc)
        qs = qc * sg
        parts = []
        for h in range(H):
            q = qs[:, h * HD:(h + 1) * HD]
            rq = lax.rsqrt(jnp.sum(q * q, axis=-1, keepdims=True) + EPS)
            parts.append(_l2_bwd(dq_ref[:, h * HD:(h + 1) * HD] * scale, q * rq, rq))
        for h in range(H):
            k = qs[:, (H + h) * HD:(H + h + 1) * HD]
            rk = lax.rsqrt(jnp.sum(k * k, axis=-1, keepdims=True) + EPS)
            parts.append(_l2_bwd(dk_ref[:, h * HD:(h + 1) * HD], k * rk, rk))
        parts.append(dv_ref[...])
        dqc = jnp.concatenate(parts, axis=-1) * (sg * (1.0 + qc * (1.0 - sg)))
        dqs = [dqc] + [_shift_up(dqc, s, carry_q[...]) for s in (1, 2, 3)]
        dqkv = cq_ref[3:4, :] * dqs[0]
        for s in range(4):
            dcq_ref[3 - s:4 - s, :] += _sum0(dqs[s] * qkv)
            if s:
                dqkv = dqkv + cq_ref[3 - s:4 - s, :] * dqs[s]
        dp_q = dqkv.astype(MXU)
        dp_ref[:, 3 * AW:w3] = dp_q
        dh = dh + _dot(dp_q, w_ref[:, 3 * AW:w3], NT)
        carry_q[...] = dqc[0:SUB, :]

        lane, a, xb, beta, g = _gate_small(ps_ref[...], sp_ref)
        dgb = dgb_ref[...]
        dbeta = jnp.where(lane < H, dgb, 0.0)
        dg = jnp.where((lane >= H) & (lane < 2 * H), dgb, 0.0)
        dalpha = dg * a * _sig(xb)
        dsp_ref[0:1, :] += _sum0(dg * g)
        dsp_ref[1:2, :] += _sum0(dalpha)
        dp_z = jnp.concatenate([dz_ref[...], dbeta * beta * (1.0 - beta) + dalpha], axis=-1).astype(MXU)
        dp_ref[:, w3:P_PAD] = dp_z
        dh = dh + _dot(dp_z, w_ref[:, w3:P_PAD], NT)

        n, r = _rms(x_ref[...])
        nw, sc = vec_ref[0:1, :], mod_ref[1:2, :]
        accv_ref[0:1, :] += _sum0(dh)
        accv_ref[1:2, :] += _sum0(dh * n * nw)
        accv_ref[2:3, :] += _sum0(dh * n * (1.0 + sc))
        dx_ref[...] = _rms_bwd(dh * nw * (1.0 + sc), n, r) + dx2_ref[...]

    row = lambda i: (ni - 1 - i, 0)
    hrow = pl.BlockSpec((T, H * HD), row)
    rowD = pl.BlockSpec((T, D), row)
    return pl.pallas_call(
        body, name="pre_in_bwd", grid=(ni,),
        in_specs=[pl.BlockSpec((T, w3), row), pl.BlockSpec((T, AW), row), pl.BlockSpec((T, 3 * H * HD), row),
                  pl.BlockSpec((T, LANES), lambda i: (ni - 1 - i, (P_PAD - LANES) // LANES)),
                  hrow, hrow, hrow, pl.BlockSpec((T, AW), row), hrow,
                  pl.BlockSpec((T, LANES), row),
                  _full((SUB, AW)), _full((SUB, 3 * H * HD)), _full((SUB, LANES)),
                  _full((D, P_PAD)), rowD, rowD, _full((SUB, D)), _full((SUB, D))],
        out_specs=[pl.BlockSpec((T, P_PAD), row), rowD, _full((SUB, AW)), _full((SUB, 3 * H * HD)), _full((SUB, LANES)),
                   _full((SUB, D))],
        out_shape=[jax.ShapeDtypeStruct((L, P_PAD), MXU), jax.ShapeDtypeStruct((L, D), F32),
                   jax.ShapeDtypeStruct((SUB, AW), F32), jax.ShapeDtypeStruct((SUB, 3 * H * HD), F32),
                   jax.ShapeDtypeStruct((SUB, LANES), F32), jax.ShapeDtypeStruct((SUB, D), F32)],
        scratch_shapes=[pltpu.VMEM((SUB, AW), F32), pltpu.VMEM((SUB, 3 * H * HD), F32)],
        compiler_params=_params(1),
    )(p, cub, qcb, p, dqn, dkn, dvs, dya, dz, dgb, pa, cq, sp, w_in, x, dx2, modrows, vec)


def _wgrad(a, b, tm, tn, name):
    L, m = a.shape
    n = b.shape[1]
    tl = _tile(L, 1024)
    tm, tn = _tile(m, tm), _tile(n, tn)
    nl = L // tl

    def body(a_ref, b_ref, o_ref, acc):
        @pl.when(pl.program_id(2) == 0)
        def _():
            acc[...] = jnp.zeros_like(acc)

        acc[...] += _dot(a_ref[...], b_ref[...], TN)

        @pl.when(pl.program_id(2) == nl - 1)
        def _():
            o_ref[...] = acc[...].astype(o_ref.dtype)

    return pl.pallas_call(
        body, name=name, grid=(m // tm, n // tn, nl),
        in_specs=[pl.BlockSpec((tl, tm), lambda i, j, l: (l, i)), pl.BlockSpec((tl, tn), lambda i, j, l: (l, j))],
        out_specs=pl.BlockSpec((tm, tn), lambda i, j, l: (i, j)),
        out_shape=jax.ShapeDtypeStruct((m, n), MXU), scratch_shapes=[pltpu.VMEM((tm, tn), F32)],
        compiler_params=_params(3),
    )(a, b)


def _wgrad_cols(a, b, tm, n_shard, wpad, count, name):
    L, m = a.shape
    n = b.shape[1]
    tl = _tile(L, 1024)
    tm = _tile(m, tm)
    nl = L // tl
    wins = _shard_windows(n_shard, count)
    assert all(a_ * LANES + win <= n for a_, _, win in wins), (wins, n)

    def body(a_ref, b_ref, o_ref, acc):
        @pl.when(pl.program_id(1) == 0)
        def _():
            acc[...] = jnp.zeros_like(acc)

        acc[...] += _dot(a_ref[...], b_ref[...], TN)

        @pl.when(pl.program_id(1) == nl - 1)
        def _():
            for k, (a_, s, win) in enumerate(wins):
                xk = acc[:, a_ * LANES:a_ * LANES + win]
                if s:
                    xk = pltpu.roll(xk, win - s, 1)
                o_ref[k] = _fit_lanes(xk, wpad).astype(o_ref.dtype)

    return pl.pallas_call(
        body, name=name, grid=(m // tm, nl),
        in_specs=[pl.BlockSpec((tl, tm), lambda i, l: (l, i)), pl.BlockSpec((tl, n), lambda i, l: (l, 0))],
        out_specs=pl.BlockSpec((count, tm, wpad), lambda i, l: (0, i, 0)),
        out_shape=jax.ShapeDtypeStruct((count, m, wpad), MXU),
        scratch_shapes=[pltpu.VMEM((tm, n), F32)],
        compiler_params=_params(2),
    )(a, b)


def _adamw(w, g, m, v, name):
    r, n = w.shape
    tr = _tile(r, 512)
    bc1 = 1.0 - ADAM_B1 ** ADAM_STEP
    bc2 = 1.0 - ADAM_B2 ** ADAM_STEP

    def body(w_ref, g_ref, m_ref, v_ref, d_ref, nm_ref, nv_ref):
        gv = g_ref[...]
        nm = ADAM_B1 * m_ref[...] + (1.0 - ADAM_B1) * gv
        nv = ADAM_B2 * v_ref[...] + (1.0 - ADAM_B2) * (gv * gv)
        nm_ref[...] = nm
        nv_ref[...] = nv
        d_ref[...] = -ADAM_LR * ((nm / bc1) / (jnp.sqrt(nv / bc2) + ADAM_EPS) + ADAM_WD * w_ref[...])

    spec = pl.BlockSpec((tr, n), lambda i: (i, 0))
    return pl.pallas_call(
        body, name=name, grid=(r // tr,), in_specs=[spec] * 4, out_specs=[spec] * 3,
        out_shape=[jax.ShapeDtypeStruct((r, n), F32)] * 3, compiler_params=_params(1),
    )(w, g, m, v)


def _rows8(rows, width):
    out = jnp.zeros((SUB, width), F32)
    for r, vrow in enumerate(rows):
        out = out.at[r, :vrow.shape[0]].set(vrow)
    return out


def _at_lanes(v4, start):
    return jnp.zeros((LANES,), F32).at[start:start + v4.shape[0]].set(v4)


def _pad_rows(flat, mult):
    n = flat.shape[0]
    pad = (-n) % mult
    return jnp.pad(flat, (0, pad)) if pad else flat


IN_PAD = 512
UP_PAD = 768


def _local_fwd_bwd(x, target, mod_full, small_w, full_w, on_grads=None):
    norm1_w, norm2_w, norm_a_w, a_log, dt_bias, norm_dn_w, norm_f_w = small_w
    w_in_f, w_out_f, w_up_f, w_down_f, conv_a_f, conv_q_f, conv_f_f = full_w

    def layer_params(i):
        modrows = jnp.concatenate([mod_full[i], jnp.zeros((SUB - N_MOD, D), F32)], axis=0)
        vec = _rows8([norm1_w[i], norm2_w[i]], D)
        pa = _rows8([conv_a_f[i, 0], conv_a_f[i, 1], conv_a_f[i, 2], norm_a_w[i]], AW)
        cq = _rows8([conv_q_f[i, k] for k in range(4)], 3 * H * HD)
        sp = _rows8([_at_lanes(a_log[i], H), _at_lanes(dt_bias[i], H), norm_dn_w[i]], LANES)
        cff = _rows8([conv_f_f[i, k] for k in range(3)], 2 * DFF)
        return modrows, vec, pa, cq, sp, cff

    saved = []
    xi = x
    for i in range(DEPTH):
        modrows, vec, pa, cq, sp, cff = layer_params(i)
        p, h1, qn, kn, vs, gb, ya, cub, qcb = _in_pre_fwd(xi, modrows, vec, w_in_f[i], pa, cq, sp)
        o, states, tinvs = _gdr_fwd(qn, kn, vs, gb)
        y, x2, yb = _post_fwd(o, p, ya, xi, modrows, sp, w_out_f[i])
        h2, gp0, up0, gc0, uc0, f0, d0 = _ffn_fwd_half(x2, modrows, vec, w_up_f[i], cff, w_down_f[i], 0, None)
        gp1, up1, gc1, uc1, f1, dff, x3 = _ffn_fwd_half(x2, modrows, vec, w_up_f[i], cff, w_down_f[i], 1, d0)
        saved.append(dict(x=xi, p=p, h1=h1, qn=qn, kn=kn, vs=vs, gb=gb, ya=ya, cub=cub, qcb=qcb, o=o, states=states,
                          tinvs=tinvs, y=y, x2=x2, yb=yb,
                          h2=h2, gpre=(gp0, gp1), upre=(up0, up1), gc=(gc0, gc1), uc=(uc0, uc1), f=(f0, f1), d=dff))
        xi = x3

    dx, facc = _final(xi, target, _rows8([norm_f_w], D))
    loss_local = jnp.sum(facc[0])
    d_norm_f = facc[1]

    gw_in, gw_out, gw_up, gw_down = [None] * DEPTH, [None] * DEPTH, [None] * DEPTH, [None] * DEPTH
    g_small = [None] * DEPTH
    for i in reversed(range(DEPTH)):
        s = saved[i]
        modrows, vec, pa, cq, sp, cff = layer_params(i)
        dd, dgp0, dup0, dh0, dcg0, dcu0 = _ffn_bwd_half(dx, modrows, s["gpre"][0], s["upre"][0], s["gc"][0], s["uc"][0],
                                                        cff, w_down_f[i], w_up_f[i], 0, None)
        dgp1, dup1, dx2, accf, dcg1, dcu1 = _ffn_bwd_half(dx, modrows, s["gpre"][1], s["upre"][1], s["gc"][1], s["uc"][1],
                                                          cff, w_down_f[i], w_up_f[i], 1, (s["d"], s["x2"], vec, dh0))
        n_up, up_pad = 2 * DFF // N_DEV, UP_PAD
        gw_up[i] = jnp.concatenate([_wgrad_cols(s["h2"], t, 1024, n_up, up_pad, FF_CW // n_up, "wgrad_up")
                                    for t in (dgp0, dgp1, dup0, dup1)], axis=0)
        gw_down[i] = jnp.concatenate([_wgrad(s["f"][0], dd, FF_CW, 1024, "wgrad_down"),
                                      _wgrad(s["f"][1], dd, FF_CW, 1024, "wgrad_down")],
                                     axis=0).reshape(N_DEV, DFF // N_DEV, D)
        if on_grads is not None:
            on_grads(i, "ffn", [gw_up[i], gw_down[i]])
        dy, do, dz, dya, accp, accs = _post_bwd(dx2, s["y"], s["o"], s["p"], modrows, sp, w_out_f[i])
        gw_out[i] = jnp.concatenate([_wgrad(s["ya"], dy, 512, 1024, "wgrad_out"),
                                     _wgrad(s["yb"], dy, 512, 1024, "wgrad_out")], axis=0).reshape(N_DEV, D // N_DEV, D)
        dqn, dkn, dvs, dgb = _gdr_bwd(s["qn"], s["kn"], s["vs"], s["gb"], s["states"], s["tinvs"], do)
        dp, dx, dpa, dcq, dsp, acci = _pre_in_bwd(s["p"], s["cub"], s["qcb"], dqn, dkn, dvs, dya, dz, dgb, pa, cq, sp,
                                                  w_in_f[i], s["x"], dx2, modrows, vec)
        gw_in[i] = _wgrad_cols(s["h1"], dp, 1024, P_IN // N_DEV, IN_PAD, N_DEV, "wgrad_in")
        dconv_ff = jnp.concatenate([dcg0, dcg1, dcu0, dcu1], axis=1)[0:3]
        dmod = jnp.stack([acci[0], acci[1], accp[0], accf[1], accf[2], accf[0]])
        g_small[i] = dict(norm1=acci[2], norm2=accf[3], norm_a=dpa[3], a_log=dsp[0, H:2 * H], dt_bias=dsp[1, H:2 * H],
                          norm_dn=accs[0], conv_a=dpa[0:3], conv_qkv=dcq[0:4], conv_ff=dconv_ff, dmod=dmod.reshape(-1))
        if on_grads is not None:
            on_grads(i, "mix", [gw_in[i], gw_out[i]])
    return loss_local, dx, gw_in, gw_out, gw_up, gw_down, g_small, d_norm_f


def kernel(x, c, ada_w, ada_b, norm1_w, w_in, conv_a_w, norm_a_w, conv_qkv_w, a_log, dt_bias, norm_dn_w, w_out, norm2_w, w_up, conv_ff_w, w_down, norm_f_w, loss_target, m_ada_w, m_ada_b, m_norm1_w, m_w_in, m_conv_a_w, m_norm_a_w, m_conv_qkv_w, m_a_log, m_dt_bias, m_norm_dn_w, m_w_out, m_norm2_w, m_w_up, m_conv_ff_w, m_w_down, m_norm_f_w, v_ada_w, v_ada_b, v_norm1_w, v_w_in, v_conv_a_w, v_norm_a_w, v_conv_qkv_w, v_a_log, v_dt_bias, v_norm_dn_w, v_w_out, v_norm2_w, v_w_up, v_conv_ff_w, v_w_down, v_norm_f_w):
    ax, ay, ac = lax.axis_index("x"), lax.axis_index("y"), lax.axis_index("c")
    me = 4 * ax + 2 * ay + ac
    x = x[0]
    target = loss_target[0]
    n_in, n_up = P_IN // N_DEV, 2 * DFF // N_DEV

    def lane_pad(t, width):
        return jnp.pad(t.astype(MXU), ((0, 0), (0, 0), (0, width - t.shape[-1])))

    conv_blob = _pad_rows(jnp.concatenate([t.reshape(-1) for t in (conv_a_w, conv_qkv_w, conv_ff_w)]),
                          SUB * LANES).reshape(-1, LANES)
    c_rows = jnp.zeros((SUB, D), F32).at[0].set(c[0])
    send = [lane_pad(w_in, IN_PAD), w_out.astype(MXU), lane_pad(w_up, UP_PAD), w_down.astype(MXU)]
    got = [None] * DEPTH
    g_in0, g_conv, g_c = _all_gather([send[0][0], conv_blob, c_rows], "gather_weights", in_vmem=False)
    shards, _ = lax.optimization_barrier(([t[0] for t in send[1:]], g_c))
    got[0] = [g_in0] + _all_gather_async(shards, "gather_weights_l0", collective_id=0)
    for i in range(1, DEPTH):
        shards, _ = lax.optimization_barrier(([t[i] for t in send], g_c))
        got[i] = _all_gather_async(shards, "gather_weights_l%d" % i, collective_id=i)
    w_in_f = [_interleave_cols(g[0][:, None], n_in, P_PAD, "interleave_w_in")[0] for g in got]
    w_up_f = [_interleave_cols(g[2][:, None], n_up, 2 * DFF, "interleave_w_up")[0] for g in got]
    w_out_f = [g[1].reshape(D, D) for g in got]
    w_down_f = [g[3].reshape(DFF, D) for g in got]
    sg = g_conv.reshape(N_DEV, -1)
    o1 = conv_a_w.size
    o2 = o1 + conv_qkv_w.size
    o3 = o2 + conv_ff_w.size
    conv_a_f = sg[:, 0:o1].reshape(N_DEV, DEPTH, 3, AW // N_DEV).transpose(1, 2, 0, 3).reshape(DEPTH, 3, AW)
    conv_q_f = sg[:, o1:o2].reshape(N_DEV, DEPTH, 4, 3 * H * HD // N_DEV).transpose(1, 2, 0, 3).reshape(DEPTH, 4, 3 * H * HD)
    conv_f_f = sg[:, o2:o3].reshape(N_DEV, DEPTH, 3, n_up).transpose(1, 2, 0, 3).reshape(DEPTH, 3, 2 * DFF)

    c_all = jnp.concatenate([g_c[:, 0], jnp.zeros((16 - N_DEV, D), F32)], axis=0)
    n_ada = N_MOD * D // N_DEV
    ada_b_cols = lax.dynamic_slice_in_dim(ada_b, me * n_ada, n_ada, axis=1)[:, None, :]
    mod_sh = _mod_fwd(c_all, ada_w, ada_b_cols)
    mod_all = _all_gather([mod_sh.reshape(DEPTH * 16, n_ada)], "gather_mod", in_vmem=True)[0]
    mod_all = mod_all.reshape(N_DEV, DEPTH, 16, n_ada)
    mod_mine = lax.dynamic_index_in_dim(mod_all, me, axis=2, keepdims=False)
    mod_full = mod_mine.transpose(1, 0, 2).reshape(DEPTH, N_MOD, D)

    tags = ["w_in", "w_out", "w_up", "w_down"]
    received = [dict() for _ in range(DEPTH)]

    def on_grads(i, part, gs_i):
        first_id = DEPTH if part == "ffn" else 2 * DEPTH
        got_i = _rs_exchange_async(gs_i, "rs_exchange_%s_l%d" % (part, i), collective_id=first_id + i)
        received[i].update(zip(("w_up", "w_down") if part == "ffn" else ("w_in", "w_out"), got_i))

    loss_local, dx, _, _, _, _, g_small, d_norm_f = _local_fwd_bwd(
        x, target, mod_full, (norm1_w, norm2_w, norm_a_w, a_log, dt_bias, norm_dn_w, norm_f_w),
        (w_in_f, w_out_f, w_up_f, w_down_f, conv_a_f, conv_q_f, conv_f_f), on_grads)
    loss = lax.psum(loss_local, ("x", "y", "c"))
    grad_x = dx[None]

    keys = ["dmod", "norm1", "norm2", "norm_a", "a_log", "dt_bias", "norm_dn", "conv_a", "conv_qkv", "conv_ff"]
    stacked = {k: jnp.stack([g_small[i][k] for i in range(DEPTH)]) for k in keys}
    flat_parts = [stacked[k].reshape(-1) for k in keys] + [d_norm_f]
    sizes = [int(t.shape[0]) for t in flat_parts]
    sflat = _pad_rows(jnp.concatenate(flat_parts), SUB * LANES).reshape(-1, LANES)
    sall = _all_gather([sflat], "gather_small_grads", in_vmem=True)[0]
    ssum = _sum_devices(sall).reshape(-1)
    so = [0]
    for sz in sizes:
        so.append(so[-1] + sz)
    red = {k: ssum[so[n]:so[n + 1]].reshape(stacked[k].shape) for n, k in enumerate(keys)}
    g_norm_f = ssum[so[len(keys)]:so[len(keys) + 1]]
    dmod_all = sall[:, 0:sizes[0] // LANES, :].reshape(N_DEV, DEPTH, N_MOD * D)

    g_ada_b = red["dmod"].reshape(DEPTH, N_MOD * D)
    dmod_cols = lax.dynamic_slice_in_dim(dmod_all, me * n_ada, n_ada, axis=2).transpose(1, 0, 2)
    dmod_cols = jnp.concatenate([dmod_cols, jnp.zeros((DEPTH, 16 - N_DEV, n_ada), F32)], axis=1)
    g_ada_w = _mod_bwd(c_all, dmod_cols)
    g_conv_a = lax.dynamic_slice_in_dim(red["conv_a"], me * (AW // N_DEV), AW // N_DEV, axis=2)
    g_conv_qkv = lax.dynamic_slice_in_dim(red["conv_qkv"], me * (3 * H * HD // N_DEV), 3 * H * HD // N_DEV, axis=2)
    g_conv_ff = lax.dynamic_slice_in_dim(red["conv_ff"], me * n_up, n_up, axis=2)

    widths = dict(w_in=n_in, w_out=D, w_up=n_up, w_down=D)
    g_w_in, g_w_out, g_w_up, g_w_down = [
        jnp.stack([_rs_sum(received[i][t], widths[t], "rs_sum_" + t) for i in range(DEPTH)]) for t in tags]

    grads = dict(ada_w=g_ada_w, ada_b=g_ada_b, norm1_w=red["norm1"], w_in=g_w_in, conv_a_w=g_conv_a,
                 norm_a_w=red["norm_a"], conv_qkv_w=g_conv_qkv, a_log=red["a_log"], dt_bias=red["dt_bias"],
                 norm_dn_w=red["norm_dn"], w_out=g_w_out, norm2_w=red["norm2"], w_up=g_w_up, conv_ff_w=g_conv_ff,
                 w_down=g_w_down, norm_f_w=g_norm_f)
    weights = dict(ada_w=ada_w, ada_b=ada_b, norm1_w=norm1_w, w_in=w_in, conv_a_w=conv_a_w, norm_a_w=norm_a_w,
                   conv_qkv_w=conv_qkv_w, a_log=a_log, dt_bias=dt_bias, norm_dn_w=norm_dn_w, w_out=w_out,
                   norm2_w=norm2_w, w_up=w_up, conv_ff_w=conv_ff_w, w_down=w_down, norm_f_w=norm_f_w)
    ms = dict(ada_w=m_ada_w, ada_b=m_ada_b, norm1_w=m_norm1_w, w_in=m_w_in, conv_a_w=m_conv_a_w, norm_a_w=m_norm_a_w,
              conv_qkv_w=m_conv_qkv_w, a_log=m_a_log, dt_bias=m_dt_bias, norm_dn_w=m_norm_dn_w, w_out=m_w_out,
              norm2_w=m_norm2_w, w_up=m_w_up, conv_ff_w=m_conv_ff_w, w_down=m_w_down, norm_f_w=m_norm_f_w)
    vs_ = dict(ada_w=v_ada_w, ada_b=v_ada_b, norm1_w=v_norm1_w, w_in=v_w_in, conv_a_w=v_conv_a_w, norm_a_w=v_norm_a_w,
               conv_qkv_w=v_conv_qkv_w, a_log=v_a_log, dt_bias=v_dt_bias, norm_dn_w=v_norm_dn_w, w_out=v_w_out,
               norm2_w=v_norm2_w, w_up=v_w_up, conv_ff_w=v_conv_ff_w, w_down=v_w_down, norm_f_w=v_norm_f_w)
    names = list(weights)
    big_names = ["ada_w", "w_in", "w_out", "w_up", "w_down"]
    delta, new_m, new_v = {}, {}, {}
    for n in big_names:
        shp = weights[n].shape
        two = lambda t: t.reshape(-1, shp[-1])
        dl, nm, nv = _adamw(two(weights[n]), two(grads[n]), two(ms[n]), two(vs_[n]), "adamw_" + n)
        delta[n], new_m[n], new_v[n] = dl.reshape(shp), nm.reshape(shp), nv.reshape(shp)
    small_names = [n for n in names if n not in big_names]

    def pack(dct):
        return _pad_rows(jnp.concatenate([dct[n].reshape(-1) for n in small_names]), SUB * LANES).reshape(-1, LANES)

    dl, nm, nv = _adamw(pack(weights), pack(grads), pack(ms), pack(vs_), "adamw_small")
    off = 0
    for n in small_names:
        sz, shp = weights[n].size, weights[n].shape
        delta[n] = dl.reshape(-1)[off:off + sz].reshape(shp)
        new_m[n] = nm.reshape(-1)[off:off + sz].reshape(shp)
        new_v[n] = nv.reshape(-1)[off:off + sz].reshape(shp)
        off += sz

    return (loss, grad_x, *[grads[n] for n in names], *[delta[n] for n in names],
            *[new_m[n] for n in names], *[new_v[n] for n in names])
```

```python
import functools
import math

import jax
import jax.numpy as jnp
from jax import lax
from jax.experimental import pallas as pl
from jax.experimental.pallas import tpu as pltpu
from jax.experimental.pallas import tpu_sc as plsc

F32 = jnp.float32
MXU = jnp.bfloat16

D = 1024
DEPTH = 4
N_MOD = 6
AW = 512
A_GROUP = 64
H = 4
HD = 128
CK = 64
DFF = 2816
P_IN = 3592
P_PAD = 3712
EPS = 1e-6
N_DEV = 8
LANES = 128
SUB = 8
VMEM_LIMIT = 56 * 1024 * 1024

ADAM_LR, ADAM_B1, ADAM_B2, ADAM_EPS, ADAM_WD, ADAM_STEP = 0.001, 0.9, 0.999, 1e-08, 0.01, 10

NN = ((1,), (0,))
NT = ((1,), (1,))
TN = ((0,), (0,))
HI = lax.Precision.HIGHEST
MESH = pl.DeviceIdType.MESH


def _dot(a, b, dims, prec=None):
    if prec is None:
        a = a.astype(MXU) if a.dtype == F32 else a
        b = b.astype(MXU) if b.dtype == F32 else b
    return lax.dot_general(a, b, (dims, ((), ())), precision=prec, preferred_element_type=F32)


def _params(n_grid=0, limit=VMEM_LIMIT):
    sem = ("arbitrary",) * n_grid if n_grid else None
    return pltpu.CompilerParams(dimension_semantics=sem, vmem_limit_bytes=limit)


def _tile(n, want):
    if n <= want:
        return n
    t = want - want % SUB
    while n % t:
        t -= SUB
    assert t > 0, (n, want)
    return t


def _full(shape):
    nd = len(shape)
    return pl.BlockSpec(shape, lambda *_: (0,) * nd)


def _sig(x):
    return jax.nn.sigmoid(x)


def _rms(x):
    r = lax.rsqrt(jnp.mean(x * x, axis=-1, keepdims=True) + EPS)
    return x * r, r


def _rms_bwd(dn, n, r):
    return r * (dn - n * jnp.mean(dn * n, axis=-1, keepdims=True))


def _l2_bwd(dn, n, r):
    return r * (dn - n * jnp.sum(dn * n, axis=-1, keepdims=True))


def _sum0(x):
    return jnp.sum(x, axis=0, keepdims=True)


def _shift_down(x, s, halo):
    ext = jnp.concatenate([halo, x], axis=0)
    return pltpu.roll(ext, s, 0)[SUB:, :]


def _shift_up(x, s, halo):
    t = x.shape[0]
    ext = jnp.concatenate([x, halo], axis=0)
    return pltpu.roll(ext, t + SUB - s, 0)[:t, :]


def _conv_fwd(x, w_ref, width, halo):
    sh = [x] + [_shift_down(x, s, halo) for s in range(1, width)]
    out = w_ref[width - 1:width, :] * sh[0]
    for s in range(1, width):
        out = out + w_ref[width - 1 - s:width - s, :] * sh[s]
    return out, sh


def _blockdiag_mean(n, group):
    r = lax.shift_right_logical(lax.broadcasted_iota(jnp.int32, (n, n), 0), int(math.log2(group)))
    c = lax.shift_right_logical(lax.broadcasted_iota(jnp.int32, (n, n), 1), int(math.log2(group)))
    return jnp.where(r == c, 1.0 / group, 0.0).astype(F32)


def _softplus(x):
    return jnp.maximum(x, 0.0) + jnp.log(1.0 + jnp.exp(-jnp.abs(x)))


def _my_place():
    return lax.axis_index("x"), lax.axis_index("y"), lax.axis_index("c")


def _all_gather(shards, name, in_vmem):
    nt = len(shards)

    def body(*refs):
        x_refs, out_refs = refs[:nt], refs[nt:2 * nt]
        send_sems, recv_sems, local_sems = refs[2 * nt:]
        x, y, c = _my_place()
        me, sibling = (x, y, c), (x, y, 1 - c)
        chips = [(1 - x, y), (x, 1 - y), (1 - x, 1 - y)]
        everything = []
        for t in range(nt):
            x_ref, out_ref = x_refs[t], out_refs[t]

            def blk(px, py, pc, out_ref=out_ref):
                return out_ref.at[4 * px + 2 * py + pc]

            def copy(k, block, to, src=None, t=t, blk=blk):
                return pltpu.make_async_remote_copy(
                    src_ref=blk(*block) if src is None else src, dst_ref=blk(*block),
                    send_sem=send_sems.at[7 * t + k], recv_sem=recv_sems.at[7 * t + k], device_id=to, device_id_type=MESH)

            mine = pltpu.make_async_copy(x_ref, blk(*me), local_sems.at[t])
            mine.start()
            first = [copy(0, me, sibling, src=x_ref)]
            first += [copy(1 + j, me, (*chip, c), src=x_ref) for j, chip in enumerate(chips)]
            for cp in first:
                cp.start()
            everything.append((copy, mine, first))
        sends = []
        for copy, mine, first in everything:
            passed = [copy(4 + j, (*chip, c), sibling) for j, chip in enumerate(chips)]
            for j, chip in enumerate(chips):
                copy(1 + j, (*chip, c), me).wait_recv()
                passed[j].start()
            sends += first + passed
        for copy, mine, first in everything:
            copy(0, sibling, me).wait_recv()
            for j, chip in enumerate(chips):
                copy(4 + j, (*chip, 1 - c), me).wait_recv()
        for cp in sends:
            cp.wait_send()
        for copy, mine, first in everything:
            mine.wait()

    space = pltpu.VMEM if in_vmem else pl.ANY
    return pl.pallas_call(
        body, name=name,
        out_shape=[jax.ShapeDtypeStruct((N_DEV,) + s.shape, s.dtype) for s in shards],
        in_specs=[pl.BlockSpec(memory_space=space)] * nt,
        out_specs=[pl.BlockSpec(memory_space=space)] * nt,
        scratch_shapes=[pltpu.SemaphoreType.DMA((7 * nt,)), pltpu.SemaphoreType.DMA((7 * nt,)),
                        pltpu.SemaphoreType.DMA((nt,))],
        compiler_params=pltpu.CompilerParams(vmem_limit_bytes=VMEM_LIMIT),
    )(*shards)


def _all_gather_async(shards, name, collective_id):
    nt = len(shards)
    hbm = pltpu.MemorySpace.HBM
    x_refs = [jax.new_ref(s, memory_space=hbm) for s in shards]
    out_refs = [jax.empty_ref(jax.ShapeDtypeStruct((N_DEV,) + s.shape, s.dtype), memory_space=hbm) for s in shards]

    @pl.kernel(mesh=plsc.ScalarSubcoreMesh(axis_name="sequencer", num_cores=1), name=name,
               scratch_types=(pltpu.SemaphoreType.DMA((7 * nt,)), pltpu.SemaphoreType.DMA((7 * nt,)),
                              pltpu.SemaphoreType.DMA((nt,))),
               compiler_params=pltpu.CompilerParams(collective_id=collective_id))
    def launch(send_sems, recv_sems, local_sems):
        x, y, c = _my_place()
        me, sibling = (x, y, c), (x, y, 1 - c)
        chips = [(1 - x, y), (x, 1 - y), (1 - x, 1 - y)]
        barrier = pltpu.get_barrier_semaphore()
        for peer in [sibling] + [(*chip, c) for chip in chips]:
            pl.semaphore_signal(barrier, inc=1, device_id=peer, device_id_type=MESH)
        pl.semaphore_wait(barrier, 4)
        everything = []
        for t in range(nt):
            x_ref, out_ref = x_refs[t], out_refs[t]

            def blk(px, py, pc, out_ref=out_ref):
                return out_ref.at[4 * px + 2 * py + pc]

            def copy(k, block, to, src=None, t=t, blk=blk):
                return pltpu.make_async_remote_copy(
                    src_ref=blk(*block) if src is None else src, dst_ref=blk(*block),
                    send_sem=send_sems.at[7 * t + k], recv_sem=recv_sems.at[7 * t + k], device_id=to, device_id_type=MESH)

            mine = pltpu.make_async_copy(x_ref, blk(*me), local_sems.at[t])
            mine.start()
            first = [copy(0, me, sibling, src=x_ref)]
            first += [copy(1 + j, me, (*chip, c), src=x_ref) for j, chip in enumerate(chips)]
            for cp in first:
                cp.start()
            everything.append((copy, mine, first))
        sends = []
        for copy, mine, first in everything:
            passed = [copy(4 + j, (*chip, c), sibling) for j, chip in enumerate(chips)]
            for j, chip in enumerate(chips):
                copy(1 + j, (*chip, c), me).wait_recv()
                passed[j].start()
            sends += first + passed
        for copy, mine, first in everything:
            copy(0, sibling, me).wait_recv()
            for j, chip in enumerate(chips):
                copy(4 + j, (*chip, 1 - c), me).wait_recv()
        for cp in sends:
            cp.wait_send()
        for copy, mine, first in everything:
            mine.wait()

    launch()
    return [r[...] for r in out_refs]


def _rs_exchange_async(srcs, name, collective_id):
    nt = len(srcs)
    hbm = pltpu.MemorySpace.HBM
    src_refs = [jax.new_ref(s, memory_space=hbm) for s in srcs]
    out_refs = [jax.empty_ref(jax.ShapeDtypeStruct(s.shape, s.dtype), memory_space=hbm) for s in srcs]
    flips = [(fx, fy, fc) for fx in (0, 1) for fy in (0, 1) for fc in (0, 1)][1:]

    @pl.kernel(mesh=plsc.ScalarSubcoreMesh(axis_name="sequencer", num_cores=1), name=name,
               scratch_types=(pltpu.SemaphoreType.DMA((7 * nt,)), pltpu.SemaphoreType.DMA((7 * nt,)),
                              pltpu.SemaphoreType.DMA((nt,))),
               compiler_params=pltpu.CompilerParams(collective_id=collective_id))
    def launch(send_sems, recv_sems, local_sems):
        x, y, c = _my_place()
        me = 4 * x + 2 * y + c
        peers = [(1 - x if fx else x, 1 - y if fy else y, 1 - c if fc else c) for fx, fy, fc in flips]
        barrier = pltpu.get_barrier_semaphore()
        for peer in peers:
            pl.semaphore_signal(barrier, inc=1, device_id=peer, device_id_type=MESH)
        pl.semaphore_wait(barrier, len(peers))
        own = [pltpu.make_async_copy(src_refs[t].at[me], out_refs[t].at[me], local_sems.at[t]) for t in range(nt)]
        copies = [pltpu.make_async_remote_copy(
            src_ref=src_refs[t].at[4 * px + 2 * py + pc], dst_ref=out_refs[t].at[me],
            send_sem=send_sems.at[7 * t + f], recv_sem=recv_sems.at[7 * t + f],
            device_id=(px, py, pc), device_id_type=MESH) for t in range(nt) for f, (px, py, pc) in enumerate(peers)]
        for cp in own + copies:
            cp.start()
        for cp in copies + own:
            cp.wait()

    launch()
    return [r[...] for r in out_refs]


def _rs_sum(recv, name):
    _, r, n = recv.shape
    tr = _tile(r, 512)

    def body(r_ref, o_ref):
        s = r_ref[0].astype(F32)
        for k in range(1, N_DEV):
            s = s + r_ref[k].astype(F32)
        o_ref[...] = s

    return pl.pallas_call(
        body, name=name, grid=(r // tr,),
        in_specs=[pl.BlockSpec((N_DEV, tr, n), lambda i: (0, i, 0))],
        out_specs=pl.BlockSpec((tr, n), lambda i: (i, 0)),
        out_shape=jax.ShapeDtypeStruct((r, n), F32), compiler_params=_params(1),
    )(recv)


def _shard_windows(n_shard, count, first=0):
    out = []
    for k in range(first, first + count):
        off = n_shard * k
        a, s = off // LANES, off % LANES
        out.append((a, s, -(-(s + n_shard) // LANES) * LANES))
    return out


def _fit_lanes(x, width):
    have = x.shape[1]
    if have < width:
        return jnp.concatenate([x, jnp.zeros((x.shape[0], width - have), x.dtype)], axis=-1)
    return x[:, :width]


def _interleave_cols(g, n_shard, w_out, name):
    nd, nl, rows, wpad = g.shape
    rb = _tile(rows, 256)
    wins = _shard_windows(n_shard, nd)

    def body(g_ref, o_ref, acc):
        acc[...] = jnp.zeros_like(acc)
        for k, (a, s, win) in enumerate(wins):
            xk = _fit_lanes(g_ref[k].astype(F32), win)
            if s:
                xk = pltpu.roll(xk, s, 1)
            acc[:, a * LANES:a * LANES + win] += xk
        o_ref[...] = acc[...].astype(o_ref.dtype)

    return pl.pallas_call(
        body, name=name, grid=(nl, rows // rb),
        in_specs=[pl.BlockSpec((nd, None, rb, wpad), lambda l, i: (0, l, i, 0))],
        out_specs=pl.BlockSpec((None, rb, w_out), lambda l, i: (l, i, 0)),
        out_shape=jax.ShapeDtypeStruct((nl, rows, w_out), g.dtype),
        scratch_shapes=[pltpu.VMEM((rb, w_out), F32)],
        compiler_params=_params(2),
    )(g)


def _sum_devices(g):
    _, r, n = g.shape

    def body(g_ref, o_ref):
        s = g_ref[0]
        for t in range(1, N_DEV):
            s = s + g_ref[t]
        o_ref[...] = s

    return pl.pallas_call(
        body, name="sum_devices", out_shape=jax.ShapeDtypeStruct((r, n), F32),
        in_specs=[pl.BlockSpec(memory_space=pltpu.VMEM)], out_specs=pl.BlockSpec(memory_space=pltpu.VMEM),
        compiler_params=pltpu.CompilerParams(vmem_limit_bytes=VMEM_LIMIT),
    )(g)


def _mod_fwd(c_all, ada_w, ada_b_cols):
    nl, _, nc = ada_w.shape

    def body(c_ref, w_ref, b_ref, o_ref):
        cv = c_ref[...]
        act = (cv * _sig(cv)).astype(MXU)
        o_ref[...] = _dot(act, w_ref[...].astype(MXU), NN) + b_ref[...]

    return pl.pallas_call(
        body, name="mod_fwd", grid=(nl,),
        in_specs=[_full((16, D)), pl.BlockSpec((None, D, nc), lambda i: (i, 0, 0)),
                  pl.BlockSpec((None, 1, nc), lambda i: (i, 0, 0))],
        out_specs=pl.BlockSpec((None, 16, nc), lambda i: (i, 0, 0)),
        out_shape=jax.ShapeDtypeStruct((nl, 16, nc), F32), compiler_params=_params(1),
    )(c_all, ada_w, ada_b_cols)


def _mod_bwd(c_all, dmod_cols):
    nl, _, nc = dmod_cols.shape

    def body(c_ref, d_ref, o_ref):
        cv = c_ref[...]
        act = (cv * _sig(cv)).astype(MXU)
        o_ref[...] = _dot(act, d_ref[...].astype(MXU), TN)

    return pl.pallas_call(
        body, name="mod_bwd", grid=(nl,),
        in_specs=[_full((16, D)), pl.BlockSpec((None, 16, nc), lambda i: (i, 0, 0))],
        out_specs=pl.BlockSpec((None, D, nc), lambda i: (i, 0, 0)),
        out_shape=jax.ShapeDtypeStruct((nl, D, nc), F32), compiler_params=_params(1),
    )(c_all, dmod_cols)


def _gate_small(s, sp_ref):
    lane = lax.broadcasted_iota(jnp.int32, s.shape, 1)
    a = -jnp.exp(sp_ref[0:1, :])
    xb = s + sp_ref[1:2, :]
    beta = _sig(s)
    g = a * _softplus(xb)
    return lane, a, xb, beta, g


def _in_pre_fwd(x, modrows, vec, w_in, pa, cq, sp):
    L = x.shape[0]
    T = _tile(L, 512)
    scale = HD ** -0.5
    w3 = 3 * AW + 3 * H * HD

    def body(x_ref, mod_ref, vec_ref, w_ref, pa_ref, cq_ref, sp_ref,
             p_ref, h_ref, qn_ref, kn_ref, vs_ref, gb_ref, ya_ref, cu_ref, qc_ref, u_carry, q_carry):
        @pl.when(pl.program_id(0) == 0)
        def _():
            u_carry[...] = jnp.zeros_like(u_carry)
            q_carry[...] = jnp.zeros_like(q_carry)

        n, _ = _rms(x_ref[...])
        hb = (n * vec_ref[0:1, :] * (1.0 + mod_ref[1:2, :]) + mod_ref[0:1, :]).astype(MXU)
        h_ref[...] = hb
        pm_a = _dot(hb, w_ref[:, 0:3 * AW], NN)
        p_ref[:, 0:3 * AW] = pm_a
        pm_q = _dot(hb, w_ref[:, 3 * AW:w3], NN)
        p_ref[:, 3 * AW:w3] = pm_q

        a_b = pm_a[:, 0:AW]
        u = pm_a[:, AW:2 * AW] * pm_a[:, 2 * AW:3 * AW]
        cu, _ = _conv_fwd(u, pa_ref, 3, u_carry[...])
        cu_ref[...] = cu.astype(MXU)
        u_carry[...] = u[T - SUB:T, :]
        yp = a_b * cu
        ms = _dot_f32(yp * yp, _blockdiag_mean(AW, A_GROUP), NN, exact="b")
        ya_ref[...] = (yp * lax.rsqrt(ms + EPS) * pa_ref[3:4, :]).astype(MXU)

        pm_z = _dot(hb, w_ref[:, w3:P_PAD], NN)
        p_ref[:, w3:P_PAD] = pm_z
        qkv = pm_q
        qc, _ = _conv_fwd(qkv, cq_ref, 4, q_carry[...])
        qc_ref[...] = qc.astype(MXU)
        q_carry[...] = qkv[T - SUB:T, :]
        qs = qc * _sig(qc)
        for h in range(H):
            q = qs[:, h * HD:(h + 1) * HD]
            qn_ref[:, h * HD:(h + 1) * HD] = q * (lax.rsqrt(jnp.sum(q * q, axis=-1, keepdims=True) + EPS) * scale)
            k = qs[:, (H + h) * HD:(H + h + 1) * HD]
            kn_ref[:, h * HD:(h + 1) * HD] = k * lax.rsqrt(jnp.sum(k * k, axis=-1, keepdims=True) + EPS)
        vs_ref[...] = qs[:, 2 * H * HD:3 * H * HD]

        lane, _, _, beta, g = _gate_small(pm_z[:, H * HD:H * HD + LANES], sp_ref)
        gb_ref[...] = jnp.where(lane < H, beta, jnp.where(lane < 2 * H, g, 0.0))

    row = lambda i: (i, 0)
    return pl.pallas_call(
        body, name="in_pre_fwd", grid=(L // T,),
        in_specs=[pl.BlockSpec((T, D), row), _full((SUB, D)), _full((SUB, D)), _full((D, P_PAD)),
                  _full((SUB, AW)), _full((SUB, 3 * H * HD)), _full((SUB, LANES))],
        out_specs=[pl.BlockSpec((T, P_PAD), row), pl.BlockSpec((T, D), row)]
        + [pl.BlockSpec((T, H * HD), row)] * 3 + [pl.BlockSpec((T, LANES), row), pl.BlockSpec((T, AW), row),
                                                  pl.BlockSpec((T, AW), row), pl.BlockSpec((T, 3 * H * HD), row)],
        out_shape=[jax.ShapeDtypeStruct((L, P_PAD), F32), jax.ShapeDtypeStruct((L, D), MXU)]
        + [jax.ShapeDtypeStruct((L, H * HD), F32)] * 3
        + [jax.ShapeDtypeStruct((L, LANES), F32), jax.ShapeDtypeStruct((L, AW), MXU),
           jax.ShapeDtypeStruct((L, AW), MXU), jax.ShapeDtypeStruct((L, 3 * H * HD), MXU)],
        scratch_shapes=[pltpu.VMEM((SUB, AW), F32), pltpu.VMEM((SUB, 3 * H * HD), F32)],
        compiler_params=_params(1),
    )(x, modrows, vec, w_in, pa, cq, sp)


def _gdr_masks():
    r = lax.broadcasted_iota(jnp.int32, (CK, CK), 0)
    c = lax.broadcasted_iota(jnp.int32, (CK, CK), 1)
    return r >= c, r > c


def _head_cols(gbt, h):
    return gbt[:, h:h + 1], gbt[:, H + h:H + h + 1]


def _split(x, parts):
    out = []
    for _ in range(parts):
        hi = x.astype(jnp.bfloat16)
        out.append(hi)
        x = x - hi.astype(F32)
    return out


def _dot_f32(a, b, dims, exact=None):
    if exact == "a":
        ab = a.astype(jnp.bfloat16)
        return sum(_dot(ab, t, dims) for t in _split(b, 3))
    if exact == "b":
        bb = b.astype(jnp.bfloat16)
        return sum(_dot(t, bb, dims) for t in _split(a, 3))
    ah, al = _split(a, 2)
    bh, bl = _split(b, 2)
    return _dot(ah, bh, dims) + _dot(ah, bl, dims) + _dot(al, bh, dims)


def _gdr_consts():
    causal, strict = _gdr_masks()
    return dict(causal=causal, strict=strict, tril=jnp.where(causal, 1.0, 0.0).astype(F32),
                eye=jnp.where(causal & jnp.logical_not(strict), 1.0, 0.0).astype(F32),
                bcast=jnp.full((CK, HD), 1.0 / HD, F32))


def _dots(a, b, dims):
    return [_dot(x, y, dims) for x, y in zip(a, b)]


def _dots_f32(a, b, dims, exact=None):
    n = len(a)
    if exact == "a":
        lhs = [[x.astype(jnp.bfloat16)] * 3 for x in a]
        rhs = [_split(y, 3) for y in b]
    elif exact == "b":
        lhs = [_split(x, 3) for x in a]
        rhs = [[y.astype(jnp.bfloat16)] * 3 for y in b]
    else:
        sa = [_split(x, 2) for x in a]
        sb = [_split(y, 2) for y in b]
        lhs = [[s[0], s[0], s[1]] for s in sa]
        rhs = [[s[0], s[1], s[0]] for s in sb]
    terms = [[_dot(lhs[i][t], rhs[i][t], dims) for i in range(n)] for t in range(3)]
    return [terms[0][i] + terms[1][i] + terms[2][i] for i in range(n)]


def _gdr_local(q, k, v, beta, g, cst, tinv=None):
    n = len(q)
    R = range(n)
    causal, strict = cst["causal"], cst["strict"]
    gc = _dots_f32([cst["tril"]] * n, [jnp.broadcast_to(g[i], (CK, HD)) for i in R], NN, exact="a")
    g_row = _dots_f32([cst["bcast"]] * n, gc, NT, exact="a")
    decay = [jnp.where(causal, jnp.exp(jnp.where(causal, gc[i][:, 0:CK] - g_row[i], 0.0)), 0.0) for i in R]
    eg = [jnp.exp(gc[i]) for i in R]
    gl = [gc[i][CK - 1:CK, :] for i in R]
    ek = [jnp.exp(gl[i] - gc[i]) for i in R]
    cd = [jnp.exp(gl[i]) for i in R]
    kb = [k[i] * beta[i] for i in R]
    pk = _dots(kb, k, NT)
    if tinv is None:
        xp = [-jnp.where(strict, pk[i] * decay[i], 0.0) for i in R]
        tinv = [cst["eye"] + xp[i] for i in R]
        for _ in range(5):
            xp = _dots_f32(xp, xp, NN)
            tx = _dots_f32(tinv, xp, NN)
            tinv = [tinv[i] + tx[i] for i in R]
    u = _dots(tinv, [v[i] * beta[i] for i in R], NN)
    w = _dots(tinv, [kb[i] * eg[i] for i in R], NN)
    qk = _dots(q, k, NT)
    intra = [jnp.where(causal, qk[i] * decay[i], 0.0) for i in R]
    return dict(decay=decay, eg=eg, ek=ek, cd=cd, kb=kb, pk=pk, tinv=tinv, u=u, w=w, qk=qk, intra=intra,
                q_dec=[q[i] * eg[i] for i in R], k_dec=[k[i] * ek[i] for i in R])


GDR_SUB = 8


def _gdr_fwd(qn, kn, vs, gb):
    L = qn.shape[0]
    nc = L // CK
    cb = min(8, nc)
    rb = cb * CK
    nb = nc // cb
    nsub = GDR_SUB if cb % GDR_SUB == 0 else 1

    def body(q_ref, k_ref, v_ref, gb_ref, o_ref, st_ref, ti_ref, s_ref):
        @pl.when(pl.program_id(0) == 0)
        def _():
            s_ref[...] = jnp.zeros_like(s_ref)

        cst = _gdr_consts()
        heads = range(H)

        def group(gi, carry):
            rows = [pl.ds(pl.multiple_of((gi * nsub + j) * CK, CK), CK) for j in range(nsub)]
            chains = [(j, h) for j in range(nsub) for h in heads]
            gbt = [gb_ref[rows[j], :] for j in range(nsub)]
            cols = lambda h: slice(h * HD, (h + 1) * HD)
            t = _gdr_local([q_ref[rows[j], cols(h)] for j, h in chains], [k_ref[rows[j], cols(h)] for j, h in chains],
                           [v_ref[rows[j], cols(h)] for j, h in chains],
                           [_head_cols(gbt[j], h)[0] for j, h in chains], [_head_cols(gbt[j], h)[1] for j, h in chains], cst)
            s = [s_ref[h] for h in heads]
            for j in range(nsub):
                at = lambda key: [t[key][j * H + h] for h in heads]
                for h in heads:
                    st_ref[h, gi * nsub + j] = s[h]
                    ti_ref[h, gi * nsub + j] = t["tinv"][j * H + h]
                ws = _dots(at("w"), s, NN)
                v_new = [u_h - ws_h for u_h, ws_h in zip(at("u"), ws)]
                o_s = _dots(at("q_dec"), s, NN)
                o_v = _dots(at("intra"), v_new, NN)
                kv = _dots(at("k_dec"), v_new, TN)
                cd = at("cd")
                for h in heads:
                    o_ref[rows[j], cols(h)] = o_s[h] + o_v[h]
                s = [s[h] * cd[h] + kv[h] for h in heads]
            for h in heads:
                s_ref[h] = s[h]
            return carry

        lax.fori_loop(0, cb // nsub, group, 0)

    blk = pl.BlockSpec((rb, H * HD), lambda b: (b, 0))
    return pl.pallas_call(
        body, name="gdr_fwd", grid=(nb,),
        in_specs=[blk, blk, blk, pl.BlockSpec((rb, LANES), lambda b: (b, 0))],
        out_specs=[blk, pl.BlockSpec((H, cb, HD, HD), lambda b: (0, b, 0, 0)),
                   pl.BlockSpec((H, cb, CK, CK), lambda b: (0, b, 0, 0))],
        out_shape=[jax.ShapeDtypeStruct((L, H * HD), F32), jax.ShapeDtypeStruct((H, nc, HD, HD), F32),
                   jax.ShapeDtypeStruct((H, nc, CK, CK), F32)],
        scratch_shapes=[pltpu.VMEM((H, HD, HD), F32)],
        compiler_params=_params(1),
    )(qn, kn, vs, gb)


def _gdr_bwd(qn, kn, vs, gb, states, tinvs, do):
    L = qn.shape[0]
    nc = L // CK
    cb = min(8, nc)
    rb = cb * CK
    nb = nc // cb
    nsub = GDR_SUB if cb % GDR_SUB == 0 else 1

    def body(q_ref, k_ref, v_ref, gb_ref, st_ref, ti_ref, do_ref, dq_ref, dk_ref, dv_ref, dgb_ref, ds_ref):
        @pl.when(pl.program_id(0) == 0)
        def _():
            ds_ref[...] = jnp.zeros_like(ds_ref)

        cst = _gdr_consts()
        causal, strict = cst["causal"], cst["strict"]
        ones = jnp.ones((CK, HD), F32)
        row = lax.broadcasted_iota(jnp.int32, (CK, HD), 0)
        lane = lax.broadcasted_iota(jnp.int32, (CK, LANES), 1)

        heads = range(H)
        rsum = lambda x: jnp.sum(x, axis=-1, keepdims=True)

        def group(gj, carry):
            gi = cb // nsub - 1 - gj
            rows = [pl.ds(pl.multiple_of((gi * nsub + j) * CK, CK), CK) for j in range(nsub)]
            chains = [(j, h) for j in range(nsub) for h in heads]
            gbt = [gb_ref[rows[j], :] for j in range(nsub)]
            cols = lambda h: slice(h * HD, (h + 1) * HD)
            q_all = [q_ref[rows[j], cols(h)] for j, h in chains]
            k_all = [k_ref[rows[j], cols(h)] for j, h in chains]
            v_all = [v_ref[rows[j], cols(h)] for j, h in chains]
            beta_all = [_head_cols(gbt[j], h)[0] for j, h in chains]
            t = _gdr_local(q_all, k_all, v_all, beta_all, [_head_cols(gbt[j], h)[1] for j, h in chains], cst,
                           tinv=[ti_ref[h, gi * nsub + j] for j, h in chains])
            ds_out = [ds_ref[h] for h in heads]
            for j in reversed(range(nsub)):
                at = lambda key: [t[key][j * H + h] for h in heads]
                pick = lambda lst: [lst[j * H + h] for h in heads]
                q, k, v, beta = pick(q_all), pick(k_all), pick(v_all), pick(beta_all)
                u, w, tinv, decay = at("u"), at("w"), at("tinv"), at("decay")
                eg, ek, cd, kb = at("eg"), at("ek"), at("cd"), at("kb")
                q_dec, k_dec, intra, pk, qk = at("q_dec"), at("k_dec"), at("intra"), at("pk"), at("qk")
                s = [st_ref[h, gi * nsub + j] for h in heads]
                dout = [do_ref[rows[j], cols(h)] for h in heads]

                ws = _dots(w, s, NN)
                v_new = [u[h] - ws[h] for h in heads]
                dq_dec = _dots(dout, s, NT)
                qd = _dots(q_dec, dout, TN)
                di = _dots(dout, v_new, NT)
                dintra = [jnp.where(causal, di[h], 0.0) for h in heads]
                ido = _dots(intra, dout, TN)
                kds = _dots(k_dec, ds_out, NN)
                dv_new = [ido[h] + kds[h] for h in heads]
                dk_dec = _dots(v_new, ds_out, NT)
                dcd = [jnp.sum(jnp.sum(ds_out[h] * s[h], axis=1, keepdims=True), axis=0, keepdims=True) for h in heads]
                dvs = _dots(dv_new, s, NT)
                dw = [-dvs[h] for h in heads]
                wdv = _dots(w, dv_new, TN)
                ds_new = [qd[h] + ds_out[h] * cd[h] - wdv[h] for h in heads]
                dru = _dots(tinv, dv_new, TN)
                drw = _dots(tinv, dw, TN)
                dl1 = _dots(dru, u, NT)
                dl2 = _dots(drw, w, NT)
                dlower = [-jnp.where(strict, dl1[h] + dl2[h], 0.0) for h in heads]
                dv = [dru[h] * beta[h] for h in heads]
                dbeta = [rsum(dru[h] * v[h]) for h in heads]
                dgc = [rsum(drw[h] * kb[h]) * eg[h] for h in heads]
                dpk = [dlower[h] * decay[h] for h in heads]
                dqk = [dintra[h] * decay[h] for h in heads]
                dpk_k = _dots(dpk, k, NN)
                dkb = [drw[h] * eg[h] + dpk_k[h] for h in heads]
                dk1 = _dots(dpk, kb, TN)
                dq1 = _dots(dqk, k, NN)
                dk2 = _dots(dqk, q, TN)
                m = [(dlower[h] * pk[h] + dintra[h] * qk[h]) * decay[h] for h in heads]
                mcol = _dots_f32(m, [ones] * H, TN, exact="b")
                e = [rsum(dk_dec[h] * k_dec[h]) for h in heads]
                dgl = [jnp.sum(e[h], axis=0, keepdims=True) + dcd[h] * cd[h] for h in heads]
                dgc = [dgc[h] + rsum(m[h]) - mcol[h] + rsum(dq_dec[h] * q_dec[h]) - e[h]
                       + jnp.where(row == CK - 1, dgl[h], 0.0) for h in heads]
                dg = _dots_f32([cst["tril"]] * H, dgc, TN, exact="a")
                dgb = jnp.zeros((CK, LANES), F32)
                for h in heads:
                    dq_ref[rows[j], cols(h)] = dq1[h] + dq_dec[h] * eg[h]
                    dk_ref[rows[j], cols(h)] = dk1[h] + dk2[h] + dk_dec[h] * ek[h] + dkb[h] * beta[h]
                    dv_ref[rows[j], cols(h)] = dv[h]
                    db = dbeta[h] + rsum(dkb[h] * k[h])
                    dgb = dgb + jnp.where(lane == h, db, 0.0) + jnp.where(lane == H + h, dg[h], 0.0)
                dgb_ref[rows[j], :] = dgb
                ds_out = ds_new
            for h in heads:
                ds_ref[h] = ds_out[h]
            return carry

        lax.fori_loop(0, cb // nsub, group, 0)

    blk = pl.BlockSpec((rb, H * HD), lambda b: (nb - 1 - b, 0))
    sblk = pl.BlockSpec((rb, LANES), lambda b: (nb - 1 - b, 0))
    return pl.pallas_call(
        body, name="gdr_bwd", grid=(nb,),
        in_specs=[blk, blk, blk, sblk, pl.BlockSpec((H, cb, HD, HD), lambda b: (0, nb - 1 - b, 0, 0)),
                  pl.BlockSpec((H, cb, CK, CK), lambda b: (0, nb - 1 - b, 0, 0)), blk],
        out_specs=[blk, blk, blk, sblk],
        out_shape=[jax.ShapeDtypeStruct((L, H * HD), F32)] * 3 + [jax.ShapeDtypeStruct((L, LANES), F32)],
        scratch_shapes=[pltpu.VMEM((H, HD, HD), F32)],
        compiler_params=_params(1),
    )(qn, kn, vs, gb, states, tinvs, do)


def _post_fwd(o, p, ya, x, modrows, sp, w_out):
    L = x.shape[0]
    T = _tile(L, 512)

    def body(o_ref, z_ref, ya_ref, x_ref, mod_ref, sp_ref, w_ref, y_ref, x2_ref, yb_ref):
        ndw = sp_ref[2:3, :]
        z = z_ref[...]
        sz = z * _sig(z)
        parts = []
        for h in range(H):
            n, _ = _rms(o_ref[:, h * HD:(h + 1) * HD])
            parts.append(n * ndw * sz[:, h * HD:(h + 1) * HD])
        yb = jnp.concatenate(parts, axis=-1).astype(MXU)
        yb_ref[...] = yb
        y = _dot(ya_ref[...], w_ref[0:AW, :], NN) + _dot(yb, w_ref[AW:2 * AW, :], NN)
        y_ref[...] = y
        x2_ref[...] = x_ref[...] + mod_ref[2:3, :] * y

    row = lambda i: (i, 0)
    zcol = (3 * AW + 3 * H * HD) // (H * HD)
    return pl.pallas_call(
        body, name="post_fwd", grid=(L // T,),
        in_specs=[pl.BlockSpec((T, H * HD), row), pl.BlockSpec((T, H * HD), lambda i: (i, zcol)),
                  pl.BlockSpec((T, AW), row), pl.BlockSpec((T, D), row), _full((SUB, D)), _full((SUB, LANES)),
                  _full((D, D))],
        out_specs=[pl.BlockSpec((T, D), row), pl.BlockSpec((T, D), row), pl.BlockSpec((T, H * HD), row)],
        out_shape=[jax.ShapeDtypeStruct((L, D), F32), jax.ShapeDtypeStruct((L, D), F32),
                   jax.ShapeDtypeStruct((L, H * HD), MXU)],
        compiler_params=_params(1),
    )(o, p, ya, x, modrows, sp, w_out)


FF_COLS = 2
FF_CW = DFF // FF_COLS
FF_ROWS = 512


def _ffn_fwd_half(x2, modrows, vec, w_up, cff, w_down, j, d_prev):
    assert FF_COLS == 2
    L = x2.shape[0]
    T = _tile(L, FF_ROWS)
    nj = FF_COLS
    last = d_prev is not None

    def body(*refs):
        x_ref, mod_ref, vec_ref, wg_ref, wu_ref, cg_ref, cu_ref, wd_ref = refs[:8]
        if last:
            dp_ref, gp_ref, up_ref, gc_ref, uc_ref, f_ref, d_ref, x3_ref, carry_g, carry_u = refs[8:]
        else:
            h_ref, gp_ref, up_ref, gc_ref, uc_ref, f_ref, d_ref, carry_g, carry_u = refs[8:]

        @pl.when(pl.program_id(0) == 0)
        def _():
            carry_g[...] = jnp.zeros_like(carry_g)
            carry_u[...] = jnp.zeros_like(carry_u)

        xv = x_ref[...]
        n, _ = _rms(xv)
        hb = (n * vec_ref[1:2, :] * (1.0 + mod_ref[4:5, :]) + mod_ref[3:4, :]).astype(MXU)
        if not last:
            h_ref[...] = hb
        g = _dot(hb, wg_ref[...], NN)
        u = _dot(hb, wu_ref[...], NN)
        gp_ref[...] = g.astype(MXU)
        up_ref[...] = u.astype(MXU)
        gc, _ = _conv_fwd(g, cg_ref, 3, carry_g[...])
        uc, _ = _conv_fwd(u, cu_ref, 3, carry_u[...])
        carry_g[...] = g[T - SUB:T, :]
        carry_u[...] = u[T - SUB:T, :]
        gc_ref[...] = gc.astype(MXU)
        uc_ref[...] = uc.astype(MXU)
        fb = (gc * _sig(gc) * uc).astype(MXU)
        f_ref[...] = fb
        part = _dot(fb, wd_ref[...], NN)
        if last:
            dv = dp_ref[...] + part
            d_ref[...] = dv
            x3_ref[...] = xv + mod_ref[5:6, :] * dv
        else:
            d_ref[...] = part

    row = lambda i: (i, 0)
    rowD = pl.BlockSpec((T, D), row)
    rowC = pl.BlockSpec((T, FF_CW), row)
    in_specs = [rowD, _full((SUB, D)), _full((SUB, D)),
                pl.BlockSpec((D, FF_CW), lambda i: (0, j)), pl.BlockSpec((D, FF_CW), lambda i: (0, nj + j)),
                pl.BlockSpec((SUB, FF_CW), lambda i: (0, j)), pl.BlockSpec((SUB, FF_CW), lambda i: (0, nj + j)),
                pl.BlockSpec((FF_CW, D), lambda i: (j, 0))]
    half = [jax.ShapeDtypeStruct((L, FF_CW), MXU)] * 5
    args = [x2, modrows, vec, w_up, w_up, cff, cff, w_down]
    if last:
        in_specs.append(rowD)
        args.append(d_prev)
        out_specs = [rowC] * 5 + [rowD, rowD]
        out_shape = half + [jax.ShapeDtypeStruct((L, D), F32), jax.ShapeDtypeStruct((L, D), F32)]
    else:
        out_specs = [rowD] + [rowC] * 5 + [rowD]
        out_shape = [jax.ShapeDtypeStruct((L, D), MXU)] + half + [jax.ShapeDtypeStruct((L, D), F32)]
    return pl.pallas_call(
        body, name="ffn_fwd_last" if last else "ffn_fwd_first", grid=(L // T,),
        in_specs=in_specs, out_specs=out_specs, out_shape=out_shape,
        scratch_shapes=[pltpu.VMEM((SUB, FF_CW), F32), pltpu.VMEM((SUB, FF_CW), F32)],
        compiler_params=_params(1),
    )(*args)


def _ffn_bwd_half(dx3, modrows, gpre, upre, gcv, ucv, cff, w_down, w_up, j, tail):
    assert FF_COLS == 2
    L = dx3.shape[0]
    T = _tile(L, FF_ROWS)
    ni, nj = L // T, FF_COLS
    last = tail is not None

    def body(*refs):
        dx3_ref, mod_ref, gp_ref, up_ref, gc_ref, uc_ref, cg_ref, cu_ref, wd_ref, wg_ref, wu_ref = refs[:11]
        if last:
            (d_ref, x2_ref, vec_ref, dhp_ref, dgp_ref, dup_ref, dx2_ref, accv_ref, dcg_ref, dcu_ref,
             carry_g, carry_u) = refs[11:]
        else:
            dd_ref, dgp_ref, dup_ref, dh_ref, dcg_ref, dcu_ref, carry_g, carry_u = refs[11:]
        i = pl.program_id(0)

        @pl.when(i == 0)
        def _():
            carry_g[...] = jnp.zeros_like(carry_g)
            carry_u[...] = jnp.zeros_like(carry_u)
            dcg_ref[...] = jnp.zeros_like(dcg_ref)
            dcu_ref[...] = jnp.zeros_like(dcu_ref)
            if last:
                accv_ref[...] = jnp.zeros_like(accv_ref)

        dx3v = dx3_ref[...]
        ddb = (mod_ref[5:6, :] * dx3v).astype(MXU)
        if not last:
            dd_ref[...] = ddb
        g, u = gp_ref[...].astype(F32), up_ref[...].astype(F32)
        gc, uc = gc_ref[...].astype(F32), uc_ref[...].astype(F32)
        sg = _sig(gc)
        df = _dot(ddb, wd_ref[...], NT)
        duc = df * (gc * sg)
        dgc = df * uc * (sg * (1.0 + gc * (1.0 - sg)))
        dgs = [dgc] + [_shift_up(dgc, s, carry_g[...]) for s in (1, 2)]
        dus = [duc] + [_shift_up(duc, s, carry_u[...]) for s in (1, 2)]
        for s in range(3):
            dcg_ref[2 - s:3 - s, :] += _sum0(dgs[s] * g)
            dcu_ref[2 - s:3 - s, :] += _sum0(dus[s] * u)
        dg = (cg_ref[2:3, :] * dgs[0] + cg_ref[1:2, :] * dgs[1] + cg_ref[0:1, :] * dgs[2]).astype(MXU)
        du = (cu_ref[2:3, :] * dus[0] + cu_ref[1:2, :] * dus[1] + cu_ref[0:1, :] * dus[2]).astype(MXU)
        carry_g[...] = dgc[0:SUB, :]
        carry_u[...] = duc[0:SUB, :]
        dgp_ref[...] = dg
        dup_ref[...] = du
        dh = _dot(dg, wg_ref[...], NT) + _dot(du, wu_ref[...], NT)
        if last:
            dh = dh + dhp_ref[...]
            accv_ref[0:1, :] += _sum0(dx3v * d_ref[...])
            n, r = _rms(x2_ref[...])
            nw, sc = vec_ref[1:2, :], mod_ref[4:5, :]
            accv_ref[1:2, :] += _sum0(dh)
            accv_ref[2:3, :] += _sum0(dh * n * nw)
            accv_ref[3:4, :] += _sum0(dh * n * (1.0 + sc))
            dx2_ref[...] = _rms_bwd(dh * nw * (1.0 + sc), n, r) + dx3v
        else:
            dh_ref[...] = dh

    row = lambda i: (ni - 1 - i, 0)
    rowD = pl.BlockSpec((T, D), row)
    rowC = pl.BlockSpec((T, FF_CW), row)
    in_specs = [rowD, _full((SUB, D)), rowC, rowC, rowC, rowC,
                pl.BlockSpec((SUB, FF_CW), lambda i: (0, j)), pl.BlockSpec((SUB, FF_CW), lambda i: (0, nj + j)),
                pl.BlockSpec((FF_CW, D), lambda i: (j, 0)),
                pl.BlockSpec((D, FF_CW), lambda i: (0, j)), pl.BlockSpec((D, FF_CW), lambda i: (0, nj + j))]
    args = [dx3, modrows, gpre, upre, gcv, ucv, cff, cff, w_down, w_up, w_up]
    halfb = [jax.ShapeDtypeStruct((L, FF_CW), MXU), jax.ShapeDtypeStruct((L, FF_CW), MXU)]
    dconv = [jax.ShapeDtypeStruct((SUB, FF_CW), F32)] * 2
    if last:
        d, x2, vec, dh_prev = tail
        in_specs += [rowD, rowD, _full((SUB, D)), rowD]
        args += [d, x2, vec, dh_prev]
        out_specs = [rowC, rowC, rowD, _full((SUB, D)), _full((SUB, FF_CW)), _full((SUB, FF_CW))]
        out_shape = halfb + [jax.ShapeDtypeStruct((L, D), F32), jax.ShapeDtypeStruct((SUB, D), F32)] + dconv
    else:
        out_specs = [rowD, rowC, rowC, rowD, _full((SUB, FF_CW)), _full((SUB, FF_CW))]
        out_shape = [jax.ShapeDtypeStruct((L, D), MXU)] + halfb + [jax.ShapeDtypeStruct((L, D), F32)] + dconv
    return pl.pallas_call(
        body, name="ffn_bwd_last" if last else "ffn_bwd_first", grid=(ni,),
        in_specs=in_specs, out_specs=out_specs, out_shape=out_shape,
        scratch_shapes=[pltpu.VMEM((SUB, FF_CW), F32), pltpu.VMEM((SUB, FF_CW), F32)],
        compiler_params=_params(1),
    )(*args)


def _final(x, target, nf):
    L = x.shape[0]
    T = _tile(L, 256)

    def body(x_ref, t_ref, nf_ref, dx_ref, acc_ref):
        @pl.when(pl.program_id(0) == 0)
        def _():
            acc_ref[...] = jnp.zeros_like(acc_ref)

        n, r = _rms(x_ref[...])
        w = nf_ref[0:1, :]
        err = n * w - t_ref[...]
        acc_ref[0:1, :] += (0.5 / D) * _sum0(err * err)
        dy = err * (1.0 / D)
        acc_ref[1:2, :] += _sum0(dy * n)
        dx_ref[...] = _rms_bwd(dy * w, n, r)

    row = lambda i: (i, 0)
    return pl.pallas_call(
        body, name="final_norm_loss", grid=(L // T,),
        in_specs=[pl.BlockSpec((T, D), row), pl.BlockSpec((T, D), row), _full((SUB, D))],
        out_specs=[pl.BlockSpec((T, D), row), _full((SUB, D))],
        out_shape=[jax.ShapeDtypeStruct((L, D), F32), jax.ShapeDtypeStruct((SUB, D), F32)],
        compiler_params=_params(1),
    )(x, target, nf)


def _post_bwd(dx2, y, o, p, modrows, sp, w_out):
    L = dx2.shape[0]
    T = _tile(L, 512)

    def body(dx2_ref, y_ref, o_ref, z_ref, mod_ref, sp_ref, w_ref, dy_ref, do_ref, dz_ref, dya_ref, accv_ref, accs_ref):
        @pl.when(pl.program_id(0) == 0)
        def _():
            accv_ref[...] = jnp.zeros_like(accv_ref)
            accs_ref[...] = jnp.zeros_like(accs_ref)

        dx2v = dx2_ref[...]
        accv_ref[0:1, :] += _sum0(dx2v * y_ref[...])
        dyb = (mod_ref[2:3, :] * dx2v).astype(MXU)
        dy_ref[...] = dyb
        dyc = _dot(dyb, w_ref[...], NT)
        dya_ref[...] = dyc[:, 0:AW]
        ndw = sp_ref[2:3, :]
        z = z_ref[...]
        sgz = _sig(z)
        dsz = sgz * (1.0 + z * (1.0 - sgz))
        dndw = jnp.zeros((1, HD), F32)
        for h in range(H):
            sl = slice(h * HD, (h + 1) * HD)
            n, r = _rms(o_ref[:, sl])
            dyh = dyc[:, AW + h * HD:AW + (h + 1) * HD]
            zh = z[:, sl]
            don = dyh * (zh * sgz[:, sl])
            dz_ref[:, sl] = dyh * (n * ndw) * dsz[:, sl]
            dndw = dndw + _sum0(don * n)
            do_ref[:, sl] = _rms_bwd(don * ndw, n, r)
        accs_ref[0:1, :] += dndw

    row = lambda i: (i, 0)
    zcol = (3 * AW + 3 * H * HD) // (H * HD)
    return pl.pallas_call(
        body, name="post_bwd", grid=(L // T,),
        in_specs=[pl.BlockSpec((T, D), row), pl.BlockSpec((T, D), row), pl.BlockSpec((T, H * HD), row),
                  pl.BlockSpec((T, H * HD), lambda i: (i, zcol)), _full((SUB, D)), _full((SUB, LANES)), _full((D, D))],
        out_specs=[pl.BlockSpec((T, D), row)] + [pl.BlockSpec((T, H * HD), row)] * 3 + [_full((SUB, D)), _full((SUB, LANES))],
        out_shape=[jax.ShapeDtypeStruct((L, D), MXU)] + [jax.ShapeDtypeStruct((L, H * HD), F32)] * 3
        + [jax.ShapeDtypeStruct((SUB, D), F32), jax.ShapeDtypeStruct((SUB, LANES), F32)],
        compiler_params=_params(1),
    )(dx2, y, o, p, modrows, sp, w_out)


def _pre_in_bwd(p, cub, qcb, dqn, dkn, dvs, dya, dz, dgb, pa, cq, sp, w_in, x, dx2, modrows, vec):
    L = p.shape[0]
    T = _tile(L, 256)
    ni = L // T
    scale = HD ** -0.5
    w3 = 3 * AW + 3 * H * HD

    def body(pm_ref, cu_ref, qc_ref, ps_ref, dq_ref, dk_ref, dv_ref, dya_ref, dz_ref, dgb_ref, pa_ref, cq_ref, sp_ref,
             w_ref, x_ref, dx2_ref, mod_ref, vec_ref,
             dp_ref, dx_ref, dpa_ref, dcq_ref, dsp_ref, accv_ref, carry_u, carry_q):
        i = pl.program_id(0)

        @pl.when(i == 0)
        def _():
            dpa_ref[...] = jnp.zeros_like(dpa_ref)
            dcq_ref[...] = jnp.zeros_like(dcq_ref)
            dsp_ref[...] = jnp.zeros_like(dsp_ref)
            accv_ref[...] = jnp.zeros_like(accv_ref)
            carry_u[...] = jnp.zeros_like(carry_u)
            carry_q[...] = jnp.zeros_like(carry_q)

        a_b, a_c, a_x = pm_ref[:, 0:AW], pm_ref[:, AW:2 * AW], pm_ref[:, 2 * AW:3 * AW]
        u = a_c * a_x
        cu = cu_ref[...].astype(F32)
        yp = a_b * cu
        bd = _blockdiag_mean(AW, A_GROUP)
        ra = lax.rsqrt(_dot_f32(yp * yp, bd, NN, exact="b") + EPS)
        na = yp * ra
        dya = dya_ref[...]
        dpa_ref[3:4, :] += _sum0(dya * na)
        dna = dya * pa_ref[3:4, :]
        dyp = ra * (dna - na * _dot_f32(dna * na, bd, NN, exact="b"))
        dcu = dyp * a_b
        dcs = [dcu] + [_shift_up(dcu, s, carry_u[...]) for s in (1, 2)]
        du = pa_ref[2:3, :] * dcs[0]
        for s in range(3):
            dpa_ref[2 - s:3 - s, :] += _sum0(dcs[s] * u)
            if s:
                du = du + pa_ref[2 - s:3 - s, :] * dcs[s]
        carry_u[...] = dcu[0:SUB, :]
        dp_a = jnp.concatenate([dyp * cu, du * a_x, du * a_c], axis=-1).astype(MXU)
        dp_ref[:, 0:3 * AW] = dp_a
        dh = _dot(dp_a, w_ref[:, 0:3 * AW], NT)

        qkv = pm_ref[:, 3 * AW:w3]
        qc = qc_ref[...].astype(F32)
        sg = _sig(qc)
        qs = qc * sg
        parts = []
        for h in range(H):
            q = qs[:, h * HD:(h + 1) * HD]
            rq = lax.rsqrt(jnp.sum(q * q, axis=-1, keepdims=True) + EPS)
            parts.append(_l2_bwd(dq_ref[:, h * HD:(h + 1) * HD] * scale, q * rq, rq))
        for h in range(H):
            k = qs[:, (H + h) * HD:(H + h + 1) * HD]
            rk = lax.rsqrt(jnp.sum(k * k, axis=-1, keepdims=True) + EPS)
            parts.append(_l2_bwd(dk_ref[:, h * HD:(h + 1) * HD], k * rk, rk))
        parts.append(dv_ref[...])
        dqc = jnp.concatenate(parts, axis=-1) * (sg * (1.0 + qc * (1.0 - sg)))
        dqs = [dqc] + [_shift_up(dqc, s, carry_q[...]) for s in (1, 2, 3)]
        dqkv = cq_ref[3:4, :] * dqs[0]
        for s in range(4):
            dcq_ref[3 - s:4 - s, :] += _sum0(dqs[s] * qkv)
            if s:
                dqkv = dqkv + cq_ref[3 - s:4 - s, :] * dqs[s]
        dp_q = dqkv.astype(MXU)
        dp_ref[:, 3 * AW:w3] = dp_q
        dh = dh + _dot(dp_q, w_ref[:, 3 * AW:w3], NT)
        carry_q[...] = dqc[0:SUB, :]

        lane, a, xb, beta, g = _gate_small(ps_ref[...], sp_ref)
        dgb = dgb_ref[...]
        dbeta = jnp.where(lane < H, dgb, 0.0)
        dg = jnp.where((lane >= H) & (lane < 2 * H), dgb, 0.0)
        dalpha = dg * a * _sig(xb)
        dsp_ref[0:1, :] += _sum0(dg * g)
        dsp_ref[1:2, :] += _sum0(dalpha)
        dp_z = jnp.concatenate([dz_ref[...], dbeta * beta * (1.0 - beta) + dalpha], axis=-1).astype(MXU)
        dp_ref[:, w3:P_PAD] = dp_z
        dh = dh + _dot(dp_z, w_ref[:, w3:P_PAD], NT)

        n, r = _rms(x_ref[...])
        nw, sc = vec_ref[0:1, :], mod_ref[1:2, :]
        accv_ref[0:1, :] += _sum0(dh)
        accv_ref[1:2, :] += _sum0(dh * n * nw)
        accv_ref[2:3, :] += _sum0(dh * n * (1.0 + sc))
        dx_ref[...] = _rms_bwd(dh * nw * (1.0 + sc), n, r) + dx2_ref[...]

    row = lambda i: (ni - 1 - i, 0)
    hrow = pl.BlockSpec((T, H * HD), row)
    rowD = pl.BlockSpec((T, D), row)
    return pl.pallas_call(
        body, name="pre_in_bwd", grid=(ni,),
        in_specs=[pl.BlockSpec((T, w3), row), pl.BlockSpec((T, AW), row), pl.BlockSpec((T, 3 * H * HD), row),
                  pl.BlockSpec((T, LANES), lambda i: (ni - 1 - i, (P_PAD - LANES) // LANES)),
                  hrow, hrow, hrow, pl.BlockSpec((T, AW), row), hrow,
                  pl.BlockSpec((T, LANES), row),
                  _full((SUB, AW)), _full((SUB, 3 * H * HD)), _full((SUB, LANES)),
                  _full((D, P_PAD)), rowD, rowD, _full((SUB, D)), _full((SUB, D))],
        out_specs=[pl.BlockSpec((T, P_PAD), row), rowD, _full((SUB, AW)), _full((SUB, 3 * H * HD)), _full((SUB, LANES)),
                   _full((SUB, D))],
        out_shape=[jax.ShapeDtypeStruct((L, P_PAD), MXU), jax.ShapeDtypeStruct((L, D), F32),
                   jax.ShapeDtypeStruct((SUB, AW), F32), jax.ShapeDtypeStruct((SUB, 3 * H * HD), F32),
                   jax.ShapeDtypeStruct((SUB, LANES), F32), jax.ShapeDtypeStruct((SUB, D), F32)],
        scratch_shapes=[pltpu.VMEM((SUB, AW), F32), pltpu.VMEM((SUB, 3 * H * HD), F32)],
        compiler_params=_params(1),
    )(p, cub, qcb, p, dqn, dkn, dvs, dya, dz, dgb, pa, cq, sp, w_in, x, dx2, modrows, vec)


def _wgrad(a, b, tm, tn, name):
    L, m = a.shape
    n = b.shape[1]
    tl = _tile(L, 1024)
    tm, tn = _tile(m, tm), _tile(n, tn)
    nl = L // tl

    def body(a_ref, b_ref, o_ref, acc):
        @pl.when(pl.program_id(2) == 0)
        def _():
            acc[...] = jnp.zeros_like(acc)

        acc[...] += _dot(a_ref[...], b_ref[...], TN)

        @pl.when(pl.program_id(2) == nl - 1)
        def _():
            o_ref[...] = acc[...].astype(o_ref.dtype)

    return pl.pallas_call(
        body, name=name, grid=(m // tm, n // tn, nl),
        in_specs=[pl.BlockSpec((tl, tm), lambda i, j, l: (l, i)), pl.BlockSpec((tl, tn), lambda i, j, l: (l, j))],
        out_specs=pl.BlockSpec((tm, tn), lambda i, j, l: (i, j)),
        out_shape=jax.ShapeDtypeStruct((m, n), MXU), scratch_shapes=[pltpu.VMEM((tm, tn), F32)],
        compiler_params=_params(3),
    )(a, b)


def _wgrad_cols(a, bs, tm, tl, n_shard, wpad, count, name):
    L, m = a.shape
    n = bs[0].shape[1]
    nb = len(bs)
    tl = _tile(L, tl)
    tm = _tile(m, tm)
    nl = L // tl
    wins = _shard_windows(n_shard, count)
    assert all(a_ * LANES + win <= n for a_, _, win in wins), (wins, n)

    def body(*refs):
        a_ref, b_refs, o_ref, accs = refs[0], refs[1:1 + nb], refs[1 + nb], refs[2 + nb:]

        @pl.when(pl.program_id(1) == 0)
        def _():
            for acc in accs:
                acc[...] = jnp.zeros_like(acc)

        av = a_ref[...]
        for b_ref, acc in zip(b_refs, accs):
            acc[...] += _dot(av, b_ref[...], TN)

        @pl.when(pl.program_id(1) == nl - 1)
        def _():
            for p, acc in enumerate(accs):
                for k, (a_, s, win) in enumerate(wins):
                    xk = acc[:, a_ * LANES:a_ * LANES + win]
                    if s:
                        xk = pltpu.roll(xk, win - s, 1)
                    o_ref[p * count + k] = _fit_lanes(xk, wpad).astype(o_ref.dtype)

    return pl.pallas_call(
        body, name=name, grid=(m // tm, nl),
        in_specs=[pl.BlockSpec((tl, tm), lambda i, l: (l, i))] + [pl.BlockSpec((tl, n), lambda i, l: (l, 0))] * nb,
        out_specs=pl.BlockSpec((nb * count, tm, wpad), lambda i, l: (0, i, 0)),
        out_shape=jax.ShapeDtypeStruct((nb * count, m, wpad), MXU),
        scratch_shapes=[pltpu.VMEM((tm, n), F32)] * nb,
        compiler_params=_params(2),
    )(a, *bs)


def _adamw(w, g, m, v, name):
    r, n = w.shape
    tr = _tile(r, 512)
    bc1 = 1.0 - ADAM_B1 ** ADAM_STEP
    bc2 = 1.0 - ADAM_B2 ** ADAM_STEP

    def body(w_ref, g_ref, m_ref, v_ref, d_ref, nm_ref, nv_ref):
        gv = g_ref[...]
        nm = ADAM_B1 * m_ref[...] + (1.0 - ADAM_B1) * gv
        nv = ADAM_B2 * v_ref[...] + (1.0 - ADAM_B2) * (gv * gv)
        nm_ref[...] = nm
        nv_ref[...] = nv
        d_ref[...] = -ADAM_LR * ((nm / bc1) / (jnp.sqrt(nv / bc2) + ADAM_EPS) + ADAM_WD * w_ref[...])

    spec = pl.BlockSpec((tr, n), lambda i: (i, 0))
    return pl.pallas_call(
        body, name=name, grid=(r // tr,), in_specs=[spec] * 4, out_specs=[spec] * 3,
        out_shape=[jax.ShapeDtypeStruct((r, n), F32)] * 3, compiler_params=_params(1),
    )(w, g, m, v)


def _rows8(rows, width):
    out = jnp.zeros((SUB, width), F32)
    for r, vrow in enumerate(rows):
        out = out.at[r, :vrow.shape[0]].set(vrow)
    return out


def _at_lanes(v4, start):
    return jnp.zeros((LANES,), F32).at[start:start + v4.shape[0]].set(v4)


def _pad_rows(flat, mult):
    n = flat.shape[0]
    pad = (-n) % mult
    return jnp.pad(flat, (0, pad)) if pad else flat


IN_PAD = 512
UP_PAD = 768


def _local_fwd_bwd(x, target, mod_full, small_w, full_w, on_grads=None):
    norm1_w, norm2_w, norm_a_w, a_log, dt_bias, norm_dn_w, norm_f_w = small_w
    w_in_f, w_out_f, w_up_f, w_down_f, conv_a_f, conv_q_f, conv_f_f = full_w

    def layer_params(i):
        modrows = jnp.concatenate([mod_full[i], jnp.zeros((SUB - N_MOD, D), F32)], axis=0)
        vec = _rows8([norm1_w[i], norm2_w[i]], D)
        pa = _rows8([conv_a_f[i, 0], conv_a_f[i, 1], conv_a_f[i, 2], norm_a_w[i]], AW)
        cq = _rows8([conv_q_f[i, k] for k in range(4)], 3 * H * HD)
        sp = _rows8([_at_lanes(a_log[i], H), _at_lanes(dt_bias[i], H), norm_dn_w[i]], LANES)
        cff = _rows8([conv_f_f[i, k] for k in range(3)], 2 * DFF)
        return modrows, vec, pa, cq, sp, cff

    saved = []
    xi = x
    for i in range(DEPTH):
        modrows, vec, pa, cq, sp, cff = layer_params(i)
        p, h1, qn, kn, vs, gb, ya, cub, qcb = _in_pre_fwd(xi, modrows, vec, w_in_f[i], pa, cq, sp)
        o, states, tinvs = _gdr_fwd(qn, kn, vs, gb)
        y, x2, yb = _post_fwd(o, p, ya, xi, modrows, sp, w_out_f[i])
        h2, gp0, up0, gc0, uc0, f0, d0 = _ffn_fwd_half(x2, modrows, vec, w_up_f[i], cff, w_down_f[i], 0, None)
        gp1, up1, gc1, uc1, f1, dff, x3 = _ffn_fwd_half(x2, modrows, vec, w_up_f[i], cff, w_down_f[i], 1, d0)
        saved.append(dict(x=xi, p=p, h1=h1, qn=qn, kn=kn, vs=vs, gb=gb, ya=ya, cub=cub, qcb=qcb, o=o, states=states,
                          tinvs=tinvs, y=y, x2=x2, yb=yb,
                          h2=h2, gpre=(gp0, gp1), upre=(up0, up1), gc=(gc0, gc1), uc=(uc0, uc1), f=(f0, f1), d=dff))
        xi = x3

    dx, facc = _final(xi, target, _rows8([norm_f_w], D))
    loss_local = jnp.sum(facc[0])
    d_norm_f = facc[1]

    gw_in, gw_out, gw_up, gw_down = [None] * DEPTH, [None] * DEPTH, [None] * DEPTH, [None] * DEPTH
    g_small = [None] * DEPTH
    for i in reversed(range(DEPTH)):
        s = saved[i]
        modrows, vec, pa, cq, sp, cff = layer_params(i)
        dd, dgp0, dup0, dh0, dcg0, dcu0 = _ffn_bwd_half(dx, modrows, s["gpre"][0], s["upre"][0], s["gc"][0], s["uc"][0],
                                                        cff, w_down_f[i], w_up_f[i], 0, None)
        dgp1, dup1, dx2, accf, dcg1, dcu1 = _ffn_bwd_half(dx, modrows, s["gpre"][1], s["upre"][1], s["gc"][1], s["uc"][1],
                                                          cff, w_down_f[i], w_up_f[i], 1, (s["d"], s["x2"], vec, dh0))
        n_up, up_pad = 2 * DFF // N_DEV, UP_PAD
        gw_up[i] = _wgrad_cols(s["h2"], [dgp0, dgp1, dup0, dup1], 512, 512, n_up, up_pad, FF_CW // n_up, "wgrad_up")
        gw_down[i] = jnp.concatenate([_wgrad(s["f"][0], dd, FF_CW, 1024, "wgrad_down"),
                                      _wgrad(s["f"][1], dd, FF_CW, 1024, "wgrad_down")],
                                     axis=0).reshape(N_DEV, DFF // N_DEV, D)
        if on_grads is not None:
            on_grads(i, "ffn", [gw_up[i], gw_down[i]])
        dy, do, dz, dya, accp, accs = _post_bwd(dx2, s["y"], s["o"], s["p"], modrows, sp, w_out_f[i])
        gw_out[i] = jnp.concatenate([_wgrad(s["ya"], dy, 512, 1024, "wgrad_out"),
                                     _wgrad(s["yb"], dy, 512, 1024, "wgrad_out")], axis=0).reshape(N_DEV, D // N_DEV, D)
        dqn, dkn, dvs, dgb = _gdr_bwd(s["qn"], s["kn"], s["vs"], s["gb"], s["states"], s["tinvs"], do)
        dp, dx, dpa, dcq, dsp, acci = _pre_in_bwd(s["p"], s["cub"], s["qcb"], dqn, dkn, dvs, dya, dz, dgb, pa, cq, sp,
                                                  w_in_f[i], s["x"], dx2, modrows, vec)
        gw_in[i] = _wgrad_cols(s["h1"], [dp], 1024, 1024, P_IN // N_DEV, IN_PAD, N_DEV, "wgrad_in")
        dconv_ff = jnp.concatenate([dcg0, dcg1, dcu0, dcu1], axis=1)[0:3]
        dmod = jnp.stack([acci[0], acci[1], accp[0], accf[1], accf[2], accf[0]])
        g_small[i] = dict(norm1=acci[2], norm2=accf[3], norm_a=dpa[3], a_log=dsp[0, H:2 * H], dt_bias=dsp[1, H:2 * H],
                          norm_dn=accs[0], conv_a=dpa[0:3], conv_qkv=dcq[0:4], conv_ff=dconv_ff, dmod=dmod.reshape(-1))
        if on_grads is not None:
            on_grads(i, "mix", [gw_in[i], gw_out[i]])
    return loss_local, dx, gw_in, gw_out, gw_up, gw_down, g_small, d_norm_f


def kernel(x, c, ada_w, ada_b, norm1_w, w_in, conv_a_w, norm_a_w, conv_qkv_w, a_log, dt_bias, norm_dn_w, w_out, norm2_w, w_up, conv_ff_w, w_down, norm_f_w, loss_target, m_ada_w, m_ada_b, m_norm1_w, m_w_in, m_conv_a_w, m_norm_a_w, m_conv_qkv_w, m_a_log, m_dt_bias, m_norm_dn_w, m_w_out, m_norm2_w, m_w_up, m_conv_ff_w, m_w_down, m_norm_f_w, v_ada_w, v_ada_b, v_norm1_w, v_w_in, v_conv_a_w, v_norm_a_w, v_conv_qkv_w, v_a_log, v_dt_bias, v_norm_dn_w, v_w_out, v_norm2_w, v_w_up, v_conv_ff_w, v_w_down, v_norm_f_w):
    ax, ay, ac = lax.axis_index("x"), lax.axis_index("y"), lax.axis_index("c")
    me = 4 * ax + 2 * ay + ac
    x = x[0]
    target = loss_target[0]
    n_in, n_up = P_IN // N_DEV, 2 * DFF // N_DEV

    def lane_pad(t, width):
        return jnp.pad(t.astype(MXU), ((0, 0), (0, 0), (0, width - t.shape[-1])))

    conv_blob = _pad_rows(jnp.concatenate([t.reshape(-1) for t in (conv_a_w, conv_qkv_w, conv_ff_w)]),
                          SUB * LANES).reshape(-1, LANES)
    c_rows = jnp.zeros((SUB, D), F32).at[0].set(c[0])
    send = [lane_pad(w_in, IN_PAD), w_out.astype(MXU), lane_pad(w_up, UP_PAD), w_down.astype(MXU)]
    got = [None] * DEPTH
    g_in0, g_conv, g_c = _all_gather([send[0][0], conv_blob, c_rows], "gather_weights", in_vmem=False)
    shards, _ = lax.optimization_barrier(([t[0] for t in send[1:]], g_c))
    got[0] = [g_in0] + _all_gather_async(shards, "gather_weights_l0", collective_id=0)
    for i in range(1, DEPTH):
        shards, _ = lax.optimization_barrier(([t[i] for t in send], g_c))
        got[i] = _all_gather_async(shards, "gather_weights_l%d" % i, collective_id=i)
    w_in_f = [_interleave_cols(g[0][:, None], n_in, P_PAD, "interleave_w_in")[0] for g in got]
    w_up_f = [_interleave_cols(g[2][:, None], n_up, 2 * DFF, "interleave_w_up")[0] for g in got]
    w_out_f = [g[1].reshape(D, D) for g in got]
    w_down_f = [g[3].reshape(DFF, D) for g in got]
    sg = g_conv.reshape(N_DEV, -1)
    o1 = conv_a_w.size
    o2 = o1 + conv_qkv_w.size
    o3 = o2 + conv_ff_w.size
    conv_a_f = sg[:, 0:o1].reshape(N_DEV, DEPTH, 3, AW // N_DEV).transpose(1, 2, 0, 3).reshape(DEPTH, 3, AW)
    conv_q_f = sg[:, o1:o2].reshape(N_DEV, DEPTH, 4, 3 * H * HD // N_DEV).transpose(1, 2, 0, 3).reshape(DEPTH, 4, 3 * H * HD)
    conv_f_f = sg[:, o2:o3].reshape(N_DEV, DEPTH, 3, n_up).transpose(1, 2, 0, 3).reshape(DEPTH, 3, 2 * DFF)

    c_all = jnp.concatenate([g_c[:, 0], jnp.zeros((16 - N_DEV, D), F32)], axis=0)
    n_ada = N_MOD * D // N_DEV
    ada_b_cols = lax.dynamic_slice_in_dim(ada_b, me * n_ada, n_ada, axis=1)[:, None, :]
    mod_sh = _mod_fwd(c_all, ada_w, ada_b_cols)
    mod_all = _all_gather([mod_sh.reshape(DEPTH * 16, n_ada)], "gather_mod", in_vmem=True)[0]
    mod_all = mod_all.reshape(N_DEV, DEPTH, 16, n_ada)
    mod_mine = lax.dynamic_index_in_dim(mod_all, me, axis=2, keepdims=False)
    mod_full = mod_mine.transpose(1, 0, 2).reshape(DEPTH, N_MOD, D)

    tags = ["w_in", "w_out", "w_up", "w_down"]
    received = [dict() for _ in range(DEPTH)]

    def on_grads(i, part, gs_i):
        first_id = DEPTH if part == "ffn" else 2 * DEPTH
        got_i = _rs_exchange_async(gs_i, "rs_exchange_%s_l%d" % (part, i), collective_id=first_id + i)
        received[i].update(zip(("w_up", "w_down") if part == "ffn" else ("w_in", "w_out"), got_i))

    loss_local, dx, _, _, _, _, g_small, d_norm_f = _local_fwd_bwd(
        x, target, mod_full, (norm1_w, norm2_w, norm_a_w, a_log, dt_bias, norm_dn_w, norm_f_w),
        (w_in_f, w_out_f, w_up_f, w_down_f, conv_a_f, conv_q_f, conv_f_f), on_grads)
    loss = lax.psum(loss_local, ("x", "y", "c"))
    grad_x = dx[None]

    keys = ["dmod", "norm1", "norm2", "norm_a", "a_log", "dt_bias", "norm_dn", "conv_a", "conv_qkv", "conv_ff"]
    stacked = {k: jnp.stack([g_small[i][k] for i in range(DEPTH)]) for k in keys}
    flat_parts = [stacked[k].reshape(-1) for k in keys] + [d_norm_f]
    sizes = [int(t.shape[0]) for t in flat_parts]
    sflat = _pad_rows(jnp.concatenate(flat_parts), SUB * LANES).reshape(-1, LANES)
    sall = _all_gather([sflat], "gather_small_grads", in_vmem=True)[0]
    ssum = _sum_devices(sall).reshape(-1)
    so = [0]
    for sz in sizes:
        so.append(so[-1] + sz)
    red = {k: ssum[so[n]:so[n + 1]].reshape(stacked[k].shape) for n, k in enumerate(keys)}
    g_norm_f = ssum[so[len(keys)]:so[len(keys) + 1]]
    dmod_all = sall[:, 0:sizes[0] // LANES, :].reshape(N_DEV, DEPTH, N_MOD * D)

    g_ada_b = red["dmod"].reshape(DEPTH, N_MOD * D)
    dmod_cols = lax.dynamic_slice_in_dim(dmod_all, me * n_ada, n_ada, axis=2).transpose(1, 0, 2)
    dmod_cols = jnp.concatenate([dmod_cols, jnp.zeros((DEPTH, 16 - N_DEV, n_ada), F32)], axis=1)
    g_ada_w = _mod_bwd(c_all, dmod_cols)
    g_conv_a = lax.dynamic_slice_in_dim(red["conv_a"], me * (AW // N_DEV), AW // N_DEV, axis=2)
    g_conv_qkv = lax.dynamic_slice_in_dim(red["conv_qkv"], me * (3 * H * HD // N_DEV), 3 * H * HD // N_DEV, axis=2)
    g_conv_ff = lax.dynamic_slice_in_dim(red["conv_ff"], me * n_up, n_up, axis=2)

    mine = [jnp.stack([_rs_sum(received[i][t], "rs_sum_" + t) for i in range(DEPTH)]) for t in tags]
    g_w_in = mine[0][:, :, :n_in]
    g_w_out = mine[1]
    g_w_up = mine[2][:, :, :n_up]
    g_w_down = mine[3]

    grads = dict(ada_w=g_ada_w, ada_b=g_ada_b, norm1_w=red["norm1"], w_in=g_w_in, conv_a_w=g_conv_a,
                 norm_a_w=red["norm_a"], conv_qkv_w=g_conv_qkv, a_log=red["a_log"], dt_bias=red["dt_bias"],
                 norm_dn_w=red["norm_dn"], w_out=g_w_out, norm2_w=red["norm2"], w_up=g_w_up, conv_ff_w=g_conv_ff,
                 w_down=g_w_down, norm_f_w=g_norm_f)
    weights = dict(ada_w=ada_w, ada_b=ada_b, norm1_w=norm1_w, w_in=w_in, conv_a_w=conv_a_w, norm_a_w=norm_a_w,
                   conv_qkv_w=conv_qkv_w, a_log=a_log, dt_bias=dt_bias, norm_dn_w=norm_dn_w, w_out=w_out,
                   norm2_w=norm2_w, w_up=w_up, conv_ff_w=conv_ff_w, w_down=w_down, norm_f_w=norm_f_w)
    ms = dict(ada_w=m_ada_w, ada_b=m_ada_b, norm1_w=m_norm1_w, w_in=m_w_in, conv_a_w=m_conv_a_w, norm_a_w=m_norm_a_w,
              conv_qkv_w=m_conv_qkv_w, a_log=m_a_log, dt_bias=m_dt_bias, norm_dn_w=m_norm_dn_w, w_out=m_w_out,
              norm2_w=m_norm2_w, w_up=m_w_up, conv_ff_w=m_conv_ff_w, w_down=m_w_down, norm_f_w=m_norm_f_w)
    vs_ = dict(ada_w=v_ada_w, ada_b=v_ada_b, norm1_w=v_norm1_w, w_in=v_w_in, conv_a_w=v_conv_a_w, norm_a_w=v_norm_a_w,
               conv_qkv_w=v_conv_qkv_w, a_log=v_a_log, dt_bias=v_dt_bias, norm_dn_w=v_norm_dn_w, w_out=v_w_out,
               norm2_w=v_norm2_w, w_up=v_w_up, conv_ff_w=v_conv_ff_w, w_down=v_w_down, norm_f_w=v_norm_f_w)
    names = list(weights)
    big_names = ["ada_w", "w_in", "w_out", "w_up", "w_down"]
    delta, new_m, new_v = {}, {}, {}
    for n in big_names:
        shp = weights[n].shape
        two = lambda t: t.reshape(-1, shp[-1])
        dl, nm, nv = _adamw(two(weights[n]), two(grads[n]), two(ms[n]), two(vs_[n]), "adamw_" + n)
        delta[n], new_m[n], new_v[n] = dl.reshape(shp), nm.reshape(shp), nv.reshape(shp)
    small_names = [n for n in names if n not in big_names]

    def pack(dct):
        return _pad_rows(jnp.concatenate([dct[n].reshape(-1) for n in small_names]), SUB * LANES).reshape(-1, LANES)

    dl, nm, nv = _adamw(pack(weights), pack(grads), pack(ms), pack(vs_), "adamw_small")
    off = 0
    for n in small_names:
        sz, shp = weights[n].size, weights[n].shape
        delta[n] = dl.reshape(-1)[off:off + sz].reshape(shp)
        new_m[n] = nm.reshape(-1)[off:off + sz].reshape(shp)
        new_v[n] = nv.reshape(-1)[off:off + sz].reshape(shp)
        off += sz

    return (loss, grad_x, *[grads[n] for n in names], *[delta[n] for n in names],
            *[new_m[n] for n in names], *[new_v[n] for n in names])
```

```python
import functools
import math

import jax
import jax.numpy as jnp
from jax import lax
from jax.experimental import pallas as pl
from jax.experimental.pallas import tpu as pltpu
from jax.experimental.pallas import tpu_sc as plsc

F32 = jnp.float32
MXU = jnp.bfloat16

D = 1024
DEPTH = 4
N_MOD = 6
AW = 512
A_GROUP = 64
H = 4
HD = 128
CK = 64
DFF = 2816
P_IN = 3592
P_PAD = 3712
EPS = 1e-6
N_DEV = 8
LANES = 128
SUB = 8
VMEM_LIMIT = 56 * 1024 * 1024

ADAM_LR, ADAM_B1, ADAM_B2, ADAM_EPS, ADAM_WD, ADAM_STEP = 0.001, 0.9, 0.999, 1e-08, 0.01, 10

NN = ((1,), (0,))
NT = ((1,), (1,))
TN = ((0,), (0,))
HI = lax.Precision.HIGHEST
MESH = pl.DeviceIdType.MESH


def _dot(a, b, dims, prec=None):
    if prec is None:
        a = a.astype(MXU) if a.dtype == F32 else a
        b = b.astype(MXU) if b.dtype == F32 else b
    return lax.dot_general(a, b, (dims, ((), ())), precision=prec, preferred_element_type=F32)


def _params(n_grid=0, limit=VMEM_LIMIT):
    sem = ("arbitrary",) * n_grid if n_grid else None
    return pltpu.CompilerParams(dimension_semantics=sem, vmem_limit_bytes=limit)


def _tile(n, want):
    if n <= want:
        return n
    t = want - want % SUB
    while n % t:
        t -= SUB
    assert t > 0, (n, want)
    return t


def _full(shape):
    nd = len(shape)
    return pl.BlockSpec(shape, lambda *_: (0,) * nd)


def _sig(x):
    return jax.nn.sigmoid(x)


def _rms(x):
    r = lax.rsqrt(jnp.mean(x * x, axis=-1, keepdims=True) + EPS)
    return x * r, r


def _rms_bwd(dn, n, r):
    return r * (dn - n * jnp.mean(dn * n, axis=-1, keepdims=True))


def _l2_bwd(dn, n, r):
    return r * (dn - n * jnp.sum(dn * n, axis=-1, keepdims=True))


def _sum0(x):
    return jnp.sum(x, axis=0, keepdims=True)


def _shift_down(x, s, halo):
    ext = jnp.concatenate([halo, x], axis=0)
    return pltpu.roll(ext, s, 0)[SUB:, :]


def _shift_up(x, s, halo):
    t = x.shape[0]
    ext = jnp.concatenate([x, halo], axis=0)
    return pltpu.roll(ext, t + SUB - s, 0)[:t, :]


def _conv_fwd(x, w_ref, width, halo):
    sh = [x] + [_shift_down(x, s, halo) for s in range(1, width)]
    out = w_ref[width - 1:width, :] * sh[0]
    for s in range(1, width):
        out = out + w_ref[width - 1 - s:width - s, :] * sh[s]
    return out, sh


def _blockdiag_mean(n, group):
    r = lax.shift_right_logical(lax.broadcasted_iota(jnp.int32, (n, n), 0), int(math.log2(group)))
    c = lax.shift_right_logical(lax.broadcasted_iota(jnp.int32, (n, n), 1), int(math.log2(group)))
    return jnp.where(r == c, 1.0 / group, 0.0).astype(F32)


def _softplus(x):
    return jnp.maximum(x, 0.0) + jnp.log(1.0 + jnp.exp(-jnp.abs(x)))


def _my_place():
    return lax.axis_index("x"), lax.axis_index("y"), lax.axis_index("c")


def _all_gather(shards, name, in_vmem):
    nt = len(shards)

    def body(*refs):
        x_refs, out_refs = refs[:nt], refs[nt:2 * nt]
        send_sems, recv_sems, local_sems = refs[2 * nt:]
        x, y, c = _my_place()
        me, sibling = (x, y, c), (x, y, 1 - c)
        chips = [(1 - x, y), (x, 1 - y), (1 - x, 1 - y)]
        everything = []
        for t in range(nt):
            x_ref, out_ref = x_refs[t], out_refs[t]

            def blk(px, py, pc, out_ref=out_ref):
                return out_ref.at[4 * px + 2 * py + pc]

            def copy(k, block, to, src=None, t=t, blk=blk):
                return pltpu.make_async_remote_copy(
                    src_ref=blk(*block) if src is None else src, dst_ref=blk(*block),
                    send_sem=send_sems.at[7 * t + k], recv_sem=recv_sems.at[7 * t + k], device_id=to, device_id_type=MESH)

            mine = pltpu.make_async_copy(x_ref, blk(*me), local_sems.at[t])
            mine.start()
            first = [copy(0, me, sibling, src=x_ref)]
            first += [copy(1 + j, me, (*chip, c), src=x_ref) for j, chip in enumerate(chips)]
            for cp in first:
                cp.start()
            everything.append((copy, mine, first))
        sends = []
        for copy, mine, first in everything:
            passed = [copy(4 + j, (*chip, c), sibling) for j, chip in enumerate(chips)]
            for j, chip in enumerate(chips):
                copy(1 + j, (*chip, c), me).wait_recv()
                passed[j].start()
            sends += first + passed
        for copy, mine, first in everything:
            copy(0, sibling, me).wait_recv()
            for j, chip in enumerate(chips):
                copy(4 + j, (*chip, 1 - c), me).wait_recv()
        for cp in sends:
            cp.wait_send()
        for copy, mine, first in everything:
            mine.wait()

    space = pltpu.VMEM if in_vmem else pl.ANY
    return pl.pallas_call(
        body, name=name,
        out_shape=[jax.ShapeDtypeStruct((N_DEV,) + s.shape, s.dtype) for s in shards],
        in_specs=[pl.BlockSpec(memory_space=space)] * nt,
        out_specs=[pl.BlockSpec(memory_space=space)] * nt,
        scratch_shapes=[pltpu.SemaphoreType.DMA((7 * nt,)), pltpu.SemaphoreType.DMA((7 * nt,)),
                        pltpu.SemaphoreType.DMA((nt,))],
        compiler_params=pltpu.CompilerParams(vmem_limit_bytes=VMEM_LIMIT),
    )(*shards)


def _all_gather_async(shards, name, collective_id):
    nt = len(shards)
    hbm = pltpu.MemorySpace.HBM
    x_refs = [jax.new_ref(s, memory_space=hbm) for s in shards]
    out_refs = [jax.empty_ref(jax.ShapeDtypeStruct((N_DEV,) + s.shape, s.dtype), memory_space=hbm) for s in shards]

    @pl.kernel(mesh=plsc.ScalarSubcoreMesh(axis_name="sequencer", num_cores=1), name=name,
               scratch_types=(pltpu.SemaphoreType.DMA((7 * nt,)), pltpu.SemaphoreType.DMA((7 * nt,)),
                              pltpu.SemaphoreType.DMA((nt,))),
               compiler_params=pltpu.CompilerParams(collective_id=collective_id))
    def launch(send_sems, recv_sems, local_sems):
        x, y, c = _my_place()
        me, sibling = (x, y, c), (x, y, 1 - c)
        chips = [(1 - x, y), (x, 1 - y), (1 - x, 1 - y)]
        barrier = pltpu.get_barrier_semaphore()
        for peer in [sibling] + [(*chip, c) for chip in chips]:
            pl.semaphore_signal(barrier, inc=1, device_id=peer, device_id_type=MESH)
        pl.semaphore_wait(barrier, 4)
        everything = []
        for t in range(nt):
            x_ref, out_ref = x_refs[t], out_refs[t]

            def blk(px, py, pc, out_ref=out_ref):
                return out_ref.at[4 * px + 2 * py + pc]

            def copy(k, block, to, src=None, t=t, blk=blk):
                return pltpu.make_async_remote_copy(
                    src_ref=blk(*block) if src is None else src, dst_ref=blk(*block),
                    send_sem=send_sems.at[7 * t + k], recv_sem=recv_sems.at[7 * t + k], device_id=to, device_id_type=MESH)

            mine = pltpu.make_async_copy(x_ref, blk(*me), local_sems.at[t])
            mine.start()
            first = [copy(0, me, sibling, src=x_ref)]
            first += [copy(1 + j, me, (*chip, c), src=x_ref) for j, chip in enumerate(chips)]
            for cp in first:
                cp.start()
            everything.append((copy, mine, first))
        sends = []
        for copy, mine, first in everything:
            passed = [copy(4 + j, (*chip, c), sibling) for j, chip in enumerate(chips)]
            for j, chip in enumerate(chips):
                copy(1 + j, (*chip, c), me).wait_recv()
                passed[j].start()
            sends += first + passed
        for copy, mine, first in everything:
            copy(0, sibling, me).wait_recv()
            for j, chip in enumerate(chips):
                copy(4 + j, (*chip, 1 - c), me).wait_recv()
        for cp in sends:
            cp.wait_send()
        for copy, mine, first in everything:
            mine.wait()

    launch()
    return [r[...] for r in out_refs]


def _rs_exchange_async(srcs, name, collective_id):
    nt = len(srcs)
    hbm = pltpu.MemorySpace.HBM
    src_refs = [jax.new_ref(s, memory_space=hbm) for s in srcs]
    out_refs = [jax.empty_ref(jax.ShapeDtypeStruct(s.shape, s.dtype), memory_space=hbm) for s in srcs]
    flips = [(fx, fy, fc) for fx in (0, 1) for fy in (0, 1) for fc in (0, 1)][1:]

    @pl.kernel(mesh=plsc.ScalarSubcoreMesh(axis_name="sequencer", num_cores=1), name=name,
               scratch_types=(pltpu.SemaphoreType.DMA((7 * nt,)), pltpu.SemaphoreType.DMA((7 * nt,)),
                              pltpu.SemaphoreType.DMA((nt,))),
               compiler_params=pltpu.CompilerParams(collective_id=collective_id))
    def launch(send_sems, recv_sems, local_sems):
        x, y, c = _my_place()
        me = 4 * x + 2 * y + c
        peers = [(1 - x if fx else x, 1 - y if fy else y, 1 - c if fc else c) for fx, fy, fc in flips]
        barrier = pltpu.get_barrier_semaphore()
        for peer in peers:
            pl.semaphore_signal(barrier, inc=1, device_id=peer, device_id_type=MESH)
        pl.semaphore_wait(barrier, len(peers))
        own = [pltpu.make_async_copy(src_refs[t].at[me], out_refs[t].at[me], local_sems.at[t]) for t in range(nt)]
        copies = [pltpu.make_async_remote_copy(
            src_ref=src_refs[t].at[4 * px + 2 * py + pc], dst_ref=out_refs[t].at[me],
            send_sem=send_sems.at[7 * t + f], recv_sem=recv_sems.at[7 * t + f],
            device_id=(px, py, pc), device_id_type=MESH) for t in range(nt) for f, (px, py, pc) in enumerate(peers)]
        for cp in own + copies:
            cp.start()
        for cp in copies + own:
            cp.wait()

    launch()
    return [r[...] for r in out_refs]


def _rs_sum(recv, name):
    _, r, n = recv.shape
    tr = _tile(r, 512)

    def body(r_ref, o_ref):
        s = r_ref[0].astype(F32)
        for k in range(1, N_DEV):
            s = s + r_ref[k].astype(F32)
        o_ref[...] = s

    return pl.pallas_call(
        body, name=name, grid=(r // tr,),
        in_specs=[pl.BlockSpec((N_DEV, tr, n), lambda i: (0, i, 0))],
        out_specs=pl.BlockSpec((tr, n), lambda i: (i, 0)),
        out_shape=jax.ShapeDtypeStruct((r, n), F32), compiler_params=_params(1),
    )(recv)


def _shard_windows(n_shard, count, first=0):
    out = []
    for k in range(first, first + count):
        off = n_shard * k
        a, s = off // LANES, off % LANES
        out.append((a, s, -(-(s + n_shard) // LANES) * LANES))
    return out


def _fit_lanes(x, width):
    have = x.shape[1]
    if have < width:
        return jnp.concatenate([x, jnp.zeros((x.shape[0], width - have), x.dtype)], axis=-1)
    return x[:, :width]


def _interleave_cols(g, n_shard, w_out, name):
    nd, nl, rows, wpad = g.shape
    rb = _tile(rows, 256)
    wins = _shard_windows(n_shard, nd)

    def body(g_ref, o_ref, acc):
        acc[...] = jnp.zeros_like(acc)
        for k, (a, s, win) in enumerate(wins):
            xk = _fit_lanes(g_ref[k].astype(F32), win)
            if s:
                xk = pltpu.roll(xk, s, 1)
            acc[:, a * LANES:a * LANES + win] += xk
        o_ref[...] = acc[...].astype(o_ref.dtype)

    return pl.pallas_call(
        body, name=name, grid=(nl, rows // rb),
        in_specs=[pl.BlockSpec((nd, None, rb, wpad), lambda l, i: (0, l, i, 0))],
        out_specs=pl.BlockSpec((None, rb, w_out), lambda l, i: (l, i, 0)),
        out_shape=jax.ShapeDtypeStruct((nl, rows, w_out), g.dtype),
        scratch_shapes=[pltpu.VMEM((rb, w_out), F32)],
        compiler_params=_params(2),
    )(g)


def _sum_devices(g):
    _, r, n = g.shape

    def body(g_ref, o_ref):
        s = g_ref[0]
        for t in range(1, N_DEV):
            s = s + g_ref[t]
        o_ref[...] = s

    return pl.pallas_call(
        body, name="sum_devices", out_shape=jax.ShapeDtypeStruct((r, n), F32),
        in_specs=[pl.BlockSpec(memory_space=pltpu.VMEM)], out_specs=pl.BlockSpec(memory_space=pltpu.VMEM),
        compiler_params=pltpu.CompilerParams(vmem_limit_bytes=VMEM_LIMIT),
    )(g)


def _mod_fwd(c_all, ada_w, ada_b_cols):
    nl, _, nc = ada_w.shape

    def body(c_ref, w_ref, b_ref, o_ref):
        cv = c_ref[...]
        act = (cv * _sig(cv)).astype(MXU)
        o_ref[...] = _dot(act, w_ref[...].astype(MXU), NN) + b_ref[...]

    return pl.pallas_call(
        body, name="mod_fwd", grid=(nl,),
        in_specs=[_full((16, D)), pl.BlockSpec((None, D, nc), lambda i: (i, 0, 0)),
                  pl.BlockSpec((None, 1, nc), lambda i: (i, 0, 0))],
        out_specs=pl.BlockSpec((None, 16, nc), lambda i: (i, 0, 0)),
        out_shape=jax.ShapeDtypeStruct((nl, 16, nc), F32), compiler_params=_params(1),
    )(c_all, ada_w, ada_b_cols)


def _mod_bwd(c_all, dmod_cols):
    nl, _, nc = dmod_cols.shape

    def body(c_ref, d_ref, o_ref):
        cv = c_ref[...]
        act = (cv * _sig(cv)).astype(MXU)
        o_ref[...] = _dot(act, d_ref[...].astype(MXU), TN)

    return pl.pallas_call(
        body, name="mod_bwd", grid=(nl,),
        in_specs=[_full((16, D)), pl.BlockSpec((None, 16, nc), lambda i: (i, 0, 0))],
        out_specs=pl.BlockSpec((None, D, nc), lambda i: (i, 0, 0)),
        out_shape=jax.ShapeDtypeStruct((nl, D, nc), F32), compiler_params=_params(1),
    )(c_all, dmod_cols)


def _gate_small(s, sp_ref):
    lane = lax.broadcasted_iota(jnp.int32, s.shape, 1)
    a = -jnp.exp(sp_ref[0:1, :])
    xb = s + sp_ref[1:2, :]
    beta = _sig(s)
    g = a * _softplus(xb)
    return lane, a, xb, beta, g


def _in_pre_fwd(x, modrows, vec, w_in, pa, cq, sp):
    L = x.shape[0]
    T = _tile(L, 512)
    scale = HD ** -0.5
    w3 = 3 * AW + 3 * H * HD

    def body(x_ref, mod_ref, vec_ref, w_ref, pa_ref, cq_ref, sp_ref,
             p_ref, h_ref, qn_ref, kn_ref, vs_ref, gb_ref, ya_ref, cu_ref, qc_ref, u_carry, q_carry):
        @pl.when(pl.program_id(0) == 0)
        def _():
            u_carry[...] = jnp.zeros_like(u_carry)
            q_carry[...] = jnp.zeros_like(q_carry)

        n, _ = _rms(x_ref[...])
        hb = (n * vec_ref[0:1, :] * (1.0 + mod_ref[1:2, :]) + mod_ref[0:1, :]).astype(MXU)
        h_ref[...] = hb
        pm_a = _dot(hb, w_ref[:, 0:3 * AW], NN)
        p_ref[:, 0:3 * AW] = pm_a
        pm_q = _dot(hb, w_ref[:, 3 * AW:w3], NN)
        p_ref[:, 3 * AW:w3] = pm_q

        a_b = pm_a[:, 0:AW]
        u = pm_a[:, AW:2 * AW] * pm_a[:, 2 * AW:3 * AW]
        cu, _ = _conv_fwd(u, pa_ref, 3, u_carry[...])
        cu_ref[...] = cu.astype(MXU)
        u_carry[...] = u[T - SUB:T, :]
        yp = a_b * cu
        ms = _dot_f32(yp * yp, _blockdiag_mean(AW, A_GROUP), NN, exact="b")
        ya_ref[...] = (yp * lax.rsqrt(ms + EPS) * pa_ref[3:4, :]).astype(MXU)

        pm_z = _dot(hb, w_ref[:, w3:P_PAD], NN)
        p_ref[:, w3:P_PAD] = pm_z
        qkv = pm_q
        qc, _ = _conv_fwd(qkv, cq_ref, 4, q_carry[...])
        qc_ref[...] = qc.astype(MXU)
        q_carry[...] = qkv[T - SUB:T, :]
        qs = qc * _sig(qc)
        for h in range(H):
            q = qs[:, h * HD:(h + 1) * HD]
            qn_ref[:, h * HD:(h + 1) * HD] = q * (lax.rsqrt(jnp.sum(q * q, axis=-1, keepdims=True) + EPS) * scale)
            k = qs[:, (H + h) * HD:(H + h + 1) * HD]
            kn_ref[:, h * HD:(h + 1) * HD] = k * lax.rsqrt(jnp.sum(k * k, axis=-1, keepdims=True) + EPS)
        vs_ref[...] = qs[:, 2 * H * HD:3 * H * HD]

        lane, _, _, beta, g = _gate_small(pm_z[:, H * HD:H * HD + LANES], sp_ref)
        gb_ref[...] = jnp.where(lane < H, beta, jnp.where(lane < 2 * H, g, 0.0))

    row = lambda i: (i, 0)
    return pl.pallas_call(
        body, name="in_pre_fwd", grid=(L // T,),
        in_specs=[pl.BlockSpec((T, D), row), _full((SUB, D)), _full((SUB, D)), _full((D, P_PAD)),
                  _full((SUB, AW)), _full((SUB, 3 * H * HD)), _full((SUB, LANES))],
        out_specs=[pl.BlockSpec((T, P_PAD), row), pl.BlockSpec((T, D), row)]
        + [pl.BlockSpec((T, H * HD), row)] * 3 + [pl.BlockSpec((T, LANES), row), pl.BlockSpec((T, AW), row),
                                                  pl.BlockSpec((T, AW), row), pl.BlockSpec((T, 3 * H * HD), row)],
        out_shape=[jax.ShapeDtypeStruct((L, P_PAD), F32), jax.ShapeDtypeStruct((L, D), MXU)]
        + [jax.ShapeDtypeStruct((L, H * HD), F32)] * 3
        + [jax.ShapeDtypeStruct((L, LANES), F32), jax.ShapeDtypeStruct((L, AW), MXU),
           jax.ShapeDtypeStruct((L, AW), MXU), jax.ShapeDtypeStruct((L, 3 * H * HD), MXU)],
        scratch_shapes=[pltpu.VMEM((SUB, AW), F32), pltpu.VMEM((SUB, 3 * H * HD), F32)],
        compiler_params=_params(1),
    )(x, modrows, vec, w_in, pa, cq, sp)


def _gdr_masks():
    r = lax.broadcasted_iota(jnp.int32, (CK, CK), 0)
    c = lax.broadcasted_iota(jnp.int32, (CK, CK), 1)
    return r >= c, r > c


def _head_cols(gbt, h):
    return gbt[:, h:h + 1], gbt[:, H + h:H + h + 1]


def _split(x, parts):
    out = []
    for _ in range(parts):
        hi = x.astype(jnp.bfloat16)
        out.append(hi)
        x = x - hi.astype(F32)
    return out


def _dot_f32(a, b, dims, exact=None):
    if exact == "a":
        ab = a.astype(jnp.bfloat16)
        return sum(_dot(ab, t, dims) for t in _split(b, 3))
    if exact == "b":
        bb = b.astype(jnp.bfloat16)
        return sum(_dot(t, bb, dims) for t in _split(a, 3))
    ah, al = _split(a, 2)
    bh, bl = _split(b, 2)
    return _dot(ah, bh, dims) + _dot(ah, bl, dims) + _dot(al, bh, dims)


def _gdr_consts():
    causal, strict = _gdr_masks()
    return dict(causal=causal, strict=strict, tril=jnp.where(causal, 1.0, 0.0).astype(F32),
                eye=jnp.where(causal & jnp.logical_not(strict), 1.0, 0.0).astype(F32),
                bcast=jnp.full((CK, HD), 1.0 / HD, F32))


def _dots(a, b, dims):
    return [_dot(x, y, dims) for x, y in zip(a, b)]


def _dots_f32(a, b, dims, exact=None):
    n = len(a)
    if exact == "a":
        lhs = [[x.astype(jnp.bfloat16)] * 3 for x in a]
        rhs = [_split(y, 3) for y in b]
    elif exact == "b":
        lhs = [_split(x, 3) for x in a]
        rhs = [[y.astype(jnp.bfloat16)] * 3 for y in b]
    else:
        sa = [_split(x, 2) for x in a]
        sb = [_split(y, 2) for y in b]
        lhs = [[s[0], s[0], s[1]] for s in sa]
        rhs = [[s[0], s[1], s[0]] for s in sb]
    terms = [[_dot(lhs[i][t], rhs[i][t], dims) for i in range(n)] for t in range(3)]
    return [terms[0][i] + terms[1][i] + terms[2][i] for i in range(n)]


def _gdr_local(q, k, v, beta, g, cst, tinv=None):
    n = len(q)
    R = range(n)
    causal, strict = cst["causal"], cst["strict"]
    gc = _dots_f32([cst["tril"]] * n, [jnp.broadcast_to(g[i], (CK, HD)) for i in R], NN, exact="a")
    g_row = _dots_f32([cst["bcast"]] * n, gc, NT, exact="a")
    decay = [jnp.where(causal, jnp.exp(jnp.where(causal, gc[i][:, 0:CK] - g_row[i], 0.0)), 0.0) for i in R]
    eg = [jnp.exp(gc[i]) for i in R]
    gl = [gc[i][CK - 1:CK, :] for i in R]
    ek = [jnp.exp(gl[i] - gc[i]) for i in R]
    cd = [jnp.exp(gl[i]) for i in R]
    kb = [k[i] * beta[i] for i in R]
    pk = _dots(kb, k, NT)
    if tinv is None:
        xp = [-jnp.where(strict, pk[i] * decay[i], 0.0) for i in R]
        tinv = [cst["eye"] + xp[i] for i in R]
        for _ in range(5):
            xp = _dots_f32(xp, xp, NN)
            tx = _dots_f32(tinv, xp, NN)
            tinv = [tinv[i] + tx[i] for i in R]
    u = _dots(tinv, [v[i] * beta[i] for i in R], NN)
    w = _dots(tinv, [kb[i] * eg[i] for i in R], NN)
    qk = _dots(q, k, NT)
    intra = [jnp.where(causal, qk[i] * decay[i], 0.0) for i in R]
    return dict(decay=decay, eg=eg, ek=ek, cd=cd, kb=kb, pk=pk, tinv=tinv, u=u, w=w, qk=qk, intra=intra,
                q_dec=[q[i] * eg[i] for i in R], k_dec=[k[i] * ek[i] for i in R])


GDR_SUB = 8


def _gdr_fwd(qn, kn, vs, gb):
    L = qn.shape[0]
    nc = L // CK
    cb = min(8, nc)
    rb = cb * CK
    nb = nc // cb
    nsub = GDR_SUB if cb % GDR_SUB == 0 else 1

    def body(q_ref, k_ref, v_ref, gb_ref, o_ref, st_ref, ti_ref, s_ref):
        @pl.when(pl.program_id(0) == 0)
        def _():
            s_ref[...] = jnp.zeros_like(s_ref)

        cst = _gdr_consts()
        heads = range(H)

        def group(gi, carry):
            rows = [pl.ds(pl.multiple_of((gi * nsub + j) * CK, CK), CK) for j in range(nsub)]
            chains = [(j, h) for j in range(nsub) for h in heads]
            gbt = [gb_ref[rows[j], :] for j in range(nsub)]
            cols = lambda h: slice(h * HD, (h + 1) * HD)
            t = _gdr_local([q_ref[rows[j], cols(h)] for j, h in chains], [k_ref[rows[j], cols(h)] for j, h in chains],
                           [v_ref[rows[j], cols(h)] for j, h in chains],
                           [_head_cols(gbt[j], h)[0] for j, h in chains], [_head_cols(gbt[j], h)[1] for j, h in chains], cst)
            s = [s_ref[h] for h in heads]
            for j in range(nsub):
                at = lambda key: [t[key][j * H + h] for h in heads]
                for h in heads:
                    st_ref[h, gi * nsub + j] = s[h]
                    ti_ref[h, gi * nsub + j] = t["tinv"][j * H + h]
                ws = _dots(at("w"), s, NN)
                v_new = [u_h - ws_h for u_h, ws_h in zip(at("u"), ws)]
                o_s = _dots(at("q_dec"), s, NN)
                o_v = _dots(at("intra"), v_new, NN)
                kv = _dots(at("k_dec"), v_new, TN)
                cd = at("cd")
                for h in heads:
                    o_ref[rows[j], cols(h)] = o_s[h] + o_v[h]
                s = [s[h] * cd[h] + kv[h] for h in heads]
            for h in heads:
                s_ref[h] = s[h]
            return carry

        lax.fori_loop(0, cb // nsub, group, 0)

    blk = pl.BlockSpec((rb, H * HD), lambda b: (b, 0))
    return pl.pallas_call(
        body, name="gdr_fwd", grid=(nb,),
        in_specs=[blk, blk, blk, pl.BlockSpec((rb, LANES), lambda b: (b, 0))],
        out_specs=[blk, pl.BlockSpec((H, cb, HD, HD), lambda b: (0, b, 0, 0)),
                   pl.BlockSpec((H, cb, CK, CK), lambda b: (0, b, 0, 0))],
        out_shape=[jax.ShapeDtypeStruct((L, H * HD), F32), jax.ShapeDtypeStruct((H, nc, HD, HD), F32),
                   jax.ShapeDtypeStruct((H, nc, CK, CK), F32)],
        scratch_shapes=[pltpu.VMEM((H, HD, HD), F32)],
        compiler_params=_params(1),
    )(qn, kn, vs, gb)


def _gdr_bwd(qn, kn, vs, gb, states, tinvs, do):
    L = qn.shape[0]
    nc = L // CK
    cb = min(8, nc)
    rb = cb * CK
    nb = nc // cb
    nsub = GDR_SUB if cb % GDR_SUB == 0 else 1

    def body(q_ref, k_ref, v_ref, gb_ref, st_ref, ti_ref, do_ref, dq_ref, dk_ref, dv_ref, dgb_ref, ds_ref):
        @pl.when(pl.program_id(0) == 0)
        def _():
            ds_ref[...] = jnp.zeros_like(ds_ref)

        cst = _gdr_consts()
        causal, strict = cst["causal"], cst["strict"]
        ones = jnp.ones((CK, HD), F32)
        row = lax.broadcasted_iota(jnp.int32, (CK, HD), 0)
        lane = lax.broadcasted_iota(jnp.int32, (CK, LANES), 1)

        heads = range(H)
        rsum = lambda x: jnp.sum(x, axis=-1, keepdims=True)

        def group(gj, carry):
            gi = cb // nsub - 1 - gj
            rows = [pl.ds(pl.multiple_of((gi * nsub + j) * CK, CK), CK) for j in range(nsub)]
            chains = [(j, h) for j in range(nsub) for h in heads]
            gbt = [gb_ref[rows[j], :] for j in range(nsub)]
            cols = lambda h: slice(h * HD, (h + 1) * HD)
            q_all = [q_ref[rows[j], cols(h)] for j, h in chains]
            k_all = [k_ref[rows[j], cols(h)] for j, h in chains]
            v_all = [v_ref[rows[j], cols(h)] for j, h in chains]
            beta_all = [_head_cols(gbt[j], h)[0] for j, h in chains]
            t = _gdr_local(q_all, k_all, v_all, beta_all, [_head_cols(gbt[j], h)[1] for j, h in chains], cst,
                           tinv=[ti_ref[h, gi * nsub + j] for j, h in chains])
            ds_out = [ds_ref[h] for h in heads]
            for j in reversed(range(nsub)):
                at = lambda key: [t[key][j * H + h] for h in heads]
                pick = lambda lst: [lst[j * H + h] for h in heads]
                q, k, v, beta = pick(q_all), pick(k_all), pick(v_all), pick(beta_all)
                u, w, tinv, decay = at("u"), at("w"), at("tinv"), at("decay")
                eg, ek, cd, kb = at("eg"), at("ek"), at("cd"), at("kb")
                q_dec, k_dec, intra, pk, qk = at("q_dec"), at("k_dec"), at("intra"), at("pk"), at("qk")
                s = [st_ref[h, gi * nsub + j] for h in heads]
                dout = [do_ref[rows[j], cols(h)] for h in heads]

                ws = _dots(w, s, NN)
                v_new = [u[h] - ws[h] for h in heads]
                dq_dec = _dots(dout, s, NT)
                qd = _dots(q_dec, dout, TN)
                di = _dots(dout, v_new, NT)
                dintra = [jnp.where(causal, di[h], 0.0) for h in heads]
                ido = _dots(intra, dout, TN)
                kds = _dots(k_dec, ds_out, NN)
                dv_new = [ido[h] + kds[h] for h in heads]
                dk_dec = _dots(v_new, ds_out, NT)
                dcd = [jnp.sum(jnp.sum(ds_out[h] * s[h], axis=1, keepdims=True), axis=0, keepdims=True) for h in heads]
                dvs = _dots(dv_new, s, NT)
                dw = [-dvs[h] for h in heads]
                wdv = _dots(w, dv_new, TN)
                ds_new = [qd[h] + ds_out[h] * cd[h] - wdv[h] for h in heads]
                dru = _dots(tinv, dv_new, TN)
                drw = _dots(tinv, dw, TN)
                dl1 = _dots(dru, u, NT)
                dl2 = _dots(drw, w, NT)
                dlower = [-jnp.where(strict, dl1[h] + dl2[h], 0.0) for h in heads]
                dv = [dru[h] * beta[h] for h in heads]
                dbeta = [rsum(dru[h] * v[h]) for h in heads]
                dgc = [rsum(drw[h] * kb[h]) * eg[h] for h in heads]
                dpk = [dlower[h] * decay[h] for h in heads]
                dqk = [dintra[h] * decay[h] for h in heads]
                dpk_k = _dots(dpk, k, NN)
                dkb = [drw[h] * eg[h] + dpk_k[h] for h in heads]
                dk1 = _dots(dpk, kb, TN)
                dq1 = _dots(dqk, k, NN)
                dk2 = _dots(dqk, q, TN)
                m = [(dlower[h] * pk[h] + dintra[h] * qk[h]) * decay[h] for h in heads]
                mcol = _dots_f32(m, [ones] * H, TN, exact="b")
                e = [rsum(dk_dec[h] * k_dec[h]) for h in heads]
                dgl = [jnp.sum(e[h], axis=0, keepdims=True) + dcd[h] * cd[h] for h in heads]
                dgc = [dgc[h] + rsum(m[h]) - mcol[h] + rsum(dq_dec[h] * q_dec[h]) - e[h]
                       + jnp.where(row == CK - 1, dgl[h], 0.0) for h in heads]
                dg = _dots_f32([cst["tril"]] * H, dgc, TN, exact="a")
                dgb = jnp.zeros((CK, LANES), F32)
                for h in heads:
                    dq_ref[rows[j], cols(h)] = dq1[h] + dq_dec[h] * eg[h]
                    dk_ref[rows[j], cols(h)] = dk1[h] + dk2[h] + dk_dec[h] * ek[h] + dkb[h] * beta[h]
                    dv_ref[rows[j], cols(h)] = dv[h]
                    db = dbeta[h] + rsum(dkb[h] * k[h])
                    dgb = dgb + jnp.where(lane == h, db, 0.0) + jnp.where(lane == H + h, dg[h], 0.0)
                dgb_ref[rows[j], :] = dgb
                ds_out = ds_new
            for h in heads:
                ds_ref[h] = ds_out[h]
            return carry

        lax.fori_loop(0, cb // nsub, group, 0)

    blk = pl.BlockSpec((rb, H * HD), lambda b: (nb - 1 - b, 0))
    sblk = pl.BlockSpec((rb, LANES), lambda b: (nb - 1 - b, 0))
    return pl.pallas_call(
        body, name="gdr_bwd", grid=(nb,),
        in_specs=[blk, blk, blk, sblk, pl.BlockSpec((H, cb, HD, HD), lambda b: (0, nb - 1 - b, 0, 0)),
                  pl.BlockSpec((H, cb, CK, CK), lambda b: (0, nb - 1 - b, 0, 0)), blk],
        out_specs=[blk, blk, blk, sblk],
        out_shape=[jax.ShapeDtypeStruct((L, H * HD), F32)] * 3 + [jax.ShapeDtypeStruct((L, LANES), F32)],
        scratch_shapes=[pltpu.VMEM((H, HD, HD), F32)],
        compiler_params=_params(1),
    )(qn, kn, vs, gb, states, tinvs, do)


def _post_fwd(o, p, ya, x, modrows, sp, w_out):
    L = x.shape[0]
    T = _tile(L, 512)

    def body(o_ref, z_ref, ya_ref, x_ref, mod_ref, sp_ref, w_ref, y_ref, x2_ref, yb_ref):
        ndw = sp_ref[2:3, :]
        z = z_ref[...]
        sz = z * _sig(z)
        parts = []
        for h in range(H):
            n, _ = _rms(o_ref[:, h * HD:(h + 1) * HD])
            parts.append(n * ndw * sz[:, h * HD:(h + 1) * HD])
        yb = jnp.concatenate(parts, axis=-1).astype(MXU)
        yb_ref[...] = yb
        y = _dot(ya_ref[...], w_ref[0:AW, :], NN) + _dot(yb, w_ref[AW:2 * AW, :], NN)
        y_ref[...] = y
        x2_ref[...] = x_ref[...] + mod_ref[2:3, :] * y

    row = lambda i: (i, 0)
    zcol = (3 * AW + 3 * H * HD) // (H * HD)
    return pl.pallas_call(
        body, name="post_fwd", grid=(L // T,),
        in_specs=[pl.BlockSpec((T, H * HD), row), pl.BlockSpec((T, H * HD), lambda i: (i, zcol)),
                  pl.BlockSpec((T, AW), row), pl.BlockSpec((T, D), row), _full((SUB, D)), _full((SUB, LANES)),
                  _full((D, D))],
        out_specs=[pl.BlockSpec((T, D), row), pl.BlockSpec((T, D), row), pl.BlockSpec((T, H * HD), row)],
        out_shape=[jax.ShapeDtypeStruct((L, D), F32), jax.ShapeDtypeStruct((L, D), F32),
                   jax.ShapeDtypeStruct((L, H * HD), MXU)],
        compiler_params=_params(1),
    )(o, p, ya, x, modrows, sp, w_out)


FF_COLS = 2
FF_CW = DFF // FF_COLS
FF_ROWS = 512


def _ffn_fwd_half(x2, modrows, vec, w_up, cff, w_down, j, d_prev):
    assert FF_COLS == 2
    L = x2.shape[0]
    T = _tile(L, FF_ROWS)
    nj = FF_COLS
    last = d_prev is not None

    def body(*refs):
        x_ref, mod_ref, vec_ref, wg_ref, wu_ref, cg_ref, cu_ref, wd_ref = refs[:8]
        if last:
            dp_ref, gp_ref, up_ref, gc_ref, uc_ref, f_ref, d_ref, x3_ref, carry_g, carry_u = refs[8:]
        else:
            h_ref, gp_ref, up_ref, gc_ref, uc_ref, f_ref, d_ref, carry_g, carry_u = refs[8:]

        @pl.when(pl.program_id(0) == 0)
        def _():
            carry_g[...] = jnp.zeros_like(carry_g)
            carry_u[...] = jnp.zeros_like(carry_u)

        xv = x_ref[...]
        n, _ = _rms(xv)
        hb = (n * vec_ref[1:2, :] * (1.0 + mod_ref[4:5, :]) + mod_ref[3:4, :]).astype(MXU)
        if not last:
            h_ref[...] = hb
        g = _dot(hb, wg_ref[...], NN)
        u = _dot(hb, wu_ref[...], NN)
        gp_ref[...] = g.astype(MXU)
        up_ref[...] = u.astype(MXU)
        gc, _ = _conv_fwd(g, cg_ref, 3, carry_g[...])
        uc, _ = _conv_fwd(u, cu_ref, 3, carry_u[...])
        carry_g[...] = g[T - SUB:T, :]
        carry_u[...] = u[T - SUB:T, :]
        gc_ref[...] = gc.astype(MXU)
        uc_ref[...] = uc.astype(MXU)
        fb = (gc * _sig(gc) * uc).astype(MXU)
        f_ref[...] = fb
        part = _dot(fb, wd_ref[...], NN)
        if last:
            dv = dp_ref[...] + part
            d_ref[...] = dv
            x3_ref[...] = xv + mod_ref[5:6, :] * dv
        else:
            d_ref[...] = part

    row = lambda i: (i, 0)
    rowD = pl.BlockSpec((T, D), row)
    rowC = pl.BlockSpec((T, FF_CW), row)
    in_specs = [rowD, _full((SUB, D)), _full((SUB, D)),
                pl.BlockSpec((D, FF_CW), lambda i: (0, j)), pl.BlockSpec((D, FF_CW), lambda i: (0, nj + j)),
                pl.BlockSpec((SUB, FF_CW), lambda i: (0, j)), pl.BlockSpec((SUB, FF_CW), lambda i: (0, nj + j)),
                pl.BlockSpec((FF_CW, D), lambda i: (j, 0))]
    half = [jax.ShapeDtypeStruct((L, FF_CW), MXU)] * 5
    args = [x2, modrows, vec, w_up, w_up, cff, cff, w_down]
    if last:
        in_specs.append(rowD)
        args.append(d_prev)
        out_specs = [rowC] * 5 + [rowD, rowD]
        out_shape = half + [jax.ShapeDtypeStruct((L, D), F32), jax.ShapeDtypeStruct((L, D), F32)]
    else:
        out_specs = [rowD] + [rowC] * 5 + [rowD]
        out_shape = [jax.ShapeDtypeStruct((L, D), MXU)] + half + [jax.ShapeDtypeStruct((L, D), F32)]
    return pl.pallas_call(
        body, name="ffn_fwd_last" if last else "ffn_fwd_first", grid=(L // T,),
        in_specs=in_specs, out_specs=out_specs, out_shape=out_shape,
        scratch_shapes=[pltpu.VMEM((SUB, FF_CW), F32), pltpu.VMEM((SUB, FF_CW), F32)],
        compiler_params=_params(1),
    )(*args)


def _ffn_bwd_half(dx3, modrows, gpre, upre, gcv, ucv, cff, w_down, w_up, j, tail):
    assert FF_COLS == 2
    L = dx3.shape[0]
    T = _tile(L, FF_ROWS)
    ni, nj = L // T, FF_COLS
    last = tail is not None

    def body(*refs):
        dx3_ref, mod_ref, gp_ref, up_ref, gc_ref, uc_ref, cg_ref, cu_ref, wd_ref, wg_ref, wu_ref = refs[:11]
        if last:
            (d_ref, x2_ref, vec_ref, dhp_ref, dgp_ref, dup_ref, dx2_ref, accv_ref, dcg_ref, dcu_ref,
             carry_g, carry_u) = refs[11:]
        else:
            dd_ref, dgp_ref, dup_ref, dh_ref, dcg_ref, dcu_ref, carry_g, carry_u = refs[11:]
        i = pl.program_id(0)

        @pl.when(i == 0)
        def _():
            carry_g[...] = jnp.zeros_like(carry_g)
            carry_u[...] = jnp.zeros_like(carry_u)
            dcg_ref[...] = jnp.zeros_like(dcg_ref)
            dcu_ref[...] = jnp.zeros_like(dcu_ref)
            if last:
                accv_ref[...] = jnp.zeros_like(accv_ref)

        dx3v = dx3_ref[...]
        ddb = (mod_ref[5:6, :] * dx3v).astype(MXU)
        if not last:
            dd_ref[...] = ddb
        g, u = gp_ref[...].astype(F32), up_ref[...].astype(F32)
        gc, uc = gc_ref[...].astype(F32), uc_ref[...].astype(F32)
        sg = _sig(gc)
        df = _dot(ddb, wd_ref[...], NT)
        duc = df * (gc * sg)
        dgc = df * uc * (sg * (1.0 + gc * (1.0 - sg)))
        dgs = [dgc] + [_shift_up(dgc, s, carry_g[...]) for s in (1, 2)]
        dus = [duc] + [_shift_up(duc, s, carry_u[...]) for s in (1, 2)]
        for s in range(3):
            dcg_ref[2 - s:3 - s, :] += _sum0(dgs[s] * g)
            dcu_ref[2 - s:3 - s, :] += _sum0(dus[s] * u)
        dg = (cg_ref[2:3, :] * dgs[0] + cg_ref[1:2, :] * dgs[1] + cg_ref[0:1, :] * dgs[2]).astype(MXU)
        du = (cu_ref[2:3, :] * dus[0] + cu_ref[1:2, :] * dus[1] + cu_ref[0:1, :] * dus[2]).astype(MXU)
        carry_g[...] = dgc[0:SUB, :]
        carry_u[...] = duc[0:SUB, :]
        dgp_ref[...] = dg
        dup_ref[...] = du
        dh = _dot(dg, wg_ref[...], NT) + _dot(du, wu_ref[...], NT)
        if last:
            dh = dh + dhp_ref[...]
            accv_ref[0:1, :] += _sum0(dx3v * d_ref[...])
            n, r = _rms(x2_ref[...])
            nw, sc = vec_ref[1:2, :], mod_ref[4:5, :]
            accv_ref[1:2, :] += _sum0(dh)
            accv_ref[2:3, :] += _sum0(dh * n * nw)
            accv_ref[3:4, :] += _sum0(dh * n * (1.0 + sc))
            dx2_ref[...] = _rms_bwd(dh * nw * (1.0 + sc), n, r) + dx3v
        else:
            dh_ref[...] = dh

    row = lambda i: (ni - 1 - i, 0)
    rowD = pl.BlockSpec((T, D), row)
    rowC = pl.BlockSpec((T, FF_CW), row)
    in_specs = [rowD, _full((SUB, D)), rowC, rowC, rowC, rowC,
                pl.BlockSpec((SUB, FF_CW), lambda i: (0, j)), pl.BlockSpec((SUB, FF_CW), lambda i: (0, nj + j)),
                pl.BlockSpec((FF_CW, D), lambda i: (j, 0)),
                pl.BlockSpec((D, FF_CW), lambda i: (0, j)), pl.BlockSpec((D, FF_CW), lambda i: (0, nj + j))]
    args = [dx3, modrows, gpre, upre, gcv, ucv, cff, cff, w_down, w_up, w_up]
    halfb = [jax.ShapeDtypeStruct((L, FF_CW), MXU), jax.ShapeDtypeStruct((L, FF_CW), MXU)]
    dconv = [jax.ShapeDtypeStruct((SUB, FF_CW), F32)] * 2
    if last:
        d, x2, vec, dh_prev = tail
        in_specs += [rowD, rowD, _full((SUB, D)), rowD]
        args += [d, x2, vec, dh_prev]
        out_specs = [rowC, rowC, rowD, _full((SUB, D)), _full((SUB, FF_CW)), _full((SUB, FF_CW))]
        out_shape = halfb + [jax.ShapeDtypeStruct((L, D), F32), jax.ShapeDtypeStruct((SUB, D), F32)] + dconv
    else:
        out_specs = [rowD, rowC, rowC, rowD, _full((SUB, FF_CW)), _full((SUB, FF_CW))]
        out_shape = [jax.ShapeDtypeStruct((L, D), MXU)] + halfb + [jax.ShapeDtypeStruct((L, D), F32)] + dconv
    return pl.pallas_call(
        body, name="ffn_bwd_last" if last else "ffn_bwd_first", grid=(ni,),
        in_specs=in_specs, out_specs=out_specs, out_shape=out_shape,
        scratch_shapes=[pltpu.VMEM((SUB, FF_CW), F32), pltpu.VMEM((SUB, FF_CW), F32)],
        compiler_params=_params(1),
    )(*args)


def _final(x, target, nf):
    L = x.shape[0]
    T = _tile(L, 256)

    def body(x_ref, t_ref, nf_ref, dx_ref, acc_ref):
        @pl.when(pl.program_id(0) == 0)
        def _():
            acc_ref[...] = jnp.zeros_like(acc_ref)

        n, r = _rms(x_ref[...])
        w = nf_ref[0:1, :]
        err = n * w - t_ref[...]
        acc_ref[0:1, :] += (0.5 / D) * _sum0(err * err)
        dy = err * (1.0 / D)
        acc_ref[1:2, :] += _sum0(dy * n)
        dx_ref[...] = _rms_bwd(dy * w, n, r)

    row = lambda i: (i, 0)
    return pl.pallas_call(
        body, name="final_norm_loss", grid=(L // T,),
        in_specs=[pl.BlockSpec((T, D), row), pl.BlockSpec((T, D), row), _full((SUB, D))],
        out_specs=[pl.BlockSpec((T, D), row), _full((SUB, D))],
        out_shape=[jax.ShapeDtypeStruct((L, D), F32), jax.ShapeDtypeStruct((SUB, D), F32)],
        compiler_params=_params(1),
    )(x, target, nf)


def _post_bwd(dx2, y, o, p, modrows, sp, w_out):
    L = dx2.shape[0]
    T = _tile(L, 512)

    def body(dx2_ref, y_ref, o_ref, z_ref, mod_ref, sp_ref, w_ref, dy_ref, do_ref, dz_ref, dya_ref, accv_ref, accs_ref):
        @pl.when(pl.program_id(0) == 0)
        def _():
            accv_ref[...] = jnp.zeros_like(accv_ref)
            accs_ref[...] = jnp.zeros_like(accs_ref)

        dx2v = dx2_ref[...]
        accv_ref[0:1, :] += _sum0(dx2v * y_ref[...])
        dyb = (mod_ref[2:3, :] * dx2v).astype(MXU)
        dy_ref[...] = dyb
        dyc = _dot(dyb, w_ref[...], NT)
        dya_ref[...] = dyc[:, 0:AW]
        ndw = sp_ref[2:3, :]
        z = z_ref[...]
        sgz = _sig(z)
        dsz = sgz * (1.0 + z * (1.0 - sgz))
        dndw = jnp.zeros((1, HD), F32)
        for h in range(H):
            sl = slice(h * HD, (h + 1) * HD)
            n, r = _rms(o_ref[:, sl])
            dyh = dyc[:, AW + h * HD:AW + (h + 1) * HD]
            zh = z[:, sl]
            don = dyh * (zh * sgz[:, sl])
            dz_ref[:, sl] = dyh * (n * ndw) * dsz[:, sl]
            dndw = dndw + _sum0(don * n)
            do_ref[:, sl] = _rms_bwd(don * ndw, n, r)
        accs_ref[0:1, :] += dndw

    row = lambda i: (i, 0)
    zcol = (3 * AW + 3 * H * HD) // (H * HD)
    return pl.pallas_call(
        body, name="post_bwd", grid=(L // T,),
        in_specs=[pl.BlockSpec((T, D), row), pl.BlockSpec((T, D), row), pl.BlockSpec((T, H * HD), row),
                  pl.BlockSpec((T, H * HD), lambda i: (i, zcol)), _full((SUB, D)), _full((SUB, LANES)), _full((D, D))],
        out_specs=[pl.BlockSpec((T, D), row)] + [pl.BlockSpec((T, H * HD), row)] * 3 + [_full((SUB, D)), _full((SUB, LANES))],
        out_shape=[jax.ShapeDtypeStruct((L, D), MXU)] + [jax.ShapeDtypeStruct((L, H * HD), F32)] * 3
        + [jax.ShapeDtypeStruct((SUB, D), F32), jax.ShapeDtypeStruct((SUB, LANES), F32)],
        compiler_params=_params(1),
    )(dx2, y, o, p, modrows, sp, w_out)


def _pre_in_bwd(p, cub, qcb, dqn, dkn, dvs, dya, dz, dgb, pa, cq, sp, w_in, x, dx2, modrows, vec):
    L = p.shape[0]
    T = _tile(L, 256)
    ni = L // T
    scale = HD ** -0.5
    w3 = 3 * AW + 3 * H * HD

    def body(pm_ref, cu_ref, qc_ref, ps_ref, dq_ref, dk_ref, dv_ref, dya_ref, dz_ref, dgb_ref, pa_ref, cq_ref, sp_ref,
             w_ref, x_ref, dx2_ref, mod_ref, vec_ref,
             dp_ref, dx_ref, dpa_ref, dcq_ref, dsp_ref, accv_ref, carry_u, carry_q):
        i = pl.program_id(0)

        @pl.when(i == 0)
        def _():
            dpa_ref[...] = jnp.zeros_like(dpa_ref)
            dcq_ref[...] = jnp.zeros_like(dcq_ref)
            dsp_ref[...] = jnp.zeros_like(dsp_ref)
            accv_ref[...] = jnp.zeros_like(accv_ref)
            carry_u[...] = jnp.zeros_like(carry_u)
            carry_q[...] = jnp.zeros_like(carry_q)

        a_b, a_c, a_x = pm_ref[:, 0:AW], pm_ref[:, AW:2 * AW], pm_ref[:, 2 * AW:3 * AW]
        u = a_c * a_x
        cu = cu_ref[...].astype(F32)
        yp = a_b * cu
        bd = _blockdiag_mean(AW, A_GROUP)
        ra = lax.rsqrt(_dot_f32(yp * yp, bd, NN, exact="b") + EPS)
        na = yp * ra
        dya = dya_ref[...]
        dpa_ref[3:4, :] += _sum0(dya * na)
        dna = dya * pa_ref[3:4, :]
        dyp = ra * (dna - na * _dot_f32(dna * na, bd, NN, exact="b"))
        dcu = dyp * a_b
        dcs = [dcu] + [_shift_up(dcu, s, carry_u[...]) for s in (1, 2)]
        du = pa_ref[2:3, :] * dcs[0]
        for s in range(3):
            dpa_ref[2 - s:3 - s, :] += _sum0(dcs[s] * u)
            if s:
                du = du + pa_ref[2 - s:3 - s, :] * dcs[s]
        carry_u[...] = dcu[0:SUB, :]
        dp_a = jnp.concatenate([dyp * cu, du * a_x, du * a_c], axis=-1).astype(MXU)
        dp_ref[:, 0:3 * AW] = dp_a
        dh = _dot(dp_a, w_ref[:, 0:3 * AW], NT)

        qkv = pm_ref[:, 3 * AW:w3]
        qc = qc_ref[...].astype(F32)
        sg = _sig(qc)
        qs = qc * sg
        parts = []
        for h in range(H):
            q = qs[:, h * HD:(h + 1) * HD]
            rq = lax.rsqrt(jnp.sum(q * q, axis=-1, keepdims=True) + EPS)
            parts.append(_l2_bwd(dq_ref[:, h * HD:(h + 1) * HD] * scale, q * rq, rq))
        for h in range(H):
            k = qs[:, (H + h) * HD:(H + h + 1) * HD]
            rk = lax.rsqrt(jnp.sum(k * k, axis=-1, keepdims=True) + EPS)
            parts.append(_l2_bwd(dk_ref[:, h * HD:(h + 1) * HD], k * rk, rk))
        parts.append(dv_ref[...])
        dqc = jnp.concatenate(parts, axis=-1) * (sg * (1.0 + qc * (1.0 - sg)))
        dqs = [dqc] + [_shift_up(dqc, s, carry_q[...]) for s in (1, 2, 3)]
        dqkv = cq_ref[3:4, :] * dqs[0]
        for s in range(4):
            dcq_ref[3 - s:4 - s, :] += _sum0(dqs[s] * qkv)
            if s:
                dqkv = dqkv + cq_ref[3 - s:4 - s, :] * dqs[s]
        dp_q = dqkv.astype(MXU)
        dp_ref[:, 3 * AW:w3] = dp_q
        dh = dh + _dot(dp_q, w_ref[:, 3 * AW:w3], NT)
        carry_q[...] = dqc[0:SUB, :]

        lane, a, xb, beta, g = _gate_small(ps_ref[...], sp_ref)
        dgb = dgb_ref[...]
        dbeta = jnp.where(lane < H, dgb, 0.0)
        dg = jnp.where((lane >= H) & (lane < 2 * H), dgb, 0.0)
        dalpha = dg * a * _sig(xb)
        dsp_ref[0:1, :] += _sum0(dg * g)
        dsp_ref[1:2, :] += _sum0(dalpha)
        dp_z = jnp.concatenate([dz_ref[...], dbeta * beta * (1.0 - beta) + dalpha], axis=-1).astype(MXU)
        dp_ref[:, w3:P_PAD] = dp_z
        dh = dh + _dot(dp_z, w_ref[:, w3:P_PAD], NT)

        n, r = _rms(x_ref[...])
        nw, sc = vec_ref[0:1, :], mod_ref[1:2, :]
        accv_ref[0:1, :] += _sum0(dh)
        accv_ref[1:2, :] += _sum0(dh * n * nw)
        accv_ref[2:3, :] += _sum0(dh * n * (1.0 + sc))
        dx_ref[...] = _rms_bwd(dh * nw * (1.0 + sc), n, r) + dx2_ref[...]

    row = lambda i: (ni - 1 - i, 0)
    hrow = pl.BlockSpec((T, H * HD), row)
    rowD = pl.BlockSpec((T, D), row)
    return pl.pallas_call(
        body, name="pre_in_bwd", grid=(ni,),
        in_specs=[pl.BlockSpec((T, w3), row), pl.BlockSpec((T, AW), row), pl.BlockSpec((T, 3 * H * HD), row),
                  pl.BlockSpec((T, LANES), lambda i: (ni - 1 - i, (P_PAD - LANES) // LANES)),
                  hrow, hrow, hrow, pl.BlockSpec((T, AW), row), hrow,
                  pl.BlockSpec((T, LANES), row),
                  _full((SUB, AW)), _full((SUB, 3 * H * HD)), _full((SUB, LANES)),
                  _full((D, P_PAD)), rowD, rowD, _full((SUB, D)), _full((SUB, D))],
        out_specs=[pl.BlockSpec((T, P_PAD), row), rowD, _full((SUB, AW)), _full((SUB, 3 * H * HD)), _full((SUB, LANES)),
                   _full((SUB, D))],
        out_shape=[jax.ShapeDtypeStruct((L, P_PAD), MXU), jax.ShapeDtypeStruct((L, D), F32),
                   jax.ShapeDtypeStruct((SUB, AW), F32), jax.ShapeDtypeStruct((SUB, 3 * H * HD), F32),
                   jax.ShapeDtypeStruct((SUB, LANES), F32), jax.ShapeDtypeStruct((SUB, D), F32)],
        scratch_shapes=[pltpu.VMEM((SUB, AW), F32), pltpu.VMEM((SUB, 3 * H * HD), F32)],
        compiler_params=_params(1),
    )(p, cub, qcb, p, dqn, dkn, dvs, dya, dz, dgb, pa, cq, sp, w_in, x, dx2, modrows, vec)


def _wgrad_rows(pieces, b, name):
    L, m = pieces[0].shape
    n = b.shape[1]
    na = len(pieces)
    tl = _tile(L, 1024)
    nl = L // tl

    def body(*refs):
        a_refs, b_ref, o_ref, accs = refs[:na], refs[na], refs[na + 1], refs[na + 2:]

        @pl.when(pl.program_id(0) == 0)
        def _():
            for acc in accs:
                acc[...] = jnp.zeros_like(acc)

        bv = b_ref[...]
        for a_ref, acc in zip(a_refs, accs):
            acc[...] += _dot(a_ref[...], bv, TN)

        @pl.when(pl.program_id(0) == nl - 1)
        def _():
            for p, acc in enumerate(accs):
                o_ref[p] = acc[...].astype(o_ref.dtype)

    return pl.pallas_call(
        body, name=name, grid=(nl,),
        in_specs=[pl.BlockSpec((tl, m), lambda l: (l, 0))] * na + [pl.BlockSpec((tl, n), lambda l: (l, 0))],
        out_specs=pl.BlockSpec((na, m, n), lambda l: (0, 0, 0)),
        out_shape=jax.ShapeDtypeStruct((na, m, n), MXU), scratch_shapes=[pltpu.VMEM((m, n), F32)] * na,
        compiler_params=_params(1),
    )(*pieces, b)


def _wgrad_cols(a, bs, tm, tl, n_shard, wpad, count, name):
    L, m = a.shape
    n = bs[0].shape[1]
    nb = len(bs)
    tl = _tile(L, tl)
    tm = _tile(m, tm)
    nl = L // tl
    wins = _shard_windows(n_shard, count)
    assert all(a_ * LANES + win <= n for a_, _, win in wins), (wins, n)

    def body(*refs):
        a_ref, b_refs, o_ref, accs = refs[0], refs[1:1 + nb], refs[1 + nb], refs[2 + nb:]

        @pl.when(pl.program_id(1) == 0)
        def _():
            for acc in accs:
                acc[...] = jnp.zeros_like(acc)

        av = a_ref[...]
        for b_ref, acc in zip(b_refs, accs):
            acc[...] += _dot(av, b_ref[...], TN)

        @pl.when(pl.program_id(1) == nl - 1)
        def _():
            for p, acc in enumerate(accs):
                for k, (a_, s, win) in enumerate(wins):
                    xk = acc[:, a_ * LANES:a_ * LANES + win]
                    if s:
                        xk = pltpu.roll(xk, win - s, 1)
                    o_ref[p * count + k] = _fit_lanes(xk, wpad).astype(o_ref.dtype)

    return pl.pallas_call(
        body, name=name, grid=(m // tm, nl),
        in_specs=[pl.BlockSpec((tl, tm), lambda i, l: (l, i))] + [pl.BlockSpec((tl, n), lambda i, l: (l, 0))] * nb,
        out_specs=pl.BlockSpec((nb * count, tm, wpad), lambda i, l: (0, i, 0)),
        out_shape=jax.ShapeDtypeStruct((nb * count, m, wpad), MXU),
        scratch_shapes=[pltpu.VMEM((tm, n), F32)] * nb,
        compiler_params=_params(2),
    )(a, *bs)


def _adamw(w, g, m, v, name):
    r, n = w.shape
    tr = _tile(r, 512)
    bc1 = 1.0 - ADAM_B1 ** ADAM_STEP
    bc2 = 1.0 - ADAM_B2 ** ADAM_STEP

    def body(w_ref, g_ref, m_ref, v_ref, d_ref, nm_ref, nv_ref):
        gv = g_ref[...]
        nm = ADAM_B1 * m_ref[...] + (1.0 - ADAM_B1) * gv
        nv = ADAM_B2 * v_ref[...] + (1.0 - ADAM_B2) * (gv * gv)
        nm_ref[...] = nm
        nv_ref[...] = nv
        d_ref[...] = -ADAM_LR * ((nm / bc1) / (jnp.sqrt(nv / bc2) + ADAM_EPS) + ADAM_WD * w_ref[...])

    spec = pl.BlockSpec((tr, n), lambda i: (i, 0))
    return pl.pallas_call(
        body, name=name, grid=(r // tr,), in_specs=[spec] * 4, out_specs=[spec] * 3,
        out_shape=[jax.ShapeDtypeStruct((r, n), F32)] * 3, compiler_params=_params(1),
    )(w, g, m, v)


def _rows8(rows, width):
    out = jnp.zeros((SUB, width), F32)
    for r, vrow in enumerate(rows):
        out = out.at[r, :vrow.shape[0]].set(vrow)
    return out


def _at_lanes(v4, start):
    return jnp.zeros((LANES,), F32).at[start:start + v4.shape[0]].set(v4)


def _pad_rows(flat, mult):
    n = flat.shape[0]
    pad = (-n) % mult
    return jnp.pad(flat, (0, pad)) if pad else flat


IN_PAD = 512
UP_PAD = 768


def _local_fwd_bwd(x, target, mod_full, small_w, full_w, on_grads=None):
    norm1_w, norm2_w, norm_a_w, a_log, dt_bias, norm_dn_w, norm_f_w = small_w
    w_in_f, w_out_f, w_up_f, w_down_f, conv_a_f, conv_q_f, conv_f_f = full_w

    def layer_params(i):
        modrows = jnp.concatenate([mod_full[i], jnp.zeros((SUB - N_MOD, D), F32)], axis=0)
        vec = _rows8([norm1_w[i], norm2_w[i]], D)
        pa = _rows8([conv_a_f[i, 0], conv_a_f[i, 1], conv_a_f[i, 2], norm_a_w[i]], AW)
        cq = _rows8([conv_q_f[i, k] for k in range(4)], 3 * H * HD)
        sp = _rows8([_at_lanes(a_log[i], H), _at_lanes(dt_bias[i], H), norm_dn_w[i]], LANES)
        cff = _rows8([conv_f_f[i, k] for k in range(3)], 2 * DFF)
        return modrows, vec, pa, cq, sp, cff

    saved = []
    xi = x
    for i in range(DEPTH):
        modrows, vec, pa, cq, sp, cff = layer_params(i)
        p, h1, qn, kn, vs, gb, ya, cub, qcb = _in_pre_fwd(xi, modrows, vec, w_in_f[i], pa, cq, sp)
        o, states, tinvs = _gdr_fwd(qn, kn, vs, gb)
        y, x2, yb = _post_fwd(o, p, ya, xi, modrows, sp, w_out_f[i])
        h2, gp0, up0, gc0, uc0, f0, d0 = _ffn_fwd_half(x2, modrows, vec, w_up_f[i], cff, w_down_f[i], 0, None)
        gp1, up1, gc1, uc1, f1, dff, x3 = _ffn_fwd_half(x2, modrows, vec, w_up_f[i], cff, w_down_f[i], 1, d0)
        saved.append(dict(x=xi, p=p, h1=h1, qn=qn, kn=kn, vs=vs, gb=gb, ya=ya, cub=cub, qcb=qcb, o=o, states=states,
                          tinvs=tinvs, y=y, x2=x2, yb=yb,
                          h2=h2, gpre=(gp0, gp1), upre=(up0, up1), gc=(gc0, gc1), uc=(uc0, uc1), f=(f0, f1), d=dff))
        xi = x3

    dx, facc = _final(xi, target, _rows8([norm_f_w], D))
    loss_local = jnp.sum(facc[0])
    d_norm_f = facc[1]

    gw_in, gw_out, gw_up, gw_down = [None] * DEPTH, [None] * DEPTH, [None] * DEPTH, [None] * DEPTH
    g_small = [None] * DEPTH
    for i in reversed(range(DEPTH)):
        s = saved[i]
        modrows, vec, pa, cq, sp, cff = layer_params(i)
        dd, dgp0, dup0, dh0, dcg0, dcu0 = _ffn_bwd_half(dx, modrows, s["gpre"][0], s["upre"][0], s["gc"][0], s["uc"][0],
                                                        cff, w_down_f[i], w_up_f[i], 0, None)
        dgp1, dup1, dx2, accf, dcg1, dcu1 = _ffn_bwd_half(dx, modrows, s["gpre"][1], s["upre"][1], s["gc"][1], s["uc"][1],
                                                          cff, w_down_f[i], w_up_f[i], 1, (s["d"], s["x2"], vec, dh0))
        n_up, up_pad = 2 * DFF // N_DEV, UP_PAD
        gw_up[i] = _wgrad_cols(s["h2"], [dgp0, dgp1, dup0, dup1], 512, 512, n_up, up_pad, FF_CW // n_up, "wgrad_up")
        gw_down[i] = _wgrad_rows([s["f"][0], s["f"][1]], dd, "wgrad_down").reshape(N_DEV, DFF // N_DEV, D)
        if on_grads is not None:
            on_grads(i, "ffn", [gw_up[i], gw_down[i]])
        dy, do, dz, dya, accp, accs = _post_bwd(dx2, s["y"], s["o"], s["p"], modrows, sp, w_out_f[i])
        gw_out[i] = _wgrad_rows([s["ya"], s["yb"]], dy, "wgrad_out").reshape(N_DEV, D // N_DEV, D)
        dqn, dkn, dvs, dgb = _gdr_bwd(s["qn"], s["kn"], s["vs"], s["gb"], s["states"], s["tinvs"], do)
        dp, dx, dpa, dcq, dsp, acci = _pre_in_bwd(s["p"], s["cub"], s["qcb"], dqn, dkn, dvs, dya, dz, dgb, pa, cq, sp,
                                                  w_in_f[i], s["x"], dx2, modrows, vec)
        gw_in[i] = _wgrad_cols(s["h1"], [dp], 1024, 1024, P_IN // N_DEV, IN_PAD, N_DEV, "wgrad_in")
        dconv_ff = jnp.concatenate([dcg0, dcg1, dcu0, dcu1], axis=1)[0:3]
        dmod = jnp.stack([acci[0], acci[1], accp[0], accf[1], accf[2], accf[0]])
        g_small[i] = dict(norm1=acci[2], norm2=accf[3], norm_a=dpa[3], a_log=dsp[0, H:2 * H], dt_bias=dsp[1, H:2 * H],
                          norm_dn=accs[0], conv_a=dpa[0:3], conv_qkv=dcq[0:4], conv_ff=dconv_ff, dmod=dmod.reshape(-1))
        if on_grads is not None:
            on_grads(i, "mix", [gw_in[i], gw_out[i]])
    return loss_local, dx, gw_in, gw_out, gw_up, gw_down, g_small, d_norm_f


def kernel(x, c, ada_w, ada_b, norm1_w, w_in, conv_a_w, norm_a_w, conv_qkv_w, a_log, dt_bias, norm_dn_w, w_out, norm2_w, w_up, conv_ff_w, w_down, norm_f_w, loss_target, m_ada_w, m_ada_b, m_norm1_w, m_w_in, m_conv_a_w, m_norm_a_w, m_conv_qkv_w, m_a_log, m_dt_bias, m_norm_dn_w, m_w_out, m_norm2_w, m_w_up, m_conv_ff_w, m_w_down, m_norm_f_w, v_ada_w, v_ada_b, v_norm1_w, v_w_in, v_conv_a_w, v_norm_a_w, v_conv_qkv_w, v_a_log, v_dt_bias, v_norm_dn_w, v_w_out, v_norm2_w, v_w_up, v_conv_ff_w, v_w_down, v_norm_f_w):
    ax, ay, ac = lax.axis_index("x"), lax.axis_index("y"), lax.axis_index("c")
    me = 4 * ax + 2 * ay + ac
    x = x[0]
    target = loss_target[0]
    n_in, n_up = P_IN // N_DEV, 2 * DFF // N_DEV

    def lane_pad(t, width):
        return jnp.pad(t.astype(MXU), ((0, 0), (0, 0), (0, width - t.shape[-1])))

    conv_blob = _pad_rows(jnp.concatenate([t.reshape(-1) for t in (conv_a_w, conv_qkv_w, conv_ff_w)]),
                          SUB * LANES).reshape(-1, LANES)
    c_rows = jnp.zeros((SUB, D), F32).at[0].set(c[0])
    send = [lane_pad(w_in, IN_PAD), w_out.astype(MXU), lane_pad(w_up, UP_PAD), w_down.astype(MXU)]
    got = [None] * DEPTH
    g_in0, g_conv, g_c = _all_gather([send[0][0], conv_blob, c_rows], "gather_weights", in_vmem=False)
    shards, _ = lax.optimization_barrier(([t[0] for t in send[1:]], g_c))
    got[0] = [g_in0] + _all_gather_async(shards, "gather_weights_l0", collective_id=0)
    for i in range(1, DEPTH):
        shards, _ = lax.optimization_barrier(([t[i] for t in send], g_c))
        got[i] = _all_gather_async(shards, "gather_weights_l%d" % i, collective_id=i)
    w_in_f = [_interleave_cols(g[0][:, None], n_in, P_PAD, "interleave_w_in")[0] for g in got]
    w_up_f = [_interleave_cols(g[2][:, None], n_up, 2 * DFF, "interleave_w_up")[0] for g in got]
    w_out_f = [g[1].reshape(D, D) for g in got]
    w_down_f = [g[3].reshape(DFF, D) for g in got]
    sg = g_conv.reshape(N_DEV, -1)
    o1 = conv_a_w.size
    o2 = o1 + conv_qkv_w.size
    o3 = o2 + conv_ff_w.size
    conv_a_f = sg[:, 0:o1].reshape(N_DEV, DEPTH, 3, AW // N_DEV).transpose(1, 2, 0, 3).reshape(DEPTH, 3, AW)
    conv_q_f = sg[:, o1:o2].reshape(N_DEV, DEPTH, 4, 3 * H * HD // N_DEV).transpose(1, 2, 0, 3).reshape(DEPTH, 4, 3 * H * HD)
    conv_f_f = sg[:, o2:o3].reshape(N_DEV, DEPTH, 3, n_up).transpose(1, 2, 0, 3).reshape(DEPTH, 3, 2 * DFF)

    c_all = jnp.concatenate([g_c[:, 0], jnp.zeros((16 - N_DEV, D), F32)], axis=0)
    n_ada = N_MOD * D // N_DEV
    ada_b_cols = lax.dynamic_slice_in_dim(ada_b, me * n_ada, n_ada, axis=1)[:, None, :]
    mod_sh = _mod_fwd(c_all, ada_w, ada_b_cols)
    mod_all = _all_gather([mod_sh.reshape(DEPTH * 16, n_ada)], "gather_mod", in_vmem=True)[0]
    mod_all = mod_all.reshape(N_DEV, DEPTH, 16, n_ada)
    mod_mine = lax.dynamic_index_in_dim(mod_all, me, axis=2, keepdims=False)
    mod_full = mod_mine.transpose(1, 0, 2).reshape(DEPTH, N_MOD, D)

    tags = ["w_in", "w_out", "w_up", "w_down"]
    received = [dict() for _ in range(DEPTH)]

    def on_grads(i, part, gs_i):
        first_id = DEPTH if part == "ffn" else 2 * DEPTH
        got_i = _rs_exchange_async(gs_i, "rs_exchange_%s_l%d" % (part, i), collective_id=first_id + i)
        received[i].update(zip(("w_up", "w_down") if part == "ffn" else ("w_in", "w_out"), got_i))

    loss_local, dx, _, _, _, _, g_small, d_norm_f = _local_fwd_bwd(
        x, target, mod_full, (norm1_w, norm2_w, norm_a_w, a_log, dt_bias, norm_dn_w, norm_f_w),
        (w_in_f, w_out_f, w_up_f, w_down_f, conv_a_f, conv_q_f, conv_f_f), on_grads)
    loss = lax.psum(loss_local, ("x", "y", "c"))
    grad_x = dx[None]

    keys = ["dmod", "norm1", "norm2", "norm_a", "a_log", "dt_bias", "norm_dn", "conv_a", "conv_qkv", "conv_ff"]
    stacked = {k: jnp.stack([g_small[i][k] for i in range(DEPTH)]) for k in keys}
    flat_parts = [stacked[k].reshape(-1) for k in keys] + [d_norm_f]
    sizes = [int(t.shape[0]) for t in flat_parts]
    sflat = _pad_rows(jnp.concatenate(flat_parts), SUB * LANES).reshape(-1, LANES)
    sall = _all_gather([sflat], "gather_small_grads", in_vmem=True)[0]
    ssum = _sum_devices(sall).reshape(-1)
    so = [0]
    for sz in sizes:
        so.append(so[-1] + sz)
    red = {k: ssum[so[n]:so[n + 1]].reshape(stacked[k].shape) for n, k in enumerate(keys)}
    g_norm_f = ssum[so[len(keys)]:so[len(keys) + 1]]
    dmod_all = sall[:, 0:sizes[0] // LANES, :].reshape(N_DEV, DEPTH, N_MOD * D)

    g_ada_b = red["dmod"].reshape(DEPTH, N_MOD * D)
    dmod_cols = lax.dynamic_slice_in_dim(dmod_all, me * n_ada, n_ada, axis=2).transpose(1, 0, 2)
    dmod_cols = jnp.concatenate([dmod_cols, jnp.zeros((DEPTH, 16 - N_DEV, n_ada), F32)], axis=1)
    g_ada_w = _mod_bwd(c_all, dmod_cols)
    g_conv_a = lax.dynamic_slice_in_dim(red["conv_a"], me * (AW // N_DEV), AW // N_DEV, axis=2)
    g_conv_qkv = lax.dynamic_slice_in_dim(red["conv_qkv"], me * (3 * H * HD // N_DEV), 3 * H * HD // N_DEV, axis=2)
    g_conv_ff = lax.dynamic_slice_in_dim(red["conv_ff"], me * n_up, n_up, axis=2)

    mine = [jnp.stack([_rs_sum(received[i][t], "rs_sum_" + t) for i in range(DEPTH)]) for t in tags]
    g_w_in = mine[0][:, :, :n_in]
    g_w_out = mine[1]
    g_w_up = mine[2][:, :, :n_up]
    g_w_down = mine[3]

    grads = dict(ada_w=g_ada_w, ada_b=g_ada_b, norm1_w=red["norm1"], w_in=g_w_in, conv_a_w=g_conv_a,
                 norm_a_w=red["norm_a"], conv_qkv_w=g_conv_qkv, a_log=red["a_log"], dt_bias=red["dt_bias"],
                 norm_dn_w=red["norm_dn"], w_out=g_w_out, norm2_w=red["norm2"], w_up=g_w_up, conv_ff_w=g_conv_ff,
                 w_down=g_w_down, norm_f_w=g_norm_f)
    weights = dict(ada_w=ada_w, ada_b=ada_b, norm1_w=norm1_w, w_in=w_in, conv_a_w=conv_a_w, norm_a_w=norm_a_w,
                   conv_qkv_w=conv_qkv_w, a_log=a_log, dt_bias=dt_bias, norm_dn_w=norm_dn_w, w_out=w_out,
                   norm2_w=norm2_w, w_up=w_up, conv_ff_w=conv_ff_w, w_down=w_down, norm_f_w=norm_f_w)
    ms = dict(ada_w=m_ada_w, ada_b=m_ada_b, norm1_w=m_norm1_w, w_in=m_w_in, conv_a_w=m_conv_a_w, norm_a_w=m_norm_a_w,
              conv_qkv_w=m_conv_qkv_w, a_log=m_a_log, dt_bias=m_dt_bias, norm_dn_w=m_norm_dn_w, w_out=m_w_out,
              norm2_w=m_norm2_w, w_up=m_w_up, conv_ff_w=m_conv_ff_w, w_down=m_w_down, norm_f_w=m_norm_f_w)
    vs_ = dict(ada_w=v_ada_w, ada_b=v_ada_b, norm1_w=v_norm1_w, w_in=v_w_in, conv_a_w=v_conv_a_w, norm_a_w=v_norm_a_w,
               conv_qkv_w=v_conv_qkv_w, a_log=v_a_log, dt_bias=v_dt_bias, norm_dn_w=v_norm_dn_w, w_out=v_w_out,
               norm2_w=v_norm2_w, w_up=v_w_up, conv_ff_w=v_conv_ff_w, w_down=v_w_down, norm_f_w=v_norm_f_w)
    names = list(weights)
    big_names = ["ada_w", "w_in", "w_out", "w_up", "w_down"]
    delta, new_m, new_v = {}, {}, {}
    for n in big_names:
        shp = weights[n].shape
        two = lambda t: t.reshape(-1, shp[-1])
        dl, nm, nv = _adamw(two(weights[n]), two(grads[n]), two(ms[n]), two(vs_[n]), "adamw_" + n)
        delta[n], new_m[n], new_v[n] = dl.reshape(shp), nm.reshape(shp), nv.reshape(shp)
    small_names = [n for n in names if n not in big_names]

    def pack(dct):
        return _pad_rows(jnp.concatenate([dct[n].reshape(-1) for n in small_names]), SUB * LANES).reshape(-1, LANES)

    dl, nm, nv = _adamw(pack(weights), pack(grads), pack(ms), pack(vs_), "adamw_small")
    off = 0
    for n in small_names:
        sz, shp = weights[n].size, weights[n].shape
        delta[n] = dl.reshape(-1)[off:off + sz].reshape(shp)
        new_m[n] = nm.reshape(-1)[off:off + sz].reshape(shp)
        new_v[n] = nv.reshape(-1)[off:off + sz].reshape(shp)
        off += sz

    return (loss, grad_x, *[grads[n] for n in names], *[delta[n] for n in names],
            *[new_m[n] for n in names], *[new_v[n] for n in names])
```
